```python
import jax, jax.numpy as jnp
from jax import lax
import numpy as np

D_MODEL = 1024
BATCH = 8
SEQ = 8192
DEPTH = 2

CHUNK = 64
D_MIX = D_MODEL
D_RET = D_MIX // 2
D_SB = D_MIX - D_RET
RET_HEADS = 4
RET_HEAD_DIM = D_RET // RET_HEADS
SB_HEADS = 8
SB_HEAD_DIM = D_SB // SB_HEADS
Q_BLOCK = 128
ROPE_BASE = 10000.0
EPS = 1e-6
ADA_SCALE = 0.2
SPLIT_SIZES = [D_RET] * 4 + [D_SB] * 4
D_IN = sum(SPLIT_SIZES)

kernel_name = "hybrid_retention_stickbreaking_block"


def rms_norm(x, g):
    xf = x.astype(jnp.float32)
    return xf * lax.rsqrt(jnp.mean(xf * xf, axis=-1, keepdims=True) + EPS) * g.astype(jnp.float32)


def split_heads(x, n_heads):
    b, s, _ = x.shape
    return x.reshape(b, s, n_heads, -1).transpose(0, 2, 1, 3)


def merge_heads(x):
    b, h, s, d = x.shape
    return x.transpose(0, 2, 1, 3).reshape(b, s, h * d)


def rotary(x, pos):
    half = x.shape[-1] // 2
    inv = ROPE_BASE ** (-jnp.arange(half, dtype=jnp.float32) / half)
    ang = pos[:, None] * inv[None, :]
    cos, sin = jnp.cos(ang), jnp.sin(ang)
    x1, x2 = x[..., :half], x[..., half:]
    return jnp.concatenate([x1 * cos - x2 * sin, x1 * sin + x2 * cos], axis=-1)


def retention(q, k, v):
    b, h, s, d = q.shape
    n = s // CHUNK
    log_gamma = jnp.log1p(-(2.0 ** (-5.0 - jnp.arange(h, dtype=jnp.float32))))
    idx = jnp.arange(CHUNK, dtype=jnp.float32)
    dmask = jnp.exp(jnp.abs(idx[:, None] - idx[None, :])[None] * log_gamma[:, None, None])
    q_dec = jnp.exp((idx + 1.0)[None, :] * log_gamma[:, None])
    k_dec = jnp.exp((CHUNK - 1.0 - idx)[None, :] * log_gamma[:, None])
    chunk_dec = jnp.exp(CHUNK * log_gamma)

    k = k * (d ** -0.5)
    qc = q.reshape(b, h, n, CHUNK, d)
    kc = k.reshape(b, h, n, CHUNK, d)
    vc = v.reshape(b, h, n, CHUNK, v.shape[-1])

    scores = jnp.einsum('bhncd,bhnmd->bhncm', qc, kc) * dmask[None, :, None]
    o_inner = jnp.einsum('bhncm,bhnme->bhnce', scores, vc)

    kv = jnp.einsum('bhnmd,bhnme->nbhde', kc * k_dec[None, :, None, :, None], vc)

    def step(state, kv_i):
        return state * chunk_dec[None, :, None, None] + kv_i, state

    init = jnp.zeros((b, h, d, v.shape[-1]), dtype=kv.dtype)
    _, prev = lax.scan(step, init, kv)
    o_cross = jnp.einsum('bhncd,nbhde->bhnce', qc, prev) * q_dec[None, :, None, :, None]

    o = (o_inner + o_cross).reshape(b, h, s, v.shape[-1])
    mu = jnp.mean(o, axis=-1, keepdims=True)
    var = jnp.mean(jnp.square(o - mu), axis=-1, keepdims=True)
    return (o - mu) * lax.rsqrt(var + EPS)


def stick_breaking(q, k, v):
    b, h, s, d = q.shape
    nb = s // Q_BLOCK
    scale = d ** -0.5
    q_blocks = q.reshape(b, h, nb, Q_BLOCK, d).transpose(2, 0, 1, 3, 4)
    key_pos = jnp.arange(s)

    def block(args):
        qi, bi = args
        t = bi * Q_BLOCK + jnp.arange(Q_BLOCK)
        mask = key_pos[None, :] < t[:, None]
        z = jnp.einsum('bhqd,bhsd->bhqs', qi, k) * scale
        log_beta = jax.nn.log_sigmoid(z)
        log_keep = jnp.where(mask, jax.nn.log_sigmoid(-z), 0.0)
        suffix = lax.cumsum(log_keep, axis=3, reverse=True) - log_keep
        a = jnp.where(mask, jnp.exp(log_beta + suffix), 0.0)
        return jnp.einsum('bhqs,bhse->bhqe', a, v)

    out = lax.map(block, (q_blocks, jnp.arange(nb)))
    return out.transpose(1, 2, 0, 3, 4).reshape(b, h, s, v.shape[-1])


def hybrid_layer(x, c_act, norm_g, w_ada, b_ada, w_in, w_out, pos):
    mod = c_act @ w_ada + b_ada
    shift, scale, gate = jnp.split(mod, 3, axis=-1)
    h = rms_norm(x, norm_g) * (1.0 + scale[:, None, :]) + shift[:, None, :]
    proj = h @ w_in
    split_at = [int(i) for i in np.cumsum(SPLIT_SIZES)[:-1]]
    rq, rk, rv, rg, sq, sk, sv, sg = jnp.split(proj, split_at, axis=-1)

    y_ret = retention(rotary(split_heads(rq, RET_HEADS), pos),
                      rotary(split_heads(rk, RET_HEADS), pos),
                      split_heads(rv, RET_HEADS))
    y_ret = merge_heads(y_ret) * jax.nn.silu(rg)

    y_sb = stick_breaking(split_heads(sq, SB_HEADS), split_heads(sk, SB_HEADS),
                          split_heads(sv, SB_HEADS))
    y_sb = merge_heads(y_sb) * jax.nn.silu(sg)

    y = jnp.concatenate([y_ret, y_sb], axis=-1) @ w_out
    return x + gate[:, None, :] * y


def _fwd_setup_inputs(seed: int = 0) -> dict:
    key = jax.random.key(seed)
    ks = jax.random.split(key, 8)
    f32 = jnp.float32
    x = jax.random.normal(ks[0], (BATCH, SEQ, D_MODEL), f32)
    c = jax.random.normal(ks[1], (BATCH, D_MODEL), f32)
    norm_g = 1.0 + 0.02 * jax.random.normal(ks[2], (DEPTH, D_MODEL), f32)
    w_ada = jax.random.normal(ks[3], (DEPTH, D_MODEL, 3 * D_MODEL), f32) * (ADA_SCALE * D_MODEL ** -0.5)
    b_ada = 0.02 * jax.random.normal(ks[4], (DEPTH, 3 * D_MODEL), f32)
    w_in = jax.random.normal(ks[5], (DEPTH, D_MODEL, D_IN), f32) * (D_MODEL ** -0.5)
    w_out = jax.random.normal(ks[6], (DEPTH, D_MIX, D_MODEL), f32) * (D_MIX ** -0.5)
    final_g = 1.0 + 0.02 * jax.random.normal(ks[7], (D_MODEL,), f32)
    return {"x": x, "c": c, "norm_g": norm_g, "w_ada": w_ada, "b_ada": b_ada,
            "w_in": w_in, "w_out": w_out, "final_g": final_g}


def _fwd_reference(x, c, norm_g, w_ada, b_ada, w_in, w_out, final_g):
    pos = jnp.arange(x.shape[1], dtype=jnp.float32)
    c_act = jax.nn.silu(c.astype(jnp.float32))
    h = x.astype(jnp.float32)
    for layer in range(DEPTH):
        h = hybrid_layer(h, c_act, norm_g[layer], w_ada[layer], b_ada[layer],
                         w_in[layer], w_out[layer], pos)
    return rms_norm(h, final_g).astype(x.dtype)


import jax as _jax
import jax.numpy as _jnp

TWIN_FORMAT = 'train_step'
FWD_PARAMS = ['x', 'c', 'norm_g', 'w_ada', 'b_ada', 'w_in', 'w_out', 'final_g']
TWIN_WEIGHTS = ['norm_g', 'w_ada', 'b_ada', 'w_in', 'w_out', 'final_g']
TWIN_DIFF_INPUT = 'x'
TWIN_INPUTS = ['x', 'c', 'norm_g', 'w_ada', 'b_ada', 'w_in', 'w_out', 'final_g', 'loss_target', 'm_norm_g', 'm_w_ada', 'm_b_ada', 'm_w_in', 'm_w_out', 'm_final_g', 'v_norm_g', 'v_w_ada', 'v_b_ada', 'v_w_in', 'v_w_out', 'v_final_g']
TWIN_OUTPUTS = ['loss', 'grad_x', 'grad_norm_g', 'grad_w_ada', 'grad_b_ada', 'grad_w_in', 'grad_w_out', 'grad_final_g', 'delta_norm_g', 'delta_w_ada', 'delta_b_ada', 'delta_w_in', 'delta_w_out', 'delta_final_g', 'new_m_norm_g', 'new_m_w_ada', 'new_m_b_ada', 'new_m_w_in', 'new_m_w_out', 'new_m_final_g', 'new_v_norm_g', 'new_v_w_ada', 'new_v_b_ada', 'new_v_w_in', 'new_v_w_out', 'new_v_final_g']
TWIN_LEAF_KINDS = {'loss': 'loss', 'grad_x': 'grad_x', 'grad_norm_g': 'grad_w', 'grad_w_ada': 'grad_w', 'grad_b_ada': 'grad_w', 'grad_w_in': 'grad_w', 'grad_w_out': 'grad_w', 'grad_final_g': 'grad_w', 'delta_norm_g': 'delta_w', 'delta_w_ada': 'delta_w', 'delta_b_ada': 'delta_w', 'delta_w_in': 'delta_w', 'delta_w_out': 'delta_w', 'delta_final_g': 'delta_w', 'new_m_norm_g': 'new_m', 'new_m_w_ada': 'new_m', 'new_m_b_ada': 'new_m', 'new_m_w_in': 'new_m', 'new_m_w_out': 'new_m', 'new_m_final_g': 'new_m', 'new_v_norm_g': 'new_v', 'new_v_w_ada': 'new_v', 'new_v_b_ada': 'new_v', 'new_v_w_in': 'new_v', 'new_v_w_out': 'new_v', 'new_v_final_g': 'new_v'}


def _forward(args):
    return _fwd_reference(*[args[k] for k in FWD_PARAMS])


def _output_shape():
    def fwd():
        inp = _fwd_setup_inputs(0)
        return _fwd_reference(*[inp[k] for k in FWD_PARAMS])
    out = _jax.eval_shape(fwd)
    return out.shape, out.dtype

N_MICROBATCH = 1
ADAM_LR = 0.001
ADAM_B1 = 0.9
ADAM_B2 = 0.999
ADAM_EPS = 1e-08
ADAM_WD = 0.01
ADAM_STEP = 10
PER_EXAMPLE_BATCH_AXIS = {'x': 0, 'c': 0, 'loss_target': 0}
SHARED_INPUTS = []
_WEIGHT_DTYPES = {'norm_g': _jnp.float32, 'w_ada': _jnp.float32, 'b_ada': _jnp.float32, 'w_in': _jnp.float32, 'w_out': _jnp.float32, 'final_g': _jnp.float32}
MOMENT_SCALE = {'norm_g': 3.183078e-02, 'w_ada': 4.856624e-02, 'b_ada': 8.108708e-02, 'w_in': 1.501518e-02, 'w_out': 1.575674e-02, 'final_g': 6.393724e+01}


def _to_microbatches(a, axis):
    t = _jnp.moveaxis(a, axis, 0)
    t = t.reshape((N_MICROBATCH, t.shape[0] // N_MICROBATCH) + t.shape[1:])
    return _jnp.moveaxis(t, 1, axis + 1)


def setup_inputs(seed: int = 0) -> dict:
    inp = _fwd_setup_inputs(seed)
    key = _jax.random.fold_in(_jax.random.key(seed), 7919)
    shape, _ = _output_shape()
    out = dict(inp)
    out["loss_target"] = _jax.random.normal(_jax.random.fold_in(key, 0), shape, _jnp.float32)
    for i, name in enumerate(TWIN_WEIGHTS):
        w = inp[name].astype(_jnp.float32)
        if MOMENT_SCALE is None:
            s = _jnp.sqrt(_jnp.mean(_jnp.square(w)) + 1e-30)
        else:
            s = MOMENT_SCALE[name]
        km, kv = _jax.random.split(_jax.random.fold_in(key, i + 1))
        out[name] = w
        out["m_" + name] = s * _jax.random.normal(km, w.shape, _jnp.float32)
        out["v_" + name] = (s * s) * _jax.random.uniform(kv, w.shape, _jnp.float32, 0.5, 1.5)
    if N_MICROBATCH > 1:
        for name, axis in PER_EXAMPLE_BATCH_AXIS.items():
            out[name] = _to_microbatches(out[name], axis)
    return {'x': out['x'], 'c': out['c'], 'norm_g': out['norm_g'], 'w_ada': out['w_ada'], 'b_ada': out['b_ada'], 'w_in': out['w_in'], 'w_out': out['w_out'], 'final_g': out['final_g'], 'loss_target': out['loss_target'], 'm_norm_g': out['m_norm_g'], 'm_w_ada': out['m_w_ada'], 'm_b_ada': out['m_b_ada'], 'm_w_in': out['m_w_in'], 'm_w_out': out['m_w_out'], 'm_final_g': out['m_final_g'], 'v_norm_g': out['v_norm_g'], 'v_w_ada': out['v_w_ada'], 'v_b_ada': out['v_b_ada'], 'v_w_in': out['v_w_in'], 'v_w_out': out['v_w_out'], 'v_final_g': out['v_final_g']}


def _loss(weights, diff, rest, loss_target):
    with _jax.named_scope("forward"):
        args = {**rest, TWIN_DIFF_INPUT: diff, **{k: w.astype(_WEIGHT_DTYPES[k]) for k, w in weights.items()}}
        y = _forward(args)
    with _jax.named_scope("loss_head"):
        err = _jnp.square(y.astype(_jnp.float32) - loss_target)
        return 0.5 * _jnp.sum(_jnp.mean(err, axis=-1)) if err.ndim else 0.5 * err


def _adamw(w, g, m, v):
    m = ADAM_B1 * m + (1.0 - ADAM_B1) * g
    v = ADAM_B2 * v + (1.0 - ADAM_B2) * _jnp.square(g)
    m_hat = m / (1.0 - ADAM_B1 ** ADAM_STEP)
    v_hat = v / (1.0 - ADAM_B2 ** ADAM_STEP)
    delta = -ADAM_LR * (m_hat / (_jnp.sqrt(v_hat) + ADAM_EPS) + ADAM_WD * w)
    return delta, m, v


def reference(x, c, norm_g, w_ada, b_ada, w_in, w_out, final_g, loss_target, m_norm_g, m_w_ada, m_b_ada, m_w_in, m_w_out, m_final_g, v_norm_g, v_w_ada, v_b_ada, v_w_in, v_w_out, v_final_g):
    given = dict(x=x, c=c, norm_g=norm_g, w_ada=w_ada, b_ada=b_ada, w_in=w_in, w_out=w_out, final_g=final_g, loss_target=loss_target, m_norm_g=m_norm_g, m_w_ada=m_w_ada, m_b_ada=m_b_ada, m_w_in=m_w_in, m_w_out=m_w_out, m_final_g=m_final_g, v_norm_g=v_norm_g, v_w_ada=v_w_ada, v_b_ada=v_b_ada, v_w_in=v_w_in, v_w_out=v_w_out, v_final_g=v_final_g)
    weights = {n: given[n] for n in TWIN_WEIGHTS}
    shared = {n: given[n] for n in SHARED_INPUTS}
    per_example = {n: given[n] for n in ['x', 'c']}
    grad_fn = _jax.value_and_grad(_loss, argnums=(0, 1))

    def one_microbatch(ex, loss_target):
        ex = dict(ex)
        diff = ex.pop(TWIN_DIFF_INPUT)
        return grad_fn(weights, diff, {**shared, **ex}, loss_target)

    if N_MICROBATCH == 1:
        loss, (grad_w, grad_x) = one_microbatch(per_example, given["loss_target"])
    else:
        def body(carry, xs):
            loss_sum, grad_sum = carry
            l_k, (gw_k, gx_k) = one_microbatch(xs[0], xs[1])
            with _jax.named_scope("update"):
                return (loss_sum + l_k, _jax.tree.map(_jnp.add, grad_sum, gw_k)), gx_k

        init = (_jnp.zeros((), _jnp.float32), _jax.tree.map(_jnp.zeros_like, weights))
        (loss, grad_w), grad_x = _jax.lax.scan(body, init, (per_example, given["loss_target"]))
    with _jax.named_scope("update"):
        delta_w, new_m, new_v = {}, {}, {}
        for n in TWIN_WEIGHTS:
            delta_w[n], new_m[n], new_v[n] = _adamw(weights[n], grad_w[n], given["m_" + n], given["v_" + n])
    return (loss, grad_x, *[grad_w[n] for n in TWIN_WEIGHTS], *[delta_w[n] for n in TWIN_WEIGHTS],
            *[new_m[n] for n in TWIN_WEIGHTS], *[new_v[n] for n in TWIN_WEIGHTS])
```

```python
import functools

import jax
import jax.numpy as jnp
from jax import lax
from jax.experimental import pallas as pl
from jax.experimental.pallas import tpu as pltpu

F32 = jnp.float32
MXU_DTYPE = jnp.bfloat16

D_MODEL = 1024
DEPTH = 2
N_DEV = 8
CHUNK = 64
D_RET = 512
D_SB = 512
RET_HEADS = 4
RET_HEAD_DIM = 128
SB_HEADS = 8
SB_HEAD_DIM = 64
D_IN = 4096
ROPE_BASE = 10000.0
EPS = 1e-6
SB_SCALE = SB_HEAD_DIM ** -0.5
RET_KSCALE = RET_HEAD_DIM ** -0.5

ADAM_LR = 0.001
ADAM_B1 = 0.9
ADAM_B2 = 0.999
ADAM_EPS = 1e-08
ADAM_WD = 0.01
ADAM_STEP = 10

V7X_VMEM_BYTES = 64 * 2 ** 20
VMEM_LIMIT = V7X_VMEM_BYTES - 8 * 2 ** 20
LANES = 128

_NT = (((1,), (1,)), ((), ()))
_TN = (((0,), (0,)), ((), ()))


def _dot(a, b):
    return jnp.dot(a, b, preferred_element_type=F32)


def _dot_nt(a, b):
    return lax.dot_general(a, b, _NT, preferred_element_type=F32)


def _dot_tn(a, b):
    return lax.dot_general(a, b, _TN, preferred_element_type=F32)


def _mx(x):
    return x.astype(MXU_DTYPE)


def _sigmoid(x):
    return 1.0 / (1.0 + jnp.exp(-x))


def _params(sem=None):
    return pltpu.CompilerParams(dimension_semantics=sem, vmem_limit_bytes=VMEM_LIMIT)


def _row_tile(s):
    return min(512, s)


def _ln_proj(x, shift, scale1p, g, w_in_b):
    s = x.shape[0]
    ts = _row_tile(s)

    def body(x_ref, sh_ref, sc_ref, g_ref, w_ref, ret_ref, qkv_ref, sg_ref):
        xv = x_ref[...]
        rstd = lax.rsqrt(jnp.mean(xv * xv, axis=-1, keepdims=True) + EPS)
        h = (xv * rstd * g_ref[...]) * sc_ref[...] + sh_ref[...]
        hb = _mx(h)
        for n in range(4):
            ret_ref[:, n * 512:(n + 1) * 512] = _dot(hb, w_ref[:, n * 512:(n + 1) * 512])
        qkv_ref[:, 0:512] = _mx(_dot(hb, w_ref[:, 2048:2560]) * SB_SCALE)
        qkv_ref[:, 512:1024] = _mx(_dot(hb, w_ref[:, 2560:3072]))
        qkv_ref[:, 1024:1536] = _mx(_dot(hb, w_ref[:, 3072:3584]))
        sg_ref[...] = _dot(hb, w_ref[:, 3584:4096])

    vec = pl.BlockSpec((1, D_MODEL), lambda i: (0, 0))
    return pl.pallas_call(
        body, name="ln_proj", grid=(s // ts,),
        in_specs=[pl.BlockSpec((ts, D_MODEL), lambda i: (i, 0)), vec, vec, vec,
                  pl.BlockSpec((D_MODEL, D_IN), lambda i: (0, 0))],
        out_specs=[pl.BlockSpec((ts, 2048), lambda i: (i, 0)),
                   pl.BlockSpec((ts, 1536), lambda i: (i, 0)),
                   pl.BlockSpec((ts, 512), lambda i: (i, 0))],
        out_shape=[jax.ShapeDtypeStruct((s, 2048), F32),
                   jax.ShapeDtypeStruct((s, 1536), MXU_DTYPE),
                   jax.ShapeDtypeStruct((s, 512), F32)],
        compiler_params=_params(("arbitrary",)),
    )(x, shift, scale1p, g, w_in_b)


def _out_proj(x, gate, y_r, y_s, w_out_b):
    s = x.shape[0]
    ts = _row_tile(s)

    def body(x_ref, gate_ref, yr_ref, ys_ref, w_ref, o_ref):
        t = _dot(yr_ref[...], w_ref[0:512, :]) + _dot(ys_ref[...], w_ref[512:1024, :])
        o_ref[...] = x_ref[...] + gate_ref[...] * t

    return pl.pallas_call(
        body, name="out_proj", grid=(s // ts,),
        in_specs=[pl.BlockSpec((ts, D_MODEL), lambda i: (i, 0)),
                  pl.BlockSpec((1, D_MODEL), lambda i: (0, 0)),
                  pl.BlockSpec((ts, 512), lambda i: (i, 0)),
                  pl.BlockSpec((ts, 512), lambda i: (i, 0)),
                  pl.BlockSpec((D_MODEL, D_MODEL), lambda i: (0, 0))],
        out_specs=pl.BlockSpec((ts, D_MODEL), lambda i: (i, 0)),
        out_shape=jax.ShapeDtypeStruct((s, D_MODEL), F32),
        compiler_params=_params(("arbitrary",)),
    )(x, gate, y_r, y_s, w_out_b)


def _final_loss(x, fg, target):
    s = x.shape[0]
    ts = _row_tile(s)

    def body(x_ref, fg_ref, t_ref, dx_ref, loss_ref, dfg_ref):
        i = pl.program_id(0)

        @pl.when(i == 0)
        def _():
            loss_ref[...] = jnp.zeros_like(loss_ref)
            dfg_ref[...] = jnp.zeros_like(dfg_ref)

        xv = x_ref[...]
        fgv = fg_ref[...]
        rstd = lax.rsqrt(jnp.mean(xv * xv, axis=-1, keepdims=True) + EPS)
        xn = xv * rstd
        err = xn * fgv - t_ref[...]
        tok = jnp.mean(err * err, axis=-1, keepdims=True)
        loss_ref[...] += 0.5 * jnp.sum(tok, axis=0, keepdims=True)
        dy = err * (1.0 / D_MODEL)
        dfg_ref[...] += jnp.sum(dy * xn, axis=0, keepdims=True)
        dxn = dy * fgv
        dx_ref[...] = rstd * (dxn - xn * jnp.mean(dxn * xn, axis=-1, keepdims=True))

    return pl.pallas_call(
        body, name="final_loss", grid=(s // ts,),
        in_specs=[pl.BlockSpec((ts, D_MODEL), lambda i: (i, 0)),
                  pl.BlockSpec((1, D_MODEL), lambda i: (0, 0)),
                  pl.BlockSpec((ts, D_MODEL), lambda i: (i, 0))],
        out_specs=[pl.BlockSpec((ts, D_MODEL), lambda i: (i, 0)),
                   pl.BlockSpec((1, 1), lambda i: (0, 0)),
                   pl.BlockSpec((1, D_MODEL), lambda i: (0, 0))],
        out_shape=[jax.ShapeDtypeStruct((s, D_MODEL), F32),
                   jax.ShapeDtypeStruct((1, 1), F32),
                   jax.ShapeDtypeStruct((1, D_MODEL), F32)],
        compiler_params=_params(("arbitrary",)),
    )(x, fg, target)


def _out_proj_bwd(dx_out, gate, y_r, y_s, w_out_b):
    s = dx_out.shape[0]
    ts = _row_tile(s)

    def body(dx_ref, gate_ref, yr_ref, ys_ref, w_ref, dy_ref, dw_ref, dgate_ref):
        i = pl.program_id(0)

        @pl.when(i == 0)
        def _():
            dw_ref[...] = jnp.zeros_like(dw_ref)
            dgate_ref[...] = jnp.zeros_like(dgate_ref)

        dxv = dx_ref[...]
        dt = _mx(dxv * gate_ref[...])
        yr = yr_ref[...]
        ys = ys_ref[...]
        dy_ref[:, 0:512] = _dot_nt(dt, w_ref[0:512, :])
        dy_ref[:, 512:1024] = _dot_nt(dt, w_ref[512:1024, :])
        dw_ref[0:512, :] += _dot_tn(yr, dt)
        dw_ref[512:1024, :] += _dot_tn(ys, dt)
        t = _dot(yr, w_ref[0:512, :]) + _dot(ys, w_ref[512:1024, :])
        dgate_ref[...] += jnp.sum(dxv * t, axis=0, keepdims=True)

    return pl.pallas_call(
        body, name="out_proj_bwd", grid=(s // ts,),
        in_specs=[pl.BlockSpec((ts, D_MODEL), lambda i: (i, 0)),
                  pl.BlockSpec((1, D_MODEL), lambda i: (0, 0)),
                  pl.BlockSpec((ts, 512), lambda i: (i, 0)),
                  pl.BlockSpec((ts, 512), lambda i: (i, 0)),
                  pl.BlockSpec((D_MODEL, D_MODEL), lambda i: (0, 0))],
        out_specs=[pl.BlockSpec((ts, D_MODEL), lambda i: (i, 0)),
                   pl.BlockSpec((D_MODEL, D_MODEL), lambda i: (0, 0)),
                   pl.BlockSpec((1, D_MODEL), lambda i: (0, 0))],
        out_shape=[jax.ShapeDtypeStruct((s, D_MODEL), F32),
                   jax.ShapeDtypeStruct((D_MODEL, D_MODEL), F32),
                   jax.ShapeDtypeStruct((1, D_MODEL), F32)],
        compiler_params=_params(("arbitrary",)),
    )(dx_out, gate, y_r, y_s, w_out_b)


def _in_proj_bwd_x(x, dx_out, dproj, shift, scale1p, g, w_in_b):
    s = x.shape[0]
    ts = min(256, s)

    def body(x_ref, dxo_ref, dp_ref, sc_ref, g_ref, w_ref, dx_ref, dsh_ref, dsc_ref, dg_ref):
        i = pl.program_id(0)

        @pl.when(i == 0)
        def _():
            dsh_ref[...] = jnp.zeros_like(dsh_ref)
            dsc_ref[...] = jnp.zeros_like(dsc_ref)
            dg_ref[...] = jnp.zeros_like(dg_ref)

        dh = _dot_nt(dp_ref[...], w_ref[...])
        xv = x_ref[...]
        gv = g_ref[...]
        scv = sc_ref[...]
        rstd = lax.rsqrt(jnp.mean(xv * xv, axis=-1, keepdims=True) + EPS)
        xn = xv * rstd
        dsh_ref[...] += jnp.sum(dh, axis=0, keepdims=True)
        dsc_ref[...] += jnp.sum(dh * (xn * gv), axis=0, keepdims=True)
        dhs = dh * scv
        dg_ref[...] += jnp.sum(dhs * xn, axis=0, keepdims=True)
        dxn = dhs * gv
        dx_ref[...] = rstd * (dxn - xn * jnp.mean(dxn * xn, axis=-1, keepdims=True)) + dxo_ref[...]

    del shift
    vec = pl.BlockSpec((1, D_MODEL), lambda i: (0, 0))
    return pl.pallas_call(
        body, name="in_proj_bwd_x", grid=(s // ts,),
        in_specs=[pl.BlockSpec((ts, D_MODEL), lambda i: (i, 0)),
                  pl.BlockSpec((ts, D_MODEL), lambda i: (i, 0)),
                  pl.BlockSpec((ts, D_IN), lambda i: (i, 0)),
                  vec, vec,
                  pl.BlockSpec((D_MODEL, D_IN), lambda i: (0, 0))],
        out_specs=[pl.BlockSpec((ts, D_MODEL), lambda i: (i, 0)), vec, vec, vec],
        out_shape=[jax.ShapeDtypeStruct((s, D_MODEL), F32),
                   jax.ShapeDtypeStruct((1, D_MODEL), F32),
                   jax.ShapeDtypeStruct((1, D_MODEL), F32),
                   jax.ShapeDtypeStruct((1, D_MODEL), F32)],
        compiler_params=_params(("arbitrary",)),
    )(x, dx_out, dproj, scale1p, g, w_in_b)


def _in_proj_bwd_w(x, dproj, shift, scale1p, g):
    s = x.shape[0]
    ts = _row_tile(s)
    nb = D_IN // N_DEV

    def body(x_ref, dp_ref, sh_ref, sc_ref, g_ref, dw_ref):
        i = pl.program_id(1)

        @pl.when(i == 0)
        def _():
            dw_ref[...] = jnp.zeros_like(dw_ref)

        xv = x_ref[...]
        rstd = lax.rsqrt(jnp.mean(xv * xv, axis=-1, keepdims=True) + EPS)
        h = (xv * rstd * g_ref[...]) * sc_ref[...] + sh_ref[...]
        dw_ref[...] += _dot_tn(_mx(h), dp_ref[...])

    vec = pl.BlockSpec((1, D_MODEL), lambda j, i: (0, 0))
    return pl.pallas_call(
        body, name="in_proj_bwd_w", grid=(N_DEV, s // ts),
        in_specs=[pl.BlockSpec((ts, D_MODEL), lambda j, i: (i, 0)),
                  pl.BlockSpec((ts, nb), lambda j, i: (i, j)),
                  vec, vec, vec],
        out_specs=pl.BlockSpec((None, D_MODEL, nb), lambda j, i: (j, 0, 0)),
        out_shape=jax.ShapeDtypeStruct((N_DEV, D_MODEL, nb), F32),
        compiler_params=_params(("arbitrary", "arbitrary")),
    )(x, dproj, shift, scale1p, g)


RET_TILE = 256


def _ret_tables(s):
    t = min(RET_TILE, s)
    half = RET_HEAD_DIM // 2
    pos = jnp.arange(s, dtype=F32)
    inv = ROPE_BASE ** (-jnp.arange(half, dtype=F32) / half)
    ang = pos[:, None] * inv[None, :]
    cos, sin = jnp.cos(ang), jnp.sin(ang)
    cos2 = jnp.concatenate([cos, cos], axis=1)
    sin2 = jnp.concatenate([-sin, sin], axis=1)
    lg = jnp.log1p(-(2.0 ** (-5.0 - jnp.arange(RET_HEADS, dtype=F32))))[:, None, None]
    n = jnp.arange(t)
    dist = (n[:, None] - n[None, :]).astype(F32)[None]
    cn = (n // CHUNK)[:, None]
    cm = (n // CHUNK)[None, :]
    mask = jnp.where((cn == cm)[None], jnp.exp(jnp.abs(dist) * lg),
                     jnp.where((cm < cn)[None], jnp.exp(dist * lg), 0.0))
    nf = n.astype(F32)[None, :, None]
    dq = jnp.broadcast_to(jnp.exp((nf + 1.0) * lg), (RET_HEADS, t, LANES))
    dk = jnp.broadcast_to(jnp.exp((t - 1.0 - nf) * lg), (RET_HEADS, t, LANES))
    gt = jnp.broadcast_to(jnp.exp(float(t) * lg), (RET_HEADS, 1, LANES))
    return cos2, sin2, mask, dq, dk, gt


def _roll_half(x):
    return pltpu.roll(x, RET_HEAD_DIM // 2, 1)


def _ret_head_fwd(q, k, v, cos, sin, m, dq, dk, s0):
    qr = q * cos + _roll_half(q) * sin
    kr = (k * cos + _roll_half(k) * sin) * RET_KSCALE
    qb, kb, vb = _mx(qr), _mx(kr), _mx(v)
    p = _dot_nt(qb, kb) * m
    pb = _mx(p)
    o = _dot(pb, vb) + _dot(qb, _mx(s0)) * dq
    kdb = _mx(kr * dk)
    mu = jnp.mean(o, axis=-1, keepdims=True)
    oc = o - mu
    rstd = lax.rsqrt(jnp.mean(oc * oc, axis=-1, keepdims=True) + EPS)
    return qb, kb, vb, pb, kdb, oc * rstd, rstd


def _retention_fwd(ret, tables):
    cos2, sin2, mask, dq, dk, gt = tables
    s = ret.shape[0]
    t = mask.shape[1]
    nt = s // t
    hd = RET_HEAD_DIM

    def body(ret_ref, cos_ref, sin_ref, m_ref, dq_ref, dk_ref, gt_ref, y_ref, st_ref, s_scr):
        i = pl.program_id(0)

        @pl.when(i == 0)
        def _():
            s_scr[...] = jnp.zeros_like(s_scr)

        cos = cos_ref[...]
        sin = sin_ref[...]
        for h in range(RET_HEADS):
            q = ret_ref[:, h * hd:(h + 1) * hd]
            k = ret_ref[:, 512 + h * hd:512 + (h + 1) * hd]
            v = ret_ref[:, 1024 + h * hd:1024 + (h + 1) * hd]
            g = ret_ref[:, 1536 + h * hd:1536 + (h + 1) * hd]
            s0 = s_scr[h]
            st_ref[h] = s0
            _, _, vb, _, kdb, gn, _ = _ret_head_fwd(q, k, v, cos, sin, m_ref[h], dq_ref[h], dk_ref[h], s0)
            y_ref[:, h * hd:(h + 1) * hd] = (gn * (g * _sigmoid(g))).astype(y_ref.dtype)
            s_scr[h] = s0 * gt_ref[h] + _dot_tn(kdb, vb)

    full3 = lambda a: pl.BlockSpec(a.shape, lambda i: (0, 0, 0))
    return pl.pallas_call(
        body, name="retention_fwd", grid=(nt,),
        in_specs=[pl.BlockSpec((t, 2048), lambda i: (i, 0)),
                  pl.BlockSpec((t, LANES), lambda i: (i, 0)),
                  pl.BlockSpec((t, LANES), lambda i: (i, 0)),
                  full3(mask), full3(dq), full3(dk), full3(gt)],
        out_specs=[pl.BlockSpec((t, 512), lambda i: (i, 0)),
                   pl.BlockSpec((None, RET_HEADS, hd, hd), lambda i: (i, 0, 0, 0))],
        out_shape=[jax.ShapeDtypeStruct((s, 512), MXU_DTYPE),
                   jax.ShapeDtypeStruct((nt, RET_HEADS, hd, hd), F32)],
        scratch_shapes=[pltpu.VMEM((RET_HEADS, hd, hd), F32)],
        compiler_params=_params(("arbitrary",)),
    )(ret, cos2, sin2, mask, dq, dk, gt)


def _retention_bwd(ret, states, dy, tables):
    cos2, sin2, mask, dq, dk, gt = tables
    s = ret.shape[0]
    t = mask.shape[1]
    nt = s // t
    hd = RET_HEAD_DIM

    def body(ret_ref, st_ref, dy_ref, cos_ref, sin_ref, m_ref, dq_ref, dk_ref, gt_ref, d_ref, ds_scr):
        i = pl.program_id(0)

        @pl.when(i == 0)
        def _():
            ds_scr[...] = jnp.zeros_like(ds_scr)

        cos = cos_ref[...]
        sin = sin_ref[...]
        for h in range(RET_HEADS):
            q = ret_ref[:, h * hd:(h + 1) * hd]
            k = ret_ref[:, 512 + h * hd:512 + (h + 1) * hd]
            v = ret_ref[:, 1024 + h * hd:1024 + (h + 1) * hd]
            g = ret_ref[:, 1536 + h * hd:1536 + (h + 1) * hd]
            dyv = dy_ref[:, h * hd:(h + 1) * hd]
            m = m_ref[h]
            dqv = dq_ref[h]
            dkv = dk_ref[h]
            s0 = st_ref[h]
            ds = ds_scr[h]
            qb, kb, vb, pb, kdb, gn, rstd = _ret_head_fwd(q, k, v, cos, sin, m, dqv, dkv, s0)
            sg = _sigmoid(g)
            d_ref[:, 1536 + h * hd:1536 + (h + 1) * hd] = (
                dyv * gn * (sg * (1.0 + g * (1.0 - sg)))).astype(d_ref.dtype)
            dgn = dyv * (g * sg)
            do = rstd * (dgn - jnp.mean(dgn, axis=-1, keepdims=True)
                         - gn * jnp.mean(dgn * gn, axis=-1, keepdims=True))
            dob = _mx(do)
            dodb = _mx(do * dqv)
            dsb = _mx(ds)
            d_ref[:, 1024 + h * hd:1024 + (h + 1) * hd] = (
                _dot_tn(pb, dob) + _dot(kdb, dsb)).astype(d_ref.dtype)
            dpb = _mx(_dot_nt(dob, vb) * m)
            dqr = _dot(dpb, kb) + _dot_nt(dodb, _mx(s0))
            dkr = (_dot_tn(dpb, qb) + _dot_nt(vb, dsb) * dkv) * RET_KSCALE
            d_ref[:, h * hd:(h + 1) * hd] = (dqr * cos + _roll_half(dqr * sin)).astype(d_ref.dtype)
            d_ref[:, 512 + h * hd:512 + (h + 1) * hd] = (
                dkr * cos + _roll_half(dkr * sin)).astype(d_ref.dtype)
            ds_scr[h] = ds * gt_ref[h] + _dot_tn(qb, dodb)

    full3 = lambda a: pl.BlockSpec(a.shape, lambda i: (0, 0, 0))
    rev = lambda i: (nt - 1 - i, 0)
    return pl.pallas_call(
        body, name="retention_bwd", grid=(nt,),
        in_specs=[pl.BlockSpec((t, 2048), rev),
                  pl.BlockSpec((None, RET_HEADS, hd, hd), lambda i: (nt - 1 - i, 0, 0, 0)),
                  pl.BlockSpec((t, 512), rev),
                  pl.BlockSpec((t, LANES), rev),
                  pl.BlockSpec((t, LANES), rev),
                  full3(mask), full3(dq), full3(dk), full3(gt)],
        out_specs=pl.BlockSpec((t, 2048), rev),
        out_shape=jax.ShapeDtypeStruct((s, 2048), MXU_DTYPE),
        scratch_shapes=[pltpu.VMEM((RET_HEADS, hd, hd), F32)],
        compiler_params=_params(("arbitrary",)),
    )(ret, states, dy, cos2, sin2, mask, dq, dk, gt)


SB_BLOCK = 256


def _split_dot(x, u):
    hi = _mx(x)
    lo = _mx(x - hi.astype(F32))
    return _dot(hi, u) + _dot(lo, u)


def _two_sum(a, b):
    s = a + b
    bb = s - a
    return s, (a - (s - bb)) + (b - bb)


def _acc2(hi, lo, x):
    s, e = _two_sum(hi, x)
    e = e + lo
    hi2 = s + e
    return hi2, e - (hi2 - s)


def _sb_logits(q, k, causal):
    z = _dot_nt(q, k)
    l1p = jnp.log1p(jnp.exp(-jnp.abs(z)))
    lk = -(jnp.maximum(z, 0.0) + l1p)
    if causal is not None:
        lk = jnp.where(causal, lk, 0.0)
    return jnp.minimum(z, 0.0) - l1p, lk


def _sb_fwd(qkv, sg):
    s = qkv.shape[0]
    blk = min(SB_BLOCK, s)
    nq = s // blk
    hd = SB_HEAD_DIM

    def body(q_ref, k_ref, v_ref, g_ref, y_ref, o_ref, t_ref):
        i = pl.program_id(1)
        row = lax.broadcasted_iota(jnp.int32, (blk, blk), 0)
        col = lax.broadcasted_iota(jnp.int32, (blk, blk), 1)
        causal = row > col
        u_gt = causal.astype(MXU_DTYPE)
        lane = lax.broadcasted_iota(jnp.int32, (blk, hd), 1)
        for hh in range(2):
            ls = slice(hh * hd, (hh + 1) * hd)
            q = q_ref[:, ls]

            def block(j, rh, rl, causal_mask):
                start = pl.multiple_of(j * blk, blk)
                lb, lk = _sb_logits(q, k_ref[pl.ds(start, blk), ls], causal_mask)
                a = jnp.exp(lb + _split_dot(lk, u_gt) + rh)
                if causal_mask is not None:
                    a = jnp.where(causal_mask, a, 0.0)
                rh, rl = _acc2(rh, rl, jnp.sum(lk, axis=1, keepdims=True))
                return _dot(_mx(a), v_ref[pl.ds(start, blk), ls]), rh, rl

            zero = jnp.zeros((blk, 1), F32)
            first = block(i, zero, zero, causal)

            def step(jj, carry):
                acc, rh, rl = carry
                pv, rh, rl = block(i - 1 - jj, rh, rl, None)
                return acc + pv, rh, rl

            acc, rh, rl = lax.fori_loop(0, i, step, first)
            g = g_ref[:, ls]
            o_ref[:, ls] = acc
            y_ref[:, ls] = (acc * (g * _sigmoid(g))).astype(y_ref.dtype)
            t_ref[:, ls] = jnp.where(lane < hd // 2, rh, rl)

    qblk = pl.BlockSpec((blk, LANES), lambda p, i: (i, p))
    return pl.pallas_call(
        body, name="stickbreak_fwd", grid=(SB_HEADS // 2, nq),
        in_specs=[qblk,
                  pl.BlockSpec((s, LANES), lambda p, i: (0, 4 + p)),
                  pl.BlockSpec((s, LANES), lambda p, i: (0, 8 + p)),
                  qblk],
        out_specs=[qblk, qblk, qblk],
        out_shape=[jax.ShapeDtypeStruct((s, 512), MXU_DTYPE),
                   jax.ShapeDtypeStruct((s, 512), F32),
                   jax.ShapeDtypeStruct((s, 512), F32)],
        compiler_params=_params(("arbitrary", "arbitrary")),
    )(qkv, qkv, qkv, sg)


def _sb_bwd(qkv, sg, o, tot, dy):
    s = qkv.shape[0]
    blk = min(SB_BLOCK, s)
    nq = s // blk
    hd = SB_HEAD_DIM

    def body(q_ref, k_ref, v_ref, g_ref, o_ref, t_ref, dy_ref, dq_ref, dk_ref, dv_ref, dg_ref, dk_scr, dv_scr):
        i = pl.program_id(1)

        @pl.when(i == 0)
        def _():
            dk_scr[...] = jnp.zeros_like(dk_scr)
            dv_scr[...] = jnp.zeros_like(dv_scr)

        row = lax.broadcasted_iota(jnp.int32, (blk, blk), 0)
        col = lax.broadcasted_iota(jnp.int32, (blk, blk), 1)
        causal = row > col
        u_le = (row <= col).astype(MXU_DTYPE)
        u_lt = (row < col).astype(MXU_DTYPE)
        for hh in range(2):
            ls = slice(hh * hd, (hh + 1) * hd)
            q = q_ref[:, ls]
            g = g_ref[:, ls]
            ov = o_ref[:, ls]
            dyv = dy_ref[:, ls]
            sgm = _sigmoid(g)
            dg_ref[:, ls] = (dyv * ov * (sgm * (1.0 + g * (1.0 - sgm)))).astype(dg_ref.dtype)
            dob = _mx(dyv * (g * sgm))

            def block(j, wh, wl, pg, causal_mask):
                start = pl.multiple_of(j * blk, blk)
                k = k_ref[pl.ds(start, blk), ls]
                v = v_ref[pl.ds(start, blk), ls]
                lb, lk = _sb_logits(q, k, causal_mask)
                a = jnp.exp(lb + ((wh - _split_dot(lk, u_le)) + wl))
                if causal_mask is not None:
                    a = jnp.where(causal_mask, a, 0.0)
                gm = _dot_nt(dob, v) * a
                dz = gm - (gm + (pg + _split_dot(gm, u_lt))) * jnp.exp(lb)
                if causal_mask is not None:
                    dz = jnp.where(causal_mask, dz, 0.0)
                dzb = _mx(dz)
                dk_scr[hh, pl.ds(start, blk), :] += _dot_tn(dzb, q)
                dv_scr[hh, pl.ds(start, blk), :] += _dot_tn(_mx(a), dob)
                wh, wl = _acc2(wh, wl, -jnp.sum(lk, axis=1, keepdims=True))
                return _dot(dzb, k), wh, wl, pg + jnp.sum(gm, axis=1, keepdims=True)

            def step(j, carry):
                acc, wh, wl, pg = carry
                dqp, wh, wl, pg = block(j, wh, wl, pg, None)
                return acc + dqp, wh, wl, pg

            tv = t_ref[:, ls]
            init = (jnp.zeros((blk, hd), F32), tv[:, 0:1], tv[:, hd // 2:hd // 2 + 1], jnp.zeros((blk, 1), F32))
            acc, wh, wl, pg = lax.fori_loop(0, i, step, init)
            dqp, _, _, _ = block(i, wh, wl, pg, causal)
            dq_ref[:, ls] = ((acc + dqp) * SB_SCALE).astype(dq_ref.dtype)

        @pl.when(i == nq - 1)
        def _():
            for hh in range(2):
                ls = slice(hh * hd, (hh + 1) * hd)
                dk_ref[:, ls] = dk_scr[hh].astype(dk_ref.dtype)
                dv_ref[:, ls] = dv_scr[hh].astype(dv_ref.dtype)

    qblk = lambda c0: pl.BlockSpec((blk, LANES), lambda p, i: (i, c0 + p))
    full = lambda c0: pl.BlockSpec((s, LANES), lambda p, i: (0, c0 + p))
    half = jax.ShapeDtypeStruct((s, 512), MXU_DTYPE)
    return pl.pallas_call(
        body, name="stickbreak_bwd", grid=(SB_HEADS // 2, nq),
        in_specs=[qblk(0), full(4), full(8), qblk(0), qblk(0), qblk(0), qblk(4)],
        out_specs=[qblk(0), full(0), full(0), qblk(0)],
        out_shape=[half, half, half, half],
        scratch_shapes=[pltpu.VMEM((2, s, hd), F32), pltpu.VMEM((2, s, hd), F32)],
        compiler_params=_params(("arbitrary", "arbitrary")),
    )(qkv, qkv, qkv, sg, o, tot, dy)


def _layer_fwd(x, mod, norm_g, w_in_b, w_out_b, tables):
    shift, scale1p, gate = mod[0:1], 1.0 + mod[1:2], mod[2:3]
    ret, qkv, sg = _ln_proj(x, shift, scale1p, norm_g, w_in_b)
    y_r, states = _retention_fwd(ret, tables)
    y_s, o_s, tot_s = _sb_fwd(qkv, sg)
    x_next = _out_proj(x, gate, y_r, y_s, w_out_b)
    saved = (x, shift, scale1p, gate, ret, qkv, sg, y_r, states, y_s, o_s, tot_s)
    return x_next, saved


def _layer_bwd(dx_out, saved, norm_g, w_in_b, w_out_b, tables):
    x, shift, scale1p, gate, ret, qkv, sg, y_r, states, y_s, o_s, tot_s = saved
    dy, dw_out, dgate = _out_proj_bwd(dx_out, gate, y_r, y_s, w_out_b)
    d_ret = _retention_bwd(ret, states, dy, tables)
    d_q, d_k, d_v, d_g = _sb_bwd(qkv, sg, o_s, tot_s, dy)
    dproj = jnp.concatenate([d_ret, d_q, d_k, d_v, d_g], axis=1)
    dx, dshift, dscale, dnorm_g = _in_proj_bwd_x(x, dx_out, dproj, shift, scale1p, norm_g, w_in_b)
    dw_in = _in_proj_bwd_w(x, dproj, shift, scale1p, norm_g)
    dmod = jnp.concatenate([dshift, dscale, dgate], axis=1)
    return dx, dw_in, dw_out, dmod, dnorm_g


MESH_IDS = pl.DeviceIdType.MESH
N_PEERS = N_DEV - 1
HBM_SPEC = pl.BlockSpec(memory_space=pl.ANY)


def _my_place():
    return lax.axis_index("x"), lax.axis_index("y"), lax.axis_index("c")


def _linear(px, py, pc):
    return 4 * px + 2 * py + pc


def _all_gather(blocks):
    n_arr = len(blocks)

    def body(*refs):
        x_refs = refs[:n_arr]
        out_refs = refs[n_arr:2 * n_arr]
        send_sems, recv_sems, local_sems = refs[2 * n_arr:]
        x, y, c = _my_place()
        me, sibling = (x, y, c), (x, y, 1 - c)
        chips = [(1 - x, y), (x, 1 - y), (1 - x, 1 - y)]

        def rows(a, place):
            m = x_refs[a].shape[0]
            return out_refs[a].at[pl.ds(_linear(*place) * m, m), :]

        def copy(a, k, block, to, src=None):
            return pltpu.make_async_remote_copy(
                src_ref=rows(a, block) if src is None else src, dst_ref=rows(a, block),
                send_sem=send_sems.at[a * N_PEERS + k], recv_sem=recv_sems.at[a * N_PEERS + k],
                device_id=to, device_id_type=MESH_IDS)

        mine = [pltpu.make_async_copy(x_refs[a], rows(a, me), local_sems.at[a]) for a in range(n_arr)]
        for cp in mine:
            cp.start()
        first = []
        for a in range(n_arr):
            first.append(copy(a, 0, me, sibling, src=x_refs[a]))
            first += [copy(a, 1 + j, me, (*chip, c), src=x_refs[a]) for j, chip in enumerate(chips)]
        for cp in first:
            cp.start()
        passed = []
        for j, chip in enumerate(chips):
            for a in range(n_arr):
                copy(a, 1 + j, (*chip, c), me).wait_recv()
                fwd = copy(a, 4 + j, (*chip, c), sibling)
                fwd.start()
                passed.append(fwd)
        for a in range(n_arr):
            copy(a, 0, sibling, me).wait_recv()
            for j, chip in enumerate(chips):
                copy(a, 4 + j, (*chip, 1 - c), me).wait_recv()
        for cp in first + passed:
            cp.wait_send()
        for cp in mine:
            cp.wait()

    return pl.pallas_call(
        body, name="all_gather",
        out_shape=[jax.ShapeDtypeStruct((N_DEV * b.shape[0], b.shape[1]), b.dtype) for b in blocks],
        in_specs=[HBM_SPEC] * n_arr, out_specs=[HBM_SPEC] * n_arr,
        scratch_shapes=[pltpu.SemaphoreType.DMA((n_arr * N_PEERS,)),
                        pltpu.SemaphoreType.DMA((n_arr * N_PEERS,)),
                        pltpu.SemaphoreType.DMA((n_arr,))],
    )(*blocks)


def _exchange(parts):
    n_arr = len(parts)

    def body(*refs):
        p_refs = refs[:n_arr]
        o_refs = refs[n_arr:2 * n_arr]
        send_sems, recv_sems, local_sems = refs[2 * n_arr:]
        x, y, c = _my_place()
        me = _linear(x, y, c)
        mine = [pltpu.make_async_copy(p_refs[a].at[me], o_refs[a].at[me], local_sems.at[a]) for a in range(n_arr)]
        for cp in mine:
            cp.start()
        copies = []
        for r in range(1, N_DEV):
            peer = (1 - x if r & 4 else x, 1 - y if r & 2 else y, 1 - c if r & 1 else c)
            for a in range(n_arr):
                cp = pltpu.make_async_remote_copy(
                    src_ref=p_refs[a].at[_linear(*peer)], dst_ref=o_refs[a].at[me],
                    send_sem=send_sems.at[a * N_PEERS + r - 1], recv_sem=recv_sems.at[a * N_PEERS + r - 1],
                    device_id=peer, device_id_type=MESH_IDS)
                cp.start()
                copies.append(cp)
        for cp in copies:
            cp.wait_recv()
        for cp in copies:
            cp.wait_send()
        for cp in mine:
            cp.wait()

    return pl.pallas_call(
        body, name="exchange",
        out_shape=[jax.ShapeDtypeStruct(p.shape, p.dtype) for p in parts],
        in_specs=[HBM_SPEC] * n_arr, out_specs=[HBM_SPEC] * n_arr,
        scratch_shapes=[pltpu.SemaphoreType.DMA((n_arr * N_PEERS,)),
                        pltpu.SemaphoreType.DMA((n_arr * N_PEERS,)),
                        pltpu.SemaphoreType.DMA((n_arr,))],
    )(*parts)


def _ada_fwd(c_all, w_ada, b_cols):
    cols = w_ada.shape[2]

    def body(c_ref, w_ref, b_ref, ca_ref, mod_ref):
        cv = c_ref[...]
        ca = cv * _sigmoid(cv)
        ca_ref[...] = ca
        cb = _mx(jnp.concatenate([ca, ca], axis=0))
        for l in range(DEPTH):
            mod_ref[l * N_DEV:(l + 1) * N_DEV, :] = _dot(cb, _mx(w_ref[l]))[0:N_DEV] + b_ref[l]

    return pl.pallas_call(
        body, name="ada_fwd",
        out_shape=[jax.ShapeDtypeStruct((N_DEV, D_MODEL), F32),
                   jax.ShapeDtypeStruct((DEPTH * N_DEV, cols), F32)],
        compiler_params=_params(),
    )(c_all, w_ada, b_cols)


def _ada_bwd(c_act_t, dmod_cols):
    cols = dmod_cols.shape[2]

    def body(ca_ref, dm_ref, o_ref):
        ca = _mx(ca_ref[...]).astype(F32)
        for l in range(DEPTH):
            o_ref[l] = jnp.dot(ca, _mx(dm_ref[l]).astype(F32),
                               precision=lax.Precision.HIGHEST, preferred_element_type=F32)

    return pl.pallas_call(
        body, name="ada_bwd",
        out_shape=jax.ShapeDtypeStruct((DEPTH, D_MODEL, cols), F32),
        compiler_params=_params(),
    )(c_act_t, dmod_cols)


def _sum_adamw(parts, w, m, v):
    n_slab, rows, cols = parts.shape
    tr = min(256, rows)

    def body(p_ref, w_ref, m_ref, v_ref, g_ref, d_ref, mo_ref, vo_ref):
        g = p_ref[0]
        for sl in range(1, n_slab):
            g = g + p_ref[sl]
        m2 = ADAM_B1 * m_ref[...] + (1.0 - ADAM_B1) * g
        v2 = ADAM_B2 * v_ref[...] + (1.0 - ADAM_B2) * (g * g)
        m_hat = m2 / (1.0 - ADAM_B1 ** ADAM_STEP)
        v_hat = v2 / (1.0 - ADAM_B2 ** ADAM_STEP)
        g_ref[...] = g
        d_ref[...] = -ADAM_LR * (m_hat / (jnp.sqrt(v_hat) + ADAM_EPS) + ADAM_WD * w_ref[...])
        mo_ref[...] = m2
        vo_ref[...] = v2

    blk = pl.BlockSpec((tr, cols), lambda i: (i, 0))
    shp = jax.ShapeDtypeStruct((rows, cols), F32)
    return pl.pallas_call(
        body, name="sum_adamw", grid=(rows // tr,),
        in_specs=[pl.BlockSpec((n_slab, tr, cols), lambda i: (0, i, 0)), blk, blk, blk],
        out_specs=[blk, blk, blk, blk],
        out_shape=[shp, shp, shp, shp],
        compiler_params=_params(("arbitrary",)),
    )(parts, w, m, v)


SMALL_ROWS = 16


def kernel(x, c, norm_g, w_ada, b_ada, w_in, w_out, final_g, loss_target, m_norm_g, m_w_ada, m_b_ada, m_w_in, m_w_out, m_final_g, v_norm_g, v_w_ada, v_b_ada, v_w_in, v_w_out, v_final_g):
    me = _linear(*_my_place())
    in_cols = w_in.shape[2]
    out_rows = w_out.shape[1]
    ada_cols = w_ada.shape[2]

    g_in, g_out, g_c = _all_gather([
        _mx(w_in).reshape(DEPTH * D_MODEL, in_cols),
        _mx(w_out).reshape(DEPTH * out_rows, D_MODEL),
        jnp.broadcast_to(c, (8, D_MODEL))])
    w_in_b = g_in.reshape(N_DEV, DEPTH, D_MODEL, in_cols).transpose(1, 2, 0, 3).reshape(DEPTH, D_MODEL, D_IN)
    w_out_b = g_out.reshape(N_DEV, DEPTH, out_rows, D_MODEL).transpose(1, 0, 2, 3).reshape(DEPTH, D_MODEL, D_MODEL)
    c_all = g_c.reshape(N_DEV, 8, D_MODEL)[:, 0]

    b_cols = lax.dynamic_slice_in_dim(b_ada, me * ada_cols, ada_cols, axis=1)[:, None, :]
    c_act, mod_cols = _ada_fwd(c_all, w_ada, b_cols)
    (g_mod,) = _all_gather([mod_cols])
    g_mod = g_mod.reshape(N_DEV, DEPTH, N_DEV, ada_cols)
    mod = lax.dynamic_index_in_dim(g_mod, me, axis=2, keepdims=False)
    mod = mod.transpose(1, 0, 2).reshape(DEPTH, 3, D_MODEL)

    tables = _ret_tables(x.shape[1])
    h = x[0]
    saved = []
    for l in range(DEPTH):
        h, sv = _layer_fwd(h, mod[l], norm_g[l:l + 1], w_in_b[l], w_out_b[l], tables)
        saved.append(sv)
    dx, loss_part, dfg = _final_loss(h, final_g[None], loss_target[0])
    dw_in, dw_out, small = [None] * DEPTH, [None] * DEPTH, [None] * DEPTH
    for l in reversed(range(DEPTH)):
        dx, dw_in[l], dwo, dmod, dng = _layer_bwd(dx, saved[l], norm_g[l:l + 1], w_in_b[l], w_out_b[l], tables)
        dw_out[l] = dwo.reshape(N_DEV, out_rows, D_MODEL)
        small[l] = (dmod.reshape(3, D_MODEL), dng)

    pad = jnp.zeros((SMALL_ROWS - 10, D_MODEL), F32)
    small_block = jnp.concatenate([small[0][0], small[1][0], small[0][1], small[1][1], dfg,
                                   jnp.broadcast_to(loss_part, (1, D_MODEL)), pad], axis=0)
    (g_small,) = _all_gather([small_block])
    g_small = g_small.reshape(N_DEV, SMALL_ROWS, D_MODEL)

    def small_pack(b, n, f, fill):
        return jnp.concatenate([b.reshape(6, D_MODEL), n, f[None],
                                jnp.full((SMALL_ROWS - 9, D_MODEL), fill, F32)], axis=0)

    s_g, s_d, s_m, s_v = _sum_adamw(g_small, small_pack(b_ada, norm_g, final_g, 0.0),
                                    small_pack(m_b_ada, m_norm_g, m_final_g, 0.0),
                                    small_pack(v_b_ada, v_norm_g, v_final_g, 1.0))
    loss = s_g[9, 0]

    def small_unpack(a):
        return a[0:6].reshape(DEPTH, 3 * D_MODEL), a[6:8], a[8]

    dmod_all = g_small[:, 0:6].reshape(N_DEV, DEPTH, 3 * D_MODEL).transpose(1, 0, 2)
    dmod_cols = lax.dynamic_slice_in_dim(dmod_all, me * ada_cols, ada_cols, axis=2)
    g_ada = _ada_bwd(c_act.T, dmod_cols).reshape(1, DEPTH * D_MODEL, ada_cols)
    ada = _sum_adamw(g_ada, *[a.reshape(DEPTH * D_MODEL, ada_cols) for a in (w_ada, m_w_ada, v_w_ada)])
    ada = [a.reshape(DEPTH, D_MODEL, ada_cols) for a in ada]

    recv = _exchange([dw_in[0], dw_in[1], dw_out[0], dw_out[1]])
    win = [_sum_adamw(recv[l], w_in[l], m_w_in[l], v_w_in[l]) for l in range(DEPTH)]
    wout = [_sum_adamw(recv[DEPTH + l], w_out[l], m_w_out[l], v_w_out[l]) for l in range(DEPTH)]
    win = [jnp.stack([win[0][k], win[1][k]]) for k in range(4)]
    wout = [jnp.stack([wout[0][k], wout[1][k]]) for k in range(4)]

    outs = [loss, dx[None]]
    for k in range(4):
        b, n, f = small_unpack((s_g, s_d, s_m, s_v)[k])
        outs += [n, ada[k], b, win[k], wout[k], f]
    return tuple(outs)
```

```python
import functools

import jax
import jax.numpy as jnp
from jax import lax
from jax.experimental import pallas as pl
from jax.experimental.pallas import tpu as pltpu

F32 = jnp.float32
MXU_DTYPE = jnp.bfloat16

D_MODEL = 1024
DEPTH = 2
N_DEV = 8
CHUNK = 64
D_RET = 512
D_SB = 512
RET_HEADS = 4
RET_HEAD_DIM = 128
SB_HEADS = 8
SB_HEAD_DIM = 64
D_IN = 4096
ROPE_BASE = 10000.0
EPS = 1e-6
SB_SCALE = SB_HEAD_DIM ** -0.5
RET_KSCALE = RET_HEAD_DIM ** -0.5

ADAM_LR = 0.001
ADAM_B1 = 0.9
ADAM_B2 = 0.999
ADAM_EPS = 1e-08
ADAM_WD = 0.01
ADAM_STEP = 10

V7X_VMEM_BYTES = 64 * 2 ** 20
VMEM_LIMIT = V7X_VMEM_BYTES - 8 * 2 ** 20
LANES = 128

_NT = (((1,), (1,)), ((), ()))
_TN = (((0,), (0,)), ((), ()))


def _dot(a, b):
    return jnp.dot(a, b, preferred_element_type=F32)


def _dot_nt(a, b):
    return lax.dot_general(a, b, _NT, preferred_element_type=F32)


def _dot_tn(a, b):
    return lax.dot_general(a, b, _TN, preferred_element_type=F32)


def _mx(x):
    return x.astype(MXU_DTYPE)


def _sigmoid(x):
    return 1.0 / (1.0 + jnp.exp(-x))


def _params(sem=None):
    return pltpu.CompilerParams(dimension_semantics=sem, vmem_limit_bytes=VMEM_LIMIT)


def _row_tile(s):
    return min(512, s)


def _ln_proj(x, shift, scale1p, g, w_in_b):
    s = x.shape[0]
    ts = _row_tile(s)

    def body(x_ref, sh_ref, sc_ref, g_ref, w_ref, ret_ref, qkv_ref, sg_ref):
        xv = x_ref[...]
        rstd = lax.rsqrt(jnp.mean(xv * xv, axis=-1, keepdims=True) + EPS)
        h = (xv * rstd * g_ref[...]) * sc_ref[...] + sh_ref[...]
        hb = _mx(h)
        for n in range(4):
            ret_ref[:, n * 512:(n + 1) * 512] = _dot(hb, w_ref[:, n * 512:(n + 1) * 512])
        qkv_ref[:, 0:512] = _mx(_dot(hb, w_ref[:, 2048:2560]) * SB_SCALE)
        qkv_ref[:, 512:1024] = _mx(_dot(hb, w_ref[:, 2560:3072]))
        qkv_ref[:, 1024:1536] = _mx(_dot(hb, w_ref[:, 3072:3584]))
        sg_ref[...] = _dot(hb, w_ref[:, 3584:4096])

    vec = pl.BlockSpec((1, D_MODEL), lambda i: (0, 0))
    return pl.pallas_call(
        body, name="ln_proj", grid=(s // ts,),
        in_specs=[pl.BlockSpec((ts, D_MODEL), lambda i: (i, 0)), vec, vec, vec,
                  pl.BlockSpec((D_MODEL, D_IN), lambda i: (0, 0))],
        out_specs=[pl.BlockSpec((ts, 2048), lambda i: (i, 0)),
                   pl.BlockSpec((ts, 1536), lambda i: (i, 0)),
                   pl.BlockSpec((ts, 512), lambda i: (i, 0))],
        out_shape=[jax.ShapeDtypeStruct((s, 2048), F32),
                   jax.ShapeDtypeStruct((s, 1536), MXU_DTYPE),
                   jax.ShapeDtypeStruct((s, 512), F32)],
        compiler_params=_params(("arbitrary",)),
    )(x, shift, scale1p, g, w_in_b)


def _out_proj(x, gate, y_r, y_s, w_out_b):
    s = x.shape[0]
    ts = _row_tile(s)

    def body(x_ref, gate_ref, yr_ref, ys_ref, w_ref, o_ref):
        t = _dot(yr_ref[...], w_ref[0:512, :]) + _dot(ys_ref[...], w_ref[512:1024, :])
        o_ref[...] = x_ref[...] + gate_ref[...] * t

    return pl.pallas_call(
        body, name="out_proj", grid=(s // ts,),
        in_specs=[pl.BlockSpec((ts, D_MODEL), lambda i: (i, 0)),
                  pl.BlockSpec((1, D_MODEL), lambda i: (0, 0)),
                  pl.BlockSpec((ts, 512), lambda i: (i, 0)),
                  pl.BlockSpec((ts, 512), lambda i: (i, 0)),
                  pl.BlockSpec((D_MODEL, D_MODEL), lambda i: (0, 0))],
        out_specs=pl.BlockSpec((ts, D_MODEL), lambda i: (i, 0)),
        out_shape=jax.ShapeDtypeStruct((s, D_MODEL), F32),
        compiler_params=_params(("arbitrary",)),
    )(x, gate, y_r, y_s, w_out_b)


def _final_loss(x, fg, target):
    s = x.shape[0]
    ts = _row_tile(s)

    def body(x_ref, fg_ref, t_ref, dx_ref, loss_ref, dfg_ref):
        i = pl.program_id(0)

        @pl.when(i == 0)
        def _():
            loss_ref[...] = jnp.zeros_like(loss_ref)
            dfg_ref[...] = jnp.zeros_like(dfg_ref)

        xv = x_ref[...]
        fgv = fg_ref[...]
        rstd = lax.rsqrt(jnp.mean(xv * xv, axis=-1, keepdims=True) + EPS)
        xn = xv * rstd
        err = xn * fgv - t_ref[...]
        tok = jnp.mean(err * err, axis=-1, keepdims=True)
        loss_ref[...] += 0.5 * jnp.sum(tok, axis=0, keepdims=True)
        dy = err * (1.0 / D_MODEL)
        dfg_ref[...] += jnp.sum(dy * xn, axis=0, keepdims=True)
        dxn = dy * fgv
        dx_ref[...] = rstd * (dxn - xn * jnp.mean(dxn * xn, axis=-1, keepdims=True))

    return pl.pallas_call(
        body, name="final_loss", grid=(s // ts,),
        in_specs=[pl.BlockSpec((ts, D_MODEL), lambda i: (i, 0)),
                  pl.BlockSpec((1, D_MODEL), lambda i: (0, 0)),
                  pl.BlockSpec((ts, D_MODEL), lambda i: (i, 0))],
        out_specs=[pl.BlockSpec((ts, D_MODEL), lambda i: (i, 0)),
                   pl.BlockSpec((1, 1), lambda i: (0, 0)),
                   pl.BlockSpec((1, D_MODEL), lambda i: (0, 0))],
        out_shape=[jax.ShapeDtypeStruct((s, D_MODEL), F32),
                   jax.ShapeDtypeStruct((1, 1), F32),
                   jax.ShapeDtypeStruct((1, D_MODEL), F32)],
        compiler_params=_params(("arbitrary",)),
    )(x, fg, target)


def _out_proj_bwd(dx_out, gate, y_r, y_s, w_out_b):
    s = dx_out.shape[0]
    ts = _row_tile(s)

    def body(dx_ref, gate_ref, yr_ref, ys_ref, w_ref, dy_ref, dw_ref, dgate_ref):
        i = pl.program_id(0)

        @pl.when(i == 0)
        def _():
            dw_ref[...] = jnp.zeros_like(dw_ref)
            dgate_ref[...] = jnp.zeros_like(dgate_ref)

        dxv = dx_ref[...]
        dt = _mx(dxv * gate_ref[...])
        yr = yr_ref[...]
        ys = ys_ref[...]
        dy_ref[:, 0:512] = _dot_nt(dt, w_ref[0:512, :])
        dy_ref[:, 512:1024] = _dot_nt(dt, w_ref[512:1024, :])
        dw_ref[0:512, :] += _dot_tn(yr, dt)
        dw_ref[512:1024, :] += _dot_tn(ys, dt)
        t = _dot(yr, w_ref[0:512, :]) + _dot(ys, w_ref[512:1024, :])
        dgate_ref[...] += jnp.sum(dxv * t, axis=0, keepdims=True)

    return pl.pallas_call(
        body, name="out_proj_bwd", grid=(s // ts,),
        in_specs=[pl.BlockSpec((ts, D_MODEL), lambda i: (i, 0)),
                  pl.BlockSpec((1, D_MODEL), lambda i: (0, 0)),
                  pl.BlockSpec((ts, 512), lambda i: (i, 0)),
                  pl.BlockSpec((ts, 512), lambda i: (i, 0)),
                  pl.BlockSpec((D_MODEL, D_MODEL), lambda i: (0, 0))],
        out_specs=[pl.BlockSpec((ts, D_MODEL), lambda i: (i, 0)),
                   pl.BlockSpec((D_MODEL, D_MODEL), lambda i: (0, 0)),
                   pl.BlockSpec((1, D_MODEL), lambda i: (0, 0))],
        out_shape=[jax.ShapeDtypeStruct((s, D_MODEL), F32),
                   jax.ShapeDtypeStruct((D_MODEL, D_MODEL), F32),
                   jax.ShapeDtypeStruct((1, D_MODEL), F32)],
        compiler_params=_params(("arbitrary",)),
    )(dx_out, gate, y_r, y_s, w_out_b)


def _in_proj_bwd_x(x, dx_out, dproj, shift, scale1p, g, w_in_b):
    s = x.shape[0]
    ts = min(256, s)

    def body(x_ref, dxo_ref, dp_ref, sc_ref, g_ref, w_ref, dx_ref, dsh_ref, dsc_ref, dg_ref):
        i = pl.program_id(0)

        @pl.when(i == 0)
        def _():
            dsh_ref[...] = jnp.zeros_like(dsh_ref)
            dsc_ref[...] = jnp.zeros_like(dsc_ref)
            dg_ref[...] = jnp.zeros_like(dg_ref)

        dh = _dot_nt(dp_ref[...], w_ref[...])
        xv = x_ref[...]
        gv = g_ref[...]
        scv = sc_ref[...]
        rstd = lax.rsqrt(jnp.mean(xv * xv, axis=-1, keepdims=True) + EPS)
        xn = xv * rstd
        dsh_ref[...] += jnp.sum(dh, axis=0, keepdims=True)
        dsc_ref[...] += jnp.sum(dh * (xn * gv), axis=0, keepdims=True)
        dhs = dh * scv
        dg_ref[...] += jnp.sum(dhs * xn, axis=0, keepdims=True)
        dxn = dhs * gv
        dx_ref[...] = rstd * (dxn - xn * jnp.mean(dxn * xn, axis=-1, keepdims=True)) + dxo_ref[...]

    del shift
    vec = pl.BlockSpec((1, D_MODEL), lambda i: (0, 0))
    return pl.pallas_call(
        body, name="in_proj_bwd_x", grid=(s // ts,),
        in_specs=[pl.BlockSpec((ts, D_MODEL), lambda i: (i, 0)),
                  pl.BlockSpec((ts, D_MODEL), lambda i: (i, 0)),
                  pl.BlockSpec((ts, D_IN), lambda i: (i, 0)),
                  vec, vec,
                  pl.BlockSpec((D_MODEL, D_IN), lambda i: (0, 0))],
        out_specs=[pl.BlockSpec((ts, D_MODEL), lambda i: (i, 0)), vec, vec, vec],
        out_shape=[jax.ShapeDtypeStruct((s, D_MODEL), F32),
                   jax.ShapeDtypeStruct((1, D_MODEL), F32),
                   jax.ShapeDtypeStruct((1, D_MODEL), F32),
                   jax.ShapeDtypeStruct((1, D_MODEL), F32)],
        compiler_params=_params(("arbitrary",)),
    )(x, dx_out, dproj, scale1p, g, w_in_b)


def _in_proj_bwd_w(x, dproj, shift, scale1p, g):
    s = x.shape[0]
    ts = _row_tile(s)
    nb = D_IN // N_DEV

    def body(x_ref, dp_ref, sh_ref, sc_ref, g_ref, dw_ref):
        i = pl.program_id(1)

        @pl.when(i == 0)
        def _():
            dw_ref[...] = jnp.zeros_like(dw_ref)

        xv = x_ref[...]
        rstd = lax.rsqrt(jnp.mean(xv * xv, axis=-1, keepdims=True) + EPS)
        h = (xv * rstd * g_ref[...]) * sc_ref[...] + sh_ref[...]
        dw_ref[...] += _dot_tn(_mx(h), dp_ref[...])

    vec = pl.BlockSpec((1, D_MODEL), lambda j, i: (0, 0))
    return pl.pallas_call(
        body, name="in_proj_bwd_w", grid=(N_DEV, s // ts),
        in_specs=[pl.BlockSpec((ts, D_MODEL), lambda j, i: (i, 0)),
                  pl.BlockSpec((ts, nb), lambda j, i: (i, j)),
                  vec, vec, vec],
        out_specs=pl.BlockSpec((None, D_MODEL, nb), lambda j, i: (j, 0, 0)),
        out_shape=jax.ShapeDtypeStruct((N_DEV, D_MODEL, nb), F32),
        compiler_params=_params(("arbitrary", "arbitrary")),
    )(x, dproj, shift, scale1p, g)


RET_TILE = 256


def _ret_tables(s):
    t = min(RET_TILE, s)
    half = RET_HEAD_DIM // 2
    pos = jnp.arange(s, dtype=F32)
    inv = ROPE_BASE ** (-jnp.arange(half, dtype=F32) / half)
    ang = pos[:, None] * inv[None, :]
    cos, sin = jnp.cos(ang), jnp.sin(ang)
    cos2 = jnp.concatenate([cos, cos], axis=1)
    sin2 = jnp.concatenate([-sin, sin], axis=1)
    lg = jnp.log1p(-(2.0 ** (-5.0 - jnp.arange(RET_HEADS, dtype=F32))))[:, None, None]
    n = jnp.arange(t)
    dist = (n[:, None] - n[None, :]).astype(F32)[None]
    cn = (n // CHUNK)[:, None]
    cm = (n // CHUNK)[None, :]
    mask = jnp.where((cn == cm)[None], jnp.exp(jnp.abs(dist) * lg),
                     jnp.where((cm < cn)[None], jnp.exp(dist * lg), 0.0))
    nf = n.astype(F32)[None, :, None]
    dq = jnp.broadcast_to(jnp.exp((nf + 1.0) * lg), (RET_HEADS, t, LANES))
    dk = jnp.broadcast_to(jnp.exp((t - 1.0 - nf) * lg), (RET_HEADS, t, LANES))
    gt = jnp.broadcast_to(jnp.exp(float(t) * lg), (RET_HEADS, 1, LANES))
    return cos2, sin2, mask, dq, dk, gt


def _roll_half(x):
    return pltpu.roll(x, RET_HEAD_DIM // 2, 1)


def _ret_head_fwd(q, k, v, cos, sin, m, dq, dk, s0):
    qr = q * cos + _roll_half(q) * sin
    kr = (k * cos + _roll_half(k) * sin) * RET_KSCALE
    qb, kb, vb = _mx(qr), _mx(kr), _mx(v)
    p = _dot_nt(qb, kb) * m
    pb = _mx(p)
    o = _dot(pb, vb) + _dot(qb, _mx(s0)) * dq
    kdb = _mx(kr * dk)
    mu = jnp.mean(o, axis=-1, keepdims=True)
    oc = o - mu
    rstd = lax.rsqrt(jnp.mean(oc * oc, axis=-1, keepdims=True) + EPS)
    return qb, kb, vb, pb, kdb, oc * rstd, rstd


def _retention_fwd(ret, tables):
    cos2, sin2, mask, dq, dk, gt = tables
    s = ret.shape[0]
    t = mask.shape[1]
    nt = s // t
    hd = RET_HEAD_DIM

    def body(ret_ref, cos_ref, sin_ref, m_ref, dq_ref, dk_ref, gt_ref, y_ref, st_ref, s_scr):
        i = pl.program_id(0)

        @pl.when(i == 0)
        def _():
            s_scr[...] = jnp.zeros_like(s_scr)

        cos = cos_ref[...]
        sin = sin_ref[...]
        for h in range(RET_HEADS):
            q = ret_ref[:, h * hd:(h + 1) * hd]
            k = ret_ref[:, 512 + h * hd:512 + (h + 1) * hd]
            v = ret_ref[:, 1024 + h * hd:1024 + (h + 1) * hd]
            g = ret_ref[:, 1536 + h * hd:1536 + (h + 1) * hd]
            s0 = s_scr[h]
            st_ref[h] = s0
            _, _, vb, _, kdb, gn, _ = _ret_head_fwd(q, k, v, cos, sin, m_ref[h], dq_ref[h], dk_ref[h], s0)
            y_ref[:, h * hd:(h + 1) * hd] = (gn * (g * _sigmoid(g))).astype(y_ref.dtype)
            s_scr[h] = s0 * gt_ref[h] + _dot_tn(kdb, vb)

    full3 = lambda a: pl.BlockSpec(a.shape, lambda i: (0, 0, 0))
    return pl.pallas_call(
        body, name="retention_fwd", grid=(nt,),
        in_specs=[pl.BlockSpec((t, 2048), lambda i: (i, 0)),
                  pl.BlockSpec((t, LANES), lambda i: (i, 0)),
                  pl.BlockSpec((t, LANES), lambda i: (i, 0)),
                  full3(mask), full3(dq), full3(dk), full3(gt)],
        out_specs=[pl.BlockSpec((t, 512), lambda i: (i, 0)),
                   pl.BlockSpec((None, RET_HEADS, hd, hd), lambda i: (i, 0, 0, 0))],
        out_shape=[jax.ShapeDtypeStruct((s, 512), MXU_DTYPE),
                   jax.ShapeDtypeStruct((nt, RET_HEADS, hd, hd), F32)],
        scratch_shapes=[pltpu.VMEM((RET_HEADS, hd, hd), F32)],
        compiler_params=_params(("arbitrary",)),
    )(ret, cos2, sin2, mask, dq, dk, gt)


def _retention_bwd(ret, states, dy, tables):
    cos2, sin2, mask, dq, dk, gt = tables
    s = ret.shape[0]
    t = mask.shape[1]
    nt = s // t
    hd = RET_HEAD_DIM

    def body(ret_ref, st_ref, dy_ref, cos_ref, sin_ref, m_ref, dq_ref, dk_ref, gt_ref, d_ref, ds_scr):
        i = pl.program_id(0)

        @pl.when(i == 0)
        def _():
            ds_scr[...] = jnp.zeros_like(ds_scr)

        cos = cos_ref[...]
        sin = sin_ref[...]
        for h in range(RET_HEADS):
            q = ret_ref[:, h * hd:(h + 1) * hd]
            k = ret_ref[:, 512 + h * hd:512 + (h + 1) * hd]
            v = ret_ref[:, 1024 + h * hd:1024 + (h + 1) * hd]
            g = ret_ref[:, 1536 + h * hd:1536 + (h + 1) * hd]
            dyv = dy_ref[:, h * hd:(h + 1) * hd]
            m = m_ref[h]
            dqv = dq_ref[h]
            dkv = dk_ref[h]
            s0 = st_ref[h]
            ds = ds_scr[h]
            qb, kb, vb, pb, kdb, gn, rstd = _ret_head_fwd(q, k, v, cos, sin, m, dqv, dkv, s0)
            sg = _sigmoid(g)
            d_ref[:, 1536 + h * hd:1536 + (h + 1) * hd] = (
                dyv * gn * (sg * (1.0 + g * (1.0 - sg)))).astype(d_ref.dtype)
            dgn = dyv * (g * sg)
            do = rstd * (dgn - jnp.mean(dgn, axis=-1, keepdims=True)
                         - gn * jnp.mean(dgn * gn, axis=-1, keepdims=True))
            dob = _mx(do)
            dodb = _mx(do * dqv)
            dsb = _mx(ds)
            d_ref[:, 1024 + h * hd:1024 + (h + 1) * hd] = (
                _dot_tn(pb, dob) + _dot(kdb, dsb)).astype(d_ref.dtype)
            dpb = _mx(_dot_nt(dob, vb) * m)
            dqr = _dot(dpb, kb) + _dot_nt(dodb, _mx(s0))
            dkr = (_dot_tn(dpb, qb) + _dot_nt(vb, dsb) * dkv) * RET_KSCALE
            d_ref[:, h * hd:(h + 1) * hd] = (dqr * cos + _roll_half(dqr * sin)).astype(d_ref.dtype)
            d_ref[:, 512 + h * hd:512 + (h + 1) * hd] = (
                dkr * cos + _roll_half(dkr * sin)).astype(d_ref.dtype)
            ds_scr[h] = ds * gt_ref[h] + _dot_tn(qb, dodb)

    full3 = lambda a: pl.BlockSpec(a.shape, lambda i: (0, 0, 0))
    rev = lambda i: (nt - 1 - i, 0)
    return pl.pallas_call(
        body, name="retention_bwd", grid=(nt,),
        in_specs=[pl.BlockSpec((t, 2048), rev),
                  pl.BlockSpec((None, RET_HEADS, hd, hd), lambda i: (nt - 1 - i, 0, 0, 0)),
                  pl.BlockSpec((t, 512), rev),
                  pl.BlockSpec((t, LANES), rev),
                  pl.BlockSpec((t, LANES), rev),
                  full3(mask), full3(dq), full3(dk), full3(gt)],
        out_specs=pl.BlockSpec((t, 2048), rev),
        out_shape=jax.ShapeDtypeStruct((s, 2048), MXU_DTYPE),
        scratch_shapes=[pltpu.VMEM((RET_HEADS, hd, hd), F32)],
        compiler_params=_params(("arbitrary",)),
    )(ret, states, dy, cos2, sin2, mask, dq, dk, gt)


SB_BLOCK = 256


SB_SKIP = 104.0


def _split_dot(x, u):
    hi = lax.bitcast_convert_type(lax.bitcast_convert_type(x, jnp.uint32) & jnp.uint32(0xFFFF0000), F32)
    return _dot(_mx(hi), u) + _dot(_mx(x - hi), u)


def _sb_logits(q, k, causal):
    z = _dot_nt(q, k)
    l1p = jnp.log(1.0 + jnp.exp(-jnp.abs(z)))
    lk = -(jnp.maximum(z, 0.0) + l1p)
    if causal is not None:
        lk = jnp.where(causal, lk, 0.0)
    return jnp.minimum(z, 0.0) - l1p, lk


def _sb_weights(lb, lk, r, u_gt, causal):
    a = jnp.exp(lb + _split_dot(lk, u_gt) + r)
    return a if causal is None else jnp.where(causal, a, 0.0)


def _rowsum(x):
    return jnp.sum(x, axis=1, keepdims=True)


def _sb_fwd(qkv, sg):
    s = qkv.shape[0]
    blk = min(SB_BLOCK, s)
    nq = s // blk
    hd = SB_HEAD_DIM

    def body(q_ref, k_ref, v_ref, g_ref, y_ref, o_ref):
        i = pl.program_id(1)
        row = lax.broadcasted_iota(jnp.int32, (blk, blk), 0)
        col = lax.broadcasted_iota(jnp.int32, (blk, blk), 1)
        causal = row > col
        u_gt = causal.astype(MXU_DTYPE)
        heads = [slice(hh * hd, (hh + 1) * hd) for hh in range(2)]
        qs = [q_ref[:, ls] for ls in heads]

        def block(hh, j, r, causal_mask):
            start = pl.multiple_of(j * blk, blk)
            lb, lk = _sb_logits(qs[hh], k_ref[pl.ds(start, blk), heads[hh]], causal_mask)
            a = _sb_weights(lb, lk, r, u_gt, causal_mask)
            return _dot(_mx(a), v_ref[pl.ds(start, blk), heads[hh]]), r + _rowsum(lk)

        zero = jnp.zeros((blk, 1), F32)
        acc0, r0 = block(0, i, zero, causal)
        acc1, r1 = block(1, i, zero, causal)

        def more(c):
            n, _, r0, _, r1 = c
            return jnp.logical_and(n < i, jnp.max(jnp.maximum(r0, r1)) > -SB_SKIP)

        def step(c):
            n, acc0, r0, acc1, r1 = c
            pv0, r0 = block(0, i - 1 - n, r0, None)
            pv1, r1 = block(1, i - 1 - n, r1, None)
            return n + 1, acc0 + pv0, r0, acc1 + pv1, r1

        _, acc0, _, acc1, _ = lax.while_loop(more, step, (jnp.int32(0), acc0, r0, acc1, r1))
        for ls, acc in zip(heads, (acc0, acc1)):
            g = g_ref[:, ls]
            o_ref[:, ls] = acc
            y_ref[:, ls] = (acc * (g * _sigmoid(g))).astype(y_ref.dtype)

    qblk = pl.BlockSpec((blk, LANES), lambda p, i: (i, p))
    return pl.pallas_call(
        body, name="stickbreak_fwd", grid=(SB_HEADS // 2, nq),
        in_specs=[qblk,
                  pl.BlockSpec((s, LANES), lambda p, i: (0, 4 + p)),
                  pl.BlockSpec((s, LANES), lambda p, i: (0, 8 + p)),
                  qblk],
        out_specs=[qblk, qblk],
        out_shape=[jax.ShapeDtypeStruct((s, 512), MXU_DTYPE),
                   jax.ShapeDtypeStruct((s, 512), F32)],
        compiler_params=_params(("arbitrary", "arbitrary")),
    )(qkv, qkv, qkv, sg)


def _sb_bwd(qkv, sg, o, dy):
    s = qkv.shape[0]
    blk = min(SB_BLOCK, s)
    nq = s // blk
    hd = SB_HEAD_DIM
    assert nq <= LANES

    def body(q_ref, k_ref, v_ref, g_ref, o_ref, dy_ref, dq_ref, dk_ref, dv_ref, dg_ref, dk_scr, dv_scr):
        i = pl.program_id(1)

        @pl.when(i == 0)
        def _():
            dk_scr[...] = jnp.zeros_like(dk_scr)
            dv_scr[...] = jnp.zeros_like(dv_scr)

        row = lax.broadcasted_iota(jnp.int32, (blk, blk), 0)
        col = lax.broadcasted_iota(jnp.int32, (blk, blk), 1)
        lane = lax.broadcasted_iota(jnp.int32, (blk, LANES), 1)
        causal = row > col
        u_gt = causal.astype(MXU_DTYPE)
        u_lt = (row < col).astype(MXU_DTYPE)
        heads = [slice(hh * hd, (hh + 1) * hd) for hh in range(2)]
        qs = [q_ref[:, ls] for ls in heads]
        dobs = []
        for ls in heads:
            g = g_ref[:, ls]
            dyv = dy_ref[:, ls]
            sgm = _sigmoid(g)
            dg_ref[:, ls] = (dyv * o_ref[:, ls] * (sgm * (1.0 + g * (1.0 - sgm)))).astype(dg_ref.dtype)
            dobs.append(_mx(dyv * (g * sgm)))

        def scan_block(hh, j, r, rmat, causal_mask):
            start = pl.multiple_of(j * blk, blk)
            _, lk = _sb_logits(qs[hh], k_ref[pl.ds(start, blk), heads[hh]], causal_mask)
            return r + _rowsum(lk), jnp.where(lane == j, r, rmat)

        zero = jnp.zeros((blk, 1), F32)
        zmat = jnp.zeros((blk, LANES), F32)
        r0, rmat0 = scan_block(0, i, zero, zmat, causal)
        r1, rmat1 = scan_block(1, i, zero, zmat, causal)

        def more(c):
            n, r0, _, r1, _ = c
            return jnp.logical_and(n < i, jnp.max(jnp.maximum(r0, r1)) > -SB_SKIP)

        def scan_step(c):
            n, r0, rmat0, r1, rmat1 = c
            r0, rmat0 = scan_block(0, i - 1 - n, r0, rmat0, None)
            r1, rmat1 = scan_block(1, i - 1 - n, r1, rmat1, None)
            return n + 1, r0, rmat0, r1, rmat1

        n, _, rmat0, _, rmat1 = lax.while_loop(more, scan_step, (jnp.int32(0), r0, rmat0, r1, rmat1))
        rmats = (rmat0, rmat1)

        def block(hh, j, pg, causal_mask):
            ls = heads[hh]
            start = pl.multiple_of(j * blk, blk)
            k = k_ref[pl.ds(start, blk), ls]
            lb, lk = _sb_logits(qs[hh], k, causal_mask)
            r = _rowsum(jnp.where(lane == j, rmats[hh], 0.0))
            a = _sb_weights(lb, lk, r, u_gt, causal_mask)
            gm = _dot_nt(dobs[hh], v_ref[pl.ds(start, blk), ls]) * a
            dz = gm - (gm + (pg + _split_dot(gm, u_lt))) * jnp.exp(lb)
            if causal_mask is not None:
                dz = jnp.where(causal_mask, dz, 0.0)
            dzb = _mx(dz)
            dk_scr[hh, pl.ds(start, blk), :] += _dot_tn(dzb, qs[hh])
            dv_scr[hh, pl.ds(start, blk), :] += _dot_tn(_mx(a), dobs[hh])
            return _dot(dzb, k), pg + _rowsum(gm)

        def step(t, c):
            acc0, pg0, acc1, pg1 = c
            dq0, pg0 = block(0, i - n + t, pg0, None)
            dq1, pg1 = block(1, i - n + t, pg1, None)
            return acc0 + dq0, pg0, acc1 + dq1, pg1

        zacc = jnp.zeros((blk, hd), F32)
        acc0, pg0, acc1, pg1 = lax.fori_loop(0, n, step, (zacc, zero, zacc, zero))
        dq0, _ = block(0, i, pg0, causal)
        dq1, _ = block(1, i, pg1, causal)
        dq_ref[:, heads[0]] = ((acc0 + dq0) * SB_SCALE).astype(dq_ref.dtype)
        dq_ref[:, heads[1]] = ((acc1 + dq1) * SB_SCALE).astype(dq_ref.dtype)

        @pl.when(i == nq - 1)
        def _():
            for hh in range(2):
                ls = slice(hh * hd, (hh + 1) * hd)
                dk_ref[:, ls] = dk_scr[hh].astype(dk_ref.dtype)
                dv_ref[:, ls] = dv_scr[hh].astype(dv_ref.dtype)

    qblk = lambda c0: pl.BlockSpec((blk, LANES), lambda p, i: (i, c0 + p))
    full = lambda c0: pl.BlockSpec((s, LANES), lambda p, i: (0, c0 + p))
    half = jax.ShapeDtypeStruct((s, 512), MXU_DTYPE)
    return pl.pallas_call(
        body, name="stickbreak_bwd", grid=(SB_HEADS // 2, nq),
        in_specs=[qblk(0), full(4), full(8), qblk(0), qblk(0), qblk(4)],
        out_specs=[qblk(0), full(0), full(0), qblk(0)],
        out_shape=[half, half, half, half],
        scratch_shapes=[pltpu.VMEM((2, s, hd), F32), pltpu.VMEM((2, s, hd), F32)],
        compiler_params=_params(("arbitrary", "arbitrary")),
    )(qkv, qkv, qkv, sg, o, dy)


def _layer_fwd(x, mod, norm_g, w_in_b, w_out_b, tables):
    shift, scale1p, gate = mod[0:1], 1.0 + mod[1:2], mod[2:3]
    ret, qkv, sg = _ln_proj(x, shift, scale1p, norm_g, w_in_b)
    y_r, states = _retention_fwd(ret, tables)
    y_s, o_s = _sb_fwd(qkv, sg)
    x_next = _out_proj(x, gate, y_r, y_s, w_out_b)
    saved = (x, shift, scale1p, gate, ret, qkv, sg, y_r, states, y_s, o_s)
    return x_next, saved


def _layer_bwd(dx_out, saved, norm_g, w_in_b, w_out_b, tables):
    x, shift, scale1p, gate, ret, qkv, sg, y_r, states, y_s, o_s = saved
    dy, dw_out, dgate = _out_proj_bwd(dx_out, gate, y_r, y_s, w_out_b)
    d_ret = _retention_bwd(ret, states, dy, tables)
    d_q, d_k, d_v, d_g = _sb_bwd(qkv, sg, o_s, dy)
    dproj = jnp.concatenate([d_ret, d_q, d_k, d_v, d_g], axis=1)
    dx, dshift, dscale, dnorm_g = _in_proj_bwd_x(x, dx_out, dproj, shift, scale1p, norm_g, w_in_b)
    dw_in = _in_proj_bwd_w(x, dproj, shift, scale1p, norm_g)
    dmod = jnp.concatenate([dshift, dscale, dgate], axis=1)
    return dx, dw_in, dw_out, dmod, dnorm_g


MESH_IDS = pl.DeviceIdType.MESH
N_PEERS = N_DEV - 1
HBM_SPEC = pl.BlockSpec(memory_space=pl.ANY)


def _my_place():
    return lax.axis_index("x"), lax.axis_index("y"), lax.axis_index("c")


def _linear(px, py, pc):
    return 4 * px + 2 * py + pc


def _all_gather(blocks):
    n_arr = len(blocks)

    def body(*refs):
        x_refs = refs[:n_arr]
        out_refs = refs[n_arr:2 * n_arr]
        send_sems, recv_sems, local_sems = refs[2 * n_arr:]
        x, y, c = _my_place()
        me, sibling = (x, y, c), (x, y, 1 - c)
        chips = [(1 - x, y), (x, 1 - y), (1 - x, 1 - y)]

        def rows(a, place):
            m = x_refs[a].shape[0]
            return out_refs[a].at[pl.ds(_linear(*place) * m, m), :]

        def copy(a, k, block, to, src=None):
            return pltpu.make_async_remote_copy(
                src_ref=rows(a, block) if src is None else src, dst_ref=rows(a, block),
                send_sem=send_sems.at[a * N_PEERS + k], recv_sem=recv_sems.at[a * N_PEERS + k],
                device_id=to, device_id_type=MESH_IDS)

        mine = [pltpu.make_async_copy(x_refs[a], rows(a, me), local_sems.at[a]) for a in range(n_arr)]
        for cp in mine:
            cp.start()
        first = []
        for a in range(n_arr):
            first.append(copy(a, 0, me, sibling, src=x_refs[a]))
            first += [copy(a, 1 + j, me, (*chip, c), src=x_refs[a]) for j, chip in enumerate(chips)]
        for cp in first:
            cp.start()
        passed = []
        for j, chip in enumerate(chips):
            for a in range(n_arr):
                copy(a, 1 + j, (*chip, c), me).wait_recv()
                fwd = copy(a, 4 + j, (*chip, c), sibling)
                fwd.start()
                passed.append(fwd)
        for a in range(n_arr):
            copy(a, 0, sibling, me).wait_recv()
            for j, chip in enumerate(chips):
                copy(a, 4 + j, (*chip, 1 - c), me).wait_recv()
        for cp in first + passed:
            cp.wait_send()
        for cp in mine:
            cp.wait()

    return pl.pallas_call(
        body, name="all_gather",
        out_shape=[jax.ShapeDtypeStruct((N_DEV * b.shape[0], b.shape[1]), b.dtype) for b in blocks],
        in_specs=[HBM_SPEC] * n_arr, out_specs=[HBM_SPEC] * n_arr,
        scratch_shapes=[pltpu.SemaphoreType.DMA((n_arr * N_PEERS,)),
                        pltpu.SemaphoreType.DMA((n_arr * N_PEERS,)),
                        pltpu.SemaphoreType.DMA((n_arr,))],
    )(*blocks)


def _exchange(parts):
    n_arr = len(parts)

    def body(*refs):
        p_refs = refs[:n_arr]
        o_refs = refs[n_arr:2 * n_arr]
        send_sems, recv_sems, local_sems = refs[2 * n_arr:]
        x, y, c = _my_place()
        me = _linear(x, y, c)
        mine = [pltpu.make_async_copy(p_refs[a].at[me], o_refs[a].at[me], local_sems.at[a]) for a in range(n_arr)]
        for cp in mine:
            cp.start()
        copies = []
        for r in range(1, N_DEV):
            peer = (1 - x if r & 4 else x, 1 - y if r & 2 else y, 1 - c if r & 1 else c)
            for a in range(n_arr):
                cp = pltpu.make_async_remote_copy(
                    src_ref=p_refs[a].at[_linear(*peer)], dst_ref=o_refs[a].at[me],
                    send_sem=send_sems.at[a * N_PEERS + r - 1], recv_sem=recv_sems.at[a * N_PEERS + r - 1],
                    device_id=peer, device_id_type=MESH_IDS)
                cp.start()
                copies.append(cp)
        for cp in copies:
            cp.wait_recv()
        for cp in copies:
            cp.wait_send()
        for cp in mine:
            cp.wait()

    return pl.pallas_call(
        body, name="exchange",
        out_shape=[jax.ShapeDtypeStruct(p.shape, p.dtype) for p in parts],
        in_specs=[HBM_SPEC] * n_arr, out_specs=[HBM_SPEC] * n_arr,
        scratch_shapes=[pltpu.SemaphoreType.DMA((n_arr * N_PEERS,)),
                        pltpu.SemaphoreType.DMA((n_arr * N_PEERS,)),
                        pltpu.SemaphoreType.DMA((n_arr,))],
    )(*parts)


def _ada_fwd(c_all, w_ada, b_cols):
    cols = w_ada.shape[2]

    def body(c_ref, w_ref, b_ref, ca_ref, mod_ref):
        cv = c_ref[...]
        ca = cv * _sigmoid(cv)
        ca_ref[...] = ca
        cb = _mx(jnp.concatenate([ca, ca], axis=0))
        for l in range(DEPTH):
            mod_ref[l * N_DEV:(l + 1) * N_DEV, :] = _dot(cb, _mx(w_ref[l]))[0:N_DEV] + b_ref[l]

    return pl.pallas_call(
        body, name="ada_fwd",
        out_shape=[jax.ShapeDtypeStruct((N_DEV, D_MODEL), F32),
                   jax.ShapeDtypeStruct((DEPTH * N_DEV, cols), F32)],
        compiler_params=_params(),
    )(c_all, w_ada, b_cols)


def _ada_bwd(c_act_t, dmod_cols):
    cols = dmod_cols.shape[2]

    def body(ca_ref, dm_ref, o_ref):
        ca = _mx(ca_ref[...]).astype(F32)
        for l in range(DEPTH):
            o_ref[l] = jnp.dot(ca, _mx(dm_ref[l]).astype(F32),
                               precision=lax.Precision.HIGHEST, preferred_element_type=F32)

    return pl.pallas_call(
        body, name="ada_bwd",
        out_shape=jax.ShapeDtypeStruct((DEPTH, D_MODEL, cols), F32),
        compiler_params=_params(),
    )(c_act_t, dmod_cols)


def _sum_adamw(parts, w, m, v):
    n_slab, rows, cols = parts.shape
    tr = min(256, rows)

    def body(p_ref, w_ref, m_ref, v_ref, g_ref, d_ref, mo_ref, vo_ref):
        g = p_ref[0]
        for sl in range(1, n_slab):
            g = g + p_ref[sl]
        m2 = ADAM_B1 * m_ref[...] + (1.0 - ADAM_B1) * g
        v2 = ADAM_B2 * v_ref[...] + (1.0 - ADAM_B2) * (g * g)
        m_hat = m2 / (1.0 - ADAM_B1 ** ADAM_STEP)
        v_hat = v2 / (1.0 - ADAM_B2 ** ADAM_STEP)
        g_ref[...] = g
        d_ref[...] = -ADAM_LR * (m_hat / (jnp.sqrt(v_hat) + ADAM_EPS) + ADAM_WD * w_ref[...])
        mo_ref[...] = m2
        vo_ref[...] = v2

    blk = pl.BlockSpec((tr, cols), lambda i: (i, 0))
    shp = jax.ShapeDtypeStruct((rows, cols), F32)
    return pl.pallas_call(
        body, name="sum_adamw", grid=(rows // tr,),
        in_specs=[pl.BlockSpec((n_slab, tr, cols), lambda i: (0, i, 0)), blk, blk, blk],
        out_specs=[blk, blk, blk, blk],
        out_shape=[shp, shp, shp, shp],
        compiler_params=_params(("arbitrary",)),
    )(parts, w, m, v)


SMALL_ROWS = 16


def kernel(x, c, norm_g, w_ada, b_ada, w_in, w_out, final_g, loss_target, m_norm_g, m_w_ada, m_b_ada, m_w_in, m_w_out, m_final_g, v_norm_g, v_w_ada, v_b_ada, v_w_in, v_w_out, v_final_g):
    me = _linear(*_my_place())
    in_cols = w_in.shape[2]
    out_rows = w_out.shape[1]
    ada_cols = w_ada.shape[2]

    g_in, g_out, g_c = _all_gather([
        _mx(w_in).reshape(DEPTH * D_MODEL, in_cols),
        _mx(w_out).reshape(DEPTH * out_rows, D_MODEL),
        jnp.broadcast_to(c, (8, D_MODEL))])
    w_in_b = g_in.reshape(N_DEV, DEPTH, D_MODEL, in_cols).transpose(1, 2, 0, 3).reshape(DEPTH, D_MODEL, D_IN)
    w_out_b = g_out.reshape(N_DEV, DEPTH, out_rows, D_MODEL).transpose(1, 0, 2, 3).reshape(DEPTH, D_MODEL, D_MODEL)
    c_all = g_c.reshape(N_DEV, 8, D_MODEL)[:, 0]

    b_cols = lax.dynamic_slice_in_dim(b_ada, me * ada_cols, ada_cols, axis=1)[:, None, :]
    c_act, mod_cols = _ada_fwd(c_all, w_ada, b_cols)
    (g_mod,) = _all_gather([mod_cols])
    g_mod = g_mod.reshape(N_DEV, DEPTH, N_DEV, ada_cols)
    mod = lax.dynamic_index_in_dim(g_mod, me, axis=2, keepdims=False)
    mod = mod.transpose(1, 0, 2).reshape(DEPTH, 3, D_MODEL)

    tables = _ret_tables(x.shape[1])
    h = x[0]
    saved = []
    for l in range(DEPTH):
        h, sv = _layer_fwd(h, mod[l], norm_g[l:l + 1], w_in_b[l], w_out_b[l], tables)
        saved.append(sv)
    dx, loss_part, dfg = _final_loss(h, final_g[None], loss_target[0])
    dw_in, dw_out, small = [None] * DEPTH, [None] * DEPTH, [None] * DEPTH
    for l in reversed(range(DEPTH)):
        dx, dw_in[l], dwo, dmod, dng = _layer_bwd(dx, saved[l], norm_g[l:l + 1], w_in_b[l], w_out_b[l], tables)
        dw_out[l] = dwo.reshape(N_DEV, out_rows, D_MODEL)
        small[l] = (dmod.reshape(3, D_MODEL), dng)

    pad = jnp.zeros((SMALL_ROWS - 10, D_MODEL), F32)
    small_block = jnp.concatenate([small[0][0], small[1][0], small[0][1], small[1][1], dfg,
                                   jnp.broadcast_to(loss_part, (1, D_MODEL)), pad], axis=0)
    (g_small,) = _all_gather([small_block])
    g_small = g_small.reshape(N_DEV, SMALL_ROWS, D_MODEL)

    def small_pack(b, n, f, fill):
        return jnp.concatenate([b.reshape(6, D_MODEL), n, f[None],
                                jnp.full((SMALL_ROWS - 9, D_MODEL), fill, F32)], axis=0)

    s_g, s_d, s_m, s_v = _sum_adamw(g_small, small_pack(b_ada, norm_g, final_g, 0.0),
                                    small_pack(m_b_ada, m_norm_g, m_final_g, 0.0),
                                    small_pack(v_b_ada, v_norm_g, v_final_g, 1.0))
    loss = s_g[9, 0]

    def small_unpack(a):
        return a[0:6].reshape(DEPTH, 3 * D_MODEL), a[6:8], a[8]

    dmod_all = g_small[:, 0:6].reshape(N_DEV, DEPTH, 3 * D_MODEL).transpose(1, 0, 2)
    dmod_cols = lax.dynamic_slice_in_dim(dmod_all, me * ada_cols, ada_cols, axis=2)
    g_ada = _ada_bwd(c_act.T, dmod_cols).reshape(1, DEPTH * D_MODEL, ada_cols)
    ada = _sum_adamw(g_ada, *[a.reshape(DEPTH * D_MODEL, ada_cols) for a in (w_ada, m_w_ada, v_w_ada)])
    ada = [a.reshape(DEPTH, D_MODEL, ada_cols) for a in ada]

    recv = _exchange([dw_in[0], dw_in[1], dw_out[0], dw_out[1]])
    win = [_sum_adamw(recv[l], w_in[l], m_w_in[l], v_w_in[l]) for l in range(DEPTH)]
    wout = [_sum_adamw(recv[DEPTH + l], w_out[l], m_w_out[l], v_w_out[l]) for l in range(DEPTH)]
    win = [jnp.stack([win[0][k], win[1][k]]) for k in range(4)]
    wout = [jnp.stack([wout[0][k], wout[1][k]]) for k in range(4)]

    outs = [loss, dx[None]]
    for k in range(4):
        b, n, f = small_unpack((s_g, s_d, s_m, s_v)[k])
        outs += [n, ada[k], b, win[k], wout[k], f]
    return tuple(outs)
```

```python
import functools

import jax
import jax.numpy as jnp
from jax import lax
from jax.experimental import pallas as pl
from jax.experimental.pallas import tpu as pltpu

F32 = jnp.float32
MXU_DTYPE = jnp.bfloat16

D_MODEL = 1024
DEPTH = 2
N_DEV = 8
CHUNK = 64
D_RET = 512
D_SB = 512
RET_HEADS = 4
RET_HEAD_DIM = 128
SB_HEADS = 8
SB_HEAD_DIM = 64
D_IN = 4096
ROPE_BASE = 10000.0
EPS = 1e-6
SB_SCALE = SB_HEAD_DIM ** -0.5
RET_KSCALE = RET_HEAD_DIM ** -0.5

ADAM_LR = 0.001
ADAM_B1 = 0.9
ADAM_B2 = 0.999
ADAM_EPS = 1e-08
ADAM_WD = 0.01
ADAM_STEP = 10

V7X_VMEM_BYTES = 64 * 2 ** 20
VMEM_LIMIT = V7X_VMEM_BYTES - 8 * 2 ** 20
LANES = 128

_NT = (((1,), (1,)), ((), ()))
_TN = (((0,), (0,)), ((), ()))


def _dot(a, b):
    return jnp.dot(a, b, preferred_element_type=F32)


def _dot_nt(a, b):
    return lax.dot_general(a, b, _NT, preferred_element_type=F32)


def _dot_tn(a, b):
    return lax.dot_general(a, b, _TN, preferred_element_type=F32)


def _mx(x):
    return x.astype(MXU_DTYPE)


def _sigmoid(x):
    return 1.0 / (1.0 + jnp.exp(-x))


def _params(sem=None):
    return pltpu.CompilerParams(dimension_semantics=sem, vmem_limit_bytes=VMEM_LIMIT)


def _row_tile(s):
    return min(512, s)


def _ln_proj(x, shift, scale1p, g, w_in_b):
    s = x.shape[0]
    ts = _row_tile(s)

    def body(x_ref, sh_ref, sc_ref, g_ref, w_ref, ret_ref, qkv_ref, sg_ref):
        xv = x_ref[...]
        rstd = lax.rsqrt(jnp.mean(xv * xv, axis=-1, keepdims=True) + EPS)
        h = (xv * rstd * g_ref[...]) * sc_ref[...] + sh_ref[...]
        hb = _mx(h)
        for n in range(4):
            ret_ref[:, n * 512:(n + 1) * 512] = _dot(hb, w_ref[:, n * 512:(n + 1) * 512])
        qkv_ref[:, 0:512] = _mx(_dot(hb, w_ref[:, 2048:2560]) * SB_SCALE)
        qkv_ref[:, 512:1024] = _mx(_dot(hb, w_ref[:, 2560:3072]))
        qkv_ref[:, 1024:1536] = _mx(_dot(hb, w_ref[:, 3072:3584]))
        sg_ref[...] = _dot(hb, w_ref[:, 3584:4096])

    vec = pl.BlockSpec((1, D_MODEL), lambda i: (0, 0))
    return pl.pallas_call(
        body, name="ln_proj", grid=(s // ts,),
        in_specs=[pl.BlockSpec((ts, D_MODEL), lambda i: (i, 0)), vec, vec, vec,
                  pl.BlockSpec((D_MODEL, D_IN), lambda i: (0, 0))],
        out_specs=[pl.BlockSpec((ts, 2048), lambda i: (i, 0)),
                   pl.BlockSpec((ts, 1536), lambda i: (i, 0)),
                   pl.BlockSpec((ts, 512), lambda i: (i, 0))],
        out_shape=[jax.ShapeDtypeStruct((s, 2048), F32),
                   jax.ShapeDtypeStruct((s, 1536), MXU_DTYPE),
                   jax.ShapeDtypeStruct((s, 512), F32)],
        compiler_params=_params(("arbitrary",)),
    )(x, shift, scale1p, g, w_in_b)


def _out_proj(x, gate, y_r, y_s, w_out_b):
    s = x.shape[0]
    ts = _row_tile(s)

    def body(x_ref, gate_ref, yr_ref, ys_ref, w_ref, o_ref):
        t = _dot(yr_ref[...], w_ref[0:512, :]) + _dot(ys_ref[...], w_ref[512:1024, :])
        o_ref[...] = x_ref[...] + gate_ref[...] * t

    return pl.pallas_call(
        body, name="out_proj", grid=(s // ts,),
        in_specs=[pl.BlockSpec((ts, D_MODEL), lambda i: (i, 0)),
                  pl.BlockSpec((1, D_MODEL), lambda i: (0, 0)),
                  pl.BlockSpec((ts, 512), lambda i: (i, 0)),
                  pl.BlockSpec((ts, 512), lambda i: (i, 0)),
                  pl.BlockSpec((D_MODEL, D_MODEL), lambda i: (0, 0))],
        out_specs=pl.BlockSpec((ts, D_MODEL), lambda i: (i, 0)),
        out_shape=jax.ShapeDtypeStruct((s, D_MODEL), F32),
        compiler_params=_params(("arbitrary",)),
    )(x, gate, y_r, y_s, w_out_b)


def _final_loss(x, fg, target):
    s = x.shape[0]
    ts = _row_tile(s)

    def body(x_ref, fg_ref, t_ref, dx_ref, loss_ref, dfg_ref):
        i = pl.program_id(0)

        @pl.when(i == 0)
        def _():
            loss_ref[...] = jnp.zeros_like(loss_ref)
            dfg_ref[...] = jnp.zeros_like(dfg_ref)

        xv = x_ref[...]
        fgv = fg_ref[...]
        rstd = lax.rsqrt(jnp.mean(xv * xv, axis=-1, keepdims=True) + EPS)
        xn = xv * rstd
        err = xn * fgv - t_ref[...]
        tok = jnp.mean(err * err, axis=-1, keepdims=True)
        loss_ref[...] += 0.5 * jnp.sum(tok, axis=0, keepdims=True)
        dy = err * (1.0 / D_MODEL)
        dfg_ref[...] += jnp.sum(dy * xn, axis=0, keepdims=True)
        dxn = dy * fgv
        dx_ref[...] = rstd * (dxn - xn * jnp.mean(dxn * xn, axis=-1, keepdims=True))

    return pl.pallas_call(
        body, name="final_loss", grid=(s // ts,),
        in_specs=[pl.BlockSpec((ts, D_MODEL), lambda i: (i, 0)),
                  pl.BlockSpec((1, D_MODEL), lambda i: (0, 0)),
                  pl.BlockSpec((ts, D_MODEL), lambda i: (i, 0))],
        out_specs=[pl.BlockSpec((ts, D_MODEL), lambda i: (i, 0)),
                   pl.BlockSpec((1, 1), lambda i: (0, 0)),
                   pl.BlockSpec((1, D_MODEL), lambda i: (0, 0))],
        out_shape=[jax.ShapeDtypeStruct((s, D_MODEL), F32),
                   jax.ShapeDtypeStruct((1, 1), F32),
                   jax.ShapeDtypeStruct((1, D_MODEL), F32)],
        compiler_params=_params(("arbitrary",)),
    )(x, fg, target)


def _out_proj_bwd(dx_out, gate, y_r, y_s, w_out_b):
    s = dx_out.shape[0]
    ts = _row_tile(s)

    def body(dx_ref, gate_ref, yr_ref, ys_ref, w_ref, dy_ref, dw_ref, dgate_ref):
        i = pl.program_id(0)

        @pl.when(i == 0)
        def _():
            dw_ref[...] = jnp.zeros_like(dw_ref)
            dgate_ref[...] = jnp.zeros_like(dgate_ref)

        dxv = dx_ref[...]
        dt = _mx(dxv * gate_ref[...])
        yr = yr_ref[...]
        ys = ys_ref[...]
        dy_ref[:, 0:512] = _dot_nt(dt, w_ref[0:512, :])
        dy_ref[:, 512:1024] = _dot_nt(dt, w_ref[512:1024, :])
        dw_ref[0:512, :] += _dot_tn(yr, dt)
        dw_ref[512:1024, :] += _dot_tn(ys, dt)
        t = _dot(yr, w_ref[0:512, :]) + _dot(ys, w_ref[512:1024, :])
        dgate_ref[...] += jnp.sum(dxv * t, axis=0, keepdims=True)

    return pl.pallas_call(
        body, name="out_proj_bwd", grid=(s // ts,),
        in_specs=[pl.BlockSpec((ts, D_MODEL), lambda i: (i, 0)),
                  pl.BlockSpec((1, D_MODEL), lambda i: (0, 0)),
                  pl.BlockSpec((ts, 512), lambda i: (i, 0)),
                  pl.BlockSpec((ts, 512), lambda i: (i, 0)),
                  pl.BlockSpec((D_MODEL, D_MODEL), lambda i: (0, 0))],
        out_specs=[pl.BlockSpec((ts, D_MODEL), lambda i: (i, 0)),
                   pl.BlockSpec((D_MODEL, D_MODEL), lambda i: (0, 0)),
                   pl.BlockSpec((1, D_MODEL), lambda i: (0, 0))],
        out_shape=[jax.ShapeDtypeStruct((s, D_MODEL), F32),
                   jax.ShapeDtypeStruct((D_MODEL, D_MODEL), F32),
                   jax.ShapeDtypeStruct((1, D_MODEL), F32)],
        compiler_params=_params(("arbitrary",)),
    )(dx_out, gate, y_r, y_s, w_out_b)


def _in_proj_bwd_x(x, dx_out, dproj, shift, scale1p, g, w_in_b):
    s = x.shape[0]
    ts = min(256, s)

    def body(x_ref, dxo_ref, dp_ref, sc_ref, g_ref, w_ref, dx_ref, dsh_ref, dsc_ref, dg_ref):
        i = pl.program_id(0)

        @pl.when(i == 0)
        def _():
            dsh_ref[...] = jnp.zeros_like(dsh_ref)
            dsc_ref[...] = jnp.zeros_like(dsc_ref)
            dg_ref[...] = jnp.zeros_like(dg_ref)

        dh = _dot_nt(dp_ref[...], w_ref[...])
        xv = x_ref[...]
        gv = g_ref[...]
        scv = sc_ref[...]
        rstd = lax.rsqrt(jnp.mean(xv * xv, axis=-1, keepdims=True) + EPS)
        xn = xv * rstd
        dsh_ref[...] += jnp.sum(dh, axis=0, keepdims=True)
        dsc_ref[...] += jnp.sum(dh * (xn * gv), axis=0, keepdims=True)
        dhs = dh * scv
        dg_ref[...] += jnp.sum(dhs * xn, axis=0, keepdims=True)
        dxn = dhs * gv
        dx_ref[...] = rstd * (dxn - xn * jnp.mean(dxn * xn, axis=-1, keepdims=True)) + dxo_ref[...]

    del shift
    vec = pl.BlockSpec((1, D_MODEL), lambda i: (0, 0))
    return pl.pallas_call(
        body, name="in_proj_bwd_x", grid=(s // ts,),
        in_specs=[pl.BlockSpec((ts, D_MODEL), lambda i: (i, 0)),
                  pl.BlockSpec((ts, D_MODEL), lambda i: (i, 0)),
                  pl.BlockSpec((ts, D_IN), lambda i: (i, 0)),
                  vec, vec,
                  pl.BlockSpec((D_MODEL, D_IN), lambda i: (0, 0))],
        out_specs=[pl.BlockSpec((ts, D_MODEL), lambda i: (i, 0)), vec, vec, vec],
        out_shape=[jax.ShapeDtypeStruct((s, D_MODEL), F32),
                   jax.ShapeDtypeStruct((1, D_MODEL), F32),
                   jax.ShapeDtypeStruct((1, D_MODEL), F32),
                   jax.ShapeDtypeStruct((1, D_MODEL), F32)],
        compiler_params=_params(("arbitrary",)),
    )(x, dx_out, dproj, scale1p, g, w_in_b)


def _in_proj_bwd_w(x, dproj, shift, scale1p, g):
    s = x.shape[0]
    ts = _row_tile(s)
    nb = D_IN // N_DEV

    def body(x_ref, dp_ref, sh_ref, sc_ref, g_ref, dw_ref):
        i = pl.program_id(1)

        @pl.when(i == 0)
        def _():
            dw_ref[...] = jnp.zeros_like(dw_ref)

        xv = x_ref[...]
        rstd = lax.rsqrt(jnp.mean(xv * xv, axis=-1, keepdims=True) + EPS)
        h = (xv * rstd * g_ref[...]) * sc_ref[...] + sh_ref[...]
        dw_ref[...] += _dot_tn(_mx(h), dp_ref[...])

    vec = pl.BlockSpec((1, D_MODEL), lambda j, i: (0, 0))
    return pl.pallas_call(
        body, name="in_proj_bwd_w", grid=(N_DEV, s // ts),
        in_specs=[pl.BlockSpec((ts, D_MODEL), lambda j, i: (i, 0)),
                  pl.BlockSpec((ts, nb), lambda j, i: (i, j)),
                  vec, vec, vec],
        out_specs=pl.BlockSpec((None, D_MODEL, nb), lambda j, i: (j, 0, 0)),
        out_shape=jax.ShapeDtypeStruct((N_DEV, D_MODEL, nb), F32),
        compiler_params=_params(("arbitrary", "arbitrary")),
    )(x, dproj, shift, scale1p, g)


RET_TILE = 256


def _ret_tables(s):
    t = min(RET_TILE, s)
    half = RET_HEAD_DIM // 2
    pos = jnp.arange(s, dtype=F32)
    inv = ROPE_BASE ** (-jnp.arange(half, dtype=F32) / half)
    ang = pos[:, None] * inv[None, :]
    cos, sin = jnp.cos(ang), jnp.sin(ang)
    cos2 = jnp.concatenate([cos, cos], axis=1)
    sin2 = jnp.concatenate([-sin, sin], axis=1)
    lg = jnp.log1p(-(2.0 ** (-5.0 - jnp.arange(RET_HEADS, dtype=F32))))[:, None, None]
    n = jnp.arange(t)
    dist = (n[:, None] - n[None, :]).astype(F32)[None]
    cn = (n // CHUNK)[:, None]
    cm = (n // CHUNK)[None, :]
    mask = jnp.where((cn == cm)[None], jnp.exp(jnp.abs(dist) * lg),
                     jnp.where((cm < cn)[None], jnp.exp(dist * lg), 0.0))
    nf = n.astype(F32)[None, :, None]
    dq = jnp.broadcast_to(jnp.exp((nf + 1.0) * lg), (RET_HEADS, t, LANES))
    dk = jnp.broadcast_to(jnp.exp((t - 1.0 - nf) * lg), (RET_HEADS, t, LANES))
    gt = jnp.broadcast_to(jnp.exp(float(t) * lg), (RET_HEADS, 1, LANES))
    return cos2, sin2, mask, dq, dk, gt


def _roll_half(x):
    return pltpu.roll(x, RET_HEAD_DIM // 2, 1)


def _ret_head_fwd(q, k, v, cos, sin, m, dq, dk, s0):
    qr = q * cos + _roll_half(q) * sin
    kr = (k * cos + _roll_half(k) * sin) * RET_KSCALE
    qb, kb, vb = _mx(qr), _mx(kr), _mx(v)
    p = _dot_nt(qb, kb) * m
    pb = _mx(p)
    o = _dot(pb, vb) + _dot(qb, _mx(s0)) * dq
    kdb = _mx(kr * dk)
    mu = jnp.mean(o, axis=-1, keepdims=True)
    oc = o - mu
    rstd = lax.rsqrt(jnp.mean(oc * oc, axis=-1, keepdims=True) + EPS)
    return qb, kb, vb, pb, kdb, oc * rstd, rstd


def _retention_fwd(ret, tables):
    cos2, sin2, mask, dq, dk, gt = tables
    s = ret.shape[0]
    t = mask.shape[1]
    nt = s // t
    hd = RET_HEAD_DIM

    def body(ret_ref, cos_ref, sin_ref, m_ref, dq_ref, dk_ref, gt_ref, y_ref, st_ref, s_scr):
        i = pl.program_id(0)

        @pl.when(i == 0)
        def _():
            s_scr[...] = jnp.zeros_like(s_scr)

        cos = cos_ref[...]
        sin = sin_ref[...]
        for h in range(RET_HEADS):
            q = ret_ref[:, h * hd:(h + 1) * hd]
            k = ret_ref[:, 512 + h * hd:512 + (h + 1) * hd]
            v = ret_ref[:, 1024 + h * hd:1024 + (h + 1) * hd]
            g = ret_ref[:, 1536 + h * hd:1536 + (h + 1) * hd]
            s0 = s_scr[h]
            st_ref[h] = s0
            _, _, vb, _, kdb, gn, _ = _ret_head_fwd(q, k, v, cos, sin, m_ref[h], dq_ref[h], dk_ref[h], s0)
            y_ref[:, h * hd:(h + 1) * hd] = (gn * (g * _sigmoid(g))).astype(y_ref.dtype)
            s_scr[h] = s0 * gt_ref[h] + _dot_tn(kdb, vb)

    full3 = lambda a: pl.BlockSpec(a.shape, lambda i: (0, 0, 0))
    return pl.pallas_call(
        body, name="retention_fwd", grid=(nt,),
        in_specs=[pl.BlockSpec((t, 2048), lambda i: (i, 0)),
                  pl.BlockSpec((t, LANES), lambda i: (i, 0)),
                  pl.BlockSpec((t, LANES), lambda i: (i, 0)),
                  full3(mask), full3(dq), full3(dk), full3(gt)],
        out_specs=[pl.BlockSpec((t, 512), lambda i: (i, 0)),
                   pl.BlockSpec((None, RET_HEADS, hd, hd), lambda i: (i, 0, 0, 0))],
        out_shape=[jax.ShapeDtypeStruct((s, 512), MXU_DTYPE),
                   jax.ShapeDtypeStruct((nt, RET_HEADS, hd, hd), F32)],
        scratch_shapes=[pltpu.VMEM((RET_HEADS, hd, hd), F32)],
        compiler_params=_params(("arbitrary",)),
    )(ret, cos2, sin2, mask, dq, dk, gt)


def _retention_bwd(ret, states, dy, tables):
    cos2, sin2, mask, dq, dk, gt = tables
    s = ret.shape[0]
    t = mask.shape[1]
    nt = s // t
    hd = RET_HEAD_DIM

    def body(ret_ref, st_ref, dy_ref, cos_ref, sin_ref, m_ref, dq_ref, dk_ref, gt_ref, d_ref, ds_scr):
        i = pl.program_id(0)

        @pl.when(i == 0)
        def _():
            ds_scr[...] = jnp.zeros_like(ds_scr)

        cos = cos_ref[...]
        sin = sin_ref[...]
        for h in range(RET_HEADS):
            q = ret_ref[:, h * hd:(h + 1) * hd]
            k = ret_ref[:, 512 + h * hd:512 + (h + 1) * hd]
            v = ret_ref[:, 1024 + h * hd:1024 + (h + 1) * hd]
            g = ret_ref[:, 1536 + h * hd:1536 + (h + 1) * hd]
            dyv = dy_ref[:, h * hd:(h + 1) * hd]
            m = m_ref[h]
            dqv = dq_ref[h]
            dkv = dk_ref[h]
            s0 = st_ref[h]
            ds = ds_scr[h]
            qb, kb, vb, pb, kdb, gn, rstd = _ret_head_fwd(q, k, v, cos, sin, m, dqv, dkv, s0)
            sg = _sigmoid(g)
            d_ref[:, 1536 + h * hd:1536 + (h + 1) * hd] = (
                dyv * gn * (sg * (1.0 + g * (1.0 - sg)))).astype(d_ref.dtype)
            dgn = dyv * (g * sg)
            do = rstd * (dgn - jnp.mean(dgn, axis=-1, keepdims=True)
                         - gn * jnp.mean(dgn * gn, axis=-1, keepdims=True))
            dob = _mx(do)
            dodb = _mx(do * dqv)
            dsb = _mx(ds)
            d_ref[:, 1024 + h * hd:1024 + (h + 1) * hd] = (
                _dot_tn(pb, dob) + _dot(kdb, dsb)).astype(d_ref.dtype)
            dpb = _mx(_dot_nt(dob, vb) * m)
            dqr = _dot(dpb, kb) + _dot_nt(dodb, _mx(s0))
            dkr = (_dot_tn(dpb, qb) + _dot_nt(vb, dsb) * dkv) * RET_KSCALE
            d_ref[:, h * hd:(h + 1) * hd] = (dqr * cos + _roll_half(dqr * sin)).astype(d_ref.dtype)
            d_ref[:, 512 + h * hd:512 + (h + 1) * hd] = (
                dkr * cos + _roll_half(dkr * sin)).astype(d_ref.dtype)
            ds_scr[h] = ds * gt_ref[h] + _dot_tn(qb, dodb)

    full3 = lambda a: pl.BlockSpec(a.shape, lambda i: (0, 0, 0))
    rev = lambda i: (nt - 1 - i, 0)
    return pl.pallas_call(
        body, name="retention_bwd", grid=(nt,),
        in_specs=[pl.BlockSpec((t, 2048), rev),
                  pl.BlockSpec((None, RET_HEADS, hd, hd), lambda i: (nt - 1 - i, 0, 0, 0)),
                  pl.BlockSpec((t, 512), rev),
                  pl.BlockSpec((t, LANES), rev),
                  pl.BlockSpec((t, LANES), rev),
                  full3(mask), full3(dq), full3(dk), full3(gt)],
        out_specs=pl.BlockSpec((t, 2048), rev),
        out_shape=jax.ShapeDtypeStruct((s, 2048), MXU_DTYPE),
        scratch_shapes=[pltpu.VMEM((RET_HEADS, hd, hd), F32)],
        compiler_params=_params(("arbitrary",)),
    )(ret, states, dy, cos2, sin2, mask, dq, dk, gt)


SB_BLOCK = 256


SB_SKIP = 104.0


def _split_dots(xs, u):
    parts = []
    for x in xs:
        hi = lax.bitcast_convert_type(lax.bitcast_convert_type(x, jnp.uint32) & jnp.uint32(0xFFFF0000), F32)
        parts += [_mx(hi), _mx(x - hi)]
    out = _dot(jnp.concatenate(parts, axis=0), u)
    n = xs[0].shape[0]
    return [out[2 * k * n:(2 * k + 1) * n] + out[(2 * k + 1) * n:(2 * k + 2) * n] for k in range(len(xs))]


def _split_dot(x, u):
    return _split_dots([x], u)[0]


def _sb_pair_weights(lb, lk, allowed, u_gt):
    blk = lb.shape[0]
    lk_p, lk_d = lk[:, :blk], lk[:, blk:]
    r_d = _rowsum(lk_d)
    cs_p, cs_d = _split_dots([lk_p, lk_d], u_gt)
    a = jnp.exp(lb + jnp.concatenate([cs_p + r_d, cs_d], axis=1))
    return jnp.where(allowed, a, 0.0), r_d, r_d + _rowsum(lk_p)


def _sb_logits(q, k, causal):
    z = _dot_nt(q, k)
    l1p = jnp.log(1.0 + jnp.exp(-jnp.abs(z)))
    lk = -(jnp.maximum(z, 0.0) + l1p)
    if causal is not None:
        lk = jnp.where(causal, lk, 0.0)
    return jnp.minimum(z, 0.0) - l1p, lk


def _sb_weights(lb, lk, r, u_gt, causal):
    a = jnp.exp(lb + _split_dot(lk, u_gt) + r)
    return a if causal is None else jnp.where(causal, a, 0.0)


def _rowsum(x):
    return jnp.sum(x, axis=1, keepdims=True)


def _sb_pair_tile(i, blk):
    row = lax.broadcasted_iota(jnp.int32, (blk, 2 * blk), 0)
    col = lax.broadcasted_iota(jnp.int32, (blk, 2 * blk), 1)
    first_col = jnp.where(i >= 1, 0, blk)
    allowed = jnp.logical_and(row > col - blk, col >= first_col)
    rows_p = pl.ds(pl.multiple_of(jnp.maximum(i - 1, 0) * blk, blk), blk)
    rows_d = pl.ds(pl.multiple_of(i * blk, blk), blk)
    return allowed, rows_p, rows_d


def _sb_fwd(qkv, sg):
    s = qkv.shape[0]
    blk = min(SB_BLOCK, s)
    nq = s // blk
    hd = SB_HEAD_DIM

    def body(q_ref, k_ref, v_ref, g_ref, y_ref, o_ref):
        i = pl.program_id(1)
        row = lax.broadcasted_iota(jnp.int32, (blk, blk), 0)
        col = lax.broadcasted_iota(jnp.int32, (blk, blk), 1)
        u_gt = (row > col).astype(MXU_DTYPE)
        heads = [slice(hh * hd, (hh + 1) * hd) for hh in range(2)]
        qs = [q_ref[:, ls] for ls in heads]
        allowed, rows_p, rows_d = _sb_pair_tile(i, blk)

        def pair(hh):
            ls = heads[hh]
            kc = jnp.concatenate([k_ref[rows_p, ls], k_ref[rows_d, ls]], axis=0)
            vc = jnp.concatenate([v_ref[rows_p, ls], v_ref[rows_d, ls]], axis=0)
            lb, lk = _sb_logits(qs[hh], kc, allowed)
            a, _, r = _sb_pair_weights(lb, lk, allowed, u_gt)
            return _dot(_mx(a), vc), r

        def block(hh, j, r):
            start = pl.multiple_of(j * blk, blk)
            lb, lk = _sb_logits(qs[hh], k_ref[pl.ds(start, blk), heads[hh]], None)
            a = _sb_weights(lb, lk, r, u_gt, None)
            return _dot(_mx(a), v_ref[pl.ds(start, blk), heads[hh]]), r + _rowsum(lk)

        acc0, r0 = pair(0)
        acc1, r1 = pair(1)

        def more(c):
            n, _, r0, _, r1 = c
            return jnp.logical_and(n < i, jnp.max(jnp.maximum(r0, r1)) > -SB_SKIP)

        def step(c):
            n, acc0, r0, acc1, r1 = c
            pv0, r0 = block(0, i - 1 - n, r0)
            pv1, r1 = block(1, i - 1 - n, r1)
            return n + 1, acc0 + pv0, r0, acc1 + pv1, r1

        _, acc0, _, acc1, _ = lax.while_loop(more, step, (jnp.int32(1), acc0, r0, acc1, r1))
        for ls, acc in zip(heads, (acc0, acc1)):
            g = g_ref[:, ls]
            o_ref[:, ls] = acc
            y_ref[:, ls] = (acc * (g * _sigmoid(g))).astype(y_ref.dtype)

    qblk = pl.BlockSpec((blk, LANES), lambda p, i: (i, p))
    return pl.pallas_call(
        body, name="stickbreak_fwd", grid=(SB_HEADS // 2, nq),
        in_specs=[qblk,
                  pl.BlockSpec((s, LANES), lambda p, i: (0, 4 + p)),
                  pl.BlockSpec((s, LANES), lambda p, i: (0, 8 + p)),
                  qblk],
        out_specs=[qblk, qblk],
        out_shape=[jax.ShapeDtypeStruct((s, 512), MXU_DTYPE),
                   jax.ShapeDtypeStruct((s, 512), F32)],
        compiler_params=_params(("arbitrary", "arbitrary")),
    )(qkv, qkv, qkv, sg)


def _sb_bwd(qkv, sg, o, dy):
    s = qkv.shape[0]
    blk = min(SB_BLOCK, s)
    nq = s // blk
    hd = SB_HEAD_DIM
    assert nq <= LANES

    def body(q_ref, k_ref, v_ref, g_ref, o_ref, dy_ref, dq_ref, dk_ref, dv_ref, dg_ref, dk_scr, dv_scr):
        i = pl.program_id(1)

        @pl.when(i == 0)
        def _():
            dk_scr[...] = jnp.zeros_like(dk_scr)
            dv_scr[...] = jnp.zeros_like(dv_scr)

        row = lax.broadcasted_iota(jnp.int32, (blk, blk), 0)
        col = lax.broadcasted_iota(jnp.int32, (blk, blk), 1)
        lane = lax.broadcasted_iota(jnp.int32, (blk, LANES), 1)
        u_gt = (row > col).astype(MXU_DTYPE)
        u_lt = (row < col).astype(MXU_DTYPE)
        heads = [slice(hh * hd, (hh + 1) * hd) for hh in range(2)]
        qs = [q_ref[:, ls] for ls in heads]
        allowed, rows_p, rows_d = _sb_pair_tile(i, blk)
        dobs, kcs, lbs, a_s, ras = [], [], [], [], []
        for hh, ls in enumerate(heads):
            g = g_ref[:, ls]
            dyv = dy_ref[:, ls]
            sgm = _sigmoid(g)
            dg_ref[:, ls] = (dyv * o_ref[:, ls] * (sgm * (1.0 + g * (1.0 - sgm)))).astype(dg_ref.dtype)
            dobs.append(_mx(dyv * (g * sgm)))
            kcs.append(jnp.concatenate([k_ref[rows_p, ls], k_ref[rows_d, ls]], axis=0))
            lb, lk = _sb_logits(qs[hh], kcs[hh], allowed)
            a, _, r = _sb_pair_weights(lb, lk, allowed, u_gt)
            lbs.append(lb)
            a_s.append(a)
            ras.append(r)

        def scan_block(hh, j, r, rmat):
            start = pl.multiple_of(j * blk, blk)
            _, lk = _sb_logits(qs[hh], k_ref[pl.ds(start, blk), heads[hh]], None)
            return r + _rowsum(lk), jnp.where(lane == j, r, rmat)

        def more(c):
            n, r0, _, r1, _ = c
            return jnp.logical_and(n < i, jnp.max(jnp.maximum(r0, r1)) > -SB_SKIP)

        def scan_step(c):
            n, r0, rmat0, r1, rmat1 = c
            r0, rmat0 = scan_block(0, i - 1 - n, r0, rmat0)
            r1, rmat1 = scan_block(1, i - 1 - n, r1, rmat1)
            return n + 1, r0, rmat0, r1, rmat1

        zmat = jnp.zeros((blk, LANES), F32)
        n, _, rmat0, _, rmat1 = lax.while_loop(more, scan_step, (jnp.int32(1), ras[0], zmat, ras[1], zmat))
        rmats = (rmat0, rmat1)

        def block(hh, j, pg):
            ls = heads[hh]
            start = pl.multiple_of(j * blk, blk)
            k = k_ref[pl.ds(start, blk), ls]
            lb, lk = _sb_logits(qs[hh], k, None)
            r = _rowsum(jnp.where(lane == j, rmats[hh], 0.0))
            a = _sb_weights(lb, lk, r, u_gt, None)
            gm = _dot_nt(dobs[hh], v_ref[pl.ds(start, blk), ls]) * a
            dzb = _mx(gm - (gm + (pg + _split_dot(gm, u_lt))) * jnp.exp(lb))
            dk_scr[hh, pl.ds(start, blk), :] += _dot_tn(dzb, qs[hh])
            dv_scr[hh, pl.ds(start, blk), :] += _dot_tn(_mx(a), dobs[hh])
            return _dot(dzb, k), pg + _rowsum(gm)

        def step(t, c):
            acc0, pg0, acc1, pg1 = c
            dq0, pg0 = block(0, i - n + t, pg0)
            dq1, pg1 = block(1, i - n + t, pg1)
            return acc0 + dq0, pg0, acc1 + dq1, pg1

        zero = jnp.zeros((blk, 1), F32)
        zacc = jnp.zeros((blk, hd), F32)
        far = lax.fori_loop(0, n - 1, step, (zacc, zero, zacc, zero))
        for hh, ls in enumerate(heads):
            acc, pg = far[2 * hh], far[2 * hh + 1]
            a = a_s[hh]
            vc = jnp.concatenate([v_ref[rows_p, ls], v_ref[rows_d, ls]], axis=0)
            gm = _dot_nt(dobs[hh], vc) * a
            gm_p, gm_d = gm[:, :blk], gm[:, blk:]
            pre_p, pre_d = _split_dots([gm_p, gm_d], u_lt)
            pre = jnp.concatenate([pre_p + pg, pre_d + (pg + _rowsum(gm_p))], axis=1)
            dz = jnp.where(allowed, gm - (gm + pre) * jnp.exp(lbs[hh]), 0.0)
            dzb = _mx(dz)
            ab = _mx(a)
            dk_scr[hh, rows_p, :] += _dot_tn(dzb[:, :blk], qs[hh])
            dk_scr[hh, rows_d, :] += _dot_tn(dzb[:, blk:], qs[hh])
            dv_scr[hh, rows_p, :] += _dot_tn(ab[:, :blk], dobs[hh])
            dv_scr[hh, rows_d, :] += _dot_tn(ab[:, blk:], dobs[hh])
            dq_ref[:, ls] = ((acc + _dot(dzb, kcs[hh])) * SB_SCALE).astype(dq_ref.dtype)

        @pl.when(i == nq - 1)
        def _():
            for hh in range(2):
                ls = slice(hh * hd, (hh + 1) * hd)
                dk_ref[:, ls] = dk_scr[hh].astype(dk_ref.dtype)
                dv_ref[:, ls] = dv_scr[hh].astype(dv_ref.dtype)

    qblk = lambda c0: pl.BlockSpec((blk, LANES), lambda p, i: (i, c0 + p))
    full = lambda c0: pl.BlockSpec((s, LANES), lambda p, i: (0, c0 + p))
    half = jax.ShapeDtypeStruct((s, 512), MXU_DTYPE)
    return pl.pallas_call(
        body, name="stickbreak_bwd", grid=(SB_HEADS // 2, nq),
        in_specs=[qblk(0), full(4), full(8), qblk(0), qblk(0), qblk(4)],
        out_specs=[qblk(0), full(0), full(0), qblk(0)],
        out_shape=[half, half, half, half],
        scratch_shapes=[pltpu.VMEM((2, s, hd), F32), pltpu.VMEM((2, s, hd), F32)],
        compiler_params=_params(("arbitrary", "arbitrary")),
    )(qkv, qkv, qkv, sg, o, dy)


def _layer_fwd(x, mod, norm_g, w_in_b, w_out_b, tables):
    shift, scale1p, gate = mod[0:1], 1.0 + mod[1:2], mod[2:3]
    ret, qkv, sg = _ln_proj(x, shift, scale1p, norm_g, w_in_b)
    y_r, states = _retention_fwd(ret, tables)
    y_s, o_s = _sb_fwd(qkv, sg)
    x_next = _out_proj(x, gate, y_r, y_s, w_out_b)
    saved = (x, shift, scale1p, gate, ret, qkv, sg, y_r, states, y_s, o_s)
    return x_next, saved


def _layer_bwd(dx_out, saved, norm_g, w_in_b, w_out_b, tables):
    x, shift, scale1p, gate, ret, qkv, sg, y_r, states, y_s, o_s = saved
    dy, dw_out, dgate = _out_proj_bwd(dx_out, gate, y_r, y_s, w_out_b)
    d_ret = _retention_bwd(ret, states, dy, tables)
    d_q, d_k, d_v, d_g = _sb_bwd(qkv, sg, o_s, dy)
    dproj = jnp.concatenate([d_ret, d_q, d_k, d_v, d_g], axis=1)
    dx, dshift, dscale, dnorm_g = _in_proj_bwd_x(x, dx_out, dproj, shift, scale1p, norm_g, w_in_b)
    dw_in = _in_proj_bwd_w(x, dproj, shift, scale1p, norm_g)
    dmod = jnp.concatenate([dshift, dscale, dgate], axis=1)
    return dx, dw_in, dw_out, dmod, dnorm_g


MESH_IDS = pl.DeviceIdType.MESH
N_PEERS = N_DEV - 1
HBM_SPEC = pl.BlockSpec(memory_space=pl.ANY)


def _my_place():
    return lax.axis_index("x"), lax.axis_index("y"), lax.axis_index("c")


def _linear(px, py, pc):
    return 4 * px + 2 * py + pc


def _all_gather(blocks):
    n_arr = len(blocks)

    def body(*refs):
        x_refs = refs[:n_arr]
        out_refs = refs[n_arr:2 * n_arr]
        send_sems, recv_sems, local_sems = refs[2 * n_arr:]
        x, y, c = _my_place()
        me, sibling = (x, y, c), (x, y, 1 - c)
        chips = [(1 - x, y), (x, 1 - y), (1 - x, 1 - y)]

        def rows(a, place):
            m = x_refs[a].shape[0]
            return out_refs[a].at[pl.ds(_linear(*place) * m, m), :]

        def copy(a, k, block, to, src=None):
            return pltpu.make_async_remote_copy(
                src_ref=rows(a, block) if src is None else src, dst_ref=rows(a, block),
                send_sem=send_sems.at[a * N_PEERS + k], recv_sem=recv_sems.at[a * N_PEERS + k],
                device_id=to, device_id_type=MESH_IDS)

        mine = [pltpu.make_async_copy(x_refs[a], rows(a, me), local_sems.at[a]) for a in range(n_arr)]
        for cp in mine:
            cp.start()
        first = []
        for a in range(n_arr):
            first.append(copy(a, 0, me, sibling, src=x_refs[a]))
            first += [copy(a, 1 + j, me, (*chip, c), src=x_refs[a]) for j, chip in enumerate(chips)]
        for cp in first:
            cp.start()
        passed = []
        for j, chip in enumerate(chips):
            for a in range(n_arr):
                copy(a, 1 + j, (*chip, c), me).wait_recv()
                fwd = copy(a, 4 + j, (*chip, c), sibling)
                fwd.start()
                passed.append(fwd)
        for a in range(n_arr):
            copy(a, 0, sibling, me).wait_recv()
            for j, chip in enumerate(chips):
                copy(a, 4 + j, (*chip, 1 - c), me).wait_recv()
        for cp in first + passed:
            cp.wait_send()
        for cp in mine:
            cp.wait()

    return pl.pallas_call(
        body, name="all_gather",
        out_shape=[jax.ShapeDtypeStruct((N_DEV * b.shape[0], b.shape[1]), b.dtype) for b in blocks],
        in_specs=[HBM_SPEC] * n_arr, out_specs=[HBM_SPEC] * n_arr,
        scratch_shapes=[pltpu.SemaphoreType.DMA((n_arr * N_PEERS,)),
                        pltpu.SemaphoreType.DMA((n_arr * N_PEERS,)),
                        pltpu.SemaphoreType.DMA((n_arr,))],
    )(*blocks)


def _exchange(parts):
    n_arr = len(parts)

    def body(*refs):
        p_refs = refs[:n_arr]
        o_refs = refs[n_arr:2 * n_arr]
        send_sems, recv_sems, local_sems = refs[2 * n_arr:]
        x, y, c = _my_place()
        me = _linear(x, y, c)
        mine = [pltpu.make_async_copy(p_refs[a].at[me], o_refs[a].at[me], local_sems.at[a]) for a in range(n_arr)]
        for cp in mine:
            cp.start()
        copies = []
        for r in range(1, N_DEV):
            peer = (1 - x if r & 4 else x, 1 - y if r & 2 else y, 1 - c if r & 1 else c)
            for a in range(n_arr):
                cp = pltpu.make_async_remote_copy(
                    src_ref=p_refs[a].at[_linear(*peer)], dst_ref=o_refs[a].at[me],
                    send_sem=send_sems.at[a * N_PEERS + r - 1], recv_sem=recv_sems.at[a * N_PEERS + r - 1],
                    device_id=peer, device_id_type=MESH_IDS)
                cp.start()
                copies.append(cp)
        for cp in copies:
            cp.wait_recv()
        for cp in copies:
            cp.wait_send()
        for cp in mine:
            cp.wait()

    return pl.pallas_call(
        body, name="exchange",
        out_shape=[jax.ShapeDtypeStruct(p.shape, p.dtype) for p in parts],
        in_specs=[HBM_SPEC] * n_arr, out_specs=[HBM_SPEC] * n_arr,
        scratch_shapes=[pltpu.SemaphoreType.DMA((n_arr * N_PEERS,)),
                        pltpu.SemaphoreType.DMA((n_arr * N_PEERS,)),
                        pltpu.SemaphoreType.DMA((n_arr,))],
    )(*parts)


def _ada_fwd(c_all, w_ada, b_cols):
    cols = w_ada.shape[2]

    def body(c_ref, w_ref, b_ref, ca_ref, mod_ref):
        cv = c_ref[...]
        ca = cv * _sigmoid(cv)
        ca_ref[...] = ca
        cb = _mx(jnp.concatenate([ca, ca], axis=0))
        for l in range(DEPTH):
            mod_ref[l * N_DEV:(l + 1) * N_DEV, :] = _dot(cb, _mx(w_ref[l]))[0:N_DEV] + b_ref[l]

    return pl.pallas_call(
        body, name="ada_fwd",
        out_shape=[jax.ShapeDtypeStruct((N_DEV, D_MODEL), F32),
                   jax.ShapeDtypeStruct((DEPTH * N_DEV, cols), F32)],
        compiler_params=_params(),
    )(c_all, w_ada, b_cols)


def _ada_bwd(c_act_t, dmod_cols):
    cols = dmod_cols.shape[2]

    def body(ca_ref, dm_ref, o_ref):
        ca = _mx(ca_ref[...]).astype(F32)
        for l in range(DEPTH):
            o_ref[l] = jnp.dot(ca, _mx(dm_ref[l]).astype(F32),
                               precision=lax.Precision.HIGHEST, preferred_element_type=F32)

    return pl.pallas_call(
        body, name="ada_bwd",
        out_shape=jax.ShapeDtypeStruct((DEPTH, D_MODEL, cols), F32),
        compiler_params=_params(),
    )(c_act_t, dmod_cols)


def _sum_adamw(parts, w, m, v):
    n_slab, rows, cols = parts.shape
    tr = min(256, rows)

    def body(p_ref, w_ref, m_ref, v_ref, g_ref, d_ref, mo_ref, vo_ref):
        g = p_ref[0]
        for sl in range(1, n_slab):
            g = g + p_ref[sl]
        m2 = ADAM_B1 * m_ref[...] + (1.0 - ADAM_B1) * g
        v2 = ADAM_B2 * v_ref[...] + (1.0 - ADAM_B2) * (g * g)
        m_hat = m2 / (1.0 - ADAM_B1 ** ADAM_STEP)
        v_hat = v2 / (1.0 - ADAM_B2 ** ADAM_STEP)
        g_ref[...] = g
        d_ref[...] = -ADAM_LR * (m_hat / (jnp.sqrt(v_hat) + ADAM_EPS) + ADAM_WD * w_ref[...])
        mo_ref[...] = m2
        vo_ref[...] = v2

    blk = pl.BlockSpec((tr, cols), lambda i: (i, 0))
    shp = jax.ShapeDtypeStruct((rows, cols), F32)
    return pl.pallas_call(
        body, name="sum_adamw", grid=(rows // tr,),
        in_specs=[pl.BlockSpec((n_slab, tr, cols), lambda i: (0, i, 0)), blk, blk, blk],
        out_specs=[blk, blk, blk, blk],
        out_shape=[shp, shp, shp, shp],
        compiler_params=_params(("arbitrary",)),
    )(parts, w, m, v)


SMALL_ROWS = 16


def kernel(x, c, norm_g, w_ada, b_ada, w_in, w_out, final_g, loss_target, m_norm_g, m_w_ada, m_b_ada, m_w_in, m_w_out, m_final_g, v_norm_g, v_w_ada, v_b_ada, v_w_in, v_w_out, v_final_g):
    me = _linear(*_my_place())
    in_cols = w_in.shape[2]
    out_rows = w_out.shape[1]
    ada_cols = w_ada.shape[2]

    g_in, g_out, g_c = _all_gather([
        _mx(w_in).reshape(DEPTH * D_MODEL, in_cols),
        _mx(w_out).reshape(DEPTH * out_rows, D_MODEL),
        jnp.broadcast_to(c, (8, D_MODEL))])
    w_in_b = g_in.reshape(N_DEV, DEPTH, D_MODEL, in_cols).transpose(1, 2, 0, 3).reshape(DEPTH, D_MODEL, D_IN)
    w_out_b = g_out.reshape(N_DEV, DEPTH, out_rows, D_MODEL).transpose(1, 0, 2, 3).reshape(DEPTH, D_MODEL, D_MODEL)
    c_all = g_c.reshape(N_DEV, 8, D_MODEL)[:, 0]

    b_cols = lax.dynamic_slice_in_dim(b_ada, me * ada_cols, ada_cols, axis=1)[:, None, :]
    c_act, mod_cols = _ada_fwd(c_all, w_ada, b_cols)
    (g_mod,) = _all_gather([mod_cols])
    g_mod = g_mod.reshape(N_DEV, DEPTH, N_DEV, ada_cols)
    mod = lax.dynamic_index_in_dim(g_mod, me, axis=2, keepdims=False)
    mod = mod.transpose(1, 0, 2).reshape(DEPTH, 3, D_MODEL)

    tables = _ret_tables(x.shape[1])
    h = x[0]
    saved = []
    for l in range(DEPTH):
        h, sv = _layer_fwd(h, mod[l], norm_g[l:l + 1], w_in_b[l], w_out_b[l], tables)
        saved.append(sv)
    dx, loss_part, dfg = _final_loss(h, final_g[None], loss_target[0])
    dw_in, dw_out, small = [None] * DEPTH, [None] * DEPTH, [None] * DEPTH
    for l in reversed(range(DEPTH)):
        dx, dw_in[l], dwo, dmod, dng = _layer_bwd(dx, saved[l], norm_g[l:l + 1], w_in_b[l], w_out_b[l], tables)
        dw_out[l] = dwo.reshape(N_DEV, out_rows, D_MODEL)
        small[l] = (dmod.reshape(3, D_MODEL), dng)

    pad = jnp.zeros((SMALL_ROWS - 10, D_MODEL), F32)
    small_block = jnp.concatenate([small[0][0], small[1][0], small[0][1], small[1][1], dfg,
                                   jnp.broadcast_to(loss_part, (1, D_MODEL)), pad], axis=0)
    (g_small,) = _all_gather([small_block])
    g_small = g_small.reshape(N_DEV, SMALL_ROWS, D_MODEL)

    def small_pack(b, n, f, fill):
        return jnp.concatenate([b.reshape(6, D_MODEL), n, f[None],
                                jnp.full((SMALL_ROWS - 9, D_MODEL), fill, F32)], axis=0)

    s_g, s_d, s_m, s_v = _sum_adamw(g_small, small_pack(b_ada, norm_g, final_g, 0.0),
                                    small_pack(m_b_ada, m_norm_g, m_final_g, 0.0),
                                    small_pack(v_b_ada, v_norm_g, v_final_g, 1.0))
    loss = s_g[9, 0]

    def small_unpack(a):
        return a[0:6].reshape(DEPTH, 3 * D_MODEL), a[6:8], a[8]

    dmod_all = g_small[:, 0:6].reshape(N_DEV, DEPTH, 3 * D_MODEL).transpose(1, 0, 2)
    dmod_cols = lax.dynamic_slice_in_dim(dmod_all, me * ada_cols, ada_cols, axis=2)
    g_ada = _ada_bwd(c_act.T, dmod_cols).reshape(1, DEPTH * D_MODEL, ada_cols)
    ada = _sum_adamw(g_ada, *[a.reshape(DEPTH * D_MODEL, ada_cols) for a in (w_ada, m_w_ada, v_w_ada)])
    ada = [a.reshape(DEPTH, D_MODEL, ada_cols) for a in ada]

    recv = _exchange([dw_in[0], dw_in[1], dw_out[0], dw_out[1]])
    win = [_sum_adamw(recv[l], w_in[l], m_w_in[l], v_w_in[l]) for l in range(DEPTH)]
    wout = [_sum_adamw(recv[DEPTH + l], w_out[l], m_w_out[l], v_w_out[l]) for l in range(DEPTH)]
    win = [jnp.stack([win[0][k], win[1][k]]) for k in range(4)]
    wout = [jnp.stack([wout[0][k], wout[1][k]]) for k in range(4)]

    outs = [loss, dx[None]]
    for k in range(4):
        b, n, f = small_unpack((s_g, s_d, s_m, s_v)[k])
        outs += [n, ada[k], b, win[k], wout[k], f]
    return tuple(outs)
```

```python
import functools

import jax
import jax.numpy as jnp
from jax import lax
from jax.experimental import pallas as pl
from jax.experimental.pallas import tpu as pltpu

F32 = jnp.float32
MXU_DTYPE = jnp.bfloat16

D_MODEL = 1024
DEPTH = 2
N_DEV = 8
CHUNK = 64
D_RET = 512
D_SB = 512
RET_HEADS = 4
RET_HEAD_DIM = 128
SB_HEADS = 8
SB_HEAD_DIM = 64
D_IN = 4096
ROPE_BASE = 10000.0
EPS = 1e-6
SB_SCALE = SB_HEAD_DIM ** -0.5
RET_KSCALE = RET_HEAD_DIM ** -0.5

ADAM_LR = 0.001
ADAM_B1 = 0.9
ADAM_B2 = 0.999
ADAM_EPS = 1e-08
ADAM_WD = 0.01
ADAM_STEP = 10

V7X_VMEM_BYTES = 64 * 2 ** 20
VMEM_LIMIT = V7X_VMEM_BYTES - 8 * 2 ** 20
LANES = 128

_NT = (((1,), (1,)), ((), ()))
_TN = (((0,), (0,)), ((), ()))


def _dot(a, b):
    return jnp.dot(a, b, preferred_element_type=F32)


def _dot_nt(a, b):
    return lax.dot_general(a, b, _NT, preferred_element_type=F32)


def _dot_tn(a, b):
    return lax.dot_general(a, b, _TN, preferred_element_type=F32)


def _mx(x):
    return x.astype(MXU_DTYPE)


def _sigmoid(x):
    return 1.0 / (1.0 + jnp.exp(-x))


def _params(sem=None):
    return pltpu.CompilerParams(dimension_semantics=sem, vmem_limit_bytes=VMEM_LIMIT)


def _row_tile(s):
    return min(512, s)


def _ln_proj(x, shift, scale1p, g, w_in_b):
    s = x.shape[0]
    ts = _row_tile(s)

    def body(x_ref, sh_ref, sc_ref, g_ref, w_ref, ret_ref, qkv_ref, sg_ref):
        xv = x_ref[...]
        rstd = lax.rsqrt(jnp.mean(xv * xv, axis=-1, keepdims=True) + EPS)
        h = (xv * rstd * g_ref[...]) * sc_ref[...] + sh_ref[...]
        hb = _mx(h)
        for n in range(4):
            ret_ref[:, n * 512:(n + 1) * 512] = _dot(hb, w_ref[:, n * 512:(n + 1) * 512])
        qkv_ref[:, 0:512] = _mx(_dot(hb, w_ref[:, 2048:2560]) * SB_SCALE)
        qkv_ref[:, 512:1024] = _mx(_dot(hb, w_ref[:, 2560:3072]))
        qkv_ref[:, 1024:1536] = _mx(_dot(hb, w_ref[:, 3072:3584]))
        sg_ref[...] = _dot(hb, w_ref[:, 3584:4096])

    vec = pl.BlockSpec((1, D_MODEL), lambda i: (0, 0))
    return pl.pallas_call(
        body, name="ln_proj", grid=(s // ts,),
        in_specs=[pl.BlockSpec((ts, D_MODEL), lambda i: (i, 0)), vec, vec, vec,
                  pl.BlockSpec((D_MODEL, D_IN), lambda i: (0, 0))],
        out_specs=[pl.BlockSpec((ts, 2048), lambda i: (i, 0)),
                   pl.BlockSpec((ts, 1536), lambda i: (i, 0)),
                   pl.BlockSpec((ts, 512), lambda i: (i, 0))],
        out_shape=[jax.ShapeDtypeStruct((s, 2048), F32),
                   jax.ShapeDtypeStruct((s, 1536), MXU_DTYPE),
                   jax.ShapeDtypeStruct((s, 512), F32)],
        compiler_params=_params(("arbitrary",)),
    )(x, shift, scale1p, g, w_in_b)


def _out_proj(x, gate, y_r, y_s, w_out_b):
    s = x.shape[0]
    ts = _row_tile(s)

    def body(x_ref, gate_ref, yr_ref, ys_ref, w_ref, o_ref):
        t = _dot(yr_ref[...], w_ref[0:512, :]) + _dot(ys_ref[...], w_ref[512:1024, :])
        o_ref[...] = x_ref[...] + gate_ref[...] * t

    return pl.pallas_call(
        body, name="out_proj", grid=(s // ts,),
        in_specs=[pl.BlockSpec((ts, D_MODEL), lambda i: (i, 0)),
                  pl.BlockSpec((1, D_MODEL), lambda i: (0, 0)),
                  pl.BlockSpec((ts, 512), lambda i: (i, 0)),
                  pl.BlockSpec((ts, 512), lambda i: (i, 0)),
                  pl.BlockSpec((D_MODEL, D_MODEL), lambda i: (0, 0))],
        out_specs=pl.BlockSpec((ts, D_MODEL), lambda i: (i, 0)),
        out_shape=jax.ShapeDtypeStruct((s, D_MODEL), F32),
        compiler_params=_params(("arbitrary",)),
    )(x, gate, y_r, y_s, w_out_b)


def _final_loss(x, fg, target):
    s = x.shape[0]
    ts = _row_tile(s)

    def body(x_ref, fg_ref, t_ref, dx_ref, loss_ref, dfg_ref):
        i = pl.program_id(0)

        @pl.when(i == 0)
        def _():
            loss_ref[...] = jnp.zeros_like(loss_ref)
            dfg_ref[...] = jnp.zeros_like(dfg_ref)

        xv = x_ref[...]
        fgv = fg_ref[...]
        rstd = lax.rsqrt(jnp.mean(xv * xv, axis=-1, keepdims=True) + EPS)
        xn = xv * rstd
        err = xn * fgv - t_ref[...]
        tok = jnp.mean(err * err, axis=-1, keepdims=True)
        loss_ref[...] += 0.5 * jnp.sum(tok, axis=0, keepdims=True)
        dy = err * (1.0 / D_MODEL)
        dfg_ref[...] += jnp.sum(dy * xn, axis=0, keepdims=True)
        dxn = dy * fgv
        dx_ref[...] = rstd * (dxn - xn * jnp.mean(dxn * xn, axis=-1, keepdims=True))

    return pl.pallas_call(
        body, name="final_loss", grid=(s // ts,),
        in_specs=[pl.BlockSpec((ts, D_MODEL), lambda i: (i, 0)),
                  pl.BlockSpec((1, D_MODEL), lambda i: (0, 0)),
                  pl.BlockSpec((ts, D_MODEL), lambda i: (i, 0))],
        out_specs=[pl.BlockSpec((ts, D_MODEL), lambda i: (i, 0)),
                   pl.BlockSpec((1, 1), lambda i: (0, 0)),
                   pl.BlockSpec((1, D_MODEL), lambda i: (0, 0))],
        out_shape=[jax.ShapeDtypeStruct((s, D_MODEL), F32),
                   jax.ShapeDtypeStruct((1, 1), F32),
                   jax.ShapeDtypeStruct((1, D_MODEL), F32)],
        compiler_params=_params(("arbitrary",)),
    )(x, fg, target)


def _out_proj_bwd(dx_out, gate, y_r, y_s, w_out_b):
    s = dx_out.shape[0]
    ts = _row_tile(s)

    def body(dx_ref, gate_ref, yr_ref, ys_ref, w_ref, dy_ref, dw_ref, dgate_ref):
        i = pl.program_id(0)

        @pl.when(i == 0)
        def _():
            dw_ref[...] = jnp.zeros_like(dw_ref)
            dgate_ref[...] = jnp.zeros_like(dgate_ref)

        dxv = dx_ref[...]
        dt = _mx(dxv * gate_ref[...])
        yr = yr_ref[...]
        ys = ys_ref[...]
        dy_ref[:, 0:512] = _dot_nt(dt, w_ref[0:512, :])
        dy_ref[:, 512:1024] = _dot_nt(dt, w_ref[512:1024, :])
        dw_ref[0:512, :] += _dot_tn(yr, dt)
        dw_ref[512:1024, :] += _dot_tn(ys, dt)
        t = _dot(yr, w_ref[0:512, :]) + _dot(ys, w_ref[512:1024, :])
        dgate_ref[...] += jnp.sum(dxv * t, axis=0, keepdims=True)

    return pl.pallas_call(
        body, name="out_proj_bwd", grid=(s // ts,),
        in_specs=[pl.BlockSpec((ts, D_MODEL), lambda i: (i, 0)),
                  pl.BlockSpec((1, D_MODEL), lambda i: (0, 0)),
                  pl.BlockSpec((ts, 512), lambda i: (i, 0)),
                  pl.BlockSpec((ts, 512), lambda i: (i, 0)),
                  pl.BlockSpec((D_MODEL, D_MODEL), lambda i: (0, 0))],
        out_specs=[pl.BlockSpec((ts, D_MODEL), lambda i: (i, 0)),
                   pl.BlockSpec((D_MODEL, D_MODEL), lambda i: (0, 0)),
                   pl.BlockSpec((1, D_MODEL), lambda i: (0, 0))],
        out_shape=[jax.ShapeDtypeStruct((s, D_MODEL), F32),
                   jax.ShapeDtypeStruct((D_MODEL, D_MODEL), F32),
                   jax.ShapeDtypeStruct((1, D_MODEL), F32)],
        compiler_params=_params(("arbitrary",)),
    )(dx_out, gate, y_r, y_s, w_out_b)


def _in_proj_bwd_x(x, dx_out, dproj, shift, scale1p, g, w_in_b):
    s = x.shape[0]
    ts = min(256, s)

    def body(x_ref, dxo_ref, dp_ref, sc_ref, g_ref, w_ref, dx_ref, dsh_ref, dsc_ref, dg_ref):
        i = pl.program_id(0)

        @pl.when(i == 0)
        def _():
            dsh_ref[...] = jnp.zeros_like(dsh_ref)
            dsc_ref[...] = jnp.zeros_like(dsc_ref)
            dg_ref[...] = jnp.zeros_like(dg_ref)

        dh = _dot_nt(dp_ref[...], w_ref[...])
        xv = x_ref[...]
        gv = g_ref[...]
        scv = sc_ref[...]
        rstd = lax.rsqrt(jnp.mean(xv * xv, axis=-1, keepdims=True) + EPS)
        xn = xv * rstd
        dsh_ref[...] += jnp.sum(dh, axis=0, keepdims=True)
        dsc_ref[...] += jnp.sum(dh * (xn * gv), axis=0, keepdims=True)
        dhs = dh * scv
        dg_ref[...] += jnp.sum(dhs * xn, axis=0, keepdims=True)
        dxn = dhs * gv
        dx_ref[...] = rstd * (dxn - xn * jnp.mean(dxn * xn, axis=-1, keepdims=True)) + dxo_ref[...]

    del shift
    vec = pl.BlockSpec((1, D_MODEL), lambda i: (0, 0))
    return pl.pallas_call(
        body, name="in_proj_bwd_x", grid=(s // ts,),
        in_specs=[pl.BlockSpec((ts, D_MODEL), lambda i: (i, 0)),
                  pl.BlockSpec((ts, D_MODEL), lambda i: (i, 0)),
                  pl.BlockSpec((ts, D_IN), lambda i: (i, 0)),
                  vec, vec,
                  pl.BlockSpec((D_MODEL, D_IN), lambda i: (0, 0))],
        out_specs=[pl.BlockSpec((ts, D_MODEL), lambda i: (i, 0)), vec, vec, vec],
        out_shape=[jax.ShapeDtypeStruct((s, D_MODEL), F32),
                   jax.ShapeDtypeStruct((1, D_MODEL), F32),
                   jax.ShapeDtypeStruct((1, D_MODEL), F32),
                   jax.ShapeDtypeStruct((1, D_MODEL), F32)],
        compiler_params=_params(("arbitrary",)),
    )(x, dx_out, dproj, scale1p, g, w_in_b)


def _in_proj_bwd_w(me, x, dproj, shift, scale1p, g, dwo_parts):
    s = x.shape[0]
    ts = _row_tile(s)
    ns = s // ts
    nb = D_IN // N_DEV
    last = N_DEV - 1

    def body(me_ref, x_ref, dp_ref, sh_ref, sc_ref, g_ref, dwo_ref, rin_ref, rout_ref,
             acc, stage, in_send, in_recv, out_send, out_recv, local_sems):
        del me_ref
        t = pl.program_id(0)
        i = pl.program_id(1)
        px, py, pc = _my_place()
        mine = _linear(px, py, pc)

        def out_copy(r):
            peer = (1 - px if r & 4 else px, 1 - py if r & 2 else py, 1 - pc if r & 1 else pc)
            return pltpu.make_async_remote_copy(
                src_ref=dwo_ref.at[_linear(*peer)], dst_ref=rout_ref.at[mine],
                send_sem=out_send.at[r - 1], recv_sem=out_recv.at[r - 1],
                device_id=peer, device_id_type=MESH_IDS)

        def in_copy(step):
            dest = (mine + 1 + step) % N_DEV
            return pltpu.make_async_remote_copy(
                src_ref=stage.at[step % 2], dst_ref=rin_ref.at[mine],
                send_sem=in_send.at[step], recv_sem=in_recv.at[step],
                device_id=(dest // 4, (dest // 2) % 2, dest % 2), device_id_type=MESH_IDS)

        own_out = pltpu.make_async_copy(dwo_ref.at[mine], rout_ref.at[mine], local_sems.at[0])
        own_in = pltpu.make_async_copy(stage.at[last % 2], rin_ref.at[mine], local_sems.at[1])

        @pl.when(jnp.logical_and(t == 0, i == 0))
        def _():
            own_out.start()
            for r in range(1, N_DEV):
                out_copy(r).start()

        @pl.when(i == 0)
        def _():
            acc[...] = jnp.zeros_like(acc)

        xv = x_ref[...]
        rstd = lax.rsqrt(jnp.mean(xv * xv, axis=-1, keepdims=True) + EPS)
        h = (xv * rstd * g_ref[...]) * sc_ref[...] + sh_ref[...]
        acc[...] += _dot_tn(_mx(h), dp_ref[...])

        @pl.when(i == ns - 1)
        def _():
            @pl.when(t >= 2)
            def _():
                in_copy(t - 2).wait_send()

            stage[t % 2] = acc[...].astype(stage.dtype)

            @pl.when(t < last)
            def _():
                in_copy(t).start()

            @pl.when(t == last)
            def _():
                own_in.start()
                in_copy(last - 1).wait_send()
                for step in range(last):
                    in_copy(step).wait_recv()
                for r in range(1, N_DEV):
                    out_copy(r).wait_recv()
                    out_copy(r).wait_send()
                own_out.wait()
                own_in.wait()

    vec = pl.BlockSpec((1, D_MODEL), lambda t, i, me_ref: (0, 0))
    rows = dwo_parts.shape[1]
    return pl.pallas_call(
        body, name="in_proj_bwd_w",
        grid_spec=pltpu.PrefetchScalarGridSpec(
            num_scalar_prefetch=1, grid=(N_DEV, ns),
            in_specs=[pl.BlockSpec((ts, D_MODEL), lambda t, i, me_ref: (i, 0)),
                      pl.BlockSpec((ts, nb), lambda t, i, me_ref: (i, (me_ref[0] + 1 + t) % N_DEV)),
                      vec, vec, vec, HBM_SPEC],
            out_specs=[HBM_SPEC, HBM_SPEC],
            scratch_shapes=[pltpu.VMEM((D_MODEL, nb), F32),
                            pltpu.VMEM((2, D_MODEL, nb), MXU_DTYPE),
                            pltpu.SemaphoreType.DMA((N_PEERS,)), pltpu.SemaphoreType.DMA((N_PEERS,)),
                            pltpu.SemaphoreType.DMA((N_PEERS,)), pltpu.SemaphoreType.DMA((N_PEERS,)),
                            pltpu.SemaphoreType.DMA((2,))]),
        out_shape=[jax.ShapeDtypeStruct((N_DEV, D_MODEL, nb), MXU_DTYPE),
                   jax.ShapeDtypeStruct((N_DEV, rows, D_MODEL), MXU_DTYPE)],
        compiler_params=_params(("arbitrary", "arbitrary")),
    )(jnp.reshape(me, (1,)).astype(jnp.int32), x, dproj, shift, scale1p, g, dwo_parts)


RET_TILE = 256


def _ret_tables(s):
    t = min(RET_TILE, s)
    half = RET_HEAD_DIM // 2
    pos = jnp.arange(s, dtype=F32)
    inv = ROPE_BASE ** (-jnp.arange(half, dtype=F32) / half)
    ang = pos[:, None] * inv[None, :]
    cos, sin = jnp.cos(ang), jnp.sin(ang)
    cos2 = jnp.concatenate([cos, cos], axis=1)
    sin2 = jnp.concatenate([-sin, sin], axis=1)
    lg = jnp.log1p(-(2.0 ** (-5.0 - jnp.arange(RET_HEADS, dtype=F32))))[:, None, None]
    n = jnp.arange(t)
    dist = (n[:, None] - n[None, :]).astype(F32)[None]
    cn = (n // CHUNK)[:, None]
    cm = (n // CHUNK)[None, :]
    mask = jnp.where((cn == cm)[None], jnp.exp(jnp.abs(dist) * lg),
                     jnp.where((cm < cn)[None], jnp.exp(dist * lg), 0.0))
    nf = n.astype(F32)[None, :, None]
    dq = jnp.broadcast_to(jnp.exp((nf + 1.0) * lg), (RET_HEADS, t, LANES))
    dk = jnp.broadcast_to(jnp.exp((t - 1.0 - nf) * lg), (RET_HEADS, t, LANES))
    gt = jnp.broadcast_to(jnp.exp(float(t) * lg), (RET_HEADS, 1, LANES))
    return cos2, sin2, mask, dq, dk, gt


def _roll_half(x):
    return pltpu.roll(x, RET_HEAD_DIM // 2, 1)


def _ret_head_fwd(q, k, v, cos, sin, m, dq, dk, s0):
    qr = q * cos + _roll_half(q) * sin
    kr = (k * cos + _roll_half(k) * sin) * RET_KSCALE
    qb, kb, vb = _mx(qr), _mx(kr), _mx(v)
    p = _dot_nt(qb, kb) * m
    pb = _mx(p)
    o = _dot(pb, vb) + _dot(qb, _mx(s0)) * dq
    kdb = _mx(kr * dk)
    mu = jnp.mean(o, axis=-1, keepdims=True)
    oc = o - mu
    rstd = lax.rsqrt(jnp.mean(oc * oc, axis=-1, keepdims=True) + EPS)
    return qb, kb, vb, pb, kdb, oc * rstd, rstd


def _retention_fwd(ret, tables):
    cos2, sin2, mask, dq, dk, gt = tables
    s = ret.shape[0]
    t = mask.shape[1]
    nt = s // t
    hd = RET_HEAD_DIM

    def body(ret_ref, cos_ref, sin_ref, m_ref, dq_ref, dk_ref, gt_ref, y_ref, st_ref, s_scr):
        i = pl.program_id(0)

        @pl.when(i == 0)
        def _():
            s_scr[...] = jnp.zeros_like(s_scr)

        cos = cos_ref[...]
        sin = sin_ref[...]
        for h in range(RET_HEADS):
            q = ret_ref[:, h * hd:(h + 1) * hd]
            k = ret_ref[:, 512 + h * hd:512 + (h + 1) * hd]
            v = ret_ref[:, 1024 + h * hd:1024 + (h + 1) * hd]
            g = ret_ref[:, 1536 + h * hd:1536 + (h + 1) * hd]
            s0 = s_scr[h]
            st_ref[h] = s0
            _, _, vb, _, kdb, gn, _ = _ret_head_fwd(q, k, v, cos, sin, m_ref[h], dq_ref[h], dk_ref[h], s0)
            y_ref[:, h * hd:(h + 1) * hd] = (gn * (g * _sigmoid(g))).astype(y_ref.dtype)
            s_scr[h] = s0 * gt_ref[h] + _dot_tn(kdb, vb)

    full3 = lambda a: pl.BlockSpec(a.shape, lambda i: (0, 0, 0))
    return pl.pallas_call(
        body, name="retention_fwd", grid=(nt,),
        in_specs=[pl.BlockSpec((t, 2048), lambda i: (i, 0)),
                  pl.BlockSpec((t, LANES), lambda i: (i, 0)),
                  pl.BlockSpec((t, LANES), lambda i: (i, 0)),
                  full3(mask), full3(dq), full3(dk), full3(gt)],
        out_specs=[pl.BlockSpec((t, 512), lambda i: (i, 0)),
                   pl.BlockSpec((None, RET_HEADS, hd, hd), lambda i: (i, 0, 0, 0))],
        out_shape=[jax.ShapeDtypeStruct((s, 512), MXU_DTYPE),
                   jax.ShapeDtypeStruct((nt, RET_HEADS, hd, hd), F32)],
        scratch_shapes=[pltpu.VMEM((RET_HEADS, hd, hd), F32)],
        compiler_params=_params(("arbitrary",)),
    )(ret, cos2, sin2, mask, dq, dk, gt)


def _retention_bwd(ret, states, dy, tables):
    cos2, sin2, mask, dq, dk, gt = tables
    s = ret.shape[0]
    t = mask.shape[1]
    nt = s // t
    hd = RET_HEAD_DIM

    def body(ret_ref, st_ref, dy_ref, cos_ref, sin_ref, m_ref, dq_ref, dk_ref, gt_ref, d_ref, ds_scr):
        i = pl.program_id(0)

        @pl.when(i == 0)
        def _():
            ds_scr[...] = jnp.zeros_like(ds_scr)

        cos = cos_ref[...]
        sin = sin_ref[...]
        for h in range(RET_HEADS):
            q = ret_ref[:, h * hd:(h + 1) * hd]
            k = ret_ref[:, 512 + h * hd:512 + (h + 1) * hd]
            v = ret_ref[:, 1024 + h * hd:1024 + (h + 1) * hd]
            g = ret_ref[:, 1536 + h * hd:1536 + (h + 1) * hd]
            dyv = dy_ref[:, h * hd:(h + 1) * hd]
            m = m_ref[h]
            dqv = dq_ref[h]
            dkv = dk_ref[h]
            s0 = st_ref[h]
            ds = ds_scr[h]
            qb, kb, vb, pb, kdb, gn, rstd = _ret_head_fwd(q, k, v, cos, sin, m, dqv, dkv, s0)
            sg = _sigmoid(g)
            d_ref[:, 1536 + h * hd:1536 + (h + 1) * hd] = (
                dyv * gn * (sg * (1.0 + g * (1.0 - sg)))).astype(d_ref.dtype)
            dgn = dyv * (g * sg)
            do = rstd * (dgn - jnp.mean(dgn, axis=-1, keepdims=True)
                         - gn * jnp.mean(dgn * gn, axis=-1, keepdims=True))
            dob = _mx(do)
            dodb = _mx(do * dqv)
            dsb = _mx(ds)
            d_ref[:, 1024 + h * hd:1024 + (h + 1) * hd] = (
                _dot_tn(pb, dob) + _dot(kdb, dsb)).astype(d_ref.dtype)
            dpb = _mx(_dot_nt(dob, vb) * m)
            dqr = _dot(dpb, kb) + _dot_nt(dodb, _mx(s0))
            dkr = (_dot_tn(dpb, qb) + _dot_nt(vb, dsb) * dkv) * RET_KSCALE
            d_ref[:, h * hd:(h + 1) * hd] = (dqr * cos + _roll_half(dqr * sin)).astype(d_ref.dtype)
            d_ref[:, 512 + h * hd:512 + (h + 1) * hd] = (
                dkr * cos + _roll_half(dkr * sin)).astype(d_ref.dtype)
            ds_scr[h] = ds * gt_ref[h] + _dot_tn(qb, dodb)

    full3 = lambda a: pl.BlockSpec(a.shape, lambda i: (0, 0, 0))
    rev = lambda i: (nt - 1 - i, 0)
    return pl.pallas_call(
        body, name="retention_bwd", grid=(nt,),
        in_specs=[pl.BlockSpec((t, 2048), rev),
                  pl.BlockSpec((None, RET_HEADS, hd, hd), lambda i: (nt - 1 - i, 0, 0, 0)),
                  pl.BlockSpec((t, 512), rev),
                  pl.BlockSpec((t, LANES), rev),
                  pl.BlockSpec((t, LANES), rev),
                  full3(mask), full3(dq), full3(dk), full3(gt)],
        out_specs=pl.BlockSpec((t, 2048), rev),
        out_shape=jax.ShapeDtypeStruct((s, 2048), MXU_DTYPE),
        scratch_shapes=[pltpu.VMEM((RET_HEADS, hd, hd), F32)],
        compiler_params=_params(("arbitrary",)),
    )(ret, states, dy, cos2, sin2, mask, dq, dk, gt)


SB_BLOCK = 256


SB_SKIP = 104.0


def _split_dots(xs, u):
    parts = []
    for x in xs:
        hi = lax.bitcast_convert_type(lax.bitcast_convert_type(x, jnp.uint32) & jnp.uint32(0xFFFF0000), F32)
        parts += [_mx(hi), _mx(x - hi)]
    out = _dot(jnp.concatenate(parts, axis=0), u)
    n = xs[0].shape[0]
    return [out[2 * k * n:(2 * k + 1) * n] + out[(2 * k + 1) * n:(2 * k + 2) * n] for k in range(len(xs))]


def _split_dot(x, u):
    return _split_dots([x], u)[0]


def _sb_pair_weights(lb, lk, allowed, u_gt):
    blk = lb.shape[0]
    lk_p, lk_d = lk[:, :blk], lk[:, blk:]
    r_d = _rowsum(lk_d)
    cs_p, cs_d = _split_dots([lk_p, lk_d], u_gt)
    a = jnp.exp(lb + jnp.concatenate([cs_p + r_d, cs_d], axis=1))
    return jnp.where(allowed, a, 0.0), r_d, r_d + _rowsum(lk_p)


def _sb_logits(q, k, causal):
    z = _dot_nt(q, k)
    l1p = jnp.log(1.0 + jnp.exp(-jnp.abs(z)))
    lk = -(jnp.maximum(z, 0.0) + l1p)
    if causal is not None:
        lk = jnp.where(causal, lk, 0.0)
    return jnp.minimum(z, 0.0) - l1p, lk


def _sb_weights(lb, lk, r, u_gt, causal):
    a = jnp.exp(lb + _split_dot(lk, u_gt) + r)
    return a if causal is None else jnp.where(causal, a, 0.0)


def _rowsum(x):
    return jnp.sum(x, axis=1, keepdims=True)


def _sb_pair_tile(i, blk):
    row = lax.broadcasted_iota(jnp.int32, (blk, 2 * blk), 0)
    col = lax.broadcasted_iota(jnp.int32, (blk, 2 * blk), 1)
    first_col = jnp.where(i >= 1, 0, blk)
    allowed = jnp.logical_and(row > col - blk, col >= first_col)
    rows_p = pl.ds(pl.multiple_of(jnp.maximum(i - 1, 0) * blk, blk), blk)
    rows_d = pl.ds(pl.multiple_of(i * blk, blk), blk)
    return allowed, rows_p, rows_d


def _sb_fwd(qkv, sg):
    s = qkv.shape[0]
    blk = min(SB_BLOCK, s)
    nq = s // blk
    hd = SB_HEAD_DIM

    def body(q_ref, k_ref, v_ref, g_ref, y_ref, o_ref):
        i = pl.program_id(1)
        row = lax.broadcasted_iota(jnp.int32, (blk, blk), 0)
        col = lax.broadcasted_iota(jnp.int32, (blk, blk), 1)
        u_gt = (row > col).astype(MXU_DTYPE)
        heads = [slice(hh * hd, (hh + 1) * hd) for hh in range(2)]
        qs = [q_ref[:, ls] for ls in heads]
        allowed, rows_p, rows_d = _sb_pair_tile(i, blk)

        def pair(hh):
            ls = heads[hh]
            kc = jnp.concatenate([k_ref[rows_p, ls], k_ref[rows_d, ls]], axis=0)
            vc = jnp.concatenate([v_ref[rows_p, ls], v_ref[rows_d, ls]], axis=0)
            lb, lk = _sb_logits(qs[hh], kc, allowed)
            a, _, r = _sb_pair_weights(lb, lk, allowed, u_gt)
            return _dot(_mx(a), vc), r

        def block(hh, j, r):
            start = pl.multiple_of(j * blk, blk)
            lb, lk = _sb_logits(qs[hh], k_ref[pl.ds(start, blk), heads[hh]], None)
            a = _sb_weights(lb, lk, r, u_gt, None)
            return _dot(_mx(a), v_ref[pl.ds(start, blk), heads[hh]]), r + _rowsum(lk)

        acc0, r0 = pair(0)
        acc1, r1 = pair(1)

        def more(c):
            n, _, r0, _, r1 = c
            return jnp.logical_and(n < i, jnp.max(jnp.maximum(r0, r1)) > -SB_SKIP)

        def step(c):
            n, acc0, r0, acc1, r1 = c
            pv0, r0 = block(0, i - 1 - n, r0)
            pv1, r1 = block(1, i - 1 - n, r1)
            return n + 1, acc0 + pv0, r0, acc1 + pv1, r1

        _, acc0, _, acc1, _ = lax.while_loop(more, step, (jnp.int32(1), acc0, r0, acc1, r1))
        for ls, acc in zip(heads, (acc0, acc1)):
            g = g_ref[:, ls]
            o_ref[:, ls] = acc
            y_ref[:, ls] = (acc * (g * _sigmoid(g))).astype(y_ref.dtype)

    qblk = pl.BlockSpec((blk, LANES), lambda p, i: (i, p))
    return pl.pallas_call(
        body, name="stickbreak_fwd", grid=(SB_HEADS // 2, nq),
        in_specs=[qblk,
                  pl.BlockSpec((s, LANES), lambda p, i: (0, 4 + p)),
                  pl.BlockSpec((s, LANES), lambda p, i: (0, 8 + p)),
                  qblk],
        out_specs=[qblk, qblk],
        out_shape=[jax.ShapeDtypeStruct((s, 512), MXU_DTYPE),
                   jax.ShapeDtypeStruct((s, 512), F32)],
        compiler_params=_params(("arbitrary", "arbitrary")),
    )(qkv, qkv, qkv, sg)


def _sb_bwd(qkv, sg, o, dy):
    s = qkv.shape[0]
    blk = min(SB_BLOCK, s)
    nq = s // blk
    hd = SB_HEAD_DIM
    assert nq <= LANES

    def body(q_ref, k_ref, v_ref, g_ref, o_ref, dy_ref, dq_ref, dk_ref, dv_ref, dg_ref, dk_scr, dv_scr):
        i = pl.program_id(1)

        @pl.when(i == 0)
        def _():
            dk_scr[...] = jnp.zeros_like(dk_scr)
            dv_scr[...] = jnp.zeros_like(dv_scr)

        row = lax.broadcasted_iota(jnp.int32, (blk, blk), 0)
        col = lax.broadcasted_iota(jnp.int32, (blk, blk), 1)
        lane = lax.broadcasted_iota(jnp.int32, (blk, LANES), 1)
        u_gt = (row > col).astype(MXU_DTYPE)
        u_lt = (row < col).astype(MXU_DTYPE)
        heads = [slice(hh * hd, (hh + 1) * hd) for hh in range(2)]
        qs = [q_ref[:, ls] for ls in heads]
        allowed, rows_p, rows_d = _sb_pair_tile(i, blk)
        dobs, kcs, lbs, a_s, ras = [], [], [], [], []
        for hh, ls in enumerate(heads):
            g = g_ref[:, ls]
            dyv = dy_ref[:, ls]
            sgm = _sigmoid(g)
            dg_ref[:, ls] = (dyv * o_ref[:, ls] * (sgm * (1.0 + g * (1.0 - sgm)))).astype(dg_ref.dtype)
            dobs.append(_mx(dyv * (g * sgm)))
            kcs.append(jnp.concatenate([k_ref[rows_p, ls], k_ref[rows_d, ls]], axis=0))
            lb, lk = _sb_logits(qs[hh], kcs[hh], allowed)
            a, _, r = _sb_pair_weights(lb, lk, allowed, u_gt)
            lbs.append(lb)
            a_s.append(a)
            ras.append(r)

        def scan_block(hh, j, r, rmat):
            start = pl.multiple_of(j * blk, blk)
            _, lk = _sb_logits(qs[hh], k_ref[pl.ds(start, blk), heads[hh]], None)
            return r + _rowsum(lk), jnp.where(lane == j, r, rmat)

        def more(c):
            n, r0, _, r1, _ = c
            return jnp.logical_and(n < i, jnp.max(jnp.maximum(r0, r1)) > -SB_SKIP)

        def scan_step(c):
            n, r0, rmat0, r1, rmat1 = c
            r0, rmat0 = scan_block(0, i - 1 - n, r0, rmat0)
            r1, rmat1 = scan_block(1, i - 1 - n, r1, rmat1)
            return n + 1, r0, rmat0, r1, rmat1

        zmat = jnp.zeros((blk, LANES), F32)
        n, _, rmat0, _, rmat1 = lax.while_loop(more, scan_step, (jnp.int32(1), ras[0], zmat, ras[1], zmat))
        rmats = (rmat0, rmat1)

        def block(hh, j, pg):
            ls = heads[hh]
            start = pl.multiple_of(j * blk, blk)
            k = k_ref[pl.ds(start, blk), ls]
            lb, lk = _sb_logits(qs[hh], k, None)
            r = _rowsum(jnp.where(lane == j, rmats[hh], 0.0))
            a = _sb_weights(lb, lk, r, u_gt, None)
            gm = _dot_nt(dobs[hh], v_ref[pl.ds(start, blk), ls]) * a
            dzb = _mx(gm - (gm + (pg + _split_dot(gm, u_lt))) * jnp.exp(lb))
            dk_scr[hh, pl.ds(start, blk), :] += _dot_tn(dzb, qs[hh])
            dv_scr[hh, pl.ds(start, blk), :] += _dot_tn(_mx(a), dobs[hh])
            return _dot(dzb, k), pg + _rowsum(gm)

        def step(t, c):
            acc0, pg0, acc1, pg1 = c
            dq0, pg0 = block(0, i - n + t, pg0)
            dq1, pg1 = block(1, i - n + t, pg1)
            return acc0 + dq0, pg0, acc1 + dq1, pg1

        zero = jnp.zeros((blk, 1), F32)
        zacc = jnp.zeros((blk, hd), F32)
        far = lax.fori_loop(0, n - 1, step, (zacc, zero, zacc, zero))
        for hh, ls in enumerate(heads):
            acc, pg = far[2 * hh], far[2 * hh + 1]
            a = a_s[hh]
            vc = jnp.concatenate([v_ref[rows_p, ls], v_ref[rows_d, ls]], axis=0)
            gm = _dot_nt(dobs[hh], vc) * a
            gm_p, gm_d = gm[:, :blk], gm[:, blk:]
            pre_p, pre_d = _split_dots([gm_p, gm_d], u_lt)
            pre = jnp.concatenate([pre_p + pg, pre_d + (pg + _rowsum(gm_p))], axis=1)
            dz = jnp.where(allowed, gm - (gm + pre) * jnp.exp(lbs[hh]), 0.0)
            dzb = _mx(dz)
            ab = _mx(a)
            dk_scr[hh, rows_p, :] += _dot_tn(dzb[:, :blk], qs[hh])
            dk_scr[hh, rows_d, :] += _dot_tn(dzb[:, blk:], qs[hh])
            dv_scr[hh, rows_p, :] += _dot_tn(ab[:, :blk], dobs[hh])
            dv_scr[hh, rows_d, :] += _dot_tn(ab[:, blk:], dobs[hh])
            dq_ref[:, ls] = ((acc + _dot(dzb, kcs[hh])) * SB_SCALE).astype(dq_ref.dtype)

        @pl.when(i == nq - 1)
        def _():
            for hh in range(2):
                ls = slice(hh * hd, (hh + 1) * hd)
                dk_ref[:, ls] = dk_scr[hh].astype(dk_ref.dtype)
                dv_ref[:, ls] = dv_scr[hh].astype(dv_ref.dtype)

    qblk = lambda c0: pl.BlockSpec((blk, LANES), lambda p, i: (i, c0 + p))
    full = lambda c0: pl.BlockSpec((s, LANES), lambda p, i: (0, c0 + p))
    half = jax.ShapeDtypeStruct((s, 512), MXU_DTYPE)
    return pl.pallas_call(
        body, name="stickbreak_bwd", grid=(SB_HEADS // 2, nq),
        in_specs=[qblk(0), full(4), full(8), qblk(0), qblk(0), qblk(4)],
        out_specs=[qblk(0), full(0), full(0), qblk(0)],
        out_shape=[half, half, half, half],
        scratch_shapes=[pltpu.VMEM((2, s, hd), F32), pltpu.VMEM((2, s, hd), F32)],
        compiler_params=_params(("arbitrary", "arbitrary")),
    )(qkv, qkv, qkv, sg, o, dy)


def _layer_fwd(x, mod, norm_g, w_in_b, w_out_b, tables):
    shift, scale1p, gate = mod[0:1], 1.0 + mod[1:2], mod[2:3]
    ret, qkv, sg = _ln_proj(x, shift, scale1p, norm_g, w_in_b)
    y_r, states = _retention_fwd(ret, tables)
    y_s, o_s = _sb_fwd(qkv, sg)
    x_next = _out_proj(x, gate, y_r, y_s, w_out_b)
    saved = (x, shift, scale1p, gate, ret, qkv, sg, y_r, states, y_s, o_s)
    return x_next, saved


def _layer_bwd(me, dx_out, saved, norm_g, w_in_b, w_out_b, tables):
    x, shift, scale1p, gate, ret, qkv, sg, y_r, states, y_s, o_s = saved
    dy, dw_out, dgate = _out_proj_bwd(dx_out, gate, y_r, y_s, w_out_b)
    d_ret = _retention_bwd(ret, states, dy, tables)
    d_q, d_k, d_v, d_g = _sb_bwd(qkv, sg, o_s, dy)
    dproj = jnp.concatenate([d_ret, d_q, d_k, d_v, d_g], axis=1)
    dx, dshift, dscale, dnorm_g = _in_proj_bwd_x(x, dx_out, dproj, shift, scale1p, norm_g, w_in_b)
    dwo_parts = _mx(dw_out.reshape(N_DEV, D_MODEL // N_DEV, D_MODEL))
    r_in, r_out = _in_proj_bwd_w(me, x, dproj, shift, scale1p, norm_g, dwo_parts)
    dmod = jnp.concatenate([dshift, dscale, dgate], axis=1)
    return dx, r_in, r_out, dmod, dnorm_g


MESH_IDS = pl.DeviceIdType.MESH
N_PEERS = N_DEV - 1
HBM_SPEC = pl.BlockSpec(memory_space=pl.ANY)


def _my_place():
    return lax.axis_index("x"), lax.axis_index("y"), lax.axis_index("c")


def _linear(px, py, pc):
    return 4 * px + 2 * py + pc


def _all_gather(blocks):
    n_arr = len(blocks)

    def body(*refs):
        x_refs = refs[:n_arr]
        out_refs = refs[n_arr:2 * n_arr]
        send_sems, recv_sems, local_sems = refs[2 * n_arr:]
        x, y, c = _my_place()
        me, sibling = (x, y, c), (x, y, 1 - c)
        chips = [(1 - x, y), (x, 1 - y), (1 - x, 1 - y)]

        def rows(a, place):
            m = x_refs[a].shape[0]
            return out_refs[a].at[pl.ds(_linear(*place) * m, m), :]

        def copy(a, k, block, to, src=None):
            return pltpu.make_async_remote_copy(
                src_ref=rows(a, block) if src is None else src, dst_ref=rows(a, block),
                send_sem=send_sems.at[a * N_PEERS + k], recv_sem=recv_sems.at[a * N_PEERS + k],
                device_id=to, device_id_type=MESH_IDS)

        mine = [pltpu.make_async_copy(x_refs[a], rows(a, me), local_sems.at[a]) for a in range(n_arr)]
        for cp in mine:
            cp.start()
        first = []
        for a in range(n_arr):
            first.append(copy(a, 0, me, sibling, src=x_refs[a]))
            first += [copy(a, 1 + j, me, (*chip, c), src=x_refs[a]) for j, chip in enumerate(chips)]
        for cp in first:
            cp.start()
        passed = []
        for j, chip in enumerate(chips):
            for a in range(n_arr):
                copy(a, 1 + j, (*chip, c), me).wait_recv()
                fwd = copy(a, 4 + j, (*chip, c), sibling)
                fwd.start()
                passed.append(fwd)
        for a in range(n_arr):
            copy(a, 0, sibling, me).wait_recv()
            for j, chip in enumerate(chips):
                copy(a, 4 + j, (*chip, 1 - c), me).wait_recv()
        for cp in first + passed:
            cp.wait_send()
        for cp in mine:
            cp.wait()

    return pl.pallas_call(
        body, name="all_gather",
        out_shape=[jax.ShapeDtypeStruct((N_DEV * b.shape[0], b.shape[1]), b.dtype) for b in blocks],
        in_specs=[HBM_SPEC] * n_arr, out_specs=[HBM_SPEC] * n_arr,
        scratch_shapes=[pltpu.SemaphoreType.DMA((n_arr * N_PEERS,)),
                        pltpu.SemaphoreType.DMA((n_arr * N_PEERS,)),
                        pltpu.SemaphoreType.DMA((n_arr,))],
    )(*blocks)


def _ada_fwd(c_all, w_ada, b_cols):
    cols = w_ada.shape[2]

    def body(c_ref, w_ref, b_ref, ca_ref, mod_ref):
        cv = c_ref[...]
        ca = cv * _sigmoid(cv)
        ca_ref[...] = ca
        cb = _mx(jnp.concatenate([ca, ca], axis=0))
        for l in range(DEPTH):
            mod_ref[l * N_DEV:(l + 1) * N_DEV, :] = _dot(cb, _mx(w_ref[l]))[0:N_DEV] + b_ref[l]

    return pl.pallas_call(
        body, name="ada_fwd",
        out_shape=[jax.ShapeDtypeStruct((N_DEV, D_MODEL), F32),
                   jax.ShapeDtypeStruct((DEPTH * N_DEV, cols), F32)],
        compiler_params=_params(),
    )(c_all, w_ada, b_cols)


def _ada_bwd(c_act_t, dmod_cols):
    cols = dmod_cols.shape[2]

    def body(ca_ref, dm_ref, o_ref):
        ca = _mx(ca_ref[...]).astype(F32)
        for l in range(DEPTH):
            o_ref[l] = jnp.dot(ca, _mx(dm_ref[l]).astype(F32),
                               precision=lax.Precision.HIGHEST, preferred_element_type=F32)

    return pl.pallas_call(
        body, name="ada_bwd",
        out_shape=jax.ShapeDtypeStruct((DEPTH, D_MODEL, cols), F32),
        compiler_params=_params(),
    )(c_act_t, dmod_cols)


def _sum_adamw(parts, w, m, v):
    n_slab, rows, cols = parts.shape
    tr = min(256, rows)

    def body(p_ref, w_ref, m_ref, v_ref, g_ref, d_ref, mo_ref, vo_ref):
        g = p_ref[0].astype(F32)
        for sl in range(1, n_slab):
            g = g + p_ref[sl].astype(F32)
        m2 = ADAM_B1 * m_ref[...] + (1.0 - ADAM_B1) * g
        v2 = ADAM_B2 * v_ref[...] + (1.0 - ADAM_B2) * (g * g)
        m_hat = m2 / (1.0 - ADAM_B1 ** ADAM_STEP)
        v_hat = v2 / (1.0 - ADAM_B2 ** ADAM_STEP)
        g_ref[...] = g
        d_ref[...] = -ADAM_LR * (m_hat / (jnp.sqrt(v_hat) + ADAM_EPS) + ADAM_WD * w_ref[...])
        mo_ref[...] = m2
        vo_ref[...] = v2

    blk = pl.BlockSpec((tr, cols), lambda i: (i, 0))
    shp = jax.ShapeDtypeStruct((rows, cols), F32)
    return pl.pallas_call(
        body, name="sum_adamw", grid=(rows // tr,),
        in_specs=[pl.BlockSpec((n_slab, tr, cols), lambda i: (0, i, 0)), blk, blk, blk],
        out_specs=[blk, blk, blk, blk],
        out_shape=[shp, shp, shp, shp],
        compiler_params=_params(("arbitrary",)),
    )(parts, w, m, v)


SMALL_ROWS = 16


def kernel(x, c, norm_g, w_ada, b_ada, w_in, w_out, final_g, loss_target, m_norm_g, m_w_ada, m_b_ada, m_w_in, m_w_out, m_final_g, v_norm_g, v_w_ada, v_b_ada, v_w_in, v_w_out, v_final_g):
    me = _linear(*_my_place())
    in_cols = w_in.shape[2]
    out_rows = w_out.shape[1]
    ada_cols = w_ada.shape[2]

    g_in, g_out, g_c = _all_gather([
        _mx(w_in).reshape(DEPTH * D_MODEL, in_cols),
        _mx(w_out).reshape(DEPTH * out_rows, D_MODEL),
        jnp.broadcast_to(c, (8, D_MODEL))])
    w_in_b = g_in.reshape(N_DEV, DEPTH, D_MODEL, in_cols).transpose(1, 2, 0, 3).reshape(DEPTH, D_MODEL, D_IN)
    w_out_b = g_out.reshape(N_DEV, DEPTH, out_rows, D_MODEL).transpose(1, 0, 2, 3).reshape(DEPTH, D_MODEL, D_MODEL)
    c_all = g_c.reshape(N_DEV, 8, D_MODEL)[:, 0]

    b_cols = lax.dynamic_slice_in_dim(b_ada, me * ada_cols, ada_cols, axis=1)[:, None, :]
    c_act, mod_cols = _ada_fwd(c_all, w_ada, b_cols)
    (g_mod,) = _all_gather([mod_cols])
    g_mod = g_mod.reshape(N_DEV, DEPTH, N_DEV, ada_cols)
    mod = lax.dynamic_index_in_dim(g_mod, me, axis=2, keepdims=False)
    mod = mod.transpose(1, 0, 2).reshape(DEPTH, 3, D_MODEL)

    tables = _ret_tables(x.shape[1])
    h = x[0]
    saved = []
    for l in range(DEPTH):
        h, sv = _layer_fwd(h, mod[l], norm_g[l:l + 1], w_in_b[l], w_out_b[l], tables)
        saved.append(sv)
    dx, loss_part, dfg = _final_loss(h, final_g[None], loss_target[0])
    r_in, r_out, small = [None] * DEPTH, [None] * DEPTH, [None] * DEPTH
    for l in reversed(range(DEPTH)):
        dx, r_in[l], r_out[l], dmod, dng = _layer_bwd(me, dx, saved[l], norm_g[l:l + 1], w_in_b[l], w_out_b[l], tables)
        small[l] = (dmod.reshape(3, D_MODEL), dng)

    pad = jnp.zeros((SMALL_ROWS - 10, D_MODEL), F32)
    small_block = jnp.concatenate([small[0][0], small[1][0], small[0][1], small[1][1], dfg,
                                   jnp.broadcast_to(loss_part, (1, D_MODEL)), pad], axis=0)
    (g_small,) = _all_gather([small_block])
    g_small = g_small.reshape(N_DEV, SMALL_ROWS, D_MODEL)

    def small_pack(b, n, f, fill):
        return jnp.concatenate([b.reshape(6, D_MODEL), n, f[None],
                                jnp.full((SMALL_ROWS - 9, D_MODEL), fill, F32)], axis=0)

    s_g, s_d, s_m, s_v = _sum_adamw(g_small, small_pack(b_ada, norm_g, final_g, 0.0),
                                    small_pack(m_b_ada, m_norm_g, m_final_g, 0.0),
                                    small_pack(v_b_ada, v_norm_g, v_final_g, 1.0))
    loss = s_g[9, 0]

    def small_unpack(a):
        return a[0:6].reshape(DEPTH, 3 * D_MODEL), a[6:8], a[8]

    dmod_all = g_small[:, 0:6].reshape(N_DEV, DEPTH, 3 * D_MODEL).transpose(1, 0, 2)
    dmod_cols = lax.dynamic_slice_in_dim(dmod_all, me * ada_cols, ada_cols, axis=2)
    g_ada = _ada_bwd(c_act.T, dmod_cols).reshape(1, DEPTH * D_MODEL, ada_cols)
    ada = _sum_adamw(g_ada, *[a.reshape(DEPTH * D_MODEL, ada_cols) for a in (w_ada, m_w_ada, v_w_ada)])
    ada = [a.reshape(DEPTH, D_MODEL, ada_cols) for a in ada]

    win = [_sum_adamw(r_in[l], w_in[l], m_w_in[l], v_w_in[l]) for l in range(DEPTH)]
    wout = [_sum_adamw(r_out[l], w_out[l], m_w_out[l], v_w_out[l]) for l in range(DEPTH)]
    win = [jnp.stack([win[0][k], win[1][k]]) for k in range(4)]
    wout = [jnp.stack([wout[0][k], wout[1][k]]) for k in range(4)]

    outs = [loss, dx[None]]
    for k in range(4):
        b, n, f = small_unpack((s_g, s_d, s_m, s_v)[k])
        outs += [n, ada[k], b, win[k], wout[k], f]
    return tuple(outs)
```

```python
import functools

import jax
import jax.numpy as jnp
from jax import lax
from jax.experimental import pallas as pl
from jax.experimental.pallas import tpu as pltpu

F32 = jnp.float32
MXU_DTYPE = jnp.bfloat16

D_MODEL = 1024
DEPTH = 2
N_DEV = 8
CHUNK = 64
D_RET = 512
D_SB = 512
RET_HEADS = 4
RET_HEAD_DIM = 128
SB_HEADS = 8
SB_HEAD_DIM = 64
D_IN = 4096
ROPE_BASE = 10000.0
EPS = 1e-6
SB_SCALE = SB_HEAD_DIM ** -0.5
RET_KSCALE = RET_HEAD_DIM ** -0.5

ADAM_LR = 0.001
ADAM_B1 = 0.9
ADAM_B2 = 0.999
ADAM_EPS = 1e-08
ADAM_WD = 0.01
ADAM_STEP = 10

V7X_VMEM_BYTES = 64 * 2 ** 20
VMEM_LIMIT = V7X_VMEM_BYTES - 8 * 2 ** 20
LANES = 128

_NT = (((1,), (1,)), ((), ()))
_TN = (((0,), (0,)), ((), ()))


def _dot(a, b):
    return jnp.dot(a, b, preferred_element_type=F32)


def _dot_nt(a, b):
    return lax.dot_general(a, b, _NT, preferred_element_type=F32)


def _dot_tn(a, b):
    return lax.dot_general(a, b, _TN, preferred_element_type=F32)


def _mx(x):
    return x.astype(MXU_DTYPE)


def _sigmoid(x):
    return 1.0 / (1.0 + jnp.exp(-x))


def _params(sem=None):
    return pltpu.CompilerParams(dimension_semantics=sem, vmem_limit_bytes=VMEM_LIMIT)


def _row_tile(s):
    return min(512, s)


def _w_in_spec(w_in_g, layer):
    return pl.BlockSpec((N_DEV, None) + w_in_g.shape[2:], lambda i: (0, layer, 0, 0))


def _w_out_spec(w_out_g, layer):
    return pl.BlockSpec((N_DEV, None) + w_out_g.shape[2:], lambda i: (0, layer, 0, 0))


def _ln_proj(x, shift, scale1p, g, w_in_g, layer):
    s = x.shape[0]
    ts = _row_tile(s)

    def body(x_ref, sh_ref, sc_ref, g_ref, w_ref, ret_ref, qkv_ref, sg_ref):
        xv = x_ref[...]
        rstd = lax.rsqrt(jnp.mean(xv * xv, axis=-1, keepdims=True) + EPS)
        h = (xv * rstd * g_ref[...]) * sc_ref[...] + sh_ref[...]
        hb = _mx(h)
        for n in range(4):
            ret_ref[:, n * 512:(n + 1) * 512] = _dot(hb, w_ref[n])
        qkv_ref[:, 0:512] = _mx(_dot(hb, w_ref[4]) * SB_SCALE)
        qkv_ref[:, 512:1024] = _mx(_dot(hb, w_ref[5]))
        qkv_ref[:, 1024:1536] = _mx(_dot(hb, w_ref[6]))
        sg_ref[...] = _dot(hb, w_ref[7])

    vec = pl.BlockSpec((1, D_MODEL), lambda i: (0, 0))
    return pl.pallas_call(
        body, name="ln_proj", grid=(s // ts,),
        in_specs=[pl.BlockSpec((ts, D_MODEL), lambda i: (i, 0)), vec, vec, vec,
                  _w_in_spec(w_in_g, layer)],
        out_specs=[pl.BlockSpec((ts, 2048), lambda i: (i, 0)),
                   pl.BlockSpec((ts, 1536), lambda i: (i, 0)),
                   pl.BlockSpec((ts, 512), lambda i: (i, 0))],
        out_shape=[jax.ShapeDtypeStruct((s, 2048), F32),
                   jax.ShapeDtypeStruct((s, 1536), MXU_DTYPE),
                   jax.ShapeDtypeStruct((s, 512), F32)],
        compiler_params=_params(("arbitrary",)),
    )(x, shift, scale1p, g, w_in_g)


def _w_out_halves(w_ref):
    half = N_DEV // 2
    return (w_ref[0:half].reshape(D_RET, D_MODEL), w_ref[half:N_DEV].reshape(D_SB, D_MODEL))


def _out_proj(x, gate, y_r, y_s, w_out_g, layer):
    s = x.shape[0]
    ts = _row_tile(s)

    def body(x_ref, gate_ref, yr_ref, ys_ref, w_ref, o_ref):
        w_r, w_s = _w_out_halves(w_ref)
        t = _dot(yr_ref[...], w_r) + _dot(ys_ref[...], w_s)
        o_ref[...] = x_ref[...] + gate_ref[...] * t

    return pl.pallas_call(
        body, name="out_proj", grid=(s // ts,),
        in_specs=[pl.BlockSpec((ts, D_MODEL), lambda i: (i, 0)),
                  pl.BlockSpec((1, D_MODEL), lambda i: (0, 0)),
                  pl.BlockSpec((ts, 512), lambda i: (i, 0)),
                  pl.BlockSpec((ts, 512), lambda i: (i, 0)),
                  _w_out_spec(w_out_g, layer)],
        out_specs=pl.BlockSpec((ts, D_MODEL), lambda i: (i, 0)),
        out_shape=jax.ShapeDtypeStruct((s, D_MODEL), F32),
        compiler_params=_params(("arbitrary",)),
    )(x, gate, y_r, y_s, w_out_g)


def _final_loss(x, fg, target):
    s = x.shape[0]
    ts = _row_tile(s)

    def body(x_ref, fg_ref, t_ref, dx_ref, loss_ref, dfg_ref):
        i = pl.program_id(0)

        @pl.when(i == 0)
        def _():
            loss_ref[...] = jnp.zeros_like(loss_ref)
            dfg_ref[...] = jnp.zeros_like(dfg_ref)

        xv = x_ref[...]
        fgv = fg_ref[...]
        rstd = lax.rsqrt(jnp.mean(xv * xv, axis=-1, keepdims=True) + EPS)
        xn = xv * rstd
        err = xn * fgv - t_ref[...]
        tok = jnp.mean(err * err, axis=-1, keepdims=True)
        loss_ref[...] += 0.5 * jnp.sum(tok, axis=0, keepdims=True)
        dy = err * (1.0 / D_MODEL)
        dfg_ref[...] += jnp.sum(dy * xn, axis=0, keepdims=True)
        dxn = dy * fgv
        dx_ref[...] = rstd * (dxn - xn * jnp.mean(dxn * xn, axis=-1, keepdims=True))

    return pl.pallas_call(
        body, name="final_loss", grid=(s // ts,),
        in_specs=[pl.BlockSpec((ts, D_MODEL), lambda i: (i, 0)),
                  pl.BlockSpec((1, D_MODEL), lambda i: (0, 0)),
                  pl.BlockSpec((ts, D_MODEL), lambda i: (i, 0))],
        out_specs=[pl.BlockSpec((ts, D_MODEL), lambda i: (i, 0)),
                   pl.BlockSpec((1, 1), lambda i: (0, 0)),
                   pl.BlockSpec((1, D_MODEL), lambda i: (0, 0))],
        out_shape=[jax.ShapeDtypeStruct((s, D_MODEL), F32),
                   jax.ShapeDtypeStruct((1, 1), F32),
                   jax.ShapeDtypeStruct((1, D_MODEL), F32)],
        compiler_params=_params(("arbitrary",)),
    )(x, fg, target)


def _out_proj_bwd(dx_out, gate, y_r, y_s, w_out_g, layer):
    s = dx_out.shape[0]
    ts = _row_tile(s)

    def body(dx_ref, gate_ref, yr_ref, ys_ref, w_ref, dy_ref, dw_ref, dgate_ref):
        i = pl.program_id(0)

        @pl.when(i == 0)
        def _():
            dw_ref[...] = jnp.zeros_like(dw_ref)
            dgate_ref[...] = jnp.zeros_like(dgate_ref)

        dxv = dx_ref[...]
        dt = _mx(dxv * gate_ref[...])
        yr = yr_ref[...]
        ys = ys_ref[...]
        w_r, w_s = _w_out_halves(w_ref)
        dy_ref[:, 0:512] = _dot_nt(dt, w_r)
        dy_ref[:, 512:1024] = _dot_nt(dt, w_s)
        dw_ref[0:512, :] += _dot_tn(yr, dt)
        dw_ref[512:1024, :] += _dot_tn(ys, dt)
        t = _dot(yr, w_r) + _dot(ys, w_s)
        dgate_ref[...] += jnp.sum(dxv * t, axis=0, keepdims=True)

    return pl.pallas_call(
        body, name="out_proj_bwd", grid=(s // ts,),
        in_specs=[pl.BlockSpec((ts, D_MODEL), lambda i: (i, 0)),
                  pl.BlockSpec((1, D_MODEL), lambda i: (0, 0)),
                  pl.BlockSpec((ts, 512), lambda i: (i, 0)),
                  pl.BlockSpec((ts, 512), lambda i: (i, 0)),
                  _w_out_spec(w_out_g, layer)],
        out_specs=[pl.BlockSpec((ts, D_MODEL), lambda i: (i, 0)),
                   pl.BlockSpec((D_MODEL, D_MODEL), lambda i: (0, 0)),
                   pl.BlockSpec((1, D_MODEL), lambda i: (0, 0))],
        out_shape=[jax.ShapeDtypeStruct((s, D_MODEL), F32),
                   jax.ShapeDtypeStruct((D_MODEL, D_MODEL), F32),
                   jax.ShapeDtypeStruct((1, D_MODEL), F32)],
        compiler_params=_params(("arbitrary",)),
    )(dx_out, gate, y_r, y_s, w_out_g)


def _in_proj_bwd_x(x, dx_out, dproj, shift, scale1p, g, w_in_g, layer):
    s = x.shape[0]
    ts = min(256, s)

    def body(x_ref, dxo_ref, dp_ref, sc_ref, g_ref, w_ref, dx_ref, dsh_ref, dsc_ref, dg_ref):
        i = pl.program_id(0)

        @pl.when(i == 0)
        def _():
            dsh_ref[...] = jnp.zeros_like(dsh_ref)
            dsc_ref[...] = jnp.zeros_like(dsc_ref)
            dg_ref[...] = jnp.zeros_like(dg_ref)

        nb = D_IN // N_DEV
        dh = _dot_nt(dp_ref[:, 0:nb], w_ref[0])
        for n in range(1, N_DEV):
            dh += _dot_nt(dp_ref[:, n * nb:(n + 1) * nb], w_ref[n])
        xv = x_ref[...]
        gv = g_ref[...]
        scv = sc_ref[...]
        rstd = lax.rsqrt(jnp.mean(xv * xv, axis=-1, keepdims=True) + EPS)
        xn = xv * rstd
        dsh_ref[...] += jnp.sum(dh, axis=0, keepdims=True)
        dsc_ref[...] += jnp.sum(dh * (xn * gv), axis=0, keepdims=True)
        dhs = dh * scv
        dg_ref[...] += jnp.sum(dhs * xn, axis=0, keepdims=True)
        dxn = dhs * gv
        dx_ref[...] = rstd * (dxn - xn * jnp.mean(dxn * xn, axis=-1, keepdims=True)) + dxo_ref[...]

    del shift
    vec = pl.BlockSpec((1, D_MODEL), lambda i: (0, 0))
    return pl.pallas_call(
        body, name="in_proj_bwd_x", grid=(s // ts,),
        in_specs=[pl.BlockSpec((ts, D_MODEL), lambda i: (i, 0)),
                  pl.BlockSpec((ts, D_MODEL), lambda i: (i, 0)),
                  pl.BlockSpec((ts, D_IN), lambda i: (i, 0)),
                  vec, vec,
                  _w_in_spec(w_in_g, layer)],
        out_specs=[pl.BlockSpec((ts, D_MODEL), lambda i: (i, 0)), vec, vec, vec],
        out_shape=[jax.ShapeDtypeStruct((s, D_MODEL), F32),
                   jax.ShapeDtypeStruct((1, D_MODEL), F32),
                   jax.ShapeDtypeStruct((1, D_MODEL), F32),
                   jax.ShapeDtypeStruct((1, D_MODEL), F32)],
        compiler_params=_params(("arbitrary",)),
    )(x, dx_out, dproj, scale1p, g, w_in_g)


def _in_proj_bwd_w(me, x, dproj, shift, scale1p, g, dwo_parts):
    s = x.shape[0]
    ts = _row_tile(s)
    ns = s // ts
    nb = D_IN // N_DEV
    last = N_DEV - 1

    def body(me_ref, x_ref, dp_ref, sh_ref, sc_ref, g_ref, dwo_ref, rin_ref, rout_ref,
             acc, stage, h_t, in_send, in_recv, out_send, out_recv, local_sems):
        del me_ref
        t = pl.program_id(0)
        i = pl.program_id(1)
        px, py, pc = _my_place()
        mine = _linear(px, py, pc)

        def out_copy(r):
            peer = (1 - px if r & 4 else px, 1 - py if r & 2 else py, 1 - pc if r & 1 else pc)
            return pltpu.make_async_remote_copy(
                src_ref=dwo_ref.at[_linear(*peer)], dst_ref=rout_ref.at[mine],
                send_sem=out_send.at[r - 1], recv_sem=out_recv.at[r - 1],
                device_id=peer, device_id_type=MESH_IDS)

        def in_copy(step):
            dest = (mine + 1 + step) % N_DEV
            return pltpu.make_async_remote_copy(
                src_ref=stage.at[step % 2], dst_ref=rin_ref.at[mine],
                send_sem=in_send.at[step], recv_sem=in_recv.at[step],
                device_id=(dest // 4, (dest // 2) % 2, dest % 2), device_id_type=MESH_IDS)

        own_out = pltpu.make_async_copy(dwo_ref.at[mine], rout_ref.at[mine], local_sems.at[0])
        own_in = pltpu.make_async_copy(stage.at[last % 2], rin_ref.at[mine], local_sems.at[1])

        @pl.when(jnp.logical_and(t == 0, i == 0))
        def _():
            own_out.start()
            for r in range(1, N_DEV):
                out_copy(r).start()

        @pl.when(i == 0)
        def _():
            acc[...] = jnp.zeros_like(acc)

        @pl.when(t == 0)
        def _():
            xv = x_ref[...]
            rstd = lax.rsqrt(jnp.mean(xv * xv, axis=-1, keepdims=True) + EPS)
            h = (xv * rstd * g_ref[...]) * sc_ref[...] + sh_ref[...]
            h_t[i] = _mx(h.T)

        acc[...] += _dot(h_t[i], dp_ref[...])

        @pl.when(i == ns - 1)
        def _():
            @pl.when(t >= 2)
            def _():
                in_copy(t - 2).wait_send()

            stage[t % 2] = acc[...].astype(stage.dtype)

            @pl.when(t < last)
            def _():
                in_copy(t).start()

            @pl.when(t == last)
            def _():
                own_in.start()
                in_copy(last - 1).wait_send()
                for step in range(last):
                    in_copy(step).wait_recv()
                for r in range(1, N_DEV):
                    out_copy(r).wait_recv()
                    out_copy(r).wait_send()
                own_out.wait()
                own_in.wait()

    vec = pl.BlockSpec((1, D_MODEL), lambda t, i, me_ref: (0, 0))
    rows = dwo_parts.shape[1]
    return pl.pallas_call(
        body, name="in_proj_bwd_w",
        grid_spec=pltpu.PrefetchScalarGridSpec(
            num_scalar_prefetch=1, grid=(N_DEV, ns),
            in_specs=[pl.BlockSpec((ts, D_MODEL), lambda t, i, me_ref: (jnp.where(t == 0, i, ns - 1), 0)),
                      pl.BlockSpec((ts, nb), lambda t, i, me_ref: (i, (me_ref[0] + 1 + t) % N_DEV)),
                      vec, vec, vec, HBM_SPEC],
            out_specs=[HBM_SPEC, HBM_SPEC],
            scratch_shapes=[pltpu.VMEM((D_MODEL, nb), F32),
                            pltpu.VMEM((2, D_MODEL, nb), MXU_DTYPE),
                            pltpu.VMEM((ns, D_MODEL, ts), MXU_DTYPE),
                            pltpu.SemaphoreType.DMA((N_PEERS,)), pltpu.SemaphoreType.DMA((N_PEERS,)),
                            pltpu.SemaphoreType.DMA((N_PEERS,)), pltpu.SemaphoreType.DMA((N_PEERS,)),
                            pltpu.SemaphoreType.DMA((2,))]),
        out_shape=[jax.ShapeDtypeStruct((N_DEV, D_MODEL, nb), MXU_DTYPE),
                   jax.ShapeDtypeStruct((N_DEV, rows, D_MODEL), MXU_DTYPE)],
        compiler_params=_params(("arbitrary", "arbitrary")),
    )(jnp.reshape(me, (1,)).astype(jnp.int32), x, dproj, shift, scale1p, g, dwo_parts)


RET_TILE = 256


def _ret_tables(s):
    t = min(RET_TILE, s)
    half = RET_HEAD_DIM // 2
    pos = jnp.arange(s, dtype=F32)
    inv = ROPE_BASE ** (-jnp.arange(half, dtype=F32) / half)
    ang = pos[:, None] * inv[None, :]
    cos, sin = jnp.cos(ang), jnp.sin(ang)
    cos2 = jnp.concatenate([cos, cos], axis=1)
    sin2 = jnp.concatenate([-sin, sin], axis=1)
    lg = jnp.log1p(-(2.0 ** (-5.0 - jnp.arange(RET_HEADS, dtype=F32))))[:, None, None]
    n = jnp.arange(t)
    dist = (n[:, None] - n[None, :]).astype(F32)[None]
    cn = (n // CHUNK)[:, None]
    cm = (n // CHUNK)[None, :]
    mask = jnp.where((cn == cm)[None], jnp.exp(jnp.abs(dist) * lg),
                     jnp.where((cm < cn)[None], jnp.exp(dist * lg), 0.0))
    nf = n.astype(F32)[None, :, None]
    dq = jnp.broadcast_to(jnp.exp((nf + 1.0) * lg), (RET_HEADS, t, LANES))
    dk = jnp.broadcast_to(jnp.exp((t - 1.0 - nf) * lg), (RET_HEADS, t, LANES))
    gt = jnp.broadcast_to(jnp.exp(float(t) * lg), (RET_HEADS, 1, LANES))
    return cos2, sin2, mask, dq, dk, gt


def _roll_half(x):
    return pltpu.roll(x, RET_HEAD_DIM // 2, 1)


def _ret_head_fwd(q, k, v, cos, sin, m, dq, dk, s0):
    qr = q * cos + _roll_half(q) * sin
    kr = (k * cos + _roll_half(k) * sin) * RET_KSCALE
    qb, kb, vb = _mx(qr), _mx(kr), _mx(v)
    p = _dot_nt(qb, kb) * m
    pb = _mx(p)
    o = _dot(pb, vb) + _dot(qb, _mx(s0)) * dq
    kdb = _mx(kr * dk)
    mu = jnp.mean(o, axis=-1, keepdims=True)
    oc = o - mu
    rstd = lax.rsqrt(jnp.mean(oc * oc, axis=-1, keepdims=True) + EPS)
    return qb, kb, vb, pb, kdb, oc * rstd, rstd


def _retention_fwd(ret, tables):
    cos2, sin2, mask, dq, dk, gt = tables
    s = ret.shape[0]
    t = mask.shape[1]
    nt = s // t
    hd = RET_HEAD_DIM

    def body(ret_ref, cos_ref, sin_ref, m_ref, dq_ref, dk_ref, gt_ref, y_ref, st_ref, s_scr):
        i = pl.program_id(0)

        @pl.when(i == 0)
        def _():
            s_scr[...] = jnp.zeros_like(s_scr)

        cos = cos_ref[...]
        sin = sin_ref[...]
        for h in range(RET_HEADS):
            q = ret_ref[:, h * hd:(h + 1) * hd]
            k = ret_ref[:, 512 + h * hd:512 + (h + 1) * hd]
            v = ret_ref[:, 1024 + h * hd:1024 + (h + 1) * hd]
            g = ret_ref[:, 1536 + h * hd:1536 + (h + 1) * hd]
            s0 = s_scr[h]
            st_ref[h] = s0
            _, _, vb, _, kdb, gn, _ = _ret_head_fwd(q, k, v, cos, sin, m_ref[h], dq_ref[h], dk_ref[h], s0)
            y_ref[:, h * hd:(h + 1) * hd] = (gn * (g * _sigmoid(g))).astype(y_ref.dtype)
            s_scr[h] = s0 * gt_ref[h] + _dot_tn(kdb, vb)

    full3 = lambda a: pl.BlockSpec(a.shape, lambda i: (0, 0, 0))
    return pl.pallas_call(
        body, name="retention_fwd", grid=(nt,),
        in_specs=[pl.BlockSpec((t, 2048), lambda i: (i, 0)),
                  pl.BlockSpec((t, LANES), lambda i: (i, 0)),
                  pl.BlockSpec((t, LANES), lambda i: (i, 0)),
                  full3(mask), full3(dq), full3(dk), full3(gt)],
        out_specs=[pl.BlockSpec((t, 512), lambda i: (i, 0)),
                   pl.BlockSpec((None, RET_HEADS, hd, hd), lambda i: (i, 0, 0, 0))],
        out_shape=[jax.ShapeDtypeStruct((s, 512), MXU_DTYPE),
                   jax.ShapeDtypeStruct((nt, RET_HEADS, hd, hd), F32)],
        scratch_shapes=[pltpu.VMEM((RET_HEADS, hd, hd), F32)],
        compiler_params=_params(("arbitrary",)),
    )(ret, cos2, sin2, mask, dq, dk, gt)


def _retention_bwd(ret, states, dy, tables):
    cos2, sin2, mask, dq, dk, gt = tables
    s = ret.shape[0]
    t = mask.shape[1]
    nt = s // t
    hd = RET_HEAD_DIM

    def body(ret_ref, st_ref, dy_ref, cos_ref, sin_ref, m_ref, dq_ref, dk_ref, gt_ref, d_ref, ds_scr):
        i = pl.program_id(0)

        @pl.when(i == 0)
        def _():
            ds_scr[...] = jnp.zeros_like(ds_scr)

        cos = cos_ref[...]
        sin = sin_ref[...]
        for h in range(RET_HEADS):
            q = ret_ref[:, h * hd:(h + 1) * hd]
            k = ret_ref[:, 512 + h * hd:512 + (h + 1) * hd]
            v = ret_ref[:, 1024 + h * hd:1024 + (h + 1) * hd]
            g = ret_ref[:, 1536 + h * hd:1536 + (h + 1) * hd]
            dyv = dy_ref[:, h * hd:(h + 1) * hd]
            m = m_ref[h]
            dqv = dq_ref[h]
            dkv = dk_ref[h]
            s0 = st_ref[h]
            ds = ds_scr[h]
            qb, kb, vb, pb, kdb, gn, rstd = _ret_head_fwd(q, k, v, cos, sin, m, dqv, dkv, s0)
            sg = _sigmoid(g)
            d_ref[:, 1536 + h * hd:1536 + (h + 1) * hd] = (
                dyv * gn * (sg * (1.0 + g * (1.0 - sg)))).astype(d_ref.dtype)
            dgn = dyv * (g * sg)
            do = rstd * (dgn - jnp.mean(dgn, axis=-1, keepdims=True)
                         - gn * jnp.mean(dgn * gn, axis=-1, keepdims=True))
            dob = _mx(do)
            dodb = _mx(do * dqv)
            dsb = _mx(ds)
            d_ref[:, 1024 + h * hd:1024 + (h + 1) * hd] = (
                _dot_tn(pb, dob) + _dot(kdb, dsb)).astype(d_ref.dtype)
            dpb = _mx(_dot_nt(dob, vb) * m)
            dqr = _dot(dpb, kb) + _dot_nt(dodb, _mx(s0))
            dkr = (_dot_tn(dpb, qb) + _dot_nt(vb, dsb) * dkv) * RET_KSCALE
            d_ref[:, h * hd:(h + 1) * hd] = (dqr * cos + _roll_half(dqr * sin)).astype(d_ref.dtype)
            d_ref[:, 512 + h * hd:512 + (h + 1) * hd] = (
                dkr * cos + _roll_half(dkr * sin)).astype(d_ref.dtype)
            ds_scr[h] = ds * gt_ref[h] + _dot_tn(qb, dodb)

    full3 = lambda a: pl.BlockSpec(a.shape, lambda i: (0, 0, 0))
    rev = lambda i: (nt - 1 - i, 0)
    return pl.pallas_call(
        body, name="retention_bwd", grid=(nt,),
        in_specs=[pl.BlockSpec((t, 2048), rev),
                  pl.BlockSpec((None, RET_HEADS, hd, hd), lambda i: (nt - 1 - i, 0, 0, 0)),
                  pl.BlockSpec((t, 512), rev),
                  pl.BlockSpec((t, LANES), rev),
                  pl.BlockSpec((t, LANES), rev),
                  full3(mask), full3(dq), full3(dk), full3(gt)],
        out_specs=pl.BlockSpec((t, 2048), rev),
        out_shape=jax.ShapeDtypeStruct((s, 2048), MXU_DTYPE),
        scratch_shapes=[pltpu.VMEM((RET_HEADS, hd, hd), F32)],
        compiler_params=_params(("arbitrary",)),
    )(ret, states, dy, cos2, sin2, mask, dq, dk, gt)


SB_BLOCK = 256


SB_SKIP = 104.0


def _split_dots(xs, u):
    parts = []
    for x in xs:
        hi = lax.bitcast_convert_type(lax.bitcast_convert_type(x, jnp.uint32) & jnp.uint32(0xFFFF0000), F32)
        parts += [_mx(hi), _mx(x - hi)]
    out = _dot(jnp.concatenate(parts, axis=0), u)
    n = xs[0].shape[0]
    return [out[2 * k * n:(2 * k + 1) * n] + out[(2 * k + 1) * n:(2 * k + 2) * n] for k in range(len(xs))]


def _split_dot(x, u):
    return _split_dots([x], u)[0]


def _sb_pair_weights(lb, lk, allowed, u_gt):
    blk = lb.shape[0]
    lk_p, lk_d = lk[:, :blk], lk[:, blk:]
    r_d = _rowsum(lk_d)
    cs_p, cs_d = _split_dots([lk_p, lk_d], u_gt)
    a = jnp.exp(lb + jnp.concatenate([cs_p + r_d, cs_d], axis=1))
    return jnp.where(allowed, a, 0.0), r_d, r_d + _rowsum(lk_p)


def _sb_logits(q, k, causal):
    z = _dot_nt(q, k)
    l1p = jnp.log(1.0 + jnp.exp(-jnp.abs(z)))
    lk = -(jnp.maximum(z, 0.0) + l1p)
    if causal is not None:
        lk = jnp.where(causal, lk, 0.0)
    return jnp.minimum(z, 0.0) - l1p, lk


def _sb_weights(lb, lk, r, u_gt, causal):
    a = jnp.exp(lb + _split_dot(lk, u_gt) + r)
    return a if causal is None else jnp.where(causal, a, 0.0)


def _rowsum(x):
    return jnp.sum(x, axis=1, keepdims=True)


def _sb_pair_tile(i, blk):
    row = lax.broadcasted_iota(jnp.int32, (blk, 2 * blk), 0)
    col = lax.broadcasted_iota(jnp.int32, (blk, 2 * blk), 1)
    first_col = jnp.where(i >= 1, 0, blk)
    allowed = jnp.logical_and(row > col - blk, col >= first_col)
    rows_p = pl.ds(pl.multiple_of(jnp.maximum(i - 1, 0) * blk, blk), blk)
    rows_d = pl.ds(pl.multiple_of(i * blk, blk), blk)
    return allowed, rows_p, rows_d


def _sb_fwd(qkv, sg):
    s = qkv.shape[0]
    blk = min(SB_BLOCK, s)
    nq = s // blk
    hd = SB_HEAD_DIM

    def body(q_ref, k_ref, v_ref, g_ref, y_ref, o_ref):
        i = pl.program_id(1)
        row = lax.broadcasted_iota(jnp.int32, (blk, blk), 0)
        col = lax.broadcasted_iota(jnp.int32, (blk, blk), 1)
        u_gt = (row > col).astype(MXU_DTYPE)
        heads = [slice(hh * hd, (hh + 1) * hd) for hh in range(2)]
        qs = [q_ref[:, ls] for ls in heads]
        allowed, rows_p, rows_d = _sb_pair_tile(i, blk)

        def pair(hh):
            ls = heads[hh]
            kc = jnp.concatenate([k_ref[rows_p, ls], k_ref[rows_d, ls]], axis=0)
            vc = jnp.concatenate([v_ref[rows_p, ls], v_ref[rows_d, ls]], axis=0)
            lb, lk = _sb_logits(qs[hh], kc, allowed)
            a, _, r = _sb_pair_weights(lb, lk, allowed, u_gt)
            return _dot(_mx(a), vc), r

        def block(hh, j, r):
            start = pl.multiple_of(j * blk, blk)
            lb, lk = _sb_logits(qs[hh], k_ref[pl.ds(start, blk), heads[hh]], None)
            a = _sb_weights(lb, lk, r, u_gt, None)
            return _dot(_mx(a), v_ref[pl.ds(start, blk), heads[hh]]), r + _rowsum(lk)

        acc0, r0 = pair(0)
        acc1, r1 = pair(1)

        def more(c):
            n, _, r0, _, r1 = c
            return jnp.logical_and(n < i, jnp.max(jnp.maximum(r0, r1)) > -SB_SKIP)

        def step(c):
            n, acc0, r0, acc1, r1 = c
            pv0, r0 = block(0, i - 1 - n, r0)
            pv1, r1 = block(1, i - 1 - n, r1)
            return n + 1, acc0 + pv0, r0, acc1 + pv1, r1

        _, acc0, _, acc1, _ = lax.while_loop(more, step, (jnp.int32(1), acc0, r0, acc1, r1))
        for ls, acc in zip(heads, (acc0, acc1)):
            g = g_ref[:, ls]
            o_ref[:, ls] = acc
            y_ref[:, ls] = (acc * (g * _sigmoid(g))).astype(y_ref.dtype)

    qblk = pl.BlockSpec((blk, LANES), lambda p, i: (i, p))
    return pl.pallas_call(
        body, name="stickbreak_fwd", grid=(SB_HEADS // 2, nq),
        in_specs=[qblk,
                  pl.BlockSpec((s, LANES), lambda p, i: (0, 4 + p)),
                  pl.BlockSpec((s, LANES), lambda p, i: (0, 8 + p)),
                  qblk],
        out_specs=[qblk, qblk],
        out_shape=[jax.ShapeDtypeStruct((s, 512), MXU_DTYPE),
                   jax.ShapeDtypeStruct((s, 512), F32)],
        compiler_params=_params(("arbitrary", "arbitrary")),
    )(qkv, qkv, qkv, sg)


def _sb_bwd(qkv, sg, o, dy):
    s = qkv.shape[0]
    blk = min(SB_BLOCK, s)
    nq = s // blk
    hd = SB_HEAD_DIM
    assert nq <= LANES

    def body(q_ref, k_ref, v_ref, g_ref, o_ref, dy_ref, dq_ref, dk_ref, dv_ref, dg_ref, dk_scr, dv_scr):
        i = pl.program_id(1)

        @pl.when(i == 0)
        def _():
            dk_scr[...] = jnp.zeros_like(dk_scr)
            dv_scr[...] = jnp.zeros_like(dv_scr)

        row = lax.broadcasted_iota(jnp.int32, (blk, blk), 0)
        col = lax.broadcasted_iota(jnp.int32, (blk, blk), 1)
        lane = lax.broadcasted_iota(jnp.int32, (blk, LANES), 1)
        u_gt = (row > col).astype(MXU_DTYPE)
        u_lt = (row < col).astype(MXU_DTYPE)
        heads = [slice(hh * hd, (hh + 1) * hd) for hh in range(2)]
        qs = [q_ref[:, ls] for ls in heads]
        allowed, rows_p, rows_d = _sb_pair_tile(i, blk)
        dobs, kcs, lbs, a_s, ras = [], [], [], [], []
        for hh, ls in enumerate(heads):
            g = g_ref[:, ls]
            dyv = dy_ref[:, ls]
            sgm = _sigmoid(g)
            dg_ref[:, ls] = (dyv * o_ref[:, ls] * (sgm * (1.0 + g * (1.0 - sgm)))).astype(dg_ref.dtype)
            dobs.append(_mx(dyv * (g * sgm)))
            kcs.append(jnp.concatenate([k_ref[rows_p, ls], k_ref[rows_d, ls]], axis=0))
            lb, lk = _sb_logits(qs[hh], kcs[hh], allowed)
            a, _, r = _sb_pair_weights(lb, lk, allowed, u_gt)
            lbs.append(lb)
            a_s.append(a)
            ras.append(r)

        def scan_block(hh, j, r, rmat):
            start = pl.multiple_of(j * blk, blk)
            _, lk = _sb_logits(qs[hh], k_ref[pl.ds(start, blk), heads[hh]], None)
            return r + _rowsum(lk), jnp.where(lane == j, r, rmat)

        def more(c):
            n, r0, _, r1, _ = c
            return jnp.logical_and(n < i, jnp.max(jnp.maximum(r0, r1)) > -SB_SKIP)

        def scan_step(c):
            n, r0, rmat0, r1, rmat1 = c
            r0, rmat0 = scan_block(0, i - 1 - n, r0, rmat0)
            r1, rmat1 = scan_block(1, i - 1 - n, r1, rmat1)
            return n + 1, r0, rmat0, r1, rmat1

        zmat = jnp.zeros((blk, LANES), F32)
        n, _, rmat0, _, rmat1 = lax.while_loop(more, scan_step, (jnp.int32(1), ras[0], zmat, ras[1], zmat))
        rmats = (rmat0, rmat1)

        def block(hh, j, pg):
            ls = heads[hh]
            start = pl.multiple_of(j * blk, blk)
            k = k_ref[pl.ds(start, blk), ls]
            lb, lk = _sb_logits(qs[hh], k, None)
            r = _rowsum(jnp.where(lane == j, rmats[hh], 0.0))
            a = _sb_weights(lb, lk, r, u_gt, None)
            gm = _dot_nt(dobs[hh], v_ref[pl.ds(start, blk), ls]) * a
            dzb = _mx(gm - (gm + (pg + _split_dot(gm, u_lt))) * jnp.exp(lb))
            dk_scr[hh, pl.ds(start, blk), :] += _dot_tn(dzb, qs[hh])
            dv_scr[hh, pl.ds(start, blk), :] += _dot_tn(_mx(a), dobs[hh])
            return _dot(dzb, k), pg + _rowsum(gm)

        def step(t, c):
            acc0, pg0, acc1, pg1 = c
            dq0, pg0 = block(0, i - n + t, pg0)
            dq1, pg1 = block(1, i - n + t, pg1)
            return acc0 + dq0, pg0, acc1 + dq1, pg1

        zero = jnp.zeros((blk, 1), F32)
        zacc = jnp.zeros((blk, hd), F32)
        far = lax.fori_loop(0, n - 1, step, (zacc, zero, zacc, zero))
        for hh, ls in enumerate(heads):
            acc, pg = far[2 * hh], far[2 * hh + 1]
            a = a_s[hh]
            vc = jnp.concatenate([v_ref[rows_p, ls], v_ref[rows_d, ls]], axis=0)
            gm = _dot_nt(dobs[hh], vc) * a
            gm_p, gm_d = gm[:, :blk], gm[:, blk:]
            pre_p, pre_d = _split_dots([gm_p, gm_d], u_lt)
            pre = jnp.concatenate([pre_p + pg, pre_d + (pg + _rowsum(gm_p))], axis=1)
            dz = jnp.where(allowed, gm - (gm + pre) * jnp.exp(lbs[hh]), 0.0)
            dzb = _mx(dz)
            ab = _mx(a)
            dk_scr[hh, rows_p, :] += _dot_tn(dzb[:, :blk], qs[hh])
            dk_scr[hh, rows_d, :] += _dot_tn(dzb[:, blk:], qs[hh])
            dv_scr[hh, rows_p, :] += _dot_tn(ab[:, :blk], dobs[hh])
            dv_scr[hh, rows_d, :] += _dot_tn(ab[:, blk:], dobs[hh])
            dq_ref[:, ls] = ((acc + _dot(dzb, kcs[hh])) * SB_SCALE).astype(dq_ref.dtype)

        @pl.when(i == nq - 1)
        def _():
            for hh in range(2):
                ls = slice(hh * hd, (hh + 1) * hd)
                dk_ref[:, ls] = dk_scr[hh].astype(dk_ref.dtype)
                dv_ref[:, ls] = dv_scr[hh].astype(dv_ref.dtype)

    qblk = lambda c0: pl.BlockSpec((blk, LANES), lambda p, i: (i, c0 + p))
    full = lambda c0: pl.BlockSpec((s, LANES), lambda p, i: (0, c0 + p))
    half = jax.ShapeDtypeStruct((s, 512), MXU_DTYPE)
    return pl.pallas_call(
        body, name="stickbreak_bwd", grid=(SB_HEADS // 2, nq),
        in_specs=[qblk(0), full(4), full(8), qblk(0), qblk(0), qblk(4)],
        out_specs=[qblk(0), full(0), full(0), qblk(0)],
        out_shape=[half, half, half, half],
        scratch_shapes=[pltpu.VMEM((2, s, hd), F32), pltpu.VMEM((2, s, hd), F32)],
        compiler_params=_params(("arbitrary", "arbitrary")),
    )(qkv, qkv, qkv, sg, o, dy)


def _layer_fwd(layer, x, mod, norm_g, w_in_g, w_out_g, tables):
    shift, scale1p, gate = mod[0:1], 1.0 + mod[1:2], mod[2:3]
    ret, qkv, sg = _ln_proj(x, shift, scale1p, norm_g, w_in_g, layer)
    y_r, states = _retention_fwd(ret, tables)
    y_s, o_s = _sb_fwd(qkv, sg)
    x_next = _out_proj(x, gate, y_r, y_s, w_out_g, layer)
    saved = (x, shift, scale1p, gate, ret, qkv, sg, y_r, states, y_s, o_s)
    return x_next, saved


def _layer_bwd(layer, me, dx_out, saved, norm_g, w_in_g, w_out_g, tables):
    x, shift, scale1p, gate, ret, qkv, sg, y_r, states, y_s, o_s = saved
    dy, dw_out, dgate = _out_proj_bwd(dx_out, gate, y_r, y_s, w_out_g, layer)
    d_ret = _retention_bwd(ret, states, dy, tables)
    d_q, d_k, d_v, d_g = _sb_bwd(qkv, sg, o_s, dy)
    dproj = jnp.concatenate([d_ret, d_q, d_k, d_v, d_g], axis=1)
    dx, dshift, dscale, dnorm_g = _in_proj_bwd_x(x, dx_out, dproj, shift, scale1p, norm_g, w_in_g, layer)
    dwo_parts = _mx(dw_out.reshape(N_DEV, D_MODEL // N_DEV, D_MODEL))
    r_in, r_out = _in_proj_bwd_w(me, x, dproj, shift, scale1p, norm_g, dwo_parts)
    dmod = jnp.concatenate([dshift, dscale, dgate], axis=1)
    return dx, r_in, r_out, dmod, dnorm_g


MESH_IDS = pl.DeviceIdType.MESH
N_PEERS = N_DEV - 1
HBM_SPEC = pl.BlockSpec(memory_space=pl.ANY)


def _my_place():
    return lax.axis_index("x"), lax.axis_index("y"), lax.axis_index("c")


def _linear(px, py, pc):
    return 4 * px + 2 * py + pc


def _all_gather(blocks):
    n_arr = len(blocks)

    def body(*refs):
        x_refs = refs[:n_arr]
        out_refs = refs[n_arr:2 * n_arr]
        send_sems, recv_sems, local_sems = refs[2 * n_arr:]
        x, y, c = _my_place()
        me, sibling = (x, y, c), (x, y, 1 - c)
        chips = [(1 - x, y), (x, 1 - y), (1 - x, 1 - y)]

        def rows(a, place):
            m = x_refs[a].shape[0]
            return out_refs[a].at[pl.ds(_linear(*place) * m, m), :]

        def copy(a, k, block, to, src=None):
            return pltpu.make_async_remote_copy(
                src_ref=rows(a, block) if src is None else src, dst_ref=rows(a, block),
                send_sem=send_sems.at[a * N_PEERS + k], recv_sem=recv_sems.at[a * N_PEERS + k],
                device_id=to, device_id_type=MESH_IDS)

        mine = [pltpu.make_async_copy(x_refs[a], rows(a, me), local_sems.at[a]) for a in range(n_arr)]
        for cp in mine:
            cp.start()
        first = []
        for a in range(n_arr):
            first.append(copy(a, 0, me, sibling, src=x_refs[a]))
            first += [copy(a, 1 + j, me, (*chip, c), src=x_refs[a]) for j, chip in enumerate(chips)]
        for cp in first:
            cp.start()
        passed = []
        for j, chip in enumerate(chips):
            for a in range(n_arr):
                copy(a, 1 + j, (*chip, c), me).wait_recv()
                fwd = copy(a, 4 + j, (*chip, c), sibling)
                fwd.start()
                passed.append(fwd)
        for a in range(n_arr):
            copy(a, 0, sibling, me).wait_recv()
            for j, chip in enumerate(chips):
                copy(a, 4 + j, (*chip, 1 - c), me).wait_recv()
        for cp in first + passed:
            cp.wait_send()
        for cp in mine:
            cp.wait()

    return pl.pallas_call(
        body, name="all_gather",
        out_shape=[jax.ShapeDtypeStruct((N_DEV * b.shape[0], b.shape[1]), b.dtype) for b in blocks],
        in_specs=[HBM_SPEC] * n_arr, out_specs=[HBM_SPEC] * n_arr,
        scratch_shapes=[pltpu.SemaphoreType.DMA((n_arr * N_PEERS,)),
                        pltpu.SemaphoreType.DMA((n_arr * N_PEERS,)),
                        pltpu.SemaphoreType.DMA((n_arr,))],
    )(*blocks)


def _ada_fwd(c_all, w_ada, b_cols):
    cols = w_ada.shape[2]

    def body(c_ref, w_ref, b_ref, ca_ref, mod_ref):
        cv = c_ref[...]
        ca = cv * _sigmoid(cv)
        ca_ref[...] = ca
        cb = _mx(jnp.concatenate([ca, ca], axis=0))
        for l in range(DEPTH):
            mod_ref[l * N_DEV:(l + 1) * N_DEV, :] = _dot(cb, _mx(w_ref[l]))[0:N_DEV] + b_ref[l]

    return pl.pallas_call(
        body, name="ada_fwd",
        out_shape=[jax.ShapeDtypeStruct((N_DEV, D_MODEL), F32),
                   jax.ShapeDtypeStruct((DEPTH * N_DEV, cols), F32)],
        compiler_params=_params(),
    )(c_all, w_ada, b_cols)


def _ada_bwd(c_act_t, dmod_cols):
    cols = dmod_cols.shape[2]

    def body(ca_ref, dm_ref, o_ref):
        ca = _mx(ca_ref[...]).astype(F32)
        for l in range(DEPTH):
            o_ref[l] = jnp.dot(ca, _mx(dm_ref[l]).astype(F32),
                               precision=lax.Precision.HIGHEST, preferred_element_type=F32)

    return pl.pallas_call(
        body, name="ada_bwd",
        out_shape=jax.ShapeDtypeStruct((DEPTH, D_MODEL, cols), F32),
        compiler_params=_params(),
    )(c_act_t, dmod_cols)


def _adamw_store(g, w_ref, m_ref, v_ref, g_ref, d_ref, mo_ref, vo_ref):
    m2 = ADAM_B1 * m_ref[...] + (1.0 - ADAM_B1) * g
    v2 = ADAM_B2 * v_ref[...] + (1.0 - ADAM_B2) * (g * g)
    m_hat = m2 / (1.0 - ADAM_B1 ** ADAM_STEP)
    v_hat = v2 / (1.0 - ADAM_B2 ** ADAM_STEP)
    g_ref[...] = g
    d_ref[...] = -ADAM_LR * (m_hat / (jnp.sqrt(v_hat) + ADAM_EPS) + ADAM_WD * w_ref[...])
    mo_ref[...] = m2
    vo_ref[...] = v2


def _slab_sum(p_ref):
    g = p_ref[0].astype(F32)
    for sl in range(1, p_ref.shape[0]):
        g = g + p_ref[sl].astype(F32)
    return g


def _sum_adamw_layers(parts, w, m, v):
    n_slab, rows, cols = parts[0].shape
    tr = min(256, rows)
    nt = rows // tr

    def body(p0_ref, p1_ref, w_ref, m_ref, v_ref, g_ref, d_ref, mo_ref, vo_ref):
        for l, p_ref in enumerate((p0_ref, p1_ref)):
            @pl.when(pl.program_id(0) == l)
            def _():
                _adamw_store(_slab_sum(p_ref), w_ref, m_ref, v_ref, g_ref, d_ref, mo_ref, vo_ref)

    p_specs = [pl.BlockSpec((n_slab, tr, cols), lambda l, i: (0, i * (1 - l) + (nt - 1) * l, 0)),
               pl.BlockSpec((n_slab, tr, cols), lambda l, i: (0, i * l, 0))]
    blk = pl.BlockSpec((None, tr, cols), lambda l, i: (l, i, 0))
    shp = jax.ShapeDtypeStruct((DEPTH, rows, cols), F32)
    return pl.pallas_call(
        body, name="sum_adamw_layers", grid=(DEPTH, nt),
        in_specs=p_specs + [blk, blk, blk],
        out_specs=[blk, blk, blk, blk],
        out_shape=[shp, shp, shp, shp],
        compiler_params=_params(("arbitrary", "arbitrary")),
    )(parts[0], parts[1], w, m, v)


def _sum_adamw(parts, w, m, v):
    n_slab, rows, cols = parts.shape
    tr = min(256, rows)

    def body(p_ref, w_ref, m_ref, v_ref, g_ref, d_ref, mo_ref, vo_ref):
        _adamw_store(_slab_sum(p_ref), w_ref, m_ref, v_ref, g_ref, d_ref, mo_ref, vo_ref)

    blk = pl.BlockSpec((tr, cols), lambda i: (i, 0))
    shp = jax.ShapeDtypeStruct((rows, cols), F32)
    return pl.pallas_call(
        body, name="sum_adamw", grid=(rows // tr,),
        in_specs=[pl.BlockSpec((n_slab, tr, cols), lambda i: (0, i, 0)), blk, blk, blk],
        out_specs=[blk, blk, blk, blk],
        out_shape=[shp, shp, shp, shp],
        compiler_params=_params(("arbitrary",)),
    )(parts, w, m, v)


SMALL_ROWS = 16


def kernel(x, c, norm_g, w_ada, b_ada, w_in, w_out, final_g, loss_target, m_norm_g, m_w_ada, m_b_ada, m_w_in, m_w_out, m_final_g, v_norm_g, v_w_ada, v_b_ada, v_w_in, v_w_out, v_final_g):
    me = _linear(*_my_place())
    in_cols = w_in.shape[2]
    out_rows = w_out.shape[1]
    ada_cols = w_ada.shape[2]

    g_in, g_out, g_c = _all_gather([
        _mx(w_in).reshape(DEPTH * D_MODEL, in_cols),
        _mx(w_out).reshape(DEPTH * out_rows, D_MODEL),
        jnp.broadcast_to(c, (8, D_MODEL))])
    w_in_g = g_in.reshape(N_DEV, DEPTH, D_MODEL, in_cols)
    w_out_g = g_out.reshape(N_DEV, DEPTH, out_rows, D_MODEL)
    c_all = g_c.reshape(N_DEV, 8, D_MODEL)[:, 0]

    b_cols = lax.dynamic_slice_in_dim(b_ada, me * ada_cols, ada_cols, axis=1)[:, None, :]
    c_act, mod_cols = _ada_fwd(c_all, w_ada, b_cols)
    (g_mod,) = _all_gather([mod_cols])
    g_mod = g_mod.reshape(N_DEV, DEPTH, N_DEV, ada_cols)
    mod = lax.dynamic_index_in_dim(g_mod, me, axis=2, keepdims=False)
    mod = mod.transpose(1, 0, 2).reshape(DEPTH, 3, D_MODEL)

    tables = _ret_tables(x.shape[1])
    h = x[0]
    saved = []
    for l in range(DEPTH):
        h, sv = _layer_fwd(l, h, mod[l], norm_g[l:l + 1], w_in_g, w_out_g, tables)
        saved.append(sv)
    dx, loss_part, dfg = _final_loss(h, final_g[None], loss_target[0])
    r_in, r_out, small = [None] * DEPTH, [None] * DEPTH, [None] * DEPTH
    for l in reversed(range(DEPTH)):
        dx, r_in[l], r_out[l], dmod, dng = _layer_bwd(l, me, dx, saved[l], norm_g[l:l + 1], w_in_g, w_out_g, tables)
        small[l] = (dmod.reshape(3, D_MODEL), dng)

    pad = jnp.zeros((SMALL_ROWS - 10, D_MODEL), F32)
    small_block = jnp.concatenate([small[0][0], small[1][0], small[0][1], small[1][1], dfg,
                                   jnp.broadcast_to(loss_part, (1, D_MODEL)), pad], axis=0)
    (g_small,) = _all_gather([small_block])
    g_small = g_small.reshape(N_DEV, SMALL_ROWS, D_MODEL)

    def small_pack(b, n, f, fill):
        return jnp.concatenate([b.reshape(6, D_MODEL), n, f[None],
                                jnp.full((SMALL_ROWS - 9, D_MODEL), fill, F32)], axis=0)

    s_g, s_d, s_m, s_v = _sum_adamw(g_small, small_pack(b_ada, norm_g, final_g, 0.0),
                                    small_pack(m_b_ada, m_norm_g, m_final_g, 0.0),
                                    small_pack(v_b_ada, v_norm_g, v_final_g, 1.0))
    loss = s_g[9, 0]

    def small_unpack(a):
        return a[0:6].reshape(DEPTH, 3 * D_MODEL), a[6:8], a[8]

    dmod_all = g_small[:, 0:6].reshape(N_DEV, DEPTH, 3 * D_MODEL).transpose(1, 0, 2)
    dmod_cols = lax.dynamic_slice_in_dim(dmod_all, me * ada_cols, ada_cols, axis=2)
    g_ada = _ada_bwd(c_act.T, dmod_cols).reshape(1, DEPTH * D_MODEL, ada_cols)
    ada = _sum_adamw(g_ada, *[a.reshape(DEPTH * D_MODEL, ada_cols) for a in (w_ada, m_w_ada, v_w_ada)])
    ada = [a.reshape(DEPTH, D_MODEL, ada_cols) for a in ada]

    win = _sum_adamw_layers(r_in, w_in, m_w_in, v_w_in)
    wout = _sum_adamw_layers(r_out, w_out, m_w_out, v_w_out)

    outs = [loss, dx[None]]
    for k in range(4):
        b, n, f = small_unpack((s_g, s_d, s_m, s_v)[k])
        outs += [n, ada[k], b, win[k], wout[k], f]
    return tuple(outs)
```

```python
import functools

import jax
import jax.numpy as jnp
from jax import lax
from jax.experimental import pallas as pl
from jax.experimental.pallas import tpu as pltpu

F32 = jnp.float32
MXU_DTYPE = jnp.bfloat16

D_MODEL = 1024
DEPTH = 2
N_DEV = 8
CHUNK = 64
D_RET = 512
D_SB = 512
RET_HEADS = 4
RET_HEAD_DIM = 128
SB_HEADS = 8
SB_HEAD_DIM = 64
D_IN = 4096
ROPE_BASE = 10000.0
EPS = 1e-6
SB_SCALE = SB_HEAD_DIM ** -0.5
RET_KSCALE = RET_HEAD_DIM ** -0.5

ADAM_LR = 0.001
ADAM_B1 = 0.9
ADAM_B2 = 0.999
ADAM_EPS = 1e-08
ADAM_WD = 0.01
ADAM_STEP = 10

V7X_VMEM_BYTES = 64 * 2 ** 20
VMEM_LIMIT = V7X_VMEM_BYTES - 8 * 2 ** 20
LANES = 128

_NT = (((1,), (1,)), ((), ()))
_TN = (((0,), (0,)), ((), ()))


def _dot(a, b):
    return jnp.dot(a, b, preferred_element_type=F32)


def _dot_nt(a, b):
    return lax.dot_general(a, b, _NT, preferred_element_type=F32)


def _dot_tn(a, b):
    return lax.dot_general(a, b, _TN, preferred_element_type=F32)


def _mx(x):
    return x.astype(MXU_DTYPE)


def _sigmoid(x):
    return 1.0 / (1.0 + jnp.exp(-x))


def _params(sem=None):
    return pltpu.CompilerParams(dimension_semantics=sem, vmem_limit_bytes=VMEM_LIMIT)


def _row_tile(s):
    return min(512, s)


def _w_in_spec(w_in_g, layer):
    return pl.BlockSpec((N_DEV, None) + w_in_g.shape[2:], lambda i: (0, layer, 0, 0))


def _w_out_spec(w_out_g, layer):
    return pl.BlockSpec((N_DEV, None) + w_out_g.shape[2:], lambda i: (0, layer, 0, 0))


def _ln_proj(x, shift, scale1p, g, w_in_g, layer):
    s = x.shape[0]
    ts = _row_tile(s)

    def body(x_ref, sh_ref, sc_ref, g_ref, w_ref, ret_ref, qkv_ref, sg_ref):
        xv = x_ref[...]
        rstd = lax.rsqrt(jnp.mean(xv * xv, axis=-1, keepdims=True) + EPS)
        h = (xv * rstd * g_ref[...]) * sc_ref[...] + sh_ref[...]
        hb = _mx(h)
        for n in range(4):
            ret_ref[:, n * 512:(n + 1) * 512] = _dot(hb, w_ref[n])
        qkv_ref[:, 0:512] = _mx(_dot(hb, w_ref[4]) * SB_SCALE)
        qkv_ref[:, 512:1024] = _mx(_dot(hb, w_ref[5]))
        qkv_ref[:, 1024:1536] = _mx(_dot(hb, w_ref[6]))
        sg_ref[...] = _dot(hb, w_ref[7])

    vec = pl.BlockSpec((1, D_MODEL), lambda i: (0, 0))
    return pl.pallas_call(
        body, name="ln_proj", grid=(s // ts,),
        in_specs=[pl.BlockSpec((ts, D_MODEL), lambda i: (i, 0)), vec, vec, vec,
                  _w_in_spec(w_in_g, layer)],
        out_specs=[pl.BlockSpec((ts, 2048), lambda i: (i, 0)),
                   pl.BlockSpec((ts, 1536), lambda i: (i, 0)),
                   pl.BlockSpec((ts, 512), lambda i: (i, 0))],
        out_shape=[jax.ShapeDtypeStruct((s, 2048), F32),
                   jax.ShapeDtypeStruct((s, 1536), MXU_DTYPE),
                   jax.ShapeDtypeStruct((s, 512), F32)],
        compiler_params=_params(("arbitrary",)),
    )(x, shift, scale1p, g, w_in_g)


def _w_out_halves(w_ref):
    half = N_DEV // 2
    return (w_ref[0:half].reshape(D_RET, D_MODEL), w_ref[half:N_DEV].reshape(D_SB, D_MODEL))


def _out_proj(x, gate, y_r, y_s, w_out_g, layer):
    s = x.shape[0]
    ts = _row_tile(s)

    def body(x_ref, gate_ref, yr_ref, ys_ref, w_ref, o_ref):
        w_r, w_s = _w_out_halves(w_ref)
        t = _dot(yr_ref[...], w_r) + _dot(ys_ref[...], w_s)
        o_ref[...] = x_ref[...] + gate_ref[...] * t

    return pl.pallas_call(
        body, name="out_proj", grid=(s // ts,),
        in_specs=[pl.BlockSpec((ts, D_MODEL), lambda i: (i, 0)),
                  pl.BlockSpec((1, D_MODEL), lambda i: (0, 0)),
                  pl.BlockSpec((ts, 512), lambda i: (i, 0)),
                  pl.BlockSpec((ts, 512), lambda i: (i, 0)),
                  _w_out_spec(w_out_g, layer)],
        out_specs=pl.BlockSpec((ts, D_MODEL), lambda i: (i, 0)),
        out_shape=jax.ShapeDtypeStruct((s, D_MODEL), F32),
        compiler_params=_params(("arbitrary",)),
    )(x, gate, y_r, y_s, w_out_g)


def _final_loss(x, fg, target):
    s = x.shape[0]
    ts = _row_tile(s)

    def body(x_ref, fg_ref, t_ref, dx_ref, loss_ref, dfg_ref):
        i = pl.program_id(0)

        @pl.when(i == 0)
        def _():
            loss_ref[...] = jnp.zeros_like(loss_ref)
            dfg_ref[...] = jnp.zeros_like(dfg_ref)

        xv = x_ref[...]
        fgv = fg_ref[...]
        rstd = lax.rsqrt(jnp.mean(xv * xv, axis=-1, keepdims=True) + EPS)
        xn = xv * rstd
        err = xn * fgv - t_ref[...]
        tok = jnp.mean(err * err, axis=-1, keepdims=True)
        loss_ref[...] += 0.5 * jnp.sum(tok, axis=0, keepdims=True)
        dy = err * (1.0 / D_MODEL)
        dfg_ref[...] += jnp.sum(dy * xn, axis=0, keepdims=True)
        dxn = dy * fgv
        dx_ref[...] = rstd * (dxn - xn * jnp.mean(dxn * xn, axis=-1, keepdims=True))

    return pl.pallas_call(
        body, name="final_loss", grid=(s // ts,),
        in_specs=[pl.BlockSpec((ts, D_MODEL), lambda i: (i, 0)),
                  pl.BlockSpec((1, D_MODEL), lambda i: (0, 0)),
                  pl.BlockSpec((ts, D_MODEL), lambda i: (i, 0))],
        out_specs=[pl.BlockSpec((ts, D_MODEL), lambda i: (i, 0)),
                   pl.BlockSpec((1, 1), lambda i: (0, 0)),
                   pl.BlockSpec((1, D_MODEL), lambda i: (0, 0))],
        out_shape=[jax.ShapeDtypeStruct((s, D_MODEL), F32),
                   jax.ShapeDtypeStruct((1, 1), F32),
                   jax.ShapeDtypeStruct((1, D_MODEL), F32)],
        compiler_params=_params(("arbitrary",)),
    )(x, fg, target)


def _out_proj_bwd(dx_out, gate, y_r, y_s, w_out_g, layer):
    s = dx_out.shape[0]
    ts = _row_tile(s)

    def body(dx_ref, gate_ref, yr_ref, ys_ref, w_ref, dy_ref, dw_ref, dgate_ref):
        i = pl.program_id(0)

        @pl.when(i == 0)
        def _():
            dw_ref[...] = jnp.zeros_like(dw_ref)
            dgate_ref[...] = jnp.zeros_like(dgate_ref)

        dxv = dx_ref[...]
        dt = _mx(dxv * gate_ref[...])
        yr = yr_ref[...]
        ys = ys_ref[...]
        w_r, w_s = _w_out_halves(w_ref)
        dy_ref[:, 0:512] = _dot_nt(dt, w_r)
        dy_ref[:, 512:1024] = _dot_nt(dt, w_s)
        dw_ref[0:512, :] += _dot_tn(yr, dt)
        dw_ref[512:1024, :] += _dot_tn(ys, dt)
        t = _dot(yr, w_r) + _dot(ys, w_s)
        dgate_ref[...] += jnp.sum(dxv * t, axis=0, keepdims=True)

    return pl.pallas_call(
        body, name="out_proj_bwd", grid=(s // ts,),
        in_specs=[pl.BlockSpec((ts, D_MODEL), lambda i: (i, 0)),
                  pl.BlockSpec((1, D_MODEL), lambda i: (0, 0)),
                  pl.BlockSpec((ts, 512), lambda i: (i, 0)),
                  pl.BlockSpec((ts, 512), lambda i: (i, 0)),
                  _w_out_spec(w_out_g, layer)],
        out_specs=[pl.BlockSpec((ts, D_MODEL), lambda i: (i, 0)),
                   pl.BlockSpec((D_MODEL, D_MODEL), lambda i: (0, 0)),
                   pl.BlockSpec((1, D_MODEL), lambda i: (0, 0))],
        out_shape=[jax.ShapeDtypeStruct((s, D_MODEL), F32),
                   jax.ShapeDtypeStruct((D_MODEL, D_MODEL), F32),
                   jax.ShapeDtypeStruct((1, D_MODEL), F32)],
        compiler_params=_params(("arbitrary",)),
    )(dx_out, gate, y_r, y_s, w_out_g)


def _in_proj_bwd_x(x, dx_out, dproj, shift, scale1p, g, w_in_g, layer):
    s = x.shape[0]
    ts = min(256, s)

    def body(x_ref, dxo_ref, dp_ref, sc_ref, g_ref, w_ref, dx_ref, dsh_ref, dsc_ref, dg_ref):
        i = pl.program_id(0)

        @pl.when(i == 0)
        def _():
            dsh_ref[...] = jnp.zeros_like(dsh_ref)
            dsc_ref[...] = jnp.zeros_like(dsc_ref)
            dg_ref[...] = jnp.zeros_like(dg_ref)

        nb = D_IN // N_DEV
        dh = _dot_nt(dp_ref[:, 0:nb], w_ref[0])
        for n in range(1, N_DEV):
            dh += _dot_nt(dp_ref[:, n * nb:(n + 1) * nb], w_ref[n])
        xv = x_ref[...]
        gv = g_ref[...]
        scv = sc_ref[...]
        rstd = lax.rsqrt(jnp.mean(xv * xv, axis=-1, keepdims=True) + EPS)
        xn = xv * rstd
        dsh_ref[...] += jnp.sum(dh, axis=0, keepdims=True)
        dsc_ref[...] += jnp.sum(dh * (xn * gv), axis=0, keepdims=True)
        dhs = dh * scv
        dg_ref[...] += jnp.sum(dhs * xn, axis=0, keepdims=True)
        dxn = dhs * gv
        dx_ref[...] = rstd * (dxn - xn * jnp.mean(dxn * xn, axis=-1, keepdims=True)) + dxo_ref[...]

    del shift
    vec = pl.BlockSpec((1, D_MODEL), lambda i: (0, 0))
    return pl.pallas_call(
        body, name="in_proj_bwd_x", grid=(s // ts,),
        in_specs=[pl.BlockSpec((ts, D_MODEL), lambda i: (i, 0)),
                  pl.BlockSpec((ts, D_MODEL), lambda i: (i, 0)),
                  pl.BlockSpec((ts, D_IN), lambda i: (i, 0)),
                  vec, vec,
                  _w_in_spec(w_in_g, layer)],
        out_specs=[pl.BlockSpec((ts, D_MODEL), lambda i: (i, 0)), vec, vec, vec],
        out_shape=[jax.ShapeDtypeStruct((s, D_MODEL), F32),
                   jax.ShapeDtypeStruct((1, D_MODEL), F32),
                   jax.ShapeDtypeStruct((1, D_MODEL), F32),
                   jax.ShapeDtypeStruct((1, D_MODEL), F32)],
        compiler_params=_params(("arbitrary",)),
    )(x, dx_out, dproj, scale1p, g, w_in_g)


def _in_proj_bwd_w(me, x, dproj, shift, scale1p, g, dwo_parts):
    s = x.shape[0]
    ts = _row_tile(s)
    ns = s // ts
    nb = D_IN // N_DEV
    last = N_DEV - 1

    def body(me_ref, x_ref, dp_ref, sh_ref, sc_ref, g_ref, dwo_ref, rin_ref, rout_ref,
             acc, stage, h_t, in_send, in_recv, out_send, out_recv, local_sems):
        del me_ref
        t = pl.program_id(0)
        i = pl.program_id(1)
        px, py, pc = _my_place()
        mine = _linear(px, py, pc)

        def out_copy(r):
            peer = (1 - px if r & 4 else px, 1 - py if r & 2 else py, 1 - pc if r & 1 else pc)
            return pltpu.make_async_remote_copy(
                src_ref=dwo_ref.at[_linear(*peer)], dst_ref=rout_ref.at[mine],
                send_sem=out_send.at[r - 1], recv_sem=out_recv.at[r - 1],
                device_id=peer, device_id_type=MESH_IDS)

        def in_copy(step):
            dest = (mine + 1 + step) % N_DEV
            return pltpu.make_async_remote_copy(
                src_ref=stage.at[step % 2], dst_ref=rin_ref.at[mine],
                send_sem=in_send.at[step], recv_sem=in_recv.at[step],
                device_id=(dest // 4, (dest // 2) % 2, dest % 2), device_id_type=MESH_IDS)

        own_out = pltpu.make_async_copy(dwo_ref.at[mine], rout_ref.at[mine], local_sems.at[0])
        own_in = pltpu.make_async_copy(stage.at[last % 2], rin_ref.at[mine], local_sems.at[1])

        @pl.when(jnp.logical_and(t == 0, i == 0))
        def _():
            own_out.start()
            for r in range(1, N_DEV):
                out_copy(r).start()

        @pl.when(i == 0)
        def _():
            acc[...] = jnp.zeros_like(acc)

        @pl.when(t == 0)
        def _():
            xv = x_ref[...]
            rstd = lax.rsqrt(jnp.mean(xv * xv, axis=-1, keepdims=True) + EPS)
            h = (xv * rstd * g_ref[...]) * sc_ref[...] + sh_ref[...]
            h_t[i] = _mx(h.T)

        acc[...] += _dot(h_t[i], dp_ref[...])

        @pl.when(i == ns - 1)
        def _():
            @pl.when(t >= 2)
            def _():
                in_copy(t - 2).wait_send()

            stage[t % 2] = acc[...].astype(stage.dtype)

            @pl.when(t < last)
            def _():
                in_copy(t).start()

            @pl.when(t == last)
            def _():
                own_in.start()
                in_copy(last - 1).wait_send()
                for step in range(last):
                    in_copy(step).wait_recv()
                for r in range(1, N_DEV):
                    out_copy(r).wait_recv()
                    out_copy(r).wait_send()
                own_out.wait()
                own_in.wait()

    vec = pl.BlockSpec((1, D_MODEL), lambda t, i, me_ref: (0, 0))
    rows = dwo_parts.shape[1]
    return pl.pallas_call(
        body, name="in_proj_bwd_w",
        grid_spec=pltpu.PrefetchScalarGridSpec(
            num_scalar_prefetch=1, grid=(N_DEV, ns),
            in_specs=[pl.BlockSpec((ts, D_MODEL), lambda t, i, me_ref: (jnp.where(t == 0, i, ns - 1), 0)),
                      pl.BlockSpec((ts, nb), lambda t, i, me_ref: (i, (me_ref[0] + 1 + t) % N_DEV)),
                      vec, vec, vec, HBM_SPEC],
            out_specs=[HBM_SPEC, HBM_SPEC],
            scratch_shapes=[pltpu.VMEM((D_MODEL, nb), F32),
                            pltpu.VMEM((2, D_MODEL, nb), MXU_DTYPE),
                            pltpu.VMEM((ns, D_MODEL, ts), MXU_DTYPE),
                            pltpu.SemaphoreType.DMA((N_PEERS,)), pltpu.SemaphoreType.DMA((N_PEERS,)),
                            pltpu.SemaphoreType.DMA((N_PEERS,)), pltpu.SemaphoreType.DMA((N_PEERS,)),
                            pltpu.SemaphoreType.DMA((2,))]),
        out_shape=[jax.ShapeDtypeStruct((N_DEV, D_MODEL, nb), MXU_DTYPE),
                   jax.ShapeDtypeStruct((N_DEV, rows, D_MODEL), MXU_DTYPE)],
        compiler_params=_params(("arbitrary", "arbitrary")),
    )(jnp.reshape(me, (1,)).astype(jnp.int32), x, dproj, shift, scale1p, g, dwo_parts)


RET_TILE = 256


def _ret_tables(s):
    t = min(RET_TILE, s)
    half = RET_HEAD_DIM // 2
    pos = jnp.arange(s, dtype=F32)
    inv = ROPE_BASE ** (-jnp.arange(half, dtype=F32) / half)
    ang = pos[:, None] * inv[None, :]
    cos, sin = jnp.cos(ang), jnp.sin(ang)
    cos2 = jnp.concatenate([cos, cos], axis=1)
    sin2 = jnp.concatenate([-sin, sin], axis=1)
    lg = jnp.log1p(-(2.0 ** (-5.0 - jnp.arange(RET_HEADS, dtype=F32))))[:, None, None]
    n = jnp.arange(t)
    dist = (n[:, None] - n[None, :]).astype(F32)[None]
    cn = (n // CHUNK)[:, None]
    cm = (n // CHUNK)[None, :]
    mask = jnp.where((cn == cm)[None], jnp.exp(jnp.abs(dist) * lg),
                     jnp.where((cm < cn)[None], jnp.exp(dist * lg), 0.0))
    nf = n.astype(F32)[None, :, None]
    dq = jnp.broadcast_to(jnp.exp((nf + 1.0) * lg), (RET_HEADS, t, LANES))
    dk = jnp.broadcast_to(jnp.exp((t - 1.0 - nf) * lg), (RET_HEADS, t, LANES))
    gt = jnp.broadcast_to(jnp.exp(float(t) * lg), (RET_HEADS, 1, LANES))
    return cos2, sin2, mask, dq, dk, gt


def _roll_half(x):
    return pltpu.roll(x, RET_HEAD_DIM // 2, 1)


def _ret_head_fwd(q, k, v, cos, sin, m, dq, dk, s0):
    qr = q * cos + _roll_half(q) * sin
    kr = (k * cos + _roll_half(k) * sin) * RET_KSCALE
    qb, kb, vb = _mx(qr), _mx(kr), _mx(v)
    p = _dot_nt(qb, kb) * m
    pb = _mx(p)
    o = _dot(pb, vb) + _dot(qb, _mx(s0)) * dq
    kdb = _mx(kr * dk)
    mu = jnp.mean(o, axis=-1, keepdims=True)
    oc = o - mu
    rstd = lax.rsqrt(jnp.mean(oc * oc, axis=-1, keepdims=True) + EPS)
    return qb, kb, vb, pb, kdb, oc * rstd, rstd


def _retention_fwd(ret, tables):
    cos2, sin2, mask, dq, dk, gt = tables
    s = ret.shape[0]
    t = mask.shape[1]
    nt = s // t
    hd = RET_HEAD_DIM

    def body(ret_ref, cos_ref, sin_ref, m_ref, dq_ref, dk_ref, gt_ref, y_ref, st_ref, s_scr):
        i = pl.program_id(0)

        @pl.when(i == 0)
        def _():
            s_scr[...] = jnp.zeros_like(s_scr)

        cos = cos_ref[...]
        sin = sin_ref[...]
        for h in range(RET_HEADS):
            q = ret_ref[:, h * hd:(h + 1) * hd]
            k = ret_ref[:, 512 + h * hd:512 + (h + 1) * hd]
            v = ret_ref[:, 1024 + h * hd:1024 + (h + 1) * hd]
            g = ret_ref[:, 1536 + h * hd:1536 + (h + 1) * hd]
            s0 = s_scr[h]
            st_ref[h] = s0
            _, _, vb, _, kdb, gn, _ = _ret_head_fwd(q, k, v, cos, sin, m_ref[h], dq_ref[h], dk_ref[h], s0)
            y_ref[:, h * hd:(h + 1) * hd] = (gn * (g * _sigmoid(g))).astype(y_ref.dtype)
            s_scr[h] = s0 * gt_ref[h] + _dot_tn(kdb, vb)

    full3 = lambda a: pl.BlockSpec(a.shape, lambda i: (0, 0, 0))
    return pl.pallas_call(
        body, name="retention_fwd", grid=(nt,),
        in_specs=[pl.BlockSpec((t, 2048), lambda i: (i, 0)),
                  pl.BlockSpec((t, LANES), lambda i: (i, 0)),
                  pl.BlockSpec((t, LANES), lambda i: (i, 0)),
                  full3(mask), full3(dq), full3(dk), full3(gt)],
        out_specs=[pl.BlockSpec((t, 512), lambda i: (i, 0)),
                   pl.BlockSpec((None, RET_HEADS, hd, hd), lambda i: (i, 0, 0, 0))],
        out_shape=[jax.ShapeDtypeStruct((s, 512), MXU_DTYPE),
                   jax.ShapeDtypeStruct((nt, RET_HEADS, hd, hd), F32)],
        scratch_shapes=[pltpu.VMEM((RET_HEADS, hd, hd), F32)],
        compiler_params=_params(("arbitrary",)),
    )(ret, cos2, sin2, mask, dq, dk, gt)


def _retention_bwd(ret, states, dy, tables):
    cos2, sin2, mask, dq, dk, gt = tables
    s = ret.shape[0]
    t = mask.shape[1]
    nt = s // t
    hd = RET_HEAD_DIM

    def body(ret_ref, st_ref, dy_ref, cos_ref, sin_ref, m_ref, dq_ref, dk_ref, gt_ref, d_ref, ds_scr):
        i = pl.program_id(0)

        @pl.when(i == 0)
        def _():
            ds_scr[...] = jnp.zeros_like(ds_scr)

        cos = cos_ref[...]
        sin = sin_ref[...]
        for h in range(RET_HEADS):
            q = ret_ref[:, h * hd:(h + 1) * hd]
            k = ret_ref[:, 512 + h * hd:512 + (h + 1) * hd]
            v = ret_ref[:, 1024 + h * hd:1024 + (h + 1) * hd]
            g = ret_ref[:, 1536 + h * hd:1536 + (h + 1) * hd]
            dyv = dy_ref[:, h * hd:(h + 1) * hd]
            m = m_ref[h]
            dqv = dq_ref[h]
            dkv = dk_ref[h]
            s0 = st_ref[h]
            ds = ds_scr[h]
            qb, kb, vb, pb, kdb, gn, rstd = _ret_head_fwd(q, k, v, cos, sin, m, dqv, dkv, s0)
            sg = _sigmoid(g)
            d_ref[:, 1536 + h * hd:1536 + (h + 1) * hd] = (
                dyv * gn * (sg * (1.0 + g * (1.0 - sg)))).astype(d_ref.dtype)
            dgn = dyv * (g * sg)
            do = rstd * (dgn - jnp.mean(dgn, axis=-1, keepdims=True)
                         - gn * jnp.mean(dgn * gn, axis=-1, keepdims=True))
            dob = _mx(do)
            dodb = _mx(do * dqv)
            dsb = _mx(ds)
            d_ref[:, 1024 + h * hd:1024 + (h + 1) * hd] = (
                _dot_tn(pb, dob) + _dot(kdb, dsb)).astype(d_ref.dtype)
            dpb = _mx(_dot_nt(dob, vb) * m)
            dqr = _dot(dpb, kb) + _dot_nt(dodb, _mx(s0))
            dkr = (_dot_tn(dpb, qb) + _dot_nt(vb, dsb) * dkv) * RET_KSCALE
            d_ref[:, h * hd:(h + 1) * hd] = (dqr * cos + _roll_half(dqr * sin)).astype(d_ref.dtype)
            d_ref[:, 512 + h * hd:512 + (h + 1) * hd] = (
                dkr * cos + _roll_half(dkr * sin)).astype(d_ref.dtype)
            ds_scr[h] = ds * gt_ref[h] + _dot_tn(qb, dodb)

    full3 = lambda a: pl.BlockSpec(a.shape, lambda i: (0, 0, 0))
    rev = lambda i: (nt - 1 - i, 0)
    return pl.pallas_call(
        body, name="retention_bwd", grid=(nt,),
        in_specs=[pl.BlockSpec((t, 2048), rev),
                  pl.BlockSpec((None, RET_HEADS, hd, hd), lambda i: (nt - 1 - i, 0, 0, 0)),
                  pl.BlockSpec((t, 512), rev),
                  pl.BlockSpec((t, LANES), rev),
                  pl.BlockSpec((t, LANES), rev),
                  full3(mask), full3(dq), full3(dk), full3(gt)],
        out_specs=pl.BlockSpec((t, 2048), rev),
        out_shape=jax.ShapeDtypeStruct((s, 2048), MXU_DTYPE),
        scratch_shapes=[pltpu.VMEM((RET_HEADS, hd, hd), F32)],
        compiler_params=_params(("arbitrary",)),
    )(ret, states, dy, cos2, sin2, mask, dq, dk, gt)


SB_BLOCK = 256
SB_CHUNK = 32


SB_SKIP = 104.0


def _split_dots(xs, u):
    parts = []
    for x in xs:
        hi = lax.bitcast_convert_type(lax.bitcast_convert_type(x, jnp.uint32) & jnp.uint32(0xFFFF0000), F32)
        parts += [_mx(hi), _mx(x - hi)]
    out = _dot(jnp.concatenate(parts, axis=0), u)
    n = xs[0].shape[0]
    return [out[2 * k * n:(2 * k + 1) * n] + out[(2 * k + 1) * n:(2 * k + 2) * n] for k in range(len(xs))]


def _split_dot(x, u):
    return _split_dots([x], u)[0]


def _sb_pair_weights(lb, lk, allowed, u_gt):
    blk = lb.shape[0]
    lk_p, lk_d = lk[:, :blk], lk[:, blk:]
    r_d = _rowsum(lk_d)
    cs_p, cs_d = _split_dots([lk_p, lk_d], u_gt)
    a = jnp.exp(lb + jnp.concatenate([cs_p + r_d, cs_d], axis=1))
    return jnp.where(allowed, a, 0.0), r_d, r_d + _rowsum(lk_p)


def _sb_logits(q, k, causal):
    z = _dot_nt(q, k)
    l1p = jnp.log(1.0 + jnp.exp(-jnp.abs(z)))
    lk = -(jnp.maximum(z, 0.0) + l1p)
    if causal is not None:
        lk = jnp.where(causal, lk, 0.0)
    return jnp.minimum(z, 0.0) - l1p, lk


def _sb_weights(lb, lk, r, u_gt, causal):
    a = jnp.exp(lb + _split_dot(lk, u_gt) + r)
    return a if causal is None else jnp.where(causal, a, 0.0)


def _rowsum(x):
    return jnp.sum(x, axis=1, keepdims=True)


def _sb_pair_tile(i, blk):
    row = lax.broadcasted_iota(jnp.int32, (blk, 2 * blk), 0)
    col = lax.broadcasted_iota(jnp.int32, (blk, 2 * blk), 1)
    first_col = jnp.where(i >= 1, 0, blk)
    allowed = jnp.logical_and(row > col - blk, col >= first_col)
    rows_p = pl.ds(pl.multiple_of(jnp.maximum(i - 1, 0) * blk, blk), blk)
    rows_d = pl.ds(pl.multiple_of(i * blk, blk), blk)
    return allowed, rows_p, rows_d


def _sb_fwd(qkv, sg):
    s = qkv.shape[0]
    blk = min(SB_BLOCK, s)
    nq = s // blk
    hd = SB_HEAD_DIM

    ch = min(SB_CHUNK, blk)

    def body(q_ref, k_ref, v_ref, g_ref, y_ref, o_ref, z_scr, lhs_scr, cs_scr, a_scr):
        i = pl.program_id(1)
        row = lax.broadcasted_iota(jnp.int32, (blk, blk), 0)
        col = lax.broadcasted_iota(jnp.int32, (blk, blk), 1)
        u_gt = (row > col).astype(MXU_DTYPE)
        heads = [slice(hh * hd, (hh + 1) * hd) for hh in range(2)]
        qs = [q_ref[:, ls] for ls in heads]
        _, rows_p, rows_d = _sb_pair_tile(i, blk)
        has_prev = i >= 1
        crow = lax.broadcasted_iota(jnp.int32, (ch, blk), 0)
        ccol = lax.broadcasted_iota(jnp.int32, (ch, blk), 1)

        def logits(hh):
            kc = jnp.concatenate([k_ref[rows_p, heads[hh]], k_ref[rows_d, heads[hh]]], axis=0)
            z_scr[hh] = _dot_nt(qs[hh], kc)

        def keep_parts(hh):
            r_d, r_all = [], []
            for c in range(blk // ch):
                rows = pl.ds(c * ch, ch)
                causal = crow + c * ch > ccol
                z = z_scr[hh, rows, :]
                l1p = jnp.log(1.0 + jnp.exp(-jnp.abs(z)))
                lk = -(jnp.maximum(z, 0.0) + l1p)
                z_scr[hh, rows, :] = jnp.minimum(z, 0.0) - l1p
                lk_p = jnp.where(has_prev, lk[:, :blk], 0.0)
                lk_d = jnp.where(causal, lk[:, blk:], 0.0)
                for part, x in enumerate((lk_p, lk_d)):
                    hi = lax.bitcast_convert_type(
                        lax.bitcast_convert_type(x, jnp.uint32) & jnp.uint32(0xFFFF0000), F32)
                    lhs_scr[hh, pl.ds((2 * part) * blk + c * ch, ch), :] = _mx(hi)
                    lhs_scr[hh, pl.ds((2 * part + 1) * blk + c * ch, ch), :] = _mx(x - hi)
                r_d.append(_rowsum(lk_d))
                r_all.append(r_d[c] + _rowsum(lk_p))
            return r_d, jnp.concatenate(r_all, axis=0)

        def suffix_sums(hh):
            cs_scr[hh] = _dot(lhs_scr[hh], u_gt)

        def weights(hh, r_d):
            for c in range(blk // ch):
                rows = pl.ds(c * ch, ch)
                causal = crow + c * ch > ccol
                cs_p = cs_scr[hh, pl.ds(c * ch, ch), :] + cs_scr[hh, pl.ds(blk + c * ch, ch), :] + r_d[c]
                cs_d = cs_scr[hh, pl.ds(2 * blk + c * ch, ch), :] + cs_scr[hh, pl.ds(3 * blk + c * ch, ch), :]
                lb = z_scr[hh, rows, :]
                a_p = jnp.where(has_prev, jnp.exp(lb[:, :blk] + cs_p), 0.0)
                a_d = jnp.where(causal, jnp.exp(lb[:, blk:] + cs_d), 0.0)
                a_scr[hh, rows, :] = _mx(jnp.concatenate([a_p, a_d], axis=1))

        def values(hh):
            vc = jnp.concatenate([v_ref[rows_p, heads[hh]], v_ref[rows_d, heads[hh]]], axis=0)
            return _dot(a_scr[hh], vc)

        def block(hh, j, r):
            start = pl.multiple_of(j * blk, blk)
            lb, lk = _sb_logits(qs[hh], k_ref[pl.ds(start, blk), heads[hh]], None)
            a = _sb_weights(lb, lk, r, u_gt, None)
            return _dot(_mx(a), v_ref[pl.ds(start, blk), heads[hh]]), r + _rowsum(lk)

        def more(n, r0, r1):
            return jnp.logical_and(n < i, jnp.max(jnp.maximum(r0, r1)) > -SB_SKIP)

        logits(0)
        logits(1)
        rd0, r0 = keep_parts(0)
        suffix_sums(0)
        rd1, r1 = keep_parts(1)
        suffix_sums(1)
        go = more(jnp.int32(1), r0, r1)
        gates = [g_ref[:, ls] * _sigmoid(g_ref[:, ls]) for ls in heads]
        weights(0, rd0)
        acc0 = values(0)
        weights(1, rd1)
        acc1 = values(1)

        def step(c):
            _, n, acc0, r0, acc1, r1 = c
            pv0, r0 = block(0, i - 1 - n, r0)
            pv1, r1 = block(1, i - 1 - n, r1)
            return more(n + 1, r0, r1), n + 1, acc0 + pv0, r0, acc1 + pv1, r1

        _, _, acc0, _, acc1, _ = lax.while_loop(lambda c: c[0], step, (go, jnp.int32(1), acc0, r0, acc1, r1))
        for ls, acc, gate in zip(heads, (acc0, acc1), gates):
            o_ref[:, ls] = acc
            y_ref[:, ls] = (acc * gate).astype(y_ref.dtype)

    qblk = pl.BlockSpec((blk, LANES), lambda p, i: (i, p))
    return pl.pallas_call(
        body, name="stickbreak_fwd", grid=(SB_HEADS // 2, nq),
        in_specs=[qblk,
                  pl.BlockSpec((s, LANES), lambda p, i: (0, 4 + p)),
                  pl.BlockSpec((s, LANES), lambda p, i: (0, 8 + p)),
                  qblk],
        out_specs=[qblk, qblk],
        out_shape=[jax.ShapeDtypeStruct((s, 512), MXU_DTYPE),
                   jax.ShapeDtypeStruct((s, 512), F32)],
        scratch_shapes=[pltpu.VMEM((2, blk, 2 * blk), F32),
                        pltpu.VMEM((2, 4 * blk, blk), MXU_DTYPE),
                        pltpu.VMEM((2, 4 * blk, blk), F32),
                        pltpu.VMEM((2, blk, 2 * blk), MXU_DTYPE)],
        compiler_params=_params(("arbitrary", "arbitrary")),
    )(qkv, qkv, qkv, sg)


def _sb_bwd(qkv, sg, o, dy):
    s = qkv.shape[0]
    blk = min(SB_BLOCK, s)
    nq = s // blk
    hd = SB_HEAD_DIM
    assert nq <= LANES
    ch = min(SB_CHUNK, blk)

    def body(q_ref, k_ref, v_ref, g_ref, o_ref, dy_ref, dq_ref, dk_ref, dv_ref, dg_ref, dk_scr, dv_scr,
             z_scr, g_scr, lhs_scr, cs_scr, a_scr, dz_scr):
        i = pl.program_id(1)

        @pl.when(i == 0)
        def _():
            dk_scr[...] = jnp.zeros_like(dk_scr)
            dv_scr[...] = jnp.zeros_like(dv_scr)

        row = lax.broadcasted_iota(jnp.int32, (blk, blk), 0)
        col = lax.broadcasted_iota(jnp.int32, (blk, blk), 1)
        lane = lax.broadcasted_iota(jnp.int32, (blk, LANES), 1)
        u_gt = (row > col).astype(MXU_DTYPE)
        u_lt = (row < col).astype(MXU_DTYPE)
        heads = [slice(hh * hd, (hh + 1) * hd) for hh in range(2)]
        qs = [q_ref[:, ls] for ls in heads]
        _, rows_p, rows_d = _sb_pair_tile(i, blk)
        has_prev = i >= 1
        crow = lax.broadcasted_iota(jnp.int32, (ch, blk), 0)
        ccol = lax.broadcasted_iota(jnp.int32, (ch, blk), 1)
        nch = blk // ch
        dobs, kcs = [], []
        for hh, ls in enumerate(heads):
            g = g_ref[:, ls]
            dyv = dy_ref[:, ls]
            sgm = _sigmoid(g)
            dg_ref[:, ls] = (dyv * o_ref[:, ls] * (sgm * (1.0 + g * (1.0 - sgm)))).astype(dg_ref.dtype)
            dobs.append(_mx(dyv * (g * sgm)))
            kcs.append(jnp.concatenate([k_ref[rows_p, ls], k_ref[rows_d, ls]], axis=0))

        def split_rows(hh, c, part, x):
            hi = lax.bitcast_convert_type(lax.bitcast_convert_type(x, jnp.uint32) & jnp.uint32(0xFFFF0000), F32)
            lhs_scr[hh, pl.ds((2 * part) * blk + c * ch, ch), :] = _mx(hi)
            lhs_scr[hh, pl.ds((2 * part + 1) * blk + c * ch, ch), :] = _mx(x - hi)

        def summed_rows(hh, c, part):
            return (cs_scr[hh, pl.ds((2 * part) * blk + c * ch, ch), :]
                    + cs_scr[hh, pl.ds((2 * part + 1) * blk + c * ch, ch), :])

        def logits(hh):
            z_scr[hh] = _dot_nt(qs[hh], kcs[hh])
            vc = jnp.concatenate([v_ref[rows_p, heads[hh]], v_ref[rows_d, heads[hh]]], axis=0)
            g_scr[hh] = _dot_nt(dobs[hh], vc)

        def keep_parts(hh):
            r_d, r_all = [], []
            for c in range(nch):
                rows = pl.ds(c * ch, ch)
                z = z_scr[hh, rows, :]
                l1p = jnp.log(1.0 + jnp.exp(-jnp.abs(z)))
                lk = -(jnp.maximum(z, 0.0) + l1p)
                z_scr[hh, rows, :] = jnp.minimum(z, 0.0) - l1p
                lk_p = jnp.where(has_prev, lk[:, :blk], 0.0)
                lk_d = jnp.where(crow + c * ch > ccol, lk[:, blk:], 0.0)
                split_rows(hh, c, 0, lk_p)
                split_rows(hh, c, 1, lk_d)
                r_d.append(_rowsum(lk_d))
                r_all.append(r_d[c] + _rowsum(lk_p))
            return r_d, jnp.concatenate(r_all, axis=0)

        def weights(hh, r_d):
            g_p = []
            for c in range(nch):
                rows = pl.ds(c * ch, ch)
                lb = z_scr[hh, rows, :]
                a_p = jnp.where(has_prev, jnp.exp(lb[:, :blk] + (summed_rows(hh, c, 0) + r_d[c])), 0.0)
                a_d = jnp.where(crow + c * ch > ccol, jnp.exp(lb[:, blk:] + summed_rows(hh, c, 1)), 0.0)
                a = jnp.concatenate([a_p, a_d], axis=1)
                a_scr[hh, rows, :] = _mx(a)
                gm = g_scr[hh, rows, :] * a
                g_scr[hh, rows, :] = gm
                split_rows(hh, c, 0, gm[:, :blk])
                split_rows(hh, c, 1, gm[:, blk:])
                g_p.append(_rowsum(gm[:, :blk]))
            return g_p

        def logit_grads(hh, pg, g_p):
            for c in range(nch):
                rows = pl.ds(c * ch, ch)
                pre = jnp.concatenate([summed_rows(hh, c, 0) + pg[c * ch:(c + 1) * ch],
                                       summed_rows(hh, c, 1) + (pg[c * ch:(c + 1) * ch] + g_p[c])], axis=1)
                gm = g_scr[hh, rows, :]
                dz = gm - (gm + pre) * jnp.exp(z_scr[hh, rows, :])
                dz_p = jnp.where(has_prev, dz[:, :blk], 0.0)
                dz_d = jnp.where(crow + c * ch > ccol, dz[:, blk:], 0.0)
                dz_scr[hh, rows, :] = _mx(jnp.concatenate([dz_p, dz_d], axis=1))

        def products(hh, acc):
            ls = heads[hh]
            dk_scr[hh, rows_p, :] += _dot_tn(dz_scr[hh, :, 0:blk], qs[hh])
            dk_scr[hh, rows_d, :] += _dot_tn(dz_scr[hh, :, blk:2 * blk], qs[hh])
            dv_scr[hh, rows_p, :] += _dot_tn(a_scr[hh, :, 0:blk], dobs[hh])
            dv_scr[hh, rows_d, :] += _dot_tn(a_scr[hh, :, blk:2 * blk], dobs[hh])
            dq_ref[:, ls] = ((acc + _dot(dz_scr[hh], kcs[hh])) * SB_SCALE).astype(dq_ref.dtype)

        def suffix_sums(hh):
            cs_scr[hh] = _dot(lhs_scr[hh], u_gt)

        def prefix_sums(hh):
            cs_scr[hh] = _dot(lhs_scr[hh], u_lt)

        def more(n, r0, r1):
            return jnp.logical_and(n < i, jnp.max(jnp.maximum(r0, r1)) > -SB_SKIP)

        logits(0)
        logits(1)
        rd0, ra0 = keep_parts(0)
        suffix_sums(0)
        rd1, ra1 = keep_parts(1)
        suffix_sums(1)
        go = more(jnp.int32(1), ra0, ra1)
        gp0 = weights(0, rd0)
        prefix_sums(0)
        gp1 = weights(1, rd1)
        prefix_sums(1)

        def scan_block(hh, j, r, rmat):
            start = pl.multiple_of(j * blk, blk)
            _, lk = _sb_logits(qs[hh], k_ref[pl.ds(start, blk), heads[hh]], None)
            return r + _rowsum(lk), jnp.where(lane == j, r, rmat)

        def scan_step(c):
            _, n, r0, rmat0, r1, rmat1 = c
            r0, rmat0 = scan_block(0, i - 1 - n, r0, rmat0)
            r1, rmat1 = scan_block(1, i - 1 - n, r1, rmat1)
            return more(n + 1, r0, r1), n + 1, r0, rmat0, r1, rmat1

        zmat = jnp.zeros((blk, LANES), F32)
        _, n, _, rmat0, _, rmat1 = lax.while_loop(lambda c: c[0], scan_step,
                                                  (go, jnp.int32(1), ra0, zmat, ra1, zmat))
        rmats = (rmat0, rmat1)

        def block(hh, j, pg):
            ls = heads[hh]
            start = pl.multiple_of(j * blk, blk)
            k = k_ref[pl.ds(start, blk), ls]
            lb, lk = _sb_logits(qs[hh], k, None)
            r = _rowsum(jnp.where(lane == j, rmats[hh], 0.0))
            a = _sb_weights(lb, lk, r, u_gt, None)
            gm = _dot_nt(dobs[hh], v_ref[pl.ds(start, blk), ls]) * a
            dzb = _mx(gm - (gm + (pg + _split_dot(gm, u_lt))) * jnp.exp(lb))
            dk_scr[hh, pl.ds(start, blk), :] += _dot_tn(dzb, qs[hh])
            dv_scr[hh, pl.ds(start, blk), :] += _dot_tn(_mx(a), dobs[hh])
            return _dot(dzb, k), pg + _rowsum(gm)

        def step(t, c):
            acc0, pg0, acc1, pg1 = c
            dq0, pg0 = block(0, i - n + t, pg0)
            dq1, pg1 = block(1, i - n + t, pg1)
            return acc0 + dq0, pg0, acc1 + dq1, pg1

        zero = jnp.zeros((blk, 1), F32)
        zacc = jnp.zeros((blk, hd), F32)
        acc0, pg0, acc1, pg1 = lax.fori_loop(0, n - 1, step, (zacc, zero, zacc, zero))
        logit_grads(0, pg0, gp0)
        logit_grads(1, pg1, gp1)
        products(0, acc0)
        products(1, acc1)

        @pl.when(i == nq - 1)
        def _():
            for hh in range(2):
                ls = slice(hh * hd, (hh + 1) * hd)
                dk_ref[:, ls] = dk_scr[hh].astype(dk_ref.dtype)
                dv_ref[:, ls] = dv_scr[hh].astype(dv_ref.dtype)

    qblk = lambda c0: pl.BlockSpec((blk, LANES), lambda p, i: (i, c0 + p))
    full = lambda c0: pl.BlockSpec((s, LANES), lambda p, i: (0, c0 + p))
    half = jax.ShapeDtypeStruct((s, 512), MXU_DTYPE)
    return pl.pallas_call(
        body, name="stickbreak_bwd", grid=(SB_HEADS // 2, nq),
        in_specs=[qblk(0), full(4), full(8), qblk(0), qblk(0), qblk(4)],
        out_specs=[qblk(0), full(0), full(0), qblk(0)],
        out_shape=[half, half, half, half],
        scratch_shapes=[pltpu.VMEM((2, s, hd), F32), pltpu.VMEM((2, s, hd), F32),
                        pltpu.VMEM((2, blk, 2 * blk), F32),
                        pltpu.VMEM((2, blk, 2 * blk), F32),
                        pltpu.VMEM((2, 4 * blk, blk), MXU_DTYPE),
                        pltpu.VMEM((2, 4 * blk, blk), F32),
                        pltpu.VMEM((2, blk, 2 * blk), MXU_DTYPE),
                        pltpu.VMEM((2, blk, 2 * blk), MXU_DTYPE)],
        compiler_params=_params(("arbitrary", "arbitrary")),
    )(qkv, qkv, qkv, sg, o, dy)


def _layer_fwd(layer, x, mod, norm_g, w_in_g, w_out_g, tables):
    shift, scale1p, gate = mod[0:1], 1.0 + mod[1:2], mod[2:3]
    ret, qkv, sg = _ln_proj(x, shift, scale1p, norm_g, w_in_g, layer)
    y_r, states = _retention_fwd(ret, tables)
    y_s, o_s = _sb_fwd(qkv, sg)
    x_next = _out_proj(x, gate, y_r, y_s, w_out_g, layer)
    saved = (x, shift, scale1p, gate, ret, qkv, sg, y_r, states, y_s, o_s)
    return x_next, saved


def _layer_bwd(layer, me, dx_out, saved, norm_g, w_in_g, w_out_g, tables):
    x, shift, scale1p, gate, ret, qkv, sg, y_r, states, y_s, o_s = saved
    dy, dw_out, dgate = _out_proj_bwd(dx_out, gate, y_r, y_s, w_out_g, layer)
    d_ret = _retention_bwd(ret, states, dy, tables)
    d_q, d_k, d_v, d_g = _sb_bwd(qkv, sg, o_s, dy)
    dproj = jnp.concatenate([d_ret, d_q, d_k, d_v, d_g], axis=1)
    dx, dshift, dscale, dnorm_g = _in_proj_bwd_x(x, dx_out, dproj, shift, scale1p, norm_g, w_in_g, layer)
    dwo_parts = _mx(dw_out.reshape(N_DEV, D_MODEL // N_DEV, D_MODEL))
    r_in, r_out = _in_proj_bwd_w(me, x, dproj, shift, scale1p, norm_g, dwo_parts)
    dmod = jnp.concatenate([dshift, dscale, dgate], axis=1)
    return dx, r_in, r_out, dmod, dnorm_g


MESH_IDS = pl.DeviceIdType.MESH
N_PEERS = N_DEV - 1
HBM_SPEC = pl.BlockSpec(memory_space=pl.ANY)


def _my_place():
    return lax.axis_index("x"), lax.axis_index("y"), lax.axis_index("c")


def _linear(px, py, pc):
    return 4 * px + 2 * py + pc


def _all_gather(blocks):
    n_arr = len(blocks)

    def body(*refs):
        x_refs = refs[:n_arr]
        out_refs = refs[n_arr:2 * n_arr]
        send_sems, recv_sems, local_sems = refs[2 * n_arr:]
        x, y, c = _my_place()
        me, sibling = (x, y, c), (x, y, 1 - c)
        chips = [(1 - x, y), (x, 1 - y), (1 - x, 1 - y)]

        def rows(a, place):
            m = x_refs[a].shape[0]
            return out_refs[a].at[pl.ds(_linear(*place) * m, m), :]

        def copy(a, k, block, to, src=None):
            return pltpu.make_async_remote_copy(
                src_ref=rows(a, block) if src is None else src, dst_ref=rows(a, block),
                send_sem=send_sems.at[a * N_PEERS + k], recv_sem=recv_sems.at[a * N_PEERS + k],
                device_id=to, device_id_type=MESH_IDS)

        mine = [pltpu.make_async_copy(x_refs[a], rows(a, me), local_sems.at[a]) for a in range(n_arr)]
        for cp in mine:
            cp.start()
        first = []
        for a in range(n_arr):
            first.append(copy(a, 0, me, sibling, src=x_refs[a]))
            first += [copy(a, 1 + j, me, (*chip, c), src=x_refs[a]) for j, chip in enumerate(chips)]
        for cp in first:
            cp.start()
        passed = []
        for j, chip in enumerate(chips):
            for a in range(n_arr):
                copy(a, 1 + j, (*chip, c), me).wait_recv()
                fwd = copy(a, 4 + j, (*chip, c), sibling)
                fwd.start()
                passed.append(fwd)
        for a in range(n_arr):
            copy(a, 0, sibling, me).wait_recv()
            for j, chip in enumerate(chips):
                copy(a, 4 + j, (*chip, 1 - c), me).wait_recv()
        for cp in first + passed:
            cp.wait_send()
        for cp in mine:
            cp.wait()

    return pl.pallas_call(
        body, name="all_gather",
        out_shape=[jax.ShapeDtypeStruct((N_DEV * b.shape[0], b.shape[1]), b.dtype) for b in blocks],
        in_specs=[HBM_SPEC] * n_arr, out_specs=[HBM_SPEC] * n_arr,
        scratch_shapes=[pltpu.SemaphoreType.DMA((n_arr * N_PEERS,)),
                        pltpu.SemaphoreType.DMA((n_arr * N_PEERS,)),
                        pltpu.SemaphoreType.DMA((n_arr,))],
    )(*blocks)


def _ada_fwd(c_all, w_ada, b_cols):
    cols = w_ada.shape[2]

    def body(c_ref, w_ref, b_ref, ca_ref, mod_ref):
        cv = c_ref[...]
        ca = cv * _sigmoid(cv)
        ca_ref[...] = ca
        cb = _mx(jnp.concatenate([ca, ca], axis=0))
        for l in range(DEPTH):
            mod_ref[l * N_DEV:(l + 1) * N_DEV, :] = _dot(cb, _mx(w_ref[l]))[0:N_DEV] + b_ref[l]

    return pl.pallas_call(
        body, name="ada_fwd",
        out_shape=[jax.ShapeDtypeStruct((N_DEV, D_MODEL), F32),
                   jax.ShapeDtypeStruct((DEPTH * N_DEV, cols), F32)],
        compiler_params=_params(),
    )(c_all, w_ada, b_cols)


def _ada_bwd(c_act_t, dmod_cols):
    cols = dmod_cols.shape[2]

    def body(ca_ref, dm_ref, o_ref):
        ca = _mx(ca_ref[...]).astype(F32)
        for l in range(DEPTH):
            o_ref[l] = jnp.dot(ca, _mx(dm_ref[l]).astype(F32),
                               precision=lax.Precision.HIGHEST, preferred_element_type=F32)

    return pl.pallas_call(
        body, name="ada_bwd",
        out_shape=jax.ShapeDtypeStruct((DEPTH, D_MODEL, cols), F32),
        compiler_params=_params(),
    )(c_act_t, dmod_cols)


def _adamw_store(g, w_ref, m_ref, v_ref, g_ref, d_ref, mo_ref, vo_ref):
    m2 = ADAM_B1 * m_ref[...] + (1.0 - ADAM_B1) * g
    v2 = ADAM_B2 * v_ref[...] + (1.0 - ADAM_B2) * (g * g)
    m_hat = m2 / (1.0 - ADAM_B1 ** ADAM_STEP)
    v_hat = v2 / (1.0 - ADAM_B2 ** ADAM_STEP)
    g_ref[...] = g
    d_ref[...] = -ADAM_LR * (m_hat / (jnp.sqrt(v_hat) + ADAM_EPS) + ADAM_WD * w_ref[...])
    mo_ref[...] = m2
    vo_ref[...] = v2


def _slab_sum(p_ref):
    g = p_ref[0].astype(F32)
    for sl in range(1, p_ref.shape[0]):
        g = g + p_ref[sl].astype(F32)
    return g


def _sum_adamw_layers(parts, w, m, v):
    n_slab, rows, cols = parts[0].shape
    tr = min(256, rows)
    nt = rows // tr

    def body(p0_ref, p1_ref, w_ref, m_ref, v_ref, g_ref, d_ref, mo_ref, vo_ref):
        for l, p_ref in enumerate((p0_ref, p1_ref)):
            @pl.when(pl.program_id(0) == l)
            def _():
                _adamw_store(_slab_sum(p_ref), w_ref, m_ref, v_ref, g_ref, d_ref, mo_ref, vo_ref)

    p_specs = [pl.BlockSpec((n_slab, tr, cols), lambda l, i: (0, i * (1 - l) + (nt - 1) * l, 0)),
               pl.BlockSpec((n_slab, tr, cols), lambda l, i: (0, i * l, 0))]
    blk = pl.BlockSpec((None, tr, cols), lambda l, i: (l, i, 0))
    shp = jax.ShapeDtypeStruct((DEPTH, rows, cols), F32)
    return pl.pallas_call(
        body, name="sum_adamw_layers", grid=(DEPTH, nt),
        in_specs=p_specs + [blk, blk, blk],
        out_specs=[blk, blk, blk, blk],
        out_shape=[shp, shp, shp, shp],
        compiler_params=_params(("arbitrary", "arbitrary")),
    )(parts[0], parts[1], w, m, v)


def _sum_adamw(parts, w, m, v):
    n_slab, rows, cols = parts.shape
    tr = min(256, rows)

    def body(p_ref, w_ref, m_ref, v_ref, g_ref, d_ref, mo_ref, vo_ref):
        _adamw_store(_slab_sum(p_ref), w_ref, m_ref, v_ref, g_ref, d_ref, mo_ref, vo_ref)

    blk = pl.BlockSpec((tr, cols), lambda i: (i, 0))
    shp = jax.ShapeDtypeStruct((rows, cols), F32)
    return pl.pallas_call(
        body, name="sum_adamw", grid=(rows // tr,),
        in_specs=[pl.BlockSpec((n_slab, tr, cols), lambda i: (0, i, 0)), blk, blk, blk],
        out_specs=[blk, blk, blk, blk],
        out_shape=[shp, shp, shp, shp],
        compiler_params=_params(("arbitrary",)),
    )(parts, w, m, v)


SMALL_ROWS = 16


def kernel(x, c, norm_g, w_ada, b_ada, w_in, w_out, final_g, loss_target, m_norm_g, m_w_ada, m_b_ada, m_w_in, m_w_out, m_final_g, v_norm_g, v_w_ada, v_b_ada, v_w_in, v_w_out, v_final_g):
    me = _linear(*_my_place())
    in_cols = w_in.shape[2]
    out_rows = w_out.shape[1]
    ada_cols = w_ada.shape[2]

    g_in, g_out, g_c = _all_gather([
        _mx(w_in).reshape(DEPTH * D_MODEL, in_cols),
        _mx(w_out).reshape(DEPTH * out_rows, D_MODEL),
        jnp.broadcast_to(c, (8, D_MODEL))])
    w_in_g = g_in.reshape(N_DEV, DEPTH, D_MODEL, in_cols)
    w_out_g = g_out.reshape(N_DEV, DEPTH, out_rows, D_MODEL)
    c_all = g_c.reshape(N_DEV, 8, D_MODEL)[:, 0]

    b_cols = lax.dynamic_slice_in_dim(b_ada, me * ada_cols, ada_cols, axis=1)[:, None, :]
    c_act, mod_cols = _ada_fwd(c_all, w_ada, b_cols)
    (g_mod,) = _all_gather([mod_cols])
    g_mod = g_mod.reshape(N_DEV, DEPTH, N_DEV, ada_cols)
    mod = lax.dynamic_index_in_dim(g_mod, me, axis=2, keepdims=False)
    mod = mod.transpose(1, 0, 2).reshape(DEPTH, 3, D_MODEL)

    tables = _ret_tables(x.shape[1])
    h = x[0]
    saved = []
    for l in range(DEPTH):
        h, sv = _layer_fwd(l, h, mod[l], norm_g[l:l + 1], w_in_g, w_out_g, tables)
        saved.append(sv)
    dx, loss_part, dfg = _final_loss(h, final_g[None], loss_target[0])
    r_in, r_out, small = [None] * DEPTH, [None] * DEPTH, [None] * DEPTH
    for l in reversed(range(DEPTH)):
        dx, r_in[l], r_out[l], dmod, dng = _layer_bwd(l, me, dx, saved[l], norm_g[l:l + 1], w_in_g, w_out_g, tables)
        small[l] = (dmod.reshape(3, D_MODEL), dng)

    pad = jnp.zeros((SMALL_ROWS - 10, D_MODEL), F32)
    small_block = jnp.concatenate([small[0][0], small[1][0], small[0][1], small[1][1], dfg,
                                   jnp.broadcast_to(loss_part, (1, D_MODEL)), pad], axis=0)
    (g_small,) = _all_gather([small_block])
    g_small = g_small.reshape(N_DEV, SMALL_ROWS, D_MODEL)

    def small_pack(b, n, f, fill):
        return jnp.concatenate([b.reshape(6, D_MODEL), n, f[None],
                                jnp.full((SMALL_ROWS - 9, D_MODEL), fill, F32)], axis=0)

    s_g, s_d, s_m, s_v = _sum_adamw(g_small, small_pack(b_ada, norm_g, final_g, 0.0),
                                    small_pack(m_b_ada, m_norm_g, m_final_g, 0.0),
                                    small_pack(v_b_ada, v_norm_g, v_final_g, 1.0))
    loss = s_g[9, 0]

    def small_unpack(a):
        return a[0:6].reshape(DEPTH, 3 * D_MODEL), a[6:8], a[8]

    dmod_all = g_small[:, 0:6].reshape(N_DEV, DEPTH, 3 * D_MODEL).transpose(1, 0, 2)
    dmod_cols = lax.dynamic_slice_in_dim(dmod_all, me * ada_cols, ada_cols, axis=2)
    g_ada = _ada_bwd(c_act.T, dmod_cols).reshape(1, DEPTH * D_MODEL, ada_cols)
    ada = _sum_adamw(g_ada, *[a.reshape(DEPTH * D_MODEL, ada_cols) for a in (w_ada, m_w_ada, v_w_ada)])
    ada = [a.reshape(DEPTH, D_MODEL, ada_cols) for a in ada]

    win = _sum_adamw_layers(r_in, w_in, m_w_in, v_w_in)
    wout = _sum_adamw_layers(r_out, w_out, m_w_out, v_w_out)

    outs = [loss, dx[None]]
    for k in range(4):
        b, n, f = small_unpack((s_g, s_d, s_m, s_v)[k])
        outs += [n, ada[k], b, win[k], wout[k], f]
    return tuple(outs)
```

```python
import functools

import jax
import jax.numpy as jnp
from jax import lax
from jax.experimental import pallas as pl
from jax.experimental.pallas import tpu as pltpu

F32 = jnp.float32
MXU_DTYPE = jnp.bfloat16

D_MODEL = 1024
DEPTH = 2
N_DEV = 8
CHUNK = 64
D_RET = 512
D_SB = 512
RET_HEADS = 4
RET_HEAD_DIM = 128
SB_HEADS = 8
SB_HEAD_DIM = 64
D_IN = 4096
ROPE_BASE = 10000.0
EPS = 1e-6
SB_SCALE = SB_HEAD_DIM ** -0.5
RET_KSCALE = RET_HEAD_DIM ** -0.5

ADAM_LR = 0.001
ADAM_B1 = 0.9
ADAM_B2 = 0.999
ADAM_EPS = 1e-08
ADAM_WD = 0.01
ADAM_STEP = 10

V7X_VMEM_BYTES = 64 * 2 ** 20
VMEM_LIMIT = V7X_VMEM_BYTES - 8 * 2 ** 20
LANES = 128

_NT = (((1,), (1,)), ((), ()))
_TN = (((0,), (0,)), ((), ()))


def _dot(a, b):
    return jnp.dot(a, b, preferred_element_type=F32)


def _dot_nt(a, b):
    return lax.dot_general(a, b, _NT, preferred_element_type=F32)


def _dot_tn(a, b):
    return lax.dot_general(a, b, _TN, preferred_element_type=F32)


def _mx(x):
    return x.astype(MXU_DTYPE)


def _sigmoid(x):
    return 1.0 / (1.0 + jnp.exp(-x))


def _params(sem=None):
    return pltpu.CompilerParams(dimension_semantics=sem, vmem_limit_bytes=VMEM_LIMIT)


def _row_tile(s):
    return min(512, s)


def _w_in_spec(w_in_g, layer):
    return pl.BlockSpec((N_DEV, None) + w_in_g.shape[2:], lambda i: (0, layer, 0, 0))


def _w_out_spec(w_out_g, layer):
    return pl.BlockSpec((N_DEV, None) + w_out_g.shape[2:], lambda i: (0, layer, 0, 0))


def _ln_proj(x, shift, scale1p, g, w_in_g, layer):
    s = x.shape[0]
    ts = _row_tile(s)

    def body(x_ref, sh_ref, sc_ref, g_ref, w_ref, ret_ref, qkv_ref, sg_ref):
        xv = x_ref[...]
        rstd = lax.rsqrt(jnp.mean(xv * xv, axis=-1, keepdims=True) + EPS)
        h = (xv * rstd * g_ref[...]) * sc_ref[...] + sh_ref[...]
        hb = _mx(h)
        for n in range(4):
            ret_ref[:, n * 512:(n + 1) * 512] = _dot(hb, w_ref[n])
        qkv_ref[:, 0:512] = _mx(_dot(hb, w_ref[4]) * SB_SCALE)
        qkv_ref[:, 512:1024] = _mx(_dot(hb, w_ref[5]))
        qkv_ref[:, 1024:1536] = _mx(_dot(hb, w_ref[6]))
        sg_ref[...] = _dot(hb, w_ref[7])

    vec = pl.BlockSpec((1, D_MODEL), lambda i: (0, 0))
    return pl.pallas_call(
        body, name="ln_proj", grid=(s // ts,),
        in_specs=[pl.BlockSpec((ts, D_MODEL), lambda i: (i, 0)), vec, vec, vec,
                  _w_in_spec(w_in_g, layer)],
        out_specs=[pl.BlockSpec((ts, 2048), lambda i: (i, 0)),
                   pl.BlockSpec((ts, 1536), lambda i: (i, 0)),
                   pl.BlockSpec((ts, 512), lambda i: (i, 0))],
        out_shape=[jax.ShapeDtypeStruct((s, 2048), F32),
                   jax.ShapeDtypeStruct((s, 1536), MXU_DTYPE),
                   jax.ShapeDtypeStruct((s, 512), F32)],
        compiler_params=_params(("arbitrary",)),
    )(x, shift, scale1p, g, w_in_g)


def _w_out_halves(w_ref):
    half = N_DEV // 2
    return (w_ref[0:half].reshape(D_RET, D_MODEL), w_ref[half:N_DEV].reshape(D_SB, D_MODEL))


def _out_proj(x, gate, y_r, y_s, w_out_g, layer):
    s = x.shape[0]
    ts = _row_tile(s)

    def body(x_ref, gate_ref, yr_ref, ys_ref, w_ref, o_ref):
        w_r, w_s = _w_out_halves(w_ref)
        t = _dot(yr_ref[...], w_r) + _dot(ys_ref[...], w_s)
        o_ref[...] = x_ref[...] + gate_ref[...] * t

    return pl.pallas_call(
        body, name="out_proj", grid=(s // ts,),
        in_specs=[pl.BlockSpec((ts, D_MODEL), lambda i: (i, 0)),
                  pl.BlockSpec((1, D_MODEL), lambda i: (0, 0)),
                  pl.BlockSpec((ts, 512), lambda i: (i, 0)),
                  pl.BlockSpec((ts, 512), lambda i: (i, 0)),
                  _w_out_spec(w_out_g, layer)],
        out_specs=pl.BlockSpec((ts, D_MODEL), lambda i: (i, 0)),
        out_shape=jax.ShapeDtypeStruct((s, D_MODEL), F32),
        compiler_params=_params(("arbitrary",)),
    )(x, gate, y_r, y_s, w_out_g)


def _final_loss(x, fg, target):
    s = x.shape[0]
    ts = _row_tile(s)

    def body(x_ref, fg_ref, t_ref, dx_ref, loss_ref, dfg_ref):
        i = pl.program_id(0)

        @pl.when(i == 0)
        def _():
            loss_ref[...] = jnp.zeros_like(loss_ref)
            dfg_ref[...] = jnp.zeros_like(dfg_ref)

        xv = x_ref[...]
        fgv = fg_ref[...]
        rstd = lax.rsqrt(jnp.mean(xv * xv, axis=-1, keepdims=True) + EPS)
        xn = xv * rstd
        err = xn * fgv - t_ref[...]
        tok = jnp.mean(err * err, axis=-1, keepdims=True)
        loss_ref[...] += 0.5 * jnp.sum(tok, axis=0, keepdims=True)
        dy = err * (1.0 / D_MODEL)
        dfg_ref[...] += jnp.sum(dy * xn, axis=0, keepdims=True)
        dxn = dy * fgv
        dx_ref[...] = rstd * (dxn - xn * jnp.mean(dxn * xn, axis=-1, keepdims=True))

    return pl.pallas_call(
        body, name="final_loss", grid=(s // ts,),
        in_specs=[pl.BlockSpec((ts, D_MODEL), lambda i: (i, 0)),
                  pl.BlockSpec((1, D_MODEL), lambda i: (0, 0)),
                  pl.BlockSpec((ts, D_MODEL), lambda i: (i, 0))],
        out_specs=[pl.BlockSpec((ts, D_MODEL), lambda i: (i, 0)),
                   pl.BlockSpec((1, 1), lambda i: (0, 0)),
                   pl.BlockSpec((1, D_MODEL), lambda i: (0, 0))],
        out_shape=[jax.ShapeDtypeStruct((s, D_MODEL), F32),
                   jax.ShapeDtypeStruct((1, 1), F32),
                   jax.ShapeDtypeStruct((1, D_MODEL), F32)],
        compiler_params=_params(("arbitrary",)),
    )(x, fg, target)


def _out_proj_bwd(dx_out, gate, y_r, y_s, w_out_g, layer):
    s = dx_out.shape[0]
    ts = _row_tile(s)

    def body(dx_ref, gate_ref, yr_ref, ys_ref, w_ref, dy_ref, dw_ref, dgate_ref):
        i = pl.program_id(0)

        @pl.when(i == 0)
        def _():
            dw_ref[...] = jnp.zeros_like(dw_ref)
            dgate_ref[...] = jnp.zeros_like(dgate_ref)

        dxv = dx_ref[...]
        dt = _mx(dxv * gate_ref[...])
        yr = yr_ref[...]
        ys = ys_ref[...]
        w_r, w_s = _w_out_halves(w_ref)
        dy_ref[:, 0:512] = _dot_nt(dt, w_r)
        dy_ref[:, 512:1024] = _dot_nt(dt, w_s)
        dw_ref[0:512, :] += _dot_tn(yr, dt)
        dw_ref[512:1024, :] += _dot_tn(ys, dt)
        t = _dot(yr, w_r) + _dot(ys, w_s)
        dgate_ref[...] += jnp.sum(dxv * t, axis=0, keepdims=True)

    return pl.pallas_call(
        body, name="out_proj_bwd", grid=(s // ts,),
        in_specs=[pl.BlockSpec((ts, D_MODEL), lambda i: (i, 0)),
                  pl.BlockSpec((1, D_MODEL), lambda i: (0, 0)),
                  pl.BlockSpec((ts, 512), lambda i: (i, 0)),
                  pl.BlockSpec((ts, 512), lambda i: (i, 0)),
                  _w_out_spec(w_out_g, layer)],
        out_specs=[pl.BlockSpec((ts, D_MODEL), lambda i: (i, 0)),
                   pl.BlockSpec((D_MODEL, D_MODEL), lambda i: (0, 0)),
                   pl.BlockSpec((1, D_MODEL), lambda i: (0, 0))],
        out_shape=[jax.ShapeDtypeStruct((s, D_MODEL), F32),
                   jax.ShapeDtypeStruct((D_MODEL, D_MODEL), F32),
                   jax.ShapeDtypeStruct((1, D_MODEL), F32)],
        compiler_params=_params(("arbitrary",)),
    )(dx_out, gate, y_r, y_s, w_out_g)


def _in_proj_bwd_x(x, dx_out, dproj, shift, scale1p, g, w_in_g, layer):
    s = x.shape[0]
    ts = min(256, s)

    def body(x_ref, dxo_ref, dp_ref, sc_ref, g_ref, w_ref, dx_ref, dsh_ref, dsc_ref, dg_ref):
        i = pl.program_id(0)

        @pl.when(i == 0)
        def _():
            dsh_ref[...] = jnp.zeros_like(dsh_ref)
            dsc_ref[...] = jnp.zeros_like(dsc_ref)
            dg_ref[...] = jnp.zeros_like(dg_ref)

        nb = D_IN // N_DEV
        dh = _dot_nt(dp_ref[:, 0:nb], w_ref[0])
        for n in range(1, N_DEV):
            dh += _dot_nt(dp_ref[:, n * nb:(n + 1) * nb], w_ref[n])
        xv = x_ref[...]
        gv = g_ref[...]
        scv = sc_ref[...]
        rstd = lax.rsqrt(jnp.mean(xv * xv, axis=-1, keepdims=True) + EPS)
        xn = xv * rstd
        dsh_ref[...] += jnp.sum(dh, axis=0, keepdims=True)
        dsc_ref[...] += jnp.sum(dh * (xn * gv), axis=0, keepdims=True)
        dhs = dh * scv
        dg_ref[...] += jnp.sum(dhs * xn, axis=0, keepdims=True)
        dxn = dhs * gv
        dx_ref[...] = rstd * (dxn - xn * jnp.mean(dxn * xn, axis=-1, keepdims=True)) + dxo_ref[...]

    del shift
    vec = pl.BlockSpec((1, D_MODEL), lambda i: (0, 0))
    return pl.pallas_call(
        body, name="in_proj_bwd_x", grid=(s // ts,),
        in_specs=[pl.BlockSpec((ts, D_MODEL), lambda i: (i, 0)),
                  pl.BlockSpec((ts, D_MODEL), lambda i: (i, 0)),
                  pl.BlockSpec((ts, D_IN), lambda i: (i, 0)),
                  vec, vec,
                  _w_in_spec(w_in_g, layer)],
        out_specs=[pl.BlockSpec((ts, D_MODEL), lambda i: (i, 0)), vec, vec, vec],
        out_shape=[jax.ShapeDtypeStruct((s, D_MODEL), F32),
                   jax.ShapeDtypeStruct((1, D_MODEL), F32),
                   jax.ShapeDtypeStruct((1, D_MODEL), F32),
                   jax.ShapeDtypeStruct((1, D_MODEL), F32)],
        compiler_params=_params(("arbitrary",)),
    )(x, dx_out, dproj, scale1p, g, w_in_g)


def _in_proj_bwd_w(me, x, dproj, shift, scale1p, g, dwo_parts):
    s = x.shape[0]
    ts = _row_tile(s)
    ns = s // ts
    nb = D_IN // N_DEV
    last = N_DEV - 1

    def body(me_ref, x_ref, dp_ref, sh_ref, sc_ref, g_ref, dwo_ref, rin_ref, rout_ref,
             acc, stage, h_t, in_send, in_recv, out_send, out_recv, local_sems):
        del me_ref
        t = pl.program_id(0)
        i = pl.program_id(1)
        px, py, pc = _my_place()
        mine = _linear(px, py, pc)

        def out_copy(r):
            peer = (1 - px if r & 4 else px, 1 - py if r & 2 else py, 1 - pc if r & 1 else pc)
            return pltpu.make_async_remote_copy(
                src_ref=dwo_ref.at[_linear(*peer)], dst_ref=rout_ref.at[mine],
                send_sem=out_send.at[r - 1], recv_sem=out_recv.at[r - 1],
                device_id=peer, device_id_type=MESH_IDS)

        def in_copy(step):
            dest = (mine + 1 + step) % N_DEV
            return pltpu.make_async_remote_copy(
                src_ref=stage.at[step % 2], dst_ref=rin_ref.at[mine],
                send_sem=in_send.at[step], recv_sem=in_recv.at[step],
                device_id=(dest // 4, (dest // 2) % 2, dest % 2), device_id_type=MESH_IDS)

        own_out = pltpu.make_async_copy(dwo_ref.at[mine], rout_ref.at[mine], local_sems.at[0])
        own_in = pltpu.make_async_copy(stage.at[last % 2], rin_ref.at[mine], local_sems.at[1])

        @pl.when(jnp.logical_and(t == 0, i == 0))
        def _():
            own_out.start()
            for r in range(1, N_DEV):
                out_copy(r).start()

        @pl.when(i == 0)
        def _():
            acc[...] = jnp.zeros_like(acc)

        @pl.when(t == 0)
        def _():
            xv = x_ref[...]
            rstd = lax.rsqrt(jnp.mean(xv * xv, axis=-1, keepdims=True) + EPS)
            h = (xv * rstd * g_ref[...]) * sc_ref[...] + sh_ref[...]
            h_t[i] = _mx(h.T)

        acc[...] += _dot(h_t[i], dp_ref[...])

        @pl.when(i == ns - 1)
        def _():
            @pl.when(t >= 2)
            def _():
                in_copy(t - 2).wait_send()

            stage[t % 2] = acc[...].astype(stage.dtype)

            @pl.when(t < last)
            def _():
                in_copy(t).start()

            @pl.when(t == last)
            def _():
                own_in.start()
                in_copy(last - 1).wait_send()
                for step in range(last):
                    in_copy(step).wait_recv()
                for r in range(1, N_DEV):
                    out_copy(r).wait_recv()
                    out_copy(r).wait_send()
                own_out.wait()
                own_in.wait()

    vec = pl.BlockSpec((1, D_MODEL), lambda t, i, me_ref: (0, 0))
    rows = dwo_parts.shape[1]
    return pl.pallas_call(
        body, name="in_proj_bwd_w",
        grid_spec=pltpu.PrefetchScalarGridSpec(
            num_scalar_prefetch=1, grid=(N_DEV, ns),
            in_specs=[pl.BlockSpec((ts, D_MODEL), lambda t, i, me_ref: (jnp.where(t == 0, i, ns - 1), 0)),
                      pl.BlockSpec((ts, nb), lambda t, i, me_ref: (i, (me_ref[0] + 1 + t) % N_DEV)),
                      vec, vec, vec, HBM_SPEC],
            out_specs=[HBM_SPEC, HBM_SPEC],
            scratch_shapes=[pltpu.VMEM((D_MODEL, nb), F32),
                            pltpu.VMEM((2, D_MODEL, nb), MXU_DTYPE),
                            pltpu.VMEM((ns, D_MODEL, ts), MXU_DTYPE),
                            pltpu.SemaphoreType.DMA((N_PEERS,)), pltpu.SemaphoreType.DMA((N_PEERS,)),
                            pltpu.SemaphoreType.DMA((N_PEERS,)), pltpu.SemaphoreType.DMA((N_PEERS,)),
                            pltpu.SemaphoreType.DMA((2,))]),
        out_shape=[jax.ShapeDtypeStruct((N_DEV, D_MODEL, nb), MXU_DTYPE),
                   jax.ShapeDtypeStruct((N_DEV, rows, D_MODEL), MXU_DTYPE)],
        compiler_params=_params(("arbitrary", "arbitrary")),
    )(jnp.reshape(me, (1,)).astype(jnp.int32), x, dproj, shift, scale1p, g, dwo_parts)


RET_TILE = 256


def _ret_tables(s):
    t = min(RET_TILE, s)
    half = RET_HEAD_DIM // 2
    pos = jnp.arange(s, dtype=F32)
    inv = ROPE_BASE ** (-jnp.arange(half, dtype=F32) / half)
    ang = pos[:, None] * inv[None, :]
    cos, sin = jnp.cos(ang), jnp.sin(ang)
    cos2 = jnp.concatenate([cos, cos], axis=1)
    sin2 = jnp.concatenate([-sin, sin], axis=1)
    lg = jnp.log1p(-(2.0 ** (-5.0 - jnp.arange(RET_HEADS, dtype=F32))))[:, None, None]
    n = jnp.arange(t)
    dist = (n[:, None] - n[None, :]).astype(F32)[None]
    cn = (n // CHUNK)[:, None]
    cm = (n // CHUNK)[None, :]
    mask = jnp.where((cn == cm)[None], jnp.exp(jnp.abs(dist) * lg),
                     jnp.where((cm < cn)[None], jnp.exp(dist * lg), 0.0))
    nf = n.astype(F32)[None, :, None]
    dq = jnp.broadcast_to(jnp.exp((nf + 1.0) * lg), (RET_HEADS, t, LANES))
    dk = jnp.broadcast_to(jnp.exp((t - 1.0 - nf) * lg), (RET_HEADS, t, LANES))
    gt = jnp.broadcast_to(jnp.exp(float(t) * lg), (RET_HEADS, 1, LANES))
    return cos2, sin2, mask, dq, dk, gt


def _roll_half(x):
    return pltpu.roll(x, RET_HEAD_DIM // 2, 1)


def _ret_heads_fwd(ret_ref, cos, sin, m_ref, dq_ref, dk_ref, s0):
    hd = RET_HEAD_DIM
    heads = range(RET_HEADS)
    qb, kb, vb, kdb = [], [], [], []
    for h in heads:
        q = ret_ref[:, h * hd:(h + 1) * hd]
        k = ret_ref[:, 512 + h * hd:512 + (h + 1) * hd]
        kr = (k * cos + _roll_half(k) * sin) * RET_KSCALE
        qb.append(_mx(q * cos + _roll_half(q) * sin))
        kb.append(_mx(kr))
        kdb.append(_mx(kr * dk_ref[h]))
        vb.append(_mx(ret_ref[:, 1024 + h * hd:1024 + (h + 1) * hd]))
    p = [_dot_nt(qb[h], kb[h]) for h in heads]
    cross = [_dot(qb[h], _mx(s0[h])) for h in heads]
    pb = [_mx(p[h] * m_ref[h]) for h in heads]
    o = [_dot(pb[h], vb[h]) + cross[h] * dq_ref[h] for h in heads]
    gn, rstd = [], []
    for h in heads:
        oc = o[h] - jnp.mean(o[h], axis=-1, keepdims=True)
        rstd.append(lax.rsqrt(jnp.mean(oc * oc, axis=-1, keepdims=True) + EPS))
        gn.append(oc * rstd[h])
    return qb, kb, vb, pb, kdb, gn, rstd


def _retention_fwd(ret, tables):
    cos2, sin2, mask, dq, dk, gt = tables
    s = ret.shape[0]
    t = mask.shape[1]
    nt = s // t
    hd = RET_HEAD_DIM

    def body(ret_ref, cos_ref, sin_ref, m_ref, dq_ref, dk_ref, gt_ref, y_ref, st_ref, s_scr):
        i = pl.program_id(0)

        @pl.when(i == 0)
        def _():
            s_scr[...] = jnp.zeros_like(s_scr)

        s0 = [s_scr[h] for h in range(RET_HEADS)]
        _, _, vb, _, kdb, gn, _ = _ret_heads_fwd(ret_ref, cos_ref[...], sin_ref[...], m_ref, dq_ref, dk_ref, s0)
        kv = [_dot_tn(kdb[h], vb[h]) for h in range(RET_HEADS)]
        for h in range(RET_HEADS):
            g = ret_ref[:, 1536 + h * hd:1536 + (h + 1) * hd]
            st_ref[h] = s0[h]
            y_ref[:, h * hd:(h + 1) * hd] = (gn[h] * (g * _sigmoid(g))).astype(y_ref.dtype)
            s_scr[h] = s0[h] * gt_ref[h] + kv[h]

    full3 = lambda a: pl.BlockSpec(a.shape, lambda i: (0, 0, 0))
    return pl.pallas_call(
        body, name="retention_fwd", grid=(nt,),
        in_specs=[pl.BlockSpec((t, 2048), lambda i: (i, 0)),
                  pl.BlockSpec((t, LANES), lambda i: (i, 0)),
                  pl.BlockSpec((t, LANES), lambda i: (i, 0)),
                  full3(mask), full3(dq), full3(dk), full3(gt)],
        out_specs=[pl.BlockSpec((t, 512), lambda i: (i, 0)),
                   pl.BlockSpec((None, RET_HEADS, hd, hd), lambda i: (i, 0, 0, 0))],
        out_shape=[jax.ShapeDtypeStruct((s, 512), MXU_DTYPE),
                   jax.ShapeDtypeStruct((nt, RET_HEADS, hd, hd), F32)],
        scratch_shapes=[pltpu.VMEM((RET_HEADS, hd, hd), F32)],
        compiler_params=_params(("arbitrary",)),
    )(ret, cos2, sin2, mask, dq, dk, gt)


def _retention_bwd(ret, states, dy, tables):
    cos2, sin2, mask, dq, dk, gt = tables
    s = ret.shape[0]
    t = mask.shape[1]
    nt = s // t
    hd = RET_HEAD_DIM

    def body(ret_ref, st_ref, dy_ref, cos_ref, sin_ref, m_ref, dq_ref, dk_ref, gt_ref, d_ref, ds_scr):
        i = pl.program_id(0)

        @pl.when(i == 0)
        def _():
            ds_scr[...] = jnp.zeros_like(ds_scr)

        cos = cos_ref[...]
        sin = sin_ref[...]
        heads = range(RET_HEADS)
        s0 = [st_ref[h] for h in heads]
        ds = [ds_scr[h] for h in heads]
        dsb = [_mx(ds[h]) for h in heads]
        qb, kb, vb, pb, kdb, gn, rstd = _ret_heads_fwd(ret_ref, cos, sin, m_ref, dq_ref, dk_ref, s0)
        dob, dodb = [], []
        for h in heads:
            g = ret_ref[:, 1536 + h * hd:1536 + (h + 1) * hd]
            dyv = dy_ref[:, h * hd:(h + 1) * hd]
            sg = _sigmoid(g)
            d_ref[:, 1536 + h * hd:1536 + (h + 1) * hd] = (
                dyv * gn[h] * (sg * (1.0 + g * (1.0 - sg)))).astype(d_ref.dtype)
            dgn = dyv * (g * sg)
            do = rstd[h] * (dgn - jnp.mean(dgn, axis=-1, keepdims=True)
                            - gn[h] * jnp.mean(dgn * gn[h], axis=-1, keepdims=True))
            dob.append(_mx(do))
            dodb.append(_mx(do * dq_ref[h]))
        dp = [_dot_nt(dob[h], vb[h]) for h in heads]
        dv = [_dot_tn(pb[h], dob[h]) + _dot(kdb[h], dsb[h]) for h in heads]
        dq_cross = [_dot_nt(dodb[h], _mx(s0[h])) for h in heads]
        dk_cross = [_dot_nt(vb[h], dsb[h]) for h in heads]
        ds_new = [_dot_tn(qb[h], dodb[h]) for h in heads]
        dpb = [_mx(dp[h] * m_ref[h]) for h in heads]
        dqr = [_dot(dpb[h], kb[h]) + dq_cross[h] for h in heads]
        dkr = [(_dot_tn(dpb[h], qb[h]) + dk_cross[h] * dk_ref[h]) * RET_KSCALE for h in heads]
        for h in heads:
            d_ref[:, 1024 + h * hd:1024 + (h + 1) * hd] = dv[h].astype(d_ref.dtype)
            d_ref[:, h * hd:(h + 1) * hd] = (dqr[h] * cos + _roll_half(dqr[h] * sin)).astype(d_ref.dtype)
            d_ref[:, 512 + h * hd:512 + (h + 1) * hd] = (
                dkr[h] * cos + _roll_half(dkr[h] * sin)).astype(d_ref.dtype)
            ds_scr[h] = ds[h] * gt_ref[h] + ds_new[h]

    full3 = lambda a: pl.BlockSpec(a.shape, lambda i: (0, 0, 0))
    rev = lambda i: (nt - 1 - i, 0)
    return pl.pallas_call(
        body, name="retention_bwd", grid=(nt,),
        in_specs=[pl.BlockSpec((t, 2048), rev),
                  pl.BlockSpec((None, RET_HEADS, hd, hd), lambda i: (nt - 1 - i, 0, 0, 0)),
                  pl.BlockSpec((t, 512), rev),
                  pl.BlockSpec((t, LANES), rev),
                  pl.BlockSpec((t, LANES), rev),
                  full3(mask), full3(dq), full3(dk), full3(gt)],
        out_specs=pl.BlockSpec((t, 2048), rev),
        out_shape=jax.ShapeDtypeStruct((s, 2048), MXU_DTYPE),
        scratch_shapes=[pltpu.VMEM((RET_HEADS, hd, hd), F32)],
        compiler_params=_params(("arbitrary",)),
    )(ret, states, dy, cos2, sin2, mask, dq, dk, gt)


SB_BLOCK = 256
SB_CHUNK = 32


SB_SKIP = 104.0


def _split_dots(xs, u):
    parts = []
    for x in xs:
        hi = lax.bitcast_convert_type(lax.bitcast_convert_type(x, jnp.uint32) & jnp.uint32(0xFFFF0000), F32)
        parts += [_mx(hi), _mx(x - hi)]
    out = _dot(jnp.concatenate(parts, axis=0), u)
    n = xs[0].shape[0]
    return [out[2 * k * n:(2 * k + 1) * n] + out[(2 * k + 1) * n:(2 * k + 2) * n] for k in range(len(xs))]


def _split_dot(x, u):
    return _split_dots([x], u)[0]


def _sb_pair_weights(lb, lk, allowed, u_gt):
    blk = lb.shape[0]
    lk_p, lk_d = lk[:, :blk], lk[:, blk:]
    r_d = _rowsum(lk_d)
    cs_p, cs_d = _split_dots([lk_p, lk_d], u_gt)
    a = jnp.exp(lb + jnp.concatenate([cs_p + r_d, cs_d], axis=1))
    return jnp.where(allowed, a, 0.0), r_d, r_d + _rowsum(lk_p)


def _sb_logits(q, k, causal):
    z = _dot_nt(q, k)
    l1p = jnp.log(1.0 + jnp.exp(-jnp.abs(z)))
    lk = -(jnp.maximum(z, 0.0) + l1p)
    if causal is not None:
        lk = jnp.where(causal, lk, 0.0)
    return jnp.minimum(z, 0.0) - l1p, lk


def _sb_weights(lb, lk, r, u_gt, causal):
    a = jnp.exp(lb + _split_dot(lk, u_gt) + r)
    return a if causal is None else jnp.where(causal, a, 0.0)


def _rowsum(x):
    return jnp.sum(x, axis=1, keepdims=True)


def _sb_pair_tile(i, blk):
    row = lax.broadcasted_iota(jnp.int32, (blk, 2 * blk), 0)
    col = lax.broadcasted_iota(jnp.int32, (blk, 2 * blk), 1)
    first_col = jnp.where(i >= 1, 0, blk)
    allowed = jnp.logical_and(row > col - blk, col >= first_col)
    rows_p = pl.ds(pl.multiple_of(jnp.maximum(i - 1, 0) * blk, blk), blk)
    rows_d = pl.ds(pl.multiple_of(i * blk, blk), blk)
    return allowed, rows_p, rows_d


def _sb_fwd(qkv, sg):
    s = qkv.shape[0]
    blk = min(SB_BLOCK, s)
    nq = s // blk
    hd = SB_HEAD_DIM

    ch = min(SB_CHUNK, blk)

    def body(q_ref, k_ref, v_ref, g_ref, y_ref, o_ref, z_scr, lhs_scr, cs_scr, a_scr):
        i = pl.program_id(1)
        row = lax.broadcasted_iota(jnp.int32, (blk, blk), 0)
        col = lax.broadcasted_iota(jnp.int32, (blk, blk), 1)
        u_gt = (row > col).astype(MXU_DTYPE)
        heads = [slice(hh * hd, (hh + 1) * hd) for hh in range(2)]
        qs = [q_ref[:, ls] for ls in heads]
        _, rows_p, rows_d = _sb_pair_tile(i, blk)
        has_prev = i >= 1
        crow = lax.broadcasted_iota(jnp.int32, (ch, blk), 0)
        ccol = lax.broadcasted_iota(jnp.int32, (ch, blk), 1)

        def logits(hh):
            kc = jnp.concatenate([k_ref[rows_p, heads[hh]], k_ref[rows_d, heads[hh]]], axis=0)
            z_scr[hh] = _dot_nt(qs[hh], kc)

        def keep_parts(hh):
            r_d, r_all = [], []
            for c in range(blk // ch):
                rows = pl.ds(c * ch, ch)
                causal = crow + c * ch > ccol
                z = z_scr[hh, rows, :]
                l1p = jnp.log(1.0 + jnp.exp(-jnp.abs(z)))
                lk = -(jnp.maximum(z, 0.0) + l1p)
                z_scr[hh, rows, :] = jnp.minimum(z, 0.0) - l1p
                lk_p = jnp.where(has_prev, lk[:, :blk], 0.0)
                lk_d = jnp.where(causal, lk[:, blk:], 0.0)
                for part, x in enumerate((lk_p, lk_d)):
                    hi = lax.bitcast_convert_type(
                        lax.bitcast_convert_type(x, jnp.uint32) & jnp.uint32(0xFFFF0000), F32)
                    lhs_scr[hh, pl.ds((2 * part) * blk + c * ch, ch), :] = _mx(hi)
                    lhs_scr[hh, pl.ds((2 * part + 1) * blk + c * ch, ch), :] = _mx(x - hi)
                r_d.append(_rowsum(lk_d))
                r_all.append(r_d[c] + _rowsum(lk_p))
            return r_d, jnp.concatenate(r_all, axis=0)

        def suffix_sums(hh):
            cs_scr[hh] = _dot(lhs_scr[hh], u_gt)

        def weights(hh, r_d):
            for c in range(blk // ch):
                rows = pl.ds(c * ch, ch)
                causal = crow + c * ch > ccol
                cs_p = cs_scr[hh, pl.ds(c * ch, ch), :] + cs_scr[hh, pl.ds(blk + c * ch, ch), :] + r_d[c]
                cs_d = cs_scr[hh, pl.ds(2 * blk + c * ch, ch), :] + cs_scr[hh, pl.ds(3 * blk + c * ch, ch), :]
                lb = z_scr[hh, rows, :]
                a_p = jnp.where(has_prev, jnp.exp(lb[:, :blk] + cs_p), 0.0)
                a_d = jnp.where(causal, jnp.exp(lb[:, blk:] + cs_d), 0.0)
                a_scr[hh, rows, :] = _mx(jnp.concatenate([a_p, a_d], axis=1))

        def values(hh):
            vc = jnp.concatenate([v_ref[rows_p, heads[hh]], v_ref[rows_d, heads[hh]]], axis=0)
            return _dot(a_scr[hh], vc)

        def block(hh, j, r):
            start = pl.multiple_of(j * blk, blk)
            lb, lk = _sb_logits(qs[hh], k_ref[pl.ds(start, blk), heads[hh]], None)
            a = _sb_weights(lb, lk, r, u_gt, None)
            return _dot(_mx(a), v_ref[pl.ds(start, blk), heads[hh]]), r + _rowsum(lk)

        def more(n, r0, r1):
            return jnp.logical_and(n < i, jnp.max(jnp.maximum(r0, r1)) > -SB_SKIP)

        logits(0)
        logits(1)
        rd0, r0 = keep_parts(0)
        suffix_sums(0)
        rd1, r1 = keep_parts(1)
        suffix_sums(1)
        go = more(jnp.int32(1), r0, r1)
        gates = [g_ref[:, ls] * _sigmoid(g_ref[:, ls]) for ls in heads]
        weights(0, rd0)
        acc0 = values(0)
        weights(1, rd1)
        acc1 = values(1)

        def step(c):
            _, n, acc0, r0, acc1, r1 = c
            pv0, r0 = block(0, i - 1 - n, r0)
            pv1, r1 = block(1, i - 1 - n, r1)
            return more(n + 1, r0, r1), n + 1, acc0 + pv0, r0, acc1 + pv1, r1

        _, _, acc0, _, acc1, _ = lax.while_loop(lambda c: c[0], step, (go, jnp.int32(1), acc0, r0, acc1, r1))
        for ls, acc, gate in zip(heads, (acc0, acc1), gates):
            o_ref[:, ls] = acc
            y_ref[:, ls] = (acc * gate).astype(y_ref.dtype)

    qblk = pl.BlockSpec((blk, LANES), lambda p, i: (i, p))
    return pl.pallas_call(
        body, name="stickbreak_fwd", grid=(SB_HEADS // 2, nq),
        in_specs=[qblk,
                  pl.BlockSpec((s, LANES), lambda p, i: (0, 4 + p)),
                  pl.BlockSpec((s, LANES), lambda p, i: (0, 8 + p)),
                  qblk],
        out_specs=[qblk, qblk],
        out_shape=[jax.ShapeDtypeStruct((s, 512), MXU_DTYPE),
                   jax.ShapeDtypeStruct((s, 512), F32)],
        scratch_shapes=[pltpu.VMEM((2, blk, 2 * blk), F32),
                        pltpu.VMEM((2, 4 * blk, blk), MXU_DTYPE),
                        pltpu.VMEM((2, 4 * blk, blk), F32),
                        pltpu.VMEM((2, blk, 2 * blk), MXU_DTYPE)],
        compiler_params=_params(("arbitrary", "arbitrary")),
    )(qkv, qkv, qkv, sg)


def _sb_bwd(qkv, sg, o, dy):
    s = qkv.shape[0]
    blk = min(SB_BLOCK, s)
    nq = s // blk
    hd = SB_HEAD_DIM
    assert nq <= LANES
    ch = min(SB_CHUNK, blk)

    def body(q_ref, k_ref, v_ref, g_ref, o_ref, dy_ref, dq_ref, dk_ref, dv_ref, dg_ref, dk_scr, dv_scr,
             z_scr, g_scr, lhs_scr, cs_scr, a_scr, dz_scr):
        i = pl.program_id(1)

        @pl.when(i == 0)
        def _():
            dk_scr[...] = jnp.zeros_like(dk_scr)
            dv_scr[...] = jnp.zeros_like(dv_scr)

        row = lax.broadcasted_iota(jnp.int32, (blk, blk), 0)
        col = lax.broadcasted_iota(jnp.int32, (blk, blk), 1)
        lane = lax.broadcasted_iota(jnp.int32, (blk, LANES), 1)
        u_gt = (row > col).astype(MXU_DTYPE)
        u_lt = (row < col).astype(MXU_DTYPE)
        heads = [slice(hh * hd, (hh + 1) * hd) for hh in range(2)]
        qs = [q_ref[:, ls] for ls in heads]
        _, rows_p, rows_d = _sb_pair_tile(i, blk)
        has_prev = i >= 1
        crow = lax.broadcasted_iota(jnp.int32, (ch, blk), 0)
        ccol = lax.broadcasted_iota(jnp.int32, (ch, blk), 1)
        nch = blk // ch
        dobs, kcs = [], []
        for hh, ls in enumerate(heads):
            g = g_ref[:, ls]
            dyv = dy_ref[:, ls]
            sgm = _sigmoid(g)
            dg_ref[:, ls] = (dyv * o_ref[:, ls] * (sgm * (1.0 + g * (1.0 - sgm)))).astype(dg_ref.dtype)
            dobs.append(_mx(dyv * (g * sgm)))
            kcs.append(jnp.concatenate([k_ref[rows_p, ls], k_ref[rows_d, ls]], axis=0))

        def split_rows(hh, c, part, x):
            hi = lax.bitcast_convert_type(lax.bitcast_convert_type(x, jnp.uint32) & jnp.uint32(0xFFFF0000), F32)
            lhs_scr[hh, pl.ds((2 * part) * blk + c * ch, ch), :] = _mx(hi)
            lhs_scr[hh, pl.ds((2 * part + 1) * blk + c * ch, ch), :] = _mx(x - hi)

        def summed_rows(hh, c, part):
            return (cs_scr[hh, pl.ds((2 * part) * blk + c * ch, ch), :]
                    + cs_scr[hh, pl.ds((2 * part + 1) * blk + c * ch, ch), :])

        def logits(hh):
            z_scr[hh] = _dot_nt(qs[hh], kcs[hh])
            vc = jnp.concatenate([v_ref[rows_p, heads[hh]], v_ref[rows_d, heads[hh]]], axis=0)
            g_scr[hh] = _dot_nt(dobs[hh], vc)

        def keep_parts(hh):
            r_d, r_all = [], []
            for c in range(nch):
                rows = pl.ds(c * ch, ch)
                z = z_scr[hh, rows, :]
                l1p = jnp.log(1.0 + jnp.exp(-jnp.abs(z)))
                lk = -(jnp.maximum(z, 0.0) + l1p)
                z_scr[hh, rows, :] = jnp.minimum(z, 0.0) - l1p
                lk_p = jnp.where(has_prev, lk[:, :blk], 0.0)
                lk_d = jnp.where(crow + c * ch > ccol, lk[:, blk:], 0.0)
                split_rows(hh, c, 0, lk_p)
                split_rows(hh, c, 1, lk_d)
                r_d.append(_rowsum(lk_d))
                r_all.append(r_d[c] + _rowsum(lk_p))
            return r_d, jnp.concatenate(r_all, axis=0)

        def weights(hh, r_d):
            g_p = []
            for c in range(nch):
                rows = pl.ds(c * ch, ch)
                lb = z_scr[hh, rows, :]
                a_p = jnp.where(has_prev, jnp.exp(lb[:, :blk] + (summed_rows(hh, c, 0) + r_d[c])), 0.0)
                a_d = jnp.where(crow + c * ch > ccol, jnp.exp(lb[:, blk:] + summed_rows(hh, c, 1)), 0.0)
                a = jnp.concatenate([a_p, a_d], axis=1)
                a_scr[hh, rows, :] = _mx(a)
                gm = g_scr[hh, rows, :] * a
                g_scr[hh, rows, :] = gm
                split_rows(hh, c, 0, gm[:, :blk])
                split_rows(hh, c, 1, gm[:, blk:])
                g_p.append(_rowsum(gm[:, :blk]))
            return g_p

        def logit_grads(hh, pg, g_p):
            for c in range(nch):
                rows = pl.ds(c * ch, ch)
                pre = jnp.concatenate([summed_rows(hh, c, 0) + pg[c * ch:(c + 1) * ch],
                                       summed_rows(hh, c, 1) + (pg[c * ch:(c + 1) * ch] + g_p[c])], axis=1)
                gm = g_scr[hh, rows, :]
                dz = gm - (gm + pre) * jnp.exp(z_scr[hh, rows, :])
                dz_p = jnp.where(has_prev, dz[:, :blk], 0.0)
                dz_d = jnp.where(crow + c * ch > ccol, dz[:, blk:], 0.0)
                dz_scr[hh, rows, :] = _mx(jnp.concatenate([dz_p, dz_d], axis=1))

        def products(hh, acc):
            ls = heads[hh]
            dk_scr[hh, rows_p, :] += _dot_tn(dz_scr[hh, :, 0:blk], qs[hh])
            dk_scr[hh, rows_d, :] += _dot_tn(dz_scr[hh, :, blk:2 * blk], qs[hh])
            dv_scr[hh, rows_p, :] += _dot_tn(a_scr[hh, :, 0:blk], dobs[hh])
            dv_scr[hh, rows_d, :] += _dot_tn(a_scr[hh, :, blk:2 * blk], dobs[hh])
            dq_ref[:, ls] = ((acc + _dot(dz_scr[hh], kcs[hh])) * SB_SCALE).astype(dq_ref.dtype)

        def suffix_sums(hh):
            cs_scr[hh] = _dot(lhs_scr[hh], u_gt)

        def prefix_sums(hh):
            cs_scr[hh] = _dot(lhs_scr[hh], u_lt)

        def more(n, r0, r1):
            return jnp.logical_and(n < i, jnp.max(jnp.maximum(r0, r1)) > -SB_SKIP)

        logits(0)
        logits(1)
        rd0, ra0 = keep_parts(0)
        suffix_sums(0)
        rd1, ra1 = keep_parts(1)
        suffix_sums(1)
        go = more(jnp.int32(1), ra0, ra1)
        gp0 = weights(0, rd0)
        prefix_sums(0)
        gp1 = weights(1, rd1)
        prefix_sums(1)

        def scan_block(hh, j, r, rmat):
            start = pl.multiple_of(j * blk, blk)
            _, lk = _sb_logits(qs[hh], k_ref[pl.ds(start, blk), heads[hh]], None)
            return r + _rowsum(lk), jnp.where(lane == j, r, rmat)

        def scan_step(c):
            _, n, r0, rmat0, r1, rmat1 = c
            r0, rmat0 = scan_block(0, i - 1 - n, r0, rmat0)
            r1, rmat1 = scan_block(1, i - 1 - n, r1, rmat1)
            return more(n + 1, r0, r1), n + 1, r0, rmat0, r1, rmat1

        zmat = jnp.zeros((blk, LANES), F32)
        _, n, _, rmat0, _, rmat1 = lax.while_loop(lambda c: c[0], scan_step,
                                                  (go, jnp.int32(1), ra0, zmat, ra1, zmat))
        rmats = (rmat0, rmat1)

        def block(hh, j, pg):
            ls = heads[hh]
            start = pl.multiple_of(j * blk, blk)
            k = k_ref[pl.ds(start, blk), ls]
            lb, lk = _sb_logits(qs[hh], k, None)
            r = _rowsum(jnp.where(lane == j, rmats[hh], 0.0))
            a = _sb_weights(lb, lk, r, u_gt, None)
            gm = _dot_nt(dobs[hh], v_ref[pl.ds(start, blk), ls]) * a
            dzb = _mx(gm - (gm + (pg + _split_dot(gm, u_lt))) * jnp.exp(lb))
            dk_scr[hh, pl.ds(start, blk), :] += _dot_tn(dzb, qs[hh])
            dv_scr[hh, pl.ds(start, blk), :] += _dot_tn(_mx(a), dobs[hh])
            return _dot(dzb, k), pg + _rowsum(gm)

        def step(t, c):
            acc0, pg0, acc1, pg1 = c
            dq0, pg0 = block(0, i - n + t, pg0)
            dq1, pg1 = block(1, i - n + t, pg1)
            return acc0 + dq0, pg0, acc1 + dq1, pg1

        zero = jnp.zeros((blk, 1), F32)
        zacc = jnp.zeros((blk, hd), F32)
        acc0, pg0, acc1, pg1 = lax.fori_loop(0, n - 1, step, (zacc, zero, zacc, zero))
        logit_grads(0, pg0, gp0)
        logit_grads(1, pg1, gp1)
        products(0, acc0)
        products(1, acc1)

        @pl.when(i == nq - 1)
        def _():
            for hh in range(2):
                ls = slice(hh * hd, (hh + 1) * hd)
                dk_ref[:, ls] = dk_scr[hh].astype(dk_ref.dtype)
                dv_ref[:, ls] = dv_scr[hh].astype(dv_ref.dtype)

    qblk = lambda c0: pl.BlockSpec((blk, LANES), lambda p, i: (i, c0 + p))
    full = lambda c0: pl.BlockSpec((s, LANES), lambda p, i: (0, c0 + p))
    half = jax.ShapeDtypeStruct((s, 512), MXU_DTYPE)
    return pl.pallas_call(
        body, name="stickbreak_bwd", grid=(SB_HEADS // 2, nq),
        in_specs=[qblk(0), full(4), full(8), qblk(0), qblk(0), qblk(4)],
        out_specs=[qblk(0), full(0), full(0), qblk(0)],
        out_shape=[half, half, half, half],
        scratch_shapes=[pltpu.VMEM((2, s, hd), F32), pltpu.VMEM((2, s, hd), F32),
                        pltpu.VMEM((2, blk, 2 * blk), F32),
                        pltpu.VMEM((2, blk, 2 * blk), F32),
                        pltpu.VMEM((2, 4 * blk, blk), MXU_DTYPE),
                        pltpu.VMEM((2, 4 * blk, blk), F32),
                        pltpu.VMEM((2, blk, 2 * blk), MXU_DTYPE),
                        pltpu.VMEM((2, blk, 2 * blk), MXU_DTYPE)],
        compiler_params=_params(("arbitrary", "arbitrary")),
    )(qkv, qkv, qkv, sg, o, dy)


def _layer_fwd(layer, x, mod, norm_g, w_in_g, w_out_g, tables):
    shift, scale1p, gate = mod[0:1], 1.0 + mod[1:2], mod[2:3]
    ret, qkv, sg = _ln_proj(x, shift, scale1p, norm_g, w_in_g, layer)
    y_r, states = _retention_fwd(ret, tables)
    y_s, o_s = _sb_fwd(qkv, sg)
    x_next = _out_proj(x, gate, y_r, y_s, w_out_g, layer)
    saved = (x, shift, scale1p, gate, ret, qkv, sg, y_r, states, y_s, o_s)
    return x_next, saved


def _layer_bwd(layer, me, dx_out, saved, norm_g, w_in_g, w_out_g, tables):
    x, shift, scale1p, gate, ret, qkv, sg, y_r, states, y_s, o_s = saved
    dy, dw_out, dgate = _out_proj_bwd(dx_out, gate, y_r, y_s, w_out_g, layer)
    d_ret = _retention_bwd(ret, states, dy, tables)
    d_q, d_k, d_v, d_g = _sb_bwd(qkv, sg, o_s, dy)
    dproj = jnp.concatenate([d_ret, d_q, d_k, d_v, d_g], axis=1)
    dx, dshift, dscale, dnorm_g = _in_proj_bwd_x(x, dx_out, dproj, shift, scale1p, norm_g, w_in_g, layer)
    dwo_parts = _mx(dw_out.reshape(N_DEV, D_MODEL // N_DEV, D_MODEL))
    r_in, r_out = _in_proj_bwd_w(me, x, dproj, shift, scale1p, norm_g, dwo_parts)
    dmod = jnp.concatenate([dshift, dscale, dgate], axis=1)
    return dx, r_in, r_out, dmod, dnorm_g


MESH_IDS = pl.DeviceIdType.MESH
N_PEERS = N_DEV - 1
HBM_SPEC = pl.BlockSpec(memory_space=pl.ANY)


def _my_place():
    return lax.axis_index("x"), lax.axis_index("y"), lax.axis_index("c")


def _linear(px, py, pc):
    return 4 * px + 2 * py + pc


def _all_gather(blocks):
    n_arr = len(blocks)

    def body(*refs):
        x_refs = refs[:n_arr]
        out_refs = refs[n_arr:2 * n_arr]
        send_sems, recv_sems, local_sems = refs[2 * n_arr:]
        x, y, c = _my_place()
        me, sibling = (x, y, c), (x, y, 1 - c)
        chips = [(1 - x, y), (x, 1 - y), (1 - x, 1 - y)]

        def rows(a, place):
            m = x_refs[a].shape[0]
            return out_refs[a].at[pl.ds(_linear(*place) * m, m), :]

        def copy(a, k, block, to, src=None):
            return pltpu.make_async_remote_copy(
                src_ref=rows(a, block) if src is None else src, dst_ref=rows(a, block),
                send_sem=send_sems.at[a * N_PEERS + k], recv_sem=recv_sems.at[a * N_PEERS + k],
                device_id=to, device_id_type=MESH_IDS)

        mine = [pltpu.make_async_copy(x_refs[a], rows(a, me), local_sems.at[a]) for a in range(n_arr)]
        for cp in mine:
            cp.start()
        first = []
        for a in range(n_arr):
            first.append(copy(a, 0, me, sibling, src=x_refs[a]))
            first += [copy(a, 1 + j, me, (*chip, c), src=x_refs[a]) for j, chip in enumerate(chips)]
        for cp in first:
            cp.start()
        passed = []
        for j, chip in enumerate(chips):
            for a in range(n_arr):
                copy(a, 1 + j, (*chip, c), me).wait_recv()
                fwd = copy(a, 4 + j, (*chip, c), sibling)
                fwd.start()
                passed.append(fwd)
        for a in range(n_arr):
            copy(a, 0, sibling, me).wait_recv()
            for j, chip in enumerate(chips):
                copy(a, 4 + j, (*chip, 1 - c), me).wait_recv()
        for cp in first + passed:
            cp.wait_send()
        for cp in mine:
            cp.wait()

    return pl.pallas_call(
        body, name="all_gather",
        out_shape=[jax.ShapeDtypeStruct((N_DEV * b.shape[0], b.shape[1]), b.dtype) for b in blocks],
        in_specs=[HBM_SPEC] * n_arr, out_specs=[HBM_SPEC] * n_arr,
        scratch_shapes=[pltpu.SemaphoreType.DMA((n_arr * N_PEERS,)),
                        pltpu.SemaphoreType.DMA((n_arr * N_PEERS,)),
                        pltpu.SemaphoreType.DMA((n_arr,))],
    )(*blocks)


def _ada_fwd(c_all, w_ada, b_cols):
    cols = w_ada.shape[2]

    def body(c_ref, w_ref, b_ref, ca_ref, mod_ref):
        cv = c_ref[...]
        ca = cv * _sigmoid(cv)
        ca_ref[...] = ca
        cb = _mx(jnp.concatenate([ca, ca], axis=0))
        for l in range(DEPTH):
            mod_ref[l * N_DEV:(l + 1) * N_DEV, :] = _dot(cb, _mx(w_ref[l]))[0:N_DEV] + b_ref[l]

    return pl.pallas_call(
        body, name="ada_fwd",
        out_shape=[jax.ShapeDtypeStruct((N_DEV, D_MODEL), F32),
                   jax.ShapeDtypeStruct((DEPTH * N_DEV, cols), F32)],
        compiler_params=_params(),
    )(c_all, w_ada, b_cols)


def _ada_bwd(c_act_t, dmod_cols):
    cols = dmod_cols.shape[2]

    def body(ca_ref, dm_ref, o_ref):
        ca = _mx(ca_ref[...]).astype(F32)
        for l in range(DEPTH):
            o_ref[l] = jnp.dot(ca, _mx(dm_ref[l]).astype(F32),
                               precision=lax.Precision.HIGHEST, preferred_element_type=F32)

    return pl.pallas_call(
        body, name="ada_bwd",
        out_shape=jax.ShapeDtypeStruct((DEPTH, D_MODEL, cols), F32),
        compiler_params=_params(),
    )(c_act_t, dmod_cols)


def _adamw_store(g, w_ref, m_ref, v_ref, g_ref, d_ref, mo_ref, vo_ref):
    m2 = ADAM_B1 * m_ref[...] + (1.0 - ADAM_B1) * g
    v2 = ADAM_B2 * v_ref[...] + (1.0 - ADAM_B2) * (g * g)
    m_hat = m2 / (1.0 - ADAM_B1 ** ADAM_STEP)
    v_hat = v2 / (1.0 - ADAM_B2 ** ADAM_STEP)
    g_ref[...] = g
    d_ref[...] = -ADAM_LR * (m_hat / (jnp.sqrt(v_hat) + ADAM_EPS) + ADAM_WD * w_ref[...])
    mo_ref[...] = m2
    vo_ref[...] = v2


def _slab_sum(p_ref):
    g = p_ref[0].astype(F32)
    for sl in range(1, p_ref.shape[0]):
        g = g + p_ref[sl].astype(F32)
    return g


def _sum_adamw_layers(parts, w, m, v):
    n_slab, rows, cols = parts[0].shape
    tr = min(256, rows)
    nt = rows // tr

    def body(p0_ref, p1_ref, w_ref, m_ref, v_ref, g_ref, d_ref, mo_ref, vo_ref):
        for l, p_ref in enumerate((p0_ref, p1_ref)):
            @pl.when(pl.program_id(0) == l)
            def _():
                _adamw_store(_slab_sum(p_ref), w_ref, m_ref, v_ref, g_ref, d_ref, mo_ref, vo_ref)

    p_specs = [pl.BlockSpec((n_slab, tr, cols), lambda l, i: (0, i * (1 - l) + (nt - 1) * l, 0)),
               pl.BlockSpec((n_slab, tr, cols), lambda l, i: (0, i * l, 0))]
    blk = pl.BlockSpec((None, tr, cols), lambda l, i: (l, i, 0))
    shp = jax.ShapeDtypeStruct((DEPTH, rows, cols), F32)
    return pl.pallas_call(
        body, name="sum_adamw_layers", grid=(DEPTH, nt),
        in_specs=p_specs + [blk, blk, blk],
        out_specs=[blk, blk, blk, blk],
        out_shape=[shp, shp, shp, shp],
        compiler_params=_params(("arbitrary", "arbitrary")),
    )(parts[0], parts[1], w, m, v)


def _sum_adamw(parts, w, m, v):
    n_slab, rows, cols = parts.shape
    tr = min(256, rows)

    def body(p_ref, w_ref, m_ref, v_ref, g_ref, d_ref, mo_ref, vo_ref):
        _adamw_store(_slab_sum(p_ref), w_ref, m_ref, v_ref, g_ref, d_ref, mo_ref, vo_ref)

    blk = pl.BlockSpec((tr, cols), lambda i: (i, 0))
    shp = jax.ShapeDtypeStruct((rows, cols), F32)
    return pl.pallas_call(
        body, name="sum_adamw", grid=(rows // tr,),
        in_specs=[pl.BlockSpec((n_slab, tr, cols), lambda i: (0, i, 0)), blk, blk, blk],
        out_specs=[blk, blk, blk, blk],
        out_shape=[shp, shp, shp, shp],
        compiler_params=_params(("arbitrary",)),
    )(parts, w, m, v)


SMALL_ROWS = 16


def kernel(x, c, norm_g, w_ada, b_ada, w_in, w_out, final_g, loss_target, m_norm_g, m_w_ada, m_b_ada, m_w_in, m_w_out, m_final_g, v_norm_g, v_w_ada, v_b_ada, v_w_in, v_w_out, v_final_g):
    me = _linear(*_my_place())
    in_cols = w_in.shape[2]
    out_rows = w_out.shape[1]
    ada_cols = w_ada.shape[2]

    g_in, g_out, g_c = _all_gather([
        _mx(w_in).reshape(DEPTH * D_MODEL, in_cols),
        _mx(w_out).reshape(DEPTH * out_rows, D_MODEL),
        jnp.broadcast_to(c, (8, D_MODEL))])
    w_in_g = g_in.reshape(N_DEV, DEPTH, D_MODEL, in_cols)
    w_out_g = g_out.reshape(N_DEV, DEPTH, out_rows, D_MODEL)
    c_all = g_c.reshape(N_DEV, 8, D_MODEL)[:, 0]

    b_cols = lax.dynamic_slice_in_dim(b_ada, me * ada_cols, ada_cols, axis=1)[:, None, :]
    c_act, mod_cols = _ada_fwd(c_all, w_ada, b_cols)
    (g_mod,) = _all_gather([mod_cols])
    g_mod = g_mod.reshape(N_DEV, DEPTH, N_DEV, ada_cols)
    mod = lax.dynamic_index_in_dim(g_mod, me, axis=2, keepdims=False)
    mod = mod.transpose(1, 0, 2).reshape(DEPTH, 3, D_MODEL)

    tables = _ret_tables(x.shape[1])
    h = x[0]
    saved = []
    for l in range(DEPTH):
        h, sv = _layer_fwd(l, h, mod[l], norm_g[l:l + 1], w_in_g, w_out_g, tables)
        saved.append(sv)
    dx, loss_part, dfg = _final_loss(h, final_g[None], loss_target[0])
    r_in, r_out, small = [None] * DEPTH, [None] * DEPTH, [None] * DEPTH
    for l in reversed(range(DEPTH)):
        dx, r_in[l], r_out[l], dmod, dng = _layer_bwd(l, me, dx, saved[l], norm_g[l:l + 1], w_in_g, w_out_g, tables)
        small[l] = (dmod.reshape(3, D_MODEL), dng)

    pad = jnp.zeros((SMALL_ROWS - 10, D_MODEL), F32)
    small_block = jnp.concatenate([small[0][0], small[1][0], small[0][1], small[1][1], dfg,
                                   jnp.broadcast_to(loss_part, (1, D_MODEL)), pad], axis=0)
    (g_small,) = _all_gather([small_block])
    g_small = g_small.reshape(N_DEV, SMALL_ROWS, D_MODEL)

    def small_pack(b, n, f, fill):
        return jnp.concatenate([b.reshape(6, D_MODEL), n, f[None],
                                jnp.full((SMALL_ROWS - 9, D_MODEL), fill, F32)], axis=0)

    s_g, s_d, s_m, s_v = _sum_adamw(g_small, small_pack(b_ada, norm_g, final_g, 0.0),
                                    small_pack(m_b_ada, m_norm_g, m_final_g, 0.0),
                                    small_pack(v_b_ada, v_norm_g, v_final_g, 1.0))
    loss = s_g[9, 0]

    def small_unpack(a):
        return a[0:6].reshape(DEPTH, 3 * D_MODEL), a[6:8], a[8]

    dmod_all = g_small[:, 0:6].reshape(N_DEV, DEPTH, 3 * D_MODEL).transpose(1, 0, 2)
    dmod_cols = lax.dynamic_slice_in_dim(dmod_all, me * ada_cols, ada_cols, axis=2)
    g_ada = _ada_bwd(c_act.T, dmod_cols).reshape(1, DEPTH * D_MODEL, ada_cols)
    ada = _sum_adamw(g_ada, *[a.reshape(DEPTH * D_MODEL, ada_cols) for a in (w_ada, m_w_ada, v_w_ada)])
    ada = [a.reshape(DEPTH, D_MODEL, ada_cols) for a in ada]

    win = _sum_adamw_layers(r_in, w_in, m_w_in, v_w_in)
    wout = _sum_adamw_layers(r_out, w_out, m_w_out, v_w_out)

    outs = [loss, dx[None]]
    for k in range(4):
        b, n, f = small_unpack((s_g, s_d, s_m, s_v)[k])
        outs += [n, ada[k], b, win[k], wout[k], f]
    return tuple(outs)
```

```python
import functools

import jax
import jax.numpy as jnp
from jax import lax
from jax.experimental import pallas as pl
from jax.experimental.pallas import tpu as pltpu

F32 = jnp.float32
MXU_DTYPE = jnp.bfloat16

D_MODEL = 1024
DEPTH = 2
N_DEV = 8
CHUNK = 64
D_RET = 512
D_SB = 512
RET_HEADS = 4
RET_HEAD_DIM = 128
SB_HEADS = 8
SB_HEAD_DIM = 64
D_IN = 4096
ROPE_BASE = 10000.0
EPS = 1e-6
SB_SCALE = SB_HEAD_DIM ** -0.5
RET_KSCALE = RET_HEAD_DIM ** -0.5

ADAM_LR = 0.001
ADAM_B1 = 0.9
ADAM_B2 = 0.999
ADAM_EPS = 1e-08
ADAM_WD = 0.01
ADAM_STEP = 10

V7X_VMEM_BYTES = 64 * 2 ** 20
VMEM_LIMIT = V7X_VMEM_BYTES - 8 * 2 ** 20
LANES = 128

_NT = (((1,), (1,)), ((), ()))
_TN = (((0,), (0,)), ((), ()))


def _dot(a, b):
    return jnp.dot(a, b, preferred_element_type=F32)


def _dot_nt(a, b):
    return lax.dot_general(a, b, _NT, preferred_element_type=F32)


def _dot_tn(a, b):
    return lax.dot_general(a, b, _TN, preferred_element_type=F32)


def _mx(x):
    return x.astype(MXU_DTYPE)


def _sigmoid(x):
    return 1.0 / (1.0 + jnp.exp(-x))


def _params(sem=None):
    return pltpu.CompilerParams(dimension_semantics=sem, vmem_limit_bytes=VMEM_LIMIT)


def _row_tile(s):
    return min(512, s)


def _w_in_spec(w_in_g, layer):
    return pl.BlockSpec((N_DEV, None) + w_in_g.shape[2:], lambda i: (0, layer, 0, 0))


def _w_out_spec(w_out_g, layer):
    return pl.BlockSpec((N_DEV, None) + w_out_g.shape[2:], lambda i: (0, layer, 0, 0))


def _ln_proj(x, shift, scale1p, g, w_in_g, layer, gather=()):
    s = x.shape[0]
    ts = _row_tile(s)
    ns = s // ts
    n_g = len(gather)

    def body(x_ref, sh_ref, sc_ref, g_ref, w_ref, *rest):
        ret_ref, qkv_ref, sg_ref = rest[n_g:n_g + 3]
        if n_g:
            start, forward, finish = _gather_plan(rest[:n_g], rest[n_g + 3:2 * n_g + 3], *rest[2 * n_g + 3:])
            i = pl.program_id(0)
            pl.when(i == 0)(start)
            pl.when(i == max(ns - 4, 0))(forward)
        xv = x_ref[...]
        rstd = lax.rsqrt(jnp.mean(xv * xv, axis=-1, keepdims=True) + EPS)
        h = (xv * rstd * g_ref[...]) * sc_ref[...] + sh_ref[...]
        hb = _mx(h)
        for n in range(4):
            ret_ref[:, n * 512:(n + 1) * 512] = _dot(hb, w_ref[n])
        qkv_ref[:, 0:512] = _mx(_dot(hb, w_ref[4]) * SB_SCALE)
        qkv_ref[:, 512:1024] = _mx(_dot(hb, w_ref[5]))
        qkv_ref[:, 1024:1536] = _mx(_dot(hb, w_ref[6]))
        sg_ref[...] = _dot(hb, w_ref[7])
        if n_g:
            pl.when(i == ns - 1)(finish)

    vec = pl.BlockSpec((1, D_MODEL), lambda i: (0, 0))
    return pl.pallas_call(
        body, name="ln_proj_gather" if n_g else "ln_proj", grid=(ns,),
        in_specs=[pl.BlockSpec((ts, D_MODEL), lambda i: (i, 0)), vec, vec, vec,
                  _w_in_spec(w_in_g, layer)] + [HBM_SPEC] * n_g,
        out_specs=[pl.BlockSpec((ts, 2048), lambda i: (i, 0)),
                   pl.BlockSpec((ts, 1536), lambda i: (i, 0)),
                   pl.BlockSpec((ts, 512), lambda i: (i, 0))] + [HBM_SPEC] * n_g,
        out_shape=[jax.ShapeDtypeStruct((s, 2048), F32),
                   jax.ShapeDtypeStruct((s, 1536), MXU_DTYPE),
                   jax.ShapeDtypeStruct((s, 512), F32)] + _gathered_shapes(gather),
        scratch_shapes=_gather_sems(n_g) if n_g else (),
        compiler_params=_params(("arbitrary",)),
    )(x, shift, scale1p, g, w_in_g, *gather)


def _w_out_halves(w_ref):
    half = N_DEV // 2
    return (w_ref[0:half].reshape(D_RET, D_MODEL), w_ref[half:N_DEV].reshape(D_SB, D_MODEL))


def _out_proj(x, gate, y_r, y_s, w_out_g, layer):
    s = x.shape[0]
    ts = _row_tile(s)

    def body(x_ref, gate_ref, yr_ref, ys_ref, w_ref, o_ref):
        w_r, w_s = _w_out_halves(w_ref)
        t = _dot(yr_ref[...], w_r) + _dot(ys_ref[...], w_s)
        o_ref[...] = x_ref[...] + gate_ref[...] * t

    return pl.pallas_call(
        body, name="out_proj", grid=(s // ts,),
        in_specs=[pl.BlockSpec((ts, D_MODEL), lambda i: (i, 0)),
                  pl.BlockSpec((1, D_MODEL), lambda i: (0, 0)),
                  pl.BlockSpec((ts, 512), lambda i: (i, 0)),
                  pl.BlockSpec((ts, 512), lambda i: (i, 0)),
                  _w_out_spec(w_out_g, layer)],
        out_specs=pl.BlockSpec((ts, D_MODEL), lambda i: (i, 0)),
        out_shape=jax.ShapeDtypeStruct((s, D_MODEL), F32),
        compiler_params=_params(("arbitrary",)),
    )(x, gate, y_r, y_s, w_out_g)


def _final_loss(x, fg, target):
    s = x.shape[0]
    ts = _row_tile(s)

    def body(x_ref, fg_ref, t_ref, dx_ref, loss_ref, dfg_ref):
        i = pl.program_id(0)

        @pl.when(i == 0)
        def _():
            loss_ref[...] = jnp.zeros_like(loss_ref)
            dfg_ref[...] = jnp.zeros_like(dfg_ref)

        xv = x_ref[...]
        fgv = fg_ref[...]
        rstd = lax.rsqrt(jnp.mean(xv * xv, axis=-1, keepdims=True) + EPS)
        xn = xv * rstd
        err = xn * fgv - t_ref[...]
        tok = jnp.mean(err * err, axis=-1, keepdims=True)
        loss_ref[...] += 0.5 * jnp.sum(tok, axis=0, keepdims=True)
        dy = err * (1.0 / D_MODEL)
        dfg_ref[...] += jnp.sum(dy * xn, axis=0, keepdims=True)
        dxn = dy * fgv
        dx_ref[...] = rstd * (dxn - xn * jnp.mean(dxn * xn, axis=-1, keepdims=True))

    return pl.pallas_call(
        body, name="final_loss", grid=(s // ts,),
        in_specs=[pl.BlockSpec((ts, D_MODEL), lambda i: (i, 0)),
                  pl.BlockSpec((1, D_MODEL), lambda i: (0, 0)),
                  pl.BlockSpec((ts, D_MODEL), lambda i: (i, 0))],
        out_specs=[pl.BlockSpec((ts, D_MODEL), lambda i: (i, 0)),
                   pl.BlockSpec((1, 1), lambda i: (0, 0)),
                   pl.BlockSpec((1, D_MODEL), lambda i: (0, 0))],
        out_shape=[jax.ShapeDtypeStruct((s, D_MODEL), F32),
                   jax.ShapeDtypeStruct((1, 1), F32),
                   jax.ShapeDtypeStruct((1, D_MODEL), F32)],
        compiler_params=_params(("arbitrary",)),
    )(x, fg, target)


def _out_proj_bwd(dx_out, gate, y_r, y_s, w_out_g, layer):
    s = dx_out.shape[0]
    ts = _row_tile(s)

    def body(dx_ref, gate_ref, yr_ref, ys_ref, w_ref, dy_ref, dw_ref, dgate_ref):
        i = pl.program_id(0)

        @pl.when(i == 0)
        def _():
            dw_ref[...] = jnp.zeros_like(dw_ref)
            dgate_ref[...] = jnp.zeros_like(dgate_ref)

        dxv = dx_ref[...]
        dt = _mx(dxv * gate_ref[...])
        yr = yr_ref[...]
        ys = ys_ref[...]
        w_r, w_s = _w_out_halves(w_ref)
        dy_ref[:, 0:512] = _dot_nt(dt, w_r)
        dy_ref[:, 512:1024] = _dot_nt(dt, w_s)
        dw_ref[0:512, :] += _dot_tn(yr, dt)
        dw_ref[512:1024, :] += _dot_tn(ys, dt)
        t = _dot(yr, w_r) + _dot(ys, w_s)
        dgate_ref[...] += jnp.sum(dxv * t, axis=0, keepdims=True)

    return pl.pallas_call(
        body, name="out_proj_bwd", grid=(s // ts,),
        in_specs=[pl.BlockSpec((ts, D_MODEL), lambda i: (i, 0)),
                  pl.BlockSpec((1, D_MODEL), lambda i: (0, 0)),
                  pl.BlockSpec((ts, 512), lambda i: (i, 0)),
                  pl.BlockSpec((ts, 512), lambda i: (i, 0)),
                  _w_out_spec(w_out_g, layer)],
        out_specs=[pl.BlockSpec((ts, D_MODEL), lambda i: (i, 0)),
                   pl.BlockSpec((D_MODEL, D_MODEL), lambda i: (0, 0)),
                   pl.BlockSpec((1, D_MODEL), lambda i: (0, 0))],
        out_shape=[jax.ShapeDtypeStruct((s, D_MODEL), F32),
                   jax.ShapeDtypeStruct((D_MODEL, D_MODEL), F32),
                   jax.ShapeDtypeStruct((1, D_MODEL), F32)],
        compiler_params=_params(("arbitrary",)),
    )(dx_out, gate, y_r, y_s, w_out_g)


def _in_proj_bwd_x(x, dx_out, dproj, shift, scale1p, g, w_in_g, layer):
    s = x.shape[0]
    ts = min(256, s)

    def body(x_ref, dxo_ref, dp_ref, sc_ref, g_ref, w_ref, dx_ref, dsh_ref, dsc_ref, dg_ref):
        i = pl.program_id(0)

        @pl.when(i == 0)
        def _():
            dsh_ref[...] = jnp.zeros_like(dsh_ref)
            dsc_ref[...] = jnp.zeros_like(dsc_ref)
            dg_ref[...] = jnp.zeros_like(dg_ref)

        nb = D_IN // N_DEV
        dh = _dot_nt(dp_ref[:, 0:nb], w_ref[0])
        for n in range(1, N_DEV):
            dh += _dot_nt(dp_ref[:, n * nb:(n + 1) * nb], w_ref[n])
        xv = x_ref[...]
        gv = g_ref[...]
        scv = sc_ref[...]
        rstd = lax.rsqrt(jnp.mean(xv * xv, axis=-1, keepdims=True) + EPS)
        xn = xv * rstd
        dsh_ref[...] += jnp.sum(dh, axis=0, keepdims=True)
        dsc_ref[...] += jnp.sum(dh * (xn * gv), axis=0, keepdims=True)
        dhs = dh * scv
        dg_ref[...] += jnp.sum(dhs * xn, axis=0, keepdims=True)
        dxn = dhs * gv
        dx_ref[...] = rstd * (dxn - xn * jnp.mean(dxn * xn, axis=-1, keepdims=True)) + dxo_ref[...]

    del shift
    vec = pl.BlockSpec((1, D_MODEL), lambda i: (0, 0))
    return pl.pallas_call(
        body, name="in_proj_bwd_x", grid=(s // ts,),
        in_specs=[pl.BlockSpec((ts, D_MODEL), lambda i: (i, 0)),
                  pl.BlockSpec((ts, D_MODEL), lambda i: (i, 0)),
                  pl.BlockSpec((ts, D_IN), lambda i: (i, 0)),
                  vec, vec,
                  _w_in_spec(w_in_g, layer)],
        out_specs=[pl.BlockSpec((ts, D_MODEL), lambda i: (i, 0)), vec, vec, vec],
        out_shape=[jax.ShapeDtypeStruct((s, D_MODEL), F32),
                   jax.ShapeDtypeStruct((1, D_MODEL), F32),
                   jax.ShapeDtypeStruct((1, D_MODEL), F32),
                   jax.ShapeDtypeStruct((1, D_MODEL), F32)],
        compiler_params=_params(("arbitrary",)),
    )(x, dx_out, dproj, scale1p, g, w_in_g)


def _in_proj_bwd_w(me, x, dproj, shift, scale1p, g, dwo_parts):
    s = x.shape[0]
    ts = min(2 * _row_tile(s), s)
    ns = s // ts
    nb = D_IN // N_DEV
    last = N_DEV - 1

    def body(me_ref, x_ref, dp_ref, sh_ref, sc_ref, g_ref, dwo_ref, rin_ref, rout_ref,
             acc, stage, h_t, in_send, in_recv, out_send, out_recv, local_sems):
        del me_ref
        t = pl.program_id(0)
        i = pl.program_id(1)
        px, py, pc = _my_place()
        mine = _linear(px, py, pc)

        def out_copy(r):
            peer = (1 - px if r & 4 else px, 1 - py if r & 2 else py, 1 - pc if r & 1 else pc)
            return pltpu.make_async_remote_copy(
                src_ref=dwo_ref.at[_linear(*peer)], dst_ref=rout_ref.at[mine],
                send_sem=out_send.at[r - 1], recv_sem=out_recv.at[r - 1],
                device_id=peer, device_id_type=MESH_IDS)

        def in_copy(step):
            dest = (mine + 1 + step) % N_DEV
            return pltpu.make_async_remote_copy(
                src_ref=stage.at[step % 2], dst_ref=rin_ref.at[mine],
                send_sem=in_send.at[step], recv_sem=in_recv.at[step],
                device_id=(dest // 4, (dest // 2) % 2, dest % 2), device_id_type=MESH_IDS)

        own_out = pltpu.make_async_copy(dwo_ref.at[mine], rout_ref.at[mine], local_sems.at[0])
        own_in = pltpu.make_async_copy(stage.at[last % 2], rin_ref.at[mine], local_sems.at[1])

        @pl.when(jnp.logical_and(t == 0, i == 0))
        def _():
            own_out.start()
            for r in range(1, N_DEV):
                out_copy(r).start()

        @pl.when(i == 0)
        def _():
            acc[...] = jnp.zeros_like(acc)

        @pl.when(t == 0)
        def _():
            xv = x_ref[...]
            rstd = lax.rsqrt(jnp.mean(xv * xv, axis=-1, keepdims=True) + EPS)
            h = (xv * rstd * g_ref[...]) * sc_ref[...] + sh_ref[...]
            h_t[i] = _mx(h.T)

        acc[...] += _dot(h_t[i], dp_ref[...])

        @pl.when(i == ns - 1)
        def _():
            @pl.when(t >= 2)
            def _():
                in_copy(t - 2).wait_send()

            stage[t % 2] = acc[...].astype(stage.dtype)

            @pl.when(t < last)
            def _():
                in_copy(t).start()

            @pl.when(t == last)
            def _():
                own_in.start()
                in_copy(last - 1).wait_send()
                for step in range(last):
                    in_copy(step).wait_recv()
                for r in range(1, N_DEV):
                    out_copy(r).wait_recv()
                    out_copy(r).wait_send()
                own_out.wait()
                own_in.wait()

    vec = pl.BlockSpec((1, D_MODEL), lambda t, i, me_ref: (0, 0))
    rows = dwo_parts.shape[1]
    return pl.pallas_call(
        body, name="in_proj_bwd_w",
        grid_spec=pltpu.PrefetchScalarGridSpec(
            num_scalar_prefetch=1, grid=(N_DEV, ns),
            in_specs=[pl.BlockSpec((ts, D_MODEL), lambda t, i, me_ref: (jnp.where(t == 0, i, ns - 1), 0)),
                      pl.BlockSpec((ts, nb), lambda t, i, me_ref: (i, (me_ref[0] + 1 + t) % N_DEV)),
                      vec, vec, vec, HBM_SPEC],
            out_specs=[HBM_SPEC, HBM_SPEC],
            scratch_shapes=[pltpu.VMEM((D_MODEL, nb), F32),
                            pltpu.VMEM((2, D_MODEL, nb), MXU_DTYPE),
                            pltpu.VMEM((ns, D_MODEL, ts), MXU_DTYPE),
                            pltpu.SemaphoreType.DMA((N_PEERS,)), pltpu.SemaphoreType.DMA((N_PEERS,)),
                            pltpu.SemaphoreType.DMA((N_PEERS,)), pltpu.SemaphoreType.DMA((N_PEERS,)),
                            pltpu.SemaphoreType.DMA((2,))]),
        out_shape=[jax.ShapeDtypeStruct((N_DEV, D_MODEL, nb), MXU_DTYPE),
                   jax.ShapeDtypeStruct((N_DEV, rows, D_MODEL), MXU_DTYPE)],
        compiler_params=_params(("arbitrary", "arbitrary")),
    )(jnp.reshape(me, (1,)).astype(jnp.int32), x, dproj, shift, scale1p, g, dwo_parts)


RET_TILE = 256


def _ret_tables(s):
    t = min(RET_TILE, s)
    half = RET_HEAD_DIM // 2
    pos = jnp.arange(s, dtype=F32)
    inv = ROPE_BASE ** (-jnp.arange(half, dtype=F32) / half)
    ang = pos[:, None] * inv[None, :]
    cos, sin = jnp.cos(ang), jnp.sin(ang)
    cos2 = jnp.concatenate([cos, cos], axis=1)
    sin2 = jnp.concatenate([-sin, sin], axis=1)
    lg = jnp.log1p(-(2.0 ** (-5.0 - jnp.arange(RET_HEADS, dtype=F32))))[:, None, None]
    n = jnp.arange(t)
    dist = (n[:, None] - n[None, :]).astype(F32)[None]
    cn = (n // CHUNK)[:, None]
    cm = (n // CHUNK)[None, :]
    mask = jnp.where((cn == cm)[None], jnp.exp(jnp.abs(dist) * lg),
                     jnp.where((cm < cn)[None], jnp.exp(dist * lg), 0.0))
    nf = n.astype(F32)[None, :, None]
    dq = jnp.broadcast_to(jnp.exp((nf + 1.0) * lg), (RET_HEADS, t, LANES))
    dk = jnp.broadcast_to(jnp.exp((t - 1.0 - nf) * lg), (RET_HEADS, t, LANES))
    gt = jnp.broadcast_to(jnp.exp(float(t) * lg), (RET_HEADS, 1, LANES))
    return cos2, sin2, mask, dq, dk, gt


def _roll_half(x):
    return pltpu.roll(x, RET_HEAD_DIM // 2, 1)


def _ret_heads_fwd(ret_ref, cos, sin, m_ref, dq_ref, dk_ref, s0):
    hd = RET_HEAD_DIM
    heads = range(RET_HEADS)
    qb, kb, vb, kdb = [], [], [], []
    for h in heads:
        q = ret_ref[:, h * hd:(h + 1) * hd]
        k = ret_ref[:, 512 + h * hd:512 + (h + 1) * hd]
        kr = (k * cos + _roll_half(k) * sin) * RET_KSCALE
        qb.append(_mx(q * cos + _roll_half(q) * sin))
        kb.append(_mx(kr))
        kdb.append(_mx(kr * dk_ref[h]))
        vb.append(_mx(ret_ref[:, 1024 + h * hd:1024 + (h + 1) * hd]))
    p = [_dot_nt(qb[h], kb[h]) for h in heads]
    cross = [_dot(qb[h], _mx(s0[h])) for h in heads]
    pb = [_mx(p[h] * m_ref[h]) for h in heads]
    o = [_dot(pb[h], vb[h]) + cross[h] * dq_ref[h] for h in heads]
    gn, rstd = [], []
    for h in heads:
        oc = o[h] - jnp.mean(o[h], axis=-1, keepdims=True)
        rstd.append(lax.rsqrt(jnp.mean(oc * oc, axis=-1, keepdims=True) + EPS))
        gn.append(oc * rstd[h])
    return qb, kb, vb, pb, kdb, gn, rstd


def _retention_fwd(ret, tables):
    cos2, sin2, mask, dq, dk, gt = tables
    s = ret.shape[0]
    t = mask.shape[1]
    nt = s // t
    hd = RET_HEAD_DIM

    def body(ret_ref, cos_ref, sin_ref, m_ref, dq_ref, dk_ref, gt_ref, y_ref, st_ref, s_scr):
        i = pl.program_id(0)

        @pl.when(i == 0)
        def _():
            s_scr[...] = jnp.zeros_like(s_scr)

        s0 = [s_scr[h] for h in range(RET_HEADS)]
        _, _, vb, _, kdb, gn, _ = _ret_heads_fwd(ret_ref, cos_ref[...], sin_ref[...], m_ref, dq_ref, dk_ref, s0)
        kv = [_dot_tn(kdb[h], vb[h]) for h in range(RET_HEADS)]
        for h in range(RET_HEADS):
            g = ret_ref[:, 1536 + h * hd:1536 + (h + 1) * hd]
            st_ref[h] = s0[h]
            y_ref[:, h * hd:(h + 1) * hd] = (gn[h] * (g * _sigmoid(g))).astype(y_ref.dtype)
            s_scr[h] = s0[h] * gt_ref[h] + kv[h]

    full3 = lambda a: pl.BlockSpec(a.shape, lambda i: (0, 0, 0))
    return pl.pallas_call(
        body, name="retention_fwd", grid=(nt,),
        in_specs=[pl.BlockSpec((t, 2048), lambda i: (i, 0)),
                  pl.BlockSpec((t, LANES), lambda i: (i, 0)),
                  pl.BlockSpec((t, LANES), lambda i: (i, 0)),
                  full3(mask), full3(dq), full3(dk), full3(gt)],
        out_specs=[pl.BlockSpec((t, 512), lambda i: (i, 0)),
                   pl.BlockSpec((None, RET_HEADS, hd, hd), lambda i: (i, 0, 0, 0))],
        out_shape=[jax.ShapeDtypeStruct((s, 512), MXU_DTYPE),
                   jax.ShapeDtypeStruct((nt, RET_HEADS, hd, hd), F32)],
        scratch_shapes=[pltpu.VMEM((RET_HEADS, hd, hd), F32)],
        compiler_params=_params(("arbitrary",)),
    )(ret, cos2, sin2, mask, dq, dk, gt)


def _retention_bwd(ret, states, dy, tables):
    cos2, sin2, mask, dq, dk, gt = tables
    s = ret.shape[0]
    t = mask.shape[1]
    nt = s // t
    hd = RET_HEAD_DIM

    def body(ret_ref, st_ref, dy_ref, cos_ref, sin_ref, m_ref, dq_ref, dk_ref, gt_ref, d_ref, ds_scr):
        i = pl.program_id(0)

        @pl.when(i == 0)
        def _():
            ds_scr[...] = jnp.zeros_like(ds_scr)

        cos = cos_ref[...]
        sin = sin_ref[...]
        heads = range(RET_HEADS)
        s0 = [st_ref[h] for h in heads]
        ds = [ds_scr[h] for h in heads]
        dsb = [_mx(ds[h]) for h in heads]
        qb, kb, vb, pb, kdb, gn, rstd = _ret_heads_fwd(ret_ref, cos, sin, m_ref, dq_ref, dk_ref, s0)
        dob, dodb = [], []
        for h in heads:
            g = ret_ref[:, 1536 + h * hd:1536 + (h + 1) * hd]
            dyv = dy_ref[:, h * hd:(h + 1) * hd]
            sg = _sigmoid(g)
            d_ref[:, 1536 + h * hd:1536 + (h + 1) * hd] = (
                dyv * gn[h] * (sg * (1.0 + g * (1.0 - sg)))).astype(d_ref.dtype)
            dgn = dyv * (g * sg)
            do = rstd[h] * (dgn - jnp.mean(dgn, axis=-1, keepdims=True)
                            - gn[h] * jnp.mean(dgn * gn[h], axis=-1, keepdims=True))
            dob.append(_mx(do))
            dodb.append(_mx(do * dq_ref[h]))
        dp = [_dot_nt(dob[h], vb[h]) for h in heads]
        dv = [_dot_tn(pb[h], dob[h]) + _dot(kdb[h], dsb[h]) for h in heads]
        dq_cross = [_dot_nt(dodb[h], _mx(s0[h])) for h in heads]
        dk_cross = [_dot_nt(vb[h], dsb[h]) for h in heads]
        ds_new = [_dot_tn(qb[h], dodb[h]) for h in heads]
        dpb = [_mx(dp[h] * m_ref[h]) for h in heads]
        dqr = [_dot(dpb[h], kb[h]) + dq_cross[h] for h in heads]
        dkr = [(_dot_tn(dpb[h], qb[h]) + dk_cross[h] * dk_ref[h]) * RET_KSCALE for h in heads]
        for h in heads:
            d_ref[:, 1024 + h * hd:1024 + (h + 1) * hd] = dv[h].astype(d_ref.dtype)
            d_ref[:, h * hd:(h + 1) * hd] = (dqr[h] * cos + _roll_half(dqr[h] * sin)).astype(d_ref.dtype)
            d_ref[:, 512 + h * hd:512 + (h + 1) * hd] = (
                dkr[h] * cos + _roll_half(dkr[h] * sin)).astype(d_ref.dtype)
            ds_scr[h] = ds[h] * gt_ref[h] + ds_new[h]

    full3 = lambda a: pl.BlockSpec(a.shape, lambda i: (0, 0, 0))
    rev = lambda i: (nt - 1 - i, 0)
    return pl.pallas_call(
        body, name="retention_bwd", grid=(nt,),
        in_specs=[pl.BlockSpec((t, 2048), rev),
                  pl.BlockSpec((None, RET_HEADS, hd, hd), lambda i: (nt - 1 - i, 0, 0, 0)),
                  pl.BlockSpec((t, 512), rev),
                  pl.BlockSpec((t, LANES), rev),
                  pl.BlockSpec((t, LANES), rev),
                  full3(mask), full3(dq), full3(dk), full3(gt)],
        out_specs=pl.BlockSpec((t, 2048), rev),
        out_shape=jax.ShapeDtypeStruct((s, 2048), MXU_DTYPE),
        scratch_shapes=[pltpu.VMEM((RET_HEADS, hd, hd), F32)],
        compiler_params=_params(("arbitrary",)),
    )(ret, states, dy, cos2, sin2, mask, dq, dk, gt)


SB_BLOCK = 256
SB_CHUNK = 32


SB_SKIP = 104.0


def _split_dots(xs, u):
    parts = []
    for x in xs:
        hi = lax.bitcast_convert_type(lax.bitcast_convert_type(x, jnp.uint32) & jnp.uint32(0xFFFF0000), F32)
        parts += [_mx(hi), _mx(x - hi)]
    out = _dot(jnp.concatenate(parts, axis=0), u)
    n = xs[0].shape[0]
    return [out[2 * k * n:(2 * k + 1) * n] + out[(2 * k + 1) * n:(2 * k + 2) * n] for k in range(len(xs))]


def _split_dot(x, u):
    return _split_dots([x], u)[0]


def _sb_pair_weights(lb, lk, allowed, u_gt):
    blk = lb.shape[0]
    lk_p, lk_d = lk[:, :blk], lk[:, blk:]
    r_d = _rowsum(lk_d)
    cs_p, cs_d = _split_dots([lk_p, lk_d], u_gt)
    a = jnp.exp(lb + jnp.concatenate([cs_p + r_d, cs_d], axis=1))
    return jnp.where(allowed, a, 0.0), r_d, r_d + _rowsum(lk_p)


def _sb_logits(q, k, causal):
    z = _dot_nt(q, k)
    l1p = jnp.log(1.0 + jnp.exp(-jnp.abs(z)))
    lk = -(jnp.maximum(z, 0.0) + l1p)
    if causal is not None:
        lk = jnp.where(causal, lk, 0.0)
    return jnp.minimum(z, 0.0) - l1p, lk


def _sb_weights(lb, lk, r, u_gt, causal):
    a = jnp.exp(lb + _split_dot(lk, u_gt) + r)
    return a if causal is None else jnp.where(causal, a, 0.0)


def _rowsum(x):
    return jnp.sum(x, axis=1, keepdims=True)


def _sb_pair_tile(i, blk):
    row = lax.broadcasted_iota(jnp.int32, (blk, 2 * blk), 0)
    col = lax.broadcasted_iota(jnp.int32, (blk, 2 * blk), 1)
    first_col = jnp.where(i >= 1, 0, blk)
    allowed = jnp.logical_and(row > col - blk, col >= first_col)
    rows_p = pl.ds(pl.multiple_of(jnp.maximum(i - 1, 0) * blk, blk), blk)
    rows_d = pl.ds(pl.multiple_of(i * blk, blk), blk)
    return allowed, rows_p, rows_d


def _sb_fwd(qkv, sg):
    s = qkv.shape[0]
    blk = min(SB_BLOCK, s)
    nq = s // blk
    hd = SB_HEAD_DIM

    ch = min(SB_CHUNK, blk)

    def body(q_ref, k_ref, v_ref, g_ref, y_ref, o_ref, z_scr, lhs_scr, cs_scr, a_scr):
        i = pl.program_id(1)
        row = lax.broadcasted_iota(jnp.int32, (blk, blk), 0)
        col = lax.broadcasted_iota(jnp.int32, (blk, blk), 1)
        u_gt = (row > col).astype(MXU_DTYPE)
        heads = [slice(hh * hd, (hh + 1) * hd) for hh in range(2)]
        qs = [q_ref[:, ls] for ls in heads]
        _, rows_p, rows_d = _sb_pair_tile(i, blk)
        has_prev = i >= 1
        crow = lax.broadcasted_iota(jnp.int32, (ch, blk), 0)
        ccol = lax.broadcasted_iota(jnp.int32, (ch, blk), 1)

        def logits(hh):
            kc = jnp.concatenate([k_ref[rows_p, heads[hh]], k_ref[rows_d, heads[hh]]], axis=0)
            z_scr[hh] = _dot_nt(qs[hh], kc)

        def keep_parts(hh):
            r_d, r_all = [], []
            for c in range(blk // ch):
                rows = pl.ds(c * ch, ch)
                causal = crow + c * ch > ccol
                z = z_scr[hh, rows, :]
                l1p = jnp.log(1.0 + jnp.exp(-jnp.abs(z)))
                lk = -(jnp.maximum(z, 0.0) + l1p)
                z_scr[hh, rows, :] = jnp.minimum(z, 0.0) - l1p
                lk_p = jnp.where(has_prev, lk[:, :blk], 0.0)
                lk_d = jnp.where(causal, lk[:, blk:], 0.0)
                for part, x in enumerate((lk_p, lk_d)):
                    hi = lax.bitcast_convert_type(
                        lax.bitcast_convert_type(x, jnp.uint32) & jnp.uint32(0xFFFF0000), F32)
                    lhs_scr[hh, pl.ds((2 * part) * blk + c * ch, ch), :] = _mx(hi)
                    lhs_scr[hh, pl.ds((2 * part + 1) * blk + c * ch, ch), :] = _mx(x - hi)
                r_d.append(_rowsum(lk_d))
                r_all.append(r_d[c] + _rowsum(lk_p))
            return r_d, jnp.concatenate(r_all, axis=0)

        def suffix_sums(hh):
            cs_scr[hh] = _dot(lhs_scr[hh], u_gt)

        def weights(hh, r_d):
            for c in range(blk // ch):
                rows = pl.ds(c * ch, ch)
                causal = crow + c * ch > ccol
                cs_p = cs_scr[hh, pl.ds(c * ch, ch), :] + cs_scr[hh, pl.ds(blk + c * ch, ch), :] + r_d[c]
                cs_d = cs_scr[hh, pl.ds(2 * blk + c * ch, ch), :] + cs_scr[hh, pl.ds(3 * blk + c * ch, ch), :]
                lb = z_scr[hh, rows, :]
                a_p = jnp.where(has_prev, jnp.exp(lb[:, :blk] + cs_p), 0.0)
                a_d = jnp.where(causal, jnp.exp(lb[:, blk:] + cs_d), 0.0)
                a_scr[hh, rows, :] = _mx(jnp.concatenate([a_p, a_d], axis=1))

        def values(hh):
            vc = jnp.concatenate([v_ref[rows_p, heads[hh]], v_ref[rows_d, heads[hh]]], axis=0)
            return _dot(a_scr[hh], vc)

        def block(hh, j, r):
            start = pl.multiple_of(j * blk, blk)
            lb, lk = _sb_logits(qs[hh], k_ref[pl.ds(start, blk), heads[hh]], None)
            a = _sb_weights(lb, lk, r, u_gt, None)
            return _dot(_mx(a), v_ref[pl.ds(start, blk), heads[hh]]), r + _rowsum(lk)

        def more(n, r0, r1):
            return jnp.logical_and(n < i, jnp.max(jnp.maximum(r0, r1)) > -SB_SKIP)

        logits(0)
        logits(1)
        rd0, r0 = keep_parts(0)
        suffix_sums(0)
        rd1, r1 = keep_parts(1)
        suffix_sums(1)
        go = more(jnp.int32(1), r0, r1)
        gates = [g_ref[:, ls] * _sigmoid(g_ref[:, ls]) for ls in heads]
        weights(0, rd0)
        acc0 = values(0)
        weights(1, rd1)
        acc1 = values(1)

        def step(c):
            _, n, acc0, r0, acc1, r1 = c
            pv0, r0 = block(0, i - 1 - n, r0)
            pv1, r1 = block(1, i - 1 - n, r1)
            return more(n + 1, r0, r1), n + 1, acc0 + pv0, r0, acc1 + pv1, r1

        _, _, acc0, _, acc1, _ = lax.while_loop(lambda c: c[0], step, (go, jnp.int32(1), acc0, r0, acc1, r1))
        for ls, acc, gate in zip(heads, (acc0, acc1), gates):
            o_ref[:, ls] = acc
            y_ref[:, ls] = (acc * gate).astype(y_ref.dtype)

    qblk = pl.BlockSpec((blk, LANES), lambda p, i: (i, p))
    return pl.pallas_call(
        body, name="stickbreak_fwd", grid=(SB_HEADS // 2, nq),
        in_specs=[qblk,
                  pl.BlockSpec((s, LANES), lambda p, i: (0, 4 + p)),
                  pl.BlockSpec((s, LANES), lambda p, i: (0, 8 + p)),
                  qblk],
        out_specs=[qblk, qblk],
        out_shape=[jax.ShapeDtypeStruct((s, 512), MXU_DTYPE),
                   jax.ShapeDtypeStruct((s, 512), F32)],
        scratch_shapes=[pltpu.VMEM((2, blk, 2 * blk), F32),
                        pltpu.VMEM((2, 4 * blk, blk), MXU_DTYPE),
                        pltpu.VMEM((2, 4 * blk, blk), F32),
                        pltpu.VMEM((2, blk, 2 * blk), MXU_DTYPE)],
        compiler_params=_params(("arbitrary", "arbitrary")),
    )(qkv, qkv, qkv, sg)


def _sb_bwd(qkv, sg, o, dy):
    s = qkv.shape[0]
    blk = min(SB_BLOCK, s)
    nq = s // blk
    hd = SB_HEAD_DIM
    assert nq <= LANES
    ch = min(SB_CHUNK, blk)

    def body(q_ref, k_ref, v_ref, g_ref, o_ref, dy_ref, dq_ref, dk_ref, dv_ref, dg_ref, dk_scr, dv_scr,
             z_scr, g_scr, lhs_scr, cs_scr, a_scr, dz_scr):
        i = pl.program_id(1)

        @pl.when(i == 0)
        def _():
            dk_scr[...] = jnp.zeros_like(dk_scr)
            dv_scr[...] = jnp.zeros_like(dv_scr)

        row = lax.broadcasted_iota(jnp.int32, (blk, blk), 0)
        col = lax.broadcasted_iota(jnp.int32, (blk, blk), 1)
        lane = lax.broadcasted_iota(jnp.int32, (blk, LANES), 1)
        u_gt = (row > col).astype(MXU_DTYPE)
        u_lt = (row < col).astype(MXU_DTYPE)
        heads = [slice(hh * hd, (hh + 1) * hd) for hh in range(2)]
        qs = [q_ref[:, ls] for ls in heads]
        _, rows_p, rows_d = _sb_pair_tile(i, blk)
        has_prev = i >= 1
        crow = lax.broadcasted_iota(jnp.int32, (ch, blk), 0)
        ccol = lax.broadcasted_iota(jnp.int32, (ch, blk), 1)
        nch = blk // ch
        dobs, kcs = [], []
        for hh, ls in enumerate(heads):
            g = g_ref[:, ls]
            dyv = dy_ref[:, ls]
            sgm = _sigmoid(g)
            dg_ref[:, ls] = (dyv * o_ref[:, ls] * (sgm * (1.0 + g * (1.0 - sgm)))).astype(dg_ref.dtype)
            dobs.append(_mx(dyv * (g * sgm)))
            kcs.append(jnp.concatenate([k_ref[rows_p, ls], k_ref[rows_d, ls]], axis=0))

        def split_rows(hh, c, part, x):
            hi = lax.bitcast_convert_type(lax.bitcast_convert_type(x, jnp.uint32) & jnp.uint32(0xFFFF0000), F32)
            lhs_scr[hh, pl.ds((2 * part) * blk + c * ch, ch), :] = _mx(hi)
            lhs_scr[hh, pl.ds((2 * part + 1) * blk + c * ch, ch), :] = _mx(x - hi)

        def summed_rows(hh, c, part):
            return (cs_scr[hh, pl.ds((2 * part) * blk + c * ch, ch), :]
                    + cs_scr[hh, pl.ds((2 * part + 1) * blk + c * ch, ch), :])

        def logits(hh):
            z_scr[hh] = _dot_nt(qs[hh], kcs[hh])
            vc = jnp.concatenate([v_ref[rows_p, heads[hh]], v_ref[rows_d, heads[hh]]], axis=0)
            g_scr[hh] = _dot_nt(dobs[hh], vc)

        def keep_parts(hh):
            r_d, r_all = [], []
            for c in range(nch):
                rows = pl.ds(c * ch, ch)
                z = z_scr[hh, rows, :]
                l1p = jnp.log(1.0 + jnp.exp(-jnp.abs(z)))
                lk = -(jnp.maximum(z, 0.0) + l1p)
                z_scr[hh, rows, :] = jnp.minimum(z, 0.0) - l1p
                lk_p = jnp.where(has_prev, lk[:, :blk], 0.0)
                lk_d = jnp.where(crow + c * ch > ccol, lk[:, blk:], 0.0)
                split_rows(hh, c, 0, lk_p)
                split_rows(hh, c, 1, lk_d)
                r_d.append(_rowsum(lk_d))
                r_all.append(r_d[c] + _rowsum(lk_p))
            return r_d, jnp.concatenate(r_all, axis=0)

        def weights(hh, r_d):
            g_p = []
            for c in range(nch):
                rows = pl.ds(c * ch, ch)
                lb = z_scr[hh, rows, :]
                a_p = jnp.where(has_prev, jnp.exp(lb[:, :blk] + (summed_rows(hh, c, 0) + r_d[c])), 0.0)
                a_d = jnp.where(crow + c * ch > ccol, jnp.exp(lb[:, blk:] + summed_rows(hh, c, 1)), 0.0)
                a = jnp.concatenate([a_p, a_d], axis=1)
                a_scr[hh, rows, :] = _mx(a)
                gm = g_scr[hh, rows, :] * a
                g_scr[hh, rows, :] = gm
                split_rows(hh, c, 0, gm[:, :blk])
                split_rows(hh, c, 1, gm[:, blk:])
                g_p.append(_rowsum(gm[:, :blk]))
            return g_p

        def logit_grads(hh, pg, g_p):
            for c in range(nch):
                rows = pl.ds(c * ch, ch)
                pre = jnp.concatenate([summed_rows(hh, c, 0) + pg[c * ch:(c + 1) * ch],
                                       summed_rows(hh, c, 1) + (pg[c * ch:(c + 1) * ch] + g_p[c])], axis=1)
                gm = g_scr[hh, rows, :]
                dz = gm - (gm + pre) * jnp.exp(z_scr[hh, rows, :])
                dz_p = jnp.where(has_prev, dz[:, :blk], 0.0)
                dz_d = jnp.where(crow + c * ch > ccol, dz[:, blk:], 0.0)
                dz_scr[hh, rows, :] = _mx(jnp.concatenate([dz_p, dz_d], axis=1))

        def products(hh, acc):
            ls = heads[hh]
            dk_scr[hh, rows_p, :] += _dot_tn(dz_scr[hh, :, 0:blk], qs[hh])
            dk_scr[hh, rows_d, :] += _dot_tn(dz_scr[hh, :, blk:2 * blk], qs[hh])
            dv_scr[hh, rows_p, :] += _dot_tn(a_scr[hh, :, 0:blk], dobs[hh])
            dv_scr[hh, rows_d, :] += _dot_tn(a_scr[hh, :, blk:2 * blk], dobs[hh])
            dq_ref[:, ls] = ((acc + _dot(dz_scr[hh], kcs[hh])) * SB_SCALE).astype(dq_ref.dtype)

        def suffix_sums(hh):
            cs_scr[hh] = _dot(lhs_scr[hh], u_gt)

        def prefix_sums(hh):
            cs_scr[hh] = _dot(lhs_scr[hh], u_lt)

        def more(n, r0, r1):
            return jnp.logical_and(n < i, jnp.max(jnp.maximum(r0, r1)) > -SB_SKIP)

        logits(0)
        logits(1)
        rd0, ra0 = keep_parts(0)
        suffix_sums(0)
        rd1, ra1 = keep_parts(1)
        suffix_sums(1)
        go = more(jnp.int32(1), ra0, ra1)
        gp0 = weights(0, rd0)
        prefix_sums(0)
        gp1 = weights(1, rd1)
        prefix_sums(1)

        def scan_block(hh, j, r, rmat):
            start = pl.multiple_of(j * blk, blk)
            _, lk = _sb_logits(qs[hh], k_ref[pl.ds(start, blk), heads[hh]], None)
            return r + _rowsum(lk), jnp.where(lane == j, r, rmat)

        def scan_step(c):
            _, n, r0, rmat0, r1, rmat1 = c
            r0, rmat0 = scan_block(0, i - 1 - n, r0, rmat0)
            r1, rmat1 = scan_block(1, i - 1 - n, r1, rmat1)
            return more(n + 1, r0, r1), n + 1, r0, rmat0, r1, rmat1

        zmat = jnp.zeros((blk, LANES), F32)
        _, n, _, rmat0, _, rmat1 = lax.while_loop(lambda c: c[0], scan_step,
                                                  (go, jnp.int32(1), ra0, zmat, ra1, zmat))
        rmats = (rmat0, rmat1)

        def block(hh, j, pg):
            ls = heads[hh]
            start = pl.multiple_of(j * blk, blk)
            k = k_ref[pl.ds(start, blk), ls]
            lb, lk = _sb_logits(qs[hh], k, None)
            r = _rowsum(jnp.where(lane == j, rmats[hh], 0.0))
            a = _sb_weights(lb, lk, r, u_gt, None)
            gm = _dot_nt(dobs[hh], v_ref[pl.ds(start, blk), ls]) * a
            dzb = _mx(gm - (gm + (pg + _split_dot(gm, u_lt))) * jnp.exp(lb))
            dk_scr[hh, pl.ds(start, blk), :] += _dot_tn(dzb, qs[hh])
            dv_scr[hh, pl.ds(start, blk), :] += _dot_tn(_mx(a), dobs[hh])
            return _dot(dzb, k), pg + _rowsum(gm)

        def step(t, c):
            acc0, pg0, acc1, pg1 = c
            dq0, pg0 = block(0, i - n + t, pg0)
            dq1, pg1 = block(1, i - n + t, pg1)
            return acc0 + dq0, pg0, acc1 + dq1, pg1

        zero = jnp.zeros((blk, 1), F32)
        zacc = jnp.zeros((blk, hd), F32)
        acc0, pg0, acc1, pg1 = lax.fori_loop(0, n - 1, step, (zacc, zero, zacc, zero))
        logit_grads(0, pg0, gp0)
        logit_grads(1, pg1, gp1)
        products(0, acc0)
        products(1, acc1)

        @pl.when(i == nq - 1)
        def _():
            for hh in range(2):
                ls = slice(hh * hd, (hh + 1) * hd)
                dk_ref[:, ls] = dk_scr[hh].astype(dk_ref.dtype)
                dv_ref[:, ls] = dv_scr[hh].astype(dv_ref.dtype)

    qblk = lambda c0: pl.BlockSpec((blk, LANES), lambda p, i: (i, c0 + p))
    full = lambda c0: pl.BlockSpec((s, LANES), lambda p, i: (0, c0 + p))
    half = jax.ShapeDtypeStruct((s, 512), MXU_DTYPE)
    return pl.pallas_call(
        body, name="stickbreak_bwd", grid=(SB_HEADS // 2, nq),
        in_specs=[qblk(0), full(4), full(8), qblk(0), qblk(0), qblk(4)],
        out_specs=[qblk(0), full(0), full(0), qblk(0)],
        out_shape=[half, half, half, half],
        scratch_shapes=[pltpu.VMEM((2, s, hd), F32), pltpu.VMEM((2, s, hd), F32),
                        pltpu.VMEM((2, blk, 2 * blk), F32),
                        pltpu.VMEM((2, blk, 2 * blk), F32),
                        pltpu.VMEM((2, 4 * blk, blk), MXU_DTYPE),
                        pltpu.VMEM((2, 4 * blk, blk), F32),
                        pltpu.VMEM((2, blk, 2 * blk), MXU_DTYPE),
                        pltpu.VMEM((2, blk, 2 * blk), MXU_DTYPE)],
        compiler_params=_params(("arbitrary", "arbitrary")),
    )(qkv, qkv, qkv, sg, o, dy)


def _layer_fwd(layer, x, mod, norm_g, w_in_g, w_out_g, tables, gather=()):
    shift, scale1p, gate = mod[0:1], 1.0 + mod[1:2], mod[2:3]
    ret, qkv, sg, *gathered = _ln_proj(x, shift, scale1p, norm_g, w_in_g, layer, gather)
    y_r, states = _retention_fwd(ret, tables)
    y_s, o_s = _sb_fwd(qkv, sg)
    x_next = _out_proj(x, gate, y_r, y_s, w_out_g, layer)
    saved = (x, shift, scale1p, gate, ret, qkv, sg, y_r, states, y_s, o_s)
    return x_next, saved, gathered


def _layer_bwd(layer, me, dx_out, saved, norm_g, w_in_g, w_out_g, tables):
    x, shift, scale1p, gate, ret, qkv, sg, y_r, states, y_s, o_s = saved
    dy, dw_out, dgate = _out_proj_bwd(dx_out, gate, y_r, y_s, w_out_g, layer)
    d_ret = _retention_bwd(ret, states, dy, tables)
    d_q, d_k, d_v, d_g = _sb_bwd(qkv, sg, o_s, dy)
    dproj = jnp.concatenate([d_ret, d_q, d_k, d_v, d_g], axis=1)
    dx, dshift, dscale, dnorm_g = _in_proj_bwd_x(x, dx_out, dproj, shift, scale1p, norm_g, w_in_g, layer)
    dwo_parts = _mx(dw_out.reshape(N_DEV, D_MODEL // N_DEV, D_MODEL))
    r_in, r_out = _in_proj_bwd_w(me, x, dproj, shift, scale1p, norm_g, dwo_parts)
    dmod = jnp.concatenate([dshift, dscale, dgate], axis=1)
    return dx, r_in, r_out, dmod, dnorm_g


MESH_IDS = pl.DeviceIdType.MESH
N_PEERS = N_DEV - 1
HBM_SPEC = pl.BlockSpec(memory_space=pl.ANY)


def _my_place():
    return lax.axis_index("x"), lax.axis_index("y"), lax.axis_index("c")


def _linear(px, py, pc):
    return 4 * px + 2 * py + pc


def _all_gather(blocks):
    n_arr = len(blocks)

    def body(*refs):
        start, forward, finish = _gather_plan(refs[:n_arr], refs[n_arr:2 * n_arr], *refs[2 * n_arr:])
        start()
        forward()
        finish()

    return pl.pallas_call(
        body, name="all_gather",
        out_shape=_gathered_shapes(blocks),
        in_specs=[HBM_SPEC] * n_arr, out_specs=[HBM_SPEC] * n_arr,
        scratch_shapes=_gather_sems(n_arr),
    )(*blocks)


def _gathered_shapes(blocks):
    return [jax.ShapeDtypeStruct((N_DEV * b.shape[0], b.shape[1]), b.dtype) for b in blocks]


def _gather_sems(n_arr):
    return [pltpu.SemaphoreType.DMA((n_arr * N_PEERS,)), pltpu.SemaphoreType.DMA((n_arr * N_PEERS,)),
            pltpu.SemaphoreType.DMA((n_arr,))]


def _gather_plan(x_refs, out_refs, send_sems, recv_sems, local_sems):
    n_arr = len(x_refs)
    x, y, c = _my_place()
    me, sibling = (x, y, c), (x, y, 1 - c)
    chips = [(1 - x, y), (x, 1 - y), (1 - x, 1 - y)]

    def rows(a, place):
        m = x_refs[a].shape[0]
        return out_refs[a].at[pl.ds(_linear(*place) * m, m), :]

    def copy(a, k, block, to, src=None):
        return pltpu.make_async_remote_copy(
            src_ref=rows(a, block) if src is None else src, dst_ref=rows(a, block),
            send_sem=send_sems.at[a * N_PEERS + k], recv_sem=recv_sems.at[a * N_PEERS + k],
            device_id=to, device_id_type=MESH_IDS)

    mine = [pltpu.make_async_copy(x_refs[a], rows(a, me), local_sems.at[a]) for a in range(n_arr)]
    first = []
    for a in range(n_arr):
        first.append(copy(a, 0, me, sibling, src=x_refs[a]))
        first += [copy(a, 1 + j, me, (*chip, c), src=x_refs[a]) for j, chip in enumerate(chips)]
    passed = [copy(a, 4 + j, (*chip, c), sibling) for j, chip in enumerate(chips) for a in range(n_arr)]

    def start():
        for cp in mine + first:
            cp.start()

    def forward():
        for j, chip in enumerate(chips):
            for a in range(n_arr):
                copy(a, 1 + j, (*chip, c), me).wait_recv()
                passed[j * n_arr + a].start()

    def finish():
        for a in range(n_arr):
            copy(a, 0, sibling, me).wait_recv()
            for j, chip in enumerate(chips):
                copy(a, 4 + j, (*chip, 1 - c), me).wait_recv()
        for cp in first + passed:
            cp.wait_send()
        for cp in mine:
            cp.wait()

    return start, forward, finish


def _ada_fwd(c_all, w_ada, b_cols):
    cols = w_ada.shape[2]

    def body(c_ref, w_ref, b_ref, ca_ref, mod_ref):
        cv = c_ref[...]
        ca = cv * _sigmoid(cv)
        ca_ref[...] = ca
        cb = _mx(jnp.concatenate([ca, ca], axis=0))
        for l in range(DEPTH):
            mod_ref[l * N_DEV:(l + 1) * N_DEV, :] = _dot(cb, _mx(w_ref[l]))[0:N_DEV] + b_ref[l]

    return pl.pallas_call(
        body, name="ada_fwd",
        out_shape=[jax.ShapeDtypeStruct((N_DEV, D_MODEL), F32),
                   jax.ShapeDtypeStruct((DEPTH * N_DEV, cols), F32)],
        compiler_params=_params(),
    )(c_all, w_ada, b_cols)


def _ada_bwd(c_act_t, dmod_cols):
    cols = dmod_cols.shape[2]

    def body(ca_ref, dm_ref, o_ref):
        ca = _mx(ca_ref[...]).astype(F32)
        for l in range(DEPTH):
            o_ref[l] = jnp.dot(ca, _mx(dm_ref[l]).astype(F32),
                               precision=lax.Precision.HIGHEST, preferred_element_type=F32)

    return pl.pallas_call(
        body, name="ada_bwd",
        out_shape=jax.ShapeDtypeStruct((DEPTH, D_MODEL, cols), F32),
        compiler_params=_params(),
    )(c_act_t, dmod_cols)


def _adamw_store(g, w_ref, m_ref, v_ref, g_ref, d_ref, mo_ref, vo_ref):
    m2 = ADAM_B1 * m_ref[...] + (1.0 - ADAM_B1) * g
    v2 = ADAM_B2 * v_ref[...] + (1.0 - ADAM_B2) * (g * g)
    m_hat = m2 / (1.0 - ADAM_B1 ** ADAM_STEP)
    v_hat = v2 / (1.0 - ADAM_B2 ** ADAM_STEP)
    g_ref[...] = g
    d_ref[...] = -ADAM_LR * (m_hat / (jnp.sqrt(v_hat) + ADAM_EPS) + ADAM_WD * w_ref[...])
    mo_ref[...] = m2
    vo_ref[...] = v2


def _slab_sum(p_ref):
    g = p_ref[0].astype(F32)
    for sl in range(1, p_ref.shape[0]):
        g = g + p_ref[sl].astype(F32)
    return g


def _sum_adamw_layers(parts, w, m, v):
    n_slab, rows, cols = parts[0].shape
    tr = min(256, rows)
    nt = rows // tr

    def body(p0_ref, p1_ref, w_ref, m_ref, v_ref, g_ref, d_ref, mo_ref, vo_ref):
        for l, p_ref in enumerate((p0_ref, p1_ref)):
            @pl.when(pl.program_id(0) == l)
            def _():
                _adamw_store(_slab_sum(p_ref), w_ref, m_ref, v_ref, g_ref, d_ref, mo_ref, vo_ref)

    p_specs = [pl.BlockSpec((n_slab, tr, cols), lambda l, i: (0, i * (1 - l) + (nt - 1) * l, 0)),
               pl.BlockSpec((n_slab, tr, cols), lambda l, i: (0, i * l, 0))]
    blk = pl.BlockSpec((None, tr, cols), lambda l, i: (l, i, 0))
    shp = jax.ShapeDtypeStruct((DEPTH, rows, cols), F32)
    return pl.pallas_call(
        body, name="sum_adamw_layers", grid=(DEPTH, nt),
        in_specs=p_specs + [blk, blk, blk],
        out_specs=[blk, blk, blk, blk],
        out_shape=[shp, shp, shp, shp],
        compiler_params=_params(("arbitrary", "arbitrary")),
    )(parts[0], parts[1], w, m, v)


def _sum_adamw(parts, w, m, v):
    n_slab, rows, cols = parts.shape
    tr = min(256, rows)

    def body(p_ref, w_ref, m_ref, v_ref, g_ref, d_ref, mo_ref, vo_ref):
        _adamw_store(_slab_sum(p_ref), w_ref, m_ref, v_ref, g_ref, d_ref, mo_ref, vo_ref)

    blk = pl.BlockSpec((tr, cols), lambda i: (i, 0))
    shp = jax.ShapeDtypeStruct((rows, cols), F32)
    return pl.pallas_call(
        body, name="sum_adamw", grid=(rows // tr,),
        in_specs=[pl.BlockSpec((n_slab, tr, cols), lambda i: (0, i, 0)), blk, blk, blk],
        out_specs=[blk, blk, blk, blk],
        out_shape=[shp, shp, shp, shp],
        compiler_params=_params(("arbitrary",)),
    )(parts, w, m, v)


SMALL_ROWS = 16


def kernel(x, c, norm_g, w_ada, b_ada, w_in, w_out, final_g, loss_target, m_norm_g, m_w_ada, m_b_ada, m_w_in, m_w_out, m_final_g, v_norm_g, v_w_ada, v_b_ada, v_w_in, v_w_out, v_final_g):
    me = _linear(*_my_place())
    in_cols = w_in.shape[2]
    out_rows = w_out.shape[1]
    ada_cols = w_ada.shape[2]

    w_in_m, w_out_m = _mx(w_in), _mx(w_out)
    g_in, g_out, g_c = _all_gather([w_in_m[0], w_out_m[0], jnp.broadcast_to(c, (8, D_MODEL))])
    w_in_g = [g_in.reshape(N_DEV, 1, D_MODEL, in_cols), None]
    w_out_g = [g_out.reshape(N_DEV, 1, out_rows, D_MODEL), None]
    c_all = g_c.reshape(N_DEV, 8, D_MODEL)[:, 0]

    b_cols = lax.dynamic_slice_in_dim(b_ada, me * ada_cols, ada_cols, axis=1)[:, None, :]
    c_act, mod_cols = _ada_fwd(c_all, w_ada, b_cols)
    (g_mod,) = _all_gather([mod_cols])
    g_mod = g_mod.reshape(N_DEV, DEPTH, N_DEV, ada_cols)
    mod = lax.dynamic_index_in_dim(g_mod, me, axis=2, keepdims=False)
    mod = mod.transpose(1, 0, 2).reshape(DEPTH, 3, D_MODEL)

    tables = _ret_tables(x.shape[1])
    h = x[0]
    saved = []
    for l in range(DEPTH):
        nxt = (w_in_m[l + 1], w_out_m[l + 1]) if l + 1 < DEPTH else ()
        h, sv, gathered = _layer_fwd(0, h, mod[l], norm_g[l:l + 1], w_in_g[l], w_out_g[l], tables, nxt)
        if nxt:
            w_in_g[l + 1] = gathered[0].reshape(N_DEV, 1, D_MODEL, in_cols)
            w_out_g[l + 1] = gathered[1].reshape(N_DEV, 1, out_rows, D_MODEL)
        saved.append(sv)
    dx, loss_part, dfg = _final_loss(h, final_g[None], loss_target[0])
    r_in, r_out, small = [None] * DEPTH, [None] * DEPTH, [None] * DEPTH
    for l in reversed(range(DEPTH)):
        dx, r_in[l], r_out[l], dmod, dng = _layer_bwd(0, me, dx, saved[l], norm_g[l:l + 1], w_in_g[l], w_out_g[l], tables)
        small[l] = (dmod.reshape(3, D_MODEL), dng)

    pad = jnp.zeros((SMALL_ROWS - 10, D_MODEL), F32)
    small_block = jnp.concatenate([small[0][0], small[1][0], small[0][1], small[1][1], dfg,
                                   jnp.broadcast_to(loss_part, (1, D_MODEL)), pad], axis=0)
    (g_small,) = _all_gather([small_block])
    g_small = g_small.reshape(N_DEV, SMALL_ROWS, D_MODEL)

    def small_pack(b, n, f, fill):
        return jnp.concatenate([b.reshape(6, D_MODEL), n, f[None],
                                jnp.full((SMALL_ROWS - 9, D_MODEL), fill, F32)], axis=0)

    s_g, s_d, s_m, s_v = _sum_adamw(g_small, small_pack(b_ada, norm_g, final_g, 0.0),
                                    small_pack(m_b_ada, m_norm_g, m_final_g, 0.0),
                                    small_pack(v_b_ada, v_norm_g, v_final_g, 1.0))
    loss = s_g[9, 0]

    def small_unpack(a):
        return a[0:6].reshape(DEPTH, 3 * D_MODEL), a[6:8], a[8]

    dmod_all = g_small[:, 0:6].reshape(N_DEV, DEPTH, 3 * D_MODEL).transpose(1, 0, 2)
    dmod_cols = lax.dynamic_slice_in_dim(dmod_all, me * ada_cols, ada_cols, axis=2)
    g_ada = _ada_bwd(c_act.T, dmod_cols).reshape(1, DEPTH * D_MODEL, ada_cols)
    ada = _sum_adamw(g_ada, *[a.reshape(DEPTH * D_MODEL, ada_cols) for a in (w_ada, m_w_ada, v_w_ada)])
    ada = [a.reshape(DEPTH, D_MODEL, ada_cols) for a in ada]

    win = _sum_adamw_layers(r_in, w_in, m_w_in, v_w_in)
    wout = _sum_adamw_layers(r_out, w_out, m_w_out, v_w_out)

    outs = [loss, dx[None]]
    for k in range(4):
        b, n, f = small_unpack((s_g, s_d, s_m, s_v)[k])
        outs += [n, ada[k], b, win[k], wout[k], f]
    return tuple(outs)
```

```python
import functools

import jax
import jax.numpy as jnp
from jax import lax
from jax.experimental import pallas as pl
from jax.experimental.pallas import tpu as pltpu

F32 = jnp.float32
MXU_DTYPE = jnp.bfloat16

D_MODEL = 1024
DEPTH = 2
N_DEV = 8
CHUNK = 64
D_RET = 512
D_SB = 512
RET_HEADS = 4
RET_HEAD_DIM = 128
SB_HEADS = 8
SB_HEAD_DIM = 64
D_IN = 4096
ROPE_BASE = 10000.0
EPS = 1e-6
SB_SCALE = SB_HEAD_DIM ** -0.5
RET_KSCALE = RET_HEAD_DIM ** -0.5

ADAM_LR = 0.001
ADAM_B1 = 0.9
ADAM_B2 = 0.999
ADAM_EPS = 1e-08
ADAM_WD = 0.01
ADAM_STEP = 10

V7X_VMEM_BYTES = 64 * 2 ** 20
VMEM_LIMIT = V7X_VMEM_BYTES - 8 * 2 ** 20
LANES = 128

_NT = (((1,), (1,)), ((), ()))
_TN = (((0,), (0,)), ((), ()))


def _dot(a, b):
    return jnp.dot(a, b, preferred_element_type=F32)


def _dot_nt(a, b):
    return lax.dot_general(a, b, _NT, preferred_element_type=F32)


def _dot_tn(a, b):
    return lax.dot_general(a, b, _TN, preferred_element_type=F32)


def _mx(x):
    return x.astype(MXU_DTYPE)


def _sigmoid(x):
    return 1.0 / (1.0 + jnp.exp(-x))


def _params(sem=None):
    return pltpu.CompilerParams(dimension_semantics=sem, vmem_limit_bytes=VMEM_LIMIT)


def _row_tile(s):
    return min(512, s)


def _w_in_spec(w_in_g, layer):
    return pl.BlockSpec((N_DEV, None) + w_in_g.shape[2:], lambda i: (0, layer, 0, 0))


def _w_out_spec(w_out_g, layer):
    return pl.BlockSpec((N_DEV, None) + w_out_g.shape[2:], lambda i: (0, layer, 0, 0))


def _ln_proj(x, shift, scale1p, g, w_in_g, layer, gather=()):
    s = x.shape[0]
    ts = _row_tile(s)
    ns = s // ts
    n_g = len(gather)

    def body(x_ref, sh_ref, sc_ref, g_ref, w_ref, *rest):
        ret_ref, qkv_ref, sg_ref = rest[n_g:n_g + 3]
        if n_g:
            start, forward, finish = _gather_plan(rest[:n_g], rest[n_g + 3:2 * n_g + 3], *rest[2 * n_g + 3:])
            i = pl.program_id(0)
            pl.when(i == 0)(start)
            pl.when(i == max(ns - 4, 0))(forward)
        xv = x_ref[...]
        rstd = lax.rsqrt(jnp.mean(xv * xv, axis=-1, keepdims=True) + EPS)
        h = (xv * rstd * g_ref[...]) * sc_ref[...] + sh_ref[...]
        hb = _mx(h)
        for n in range(4):
            ret_ref[:, n * 512:(n + 1) * 512] = _dot(hb, w_ref[n])
        qkv_ref[:, 0:512] = _mx(_dot(hb, w_ref[4]) * SB_SCALE)
        qkv_ref[:, 512:1024] = _mx(_dot(hb, w_ref[5]))
        qkv_ref[:, 1024:1536] = _mx(_dot(hb, w_ref[6]))
        sg_ref[...] = _dot(hb, w_ref[7])
        if n_g:
            pl.when(i == ns - 1)(finish)

    vec = pl.BlockSpec((1, D_MODEL), lambda i: (0, 0))
    return pl.pallas_call(
        body, name="ln_proj_gather" if n_g else "ln_proj", grid=(ns,),
        in_specs=[pl.BlockSpec((ts, D_MODEL), lambda i: (i, 0)), vec, vec, vec,
                  _w_in_spec(w_in_g, layer)] + [HBM_SPEC] * n_g,
        out_specs=[pl.BlockSpec((ts, 2048), lambda i: (i, 0)),
                   pl.BlockSpec((ts, 1536), lambda i: (i, 0)),
                   pl.BlockSpec((ts, 512), lambda i: (i, 0))] + [HBM_SPEC] * n_g,
        out_shape=[jax.ShapeDtypeStruct((s, 2048), F32),
                   jax.ShapeDtypeStruct((s, 1536), MXU_DTYPE),
                   jax.ShapeDtypeStruct((s, 512), F32)] + _gathered_shapes(gather),
        scratch_shapes=_gather_sems(n_g) if n_g else (),
        compiler_params=_params(("arbitrary",)),
    )(x, shift, scale1p, g, w_in_g, *gather)


def _w_out_halves(w_ref):
    half = N_DEV // 2
    return (w_ref[0:half].reshape(D_RET, D_MODEL), w_ref[half:N_DEV].reshape(D_SB, D_MODEL))


def _out_proj(x, gate, y_r, y_s, w_out_g, layer):
    s = x.shape[0]
    ts = _row_tile(s)

    def body(x_ref, gate_ref, yr_ref, ys_ref, w_ref, o_ref):
        w_r, w_s = _w_out_halves(w_ref)
        t = _dot(yr_ref[...], w_r) + _dot(ys_ref[...], w_s)
        o_ref[...] = x_ref[...] + gate_ref[...] * t

    return pl.pallas_call(
        body, name="out_proj", grid=(s // ts,),
        in_specs=[pl.BlockSpec((ts, D_MODEL), lambda i: (i, 0)),
                  pl.BlockSpec((1, D_MODEL), lambda i: (0, 0)),
                  pl.BlockSpec((ts, 512), lambda i: (i, 0)),
                  pl.BlockSpec((ts, 512), lambda i: (i, 0)),
                  _w_out_spec(w_out_g, layer)],
        out_specs=pl.BlockSpec((ts, D_MODEL), lambda i: (i, 0)),
        out_shape=jax.ShapeDtypeStruct((s, D_MODEL), F32),
        compiler_params=_params(("arbitrary",)),
    )(x, gate, y_r, y_s, w_out_g)


def _final_loss(x, fg, target):
    s = x.shape[0]
    ts = _row_tile(s)

    def body(x_ref, fg_ref, t_ref, dx_ref, loss_ref, dfg_ref):
        i = pl.program_id(0)

        @pl.when(i == 0)
        def _():
            loss_ref[...] = jnp.zeros_like(loss_ref)
            dfg_ref[...] = jnp.zeros_like(dfg_ref)

        xv = x_ref[...]
        fgv = fg_ref[...]
        rstd = lax.rsqrt(jnp.mean(xv * xv, axis=-1, keepdims=True) + EPS)
        xn = xv * rstd
        err = xn * fgv - t_ref[...]
        tok = jnp.mean(err * err, axis=-1, keepdims=True)
        loss_ref[...] += 0.5 * jnp.sum(tok, axis=0, keepdims=True)
        dy = err * (1.0 / D_MODEL)
        dfg_ref[...] += jnp.sum(dy * xn, axis=0, keepdims=True)
        dxn = dy * fgv
        dx_ref[...] = rstd * (dxn - xn * jnp.mean(dxn * xn, axis=-1, keepdims=True))

    return pl.pallas_call(
        body, name="final_loss", grid=(s // ts,),
        in_specs=[pl.BlockSpec((ts, D_MODEL), lambda i: (i, 0)),
                  pl.BlockSpec((1, D_MODEL), lambda i: (0, 0)),
                  pl.BlockSpec((ts, D_MODEL), lambda i: (i, 0))],
        out_specs=[pl.BlockSpec((ts, D_MODEL), lambda i: (i, 0)),
                   pl.BlockSpec((1, 1), lambda i: (0, 0)),
                   pl.BlockSpec((1, D_MODEL), lambda i: (0, 0))],
        out_shape=[jax.ShapeDtypeStruct((s, D_MODEL), F32),
                   jax.ShapeDtypeStruct((1, 1), F32),
                   jax.ShapeDtypeStruct((1, D_MODEL), F32)],
        compiler_params=_params(("arbitrary",)),
    )(x, fg, target)


def _out_proj_bwd(dx_out, gate, y_r, y_s, w_out_g, layer):
    s = dx_out.shape[0]
    ts = _row_tile(s)

    def body(dx_ref, gate_ref, yr_ref, ys_ref, w_ref, dy_ref, dw_ref, dgate_ref):
        i = pl.program_id(0)

        @pl.when(i == 0)
        def _():
            dw_ref[...] = jnp.zeros_like(dw_ref)
            dgate_ref[...] = jnp.zeros_like(dgate_ref)

        dxv = dx_ref[...]
        dt = _mx(dxv * gate_ref[...])
        yr = yr_ref[...]
        ys = ys_ref[...]
        w_r, w_s = _w_out_halves(w_ref)
        dy_ref[:, 0:512] = _dot_nt(dt, w_r)
        dy_ref[:, 512:1024] = _dot_nt(dt, w_s)
        dw_ref[0:512, :] += _dot_tn(yr, dt)
        dw_ref[512:1024, :] += _dot_tn(ys, dt)
        t = _dot(yr, w_r) + _dot(ys, w_s)
        dgate_ref[...] += jnp.sum(dxv * t, axis=0, keepdims=True)

    return pl.pallas_call(
        body, name="out_proj_bwd", grid=(s // ts,),
        in_specs=[pl.BlockSpec((ts, D_MODEL), lambda i: (i, 0)),
                  pl.BlockSpec((1, D_MODEL), lambda i: (0, 0)),
                  pl.BlockSpec((ts, 512), lambda i: (i, 0)),
                  pl.BlockSpec((ts, 512), lambda i: (i, 0)),
                  _w_out_spec(w_out_g, layer)],
        out_specs=[pl.BlockSpec((ts, D_MODEL), lambda i: (i, 0)),
                   pl.BlockSpec((D_MODEL, D_MODEL), lambda i: (0, 0)),
                   pl.BlockSpec((1, D_MODEL), lambda i: (0, 0))],
        out_shape=[jax.ShapeDtypeStruct((s, D_MODEL), F32),
                   jax.ShapeDtypeStruct((D_MODEL, D_MODEL), F32),
                   jax.ShapeDtypeStruct((1, D_MODEL), F32)],
        compiler_params=_params(("arbitrary",)),
    )(dx_out, gate, y_r, y_s, w_out_g)


def _in_proj_bwd_x(x, dx_out, dproj, shift, scale1p, g, w_in_g, layer):
    s = x.shape[0]
    ts = min(256, s)
    nb = D_IN // N_DEV

    def body(x_ref, dxo_ref, dr_ref, d4_ref, d5_ref, d6_ref, d7_ref, sc_ref, g_ref, w_ref,
             dx_ref, dsh_ref, dsc_ref, dg_ref):
        i = pl.program_id(0)

        @pl.when(i == 0)
        def _():
            dsh_ref[...] = jnp.zeros_like(dsh_ref)
            dsc_ref[...] = jnp.zeros_like(dsc_ref)
            dg_ref[...] = jnp.zeros_like(dg_ref)

        dh = _dot_nt(dr_ref[:, 0:nb], w_ref[0])
        for n in range(1, 4):
            dh += _dot_nt(dr_ref[:, n * nb:(n + 1) * nb], w_ref[n])
        for n, d_ref in zip(range(4, N_DEV), (d4_ref, d5_ref, d6_ref, d7_ref)):
            dh += _dot_nt(d_ref[...], w_ref[n])
        xv = x_ref[...]
        gv = g_ref[...]
        scv = sc_ref[...]
        rstd = lax.rsqrt(jnp.mean(xv * xv, axis=-1, keepdims=True) + EPS)
        xn = xv * rstd
        dsh_ref[...] += jnp.sum(dh, axis=0, keepdims=True)
        dsc_ref[...] += jnp.sum(dh * (xn * gv), axis=0, keepdims=True)
        dhs = dh * scv
        dg_ref[...] += jnp.sum(dhs * xn, axis=0, keepdims=True)
        dxn = dhs * gv
        dx_ref[...] = rstd * (dxn - xn * jnp.mean(dxn * xn, axis=-1, keepdims=True)) + dxo_ref[...]

    del shift
    vec = pl.BlockSpec((1, D_MODEL), lambda i: (0, 0))
    return pl.pallas_call(
        body, name="in_proj_bwd_x", grid=(s // ts,),
        in_specs=[pl.BlockSpec((ts, D_MODEL), lambda i: (i, 0)),
                  pl.BlockSpec((ts, D_MODEL), lambda i: (i, 0)),
                  pl.BlockSpec((ts, 4 * nb), lambda i: (i, 0))]
                 + [pl.BlockSpec((ts, nb), lambda i: (i, 0))] * 4
                 + [vec, vec, _w_in_spec(w_in_g, layer)],
        out_specs=[pl.BlockSpec((ts, D_MODEL), lambda i: (i, 0)), vec, vec, vec],
        out_shape=[jax.ShapeDtypeStruct((s, D_MODEL), F32),
                   jax.ShapeDtypeStruct((1, D_MODEL), F32),
                   jax.ShapeDtypeStruct((1, D_MODEL), F32),
                   jax.ShapeDtypeStruct((1, D_MODEL), F32)],
        compiler_params=_params(("arbitrary",)),
    )(x, dx_out, *dproj, scale1p, g, w_in_g)


def _in_proj_bwd_w(me, x, dproj, shift, scale1p, g, dwo_parts):
    s = x.shape[0]
    ts = min(2 * _row_tile(s), s)
    ns = s // ts
    nb = D_IN // N_DEV
    last = N_DEV - 1

    def body(me_ref, x_ref, dr_ref, d4_ref, d5_ref, d6_ref, d7_ref, sh_ref, sc_ref, g_ref, dwo_ref,
             rin_ref, rout_ref, acc, stage, h_t, in_send, in_recv, out_send, out_recv, local_sems):
        t = pl.program_id(0)
        slab = (me_ref[0] + 1 + t) % N_DEV
        i = pl.program_id(1)
        px, py, pc = _my_place()
        mine = _linear(px, py, pc)

        def out_copy(r):
            peer = (1 - px if r & 4 else px, 1 - py if r & 2 else py, 1 - pc if r & 1 else pc)
            return pltpu.make_async_remote_copy(
                src_ref=dwo_ref.at[_linear(*peer)], dst_ref=rout_ref.at[mine],
                send_sem=out_send.at[r - 1], recv_sem=out_recv.at[r - 1],
                device_id=peer, device_id_type=MESH_IDS)

        def in_copy(step):
            dest = (mine + 1 + step) % N_DEV
            return pltpu.make_async_remote_copy(
                src_ref=stage.at[step % 2], dst_ref=rin_ref.at[mine],
                send_sem=in_send.at[step], recv_sem=in_recv.at[step],
                device_id=(dest // 4, (dest // 2) % 2, dest % 2), device_id_type=MESH_IDS)

        own_out = pltpu.make_async_copy(dwo_ref.at[mine], rout_ref.at[mine], local_sems.at[0])
        own_in = pltpu.make_async_copy(stage.at[last % 2], rin_ref.at[mine], local_sems.at[1])

        @pl.when(jnp.logical_and(t == 0, i == 0))
        def _():
            own_out.start()
            for r in range(1, N_DEV):
                out_copy(r).start()

        @pl.when(i == 0)
        def _():
            acc[...] = jnp.zeros_like(acc)

        @pl.when(t == 0)
        def _():
            xv = x_ref[...]
            rstd = lax.rsqrt(jnp.mean(xv * xv, axis=-1, keepdims=True) + EPS)
            h = (xv * rstd * g_ref[...]) * sc_ref[...] + sh_ref[...]
            h_t[i] = _mx(h.T)

        @pl.when(slab < 4)
        def _():
            acc[...] += _dot(h_t[i], dr_ref[...])

        for n, d_ref in zip(range(4, N_DEV), (d4_ref, d5_ref, d6_ref, d7_ref)):
            @pl.when(slab == n)
            def _():
                acc[...] += _dot(h_t[i], d_ref[...])

        @pl.when(i == ns - 1)
        def _():
            @pl.when(t >= 2)
            def _():
                in_copy(t - 2).wait_send()

            stage[t % 2] = acc[...].astype(stage.dtype)

            @pl.when(t < last)
            def _():
                in_copy(t).start()

            @pl.when(t == last)
            def _():
                own_in.start()
                in_copy(last - 1).wait_send()
                for step in range(last):
                    in_copy(step).wait_recv()
                for r in range(1, N_DEV):
                    out_copy(r).wait_recv()
                    out_copy(r).wait_send()
                own_out.wait()
                own_in.wait()

    def slab_of(t, me_ref):
        return (me_ref[0] + 1 + t) % N_DEV

    def part_rows(n, t, i, me_ref):
        return jnp.where(slab_of(t, me_ref) == n, i, ns - 1), 0

    vec = pl.BlockSpec((1, D_MODEL), lambda t, i, me_ref: (0, 0))
    rows = dwo_parts.shape[1]
    return pl.pallas_call(
        body, name="in_proj_bwd_w",
        grid_spec=pltpu.PrefetchScalarGridSpec(
            num_scalar_prefetch=1, grid=(N_DEV, ns),
            in_specs=[pl.BlockSpec((ts, D_MODEL), lambda t, i, me_ref: (jnp.where(t == 0, i, ns - 1), 0)),
                      pl.BlockSpec((ts, nb), lambda t, i, me_ref: (
                          jnp.where(slab_of(t, me_ref) < 4, i, ns - 1), jnp.minimum(slab_of(t, me_ref), 3)))]
                     + [pl.BlockSpec((ts, nb), functools.partial(part_rows, n)) for n in range(4, N_DEV)]
                     + [vec, vec, vec, HBM_SPEC],
            out_specs=[HBM_SPEC, HBM_SPEC],
            scratch_shapes=[pltpu.VMEM((D_MODEL, nb), F32),
                            pltpu.VMEM((2, D_MODEL, nb), MXU_DTYPE),
                            pltpu.VMEM((ns, D_MODEL, ts), MXU_DTYPE),
                            pltpu.SemaphoreType.DMA((N_PEERS,)), pltpu.SemaphoreType.DMA((N_PEERS,)),
                            pltpu.SemaphoreType.DMA((N_PEERS,)), pltpu.SemaphoreType.DMA((N_PEERS,)),
                            pltpu.SemaphoreType.DMA((2,))]),
        out_shape=[jax.ShapeDtypeStruct((N_DEV, D_MODEL, nb), MXU_DTYPE),
                   jax.ShapeDtypeStruct((N_DEV, rows, D_MODEL), MXU_DTYPE)],
        compiler_params=_params(("arbitrary", "arbitrary")),
    )(jnp.reshape(me, (1,)).astype(jnp.int32), x, *dproj, shift, scale1p, g, dwo_parts)


RET_TILE = 256


def _ret_tables(s):
    t = min(RET_TILE, s)
    half = RET_HEAD_DIM // 2
    pos = jnp.arange(s, dtype=F32)
    inv = ROPE_BASE ** (-jnp.arange(half, dtype=F32) / half)
    ang = pos[:, None] * inv[None, :]
    cos, sin = jnp.cos(ang), jnp.sin(ang)
    cos2 = jnp.concatenate([cos, cos], axis=1)
    sin2 = jnp.concatenate([-sin, sin], axis=1)
    lg = jnp.log1p(-(2.0 ** (-5.0 - jnp.arange(RET_HEADS, dtype=F32))))[:, None, None]
    n = jnp.arange(t)
    dist = (n[:, None] - n[None, :]).astype(F32)[None]
    cn = (n // CHUNK)[:, None]
    cm = (n // CHUNK)[None, :]
    mask = jnp.where((cn == cm)[None], jnp.exp(jnp.abs(dist) * lg),
                     jnp.where((cm < cn)[None], jnp.exp(dist * lg), 0.0))
    nf = n.astype(F32)[None, :, None]
    dq = jnp.broadcast_to(jnp.exp((nf + 1.0) * lg), (RET_HEADS, t, LANES))
    dk = jnp.broadcast_to(jnp.exp((t - 1.0 - nf) * lg), (RET_HEADS, t, LANES))
    gt = jnp.broadcast_to(jnp.exp(float(t) * lg), (RET_HEADS, 1, LANES))
    return cos2, sin2, mask, dq, dk, gt


def _roll_half(x):
    return pltpu.roll(x, RET_HEAD_DIM // 2, 1)


def _ret_heads_fwd(ret_ref, cos, sin, m_ref, dq_ref, dk_ref, s0):
    hd = RET_HEAD_DIM
    heads = range(RET_HEADS)
    qb, kb, vb, kdb = [], [], [], []
    for h in heads:
        q = ret_ref[:, h * hd:(h + 1) * hd]
        k = ret_ref[:, 512 + h * hd:512 + (h + 1) * hd]
        kr = (k * cos + _roll_half(k) * sin) * RET_KSCALE
        qb.append(_mx(q * cos + _roll_half(q) * sin))
        kb.append(_mx(kr))
        kdb.append(_mx(kr * dk_ref[h]))
        vb.append(_mx(ret_ref[:, 1024 + h * hd:1024 + (h + 1) * hd]))
    p = [_dot_nt(qb[h], kb[h]) for h in heads]
    cross = [_dot(qb[h], _mx(s0[h])) for h in heads]
    pb = [_mx(p[h] * m_ref[h]) for h in heads]
    o = [_dot(pb[h], vb[h]) + cross[h] * dq_ref[h] for h in heads]
    gn, rstd = [], []
    for h in heads:
        oc = o[h] - jnp.mean(o[h], axis=-1, keepdims=True)
        rstd.append(lax.rsqrt(jnp.mean(oc * oc, axis=-1, keepdims=True) + EPS))
        gn.append(oc * rstd[h])
    return qb, kb, vb, pb, kdb, gn, rstd


def _retention_fwd(ret, tables):
    cos2, sin2, mask, dq, dk, gt = tables
    s = ret.shape[0]
    t = mask.shape[1]
    nt = s // t
    hd = RET_HEAD_DIM

    def body(ret_ref, cos_ref, sin_ref, m_ref, dq_ref, dk_ref, gt_ref, y_ref, st_ref, s_scr):
        i = pl.program_id(0)

        @pl.when(i == 0)
        def _():
            s_scr[...] = jnp.zeros_like(s_scr)

        s0 = [s_scr[h] for h in range(RET_HEADS)]
        _, _, vb, _, kdb, gn, _ = _ret_heads_fwd(ret_ref, cos_ref[...], sin_ref[...], m_ref, dq_ref, dk_ref, s0)
        kv = [_dot_tn(kdb[h], vb[h]) for h in range(RET_HEADS)]
        for h in range(RET_HEADS):
            g = ret_ref[:, 1536 + h * hd:1536 + (h + 1) * hd]
            st_ref[h] = s0[h]
            y_ref[:, h * hd:(h + 1) * hd] = (gn[h] * (g * _sigmoid(g))).astype(y_ref.dtype)
            s_scr[h] = s0[h] * gt_ref[h] + kv[h]

    full3 = lambda a: pl.BlockSpec(a.shape, lambda i: (0, 0, 0))
    return pl.pallas_call(
        body, name="retention_fwd", grid=(nt,),
        in_specs=[pl.BlockSpec((t, 2048), lambda i: (i, 0)),
                  pl.BlockSpec((t, LANES), lambda i: (i, 0)),
                  pl.BlockSpec((t, LANES), lambda i: (i, 0)),
                  full3(mask), full3(dq), full3(dk), full3(gt)],
        out_specs=[pl.BlockSpec((t, 512), lambda i: (i, 0)),
                   pl.BlockSpec((None, RET_HEADS, hd, hd), lambda i: (i, 0, 0, 0))],
        out_shape=[jax.ShapeDtypeStruct((s, 512), MXU_DTYPE),
                   jax.ShapeDtypeStruct((nt, RET_HEADS, hd, hd), F32)],
        scratch_shapes=[pltpu.VMEM((RET_HEADS, hd, hd), F32)],
        compiler_params=_params(("arbitrary",)),
    )(ret, cos2, sin2, mask, dq, dk, gt)


def _retention_bwd(ret, states, dy, tables):
    cos2, sin2, mask, dq, dk, gt = tables
    s = ret.shape[0]
    t = mask.shape[1]
    nt = s // t
    hd = RET_HEAD_DIM

    def body(ret_ref, st_ref, dy_ref, cos_ref, sin_ref, m_ref, dq_ref, dk_ref, gt_ref, d_ref, ds_scr):
        i = pl.program_id(0)

        @pl.when(i == 0)
        def _():
            ds_scr[...] = jnp.zeros_like(ds_scr)

        cos = cos_ref[...]
        sin = sin_ref[...]
        heads = range(RET_HEADS)
        s0 = [st_ref[h] for h in heads]
        ds = [ds_scr[h] for h in heads]
        dsb = [_mx(ds[h]) for h in heads]
        qb, kb, vb, pb, kdb, gn, rstd = _ret_heads_fwd(ret_ref, cos, sin, m_ref, dq_ref, dk_ref, s0)
        dob, dodb = [], []
        for h in heads:
            g = ret_ref[:, 1536 + h * hd:1536 + (h + 1) * hd]
            dyv = dy_ref[:, h * hd:(h + 1) * hd]
            sg = _sigmoid(g)
            d_ref[:, 1536 + h * hd:1536 + (h + 1) * hd] = (
                dyv * gn[h] * (sg * (1.0 + g * (1.0 - sg)))).astype(d_ref.dtype)
            dgn = dyv * (g * sg)
            do = rstd[h] * (dgn - jnp.mean(dgn, axis=-1, keepdims=True)
                            - gn[h] * jnp.mean(dgn * gn[h], axis=-1, keepdims=True))
            dob.append(_mx(do))
            dodb.append(_mx(do * dq_ref[h]))
        dp = [_dot_nt(dob[h], vb[h]) for h in heads]
        dv = [_dot_tn(pb[h], dob[h]) + _dot(kdb[h], dsb[h]) for h in heads]
        dq_cross = [_dot_nt(dodb[h], _mx(s0[h])) for h in heads]
        dk_cross = [_dot_nt(vb[h], dsb[h]) for h in heads]
        ds_new = [_dot_tn(qb[h], dodb[h]) for h in heads]
        dpb = [_mx(dp[h] * m_ref[h]) for h in heads]
        dqr = [_dot(dpb[h], kb[h]) + dq_cross[h] for h in heads]
        dkr = [(_dot_tn(dpb[h], qb[h]) + dk_cross[h] * dk_ref[h]) * RET_KSCALE for h in heads]
        for h in heads:
            d_ref[:, 1024 + h * hd:1024 + (h + 1) * hd] = dv[h].astype(d_ref.dtype)
            d_ref[:, h * hd:(h + 1) * hd] = (dqr[h] * cos + _roll_half(dqr[h] * sin)).astype(d_ref.dtype)
            d_ref[:, 512 + h * hd:512 + (h + 1) * hd] = (
                dkr[h] * cos + _roll_half(dkr[h] * sin)).astype(d_ref.dtype)
            ds_scr[h] = ds[h] * gt_ref[h] + ds_new[h]

    full3 = lambda a: pl.BlockSpec(a.shape, lambda i: (0, 0, 0))
    rev = lambda i: (nt - 1 - i, 0)
    return pl.pallas_call(
        body, name="retention_bwd", grid=(nt,),
        in_specs=[pl.BlockSpec((t, 2048), rev),
                  pl.BlockSpec((None, RET_HEADS, hd, hd), lambda i: (nt - 1 - i, 0, 0, 0)),
                  pl.BlockSpec((t, 512), rev),
                  pl.BlockSpec((t, LANES), rev),
                  pl.BlockSpec((t, LANES), rev),
                  full3(mask), full3(dq), full3(dk), full3(gt)],
        out_specs=pl.BlockSpec((t, 2048), rev),
        out_shape=jax.ShapeDtypeStruct((s, 2048), MXU_DTYPE),
        scratch_shapes=[pltpu.VMEM((RET_HEADS, hd, hd), F32)],
        compiler_params=_params(("arbitrary",)),
    )(ret, states, dy, cos2, sin2, mask, dq, dk, gt)


SB_BLOCK = 256
SB_CHUNK = 32


SB_SKIP = 104.0


def _split_dots(xs, u):
    parts = []
    for x in xs:
        hi = lax.bitcast_convert_type(lax.bitcast_convert_type(x, jnp.uint32) & jnp.uint32(0xFFFF0000), F32)
        parts += [_mx(hi), _mx(x - hi)]
    out = _dot(jnp.concatenate(parts, axis=0), u)
    n = xs[0].shape[0]
    return [out[2 * k * n:(2 * k + 1) * n] + out[(2 * k + 1) * n:(2 * k + 2) * n] for k in range(len(xs))]


def _split_dot(x, u):
    return _split_dots([x], u)[0]


def _sb_pair_weights(lb, lk, allowed, u_gt):
    blk = lb.shape[0]
    lk_p, lk_d = lk[:, :blk], lk[:, blk:]
    r_d = _rowsum(lk_d)
    cs_p, cs_d = _split_dots([lk_p, lk_d], u_gt)
    a = jnp.exp(lb + jnp.concatenate([cs_p + r_d, cs_d], axis=1))
    return jnp.where(allowed, a, 0.0), r_d, r_d + _rowsum(lk_p)


def _sb_logits(q, k, causal):
    z = _dot_nt(q, k)
    l1p = jnp.log(1.0 + jnp.exp(-jnp.abs(z)))
    lk = -(jnp.maximum(z, 0.0) + l1p)
    if causal is not None:
        lk = jnp.where(causal, lk, 0.0)
    return jnp.minimum(z, 0.0) - l1p, lk


def _sb_weights(lb, lk, r, u_gt, causal):
    a = jnp.exp(lb + _split_dot(lk, u_gt) + r)
    return a if causal is None else jnp.where(causal, a, 0.0)


def _rowsum(x):
    return jnp.sum(x, axis=1, keepdims=True)


def _sb_pair_tile(i, blk):
    row = lax.broadcasted_iota(jnp.int32, (blk, 2 * blk), 0)
    col = lax.broadcasted_iota(jnp.int32, (blk, 2 * blk), 1)
    first_col = jnp.where(i >= 1, 0, blk)
    allowed = jnp.logical_and(row > col - blk, col >= first_col)
    rows_p = pl.ds(pl.multiple_of(jnp.maximum(i - 1, 0) * blk, blk), blk)
    rows_d = pl.ds(pl.multiple_of(i * blk, blk), blk)
    return allowed, rows_p, rows_d


def _sb_fwd(qkv, sg):
    s = qkv.shape[0]
    blk = min(SB_BLOCK, s)
    nq = s // blk
    hd = SB_HEAD_DIM

    ch = min(SB_CHUNK, blk)

    def body(q_ref, k_ref, v_ref, g_ref, y_ref, o_ref, z_scr, lhs_scr, cs_scr, a_scr):
        i = pl.program_id(1)
        row = lax.broadcasted_iota(jnp.int32, (blk, blk), 0)
        col = lax.broadcasted_iota(jnp.int32, (blk, blk), 1)
        u_gt = (row > col).astype(MXU_DTYPE)
        heads = [slice(hh * hd, (hh + 1) * hd) for hh in range(2)]
        qs = [q_ref[:, ls] for ls in heads]
        _, rows_p, rows_d = _sb_pair_tile(i, blk)
        has_prev = i >= 1
        crow = lax.broadcasted_iota(jnp.int32, (ch, blk), 0)
        ccol = lax.broadcasted_iota(jnp.int32, (ch, blk), 1)

        def logits(hh):
            kc = jnp.concatenate([k_ref[rows_p, heads[hh]], k_ref[rows_d, heads[hh]]], axis=0)
            z_scr[hh] = _dot_nt(qs[hh], kc)

        def keep_parts(hh):
            r_d, r_all = [], []
            for c in range(blk // ch):
                rows = pl.ds(c * ch, ch)
                causal = crow + c * ch > ccol
                z = z_scr[hh, rows, :]
                l1p = jnp.log(1.0 + jnp.exp(-jnp.abs(z)))
                lk = -(jnp.maximum(z, 0.0) + l1p)
                z_scr[hh, rows, :] = jnp.minimum(z, 0.0) - l1p
                lk_p = jnp.where(has_prev, lk[:, :blk], 0.0)
                lk_d = jnp.where(causal, lk[:, blk:], 0.0)
                for part, x in enumerate((lk_p, lk_d)):
                    hi = lax.bitcast_convert_type(
                        lax.bitcast_convert_type(x, jnp.uint32) & jnp.uint32(0xFFFF0000), F32)
                    lhs_scr[hh, pl.ds((2 * part) * blk + c * ch, ch), :] = _mx(hi)
                    lhs_scr[hh, pl.ds((2 * part + 1) * blk + c * ch, ch), :] = _mx(x - hi)
                r_d.append(_rowsum(lk_d))
                r_all.append(r_d[c] + _rowsum(lk_p))
            return r_d, jnp.concatenate(r_all, axis=0)

        def suffix_sums(hh):
            cs_scr[hh] = _dot(lhs_scr[hh], u_gt)

        def weights(hh, r_d):
            for c in range(blk // ch):
                rows = pl.ds(c * ch, ch)
                causal = crow + c * ch > ccol
                cs_p = cs_scr[hh, pl.ds(c * ch, ch), :] + cs_scr[hh, pl.ds(blk + c * ch, ch), :] + r_d[c]
                cs_d = cs_scr[hh, pl.ds(2 * blk + c * ch, ch), :] + cs_scr[hh, pl.ds(3 * blk + c * ch, ch), :]
                lb = z_scr[hh, rows, :]
                a_p = jnp.where(has_prev, jnp.exp(lb[:, :blk] + cs_p), 0.0)
                a_d = jnp.where(causal, jnp.exp(lb[:, blk:] + cs_d), 0.0)
                a_scr[hh, rows, :] = _mx(jnp.concatenate([a_p, a_d], axis=1))

        def values(hh):
            vc = jnp.concatenate([v_ref[rows_p, heads[hh]], v_ref[rows_d, heads[hh]]], axis=0)
            return _dot(a_scr[hh], vc)

        def block(hh, j, r):
            start = pl.multiple_of(j * blk, blk)
            lb, lk = _sb_logits(qs[hh], k_ref[pl.ds(start, blk), heads[hh]], None)
            a = _sb_weights(lb, lk, r, u_gt, None)
            return _dot(_mx(a), v_ref[pl.ds(start, blk), heads[hh]]), r + _rowsum(lk)

        def more(n, r0, r1):
            return jnp.logical_and(n < i, jnp.max(jnp.maximum(r0, r1)) > -SB_SKIP)

        logits(0)
        logits(1)
        rd0, r0 = keep_parts(0)
        suffix_sums(0)
        rd1, r1 = keep_parts(1)
        suffix_sums(1)
        go = more(jnp.int32(1), r0, r1)
        gates = [g_ref[:, ls] * _sigmoid(g_ref[:, ls]) for ls in heads]
        weights(0, rd0)
        acc0 = values(0)
        weights(1, rd1)
        acc1 = values(1)

        def step(c):
            _, n, acc0, r0, acc1, r1 = c
            pv0, r0 = block(0, i - 1 - n, r0)
            pv1, r1 = block(1, i - 1 - n, r1)
            return more(n + 1, r0, r1), n + 1, acc0 + pv0, r0, acc1 + pv1, r1

        _, _, acc0, _, acc1, _ = lax.while_loop(lambda c: c[0], step, (go, jnp.int32(1), acc0, r0, acc1, r1))
        for ls, acc, gate in zip(heads, (acc0, acc1), gates):
            o_ref[:, ls] = acc
            y_ref[:, ls] = (acc * gate).astype(y_ref.dtype)

    qblk = pl.BlockSpec((blk, LANES), lambda p, i: (i, p))
    return pl.pallas_call(
        body, name="stickbreak_fwd", grid=(SB_HEADS // 2, nq),
        in_specs=[qblk,
                  pl.BlockSpec((s, LANES), lambda p, i: (0, 4 + p)),
                  pl.BlockSpec((s, LANES), lambda p, i: (0, 8 + p)),
                  qblk],
        out_specs=[qblk, qblk],
        out_shape=[jax.ShapeDtypeStruct((s, 512), MXU_DTYPE),
                   jax.ShapeDtypeStruct((s, 512), F32)],
        scratch_shapes=[pltpu.VMEM((2, blk, 2 * blk), F32),
                        pltpu.VMEM((2, 4 * blk, blk), MXU_DTYPE),
                        pltpu.VMEM((2, 4 * blk, blk), F32),
                        pltpu.VMEM((2, blk, 2 * blk), MXU_DTYPE)],
        compiler_params=_params(("arbitrary", "arbitrary")),
    )(qkv, qkv, qkv, sg)


def _sb_bwd(qkv, sg, o, dy):
    s = qkv.shape[0]
    blk = min(SB_BLOCK, s)
    nq = s // blk
    hd = SB_HEAD_DIM
    assert nq <= LANES
    ch = min(SB_CHUNK, blk)

    def body(q_ref, k_ref, v_ref, g_ref, o_ref, dy_ref, dq_ref, dk_ref, dv_ref, dg_ref, dk_scr, dv_scr,
             z_scr, g_scr, lhs_scr, cs_scr, a_scr, dz_scr):
        i = pl.program_id(1)

        @pl.when(i == 0)
        def _():
            dk_scr[...] = jnp.zeros_like(dk_scr)
            dv_scr[...] = jnp.zeros_like(dv_scr)

        row = lax.broadcasted_iota(jnp.int32, (blk, blk), 0)
        col = lax.broadcasted_iota(jnp.int32, (blk, blk), 1)
        lane = lax.broadcasted_iota(jnp.int32, (blk, LANES), 1)
        u_gt = (row > col).astype(MXU_DTYPE)
        u_lt = (row < col).astype(MXU_DTYPE)
        heads = [slice(hh * hd, (hh + 1) * hd) for hh in range(2)]
        qs = [q_ref[:, ls] for ls in heads]
        _, rows_p, rows_d = _sb_pair_tile(i, blk)
        has_prev = i >= 1
        crow = lax.broadcasted_iota(jnp.int32, (ch, blk), 0)
        ccol = lax.broadcasted_iota(jnp.int32, (ch, blk), 1)
        nch = blk // ch
        dobs, kcs = [], []
        for hh, ls in enumerate(heads):
            g = g_ref[:, ls]
            dyv = dy_ref[:, ls]
            sgm = _sigmoid(g)
            dg_ref[:, ls] = (dyv * o_ref[:, ls] * (sgm * (1.0 + g * (1.0 - sgm)))).astype(dg_ref.dtype)
            dobs.append(_mx(dyv * (g * sgm)))
            kcs.append(jnp.concatenate([k_ref[rows_p, ls], k_ref[rows_d, ls]], axis=0))

        def split_rows(hh, c, part, x):
            hi = lax.bitcast_convert_type(lax.bitcast_convert_type(x, jnp.uint32) & jnp.uint32(0xFFFF0000), F32)
            lhs_scr[hh, pl.ds((2 * part) * blk + c * ch, ch), :] = _mx(hi)
            lhs_scr[hh, pl.ds((2 * part + 1) * blk + c * ch, ch), :] = _mx(x - hi)

        def summed_rows(hh, c, part):
            return (cs_scr[hh, pl.ds((2 * part) * blk + c * ch, ch), :]
                    + cs_scr[hh, pl.ds((2 * part + 1) * blk + c * ch, ch), :])

        def logits(hh):
            z_scr[hh] = _dot_nt(qs[hh], kcs[hh])
            vc = jnp.concatenate([v_ref[rows_p, heads[hh]], v_ref[rows_d, heads[hh]]], axis=0)
            g_scr[hh] = _dot_nt(dobs[hh], vc)

        def keep_parts(hh):
            r_d, r_all = [], []
            for c in range(nch):
                rows = pl.ds(c * ch, ch)
                z = z_scr[hh, rows, :]
                l1p = jnp.log(1.0 + jnp.exp(-jnp.abs(z)))
                lk = -(jnp.maximum(z, 0.0) + l1p)
                z_scr[hh, rows, :] = jnp.minimum(z, 0.0) - l1p
                lk_p = jnp.where(has_prev, lk[:, :blk], 0.0)
                lk_d = jnp.where(crow + c * ch > ccol, lk[:, blk:], 0.0)
                split_rows(hh, c, 0, lk_p)
                split_rows(hh, c, 1, lk_d)
                r_d.append(_rowsum(lk_d))
                r_all.append(r_d[c] + _rowsum(lk_p))
            return r_d, jnp.concatenate(r_all, axis=0)

        def weights(hh, r_d):
            g_p = []
            for c in range(nch):
                rows = pl.ds(c * ch, ch)
                lb = z_scr[hh, rows, :]
                a_p = jnp.where(has_prev, jnp.exp(lb[:, :blk] + (summed_rows(hh, c, 0) + r_d[c])), 0.0)
                a_d = jnp.where(crow + c * ch > ccol, jnp.exp(lb[:, blk:] + summed_rows(hh, c, 1)), 0.0)
                a = jnp.concatenate([a_p, a_d], axis=1)
                a_scr[hh, rows, :] = _mx(a)
                gm = g_scr[hh, rows, :] * a
                g_scr[hh, rows, :] = gm
                split_rows(hh, c, 0, gm[:, :blk])
                split_rows(hh, c, 1, gm[:, blk:])
                g_p.append(_rowsum(gm[:, :blk]))
            return g_p

        def logit_grads(hh, pg, g_p):
            for c in range(nch):
                rows = pl.ds(c * ch, ch)
                pre = jnp.concatenate([summed_rows(hh, c, 0) + pg[c * ch:(c + 1) * ch],
                                       summed_rows(hh, c, 1) + (pg[c * ch:(c + 1) * ch] + g_p[c])], axis=1)
                gm = g_scr[hh, rows, :]
                dz = gm - (gm + pre) * jnp.exp(z_scr[hh, rows, :])
                dz_p = jnp.where(has_prev, dz[:, :blk], 0.0)
                dz_d = jnp.where(crow + c * ch > ccol, dz[:, blk:], 0.0)
                dz_scr[hh, rows, :] = _mx(jnp.concatenate([dz_p, dz_d], axis=1))

        def products(hh, acc):
            ls = heads[hh]
            dk_scr[hh, rows_p, :] += _dot_tn(dz_scr[hh, :, 0:blk], qs[hh])
            dk_scr[hh, rows_d, :] += _dot_tn(dz_scr[hh, :, blk:2 * blk], qs[hh])
            dv_scr[hh, rows_p, :] += _dot_tn(a_scr[hh, :, 0:blk], dobs[hh])
            dv_scr[hh, rows_d, :] += _dot_tn(a_scr[hh, :, blk:2 * blk], dobs[hh])
            dq_ref[:, ls] = ((acc + _dot(dz_scr[hh], kcs[hh])) * SB_SCALE).astype(dq_ref.dtype)

        def suffix_sums(hh):
            cs_scr[hh] = _dot(lhs_scr[hh], u_gt)

        def prefix_sums(hh):
            cs_scr[hh] = _dot(lhs_scr[hh], u_lt)

        def more(n, r0, r1):
            return jnp.logical_and(n < i, jnp.max(jnp.maximum(r0, r1)) > -SB_SKIP)

        logits(0)
        logits(1)
        rd0, ra0 = keep_parts(0)
        suffix_sums(0)
        rd1, ra1 = keep_parts(1)
        suffix_sums(1)
        go = more(jnp.int32(1), ra0, ra1)
        gp0 = weights(0, rd0)
        prefix_sums(0)
        gp1 = weights(1, rd1)
        prefix_sums(1)

        def scan_block(hh, j, r, rmat):
            start = pl.multiple_of(j * blk, blk)
            _, lk = _sb_logits(qs[hh], k_ref[pl.ds(start, blk), heads[hh]], None)
            return r + _rowsum(lk), jnp.where(lane == j, r, rmat)

        def scan_step(c):
            _, n, r0, rmat0, r1, rmat1 = c
            r0, rmat0 = scan_block(0, i - 1 - n, r0, rmat0)
            r1, rmat1 = scan_block(1, i - 1 - n, r1, rmat1)
            return more(n + 1, r0, r1), n + 1, r0, rmat0, r1, rmat1

        zmat = jnp.zeros((blk, LANES), F32)
        _, n, _, rmat0, _, rmat1 = lax.while_loop(lambda c: c[0], scan_step,
                                                  (go, jnp.int32(1), ra0, zmat, ra1, zmat))
        rmats = (rmat0, rmat1)

        def block(hh, j, pg):
            ls = heads[hh]
            start = pl.multiple_of(j * blk, blk)
            k = k_ref[pl.ds(start, blk), ls]
            lb, lk = _sb_logits(qs[hh], k, None)
            r = _rowsum(jnp.where(lane == j, rmats[hh], 0.0))
            a = _sb_weights(lb, lk, r, u_gt, None)
            gm = _dot_nt(dobs[hh], v_ref[pl.ds(start, blk), ls]) * a
            dzb = _mx(gm - (gm + (pg + _split_dot(gm, u_lt))) * jnp.exp(lb))
            dk_scr[hh, pl.ds(start, blk), :] += _dot_tn(dzb, qs[hh])
            dv_scr[hh, pl.ds(start, blk), :] += _dot_tn(_mx(a), dobs[hh])
            return _dot(dzb, k), pg + _rowsum(gm)

        def step(t, c):
            acc0, pg0, acc1, pg1 = c
            dq0, pg0 = block(0, i - n + t, pg0)
            dq1, pg1 = block(1, i - n + t, pg1)
            return acc0 + dq0, pg0, acc1 + dq1, pg1

        zero = jnp.zeros((blk, 1), F32)
        zacc = jnp.zeros((blk, hd), F32)
        acc0, pg0, acc1, pg1 = lax.fori_loop(0, n - 1, step, (zacc, zero, zacc, zero))
        logit_grads(0, pg0, gp0)
        logit_grads(1, pg1, gp1)
        products(0, acc0)
        products(1, acc1)

        @pl.when(i == nq - 1)
        def _():
            for hh in range(2):
                ls = slice(hh * hd, (hh + 1) * hd)
                dk_ref[:, ls] = dk_scr[hh].astype(dk_ref.dtype)
                dv_ref[:, ls] = dv_scr[hh].astype(dv_ref.dtype)

    qblk = lambda c0: pl.BlockSpec((blk, LANES), lambda p, i: (i, c0 + p))
    full = lambda c0: pl.BlockSpec((s, LANES), lambda p, i: (0, c0 + p))
    half = jax.ShapeDtypeStruct((s, 512), MXU_DTYPE)
    return pl.pallas_call(
        body, name="stickbreak_bwd", grid=(SB_HEADS // 2, nq),
        in_specs=[qblk(0), full(4), full(8), qblk(0), qblk(0), qblk(4)],
        out_specs=[qblk(0), full(0), full(0), qblk(0)],
        out_shape=[half, half, half, half],
        scratch_shapes=[pltpu.VMEM((2, s, hd), F32), pltpu.VMEM((2, s, hd), F32),
                        pltpu.VMEM((2, blk, 2 * blk), F32),
                        pltpu.VMEM((2, blk, 2 * blk), F32),
                        pltpu.VMEM((2, 4 * blk, blk), MXU_DTYPE),
                        pltpu.VMEM((2, 4 * blk, blk), F32),
                        pltpu.VMEM((2, blk, 2 * blk), MXU_DTYPE),
                        pltpu.VMEM((2, blk, 2 * blk), MXU_DTYPE)],
        compiler_params=_params(("arbitrary", "arbitrary")),
    )(qkv, qkv, qkv, sg, o, dy)


def _layer_fwd(layer, x, mod, norm_g, w_in_g, w_out_g, tables, gather=()):
    shift, scale1p, gate = mod[0:1], 1.0 + mod[1:2], mod[2:3]
    ret, qkv, sg, *gathered = _ln_proj(x, shift, scale1p, norm_g, w_in_g, layer, gather)
    y_r, states = _retention_fwd(ret, tables)
    y_s, o_s = _sb_fwd(qkv, sg)
    x_next = _out_proj(x, gate, y_r, y_s, w_out_g, layer)
    saved = (x, shift, scale1p, gate, ret, qkv, sg, y_r, states, y_s, o_s)
    return x_next, saved, gathered


def _layer_bwd(layer, me, dx_out, saved, norm_g, w_in_g, w_out_g, tables):
    x, shift, scale1p, gate, ret, qkv, sg, y_r, states, y_s, o_s = saved
    dy, dw_out, dgate = _out_proj_bwd(dx_out, gate, y_r, y_s, w_out_g, layer)
    d_ret = _retention_bwd(ret, states, dy, tables)
    d_q, d_k, d_v, d_g = _sb_bwd(qkv, sg, o_s, dy)
    dproj = (d_ret, d_q, d_k, d_v, d_g)
    dx, dshift, dscale, dnorm_g = _in_proj_bwd_x(x, dx_out, dproj, shift, scale1p, norm_g, w_in_g, layer)
    dwo_parts = _mx(dw_out.reshape(N_DEV, D_MODEL // N_DEV, D_MODEL))
    r_in, r_out = _in_proj_bwd_w(me, x, dproj, shift, scale1p, norm_g, dwo_parts)
    dmod = jnp.concatenate([dshift, dscale, dgate], axis=1)
    return dx, r_in, r_out, dmod, dnorm_g


MESH_IDS = pl.DeviceIdType.MESH
N_PEERS = N_DEV - 1
HBM_SPEC = pl.BlockSpec(memory_space=pl.ANY)


def _my_place():
    return lax.axis_index("x"), lax.axis_index("y"), lax.axis_index("c")


def _linear(px, py, pc):
    return 4 * px + 2 * py + pc


def _all_gather(blocks):
    n_arr = len(blocks)

    def body(*refs):
        start, forward, finish = _gather_plan(refs[:n_arr], refs[n_arr:2 * n_arr], *refs[2 * n_arr:])
        start()
        forward()
        finish()

    return pl.pallas_call(
        body, name="all_gather",
        out_shape=_gathered_shapes(blocks),
        in_specs=[HBM_SPEC] * n_arr, out_specs=[HBM_SPEC] * n_arr,
        scratch_shapes=_gather_sems(n_arr),
    )(*blocks)


def _gathered_shapes(blocks):
    return [jax.ShapeDtypeStruct((N_DEV * b.shape[0], b.shape[1]), b.dtype) for b in blocks]


def _gather_sems(n_arr):
    return [pltpu.SemaphoreType.DMA((n_arr * N_PEERS,)), pltpu.SemaphoreType.DMA((n_arr * N_PEERS,)),
            pltpu.SemaphoreType.DMA((n_arr,))]


def _gather_plan(x_refs, out_refs, send_sems, recv_sems, local_sems):
    n_arr = len(x_refs)
    x, y, c = _my_place()
    me, sibling = (x, y, c), (x, y, 1 - c)
    chips = [(1 - x, y), (x, 1 - y), (1 - x, 1 - y)]

    def rows(a, place):
        m = x_refs[a].shape[0]
        return out_refs[a].at[pl.ds(_linear(*place) * m, m), :]

    def copy(a, k, block, to, src=None):
        return pltpu.make_async_remote_copy(
            src_ref=rows(a, block) if src is None else src, dst_ref=rows(a, block),
            send_sem=send_sems.at[a * N_PEERS + k], recv_sem=recv_sems.at[a * N_PEERS + k],
            device_id=to, device_id_type=MESH_IDS)

    mine = [pltpu.make_async_copy(x_refs[a], rows(a, me), local_sems.at[a]) for a in range(n_arr)]
    first = []
    for a in range(n_arr):
        first.append(copy(a, 0, me, sibling, src=x_refs[a]))
        first += [copy(a, 1 + j, me, (*chip, c), src=x_refs[a]) for j, chip in enumerate(chips)]
    passed = [copy(a, 4 + j, (*chip, c), sibling) for j, chip in enumerate(chips) for a in range(n_arr)]

    def start():
        for cp in mine + first:
            cp.start()

    def forward():
        for j, chip in enumerate(chips):
            for a in range(n_arr):
                copy(a, 1 + j, (*chip, c), me).wait_recv()
                passed[j * n_arr + a].start()

    def finish():
        for a in range(n_arr):
            copy(a, 0, sibling, me).wait_recv()
            for j, chip in enumerate(chips):
                copy(a, 4 + j, (*chip, 1 - c), me).wait_recv()
        for cp in first + passed:
            cp.wait_send()
        for cp in mine:
            cp.wait()

    return start, forward, finish


def _ada_fwd(c_all, w_ada, b_cols):
    cols = w_ada.shape[2]

    def body(c_ref, w_ref, b_ref, ca_ref, mod_ref):
        cv = c_ref[...]
        ca = cv * _sigmoid(cv)
        ca_ref[...] = ca
        cb = _mx(jnp.concatenate([ca, ca], axis=0))
        for l in range(DEPTH):
            mod_ref[l * N_DEV:(l + 1) * N_DEV, :] = _dot(cb, _mx(w_ref[l]))[0:N_DEV] + b_ref[l]

    return pl.pallas_call(
        body, name="ada_fwd",
        out_shape=[jax.ShapeDtypeStruct((N_DEV, D_MODEL), F32),
                   jax.ShapeDtypeStruct((DEPTH * N_DEV, cols), F32)],
        compiler_params=_params(),
    )(c_all, w_ada, b_cols)


def _ada_bwd(c_act_t, dmod_cols):
    cols = dmod_cols.shape[2]

    def body(ca_ref, dm_ref, o_ref):
        ca = _mx(ca_ref[...]).astype(F32)
        for l in range(DEPTH):
            o_ref[l] = jnp.dot(ca, _mx(dm_ref[l]).astype(F32),
                               precision=lax.Precision.HIGHEST, preferred_element_type=F32)

    return pl.pallas_call(
        body, name="ada_bwd",
        out_shape=jax.ShapeDtypeStruct((DEPTH, D_MODEL, cols), F32),
        compiler_params=_params(),
    )(c_act_t, dmod_cols)


def _adamw_store(g, w_ref, m_ref, v_ref, g_ref, d_ref, mo_ref, vo_ref):
    m2 = ADAM_B1 * m_ref[...] + (1.0 - ADAM_B1) * g
    v2 = ADAM_B2 * v_ref[...] + (1.0 - ADAM_B2) * (g * g)
    m_hat = m2 / (1.0 - ADAM_B1 ** ADAM_STEP)
    v_hat = v2 / (1.0 - ADAM_B2 ** ADAM_STEP)
    g_ref[...] = g
    d_ref[...] = -ADAM_LR * (m_hat / (jnp.sqrt(v_hat) + ADAM_EPS) + ADAM_WD * w_ref[...])
    mo_ref[...] = m2
    vo_ref[...] = v2


def _slab_sum(p_ref):
    g = p_ref[0].astype(F32)
    for sl in range(1, p_ref.shape[0]):
        g = g + p_ref[sl].astype(F32)
    return g


def _sum_adamw_layers(parts, w, m, v):
    n_slab, rows, cols = parts[0].shape
    tr = min(256, rows)
    nt = rows // tr

    def body(p0_ref, p1_ref, w_ref, m_ref, v_ref, g_ref, d_ref, mo_ref, vo_ref):
        for l, p_ref in enumerate((p0_ref, p1_ref)):
            @pl.when(pl.program_id(0) == l)
            def _():
                _adamw_store(_slab_sum(p_ref), w_ref, m_ref, v_ref, g_ref, d_ref, mo_ref, vo_ref)

    p_specs = [pl.BlockSpec((n_slab, tr, cols), lambda l, i: (0, i * (1 - l) + (nt - 1) * l, 0)),
               pl.BlockSpec((n_slab, tr, cols), lambda l, i: (0, i * l, 0))]
    blk = pl.BlockSpec((None, tr, cols), lambda l, i: (l, i, 0))
    shp = jax.ShapeDtypeStruct((DEPTH, rows, cols), F32)
    return pl.pallas_call(
        body, name="sum_adamw_layers", grid=(DEPTH, nt),
        in_specs=p_specs + [blk, blk, blk],
        out_specs=[blk, blk, blk, blk],
        out_shape=[shp, shp, shp, shp],
        compiler_params=_params(("arbitrary", "arbitrary")),
    )(parts[0], parts[1], w, m, v)


def _sum_adamw(parts, w, m, v):
    n_slab, rows, cols = parts.shape
    tr = min(256, rows)

    def body(p_ref, w_ref, m_ref, v_ref, g_ref, d_ref, mo_ref, vo_ref):
        _adamw_store(_slab_sum(p_ref), w_ref, m_ref, v_ref, g_ref, d_ref, mo_ref, vo_ref)

    blk = pl.BlockSpec((tr, cols), lambda i: (i, 0))
    shp = jax.ShapeDtypeStruct((rows, cols), F32)
    return pl.pallas_call(
        body, name="sum_adamw", grid=(rows // tr,),
        in_specs=[pl.BlockSpec((n_slab, tr, cols), lambda i: (0, i, 0)), blk, blk, blk],
        out_specs=[blk, blk, blk, blk],
        out_shape=[shp, shp, shp, shp],
        compiler_params=_params(("arbitrary",)),
    )(parts, w, m, v)


SMALL_ROWS = 16


def kernel(x, c, norm_g, w_ada, b_ada, w_in, w_out, final_g, loss_target, m_norm_g, m_w_ada, m_b_ada, m_w_in, m_w_out, m_final_g, v_norm_g, v_w_ada, v_b_ada, v_w_in, v_w_out, v_final_g):
    me = _linear(*_my_place())
    in_cols = w_in.shape[2]
    out_rows = w_out.shape[1]
    ada_cols = w_ada.shape[2]

    w_in_m, w_out_m = _mx(w_in), _mx(w_out)
    g_in, g_out, g_c = _all_gather([w_in_m[0], w_out_m[0], jnp.broadcast_to(c, (8, D_MODEL))])
    w_in_g = [g_in.reshape(N_DEV, 1, D_MODEL, in_cols), None]
    w_out_g = [g_out.reshape(N_DEV, 1, out_rows, D_MODEL), None]
    c_all = g_c.reshape(N_DEV, 8, D_MODEL)[:, 0]

    b_cols = lax.dynamic_slice_in_dim(b_ada, me * ada_cols, ada_cols, axis=1)[:, None, :]
    c_act, mod_cols = _ada_fwd(c_all, w_ada, b_cols)
    (g_mod,) = _all_gather([mod_cols])
    g_mod = g_mod.reshape(N_DEV, DEPTH, N_DEV, ada_cols)
    mod = lax.dynamic_index_in_dim(g_mod, me, axis=2, keepdims=False)
    mod = mod.transpose(1, 0, 2).reshape(DEPTH, 3, D_MODEL)

    tables = _ret_tables(x.shape[1])
    h = x[0]
    saved = []
    for l in range(DEPTH):
        nxt = (w_in_m[l + 1], w_out_m[l + 1]) if l + 1 < DEPTH else ()
        h, sv, gathered = _layer_fwd(0, h, mod[l], norm_g[l:l + 1], w_in_g[l], w_out_g[l], tables, nxt)
        if nxt:
            w_in_g[l + 1] = gathered[0].reshape(N_DEV, 1, D_MODEL, in_cols)
            w_out_g[l + 1] = gathered[1].reshape(N_DEV, 1, out_rows, D_MODEL)
        saved.append(sv)
    dx, loss_part, dfg = _final_loss(h, final_g[None], loss_target[0])
    r_in, r_out, small = [None] * DEPTH, [None] * DEPTH, [None] * DEPTH
    for l in reversed(range(DEPTH)):
        dx, r_in[l], r_out[l], dmod, dng = _layer_bwd(0, me, dx, saved[l], norm_g[l:l + 1], w_in_g[l], w_out_g[l], tables)
        small[l] = (dmod.reshape(3, D_MODEL), dng)

    pad = jnp.zeros((SMALL_ROWS - 10, D_MODEL), F32)
    small_block = jnp.concatenate([small[0][0], small[1][0], small[0][1], small[1][1], dfg,
                                   jnp.broadcast_to(loss_part, (1, D_MODEL)), pad], axis=0)
    (g_small,) = _all_gather([small_block])
    g_small = g_small.reshape(N_DEV, SMALL_ROWS, D_MODEL)

    def small_pack(b, n, f, fill):
        return jnp.concatenate([b.reshape(6, D_MODEL), n, f[None],
                                jnp.full((SMALL_ROWS - 9, D_MODEL), fill, F32)], axis=0)

    s_g, s_d, s_m, s_v = _sum_adamw(g_small, small_pack(b_ada, norm_g, final_g, 0.0),
                                    small_pack(m_b_ada, m_norm_g, m_final_g, 0.0),
                                    small_pack(v_b_ada, v_norm_g, v_final_g, 1.0))
    loss = s_g[9, 0]

    def small_unpack(a):
        return a[0:6].reshape(DEPTH, 3 * D_MODEL), a[6:8], a[8]

    dmod_all = g_small[:, 0:6].reshape(N_DEV, DEPTH, 3 * D_MODEL).transpose(1, 0, 2)
    dmod_cols = lax.dynamic_slice_in_dim(dmod_all, me * ada_cols, ada_cols, axis=2)
    g_ada = _ada_bwd(c_act.T, dmod_cols).reshape(1, DEPTH * D_MODEL, ada_cols)
    ada = _sum_adamw(g_ada, *[a.reshape(DEPTH * D_MODEL, ada_cols) for a in (w_ada, m_w_ada, v_w_ada)])
    ada = [a.reshape(DEPTH, D_MODEL, ada_cols) for a in ada]

    win = _sum_adamw_layers(r_in, w_in, m_w_in, v_w_in)
    wout = _sum_adamw_layers(r_out, w_out, m_w_out, v_w_out)

    outs = [loss, dx[None]]
    for k in range(4):
        b, n, f = small_unpack((s_g, s_d, s_m, s_v)[k])
        outs += [n, ada[k], b, win[k], wout[k], f]
    return tuple(outs)
```

```python
import functools

import jax
import jax.numpy as jnp
from jax import lax
from jax.experimental import pallas as pl
from jax.experimental.pallas import tpu as pltpu

F32 = jnp.float32
MXU_DTYPE = jnp.bfloat16

D_MODEL = 1024
DEPTH = 2
N_DEV = 8
CHUNK = 64
D_RET = 512
D_SB = 512
RET_HEADS = 4
RET_HEAD_DIM = 128
SB_HEADS = 8
SB_HEAD_DIM = 64
D_IN = 4096
ROPE_BASE = 10000.0
EPS = 1e-6
SB_SCALE = SB_HEAD_DIM ** -0.5
RET_KSCALE = RET_HEAD_DIM ** -0.5

ADAM_LR = 0.001
ADAM_B1 = 0.9
ADAM_B2 = 0.999
ADAM_EPS = 1e-08
ADAM_WD = 0.01
ADAM_STEP = 10

V7X_VMEM_BYTES = 64 * 2 ** 20
VMEM_LIMIT = V7X_VMEM_BYTES - 8 * 2 ** 20
LANES = 128

_NT = (((1,), (1,)), ((), ()))
_TN = (((0,), (0,)), ((), ()))


def _dot(a, b):
    return jnp.dot(a, b, preferred_element_type=F32)


def _dot_nt(a, b):
    return lax.dot_general(a, b, _NT, preferred_element_type=F32)


def _dot_tn(a, b):
    return lax.dot_general(a, b, _TN, preferred_element_type=F32)


def _mx(x):
    return x.astype(MXU_DTYPE)


def _sigmoid(x):
    return 1.0 / (1.0 + jnp.exp(-x))


def _params(sem=None):
    return pltpu.CompilerParams(dimension_semantics=sem, vmem_limit_bytes=VMEM_LIMIT)


def _row_tile(s):
    return min(512, s)


def _w_in_spec(w_in_g, layer):
    return pl.BlockSpec((N_DEV, None) + w_in_g.shape[2:], lambda i: (0, layer, 0, 0))


def _w_out_spec(w_out_g, layer):
    return pl.BlockSpec((N_DEV, None) + w_out_g.shape[2:], lambda i: (0, layer, 0, 0))


def _ln_proj(x, shift, scale1p, g, w_in_g, layer, gather=()):
    s = x.shape[0]
    ts = _row_tile(s)
    ns = s // ts
    n_g = len(gather)

    def body(x_ref, sh_ref, sc_ref, g_ref, w_ref, *rest):
        ret_ref, qkv_ref, sg_ref = rest[n_g:n_g + 3]
        if n_g:
            start, forward, finish = _gather_plan(rest[:n_g], rest[n_g + 3:2 * n_g + 3], *rest[2 * n_g + 3:])
            i = pl.program_id(0)
            pl.when(i == 0)(start)
            pl.when(i == max(ns - 4, 0))(forward)
        xv = x_ref[...]
        rstd = lax.rsqrt(jnp.mean(xv * xv, axis=-1, keepdims=True) + EPS)
        h = (xv * rstd * g_ref[...]) * sc_ref[...] + sh_ref[...]
        hb = _mx(h)
        for n in range(4):
            ret_ref[:, n * 512:(n + 1) * 512] = _dot(hb, w_ref[n])
        qkv_ref[:, 0:512] = _mx(_dot(hb, w_ref[4]) * SB_SCALE)
        qkv_ref[:, 512:1024] = _mx(_dot(hb, w_ref[5]))
        qkv_ref[:, 1024:1536] = _mx(_dot(hb, w_ref[6]))
        sg_ref[...] = _dot(hb, w_ref[7])
        if n_g:
            pl.when(i == ns - 1)(finish)

    vec = pl.BlockSpec((1, D_MODEL), lambda i: (0, 0))
    return pl.pallas_call(
        body, name="ln_proj_gather" if n_g else "ln_proj", grid=(ns,),
        in_specs=[pl.BlockSpec((ts, D_MODEL), lambda i: (i, 0)), vec, vec, vec,
                  _w_in_spec(w_in_g, layer)] + [HBM_SPEC] * n_g,
        out_specs=[pl.BlockSpec((ts, 2048), lambda i: (i, 0)),
                   pl.BlockSpec((ts, 1536), lambda i: (i, 0)),
                   pl.BlockSpec((ts, 512), lambda i: (i, 0))] + [HBM_SPEC] * n_g,
        out_shape=[jax.ShapeDtypeStruct((s, 2048), F32),
                   jax.ShapeDtypeStruct((s, 1536), MXU_DTYPE),
                   jax.ShapeDtypeStruct((s, 512), F32)] + _gathered_shapes(gather),
        scratch_shapes=_gather_sems(n_g) if n_g else (),
        compiler_params=_params(("arbitrary",)),
    )(x, shift, scale1p, g, w_in_g, *gather)


def _w_out_halves(w_ref):
    half = N_DEV // 2
    return (w_ref[0:half].reshape(D_RET, D_MODEL), w_ref[half:N_DEV].reshape(D_SB, D_MODEL))


def _out_proj(x, gate, y_r, y_s, w_out_g, layer):
    s = x.shape[0]
    ts = _row_tile(s)

    def body(x_ref, gate_ref, yr_ref, ys_ref, w_ref, o_ref):
        w_r, w_s = _w_out_halves(w_ref)
        t = _dot(yr_ref[...], w_r) + _dot(ys_ref[...], w_s)
        o_ref[...] = x_ref[...] + gate_ref[...] * t

    return pl.pallas_call(
        body, name="out_proj", grid=(s // ts,),
        in_specs=[pl.BlockSpec((ts, D_MODEL), lambda i: (i, 0)),
                  pl.BlockSpec((1, D_MODEL), lambda i: (0, 0)),
                  pl.BlockSpec((ts, 512), lambda i: (i, 0)),
                  pl.BlockSpec((ts, 512), lambda i: (i, 0)),
                  _w_out_spec(w_out_g, layer)],
        out_specs=pl.BlockSpec((ts, D_MODEL), lambda i: (i, 0)),
        out_shape=jax.ShapeDtypeStruct((s, D_MODEL), F32),
        compiler_params=_params(("arbitrary",)),
    )(x, gate, y_r, y_s, w_out_g)


def _final_loss(x, fg, target):
    s = x.shape[0]
    ts = _row_tile(s)

    def body(x_ref, fg_ref, t_ref, dx_ref, loss_ref, dfg_ref):
        i = pl.program_id(0)

        @pl.when(i == 0)
        def _():
            loss_ref[...] = jnp.zeros_like(loss_ref)
            dfg_ref[...] = jnp.zeros_like(dfg_ref)

        xv = x_ref[...]
        fgv = fg_ref[...]
        rstd = lax.rsqrt(jnp.mean(xv * xv, axis=-1, keepdims=True) + EPS)
        xn = xv * rstd
        err = xn * fgv - t_ref[...]
        tok = jnp.mean(err * err, axis=-1, keepdims=True)
        loss_ref[...] += 0.5 * jnp.sum(tok, axis=0, keepdims=True)
        dy = err * (1.0 / D_MODEL)
        dfg_ref[...] += jnp.sum(dy * xn, axis=0, keepdims=True)
        dxn = dy * fgv
        dx_ref[...] = rstd * (dxn - xn * jnp.mean(dxn * xn, axis=-1, keepdims=True))

    return pl.pallas_call(
        body, name="final_loss", grid=(s // ts,),
        in_specs=[pl.BlockSpec((ts, D_MODEL), lambda i: (i, 0)),
                  pl.BlockSpec((1, D_MODEL), lambda i: (0, 0)),
                  pl.BlockSpec((ts, D_MODEL), lambda i: (i, 0))],
        out_specs=[pl.BlockSpec((ts, D_MODEL), lambda i: (i, 0)),
                   pl.BlockSpec((1, 1), lambda i: (0, 0)),
                   pl.BlockSpec((1, D_MODEL), lambda i: (0, 0))],
        out_shape=[jax.ShapeDtypeStruct((s, D_MODEL), F32),
                   jax.ShapeDtypeStruct((1, 1), F32),
                   jax.ShapeDtypeStruct((1, D_MODEL), F32)],
        compiler_params=_params(("arbitrary",)),
    )(x, fg, target)


def _out_proj_bwd(dx_out, gate, y_r, y_s, w_out_g, layer):
    s = dx_out.shape[0]
    ts = _row_tile(s)

    def body(dx_ref, gate_ref, yr_ref, ys_ref, w_ref, dy_ref, dw_ref, dgate_ref):
        i = pl.program_id(0)

        @pl.when(i == 0)
        def _():
            dw_ref[...] = jnp.zeros_like(dw_ref)
            dgate_ref[...] = jnp.zeros_like(dgate_ref)

        dxv = dx_ref[...]
        dt = _mx(dxv * gate_ref[...])
        yr = yr_ref[...]
        ys = ys_ref[...]
        w_r, w_s = _w_out_halves(w_ref)
        dy_ref[:, 0:512] = _dot_nt(dt, w_r)
        dy_ref[:, 512:1024] = _dot_nt(dt, w_s)
        dw_ref[0:512, :] += _dot_tn(yr, dt)
        dw_ref[512:1024, :] += _dot_tn(ys, dt)
        t = _dot(yr, w_r) + _dot(ys, w_s)
        dgate_ref[...] += jnp.sum(dxv * t, axis=0, keepdims=True)

    return pl.pallas_call(
        body, name="out_proj_bwd", grid=(s // ts,),
        in_specs=[pl.BlockSpec((ts, D_MODEL), lambda i: (i, 0)),
                  pl.BlockSpec((1, D_MODEL), lambda i: (0, 0)),
                  pl.BlockSpec((ts, 512), lambda i: (i, 0)),
                  pl.BlockSpec((ts, 512), lambda i: (i, 0)),
                  _w_out_spec(w_out_g, layer)],
        out_specs=[pl.BlockSpec((ts, D_MODEL), lambda i: (i, 0)),
                   pl.BlockSpec((D_MODEL, D_MODEL), lambda i: (0, 0)),
                   pl.BlockSpec((1, D_MODEL), lambda i: (0, 0))],
        out_shape=[jax.ShapeDtypeStruct((s, D_MODEL), F32),
                   jax.ShapeDtypeStruct((D_MODEL, D_MODEL), F32),
                   jax.ShapeDtypeStruct((1, D_MODEL), F32)],
        compiler_params=_params(("arbitrary",)),
    )(dx_out, gate, y_r, y_s, w_out_g)


def _in_proj_bwd_x(x, dx_out, dproj, shift, scale1p, g, w_in_g, layer):
    s = x.shape[0]
    ts = _row_tile(s)
    nb = D_IN // N_DEV

    def body(x_ref, dxo_ref, dr_ref, d4_ref, d5_ref, d6_ref, d7_ref, sc_ref, g_ref, w_ref,
             dx_ref, dsh_ref, dsc_ref, dg_ref):
        i = pl.program_id(0)

        @pl.when(i == 0)
        def _():
            dsh_ref[...] = jnp.zeros_like(dsh_ref)
            dsc_ref[...] = jnp.zeros_like(dsc_ref)
            dg_ref[...] = jnp.zeros_like(dg_ref)

        dh = _dot_nt(dr_ref[:, 0:nb], w_ref[0])
        for n in range(1, 4):
            dh += _dot_nt(dr_ref[:, n * nb:(n + 1) * nb], w_ref[n])
        for n, d_ref in zip(range(4, N_DEV), (d4_ref, d5_ref, d6_ref, d7_ref)):
            dh += _dot_nt(d_ref[...], w_ref[n])
        xv = x_ref[...]
        gv = g_ref[...]
        scv = sc_ref[...]
        rstd = lax.rsqrt(jnp.mean(xv * xv, axis=-1, keepdims=True) + EPS)
        xn = xv * rstd
        dsh_ref[...] += jnp.sum(dh, axis=0, keepdims=True)
        dsc_ref[...] += jnp.sum(dh * (xn * gv), axis=0, keepdims=True)
        dhs = dh * scv
        dg_ref[...] += jnp.sum(dhs * xn, axis=0, keepdims=True)
        dxn = dhs * gv
        dx_ref[...] = rstd * (dxn - xn * jnp.mean(dxn * xn, axis=-1, keepdims=True)) + dxo_ref[...]

    del shift
    vec = pl.BlockSpec((1, D_MODEL), lambda i: (0, 0))
    return pl.pallas_call(
        body, name="in_proj_bwd_x", grid=(s // ts,),
        in_specs=[pl.BlockSpec((ts, D_MODEL), lambda i: (i, 0)),
                  pl.BlockSpec((ts, D_MODEL), lambda i: (i, 0)),
                  pl.BlockSpec((ts, 4 * nb), lambda i: (i, 0))]
                 + [pl.BlockSpec((ts, nb), lambda i: (i, 0))] * 4
                 + [vec, vec, _w_in_spec(w_in_g, layer)],
        out_specs=[pl.BlockSpec((ts, D_MODEL), lambda i: (i, 0)), vec, vec, vec],
        out_shape=[jax.ShapeDtypeStruct((s, D_MODEL), F32),
                   jax.ShapeDtypeStruct((1, D_MODEL), F32),
                   jax.ShapeDtypeStruct((1, D_MODEL), F32),
                   jax.ShapeDtypeStruct((1, D_MODEL), F32)],
        compiler_params=_params(("arbitrary",)),
    )(x, dx_out, *dproj, scale1p, g, w_in_g)


def _in_proj_bwd_w(me, x, dproj, shift, scale1p, g, dwo_parts):
    s = x.shape[0]
    ts = min(2 * _row_tile(s), s)
    ns = s // ts
    nb = D_IN // N_DEV
    last = N_DEV - 1

    def body(me_ref, x_ref, dr_ref, d4_ref, d5_ref, d6_ref, d7_ref, sh_ref, sc_ref, g_ref, dwo_ref,
             rin_ref, rout_ref, acc, stage, h_t, in_send, in_recv, out_send, out_recv, local_sems):
        t = pl.program_id(0)
        slab = (me_ref[0] + 1 + t) % N_DEV
        i = pl.program_id(1)
        px, py, pc = _my_place()
        mine = _linear(px, py, pc)

        def out_copy(r):
            peer = (1 - px if r & 4 else px, 1 - py if r & 2 else py, 1 - pc if r & 1 else pc)
            return pltpu.make_async_remote_copy(
                src_ref=dwo_ref.at[_linear(*peer)], dst_ref=rout_ref.at[mine],
                send_sem=out_send.at[r - 1], recv_sem=out_recv.at[r - 1],
                device_id=peer, device_id_type=MESH_IDS)

        def in_copy(step):
            dest = (mine + 1 + step) % N_DEV
            return pltpu.make_async_remote_copy(
                src_ref=stage.at[step % 2], dst_ref=rin_ref.at[mine],
                send_sem=in_send.at[step], recv_sem=in_recv.at[step],
                device_id=(dest // 4, (dest // 2) % 2, dest % 2), device_id_type=MESH_IDS)

        own_out = pltpu.make_async_copy(dwo_ref.at[mine], rout_ref.at[mine], local_sems.at[0])
        own_in = pltpu.make_async_copy(stage.at[last % 2], rin_ref.at[mine], local_sems.at[1])

        @pl.when(jnp.logical_and(t == 0, i == 0))
        def _():
            own_out.start()
            for r in range(1, N_DEV):
                out_copy(r).start()

        @pl.when(i == 0)
        def _():
            acc[...] = jnp.zeros_like(acc)

        @pl.when(t == 0)
        def _():
            xv = x_ref[...]
            rstd = lax.rsqrt(jnp.mean(xv * xv, axis=-1, keepdims=True) + EPS)
            h = (xv * rstd * g_ref[...]) * sc_ref[...] + sh_ref[...]
            h_t[i] = _mx(h.T)

        @pl.when(slab < 4)
        def _():
            acc[...] += _dot(h_t[i], dr_ref[...])

        for n, d_ref in zip(range(4, N_DEV), (d4_ref, d5_ref, d6_ref, d7_ref)):
            @pl.when(slab == n)
            def _():
                acc[...] += _dot(h_t[i], d_ref[...])

        @pl.when(i == ns - 1)
        def _():
            @pl.when(t >= 2)
            def _():
                in_copy(t - 2).wait_send()

            stage[t % 2] = acc[...].astype(stage.dtype)

            @pl.when(t < last)
            def _():
                in_copy(t).start()

            @pl.when(t == last)
            def _():
                own_in.start()
                in_copy(last - 1).wait_send()
                for step in range(last):
                    in_copy(step).wait_recv()
                for r in range(1, N_DEV):
                    out_copy(r).wait_recv()
                    out_copy(r).wait_send()
                own_out.wait()
                own_in.wait()

    def slab_of(t, me_ref):
        return (me_ref[0] + 1 + t) % N_DEV

    def part_rows(n, t, i, me_ref):
        return jnp.where(slab_of(t, me_ref) == n, i, ns - 1), 0

    vec = pl.BlockSpec((1, D_MODEL), lambda t, i, me_ref: (0, 0))
    rows = dwo_parts.shape[1]
    return pl.pallas_call(
        body, name="in_proj_bwd_w",
        grid_spec=pltpu.PrefetchScalarGridSpec(
            num_scalar_prefetch=1, grid=(N_DEV, ns),
            in_specs=[pl.BlockSpec((ts, D_MODEL), lambda t, i, me_ref: (jnp.where(t == 0, i, ns - 1), 0)),
                      pl.BlockSpec((ts, nb), lambda t, i, me_ref: (
                          jnp.where(slab_of(t, me_ref) < 4, i, ns - 1), jnp.minimum(slab_of(t, me_ref), 3)))]
                     + [pl.BlockSpec((ts, nb), functools.partial(part_rows, n)) for n in range(4, N_DEV)]
                     + [vec, vec, vec, HBM_SPEC],
            out_specs=[HBM_SPEC, HBM_SPEC],
            scratch_shapes=[pltpu.VMEM((D_MODEL, nb), F32),
                            pltpu.VMEM((2, D_MODEL, nb), MXU_DTYPE),
                            pltpu.VMEM((ns, D_MODEL, ts), MXU_DTYPE),
                            pltpu.SemaphoreType.DMA((N_PEERS,)), pltpu.SemaphoreType.DMA((N_PEERS,)),
                            pltpu.SemaphoreType.DMA((N_PEERS,)), pltpu.SemaphoreType.DMA((N_PEERS,)),
                            pltpu.SemaphoreType.DMA((2,))]),
        out_shape=[jax.ShapeDtypeStruct((N_DEV, D_MODEL, nb), MXU_DTYPE),
                   jax.ShapeDtypeStruct((N_DEV, rows, D_MODEL), MXU_DTYPE)],
        compiler_params=_params(("arbitrary", "arbitrary")),
    )(jnp.reshape(me, (1,)).astype(jnp.int32), x, *dproj, shift, scale1p, g, dwo_parts)


RET_TILE = 256


def _ret_tables(s):
    t = min(RET_TILE, s)
    half = RET_HEAD_DIM // 2
    pos = jnp.arange(s, dtype=F32)
    inv = ROPE_BASE ** (-jnp.arange(half, dtype=F32) / half)
    ang = pos[:, None] * inv[None, :]
    cos, sin = jnp.cos(ang), jnp.sin(ang)
    cos2 = jnp.concatenate([cos, cos], axis=1)
    sin2 = jnp.concatenate([-sin, sin], axis=1)
    lg = jnp.log1p(-(2.0 ** (-5.0 - jnp.arange(RET_HEADS, dtype=F32))))[:, None, None]
    n = jnp.arange(t)
    dist = (n[:, None] - n[None, :]).astype(F32)[None]
    cn = (n // CHUNK)[:, None]
    cm = (n // CHUNK)[None, :]
    mask = jnp.where((cn == cm)[None], jnp.exp(jnp.abs(dist) * lg),
                     jnp.where((cm < cn)[None], jnp.exp(dist * lg), 0.0))
    nf = n.astype(F32)[None, :, None]
    dq = jnp.broadcast_to(jnp.exp((nf + 1.0) * lg), (RET_HEADS, t, LANES))
    dk = jnp.broadcast_to(jnp.exp((t - 1.0 - nf) * lg), (RET_HEADS, t, LANES))
    gt = jnp.broadcast_to(jnp.exp(float(t) * lg), (RET_HEADS, 1, LANES))
    return cos2, sin2, mask, dq, dk, gt


def _roll_half(x):
    return pltpu.roll(x, RET_HEAD_DIM // 2, 1)


def _ret_heads_fwd(ret_ref, cos, sin, m_ref, dq_ref, dk_ref, s0):
    hd = RET_HEAD_DIM
    heads = range(RET_HEADS)
    qb, kb, vb, kdb = [], [], [], []
    for h in heads:
        q = ret_ref[:, h * hd:(h + 1) * hd]
        k = ret_ref[:, 512 + h * hd:512 + (h + 1) * hd]
        kr = (k * cos + _roll_half(k) * sin) * RET_KSCALE
        qb.append(_mx(q * cos + _roll_half(q) * sin))
        kb.append(_mx(kr))
        kdb.append(_mx(kr * dk_ref[h]))
        vb.append(_mx(ret_ref[:, 1024 + h * hd:1024 + (h + 1) * hd]))
    p = [_dot_nt(qb[h], kb[h]) for h in heads]
    cross = [_dot(qb[h], _mx(s0[h])) for h in heads]
    pb = [_mx(p[h] * m_ref[h]) for h in heads]
    o = [_dot(pb[h], vb[h]) + cross[h] * dq_ref[h] for h in heads]
    gn, rstd = [], []
    for h in heads:
        oc = o[h] - jnp.mean(o[h], axis=-1, keepdims=True)
        rstd.append(lax.rsqrt(jnp.mean(oc * oc, axis=-1, keepdims=True) + EPS))
        gn.append(oc * rstd[h])
    return qb, kb, vb, pb, kdb, gn, rstd


def _retention_fwd(ret, tables):
    cos2, sin2, mask, dq, dk, gt = tables
    s = ret.shape[0]
    t = mask.shape[1]
    nt = s // t
    hd = RET_HEAD_DIM

    def body(ret_ref, cos_ref, sin_ref, m_ref, dq_ref, dk_ref, gt_ref, y_ref, st_ref, s_scr):
        i = pl.program_id(0)

        @pl.when(i == 0)
        def _():
            s_scr[...] = jnp.zeros_like(s_scr)

        s0 = [s_scr[h] for h in range(RET_HEADS)]
        _, _, vb, _, kdb, gn, _ = _ret_heads_fwd(ret_ref, cos_ref[...], sin_ref[...], m_ref, dq_ref, dk_ref, s0)
        kv = [_dot_tn(kdb[h], vb[h]) for h in range(RET_HEADS)]
        for h in range(RET_HEADS):
            g = ret_ref[:, 1536 + h * hd:1536 + (h + 1) * hd]
            st_ref[h] = s0[h]
            y_ref[:, h * hd:(h + 1) * hd] = (gn[h] * (g * _sigmoid(g))).astype(y_ref.dtype)
            s_scr[h] = s0[h] * gt_ref[h] + kv[h]

    full3 = lambda a: pl.BlockSpec(a.shape, lambda i: (0, 0, 0))
    return pl.pallas_call(
        body, name="retention_fwd", grid=(nt,),
        in_specs=[pl.BlockSpec((t, 2048), lambda i: (i, 0)),
                  pl.BlockSpec((t, LANES), lambda i: (i, 0)),
                  pl.BlockSpec((t, LANES), lambda i: (i, 0)),
                  full3(mask), full3(dq), full3(dk), full3(gt)],
        out_specs=[pl.BlockSpec((t, 512), lambda i: (i, 0)),
                   pl.BlockSpec((None, RET_HEADS, hd, hd), lambda i: (i, 0, 0, 0))],
        out_shape=[jax.ShapeDtypeStruct((s, 512), MXU_DTYPE),
                   jax.ShapeDtypeStruct((nt, RET_HEADS, hd, hd), F32)],
        scratch_shapes=[pltpu.VMEM((RET_HEADS, hd, hd), F32)],
        compiler_params=_params(("arbitrary",)),
    )(ret, cos2, sin2, mask, dq, dk, gt)


def _retention_bwd(ret, states, dy, tables):
    cos2, sin2, mask, dq, dk, gt = tables
    s = ret.shape[0]
    t = mask.shape[1]
    nt = s // t
    hd = RET_HEAD_DIM

    def body(ret_ref, st_ref, dy_ref, cos_ref, sin_ref, m_ref, dq_ref, dk_ref, gt_ref, d_ref, ds_scr):
        i = pl.program_id(0)

        @pl.when(i == 0)
        def _():
            ds_scr[...] = jnp.zeros_like(ds_scr)

        cos = cos_ref[...]
        sin = sin_ref[...]
        heads = range(RET_HEADS)
        s0 = [st_ref[h] for h in heads]
        ds = [ds_scr[h] for h in heads]
        dsb = [_mx(ds[h]) for h in heads]
        qb, kb, vb, pb, kdb, gn, rstd = _ret_heads_fwd(ret_ref, cos, sin, m_ref, dq_ref, dk_ref, s0)
        dob, dodb = [], []
        for h in heads:
            g = ret_ref[:, 1536 + h * hd:1536 + (h + 1) * hd]
            dyv = dy_ref[:, h * hd:(h + 1) * hd]
            sg = _sigmoid(g)
            d_ref[:, 1536 + h * hd:1536 + (h + 1) * hd] = (
                dyv * gn[h] * (sg * (1.0 + g * (1.0 - sg)))).astype(d_ref.dtype)
            dgn = dyv * (g * sg)
            do = rstd[h] * (dgn - jnp.mean(dgn, axis=-1, keepdims=True)
                            - gn[h] * jnp.mean(dgn * gn[h], axis=-1, keepdims=True))
            dob.append(_mx(do))
            dodb.append(_mx(do * dq_ref[h]))
        dp = [_dot_nt(dob[h], vb[h]) for h in heads]
        dv = [_dot_tn(pb[h], dob[h]) + _dot(kdb[h], dsb[h]) for h in heads]
        dq_cross = [_dot_nt(dodb[h], _mx(s0[h])) for h in heads]
        dk_cross = [_dot_nt(vb[h], dsb[h]) for h in heads]
        ds_new = [_dot_tn(qb[h], dodb[h]) for h in heads]
        dpb = [_mx(dp[h] * m_ref[h]) for h in heads]
        dqr = [_dot(dpb[h], kb[h]) + dq_cross[h] for h in heads]
        dkr = [(_dot_tn(dpb[h], qb[h]) + dk_cross[h] * dk_ref[h]) * RET_KSCALE for h in heads]
        for h in heads:
            d_ref[:, 1024 + h * hd:1024 + (h + 1) * hd] = dv[h].astype(d_ref.dtype)
            d_ref[:, h * hd:(h + 1) * hd] = (dqr[h] * cos + _roll_half(dqr[h] * sin)).astype(d_ref.dtype)
            d_ref[:, 512 + h * hd:512 + (h + 1) * hd] = (
                dkr[h] * cos + _roll_half(dkr[h] * sin)).astype(d_ref.dtype)
            ds_scr[h] = ds[h] * gt_ref[h] + ds_new[h]

    full3 = lambda a: pl.BlockSpec(a.shape, lambda i: (0, 0, 0))
    rev = lambda i: (nt - 1 - i, 0)
    return pl.pallas_call(
        body, name="retention_bwd", grid=(nt,),
        in_specs=[pl.BlockSpec((t, 2048), rev),
                  pl.BlockSpec((None, RET_HEADS, hd, hd), lambda i: (nt - 1 - i, 0, 0, 0)),
                  pl.BlockSpec((t, 512), rev),
                  pl.BlockSpec((t, LANES), rev),
                  pl.BlockSpec((t, LANES), rev),
                  full3(mask), full3(dq), full3(dk), full3(gt)],
        out_specs=pl.BlockSpec((t, 2048), rev),
        out_shape=jax.ShapeDtypeStruct((s, 2048), MXU_DTYPE),
        scratch_shapes=[pltpu.VMEM((RET_HEADS, hd, hd), F32)],
        compiler_params=_params(("arbitrary",)),
    )(ret, states, dy, cos2, sin2, mask, dq, dk, gt)


SB_BLOCK = 256
SB_CHUNK = 32


SB_SKIP = 104.0


def _split_dots(xs, u):
    parts = []
    for x in xs:
        hi = lax.bitcast_convert_type(lax.bitcast_convert_type(x, jnp.uint32) & jnp.uint32(0xFFFF0000), F32)
        parts += [_mx(hi), _mx(x - hi)]
    out = _dot(jnp.concatenate(parts, axis=0), u)
    n = xs[0].shape[0]
    return [out[2 * k * n:(2 * k + 1) * n] + out[(2 * k + 1) * n:(2 * k + 2) * n] for k in range(len(xs))]


def _split_dot(x, u):
    return _split_dots([x], u)[0]


def _sb_pair_weights(lb, lk, allowed, u_gt):
    blk = lb.shape[0]
    lk_p, lk_d = lk[:, :blk], lk[:, blk:]
    r_d = _rowsum(lk_d)
    cs_p, cs_d = _split_dots([lk_p, lk_d], u_gt)
    a = jnp.exp(lb + jnp.concatenate([cs_p + r_d, cs_d], axis=1))
    return jnp.where(allowed, a, 0.0), r_d, r_d + _rowsum(lk_p)


def _sb_logits(q, k, causal):
    z = _dot_nt(q, k)
    l1p = jnp.log(1.0 + jnp.exp(-jnp.abs(z)))
    lk = -(jnp.maximum(z, 0.0) + l1p)
    if causal is not None:
        lk = jnp.where(causal, lk, 0.0)
    return jnp.minimum(z, 0.0) - l1p, lk


def _sb_weights(lb, lk, r, u_gt, causal):
    a = jnp.exp(lb + _split_dot(lk, u_gt) + r)
    return a if causal is None else jnp.where(causal, a, 0.0)


def _rowsum(x):
    return jnp.sum(x, axis=1, keepdims=True)


def _sb_pair_tile(i, blk):
    row = lax.broadcasted_iota(jnp.int32, (blk, 2 * blk), 0)
    col = lax.broadcasted_iota(jnp.int32, (blk, 2 * blk), 1)
    first_col = jnp.where(i >= 1, 0, blk)
    allowed = jnp.logical_and(row > col - blk, col >= first_col)
    rows_p = pl.ds(pl.multiple_of(jnp.maximum(i - 1, 0) * blk, blk), blk)
    rows_d = pl.ds(pl.multiple_of(i * blk, blk), blk)
    return allowed, rows_p, rows_d


def _sb_fwd(qkv, sg):
    s = qkv.shape[0]
    blk = min(SB_BLOCK, s)
    nq = s // blk
    hd = SB_HEAD_DIM

    ch = min(SB_CHUNK, blk)

    def body(q_ref, k_ref, v_ref, g_ref, y_ref, o_ref, z_scr, lhs_scr, cs_scr, a_scr):
        i = pl.program_id(1)
        row = lax.broadcasted_iota(jnp.int32, (blk, blk), 0)
        col = lax.broadcasted_iota(jnp.int32, (blk, blk), 1)
        u_gt = (row > col).astype(MXU_DTYPE)
        heads = [slice(hh * hd, (hh + 1) * hd) for hh in range(2)]
        qs = [q_ref[:, ls] for ls in heads]
        _, rows_p, rows_d = _sb_pair_tile(i, blk)
        has_prev = i >= 1
        crow = lax.broadcasted_iota(jnp.int32, (ch, blk), 0)
        ccol = lax.broadcasted_iota(jnp.int32, (ch, blk), 1)

        def logits(hh):
            kc = jnp.concatenate([k_ref[rows_p, heads[hh]], k_ref[rows_d, heads[hh]]], axis=0)
            z_scr[hh] = _dot_nt(qs[hh], kc)

        def keep_parts(hh):
            r_d, r_all = [], []
            for c in range(blk // ch):
                rows = pl.ds(c * ch, ch)
                causal = crow + c * ch > ccol
                z = z_scr[hh, rows, :]
                l1p = jnp.log(1.0 + jnp.exp(-jnp.abs(z)))
                lk = -(jnp.maximum(z, 0.0) + l1p)
                z_scr[hh, rows, :] = jnp.minimum(z, 0.0) - l1p
                lk_p = jnp.where(has_prev, lk[:, :blk], 0.0)
                lk_d = jnp.where(causal, lk[:, blk:], 0.0)
                lhs_scr[hh, pl.ds(c * ch, ch), :] = _mx(lk_p)
                lhs_scr[hh, pl.ds(blk + c * ch, ch), :] = _mx(lk_d)
                r_d.append(_rowsum(lk_d))
                r_all.append(r_d[c] + _rowsum(lk_p))
            return r_d, jnp.concatenate(r_all, axis=0)

        def suffix_sums(hh):
            cs_scr[hh] = _dot(lhs_scr[hh], u_gt)

        def weights(hh, r_d):
            for c in range(blk // ch):
                rows = pl.ds(c * ch, ch)
                causal = crow + c * ch > ccol
                cs_p = cs_scr[hh, pl.ds(c * ch, ch), :] + r_d[c]
                cs_d = cs_scr[hh, pl.ds(blk + c * ch, ch), :]
                lb = z_scr[hh, rows, :]
                a_p = jnp.where(has_prev, jnp.exp(lb[:, :blk] + cs_p), 0.0)
                a_d = jnp.where(causal, jnp.exp(lb[:, blk:] + cs_d), 0.0)
                a_scr[hh, rows, :] = _mx(jnp.concatenate([a_p, a_d], axis=1))

        def values(hh):
            vc = jnp.concatenate([v_ref[rows_p, heads[hh]], v_ref[rows_d, heads[hh]]], axis=0)
            return _dot(a_scr[hh], vc)

        def block(hh, j, r):
            start = pl.multiple_of(j * blk, blk)
            lb, lk = _sb_logits(qs[hh], k_ref[pl.ds(start, blk), heads[hh]], None)
            a = _sb_weights(lb, lk, r, u_gt, None)
            return _dot(_mx(a), v_ref[pl.ds(start, blk), heads[hh]]), r + _rowsum(lk)

        def more(n, r0, r1):
            return jnp.logical_and(n < i, jnp.max(jnp.maximum(r0, r1)) > -SB_SKIP)

        logits(0)
        logits(1)
        rd0, r0 = keep_parts(0)
        suffix_sums(0)
        rd1, r1 = keep_parts(1)
        suffix_sums(1)
        go = more(jnp.int32(1), r0, r1)
        gates = [g_ref[:, ls] * _sigmoid(g_ref[:, ls]) for ls in heads]
        weights(0, rd0)
        acc0 = values(0)
        weights(1, rd1)
        acc1 = values(1)

        def step(c):
            _, n, acc0, r0, acc1, r1 = c
            pv0, r0 = block(0, i - 1 - n, r0)
            pv1, r1 = block(1, i - 1 - n, r1)
            return more(n + 1, r0, r1), n + 1, acc0 + pv0, r0, acc1 + pv1, r1

        _, _, acc0, _, acc1, _ = lax.while_loop(lambda c: c[0], step, (go, jnp.int32(1), acc0, r0, acc1, r1))
        for ls, acc, gate in zip(heads, (acc0, acc1), gates):
            o_ref[:, ls] = acc
            y_ref[:, ls] = (acc * gate).astype(y_ref.dtype)

    qblk = pl.BlockSpec((blk, LANES), lambda p, i: (i, p))
    return pl.pallas_call(
        body, name="stickbreak_fwd", grid=(SB_HEADS // 2, nq),
        in_specs=[qblk,
                  pl.BlockSpec((s, LANES), lambda p, i: (0, 4 + p)),
                  pl.BlockSpec((s, LANES), lambda p, i: (0, 8 + p)),
                  qblk],
        out_specs=[qblk, qblk],
        out_shape=[jax.ShapeDtypeStruct((s, 512), MXU_DTYPE),
                   jax.ShapeDtypeStruct((s, 512), F32)],
        scratch_shapes=[pltpu.VMEM((2, blk, 2 * blk), F32),
                        pltpu.VMEM((2, 2 * blk, blk), MXU_DTYPE),
                        pltpu.VMEM((2, 2 * blk, blk), F32),
                        pltpu.VMEM((2, blk, 2 * blk), MXU_DTYPE)],
        compiler_params=_params(("arbitrary", "arbitrary")),
    )(qkv, qkv, qkv, sg)


def _sb_bwd(qkv, sg, o, dy):
    s = qkv.shape[0]
    blk = min(SB_BLOCK, s)
    nq = s // blk
    hd = SB_HEAD_DIM
    assert nq <= LANES
    ch = min(SB_CHUNK, blk)

    def body(q_ref, k_ref, v_ref, g_ref, o_ref, dy_ref, dq_ref, dk_ref, dv_ref, dg_ref, dk_scr, dv_scr,
             z_scr, g_scr, lhs_scr, cs_scr, a_scr, dz_scr):
        i = pl.program_id(1)

        @pl.when(i == 0)
        def _():
            dk_scr[...] = jnp.zeros_like(dk_scr)
            dv_scr[...] = jnp.zeros_like(dv_scr)

        row = lax.broadcasted_iota(jnp.int32, (blk, blk), 0)
        col = lax.broadcasted_iota(jnp.int32, (blk, blk), 1)
        lane = lax.broadcasted_iota(jnp.int32, (blk, LANES), 1)
        u_gt = (row > col).astype(MXU_DTYPE)
        u_lt = (row < col).astype(MXU_DTYPE)
        heads = [slice(hh * hd, (hh + 1) * hd) for hh in range(2)]
        qs = [q_ref[:, ls] for ls in heads]
        _, rows_p, rows_d = _sb_pair_tile(i, blk)
        has_prev = i >= 1
        crow = lax.broadcasted_iota(jnp.int32, (ch, blk), 0)
        ccol = lax.broadcasted_iota(jnp.int32, (ch, blk), 1)
        nch = blk // ch
        dobs, kcs = [], []
        for hh, ls in enumerate(heads):
            g = g_ref[:, ls]
            dyv = dy_ref[:, ls]
            sgm = _sigmoid(g)
            dg_ref[:, ls] = (dyv * o_ref[:, ls] * (sgm * (1.0 + g * (1.0 - sgm)))).astype(dg_ref.dtype)
            dobs.append(_mx(dyv * (g * sgm)))
            kcs.append(jnp.concatenate([k_ref[rows_p, ls], k_ref[rows_d, ls]], axis=0))

        def split_rows(hh, c, part, x):
            lhs_scr[hh, pl.ds(part * blk + c * ch, ch), :] = _mx(x)

        def summed_rows(hh, c, part):
            return cs_scr[hh, pl.ds(part * blk + c * ch, ch), :]

        def logits(hh):
            z_scr[hh] = _dot_nt(qs[hh], kcs[hh])
            vc = jnp.concatenate([v_ref[rows_p, heads[hh]], v_ref[rows_d, heads[hh]]], axis=0)
            g_scr[hh] = _dot_nt(dobs[hh], vc)

        def keep_parts(hh):
            r_d, r_all = [], []
            for c in range(nch):
                rows = pl.ds(c * ch, ch)
                z = z_scr[hh, rows, :]
                l1p = jnp.log(1.0 + jnp.exp(-jnp.abs(z)))
                lk = -(jnp.maximum(z, 0.0) + l1p)
                z_scr[hh, rows, :] = jnp.minimum(z, 0.0) - l1p
                lk_p = jnp.where(has_prev, lk[:, :blk], 0.0)
                lk_d = jnp.where(crow + c * ch > ccol, lk[:, blk:], 0.0)
                split_rows(hh, c, 0, lk_p)
                split_rows(hh, c, 1, lk_d)
                r_d.append(_rowsum(lk_d))
                r_all.append(r_d[c] + _rowsum(lk_p))
            return r_d, jnp.concatenate(r_all, axis=0)

        def weights(hh, r_d):
            g_p = []
            for c in range(nch):
                rows = pl.ds(c * ch, ch)
                lb = z_scr[hh, rows, :]
                a_p = jnp.where(has_prev, jnp.exp(lb[:, :blk] + (summed_rows(hh, c, 0) + r_d[c])), 0.0)
                a_d = jnp.where(crow + c * ch > ccol, jnp.exp(lb[:, blk:] + summed_rows(hh, c, 1)), 0.0)
                a = jnp.concatenate([a_p, a_d], axis=1)
                a_scr[hh, rows, :] = _mx(a)
                gm = g_scr[hh, rows, :] * a
                g_scr[hh, rows, :] = gm
                split_rows(hh, c, 0, gm[:, :blk])
                split_rows(hh, c, 1, gm[:, blk:])
                g_p.append(_rowsum(gm[:, :blk]))
            return g_p

        def logit_grads(hh, pg, g_p):
            for c in range(nch):
                rows = pl.ds(c * ch, ch)
                pre = jnp.concatenate([summed_rows(hh, c, 0) + pg[c * ch:(c + 1) * ch],
                                       summed_rows(hh, c, 1) + (pg[c * ch:(c + 1) * ch] + g_p[c])], axis=1)
                gm = g_scr[hh, rows, :]
                dz = gm - (gm + pre) * jnp.exp(z_scr[hh, rows, :])
                dz_p = jnp.where(has_prev, dz[:, :blk], 0.0)
                dz_d = jnp.where(crow + c * ch > ccol, dz[:, blk:], 0.0)
                dz_scr[hh, rows, :] = _mx(jnp.concatenate([dz_p, dz_d], axis=1))

        def products(hh, acc):
            ls = heads[hh]
            dk_scr[hh, rows_p, :] += _dot_tn(dz_scr[hh, :, 0:blk], qs[hh])
            dk_scr[hh, rows_d, :] += _dot_tn(dz_scr[hh, :, blk:2 * blk], qs[hh])
            dv_scr[hh, rows_p, :] += _dot_tn(a_scr[hh, :, 0:blk], dobs[hh])
            dv_scr[hh, rows_d, :] += _dot_tn(a_scr[hh, :, blk:2 * blk], dobs[hh])
            dq_ref[:, ls] = ((acc + _dot(dz_scr[hh], kcs[hh])) * SB_SCALE).astype(dq_ref.dtype)

        def suffix_sums(hh):
            cs_scr[hh] = _dot(lhs_scr[hh], u_gt)

        def prefix_sums(hh):
            cs_scr[hh] = _dot(lhs_scr[hh], u_lt)

        def more(n, r0, r1):
            return jnp.logical_and(n < i, jnp.max(jnp.maximum(r0, r1)) > -SB_SKIP)

        logits(0)
        logits(1)
        rd0, ra0 = keep_parts(0)
        suffix_sums(0)
        rd1, ra1 = keep_parts(1)
        suffix_sums(1)
        go = more(jnp.int32(1), ra0, ra1)
        gp0 = weights(0, rd0)
        prefix_sums(0)
        gp1 = weights(1, rd1)
        prefix_sums(1)

        def scan_block(hh, j, r, rmat):
            start = pl.multiple_of(j * blk, blk)
            _, lk = _sb_logits(qs[hh], k_ref[pl.ds(start, blk), heads[hh]], None)
            return r + _rowsum(lk), jnp.where(lane == j, r, rmat)

        def scan_step(c):
            _, n, r0, rmat0, r1, rmat1 = c
            r0, rmat0 = scan_block(0, i - 1 - n, r0, rmat0)
            r1, rmat1 = scan_block(1, i - 1 - n, r1, rmat1)
            return more(n + 1, r0, r1), n + 1, r0, rmat0, r1, rmat1

        zmat = jnp.zeros((blk, LANES), F32)
        _, n, _, rmat0, _, rmat1 = lax.while_loop(lambda c: c[0], scan_step,
                                                  (go, jnp.int32(1), ra0, zmat, ra1, zmat))
        rmats = (rmat0, rmat1)

        def block(hh, j, pg):
            ls = heads[hh]
            start = pl.multiple_of(j * blk, blk)
            k = k_ref[pl.ds(start, blk), ls]
            lb, lk = _sb_logits(qs[hh], k, None)
            r = _rowsum(jnp.where(lane == j, rmats[hh], 0.0))
            a = _sb_weights(lb, lk, r, u_gt, None)
            gm = _dot_nt(dobs[hh], v_ref[pl.ds(start, blk), ls]) * a
            dzb = _mx(gm - (gm + (pg + _split_dot(gm, u_lt))) * jnp.exp(lb))
            dk_scr[hh, pl.ds(start, blk), :] += _dot_tn(dzb, qs[hh])
            dv_scr[hh, pl.ds(start, blk), :] += _dot_tn(_mx(a), dobs[hh])
            return _dot(dzb, k), pg + _rowsum(gm)

        def step(t, c):
            acc0, pg0, acc1, pg1 = c
            dq0, pg0 = block(0, i - n + t, pg0)
            dq1, pg1 = block(1, i - n + t, pg1)
            return acc0 + dq0, pg0, acc1 + dq1, pg1

        zero = jnp.zeros((blk, 1), F32)
        zacc = jnp.zeros((blk, hd), F32)
        acc0, pg0, acc1, pg1 = lax.fori_loop(0, n - 1, step, (zacc, zero, zacc, zero))
        logit_grads(0, pg0, gp0)
        logit_grads(1, pg1, gp1)
        products(0, acc0)
        products(1, acc1)

        @pl.when(i == nq - 1)
        def _():
            for hh in range(2):
                ls = slice(hh * hd, (hh + 1) * hd)
                dk_ref[:, ls] = dk_scr[hh].astype(dk_ref.dtype)
                dv_ref[:, ls] = dv_scr[hh].astype(dv_ref.dtype)

    qblk = lambda c0: pl.BlockSpec((blk, LANES), lambda p, i: (i, c0 + p))
    full = lambda c0: pl.BlockSpec((s, LANES), lambda p, i: (0, c0 + p))
    half = jax.ShapeDtypeStruct((s, 512), MXU_DTYPE)
    return pl.pallas_call(
        body, name="stickbreak_bwd", grid=(SB_HEADS // 2, nq),
        in_specs=[qblk(0), full(4), full(8), qblk(0), qblk(0), qblk(4)],
        out_specs=[qblk(0), full(0), full(0), qblk(0)],
        out_shape=[half, half, half, half],
        scratch_shapes=[pltpu.VMEM((2, s, hd), F32), pltpu.VMEM((2, s, hd), F32),
                        pltpu.VMEM((2, blk, 2 * blk), F32),
                        pltpu.VMEM((2, blk, 2 * blk), F32),
                        pltpu.VMEM((2, 2 * blk, blk), MXU_DTYPE),
                        pltpu.VMEM((2, 2 * blk, blk), F32),
                        pltpu.VMEM((2, blk, 2 * blk), MXU_DTYPE),
                        pltpu.VMEM((2, blk, 2 * blk), MXU_DTYPE)],
        compiler_params=_params(("arbitrary", "arbitrary")),
    )(qkv, qkv, qkv, sg, o, dy)


def _layer_fwd(layer, x, mod, norm_g, w_in_g, w_out_g, tables, gather=()):
    shift, scale1p, gate = mod[0:1], 1.0 + mod[1:2], mod[2:3]
    ret, qkv, sg, *gathered = _ln_proj(x, shift, scale1p, norm_g, w_in_g, layer, gather)
    y_r, states = _retention_fwd(ret, tables)
    y_s, o_s = _sb_fwd(qkv, sg)
    x_next = _out_proj(x, gate, y_r, y_s, w_out_g, layer)
    saved = (x, shift, scale1p, gate, ret, qkv, sg, y_r, states, y_s, o_s)
    return x_next, saved, gathered


def _layer_bwd(layer, me, dx_out, saved, norm_g, w_in_g, w_out_g, tables):
    x, shift, scale1p, gate, ret, qkv, sg, y_r, states, y_s, o_s = saved
    dy, dw_out, dgate = _out_proj_bwd(dx_out, gate, y_r, y_s, w_out_g, layer)
    d_ret = _retention_bwd(ret, states, dy, tables)
    d_q, d_k, d_v, d_g = _sb_bwd(qkv, sg, o_s, dy)
    dproj = (d_ret, d_q, d_k, d_v, d_g)
    dx, dshift, dscale, dnorm_g = _in_proj_bwd_x(x, dx_out, dproj, shift, scale1p, norm_g, w_in_g, layer)
    dwo_parts = _mx(dw_out.reshape(N_DEV, D_MODEL // N_DEV, D_MODEL))
    r_in, r_out = _in_proj_bwd_w(me, x, dproj, shift, scale1p, norm_g, dwo_parts)
    dmod = jnp.concatenate([dshift, dscale, dgate], axis=1)
    return dx, r_in, r_out, dmod, dnorm_g


MESH_IDS = pl.DeviceIdType.MESH
N_PEERS = N_DEV - 1
HBM_SPEC = pl.BlockSpec(memory_space=pl.ANY)


def _my_place():
    return lax.axis_index("x"), lax.axis_index("y"), lax.axis_index("c")


def _linear(px, py, pc):
    return 4 * px + 2 * py + pc


def _all_gather(blocks):
    n_arr = len(blocks)

    def body(*refs):
        start, forward, finish = _gather_plan(refs[:n_arr], refs[n_arr:2 * n_arr], *refs[2 * n_arr:])
        start()
        forward()
        finish()

    return pl.pallas_call(
        body, name="all_gather",
        out_shape=_gathered_shapes(blocks),
        in_specs=[HBM_SPEC] * n_arr, out_specs=[HBM_SPEC] * n_arr,
        scratch_shapes=_gather_sems(n_arr),
    )(*blocks)


def _gathered_shapes(blocks):
    return [jax.ShapeDtypeStruct((N_DEV * b.shape[0], b.shape[1]), b.dtype) for b in blocks]


def _gather_sems(n_arr):
    return [pltpu.SemaphoreType.DMA((n_arr * N_PEERS,)), pltpu.SemaphoreType.DMA((n_arr * N_PEERS,)),
            pltpu.SemaphoreType.DMA((n_arr,))]


def _gather_plan(x_refs, out_refs, send_sems, recv_sems, local_sems):
    n_arr = len(x_refs)
    x, y, c = _my_place()
    me, sibling = (x, y, c), (x, y, 1 - c)
    chips = [(1 - x, y), (x, 1 - y), (1 - x, 1 - y)]

    def rows(a, place):
        m = x_refs[a].shape[0]
        return out_refs[a].at[pl.ds(_linear(*place) * m, m), :]

    def copy(a, k, block, to, src=None):
        return pltpu.make_async_remote_copy(
            src_ref=rows(a, block) if src is None else src, dst_ref=rows(a, block),
            send_sem=send_sems.at[a * N_PEERS + k], recv_sem=recv_sems.at[a * N_PEERS + k],
            device_id=to, device_id_type=MESH_IDS)

    mine = [pltpu.make_async_copy(x_refs[a], rows(a, me), local_sems.at[a]) for a in range(n_arr)]
    first = []
    for a in range(n_arr):
        first.append(copy(a, 0, me, sibling, src=x_refs[a]))
        first += [copy(a, 1 + j, me, (*chip, c), src=x_refs[a]) for j, chip in enumerate(chips)]
    passed = [copy(a, 4 + j, (*chip, c), sibling) for j, chip in enumerate(chips) for a in range(n_arr)]

    def start():
        for cp in mine + first:
            cp.start()

    def forward():
        for j, chip in enumerate(chips):
            for a in range(n_arr):
                copy(a, 1 + j, (*chip, c), me).wait_recv()
                passed[j * n_arr + a].start()

    def finish():
        for a in range(n_arr):
            copy(a, 0, sibling, me).wait_recv()
            for j, chip in enumerate(chips):
                copy(a, 4 + j, (*chip, 1 - c), me).wait_recv()
        for cp in first + passed:
            cp.wait_send()
        for cp in mine:
            cp.wait()

    return start, forward, finish


def _ada_fwd(c_all, w_ada, b_cols):
    cols = w_ada.shape[2]

    def body(c_ref, w_ref, b_ref, ca_ref, mod_ref):
        cv = c_ref[...]
        ca = cv * _sigmoid(cv)
        ca_ref[...] = ca
        cb = _mx(jnp.concatenate([ca, ca], axis=0))
        for l in range(DEPTH):
            mod_ref[l * N_DEV:(l + 1) * N_DEV, :] = _dot(cb, _mx(w_ref[l]))[0:N_DEV] + b_ref[l]

    return pl.pallas_call(
        body, name="ada_fwd",
        out_shape=[jax.ShapeDtypeStruct((N_DEV, D_MODEL), F32),
                   jax.ShapeDtypeStruct((DEPTH * N_DEV, cols), F32)],
        compiler_params=_params(),
    )(c_all, w_ada, b_cols)


def _ada_bwd(c_act_t, dmod_cols):
    cols = dmod_cols.shape[2]

    def body(ca_ref, dm_ref, o_ref):
        ca = _mx(ca_ref[...]).astype(F32)
        for l in range(DEPTH):
            o_ref[l] = jnp.dot(ca, _mx(dm_ref[l]).astype(F32),
                               precision=lax.Precision.HIGHEST, preferred_element_type=F32)

    return pl.pallas_call(
        body, name="ada_bwd",
        out_shape=jax.ShapeDtypeStruct((DEPTH, D_MODEL, cols), F32),
        compiler_params=_params(),
    )(c_act_t, dmod_cols)


def _adamw_store(g, w_ref, m_ref, v_ref, g_ref, d_ref, mo_ref, vo_ref):
    m2 = ADAM_B1 * m_ref[...] + (1.0 - ADAM_B1) * g
    v2 = ADAM_B2 * v_ref[...] + (1.0 - ADAM_B2) * (g * g)
    m_hat = m2 / (1.0 - ADAM_B1 ** ADAM_STEP)
    v_hat = v2 / (1.0 - ADAM_B2 ** ADAM_STEP)
    g_ref[...] = g
    d_ref[...] = -ADAM_LR * (m_hat / (jnp.sqrt(v_hat) + ADAM_EPS) + ADAM_WD * w_ref[...])
    mo_ref[...] = m2
    vo_ref[...] = v2


def _slab_sum(p_ref):
    g = p_ref[0].astype(F32)
    for sl in range(1, p_ref.shape[0]):
        g = g + p_ref[sl].astype(F32)
    return g


def _sum_adamw_layers(parts, w, m, v):
    n_slab, rows, cols = parts[0].shape
    tr = min(256, rows)
    nt = rows // tr

    def body(p0_ref, p1_ref, w_ref, m_ref, v_ref, g_ref, d_ref, mo_ref, vo_ref):
        for l, p_ref in enumerate((p0_ref, p1_ref)):
            @pl.when(pl.program_id(0) == l)
            def _():
                _adamw_store(_slab_sum(p_ref), w_ref, m_ref, v_ref, g_ref, d_ref, mo_ref, vo_ref)

    p_specs = [pl.BlockSpec((n_slab, tr, cols), lambda l, i: (0, i * (1 - l) + (nt - 1) * l, 0)),
               pl.BlockSpec((n_slab, tr, cols), lambda l, i: (0, i * l, 0))]
    blk = pl.BlockSpec((None, tr, cols), lambda l, i: (l, i, 0))
    shp = jax.ShapeDtypeStruct((DEPTH, rows, cols), F32)
    return pl.pallas_call(
        body, name="sum_adamw_layers", grid=(DEPTH, nt),
        in_specs=p_specs + [blk, blk, blk],
        out_specs=[blk, blk, blk, blk],
        out_shape=[shp, shp, shp, shp],
        compiler_params=_params(("arbitrary", "arbitrary")),
    )(parts[0], parts[1], w, m, v)


def _sum_adamw(parts, w, m, v):
    n_slab, rows, cols = parts.shape
    tr = min(256, rows)

    def body(p_ref, w_ref, m_ref, v_ref, g_ref, d_ref, mo_ref, vo_ref):
        _adamw_store(_slab_sum(p_ref), w_ref, m_ref, v_ref, g_ref, d_ref, mo_ref, vo_ref)

    blk = pl.BlockSpec((tr, cols), lambda i: (i, 0))
    shp = jax.ShapeDtypeStruct((rows, cols), F32)
    return pl.pallas_call(
        body, name="sum_adamw", grid=(rows // tr,),
        in_specs=[pl.BlockSpec((n_slab, tr, cols), lambda i: (0, i, 0)), blk, blk, blk],
        out_specs=[blk, blk, blk, blk],
        out_shape=[shp, shp, shp, shp],
        compiler_params=_params(("arbitrary",)),
    )(parts, w, m, v)


SMALL_ROWS = 16


def kernel(x, c, norm_g, w_ada, b_ada, w_in, w_out, final_g, loss_target, m_norm_g, m_w_ada, m_b_ada, m_w_in, m_w_out, m_final_g, v_norm_g, v_w_ada, v_b_ada, v_w_in, v_w_out, v_final_g):
    me = _linear(*_my_place())
    in_cols = w_in.shape[2]
    out_rows = w_out.shape[1]
    ada_cols = w_ada.shape[2]

    w_in_m, w_out_m = _mx(w_in), _mx(w_out)
    g_in, g_out, g_c = _all_gather([w_in_m[0], w_out_m[0], jnp.broadcast_to(c, (8, D_MODEL))])
    w_in_g = [g_in.reshape(N_DEV, 1, D_MODEL, in_cols), None]
    w_out_g = [g_out.reshape(N_DEV, 1, out_rows, D_MODEL), None]
    c_all = g_c.reshape(N_DEV, 8, D_MODEL)[:, 0]

    b_cols = lax.dynamic_slice_in_dim(b_ada, me * ada_cols, ada_cols, axis=1)[:, None, :]
    c_act, mod_cols = _ada_fwd(c_all, w_ada, b_cols)
    (g_mod,) = _all_gather([mod_cols])
    g_mod = g_mod.reshape(N_DEV, DEPTH, N_DEV, ada_cols)
    mod = lax.dynamic_index_in_dim(g_mod, me, axis=2, keepdims=False)
    mod = mod.transpose(1, 0, 2).reshape(DEPTH, 3, D_MODEL)

    tables = _ret_tables(x.shape[1])
    h = x[0]
    saved = []
    for l in range(DEPTH):
        nxt = (w_in_m[l + 1], w_out_m[l + 1]) if l + 1 < DEPTH else ()
        h, sv, gathered = _layer_fwd(0, h, mod[l], norm_g[l:l + 1], w_in_g[l], w_out_g[l], tables, nxt)
        if nxt:
            w_in_g[l + 1] = gathered[0].reshape(N_DEV, 1, D_MODEL, in_cols)
            w_out_g[l + 1] = gathered[1].reshape(N_DEV, 1, out_rows, D_MODEL)
        saved.append(sv)
    dx, loss_part, dfg = _final_loss(h, final_g[None], loss_target[0])
    r_in, r_out, small = [None] * DEPTH, [None] * DEPTH, [None] * DEPTH
    for l in reversed(range(DEPTH)):
        dx, r_in[l], r_out[l], dmod, dng = _layer_bwd(0, me, dx, saved[l], norm_g[l:l + 1], w_in_g[l], w_out_g[l], tables)
        small[l] = (dmod.reshape(3, D_MODEL), dng)

    pad = jnp.zeros((SMALL_ROWS - 10, D_MODEL), F32)
    small_block = jnp.concatenate([small[0][0], small[1][0], small[0][1], small[1][1], dfg,
                                   jnp.broadcast_to(loss_part, (1, D_MODEL)), pad], axis=0)
    (g_small,) = _all_gather([small_block])
    g_small = g_small.reshape(N_DEV, SMALL_ROWS, D_MODEL)

    def small_pack(b, n, f, fill):
        return jnp.concatenate([b.reshape(6, D_MODEL), n, f[None],
                                jnp.full((SMALL_ROWS - 9, D_MODEL), fill, F32)], axis=0)

    s_g, s_d, s_m, s_v = _sum_adamw(g_small, small_pack(b_ada, norm_g, final_g, 0.0),
                                    small_pack(m_b_ada, m_norm_g, m_final_g, 0.0),
                                    small_pack(v_b_ada, v_norm_g, v_final_g, 1.0))
    loss = s_g[9, 0]

    def small_unpack(a):
        return a[0:6].reshape(DEPTH, 3 * D_MODEL), a[6:8], a[8]

    dmod_all = g_small[:, 0:6].reshape(N_DEV, DEPTH, 3 * D_MODEL).transpose(1, 0, 2)
    dmod_cols = lax.dynamic_slice_in_dim(dmod_all, me * ada_cols, ada_cols, axis=2)
    g_ada = _ada_bwd(c_act.T, dmod_cols).reshape(1, DEPTH * D_MODEL, ada_cols)
    ada = _sum_adamw(g_ada, *[a.reshape(DEPTH * D_MODEL, ada_cols) for a in (w_ada, m_w_ada, v_w_ada)])
    ada = [a.reshape(DEPTH, D_MODEL, ada_cols) for a in ada]

    win = _sum_adamw_layers(r_in, w_in, m_w_in, v_w_in)
    wout = _sum_adamw_layers(r_out, w_out, m_w_out, v_w_out)

    outs = [loss, dx[None]]
    for k in range(4):
        b, n, f = small_unpack((s_g, s_d, s_m, s_v)[k])
        outs += [n, ada[k], b, win[k], wout[k], f]
    return tuple(outs)
```

```python
import functools

import jax
import jax.numpy as jnp
from jax import lax
from jax.experimental import pallas as pl
from jax.experimental.pallas import tpu as pltpu

F32 = jnp.float32
MXU_DTYPE = jnp.bfloat16

D_MODEL = 1024
DEPTH = 2
N_DEV = 8
CHUNK = 64
D_RET = 512
D_SB = 512
RET_HEADS = 4
RET_HEAD_DIM = 128
SB_HEADS = 8
SB_HEAD_DIM = 64
D_IN = 4096
ROPE_BASE = 10000.0
EPS = 1e-6
SB_SCALE = SB_HEAD_DIM ** -0.5
RET_KSCALE = RET_HEAD_DIM ** -0.5

ADAM_LR = 0.001
ADAM_B1 = 0.9
ADAM_B2 = 0.999
ADAM_EPS = 1e-08
ADAM_WD = 0.01
ADAM_STEP = 10

V7X_VMEM_BYTES = 64 * 2 ** 20
VMEM_LIMIT = V7X_VMEM_BYTES - 8 * 2 ** 20
LANES = 128

_NT = (((1,), (1,)), ((), ()))
_TN = (((0,), (0,)), ((), ()))


def _dot(a, b):
    return jnp.dot(a, b, preferred_element_type=F32)


def _dot_nt(a, b):
    return lax.dot_general(a, b, _NT, preferred_element_type=F32)


def _dot_tn(a, b):
    return lax.dot_general(a, b, _TN, preferred_element_type=F32)


def _mx(x):
    return x.astype(MXU_DTYPE)


def _sigmoid(x):
    return 1.0 / (1.0 + jnp.exp(-x))


def _params(sem=None):
    return pltpu.CompilerParams(dimension_semantics=sem, vmem_limit_bytes=VMEM_LIMIT)


def _row_tile(s):
    return min(512, s)


def _w_in_spec(w_in_g, layer):
    return pl.BlockSpec((N_DEV, None) + w_in_g.shape[2:], lambda i: (0, layer, 0, 0))


def _w_out_spec(w_out_g, layer):
    return pl.BlockSpec((N_DEV, None) + w_out_g.shape[2:], lambda i: (0, layer, 0, 0))


def _ln_proj(x, shift, scale1p, g, w_in_g, layer, gather=()):
    s = x.shape[0]
    ts = _row_tile(s)
    ns = s // ts
    n_g = len(gather)

    def body(x_ref, sh_ref, sc_ref, g_ref, w_ref, *rest):
        ret_ref, qkv_ref, sg_ref = rest[n_g:n_g + 3]
        if n_g:
            start, forward, finish = _gather_plan(rest[:n_g], rest[n_g + 3:2 * n_g + 3], *rest[2 * n_g + 3:])
            i = pl.program_id(0)
            pl.when(i == 0)(start)
            pl.when(i == max(ns - 4, 0))(forward)
        xv = x_ref[...]
        rstd = lax.rsqrt(jnp.mean(xv * xv, axis=-1, keepdims=True) + EPS)
        h = (xv * rstd * g_ref[...]) * sc_ref[...] + sh_ref[...]
        hb = _mx(h)
        for n in range(4):
            ret_ref[:, n * 512:(n + 1) * 512] = _dot(hb, w_ref[n])
        qkv_ref[:, 0:512] = _mx(_dot(hb, w_ref[4]) * SB_SCALE)
        qkv_ref[:, 512:1024] = _mx(_dot(hb, w_ref[5]))
        qkv_ref[:, 1024:1536] = _mx(_dot(hb, w_ref[6]))
        sg_ref[...] = _dot(hb, w_ref[7])
        if n_g:
            pl.when(i == ns - 1)(finish)

    vec = pl.BlockSpec((1, D_MODEL), lambda i: (0, 0))
    return pl.pallas_call(
        body, name="ln_proj_gather" if n_g else "ln_proj", grid=(ns,),
        in_specs=[pl.BlockSpec((ts, D_MODEL), lambda i: (i, 0)), vec, vec, vec,
                  _w_in_spec(w_in_g, layer)] + [HBM_SPEC] * n_g,
        out_specs=[pl.BlockSpec((ts, 2048), lambda i: (i, 0)),
                   pl.BlockSpec((ts, 1536), lambda i: (i, 0)),
                   pl.BlockSpec((ts, 512), lambda i: (i, 0))] + [HBM_SPEC] * n_g,
        out_shape=[jax.ShapeDtypeStruct((s, 2048), F32),
                   jax.ShapeDtypeStruct((s, 1536), MXU_DTYPE),
                   jax.ShapeDtypeStruct((s, 512), F32)] + _gathered_shapes(gather),
        scratch_shapes=_gather_sems(n_g) if n_g else (),
        compiler_params=_params(("arbitrary",)),
    )(x, shift, scale1p, g, w_in_g, *gather)


def _w_out_halves(w_ref):
    half = N_DEV // 2
    return (w_ref[0:half].reshape(D_RET, D_MODEL), w_ref[half:N_DEV].reshape(D_SB, D_MODEL))


def _out_proj(x, gate, y_r, y_s, w_out_g, layer):
    s = x.shape[0]
    ts = _row_tile(s)

    def body(x_ref, gate_ref, yr_ref, ys_ref, w_ref, o_ref):
        w_r, w_s = _w_out_halves(w_ref)
        t = _dot(yr_ref[...], w_r) + _dot(ys_ref[...], w_s)
        o_ref[...] = x_ref[...] + gate_ref[...] * t

    return pl.pallas_call(
        body, name="out_proj", grid=(s // ts,),
        in_specs=[pl.BlockSpec((ts, D_MODEL), lambda i: (i, 0)),
                  pl.BlockSpec((1, D_MODEL), lambda i: (0, 0)),
                  pl.BlockSpec((ts, 512), lambda i: (i, 0)),
                  pl.BlockSpec((ts, 512), lambda i: (i, 0)),
                  _w_out_spec(w_out_g, layer)],
        out_specs=pl.BlockSpec((ts, D_MODEL), lambda i: (i, 0)),
        out_shape=jax.ShapeDtypeStruct((s, D_MODEL), F32),
        compiler_params=_params(("arbitrary",)),
    )(x, gate, y_r, y_s, w_out_g)


def _final_loss(x, fg, target):
    s = x.shape[0]
    ts = _row_tile(s)

    def body(x_ref, fg_ref, t_ref, dx_ref, loss_ref, dfg_ref):
        i = pl.program_id(0)

        @pl.when(i == 0)
        def _():
            loss_ref[...] = jnp.zeros_like(loss_ref)
            dfg_ref[...] = jnp.zeros_like(dfg_ref)

        xv = x_ref[...]
        fgv = fg_ref[...]
        rstd = lax.rsqrt(jnp.mean(xv * xv, axis=-1, keepdims=True) + EPS)
        xn = xv * rstd
        err = xn * fgv - t_ref[...]
        tok = jnp.mean(err * err, axis=-1, keepdims=True)
        loss_ref[...] += 0.5 * jnp.sum(tok, axis=0, keepdims=True)
        dy = err * (1.0 / D_MODEL)
        dfg_ref[...] += jnp.sum(dy * xn, axis=0, keepdims=True)
        dxn = dy * fgv
        dx_ref[...] = rstd * (dxn - xn * jnp.mean(dxn * xn, axis=-1, keepdims=True))

    return pl.pallas_call(
        body, name="final_loss", grid=(s // ts,),
        in_specs=[pl.BlockSpec((ts, D_MODEL), lambda i: (i, 0)),
                  pl.BlockSpec((1, D_MODEL), lambda i: (0, 0)),
                  pl.BlockSpec((ts, D_MODEL), lambda i: (i, 0))],
        out_specs=[pl.BlockSpec((ts, D_MODEL), lambda i: (i, 0)),
                   pl.BlockSpec((1, 1), lambda i: (0, 0)),
                   pl.BlockSpec((1, D_MODEL), lambda i: (0, 0))],
        out_shape=[jax.ShapeDtypeStruct((s, D_MODEL), F32),
                   jax.ShapeDtypeStruct((1, 1), F32),
                   jax.ShapeDtypeStruct((1, D_MODEL), F32)],
        compiler_params=_params(("arbitrary",)),
    )(x, fg, target)


def _out_proj_bwd(dx_out, gate, y_r, y_s, w_out_g, layer):
    s = dx_out.shape[0]
    ts = _row_tile(s)

    def body(dx_ref, gate_ref, yr_ref, ys_ref, w_ref, dy_ref, dw_ref, dgate_ref):
        i = pl.program_id(0)

        @pl.when(i == 0)
        def _():
            dw_ref[...] = jnp.zeros_like(dw_ref)
            dgate_ref[...] = jnp.zeros_like(dgate_ref)

        dxv = dx_ref[...]
        dt = _mx(dxv * gate_ref[...])
        yr = yr_ref[...]
        ys = ys_ref[...]
        w_r, w_s = _w_out_halves(w_ref)
        dy_ref[:, 0:512] = _dot_nt(dt, w_r)
        dy_ref[:, 512:1024] = _dot_nt(dt, w_s)
        dw_ref[0:512, :] += _dot_tn(yr, dt)
        dw_ref[512:1024, :] += _dot_tn(ys, dt)
        t = _dot(yr, w_r) + _dot(ys, w_s)
        dgate_ref[...] += jnp.sum(dxv * t, axis=0, keepdims=True)

    return pl.pallas_call(
        body, name="out_proj_bwd", grid=(s // ts,),
        in_specs=[pl.BlockSpec((ts, D_MODEL), lambda i: (i, 0)),
                  pl.BlockSpec((1, D_MODEL), lambda i: (0, 0)),
                  pl.BlockSpec((ts, 512), lambda i: (i, 0)),
                  pl.BlockSpec((ts, 512), lambda i: (i, 0)),
                  _w_out_spec(w_out_g, layer)],
        out_specs=[pl.BlockSpec((ts, D_MODEL), lambda i: (i, 0)),
                   pl.BlockSpec((D_MODEL, D_MODEL), lambda i: (0, 0)),
                   pl.BlockSpec((1, D_MODEL), lambda i: (0, 0))],
        out_shape=[jax.ShapeDtypeStruct((s, D_MODEL), F32),
                   jax.ShapeDtypeStruct((D_MODEL, D_MODEL), F32),
                   jax.ShapeDtypeStruct((1, D_MODEL), F32)],
        compiler_params=_params(("arbitrary",)),
    )(dx_out, gate, y_r, y_s, w_out_g)


def _scatter_plan(parts_ref, recv_ref, send_sems, recv_sems, local_sem):
    px, py, pc = _my_place()
    mine = _linear(px, py, pc)

    def copy(r):
        peer = (1 - px if r & 4 else px, 1 - py if r & 2 else py, 1 - pc if r & 1 else pc)
        return pltpu.make_async_remote_copy(
            src_ref=parts_ref.at[_linear(*peer)], dst_ref=recv_ref.at[mine],
            send_sem=send_sems.at[r - 1], recv_sem=recv_sems.at[r - 1],
            device_id=peer, device_id_type=MESH_IDS)

    own = pltpu.make_async_copy(parts_ref.at[mine], recv_ref.at[mine], local_sem.at[0])

    def start():
        own.start()
        for r in range(1, N_DEV):
            copy(r).start()

    def finish():
        for r in range(1, N_DEV):
            copy(r).wait_recv()
            copy(r).wait_send()
        own.wait()

    return start, finish


def _in_proj_bwd_x(x, dx_out, dproj, shift, scale1p, g, w_in_g, layer, dwo_parts):
    s = x.shape[0]
    ts = _row_tile(s)
    ns = s // ts
    nb = D_IN // N_DEV

    def body(x_ref, dxo_ref, dr_ref, d4_ref, d5_ref, d6_ref, d7_ref, sh_ref, sc_ref, g_ref, w_ref, dwo_ref,
             dx_ref, dsh_ref, dsc_ref, dg_ref, ht_ref, rout_ref, send_sems, recv_sems, local_sem):
        i = pl.program_id(0)
        start, finish = _scatter_plan(dwo_ref, rout_ref, send_sems, recv_sems, local_sem)

        @pl.when(i == 0)
        def _():
            start()
            dsh_ref[...] = jnp.zeros_like(dsh_ref)
            dsc_ref[...] = jnp.zeros_like(dsc_ref)
            dg_ref[...] = jnp.zeros_like(dg_ref)

        dh = _dot_nt(dr_ref[:, 0:nb], w_ref[0])
        for n in range(1, 4):
            dh += _dot_nt(dr_ref[:, n * nb:(n + 1) * nb], w_ref[n])
        for n, d_ref in zip(range(4, N_DEV), (d4_ref, d5_ref, d6_ref, d7_ref)):
            dh += _dot_nt(d_ref[...], w_ref[n])
        xv = x_ref[...]
        gv = g_ref[...]
        scv = sc_ref[...]
        rstd = lax.rsqrt(jnp.mean(xv * xv, axis=-1, keepdims=True) + EPS)
        xn = xv * rstd
        xg = xn * gv
        ht_ref[...] = _mx((xg * scv + sh_ref[...]).T)
        dsh_ref[...] += jnp.sum(dh, axis=0, keepdims=True)
        dsc_ref[...] += jnp.sum(dh * xg, axis=0, keepdims=True)
        dhs = dh * scv
        dg_ref[...] += jnp.sum(dhs * xn, axis=0, keepdims=True)
        dxn = dhs * gv
        dx_ref[...] = rstd * (dxn - xn * jnp.mean(dxn * xn, axis=-1, keepdims=True)) + dxo_ref[...]
        pl.when(i == ns - 1)(finish)

    vec = pl.BlockSpec((1, D_MODEL), lambda i: (0, 0))
    return pl.pallas_call(
        body, name="in_proj_bwd_x", grid=(ns,),
        in_specs=[pl.BlockSpec((ts, D_MODEL), lambda i: (i, 0)),
                  pl.BlockSpec((ts, D_MODEL), lambda i: (i, 0)),
                  pl.BlockSpec((ts, 4 * nb), lambda i: (i, 0))]
                 + [pl.BlockSpec((ts, nb), lambda i: (i, 0))] * 4
                 + [vec, vec, vec, _w_in_spec(w_in_g, layer), HBM_SPEC],
        out_specs=[pl.BlockSpec((ts, D_MODEL), lambda i: (i, 0)), vec, vec, vec,
                   pl.BlockSpec((D_MODEL, ts), lambda i: (0, i)), HBM_SPEC],
        out_shape=[jax.ShapeDtypeStruct((s, D_MODEL), F32),
                   jax.ShapeDtypeStruct((1, D_MODEL), F32),
                   jax.ShapeDtypeStruct((1, D_MODEL), F32),
                   jax.ShapeDtypeStruct((1, D_MODEL), F32),
                   jax.ShapeDtypeStruct((D_MODEL, s), MXU_DTYPE),
                   jax.ShapeDtypeStruct(dwo_parts.shape, dwo_parts.dtype)],
        scratch_shapes=[pltpu.SemaphoreType.DMA((N_PEERS,)), pltpu.SemaphoreType.DMA((N_PEERS,)),
                        pltpu.SemaphoreType.DMA((1,))],
        compiler_params=_params(("arbitrary",)),
    )(x, dx_out, *dproj, shift, scale1p, g, w_in_g, dwo_parts)


def _in_proj_bwd_w(me, h_t, dproj):
    s = h_t.shape[1]
    ts = min(4 * _row_tile(s), s)
    ns = s // ts
    nb = D_IN // N_DEV
    last = N_DEV - 1

    def body(me_ref, ht_ref, dr_ref, d4_ref, d5_ref, d6_ref, d7_ref, rin_ref,
             acc, stage, in_send, in_recv, local_sems):
        t = pl.program_id(0)
        slab = (me_ref[0] + 1 + t) % N_DEV
        i = pl.program_id(1)
        mine = _linear(*_my_place())

        def in_copy(step):
            dest = (mine + 1 + step) % N_DEV
            return pltpu.make_async_remote_copy(
                src_ref=stage.at[step % 2], dst_ref=rin_ref.at[mine],
                send_sem=in_send.at[step], recv_sem=in_recv.at[step],
                device_id=(dest // 4, (dest // 2) % 2, dest % 2), device_id_type=MESH_IDS)

        own_in = pltpu.make_async_copy(stage.at[last % 2], rin_ref.at[mine], local_sems.at[0])

        @pl.when(i == 0)
        def _():
            acc[...] = jnp.zeros_like(acc)

        @pl.when(slab < 4)
        def _():
            acc[...] += _dot(ht_ref[...], dr_ref[...])

        for n, d_ref in zip(range(4, N_DEV), (d4_ref, d5_ref, d6_ref, d7_ref)):
            @pl.when(slab == n)
            def _():
                acc[...] += _dot(ht_ref[...], d_ref[...])

        @pl.when(i == ns - 1)
        def _():
            @pl.when(t >= 2)
            def _():
                in_copy(t - 2).wait_send()

            stage[t % 2] = acc[...].astype(stage.dtype)

            @pl.when(t < last)
            def _():
                in_copy(t).start()

            @pl.when(t == last)
            def _():
                own_in.start()
                in_copy(last - 1).wait_send()
                for step in range(last):
                    in_copy(step).wait_recv()
                own_in.wait()

    def slab_of(t, me_ref):
        return (me_ref[0] + 1 + t) % N_DEV

    def part_rows(n, t, i, me_ref):
        return jnp.where(slab_of(t, me_ref) == n, i, ns - 1), 0

    return pl.pallas_call(
        body, name="in_proj_bwd_w",
        grid_spec=pltpu.PrefetchScalarGridSpec(
            num_scalar_prefetch=1, grid=(N_DEV, ns),
            in_specs=[pl.BlockSpec((D_MODEL, ts), lambda t, i, me_ref: (0, i)),
                      pl.BlockSpec((ts, nb), lambda t, i, me_ref: (
                          jnp.where(slab_of(t, me_ref) < 4, i, ns - 1), jnp.minimum(slab_of(t, me_ref), 3)))]
                     + [pl.BlockSpec((ts, nb), functools.partial(part_rows, n)) for n in range(4, N_DEV)],
            out_specs=HBM_SPEC,
            scratch_shapes=[pltpu.VMEM((D_MODEL, nb), F32),
                            pltpu.VMEM((2, D_MODEL, nb), MXU_DTYPE),
                            pltpu.SemaphoreType.DMA((N_PEERS,)), pltpu.SemaphoreType.DMA((N_PEERS,)),
                            pltpu.SemaphoreType.DMA((1,))]),
        out_shape=jax.ShapeDtypeStruct((N_DEV, D_MODEL, nb), MXU_DTYPE),
        compiler_params=_params(("arbitrary", "arbitrary")),
    )(jnp.reshape(me, (1,)).astype(jnp.int32), h_t, *dproj)


RET_TILE = 256


def _ret_tables(s):
    t = min(RET_TILE, s)
    half = RET_HEAD_DIM // 2
    pos = jnp.arange(s, dtype=F32)
    inv = ROPE_BASE ** (-jnp.arange(half, dtype=F32) / half)
    ang = pos[:, None] * inv[None, :]
    cos, sin = jnp.cos(ang), jnp.sin(ang)
    cos2 = jnp.concatenate([cos, cos], axis=1)
    sin2 = jnp.concatenate([-sin, sin], axis=1)
    lg = jnp.log1p(-(2.0 ** (-5.0 - jnp.arange(RET_HEADS, dtype=F32))))[:, None, None]
    n = jnp.arange(t)
    dist = (n[:, None] - n[None, :]).astype(F32)[None]
    cn = (n // CHUNK)[:, None]
    cm = (n // CHUNK)[None, :]
    mask = jnp.where((cn == cm)[None], jnp.exp(jnp.abs(dist) * lg),
                     jnp.where((cm < cn)[None], jnp.exp(dist * lg), 0.0))
    nf = n.astype(F32)[None, :, None]
    dq = jnp.broadcast_to(jnp.exp((nf + 1.0) * lg), (RET_HEADS, t, LANES))
    dk = jnp.broadcast_to(jnp.exp((t - 1.0 - nf) * lg), (RET_HEADS, t, LANES))
    gt = jnp.broadcast_to(jnp.exp(float(t) * lg), (RET_HEADS, 1, LANES))
    return cos2, sin2, mask, dq, dk, gt


def _roll_half(x):
    return pltpu.roll(x, RET_HEAD_DIM // 2, 1)


def _ret_heads_fwd(ret_ref, cos, sin, m_ref, dq_ref, dk_ref, s0):
    hd = RET_HEAD_DIM
    heads = range(RET_HEADS)
    qb, kb, vb, kdb = [], [], [], []
    for h in heads:
        q = ret_ref[:, h * hd:(h + 1) * hd]
        k = ret_ref[:, 512 + h * hd:512 + (h + 1) * hd]
        kr = (k * cos + _roll_half(k) * sin) * RET_KSCALE
        qb.append(_mx(q * cos + _roll_half(q) * sin))
        kb.append(_mx(kr))
        kdb.append(_mx(kr * dk_ref[h]))
        vb.append(_mx(ret_ref[:, 1024 + h * hd:1024 + (h + 1) * hd]))
    p = [_dot_nt(qb[h], kb[h]) for h in heads]
    cross = [_dot(qb[h], _mx(s0[h])) for h in heads]
    pb = [_mx(p[h] * m_ref[h]) for h in heads]
    o = [_dot(pb[h], vb[h]) + cross[h] * dq_ref[h] for h in heads]
    gn, rstd = [], []
    for h in heads:
        oc = o[h] - jnp.mean(o[h], axis=-1, keepdims=True)
        rstd.append(lax.rsqrt(jnp.mean(oc * oc, axis=-1, keepdims=True) + EPS))
        gn.append(oc * rstd[h])
    return qb, kb, vb, pb, kdb, gn, rstd


def _retention_fwd(ret, tables):
    cos2, sin2, mask, dq, dk, gt = tables
    s = ret.shape[0]
    t = mask.shape[1]
    nt = s // t
    hd = RET_HEAD_DIM

    def body(ret_ref, cos_ref, sin_ref, m_ref, dq_ref, dk_ref, gt_ref, y_ref, st_ref, s_scr):
        i = pl.program_id(0)

        @pl.when(i == 0)
        def _():
            s_scr[...] = jnp.zeros_like(s_scr)

        s0 = [s_scr[h] for h in range(RET_HEADS)]
        _, _, vb, _, kdb, gn, _ = _ret_heads_fwd(ret_ref, cos_ref[...], sin_ref[...], m_ref, dq_ref, dk_ref, s0)
        kv = [_dot_tn(kdb[h], vb[h]) for h in range(RET_HEADS)]
        for h in range(RET_HEADS):
            g = ret_ref[:, 1536 + h * hd:1536 + (h + 1) * hd]
            st_ref[h] = s0[h]
            y_ref[:, h * hd:(h + 1) * hd] = (gn[h] * (g * _sigmoid(g))).astype(y_ref.dtype)
            s_scr[h] = s0[h] * gt_ref[h] + kv[h]

    full3 = lambda a: pl.BlockSpec(a.shape, lambda i: (0, 0, 0))
    return pl.pallas_call(
        body, name="retention_fwd", grid=(nt,),
        in_specs=[pl.BlockSpec((t, 2048), lambda i: (i, 0)),
                  pl.BlockSpec((t, LANES), lambda i: (i, 0)),
                  pl.BlockSpec((t, LANES), lambda i: (i, 0)),
                  full3(mask), full3(dq), full3(dk), full3(gt)],
        out_specs=[pl.BlockSpec((t, 512), lambda i: (i, 0)),
                   pl.BlockSpec((None, RET_HEADS, hd, hd), lambda i: (i, 0, 0, 0))],
        out_shape=[jax.ShapeDtypeStruct((s, 512), MXU_DTYPE),
                   jax.ShapeDtypeStruct((nt, RET_HEADS, hd, hd), F32)],
        scratch_shapes=[pltpu.VMEM((RET_HEADS, hd, hd), F32)],
        compiler_params=_params(("arbitrary",)),
    )(ret, cos2, sin2, mask, dq, dk, gt)


def _retention_bwd(ret, states, dy, tables):
    cos2, sin2, mask, dq, dk, gt = tables
    s = ret.shape[0]
    t = mask.shape[1]
    nt = s // t
    hd = RET_HEAD_DIM

    def body(ret_ref, st_ref, dy_ref, cos_ref, sin_ref, m_ref, dq_ref, dk_ref, gt_ref, d_ref, ds_scr):
        i = pl.program_id(0)

        @pl.when(i == 0)
        def _():
            ds_scr[...] = jnp.zeros_like(ds_scr)

        cos = cos_ref[...]
        sin = sin_ref[...]
        heads = range(RET_HEADS)
        s0 = [st_ref[h] for h in heads]
        ds = [ds_scr[h] for h in heads]
        dsb = [_mx(ds[h]) for h in heads]
        qb, kb, vb, pb, kdb, gn, rstd = _ret_heads_fwd(ret_ref, cos, sin, m_ref, dq_ref, dk_ref, s0)
        dob, dodb = [], []
        for h in heads:
            g = ret_ref[:, 1536 + h * hd:1536 + (h + 1) * hd]
            dyv = dy_ref[:, h * hd:(h + 1) * hd]
            sg = _sigmoid(g)
            d_ref[:, 1536 + h * hd:1536 + (h + 1) * hd] = (
                dyv * gn[h] * (sg * (1.0 + g * (1.0 - sg)))).astype(d_ref.dtype)
            dgn = dyv * (g * sg)
            do = rstd[h] * (dgn - jnp.mean(dgn, axis=-1, keepdims=True)
                            - gn[h] * jnp.mean(dgn * gn[h], axis=-1, keepdims=True))
            dob.append(_mx(do))
            dodb.append(_mx(do * dq_ref[h]))
        dp = [_dot_nt(dob[h], vb[h]) for h in heads]
        dv = [_dot_tn(pb[h], dob[h]) + _dot(kdb[h], dsb[h]) for h in heads]
        dq_cross = [_dot_nt(dodb[h], _mx(s0[h])) for h in heads]
        dk_cross = [_dot_nt(vb[h], dsb[h]) for h in heads]
        ds_new = [_dot_tn(qb[h], dodb[h]) for h in heads]
        dpb = [_mx(dp[h] * m_ref[h]) for h in heads]
        dqr = [_dot(dpb[h], kb[h]) + dq_cross[h] for h in heads]
        dkr = [(_dot_tn(dpb[h], qb[h]) + dk_cross[h] * dk_ref[h]) * RET_KSCALE for h in heads]
        for h in heads:
            d_ref[:, 1024 + h * hd:1024 + (h + 1) * hd] = dv[h].astype(d_ref.dtype)
            d_ref[:, h * hd:(h + 1) * hd] = (dqr[h] * cos + _roll_half(dqr[h] * sin)).astype(d_ref.dtype)
            d_ref[:, 512 + h * hd:512 + (h + 1) * hd] = (
                dkr[h] * cos + _roll_half(dkr[h] * sin)).astype(d_ref.dtype)
            ds_scr[h] = ds[h] * gt_ref[h] + ds_new[h]

    full3 = lambda a: pl.BlockSpec(a.shape, lambda i: (0, 0, 0))
    rev = lambda i: (nt - 1 - i, 0)
    return pl.pallas_call(
        body, name="retention_bwd", grid=(nt,),
        in_specs=[pl.BlockSpec((t, 2048), rev),
                  pl.BlockSpec((None, RET_HEADS, hd, hd), lambda i: (nt - 1 - i, 0, 0, 0)),
                  pl.BlockSpec((t, 512), rev),
                  pl.BlockSpec((t, LANES), rev),
                  pl.BlockSpec((t, LANES), rev),
                  full3(mask), full3(dq), full3(dk), full3(gt)],
        out_specs=pl.BlockSpec((t, 2048), rev),
        out_shape=jax.ShapeDtypeStruct((s, 2048), MXU_DTYPE),
        scratch_shapes=[pltpu.VMEM((RET_HEADS, hd, hd), F32)],
        compiler_params=_params(("arbitrary",)),
    )(ret, states, dy, cos2, sin2, mask, dq, dk, gt)


SB_BLOCK = 256
SB_CHUNK = 32


SB_SKIP = 104.0


def _split_dots(xs, u):
    parts = []
    for x in xs:
        hi = lax.bitcast_convert_type(lax.bitcast_convert_type(x, jnp.uint32) & jnp.uint32(0xFFFF0000), F32)
        parts += [_mx(hi), _mx(x - hi)]
    out = _dot(jnp.concatenate(parts, axis=0), u)
    n = xs[0].shape[0]
    return [out[2 * k * n:(2 * k + 1) * n] + out[(2 * k + 1) * n:(2 * k + 2) * n] for k in range(len(xs))]


def _split_dot(x, u):
    return _split_dots([x], u)[0]


def _sb_pair_weights(lb, lk, allowed, u_gt):
    blk = lb.shape[0]
    lk_p, lk_d = lk[:, :blk], lk[:, blk:]
    r_d = _rowsum(lk_d)
    cs_p, cs_d = _split_dots([lk_p, lk_d], u_gt)
    a = jnp.exp(lb + jnp.concatenate([cs_p + r_d, cs_d], axis=1))
    return jnp.where(allowed, a, 0.0), r_d, r_d + _rowsum(lk_p)


def _sb_logits(q, k, causal):
    z = _dot_nt(q, k)
    l1p = jnp.log(1.0 + jnp.exp(-jnp.abs(z)))
    lk = -(jnp.maximum(z, 0.0) + l1p)
    if causal is not None:
        lk = jnp.where(causal, lk, 0.0)
    return jnp.minimum(z, 0.0) - l1p, lk


def _sb_weights(lb, lk, r, u_gt, causal):
    a = jnp.exp(lb + _split_dot(lk, u_gt) + r)
    return a if causal is None else jnp.where(causal, a, 0.0)


def _rowsum(x):
    return jnp.sum(x, axis=1, keepdims=True)


def _sb_pair_tile(i, blk):
    row = lax.broadcasted_iota(jnp.int32, (blk, 2 * blk), 0)
    col = lax.broadcasted_iota(jnp.int32, (blk, 2 * blk), 1)
    first_col = jnp.where(i >= 1, 0, blk)
    allowed = jnp.logical_and(row > col - blk, col >= first_col)
    rows_p = pl.ds(pl.multiple_of(jnp.maximum(i - 1, 0) * blk, blk), blk)
    rows_d = pl.ds(pl.multiple_of(i * blk, blk), blk)
    return allowed, rows_p, rows_d


def _sb_fwd(qkv, sg):
    s = qkv.shape[0]
    blk = min(SB_BLOCK, s)
    nq = s // blk
    hd = SB_HEAD_DIM

    ch = min(SB_CHUNK, blk)

    def body(q_ref, k_ref, v_ref, g_ref, y_ref, o_ref, z_scr, lhs_scr, cs_scr, a_scr):
        i = pl.program_id(1)
        row = lax.broadcasted_iota(jnp.int32, (blk, blk), 0)
        col = lax.broadcasted_iota(jnp.int32, (blk, blk), 1)
        u_gt = (row > col).astype(MXU_DTYPE)
        heads = [slice(hh * hd, (hh + 1) * hd) for hh in range(2)]
        qs = [q_ref[:, ls] for ls in heads]
        _, rows_p, rows_d = _sb_pair_tile(i, blk)
        has_prev = i >= 1
        crow = lax.broadcasted_iota(jnp.int32, (ch, blk), 0)
        ccol = lax.broadcasted_iota(jnp.int32, (ch, blk), 1)

        def logits(hh):
            kc = jnp.concatenate([k_ref[rows_p, heads[hh]], k_ref[rows_d, heads[hh]]], axis=0)
            z_scr[hh] = _dot_nt(qs[hh], kc)

        def keep_parts(hh):
            r_d, r_all = [], []
            for c in range(blk // ch):
                rows = pl.ds(c * ch, ch)
                causal = crow + c * ch > ccol
                z = z_scr[hh, rows, :]
                l1p = jnp.log(1.0 + jnp.exp(-jnp.abs(z)))
                lk = -(jnp.maximum(z, 0.0) + l1p)
                z_scr[hh, rows, :] = jnp.minimum(z, 0.0) - l1p
                lk_p = jnp.where(has_prev, lk[:, :blk], 0.0)
                lk_d = jnp.where(causal, lk[:, blk:], 0.0)
                lhs_scr[hh, pl.ds(c * ch, ch), :] = _mx(lk_p)
                lhs_scr[hh, pl.ds(blk + c * ch, ch), :] = _mx(lk_d)
                r_d.append(_rowsum(lk_d))
                r_all.append(r_d[c] + _rowsum(lk_p))
            return r_d, jnp.concatenate(r_all, axis=0)

        def suffix_sums(hh):
            cs_scr[hh] = _dot(lhs_scr[hh], u_gt)

        def weights(hh, r_d):
            for c in range(blk // ch):
                rows = pl.ds(c * ch, ch)
                causal = crow + c * ch > ccol
                cs_p = cs_scr[hh, pl.ds(c * ch, ch), :] + r_d[c]
                cs_d = cs_scr[hh, pl.ds(blk + c * ch, ch), :]
                lb = z_scr[hh, rows, :]
                a_p = jnp.where(has_prev, jnp.exp(lb[:, :blk] + cs_p), 0.0)
                a_d = jnp.where(causal, jnp.exp(lb[:, blk:] + cs_d), 0.0)
                a_scr[hh, rows, :] = _mx(jnp.concatenate([a_p, a_d], axis=1))

        def values(hh):
            vc = jnp.concatenate([v_ref[rows_p, heads[hh]], v_ref[rows_d, heads[hh]]], axis=0)
            return _dot(a_scr[hh], vc)

        def block(hh, j, r):
            start = pl.multiple_of(j * blk, blk)
            lb, lk = _sb_logits(qs[hh], k_ref[pl.ds(start, blk), heads[hh]], None)
            a = _sb_weights(lb, lk, r, u_gt, None)
            return _dot(_mx(a), v_ref[pl.ds(start, blk), heads[hh]]), r + _rowsum(lk)

        def more(n, r0, r1):
            return jnp.logical_and(n < i, jnp.max(jnp.maximum(r0, r1)) > -SB_SKIP)

        logits(0)
        logits(1)
        gates = [g_ref[:, ls] * _sigmoid(g_ref[:, ls]) for ls in heads]
        rd0, r0 = keep_parts(0)
        suffix_sums(0)
        rd1, r1 = keep_parts(1)
        suffix_sums(1)
        go = more(jnp.int32(1), r0, r1)
        weights(0, rd0)
        acc0 = values(0)
        weights(1, rd1)
        acc1 = values(1)

        def step(c):
            _, n, acc0, r0, acc1, r1 = c
            pv0, r0 = block(0, i - 1 - n, r0)
            pv1, r1 = block(1, i - 1 - n, r1)
            return more(n + 1, r0, r1), n + 1, acc0 + pv0, r0, acc1 + pv1, r1

        _, _, acc0, _, acc1, _ = lax.while_loop(lambda c: c[0], step, (go, jnp.int32(1), acc0, r0, acc1, r1))
        for ls, acc, gate in zip(heads, (acc0, acc1), gates):
            o_ref[:, ls] = acc
            y_ref[:, ls] = (acc * gate).astype(y_ref.dtype)

    qblk = pl.BlockSpec((blk, LANES), lambda p, i: (i, p))
    return pl.pallas_call(
        body, name="stickbreak_fwd", grid=(SB_HEADS // 2, nq),
        in_specs=[qblk,
                  pl.BlockSpec((s, LANES), lambda p, i: (0, 4 + p)),
                  pl.BlockSpec((s, LANES), lambda p, i: (0, 8 + p)),
                  qblk],
        out_specs=[qblk, qblk],
        out_shape=[jax.ShapeDtypeStruct((s, 512), MXU_DTYPE),
                   jax.ShapeDtypeStruct((s, 512), F32)],
        scratch_shapes=[pltpu.VMEM((2, blk, 2 * blk), F32),
                        pltpu.VMEM((2, 2 * blk, blk), MXU_DTYPE),
                        pltpu.VMEM((2, 2 * blk, blk), F32),
                        pltpu.VMEM((2, blk, 2 * blk), MXU_DTYPE)],
        compiler_params=_params(("arbitrary", "arbitrary")),
    )(qkv, qkv, qkv, sg)


def _sb_bwd(qkv, sg, o, dy):
    s = qkv.shape[0]
    blk = min(SB_BLOCK, s)
    nq = s // blk
    hd = SB_HEAD_DIM
    assert nq <= LANES
    ch = min(SB_CHUNK, blk)

    def body(q_ref, k_ref, v_ref, g_ref, o_ref, dy_ref, dq_ref, dk_ref, dv_ref, dg_ref, dk_scr, dv_scr,
             z_scr, g_scr, lhs_scr, cs_scr, a_scr, dz_scr):
        i = pl.program_id(1)

        @pl.when(i == 0)
        def _():
            dk_scr[...] = jnp.zeros_like(dk_scr)
            dv_scr[...] = jnp.zeros_like(dv_scr)

        row = lax.broadcasted_iota(jnp.int32, (blk, blk), 0)
        col = lax.broadcasted_iota(jnp.int32, (blk, blk), 1)
        lane = lax.broadcasted_iota(jnp.int32, (blk, LANES), 1)
        u_gt = (row > col).astype(MXU_DTYPE)
        u_lt = (row < col).astype(MXU_DTYPE)
        heads = [slice(hh * hd, (hh + 1) * hd) for hh in range(2)]
        qs = [q_ref[:, ls] for ls in heads]
        _, rows_p, rows_d = _sb_pair_tile(i, blk)
        has_prev = i >= 1
        crow = lax.broadcasted_iota(jnp.int32, (ch, blk), 0)
        ccol = lax.broadcasted_iota(jnp.int32, (ch, blk), 1)
        nch = blk // ch
        kcs = [jnp.concatenate([k_ref[rows_p, ls], k_ref[rows_d, ls]], axis=0) for ls in heads]
        dobs = []

        def gate_grads(hh):
            ls = heads[hh]
            g = g_ref[:, ls]
            dyv = dy_ref[:, ls]
            sgm = _sigmoid(g)
            dg_ref[:, ls] = (dyv * o_ref[:, ls] * (sgm * (1.0 + g * (1.0 - sgm)))).astype(dg_ref.dtype)
            dobs.append(_mx(dyv * (g * sgm)))

        def split_rows(hh, c, part, x):
            lhs_scr[hh, pl.ds(part * blk + c * ch, ch), :] = _mx(x)

        def summed_rows(hh, c, part):
            return cs_scr[hh, pl.ds(part * blk + c * ch, ch), :]

        def logits(hh):
            z_scr[hh] = _dot_nt(qs[hh], kcs[hh])

        def weight_grads(hh):
            vc = jnp.concatenate([v_ref[rows_p, heads[hh]], v_ref[rows_d, heads[hh]]], axis=0)
            g_scr[hh] = _dot_nt(dobs[hh], vc)

        def keep_parts(hh):
            r_d, r_all = [], []
            for c in range(nch):
                rows = pl.ds(c * ch, ch)
                z = z_scr[hh, rows, :]
                l1p = jnp.log(1.0 + jnp.exp(-jnp.abs(z)))
                lk = -(jnp.maximum(z, 0.0) + l1p)
                z_scr[hh, rows, :] = jnp.minimum(z, 0.0) - l1p
                lk_p = jnp.where(has_prev, lk[:, :blk], 0.0)
                lk_d = jnp.where(crow + c * ch > ccol, lk[:, blk:], 0.0)
                split_rows(hh, c, 0, lk_p)
                split_rows(hh, c, 1, lk_d)
                r_d.append(_rowsum(lk_d))
                r_all.append(r_d[c] + _rowsum(lk_p))
            return r_d, jnp.concatenate(r_all, axis=0)

        def weights(hh, r_d):
            g_p = []
            for c in range(nch):
                rows = pl.ds(c * ch, ch)
                lb = z_scr[hh, rows, :]
                a_p = jnp.where(has_prev, jnp.exp(lb[:, :blk] + (summed_rows(hh, c, 0) + r_d[c])), 0.0)
                a_d = jnp.where(crow + c * ch > ccol, jnp.exp(lb[:, blk:] + summed_rows(hh, c, 1)), 0.0)
                a = jnp.concatenate([a_p, a_d], axis=1)
                a_scr[hh, rows, :] = _mx(a)
                gm = g_scr[hh, rows, :] * a
                g_scr[hh, rows, :] = gm
                split_rows(hh, c, 0, gm[:, :blk])
                split_rows(hh, c, 1, gm[:, blk:])
                g_p.append(_rowsum(gm[:, :blk]))
            return g_p

        def logit_grads(hh, pg, g_p):
            for c in range(nch):
                rows = pl.ds(c * ch, ch)
                pre = jnp.concatenate([summed_rows(hh, c, 0) + pg[c * ch:(c + 1) * ch],
                                       summed_rows(hh, c, 1) + (pg[c * ch:(c + 1) * ch] + g_p[c])], axis=1)
                gm = g_scr[hh, rows, :]
                dz = gm - (gm + pre) * jnp.exp(z_scr[hh, rows, :])
                dz_p = jnp.where(has_prev, dz[:, :blk], 0.0)
                dz_d = jnp.where(crow + c * ch > ccol, dz[:, blk:], 0.0)
                dz_scr[hh, rows, :] = _mx(jnp.concatenate([dz_p, dz_d], axis=1))

        def products(hh, acc):
            ls = heads[hh]
            dk_scr[hh, rows_p, :] += _dot_tn(dz_scr[hh, :, 0:blk], qs[hh])
            dk_scr[hh, rows_d, :] += _dot_tn(dz_scr[hh, :, blk:2 * blk], qs[hh])
            dv_scr[hh, rows_p, :] += _dot_tn(a_scr[hh, :, 0:blk], dobs[hh])
            dv_scr[hh, rows_d, :] += _dot_tn(a_scr[hh, :, blk:2 * blk], dobs[hh])
            dq_ref[:, ls] = ((acc + _dot(dz_scr[hh], kcs[hh])) * SB_SCALE).astype(dq_ref.dtype)

        def suffix_sums(hh):
            cs_scr[hh] = _dot(lhs_scr[hh], u_gt)

        def prefix_sums(hh):
            cs_scr[hh] = _dot(lhs_scr[hh], u_lt)

        def more(n, r0, r1):
            return jnp.logical_and(n < i, jnp.max(jnp.maximum(r0, r1)) > -SB_SKIP)

        logits(0)
        logits(1)
        gate_grads(0)
        gate_grads(1)
        weight_grads(0)
        weight_grads(1)
        rd0, ra0 = keep_parts(0)
        suffix_sums(0)
        rd1, ra1 = keep_parts(1)
        suffix_sums(1)
        go = more(jnp.int32(1), ra0, ra1)
        gp0 = weights(0, rd0)
        prefix_sums(0)
        gp1 = weights(1, rd1)
        prefix_sums(1)

        def scan_block(hh, j, r, rmat):
            start = pl.multiple_of(j * blk, blk)
            _, lk = _sb_logits(qs[hh], k_ref[pl.ds(start, blk), heads[hh]], None)
            return r + _rowsum(lk), jnp.where(lane == j, r, rmat)

        def scan_step(c):
            _, n, r0, rmat0, r1, rmat1 = c
            r0, rmat0 = scan_block(0, i - 1 - n, r0, rmat0)
            r1, rmat1 = scan_block(1, i - 1 - n, r1, rmat1)
            return more(n + 1, r0, r1), n + 1, r0, rmat0, r1, rmat1

        zmat = jnp.zeros((blk, LANES), F32)
        _, n, _, rmat0, _, rmat1 = lax.while_loop(lambda c: c[0], scan_step,
                                                  (go, jnp.int32(1), ra0, zmat, ra1, zmat))
        rmats = (rmat0, rmat1)

        def block(hh, j, pg):
            ls = heads[hh]
            start = pl.multiple_of(j * blk, blk)
            k = k_ref[pl.ds(start, blk), ls]
            lb, lk = _sb_logits(qs[hh], k, None)
            r = _rowsum(jnp.where(lane == j, rmats[hh], 0.0))
            a = _sb_weights(lb, lk, r, u_gt, None)
            gm = _dot_nt(dobs[hh], v_ref[pl.ds(start, blk), ls]) * a
            dzb = _mx(gm - (gm + (pg + _split_dot(gm, u_lt))) * jnp.exp(lb))
            dk_scr[hh, pl.ds(start, blk), :] += _dot_tn(dzb, qs[hh])
            dv_scr[hh, pl.ds(start, blk), :] += _dot_tn(_mx(a), dobs[hh])
            return _dot(dzb, k), pg + _rowsum(gm)

        def step(t, c):
            acc0, pg0, acc1, pg1 = c
            dq0, pg0 = block(0, i - n + t, pg0)
            dq1, pg1 = block(1, i - n + t, pg1)
            return acc0 + dq0, pg0, acc1 + dq1, pg1

        zero = jnp.zeros((blk, 1), F32)
        zacc = jnp.zeros((blk, hd), F32)
        acc0, pg0, acc1, pg1 = lax.fori_loop(0, n - 1, step, (zacc, zero, zacc, zero))
        logit_grads(0, pg0, gp0)
        products(0, acc0)
        logit_grads(1, pg1, gp1)
        products(1, acc1)

        @pl.when(i == nq - 1)
        def _():
            for hh in range(2):
                ls = slice(hh * hd, (hh + 1) * hd)
                dk_ref[:, ls] = dk_scr[hh].astype(dk_ref.dtype)
                dv_ref[:, ls] = dv_scr[hh].astype(dv_ref.dtype)

    qblk = lambda c0: pl.BlockSpec((blk, LANES), lambda p, i: (i, c0 + p))
    full = lambda c0: pl.BlockSpec((s, LANES), lambda p, i: (0, c0 + p))
    half = jax.ShapeDtypeStruct((s, 512), MXU_DTYPE)
    return pl.pallas_call(
        body, name="stickbreak_bwd", grid=(SB_HEADS // 2, nq),
        in_specs=[qblk(0), full(4), full(8), qblk(0), qblk(0), qblk(4)],
        out_specs=[qblk(0), full(0), full(0), qblk(0)],
        out_shape=[half, half, half, half],
        scratch_shapes=[pltpu.VMEM((2, s, hd), F32), pltpu.VMEM((2, s, hd), F32),
                        pltpu.VMEM((2, blk, 2 * blk), F32),
                        pltpu.VMEM((2, blk, 2 * blk), F32),
                        pltpu.VMEM((2, 2 * blk, blk), MXU_DTYPE),
                        pltpu.VMEM((2, 2 * blk, blk), F32),
                        pltpu.VMEM((2, blk, 2 * blk), MXU_DTYPE),
                        pltpu.VMEM((2, blk, 2 * blk), MXU_DTYPE)],
        compiler_params=_params(("arbitrary", "arbitrary")),
    )(qkv, qkv, qkv, sg, o, dy)


def _layer_fwd(layer, x, mod, norm_g, w_in_g, w_out_g, tables, gather=()):
    shift, scale1p, gate = mod[0:1], 1.0 + mod[1:2], mod[2:3]
    ret, qkv, sg, *gathered = _ln_proj(x, shift, scale1p, norm_g, w_in_g, layer, gather)
    y_r, states = _retention_fwd(ret, tables)
    y_s, o_s = _sb_fwd(qkv, sg)
    x_next = _out_proj(x, gate, y_r, y_s, w_out_g, layer)
    saved = (x, shift, scale1p, gate, ret, qkv, sg, y_r, states, y_s, o_s)
    return x_next, saved, gathered


def _layer_bwd(layer, me, dx_out, saved, norm_g, w_in_g, w_out_g, tables):
    x, shift, scale1p, gate, ret, qkv, sg, y_r, states, y_s, o_s = saved
    dy, dw_out, dgate = _out_proj_bwd(dx_out, gate, y_r, y_s, w_out_g, layer)
    d_ret = _retention_bwd(ret, states, dy, tables)
    d_q, d_k, d_v, d_g = _sb_bwd(qkv, sg, o_s, dy)
    dproj = (d_ret, d_q, d_k, d_v, d_g)
    dwo_parts = _mx(dw_out.reshape(N_DEV, D_MODEL // N_DEV, D_MODEL))
    dx, dshift, dscale, dnorm_g, h_t, r_out = _in_proj_bwd_x(
        x, dx_out, dproj, shift, scale1p, norm_g, w_in_g, layer, dwo_parts)
    r_in = _in_proj_bwd_w(me, h_t, dproj)
    dmod = jnp.concatenate([dshift, dscale, dgate], axis=1)
    return dx, r_in, r_out, dmod, dnorm_g


MESH_IDS = pl.DeviceIdType.MESH
N_PEERS = N_DEV - 1
HBM_SPEC = pl.BlockSpec(memory_space=pl.ANY)


def _my_place():
    return lax.axis_index("x"), lax.axis_index("y"), lax.axis_index("c")


def _linear(px, py, pc):
    return 4 * px + 2 * py + pc


def _all_gather(blocks):
    n_arr = len(blocks)

    def body(*refs):
        start, forward, finish = _gather_plan(refs[:n_arr], refs[n_arr:2 * n_arr], *refs[2 * n_arr:])
        start()
        forward()
        finish()

    return pl.pallas_call(
        body, name="all_gather",
        out_shape=_gathered_shapes(blocks),
        in_specs=[HBM_SPEC] * n_arr, out_specs=[HBM_SPEC] * n_arr,
        scratch_shapes=_gather_sems(n_arr),
    )(*blocks)


def _gathered_shapes(blocks):
    return [jax.ShapeDtypeStruct((N_DEV * b.shape[0], b.shape[1]), b.dtype) for b in blocks]


def _gather_sems(n_arr):
    return [pltpu.SemaphoreType.DMA((n_arr * N_PEERS,)), pltpu.SemaphoreType.DMA((n_arr * N_PEERS,)),
            pltpu.SemaphoreType.DMA((n_arr,))]


def _gather_plan(x_refs, out_refs, send_sems, recv_sems, local_sems):
    n_arr = len(x_refs)
    x, y, c = _my_place()
    me, sibling = (x, y, c), (x, y, 1 - c)
    chips = [(1 - x, y), (x, 1 - y), (1 - x, 1 - y)]

    def rows(a, place):
        m = x_refs[a].shape[0]
        return out_refs[a].at[pl.ds(_linear(*place) * m, m), :]

    def copy(a, k, block, to, src=None):
        return pltpu.make_async_remote_copy(
            src_ref=rows(a, block) if src is None else src, dst_ref=rows(a, block),
            send_sem=send_sems.at[a * N_PEERS + k], recv_sem=recv_sems.at[a * N_PEERS + k],
            device_id=to, device_id_type=MESH_IDS)

    mine = [pltpu.make_async_copy(x_refs[a], rows(a, me), local_sems.at[a]) for a in range(n_arr)]
    first = []
    for a in range(n_arr):
        first.append(copy(a, 0, me, sibling, src=x_refs[a]))
        first += [copy(a, 1 + j, me, (*chip, c), src=x_refs[a]) for j, chip in enumerate(chips)]
    passed = [copy(a, 4 + j, (*chip, c), sibling) for j, chip in enumerate(chips) for a in range(n_arr)]

    def start():
        for cp in mine + first:
            cp.start()

    def forward():
        for j, chip in enumerate(chips):
            for a in range(n_arr):
                copy(a, 1 + j, (*chip, c), me).wait_recv()
                passed[j * n_arr + a].start()

    def finish():
        for a in range(n_arr):
            copy(a, 0, sibling, me).wait_recv()
            for j, chip in enumerate(chips):
                copy(a, 4 + j, (*chip, 1 - c), me).wait_recv()
        for cp in first + passed:
            cp.wait_send()
        for cp in mine:
            cp.wait()

    return start, forward, finish


def _ada_fwd(c_all, w_ada, b_cols):
    cols = w_ada.shape[2]

    def body(c_ref, w_ref, b_ref, ca_ref, mod_ref):
        cv = c_ref[...]
        ca = cv * _sigmoid(cv)
        ca_ref[...] = ca
        cb = _mx(jnp.concatenate([ca, ca], axis=0))
        for l in range(DEPTH):
            mod_ref[l * N_DEV:(l + 1) * N_DEV, :] = _dot(cb, _mx(w_ref[l]))[0:N_DEV] + b_ref[l]

    return pl.pallas_call(
        body, name="ada_fwd",
        out_shape=[jax.ShapeDtypeStruct((N_DEV, D_MODEL), F32),
                   jax.ShapeDtypeStruct((DEPTH * N_DEV, cols), F32)],
        compiler_params=_params(),
    )(c_all, w_ada, b_cols)


def _ada_bwd(c_act_t, dmod_cols):
    cols = dmod_cols.shape[2]

    def body(ca_ref, dm_ref, o_ref):
        ca = _mx(ca_ref[...]).astype(F32)
        for l in range(DEPTH):
            o_ref[l] = jnp.dot(ca, _mx(dm_ref[l]).astype(F32),
                               precision=lax.Precision.HIGHEST, preferred_element_type=F32)

    return pl.pallas_call(
        body, name="ada_bwd",
        out_shape=jax.ShapeDtypeStruct((DEPTH, D_MODEL, cols), F32),
        compiler_params=_params(),
    )(c_act_t, dmod_cols)


def _adamw_store(g, w_ref, m_ref, v_ref, g_ref, d_ref, mo_ref, vo_ref):
    m2 = ADAM_B1 * m_ref[...] + (1.0 - ADAM_B1) * g
    v2 = ADAM_B2 * v_ref[...] + (1.0 - ADAM_B2) * (g * g)
    m_hat = m2 / (1.0 - ADAM_B1 ** ADAM_STEP)
    v_hat = v2 / (1.0 - ADAM_B2 ** ADAM_STEP)
    g_ref[...] = g
    d_ref[...] = -ADAM_LR * (m_hat / (jnp.sqrt(v_hat) + ADAM_EPS) + ADAM_WD * w_ref[...])
    mo_ref[...] = m2
    vo_ref[...] = v2


def _slab_sum(p_ref):
    g = p_ref[0].astype(F32)
    for sl in range(1, p_ref.shape[0]):
        g = g + p_ref[sl].astype(F32)
    return g


def _sum_adamw_layers(parts, w, m, v):
    n_slab, rows, cols = parts[0].shape
    tr = min(256, rows)
    nt = rows // tr

    def body(p0_ref, p1_ref, w_ref, m_ref, v_ref, g_ref, d_ref, mo_ref, vo_ref):
        for l, p_ref in enumerate((p0_ref, p1_ref)):
            @pl.when(pl.program_id(0) == l)
            def _():
                _adamw_store(_slab_sum(p_ref), w_ref, m_ref, v_ref, g_ref, d_ref, mo_ref, vo_ref)

    p_specs = [pl.BlockSpec((n_slab, tr, cols), lambda l, i: (0, i * (1 - l) + (nt - 1) * l, 0)),
               pl.BlockSpec((n_slab, tr, cols), lambda l, i: (0, i * l, 0))]
    blk = pl.BlockSpec((None, tr, cols), lambda l, i: (l, i, 0))
    shp = jax.ShapeDtypeStruct((DEPTH, rows, cols), F32)
    return pl.pallas_call(
        body, name="sum_adamw_layers", grid=(DEPTH, nt),
        in_specs=p_specs + [blk, blk, blk],
        out_specs=[blk, blk, blk, blk],
        out_shape=[shp, shp, shp, shp],
        compiler_params=_params(("arbitrary", "arbitrary")),
    )(parts[0], parts[1], w, m, v)


def _sum_adamw(parts, w, m, v):
    n_slab, rows, cols = parts.shape
    tr = min(256, rows)

    def body(p_ref, w_ref, m_ref, v_ref, g_ref, d_ref, mo_ref, vo_ref):
        _adamw_store(_slab_sum(p_ref), w_ref, m_ref, v_ref, g_ref, d_ref, mo_ref, vo_ref)

    blk = pl.BlockSpec((tr, cols), lambda i: (i, 0))
    shp = jax.ShapeDtypeStruct((rows, cols), F32)
    return pl.pallas_call(
        body, name="sum_adamw", grid=(rows // tr,),
        in_specs=[pl.BlockSpec((n_slab, tr, cols), lambda i: (0, i, 0)), blk, blk, blk],
        out_specs=[blk, blk, blk, blk],
        out_shape=[shp, shp, shp, shp],
        compiler_params=_params(("arbitrary",)),
    )(parts, w, m, v)


SMALL_ROWS = 16


def kernel(x, c, norm_g, w_ada, b_ada, w_in, w_out, final_g, loss_target, m_norm_g, m_w_ada, m_b_ada, m_w_in, m_w_out, m_final_g, v_norm_g, v_w_ada, v_b_ada, v_w_in, v_w_out, v_final_g):
    me = _linear(*_my_place())
    in_cols = w_in.shape[2]
    out_rows = w_out.shape[1]
    ada_cols = w_ada.shape[2]

    w_in_m, w_out_m = _mx(w_in), _mx(w_out)
    g_in, g_out, g_c = _all_gather([w_in_m[0], w_out_m[0], jnp.broadcast_to(c, (8, D_MODEL))])
    w_in_g = [g_in.reshape(N_DEV, 1, D_MODEL, in_cols), None]
    w_out_g = [g_out.reshape(N_DEV, 1, out_rows, D_MODEL), None]
    c_all = g_c.reshape(N_DEV, 8, D_MODEL)[:, 0]

    b_cols = lax.dynamic_slice_in_dim(b_ada, me * ada_cols, ada_cols, axis=1)[:, None, :]
    c_act, mod_cols = _ada_fwd(c_all, w_ada, b_cols)
    (g_mod,) = _all_gather([mod_cols])
    g_mod = g_mod.reshape(N_DEV, DEPTH, N_DEV, ada_cols)
    mod = lax.dynamic_index_in_dim(g_mod, me, axis=2, keepdims=False)
    mod = mod.transpose(1, 0, 2).reshape(DEPTH, 3, D_MODEL)

    tables = _ret_tables(x.shape[1])
    h = x[0]
    saved = []
    for l in range(DEPTH):
        nxt = (w_in_m[l + 1], w_out_m[l + 1]) if l + 1 < DEPTH else ()
        h, sv, gathered = _layer_fwd(0, h, mod[l], norm_g[l:l + 1], w_in_g[l], w_out_g[l], tables, nxt)
        if nxt:
            w_in_g[l + 1] = gathered[0].reshape(N_DEV, 1, D_MODEL, in_cols)
            w_out_g[l + 1] = gathered[1].reshape(N_DEV, 1, out_rows, D_MODEL)
        saved.append(sv)
    dx, loss_part, dfg = _final_loss(h, final_g[None], loss_target[0])
    r_in, r_out, small = [None] * DEPTH, [None] * DEPTH, [None] * DEPTH
    for l in reversed(range(DEPTH)):
        dx, r_in[l], r_out[l], dmod, dng = _layer_bwd(0, me, dx, saved[l], norm_g[l:l + 1], w_in_g[l], w_out_g[l], tables)
        small[l] = (dmod.reshape(3, D_MODEL), dng)

    pad = jnp.zeros((SMALL_ROWS - 10, D_MODEL), F32)
    small_block = jnp.concatenate([small[0][0], small[1][0], small[0][1], small[1][1], dfg,
                                   jnp.broadcast_to(loss_part, (1, D_MODEL)), pad], axis=0)
    (g_small,) = _all_gather([small_block])
    g_small = g_small.reshape(N_DEV, SMALL_ROWS, D_MODEL)

    def small_pack(b, n, f, fill):
        return jnp.concatenate([b.reshape(6, D_MODEL), n, f[None],
                                jnp.full((SMALL_ROWS - 9, D_MODEL), fill, F32)], axis=0)

    s_g, s_d, s_m, s_v = _sum_adamw(g_small, small_pack(b_ada, norm_g, final_g, 0.0),
                                    small_pack(m_b_ada, m_norm_g, m_final_g, 0.0),
                                    small_pack(v_b_ada, v_norm_g, v_final_g, 1.0))
    loss = s_g[9, 0]

    def small_unpack(a):
        return a[0:6].reshape(DEPTH, 3 * D_MODEL), a[6:8], a[8]

    dmod_all = g_small[:, 0:6].reshape(N_DEV, DEPTH, 3 * D_MODEL).transpose(1, 0, 2)
    dmod_cols = lax.dynamic_slice_in_dim(dmod_all, me * ada_cols, ada_cols, axis=2)
    g_ada = _ada_bwd(c_act.T, dmod_cols).reshape(1, DEPTH * D_MODEL, ada_cols)
    ada = _sum_adamw(g_ada, *[a.reshape(DEPTH * D_MODEL, ada_cols) for a in (w_ada, m_w_ada, v_w_ada)])
    ada = [a.reshape(DEPTH, D_MODEL, ada_cols) for a in ada]

    win = _sum_adamw_layers(r_in, w_in, m_w_in, v_w_in)
    wout = _sum_adamw_layers(r_out, w_out, m_w_out, v_w_out)

    outs = [loss, dx[None]]
    for k in range(4):
        b, n, f = small_unpack((s_g, s_d, s_m, s_v)[k])
        outs += [n, ada[k], b, win[k], wout[k], f]
    return tuple(outs)
```

```python
import functools

import jax
import jax.numpy as jnp
from jax import lax
from jax.experimental import pallas as pl
from jax.experimental.pallas import tpu as pltpu

F32 = jnp.float32
MXU_DTYPE = jnp.bfloat16

D_MODEL = 1024
DEPTH = 2
N_DEV = 8
CHUNK = 64
D_RET = 512
D_SB = 512
RET_HEADS = 4
RET_HEAD_DIM = 128
SB_HEADS = 8
SB_HEAD_DIM = 64
D_IN = 4096
ROPE_BASE = 10000.0
EPS = 1e-6
SB_SCALE = SB_HEAD_DIM ** -0.5
RET_KSCALE = RET_HEAD_DIM ** -0.5

ADAM_LR = 0.001
ADAM_B1 = 0.9
ADAM_B2 = 0.999
ADAM_EPS = 1e-08
ADAM_WD = 0.01
ADAM_STEP = 10

V7X_VMEM_BYTES = 64 * 2 ** 20
VMEM_LIMIT = V7X_VMEM_BYTES - 8 * 2 ** 20
LANES = 128

_NT = (((1,), (1,)), ((), ()))
_TN = (((0,), (0,)), ((), ()))


def _dot(a, b):
    return jnp.dot(a, b, preferred_element_type=F32)


def _dot_nt(a, b):
    return lax.dot_general(a, b, _NT, preferred_element_type=F32)


def _dot_tn(a, b):
    return lax.dot_general(a, b, _TN, preferred_element_type=F32)


def _mx(x):
    return x.astype(MXU_DTYPE)


def _sigmoid(x):
    return 1.0 / (1.0 + jnp.exp(-x))


def _params(sem=None):
    return pltpu.CompilerParams(dimension_semantics=sem, vmem_limit_bytes=VMEM_LIMIT)


def _row_tile(s):
    return min(512, s)


def _w_in_spec(w_in_g, layer):
    return pl.BlockSpec((N_DEV, None) + w_in_g.shape[2:], lambda i: (0, layer, 0, 0))


def _w_out_spec(w_out_g, layer):
    return pl.BlockSpec((N_DEV, None) + w_out_g.shape[2:], lambda i: (0, layer, 0, 0))


def _ln_proj(x, shift, scale1p, g, w_in_g, layer, gather=()):
    s = x.shape[0]
    ts = _row_tile(s)
    ns = s // ts
    n_g = len(gather)

    def body(x_ref, sh_ref, sc_ref, g_ref, w_ref, *rest):
        ret_ref, qkv_ref, sg_ref = rest[n_g:n_g + 3]
        if n_g:
            start, forward, finish = _gather_plan(rest[:n_g], rest[n_g + 3:2 * n_g + 3], *rest[2 * n_g + 3:])
            i = pl.program_id(0)
            pl.when(i == 0)(start)
            pl.when(i == max(ns - 4, 0))(forward)
        xv = x_ref[...]
        rstd = lax.rsqrt(jnp.mean(xv * xv, axis=-1, keepdims=True) + EPS)
        h = (xv * rstd * g_ref[...]) * sc_ref[...] + sh_ref[...]
        hb = _mx(h)
        for n in range(4):
            ret_ref[:, n * 512:(n + 1) * 512] = _dot(hb, w_ref[n])
        qkv_ref[:, 0:512] = _mx(_dot(hb, w_ref[4]) * SB_SCALE)
        qkv_ref[:, 512:1024] = _mx(_dot(hb, w_ref[5]))
        qkv_ref[:, 1024:1536] = _mx(_dot(hb, w_ref[6]))
        sg_ref[...] = _dot(hb, w_ref[7])
        if n_g:
            pl.when(i == ns - 1)(finish)

    vec = pl.BlockSpec((1, D_MODEL), lambda i: (0, 0))
    return pl.pallas_call(
        body, name="ln_proj_gather" if n_g else "ln_proj", grid=(ns,),
        in_specs=[pl.BlockSpec((ts, D_MODEL), lambda i: (i, 0)), vec, vec, vec,
                  _w_in_spec(w_in_g, layer)] + [HBM_SPEC] * n_g,
        out_specs=[pl.BlockSpec((ts, 2048), lambda i: (i, 0)),
                   pl.BlockSpec((ts, 1536), lambda i: (i, 0)),
                   pl.BlockSpec((ts, 512), lambda i: (i, 0))] + [HBM_SPEC] * n_g,
        out_shape=[jax.ShapeDtypeStruct((s, 2048), F32),
                   jax.ShapeDtypeStruct((s, 1536), MXU_DTYPE),
                   jax.ShapeDtypeStruct((s, 512), F32)] + _gathered_shapes(gather),
        scratch_shapes=_gather_sems(n_g) if n_g else (),
        compiler_params=_params(("arbitrary",)),
    )(x, shift, scale1p, g, w_in_g, *gather)


def _w_out_halves(w_ref):
    half = N_DEV // 2
    return (w_ref[0:half].reshape(D_RET, D_MODEL), w_ref[half:N_DEV].reshape(D_SB, D_MODEL))


def _out_proj(x, gate, y_r, y_s, w_out_g, layer):
    s = x.shape[0]
    ts = _row_tile(s)

    def body(x_ref, gate_ref, yr_ref, ys_ref, w_ref, o_ref):
        w_r, w_s = _w_out_halves(w_ref)
        t = _dot(yr_ref[...], w_r) + _dot(ys_ref[...], w_s)
        o_ref[...] = x_ref[...] + gate_ref[...] * t

    return pl.pallas_call(
        body, name="out_proj", grid=(s // ts,),
        in_specs=[pl.BlockSpec((ts, D_MODEL), lambda i: (i, 0)),
                  pl.BlockSpec((1, D_MODEL), lambda i: (0, 0)),
                  pl.BlockSpec((ts, 512), lambda i: (i, 0)),
                  pl.BlockSpec((ts, 512), lambda i: (i, 0)),
                  _w_out_spec(w_out_g, layer)],
        out_specs=pl.BlockSpec((ts, D_MODEL), lambda i: (i, 0)),
        out_shape=jax.ShapeDtypeStruct((s, D_MODEL), F32),
        compiler_params=_params(("arbitrary",)),
    )(x, gate, y_r, y_s, w_out_g)


def _final_loss(x, fg, target):
    s = x.shape[0]
    ts = _row_tile(s)

    def body(x_ref, fg_ref, t_ref, dx_ref, loss_ref, dfg_ref):
        i = pl.program_id(0)

        @pl.when(i == 0)
        def _():
            loss_ref[...] = jnp.zeros_like(loss_ref)
            dfg_ref[...] = jnp.zeros_like(dfg_ref)

        xv = x_ref[...]
        fgv = fg_ref[...]
        rstd = lax.rsqrt(jnp.mean(xv * xv, axis=-1, keepdims=True) + EPS)
        xn = xv * rstd
        err = xn * fgv - t_ref[...]
        tok = jnp.mean(err * err, axis=-1, keepdims=True)
        loss_ref[...] += 0.5 * jnp.sum(tok, axis=0, keepdims=True)
        dy = err * (1.0 / D_MODEL)
        dfg_ref[...] += jnp.sum(dy * xn, axis=0, keepdims=True)
        dxn = dy * fgv
        dx_ref[...] = rstd * (dxn - xn * jnp.mean(dxn * xn, axis=-1, keepdims=True))

    return pl.pallas_call(
        body, name="final_loss", grid=(s // ts,),
        in_specs=[pl.BlockSpec((ts, D_MODEL), lambda i: (i, 0)),
                  pl.BlockSpec((1, D_MODEL), lambda i: (0, 0)),
                  pl.BlockSpec((ts, D_MODEL), lambda i: (i, 0))],
        out_specs=[pl.BlockSpec((ts, D_MODEL), lambda i: (i, 0)),
                   pl.BlockSpec((1, 1), lambda i: (0, 0)),
                   pl.BlockSpec((1, D_MODEL), lambda i: (0, 0))],
        out_shape=[jax.ShapeDtypeStruct((s, D_MODEL), F32),
                   jax.ShapeDtypeStruct((1, 1), F32),
                   jax.ShapeDtypeStruct((1, D_MODEL), F32)],
        compiler_params=_params(("arbitrary",)),
    )(x, fg, target)


def _out_proj_bwd(dx_out, gate, y_r, y_s, w_out_g, layer):
    s = dx_out.shape[0]
    ts = _row_tile(s)

    def body(dx_ref, gate_ref, yr_ref, ys_ref, w_ref, dy_ref, dw_ref, dgate_ref):
        i = pl.program_id(0)

        @pl.when(i == 0)
        def _():
            dw_ref[...] = jnp.zeros_like(dw_ref)
            dgate_ref[...] = jnp.zeros_like(dgate_ref)

        dxv = dx_ref[...]
        dt = _mx(dxv * gate_ref[...])
        yr = yr_ref[...]
        ys = ys_ref[...]
        w_r, w_s = _w_out_halves(w_ref)
        dy_ref[:, 0:512] = _dot_nt(dt, w_r)
        dy_ref[:, 512:1024] = _dot_nt(dt, w_s)
        dw_ref[0:512, :] += _dot_tn(yr, dt)
        dw_ref[512:1024, :] += _dot_tn(ys, dt)
        t = _dot(yr, w_r) + _dot(ys, w_s)
        dgate_ref[...] += jnp.sum(dxv * t, axis=0, keepdims=True)

    return pl.pallas_call(
        body, name="out_proj_bwd", grid=(s // ts,),
        in_specs=[pl.BlockSpec((ts, D_MODEL), lambda i: (i, 0)),
                  pl.BlockSpec((1, D_MODEL), lambda i: (0, 0)),
                  pl.BlockSpec((ts, 512), lambda i: (i, 0)),
                  pl.BlockSpec((ts, 512), lambda i: (i, 0)),
                  _w_out_spec(w_out_g, layer)],
        out_specs=[pl.BlockSpec((ts, D_MODEL), lambda i: (i, 0)),
                   pl.BlockSpec((D_MODEL, D_MODEL), lambda i: (0, 0)),
                   pl.BlockSpec((1, D_MODEL), lambda i: (0, 0))],
        out_shape=[jax.ShapeDtypeStruct((s, D_MODEL), F32),
                   jax.ShapeDtypeStruct((D_MODEL, D_MODEL), F32),
                   jax.ShapeDtypeStruct((1, D_MODEL), F32)],
        compiler_params=_params(("arbitrary",)),
    )(dx_out, gate, y_r, y_s, w_out_g)


def _scatter_plan(parts_ref, recv_ref, send_sems, recv_sems, local_sem):
    px, py, pc = _my_place()
    mine = _linear(px, py, pc)

    def copy(r):
        peer = (1 - px if r & 4 else px, 1 - py if r & 2 else py, 1 - pc if r & 1 else pc)
        return pltpu.make_async_remote_copy(
            src_ref=parts_ref.at[_linear(*peer)], dst_ref=recv_ref.at[mine],
            send_sem=send_sems.at[r - 1], recv_sem=recv_sems.at[r - 1],
            device_id=peer, device_id_type=MESH_IDS)

    own = pltpu.make_async_copy(parts_ref.at[mine], recv_ref.at[mine], local_sem.at[0])

    def start():
        own.start()
        for r in range(1, N_DEV):
            copy(r).start()

    def finish():
        for r in range(1, N_DEV):
            copy(r).wait_recv()
            copy(r).wait_send()
        own.wait()

    return start, finish


def _in_proj_bwd_x(x, dx_out, dproj, shift, scale1p, g, w_in_g, layer, dwo_parts):
    s = x.shape[0]
    ts = _row_tile(s)
    ns = s // ts
    nb = D_IN // N_DEV

    def body(x_ref, dxo_ref, dr_ref, d4_ref, d5_ref, d6_ref, d7_ref, sh_ref, sc_ref, g_ref, w_ref, dwo_ref,
             dx_ref, dsh_ref, dsc_ref, dg_ref, ht_ref, rout_ref, send_sems, recv_sems, local_sem):
        i = pl.program_id(0)
        start, finish = _scatter_plan(dwo_ref, rout_ref, send_sems, recv_sems, local_sem)

        @pl.when(i == 0)
        def _():
            start()
            dsh_ref[...] = jnp.zeros_like(dsh_ref)
            dsc_ref[...] = jnp.zeros_like(dsc_ref)
            dg_ref[...] = jnp.zeros_like(dg_ref)

        dh = _dot_nt(dr_ref[:, 0:nb], w_ref[0])
        for n in range(1, 4):
            dh += _dot_nt(dr_ref[:, n * nb:(n + 1) * nb], w_ref[n])
        for n, d_ref in zip(range(4, N_DEV), (d4_ref, d5_ref, d6_ref, d7_ref)):
            dh += _dot_nt(d_ref[...], w_ref[n])
        xv = x_ref[...]
        gv = g_ref[...]
        scv = sc_ref[...]
        rstd = lax.rsqrt(jnp.mean(xv * xv, axis=-1, keepdims=True) + EPS)
        xn = xv * rstd
        xg = xn * gv
        ht_ref[...] = _mx((xg * scv + sh_ref[...]).T)
        dsh_ref[...] += jnp.sum(dh, axis=0, keepdims=True)
        dsc_ref[...] += jnp.sum(dh * xg, axis=0, keepdims=True)
        dhs = dh * scv
        dg_ref[...] += jnp.sum(dhs * xn, axis=0, keepdims=True)
        dxn = dhs * gv
        dx_ref[...] = rstd * (dxn - xn * jnp.mean(dxn * xn, axis=-1, keepdims=True)) + dxo_ref[...]
        pl.when(i == ns - 1)(finish)

    vec = pl.BlockSpec((1, D_MODEL), lambda i: (0, 0))
    return pl.pallas_call(
        body, name="in_proj_bwd_x", grid=(ns,),
        in_specs=[pl.BlockSpec((ts, D_MODEL), lambda i: (i, 0)),
                  pl.BlockSpec((ts, D_MODEL), lambda i: (i, 0)),
                  pl.BlockSpec((ts, 4 * nb), lambda i: (i, 0))]
                 + [pl.BlockSpec((ts, nb), lambda i: (i, 0))] * 4
                 + [vec, vec, vec, _w_in_spec(w_in_g, layer), HBM_SPEC],
        out_specs=[pl.BlockSpec((ts, D_MODEL), lambda i: (i, 0)), vec, vec, vec,
                   pl.BlockSpec((D_MODEL, ts), lambda i: (0, i)), HBM_SPEC],
        out_shape=[jax.ShapeDtypeStruct((s, D_MODEL), F32),
                   jax.ShapeDtypeStruct((1, D_MODEL), F32),
                   jax.ShapeDtypeStruct((1, D_MODEL), F32),
                   jax.ShapeDtypeStruct((1, D_MODEL), F32),
                   jax.ShapeDtypeStruct((D_MODEL, s), MXU_DTYPE),
                   jax.ShapeDtypeStruct(dwo_parts.shape, dwo_parts.dtype)],
        scratch_shapes=[pltpu.SemaphoreType.DMA((N_PEERS,)), pltpu.SemaphoreType.DMA((N_PEERS,)),
                        pltpu.SemaphoreType.DMA((1,))],
        compiler_params=_params(("arbitrary",)),
    )(x, dx_out, *dproj, shift, scale1p, g, w_in_g, dwo_parts)


def _in_proj_bwd_w(me, h_t, dproj):
    s = h_t.shape[1]
    ts = min(4 * _row_tile(s), s)
    ns = s // ts
    nb = D_IN // N_DEV
    n_chip = N_DEV // 2

    def flip_bits(j):
        return jnp.where(j == 0, 4, jnp.where(j == 1, 2, jnp.where(j == 2, 6, 0)))

    def slab_of(t, me_ref):
        return jnp.bitwise_xor(me_ref[0], flip_bits(t // 2) + 1 - t % 2)

    def body(me_ref, ht_ref, dr_ref, d4_ref, d5_ref, d6_ref, d7_ref, rin_ref,
             acc, stage, pre_buf, pre_send, pre_recv, sum_send, sum_recv, local_sem):
        t = pl.program_id(0)
        i = pl.program_id(1)
        j = t // 2
        summing = t % 2 == 1
        slab = slab_of(t, me_ref)
        px, py, pc = _my_place()

        def pre_copy(jj):
            return pltpu.make_async_remote_copy(
                src_ref=stage.at[jj % 2], dst_ref=pre_buf.at[jj],
                send_sem=pre_send.at[jj], recv_sem=pre_recv.at[jj],
                device_id=(px, py, 1 - pc), device_id_type=MESH_IDS)

        def sum_copy(jj):
            fx = jnp.logical_or(jj == 0, jj == 2)
            fy = jnp.logical_or(jj == 1, jj == 2)
            return pltpu.make_async_remote_copy(
                src_ref=stage.at[2 + jj % 2], dst_ref=rin_ref.at[jj],
                send_sem=sum_send.at[jj], recv_sem=sum_recv.at[jj],
                device_id=(jnp.where(fx, 1 - px, px), jnp.where(fy, 1 - py, py), pc), device_id_type=MESH_IDS)

        own = pltpu.make_async_copy(stage.at[3], rin_ref.at[n_chip - 1], local_sem.at[0])

        @pl.when(i == 0)
        def _():
            acc[...] = jnp.zeros_like(acc)

        @pl.when(slab < 4)
        def _():
            acc[...] += _dot(ht_ref[...], dr_ref[...])

        for n, d_ref in zip(range(4, N_DEV), (d4_ref, d5_ref, d6_ref, d7_ref)):
            @pl.when(slab == n)
            def _():
                acc[...] += _dot(ht_ref[...], d_ref[...])

        @pl.when(jnp.logical_and(i == ns - 1, jnp.logical_not(summing)))
        def _():
            @pl.when(j >= 2)
            def _():
                pre_copy(j - 2).wait_send()

            stage[j % 2] = acc[...].astype(stage.dtype)
            pre_copy(j).start()

        @pl.when(jnp.logical_and(i == ns - 1, summing))
        def _():
            pre_copy(j).wait_recv()

            @pl.when(j >= 2)
            def _():
                sum_copy(j - 2).wait_send()

            stage[2 + j % 2] = (acc[...] + pre_buf[j].astype(F32)).astype(stage.dtype)

            @pl.when(j < n_chip - 1)
            def _():
                sum_copy(j).start()

            @pl.when(j == n_chip - 1)
            def _():
                own.start()
                pre_copy(n_chip - 2).wait_send()
                pre_copy(n_chip - 1).wait_send()
                sum_copy(n_chip - 2).wait_send()
                for jj in range(n_chip - 1):
                    sum_copy(jj).wait_recv()
                own.wait()

    def part_rows(n, t, i, me_ref):
        return jnp.where(slab_of(t, me_ref) == n, i, ns - 1), 0

    return pl.pallas_call(
        body, name="in_proj_bwd_w",
        grid_spec=pltpu.PrefetchScalarGridSpec(
            num_scalar_prefetch=1, grid=(N_DEV, ns),
            in_specs=[pl.BlockSpec((D_MODEL, ts), lambda t, i, me_ref: (0, i)),
                      pl.BlockSpec((ts, nb), lambda t, i, me_ref: (
                          jnp.where(slab_of(t, me_ref) < 4, i, ns - 1), jnp.minimum(slab_of(t, me_ref), 3)))]
                     + [pl.BlockSpec((ts, nb), functools.partial(part_rows, n)) for n in range(4, N_DEV)],
            out_specs=HBM_SPEC,
            scratch_shapes=[pltpu.VMEM((D_MODEL, nb), F32),
                            pltpu.VMEM((4, D_MODEL, nb), MXU_DTYPE),
                            pltpu.VMEM((n_chip, D_MODEL, nb), MXU_DTYPE),
                            pltpu.SemaphoreType.DMA((n_chip,)), pltpu.SemaphoreType.DMA((n_chip,)),
                            pltpu.SemaphoreType.DMA((n_chip - 1,)), pltpu.SemaphoreType.DMA((n_chip - 1,)),
                            pltpu.SemaphoreType.DMA((1,))]),
        out_shape=jax.ShapeDtypeStruct((n_chip, D_MODEL, nb), MXU_DTYPE),
        compiler_params=_params(("arbitrary", "arbitrary")),
    )(jnp.reshape(me, (1,)).astype(jnp.int32), h_t, *dproj)


RET_TILE = 256


def _ret_tables(s):
    t = min(RET_TILE, s)
    half = RET_HEAD_DIM // 2
    pos = jnp.arange(s, dtype=F32)
    inv = ROPE_BASE ** (-jnp.arange(half, dtype=F32) / half)
    ang = pos[:, None] * inv[None, :]
    cos, sin = jnp.cos(ang), jnp.sin(ang)
    cos2 = jnp.concatenate([cos, cos], axis=1)
    sin2 = jnp.concatenate([-sin, sin], axis=1)
    lg = jnp.log1p(-(2.0 ** (-5.0 - jnp.arange(RET_HEADS, dtype=F32))))[:, None, None]
    n = jnp.arange(t)
    dist = (n[:, None] - n[None, :]).astype(F32)[None]
    cn = (n // CHUNK)[:, None]
    cm = (n // CHUNK)[None, :]
    mask = jnp.where((cn == cm)[None], jnp.exp(jnp.abs(dist) * lg),
                     jnp.where((cm < cn)[None], jnp.exp(dist * lg), 0.0))
    nf = n.astype(F32)[None, :, None]
    dq = jnp.broadcast_to(jnp.exp((nf + 1.0) * lg), (RET_HEADS, t, LANES))
    dk = jnp.broadcast_to(jnp.exp((t - 1.0 - nf) * lg), (RET_HEADS, t, LANES))
    gt = jnp.broadcast_to(jnp.exp(float(t) * lg), (RET_HEADS, 1, LANES))
    return cos2, sin2, mask, dq, dk, gt


def _roll_half(x):
    return pltpu.roll(x, RET_HEAD_DIM // 2, 1)


def _ret_heads_fwd(ret_ref, cos, sin, m_ref, dq_ref, dk_ref, s0):
    hd = RET_HEAD_DIM
    heads = range(RET_HEADS)
    qb, kb, vb, kdb = [], [], [], []
    for h in heads:
        q = ret_ref[:, h * hd:(h + 1) * hd]
        k = ret_ref[:, 512 + h * hd:512 + (h + 1) * hd]
        kr = (k * cos + _roll_half(k) * sin) * RET_KSCALE
        qb.append(_mx(q * cos + _roll_half(q) * sin))
        kb.append(_mx(kr))
        kdb.append(_mx(kr * dk_ref[h]))
        vb.append(_mx(ret_ref[:, 1024 + h * hd:1024 + (h + 1) * hd]))
    p = [_dot_nt(qb[h], kb[h]) for h in heads]
    cross = [_dot(qb[h], _mx(s0[h])) for h in heads]
    pb = [_mx(p[h] * m_ref[h]) for h in heads]
    o = [_dot(pb[h], vb[h]) + cross[h] * dq_ref[h] for h in heads]
    gn, rstd = [], []
    for h in heads:
        oc = o[h] - jnp.mean(o[h], axis=-1, keepdims=True)
        rstd.append(lax.rsqrt(jnp.mean(oc * oc, axis=-1, keepdims=True) + EPS))
        gn.append(oc * rstd[h])
    return qb, kb, vb, pb, kdb, gn, rstd


def _retention_fwd(ret, tables):
    cos2, sin2, mask, dq, dk, gt = tables
    s = ret.shape[0]
    t = mask.shape[1]
    nt = s // t
    hd = RET_HEAD_DIM

    def body(ret_ref, cos_ref, sin_ref, m_ref, dq_ref, dk_ref, gt_ref, y_ref, st_ref, s_scr):
        i = pl.program_id(0)

        @pl.when(i == 0)
        def _():
            s_scr[...] = jnp.zeros_like(s_scr)

        s0 = [s_scr[h] for h in range(RET_HEADS)]
        _, _, vb, _, kdb, gn, _ = _ret_heads_fwd(ret_ref, cos_ref[...], sin_ref[...], m_ref, dq_ref, dk_ref, s0)
        kv = [_dot_tn(kdb[h], vb[h]) for h in range(RET_HEADS)]
        for h in range(RET_HEADS):
            g = ret_ref[:, 1536 + h * hd:1536 + (h + 1) * hd]
            st_ref[h] = s0[h]
            y_ref[:, h * hd:(h + 1) * hd] = (gn[h] * (g * _sigmoid(g))).astype(y_ref.dtype)
            s_scr[h] = s0[h] * gt_ref[h] + kv[h]

    full3 = lambda a: pl.BlockSpec(a.shape, lambda i: (0, 0, 0))
    return pl.pallas_call(
        body, name="retention_fwd", grid=(nt,),
        in_specs=[pl.BlockSpec((t, 2048), lambda i: (i, 0)),
                  pl.BlockSpec((t, LANES), lambda i: (i, 0)),
                  pl.BlockSpec((t, LANES), lambda i: (i, 0)),
                  full3(mask), full3(dq), full3(dk), full3(gt)],
        out_specs=[pl.BlockSpec((t, 512), lambda i: (i, 0)),
                   pl.BlockSpec((None, RET_HEADS, hd, hd), lambda i: (i, 0, 0, 0))],
        out_shape=[jax.ShapeDtypeStruct((s, 512), MXU_DTYPE),
                   jax.ShapeDtypeStruct((nt, RET_HEADS, hd, hd), F32)],
        scratch_shapes=[pltpu.VMEM((RET_HEADS, hd, hd), F32)],
        compiler_params=_params(("arbitrary",)),
    )(ret, cos2, sin2, mask, dq, dk, gt)


def _retention_bwd(ret, states, dy, tables):
    cos2, sin2, mask, dq, dk, gt = tables
    s = ret.shape[0]
    t = mask.shape[1]
    nt = s // t
    hd = RET_HEAD_DIM

    def body(ret_ref, st_ref, dy_ref, cos_ref, sin_ref, m_ref, dq_ref, dk_ref, gt_ref, d_ref, ds_scr):
        i = pl.program_id(0)

        @pl.when(i == 0)
        def _():
            ds_scr[...] = jnp.zeros_like(ds_scr)

        cos = cos_ref[...]
        sin = sin_ref[...]
        heads = range(RET_HEADS)
        s0 = [st_ref[h] for h in heads]
        ds = [ds_scr[h] for h in heads]
        dsb = [_mx(ds[h]) for h in heads]
        qb, kb, vb, pb, kdb, gn, rstd = _ret_heads_fwd(ret_ref, cos, sin, m_ref, dq_ref, dk_ref, s0)
        dob, dodb = [], []
        for h in heads:
            g = ret_ref[:, 1536 + h * hd:1536 + (h + 1) * hd]
            dyv = dy_ref[:, h * hd:(h + 1) * hd]
            sg = _sigmoid(g)
            d_ref[:, 1536 + h * hd:1536 + (h + 1) * hd] = (
                dyv * gn[h] * (sg * (1.0 + g * (1.0 - sg)))).astype(d_ref.dtype)
            dgn = dyv * (g * sg)
            do = rstd[h] * (dgn - jnp.mean(dgn, axis=-1, keepdims=True)
                            - gn[h] * jnp.mean(dgn * gn[h], axis=-1, keepdims=True))
            dob.append(_mx(do))
            dodb.append(_mx(do * dq_ref[h]))
        dp = [_dot_nt(dob[h], vb[h]) for h in heads]
        dv = [_dot_tn(pb[h], dob[h]) + _dot(kdb[h], dsb[h]) for h in heads]
        dq_cross = [_dot_nt(dodb[h], _mx(s0[h])) for h in heads]
        dk_cross = [_dot_nt(vb[h], dsb[h]) for h in heads]
        ds_new = [_dot_tn(qb[h], dodb[h]) for h in heads]
        dpb = [_mx(dp[h] * m_ref[h]) for h in heads]
        dqr = [_dot(dpb[h], kb[h]) + dq_cross[h] for h in heads]
        dkr = [(_dot_tn(dpb[h], qb[h]) + dk_cross[h] * dk_ref[h]) * RET_KSCALE for h in heads]
        for h in heads:
            d_ref[:, 1024 + h * hd:1024 + (h + 1) * hd] = dv[h].astype(d_ref.dtype)
            d_ref[:, h * hd:(h + 1) * hd] = (dqr[h] * cos + _roll_half(dqr[h] * sin)).astype(d_ref.dtype)
            d_ref[:, 512 + h * hd:512 + (h + 1) * hd] = (
                dkr[h] * cos + _roll_half(dkr[h] * sin)).astype(d_ref.dtype)
            ds_scr[h] = ds[h] * gt_ref[h] + ds_new[h]

    full3 = lambda a: pl.BlockSpec(a.shape, lambda i: (0, 0, 0))
    rev = lambda i: (nt - 1 - i, 0)
    return pl.pallas_call(
        body, name="retention_bwd", grid=(nt,),
        in_specs=[pl.BlockSpec((t, 2048), rev),
                  pl.BlockSpec((None, RET_HEADS, hd, hd), lambda i: (nt - 1 - i, 0, 0, 0)),
                  pl.BlockSpec((t, 512), rev),
                  pl.BlockSpec((t, LANES), rev),
                  pl.BlockSpec((t, LANES), rev),
                  full3(mask), full3(dq), full3(dk), full3(gt)],
        out_specs=pl.BlockSpec((t, 2048), rev),
        out_shape=jax.ShapeDtypeStruct((s, 2048), MXU_DTYPE),
        scratch_shapes=[pltpu.VMEM((RET_HEADS, hd, hd), F32)],
        compiler_params=_params(("arbitrary",)),
    )(ret, states, dy, cos2, sin2, mask, dq, dk, gt)


SB_BLOCK = 256
SB_CHUNK = 32


SB_SKIP = 104.0


def _split_dots(xs, u):
    parts = []
    for x in xs:
        hi = lax.bitcast_convert_type(lax.bitcast_convert_type(x, jnp.uint32) & jnp.uint32(0xFFFF0000), F32)
        parts += [_mx(hi), _mx(x - hi)]
    out = _dot(jnp.concatenate(parts, axis=0), u)
    n = xs[0].shape[0]
    return [out[2 * k * n:(2 * k + 1) * n] + out[(2 * k + 1) * n:(2 * k + 2) * n] for k in range(len(xs))]


def _split_dot(x, u):
    return _split_dots([x], u)[0]


def _sb_pair_weights(lb, lk, allowed, u_gt):
    blk = lb.shape[0]
    lk_p, lk_d = lk[:, :blk], lk[:, blk:]
    r_d = _rowsum(lk_d)
    cs_p, cs_d = _split_dots([lk_p, lk_d], u_gt)
    a = jnp.exp(lb + jnp.concatenate([cs_p + r_d, cs_d], axis=1))
    return jnp.where(allowed, a, 0.0), r_d, r_d + _rowsum(lk_p)


def _sb_logits(q, k, causal):
    z = _dot_nt(q, k)
    l1p = jnp.log(1.0 + jnp.exp(-jnp.abs(z)))
    lk = -(jnp.maximum(z, 0.0) + l1p)
    if causal is not None:
        lk = jnp.where(causal, lk, 0.0)
    return jnp.minimum(z, 0.0) - l1p, lk


def _sb_weights(lb, lk, r, u_gt, causal):
    a = jnp.exp(lb + _split_dot(lk, u_gt) + r)
    return a if causal is None else jnp.where(causal, a, 0.0)


def _rowsum(x):
    return jnp.sum(x, axis=1, keepdims=True)


def _sb_pair_tile(i, blk):
    row = lax.broadcasted_iota(jnp.int32, (blk, 2 * blk), 0)
    col = lax.broadcasted_iota(jnp.int32, (blk, 2 * blk), 1)
    first_col = jnp.where(i >= 1, 0, blk)
    allowed = jnp.logical_and(row > col - blk, col >= first_col)
    rows_p = pl.ds(pl.multiple_of(jnp.maximum(i - 1, 0) * blk, blk), blk)
    rows_d = pl.ds(pl.multiple_of(i * blk, blk), blk)
    return allowed, rows_p, rows_d


def _sb_fwd(qkv, sg):
    s = qkv.shape[0]
    blk = min(SB_BLOCK, s)
    nq = s // blk
    hd = SB_HEAD_DIM

    ch = min(SB_CHUNK, blk)

    def body(q_ref, k_ref, v_ref, g_ref, y_ref, o_ref, z_scr, lhs_scr, cs_scr, a_scr):
        i = pl.program_id(1)
        row = lax.broadcasted_iota(jnp.int32, (blk, blk), 0)
        col = lax.broadcasted_iota(jnp.int32, (blk, blk), 1)
        u_gt = (row > col).astype(MXU_DTYPE)
        heads = [slice(hh * hd, (hh + 1) * hd) for hh in range(2)]
        qs = [q_ref[:, ls] for ls in heads]
        _, rows_p, rows_d = _sb_pair_tile(i, blk)
        has_prev = i >= 1
        crow = lax.broadcasted_iota(jnp.int32, (ch, blk), 0)
        ccol = lax.broadcasted_iota(jnp.int32, (ch, blk), 1)

        def logits(hh):
            kc = jnp.concatenate([k_ref[rows_p, heads[hh]], k_ref[rows_d, heads[hh]]], axis=0)
            z_scr[hh] = _dot_nt(qs[hh], kc)

        def keep_parts(hh):
            r_d, r_all = [], []
            for c in range(blk // ch):
                rows = pl.ds(c * ch, ch)
                causal = crow + c * ch > ccol
                z = z_scr[hh, rows, :]
                l1p = jnp.log(1.0 + jnp.exp(-jnp.abs(z)))
                lk = -(jnp.maximum(z, 0.0) + l1p)
                z_scr[hh, rows, :] = jnp.minimum(z, 0.0) - l1p
                lk_p = jnp.where(has_prev, lk[:, :blk], 0.0)
                lk_d = jnp.where(causal, lk[:, blk:], 0.0)
                lhs_scr[hh, pl.ds(c * ch, ch), :] = _mx(lk_p)
                lhs_scr[hh, pl.ds(blk + c * ch, ch), :] = _mx(lk_d)
                r_d.append(_rowsum(lk_d))
                r_all.append(r_d[c] + _rowsum(lk_p))
            return r_d, jnp.concatenate(r_all, axis=0)

        def suffix_sums(hh):
            cs_scr[hh] = _dot(lhs_scr[hh], u_gt)

        def weights(hh, r_d):
            for c in range(blk // ch):
                rows = pl.ds(c * ch, ch)
                causal = crow + c * ch > ccol
                cs_p = cs_scr[hh, pl.ds(c * ch, ch), :] + r_d[c]
                cs_d = cs_scr[hh, pl.ds(blk + c * ch, ch), :]
                lb = z_scr[hh, rows, :]
                a_p = jnp.where(has_prev, jnp.exp(lb[:, :blk] + cs_p), 0.0)
                a_d = jnp.where(causal, jnp.exp(lb[:, blk:] + cs_d), 0.0)
                a_scr[hh, rows, :] = _mx(jnp.concatenate([a_p, a_d], axis=1))

        def values(hh):
            vc = jnp.concatenate([v_ref[rows_p, heads[hh]], v_ref[rows_d, heads[hh]]], axis=0)
            return _dot(a_scr[hh], vc)

        def block(hh, j, r):
            start = pl.multiple_of(j * blk, blk)
            lb, lk = _sb_logits(qs[hh], k_ref[pl.ds(start, blk), heads[hh]], None)
            a = _sb_weights(lb, lk, r, u_gt, None)
            return _dot(_mx(a), v_ref[pl.ds(start, blk), heads[hh]]), r + _rowsum(lk)

        def more(n, r0, r1):
            return jnp.logical_and(n < i, jnp.max(jnp.maximum(r0, r1)) > -SB_SKIP)

        logits(0)
        logits(1)
        gates = [g_ref[:, ls] * _sigmoid(g_ref[:, ls]) for ls in heads]
        rd0, r0 = keep_parts(0)
        suffix_sums(0)
        rd1, r1 = keep_parts(1)
        suffix_sums(1)
        go = more(jnp.int32(1), r0, r1)
        weights(0, rd0)
        acc0 = values(0)
        weights(1, rd1)
        acc1 = values(1)

        def step(c):
            _, n, acc0, r0, acc1, r1 = c
            pv0, r0 = block(0, i - 1 - n, r0)
            pv1, r1 = block(1, i - 1 - n, r1)
            return more(n + 1, r0, r1), n + 1, acc0 + pv0, r0, acc1 + pv1, r1

        _, _, acc0, _, acc1, _ = lax.while_loop(lambda c: c[0], step, (go, jnp.int32(1), acc0, r0, acc1, r1))
        for ls, acc, gate in zip(heads, (acc0, acc1), gates):
            o_ref[:, ls] = acc
            y_ref[:, ls] = (acc * gate).astype(y_ref.dtype)

    qblk = pl.BlockSpec((blk, LANES), lambda p, i: (i, p))
    return pl.pallas_call(
        body, name="stickbreak_fwd", grid=(SB_HEADS // 2, nq),
        in_specs=[qblk,
                  pl.BlockSpec((s, LANES), lambda p, i: (0, 4 + p)),
                  pl.BlockSpec((s, LANES), lambda p, i: (0, 8 + p)),
                  qblk],
        out_specs=[qblk, qblk],
        out_shape=[jax.ShapeDtypeStruct((s, 512), MXU_DTYPE),
                   jax.ShapeDtypeStruct((s, 512), F32)],
        scratch_shapes=[pltpu.VMEM((2, blk, 2 * blk), F32),
                        pltpu.VMEM((2, 2 * blk, blk), MXU_DTYPE),
                        pltpu.VMEM((2, 2 * blk, blk), F32),
                        pltpu.VMEM((2, blk, 2 * blk), MXU_DTYPE)],
        compiler_params=_params(("arbitrary", "arbitrary")),
    )(qkv, qkv, qkv, sg)


def _sb_bwd(qkv, sg, o, dy):
    s = qkv.shape[0]
    blk = min(SB_BLOCK, s)
    nq = s // blk
    hd = SB_HEAD_DIM
    assert nq <= LANES
    ch = min(SB_CHUNK, blk)

    def body(q_ref, k_ref, v_ref, g_ref, o_ref, dy_ref, dq_ref, dk_ref, dv_ref, dg_ref, dk_scr, dv_scr,
             z_scr, g_scr, lhs_scr, cs_scr, a_scr, dz_scr):
        i = pl.program_id(1)

        @pl.when(i == 0)
        def _():
            dk_scr[...] = jnp.zeros_like(dk_scr)
            dv_scr[...] = jnp.zeros_like(dv_scr)

        row = lax.broadcasted_iota(jnp.int32, (blk, blk), 0)
        col = lax.broadcasted_iota(jnp.int32, (blk, blk), 1)
        lane = lax.broadcasted_iota(jnp.int32, (blk, LANES), 1)
        u_gt = (row > col).astype(MXU_DTYPE)
        u_lt = (row < col).astype(MXU_DTYPE)
        heads = [slice(hh * hd, (hh + 1) * hd) for hh in range(2)]
        qs = [q_ref[:, ls] for ls in heads]
        _, rows_p, rows_d = _sb_pair_tile(i, blk)
        has_prev = i >= 1
        crow = lax.broadcasted_iota(jnp.int32, (ch, blk), 0)
        ccol = lax.broadcasted_iota(jnp.int32, (ch, blk), 1)
        nch = blk // ch
        kcs = [jnp.concatenate([k_ref[rows_p, ls], k_ref[rows_d, ls]], axis=0) for ls in heads]
        dobs = []

        def gate_grads(hh):
            ls = heads[hh]
            g = g_ref[:, ls]
            dyv = dy_ref[:, ls]
            sgm = _sigmoid(g)
            dg_ref[:, ls] = (dyv * o_ref[:, ls] * (sgm * (1.0 + g * (1.0 - sgm)))).astype(dg_ref.dtype)
            dobs.append(_mx(dyv * (g * sgm)))

        def split_rows(hh, c, part, x):
            lhs_scr[hh, pl.ds(part * blk + c * ch, ch), :] = _mx(x)

        def summed_rows(hh, c, part):
            return cs_scr[hh, pl.ds(part * blk + c * ch, ch), :]

        def logits(hh):
            z_scr[hh] = _dot_nt(qs[hh], kcs[hh])

        def weight_grads(hh):
            vc = jnp.concatenate([v_ref[rows_p, heads[hh]], v_ref[rows_d, heads[hh]]], axis=0)
            g_scr[hh] = _dot_nt(dobs[hh], vc)

        def keep_parts(hh):
            r_d, r_all = [], []
            for c in range(nch):
                rows = pl.ds(c * ch, ch)
                z = z_scr[hh, rows, :]
                l1p = jnp.log(1.0 + jnp.exp(-jnp.abs(z)))
                lk = -(jnp.maximum(z, 0.0) + l1p)
                z_scr[hh, rows, :] = jnp.minimum(z, 0.0) - l1p
                lk_p = jnp.where(has_prev, lk[:, :blk], 0.0)
                lk_d = jnp.where(crow + c * ch > ccol, lk[:, blk:], 0.0)
                split_rows(hh, c, 0, lk_p)
                split_rows(hh, c, 1, lk_d)
                r_d.append(_rowsum(lk_d))
                r_all.append(r_d[c] + _rowsum(lk_p))
            return r_d, jnp.concatenate(r_all, axis=0)

        def weights(hh, r_d):
            g_p = []
            for c in range(nch):
                rows = pl.ds(c * ch, ch)
                lb = z_scr[hh, rows, :]
                a_p = jnp.where(has_prev, jnp.exp(lb[:, :blk] + (summed_rows(hh, c, 0) + r_d[c])), 0.0)
                a_d = jnp.where(crow + c * ch > ccol, jnp.exp(lb[:, blk:] + summed_rows(hh, c, 1)), 0.0)
                a = jnp.concatenate([a_p, a_d], axis=1)
                a_scr[hh, rows, :] = _mx(a)
                gm = g_scr[hh, rows, :] * a
                g_scr[hh, rows, :] = gm
                split_rows(hh, c, 0, gm[:, :blk])
                split_rows(hh, c, 1, gm[:, blk:])
                g_p.append(_rowsum(gm[:, :blk]))
            return g_p

        def logit_grads(hh, pg, g_p):
            for c in range(nch):
                rows = pl.ds(c * ch, ch)
                pre = jnp.concatenate([summed_rows(hh, c, 0) + pg[c * ch:(c + 1) * ch],
                                       summed_rows(hh, c, 1) + (pg[c * ch:(c + 1) * ch] + g_p[c])], axis=1)
                gm = g_scr[hh, rows, :]
                dz = gm - (gm + pre) * jnp.exp(z_scr[hh, rows, :])
                dz_p = jnp.where(has_prev, dz[:, :blk], 0.0)
                dz_d = jnp.where(crow + c * ch > ccol, dz[:, blk:], 0.0)
                dz_scr[hh, rows, :] = _mx(jnp.concatenate([dz_p, dz_d], axis=1))

        def products(hh, acc):
            ls = heads[hh]
            dk_scr[hh, rows_p, :] += _dot_tn(dz_scr[hh, :, 0:blk], qs[hh])
            dk_scr[hh, rows_d, :] += _dot_tn(dz_scr[hh, :, blk:2 * blk], qs[hh])
            dv_scr[hh, rows_p, :] += _dot_tn(a_scr[hh, :, 0:blk], dobs[hh])
            dv_scr[hh, rows_d, :] += _dot_tn(a_scr[hh, :, blk:2 * blk], dobs[hh])
            dq_ref[:, ls] = ((acc + _dot(dz_scr[hh], kcs[hh])) * SB_SCALE).astype(dq_ref.dtype)

        def suffix_sums(hh):
            cs_scr[hh] = _dot(lhs_scr[hh], u_gt)

        def prefix_sums(hh):
            cs_scr[hh] = _dot(lhs_scr[hh], u_lt)

        def more(n, r0, r1):
            return jnp.logical_and(n < i, jnp.max(jnp.maximum(r0, r1)) > -SB_SKIP)

        gate_grads(0)
        gate_grads(1)
        logits(0)
        weight_grads(0)
        logits(1)
        weight_grads(1)
        rd0, ra0 = keep_parts(0)
        suffix_sums(0)
        rd1, ra1 = keep_parts(1)
        suffix_sums(1)
        go = more(jnp.int32(1), ra0, ra1)
        gp0 = weights(0, rd0)
        prefix_sums(0)
        gp1 = weights(1, rd1)
        prefix_sums(1)

        def scan_block(hh, j, r, rmat):
            start = pl.multiple_of(j * blk, blk)
            _, lk = _sb_logits(qs[hh], k_ref[pl.ds(start, blk), heads[hh]], None)
            return r + _rowsum(lk), jnp.where(lane == j, r, rmat)

        def scan_step(c):
            _, n, r0, rmat0, r1, rmat1 = c
            r0, rmat0 = scan_block(0, i - 1 - n, r0, rmat0)
            r1, rmat1 = scan_block(1, i - 1 - n, r1, rmat1)
            return more(n + 1, r0, r1), n + 1, r0, rmat0, r1, rmat1

        zmat = jnp.zeros((blk, LANES), F32)
        _, n, _, rmat0, _, rmat1 = lax.while_loop(lambda c: c[0], scan_step,
                                                  (go, jnp.int32(1), ra0, zmat, ra1, zmat))
        rmats = (rmat0, rmat1)

        def block(hh, j, pg):
            ls = heads[hh]
            start = pl.multiple_of(j * blk, blk)
            k = k_ref[pl.ds(start, blk), ls]
            lb, lk = _sb_logits(qs[hh], k, None)
            r = _rowsum(jnp.where(lane == j, rmats[hh], 0.0))
            a = _sb_weights(lb, lk, r, u_gt, None)
            gm = _dot_nt(dobs[hh], v_ref[pl.ds(start, blk), ls]) * a
            dzb = _mx(gm - (gm + (pg + _split_dot(gm, u_lt))) * jnp.exp(lb))
            dk_scr[hh, pl.ds(start, blk), :] += _dot_tn(dzb, qs[hh])
            dv_scr[hh, pl.ds(start, blk), :] += _dot_tn(_mx(a), dobs[hh])
            return _dot(dzb, k), pg + _rowsum(gm)

        def step(t, c):
            acc0, pg0, acc1, pg1 = c
            dq0, pg0 = block(0, i - n + t, pg0)
            dq1, pg1 = block(1, i - n + t, pg1)
            return acc0 + dq0, pg0, acc1 + dq1, pg1

        zero = jnp.zeros((blk, 1), F32)
        zacc = jnp.zeros((blk, hd), F32)
        acc0, pg0, acc1, pg1 = lax.fori_loop(0, n - 1, step, (zacc, zero, zacc, zero))
        logit_grads(0, pg0, gp0)
        logit_grads(1, pg1, gp1)
        products(0, acc0)
        products(1, acc1)

        @pl.when(i == nq - 1)
        def _():
            for hh in range(2):
                ls = slice(hh * hd, (hh + 1) * hd)
                dk_ref[:, ls] = dk_scr[hh].astype(dk_ref.dtype)
                dv_ref[:, ls] = dv_scr[hh].astype(dv_ref.dtype)

    qblk = lambda c0: pl.BlockSpec((blk, LANES), lambda p, i: (i, c0 + p))
    full = lambda c0: pl.BlockSpec((s, LANES), lambda p, i: (0, c0 + p))
    half = jax.ShapeDtypeStruct((s, 512), MXU_DTYPE)
    return pl.pallas_call(
        body, name="stickbreak_bwd", grid=(SB_HEADS // 2, nq),
        in_specs=[qblk(0), full(4), full(8), qblk(0), qblk(0), qblk(4)],
        out_specs=[qblk(0), full(0), full(0), qblk(0)],
        out_shape=[half, half, half, half],
        scratch_shapes=[pltpu.VMEM((2, s, hd), F32), pltpu.VMEM((2, s, hd), F32),
                        pltpu.VMEM((2, blk, 2 * blk), F32),
                        pltpu.VMEM((2, blk, 2 * blk), F32),
                        pltpu.VMEM((2, 2 * blk, blk), MXU_DTYPE),
                        pltpu.VMEM((2, 2 * blk, blk), F32),
                        pltpu.VMEM((2, blk, 2 * blk), MXU_DTYPE),
                        pltpu.VMEM((2, blk, 2 * blk), MXU_DTYPE)],
        compiler_params=_params(("arbitrary", "arbitrary")),
    )(qkv, qkv, qkv, sg, o, dy)


def _layer_fwd(layer, x, mod, norm_g, w_in_g, w_out_g, tables, gather=()):
    shift, scale1p, gate = mod[0:1], 1.0 + mod[1:2], mod[2:3]
    ret, qkv, sg, *gathered = _ln_proj(x, shift, scale1p, norm_g, w_in_g, layer, gather)
    y_r, states = _retention_fwd(ret, tables)
    y_s, o_s = _sb_fwd(qkv, sg)
    x_next = _out_proj(x, gate, y_r, y_s, w_out_g, layer)
    saved = (x, shift, scale1p, gate, ret, qkv, sg, y_r, states, y_s, o_s)
    return x_next, saved, gathered


def _layer_bwd(layer, me, dx_out, saved, norm_g, w_in_g, w_out_g, tables):
    x, shift, scale1p, gate, ret, qkv, sg, y_r, states, y_s, o_s = saved
    dy, dw_out, dgate = _out_proj_bwd(dx_out, gate, y_r, y_s, w_out_g, layer)
    d_ret = _retention_bwd(ret, states, dy, tables)
    d_q, d_k, d_v, d_g = _sb_bwd(qkv, sg, o_s, dy)
    dproj = (d_ret, d_q, d_k, d_v, d_g)
    dwo_parts = _mx(dw_out.reshape(N_DEV, D_MODEL // N_DEV, D_MODEL))
    dx, dshift, dscale, dnorm_g, h_t, r_out = _in_proj_bwd_x(
        x, dx_out, dproj, shift, scale1p, norm_g, w_in_g, layer, dwo_parts)
    r_in = _in_proj_bwd_w(me, h_t, dproj)
    dmod = jnp.concatenate([dshift, dscale, dgate], axis=1)
    return dx, r_in, r_out, dmod, dnorm_g


MESH_IDS = pl.DeviceIdType.MESH
N_PEERS = N_DEV - 1
HBM_SPEC = pl.BlockSpec(memory_space=pl.ANY)


def _my_place():
    return lax.axis_index("x"), lax.axis_index("y"), lax.axis_index("c")


def _linear(px, py, pc):
    return 4 * px + 2 * py + pc


def _all_gather(blocks):
    n_arr = len(blocks)

    def body(*refs):
        start, forward, finish = _gather_plan(refs[:n_arr], refs[n_arr:2 * n_arr], *refs[2 * n_arr:])
        start()
        forward()
        finish()

    return pl.pallas_call(
        body, name="all_gather",
        out_shape=_gathered_shapes(blocks),
        in_specs=[HBM_SPEC] * n_arr, out_specs=[HBM_SPEC] * n_arr,
        scratch_shapes=_gather_sems(n_arr),
    )(*blocks)


def _gathered_shapes(blocks):
    return [jax.ShapeDtypeStruct((N_DEV * b.shape[0], b.shape[1]), b.dtype) for b in blocks]


def _gather_sems(n_arr):
    return [pltpu.SemaphoreType.DMA((n_arr * N_PEERS,)), pltpu.SemaphoreType.DMA((n_arr * N_PEERS,)),
            pltpu.SemaphoreType.DMA((n_arr,))]


def _gather_plan(x_refs, out_refs, send_sems, recv_sems, local_sems):
    n_arr = len(x_refs)
    x, y, c = _my_place()
    me, sibling = (x, y, c), (x, y, 1 - c)
    chips = [(1 - x, y), (x, 1 - y), (1 - x, 1 - y)]

    def rows(a, place):
        m = x_refs[a].shape[0]
        return out_refs[a].at[pl.ds(_linear(*place) * m, m), :]

    def copy(a, k, block, to, src=None):
        return pltpu.make_async_remote_copy(
            src_ref=rows(a, block) if src is None else src, dst_ref=rows(a, block),
            send_sem=send_sems.at[a * N_PEERS + k], recv_sem=recv_sems.at[a * N_PEERS + k],
            device_id=to, device_id_type=MESH_IDS)

    mine = [pltpu.make_async_copy(x_refs[a], rows(a, me), local_sems.at[a]) for a in range(n_arr)]
    first = []
    for a in range(n_arr):
        first.append(copy(a, 0, me, sibling, src=x_refs[a]))
        first += [copy(a, 1 + j, me, (*chip, c), src=x_refs[a]) for j, chip in enumerate(chips)]
    passed = [copy(a, 4 + j, (*chip, c), sibling) for j, chip in enumerate(chips) for a in range(n_arr)]

    def start():
        for cp in mine + first:
            cp.start()

    def forward():
        for j, chip in enumerate(chips):
            for a in range(n_arr):
                copy(a, 1 + j, (*chip, c), me).wait_recv()
                passed[j * n_arr + a].start()

    def finish():
        for a in range(n_arr):
            copy(a, 0, sibling, me).wait_recv()
            for j, chip in enumerate(chips):
                copy(a, 4 + j, (*chip, 1 - c), me).wait_recv()
        for cp in first + passed:
            cp.wait_send()
        for cp in mine:
            cp.wait()

    return start, forward, finish


def _ada_fwd(c_all, w_ada, b_cols):
    cols = w_ada.shape[2]

    def body(c_ref, w_ref, b_ref, ca_ref, mod_ref):
        cv = c_ref[...]
        ca = cv * _sigmoid(cv)
        ca_ref[...] = ca
        cb = _mx(jnp.concatenate([ca, ca], axis=0))
        for l in range(DEPTH):
            mod_ref[l * N_DEV:(l + 1) * N_DEV, :] = _dot(cb, _mx(w_ref[l]))[0:N_DEV] + b_ref[l]

    return pl.pallas_call(
        body, name="ada_fwd",
        out_shape=[jax.ShapeDtypeStruct((N_DEV, D_MODEL), F32),
                   jax.ShapeDtypeStruct((DEPTH * N_DEV, cols), F32)],
        compiler_params=_params(),
    )(c_all, w_ada, b_cols)


def _ada_bwd(c_act_t, dmod_cols):
    cols = dmod_cols.shape[2]

    def body(ca_ref, dm_ref, o_ref):
        ca = _mx(ca_ref[...]).astype(F32)
        for l in range(DEPTH):
            o_ref[l] = jnp.dot(ca, _mx(dm_ref[l]).astype(F32),
                               precision=lax.Precision.HIGHEST, preferred_element_type=F32)

    return pl.pallas_call(
        body, name="ada_bwd",
        out_shape=jax.ShapeDtypeStruct((DEPTH, D_MODEL, cols), F32),
        compiler_params=_params(),
    )(c_act_t, dmod_cols)


def _adamw_store(g, w_ref, m_ref, v_ref, g_ref, d_ref, mo_ref, vo_ref):
    m2 = ADAM_B1 * m_ref[...] + (1.0 - ADAM_B1) * g
    v2 = ADAM_B2 * v_ref[...] + (1.0 - ADAM_B2) * (g * g)
    m_hat = m2 / (1.0 - ADAM_B1 ** ADAM_STEP)
    v_hat = v2 / (1.0 - ADAM_B2 ** ADAM_STEP)
    g_ref[...] = g
    d_ref[...] = -ADAM_LR * (m_hat / (jnp.sqrt(v_hat) + ADAM_EPS) + ADAM_WD * w_ref[...])
    mo_ref[...] = m2
    vo_ref[...] = v2


def _slab_sum(p_ref):
    g = p_ref[0].astype(F32)
    for sl in range(1, p_ref.shape[0]):
        g = g + p_ref[sl].astype(F32)
    return g


def _sum_adamw_layers(parts, w, m, v):
    n_slab, rows, cols = parts[0].shape
    tr = min(256, rows)
    nt = rows // tr

    def body(p0_ref, p1_ref, w_ref, m_ref, v_ref, g_ref, d_ref, mo_ref, vo_ref):
        for l, p_ref in enumerate((p0_ref, p1_ref)):
            @pl.when(pl.program_id(0) == l)
            def _():
                _adamw_store(_slab_sum(p_ref), w_ref, m_ref, v_ref, g_ref, d_ref, mo_ref, vo_ref)

    p_specs = [pl.BlockSpec((n_slab, tr, cols), lambda l, i: (0, i * (1 - l) + (nt - 1) * l, 0)),
               pl.BlockSpec((n_slab, tr, cols), lambda l, i: (0, i * l, 0))]
    blk = pl.BlockSpec((None, tr, cols), lambda l, i: (l, i, 0))
    shp = jax.ShapeDtypeStruct((DEPTH, rows, cols), F32)
    return pl.pallas_call(
        body, name="sum_adamw_layers", grid=(DEPTH, nt),
        in_specs=p_specs + [blk, blk, blk],
        out_specs=[blk, blk, blk, blk],
        out_shape=[shp, shp, shp, shp],
        compiler_params=_params(("arbitrary", "arbitrary")),
    )(parts[0], parts[1], w, m, v)


def _sum_adamw(parts, w, m, v):
    n_slab, rows, cols = parts.shape
    tr = min(256, rows)

    def body(p_ref, w_ref, m_ref, v_ref, g_ref, d_ref, mo_ref, vo_ref):
        _adamw_store(_slab_sum(p_ref), w_ref, m_ref, v_ref, g_ref, d_ref, mo_ref, vo_ref)

    blk = pl.BlockSpec((tr, cols), lambda i: (i, 0))
    shp = jax.ShapeDtypeStruct((rows, cols), F32)
    return pl.pallas_call(
        body, name="sum_adamw", grid=(rows // tr,),
        in_specs=[pl.BlockSpec((n_slab, tr, cols), lambda i: (0, i, 0)), blk, blk, blk],
        out_specs=[blk, blk, blk, blk],
        out_shape=[shp, shp, shp, shp],
        compiler_params=_params(("arbitrary",)),
    )(parts, w, m, v)


SMALL_ROWS = 16


def kernel(x, c, norm_g, w_ada, b_ada, w_in, w_out, final_g, loss_target, m_norm_g, m_w_ada, m_b_ada, m_w_in, m_w_out, m_final_g, v_norm_g, v_w_ada, v_b_ada, v_w_in, v_w_out, v_final_g):
    me = _linear(*_my_place())
    in_cols = w_in.shape[2]
    out_rows = w_out.shape[1]
    ada_cols = w_ada.shape[2]

    w_in_m, w_out_m = _mx(w_in), _mx(w_out)
    g_in, g_out, g_c = _all_gather([w_in_m[0], w_out_m[0], jnp.broadcast_to(c, (8, D_MODEL))])
    w_in_g = [g_in.reshape(N_DEV, 1, D_MODEL, in_cols), None]
    w_out_g = [g_out.reshape(N_DEV, 1, out_rows, D_MODEL), None]
    c_all = g_c.reshape(N_DEV, 8, D_MODEL)[:, 0]

    b_cols = lax.dynamic_slice_in_dim(b_ada, me * ada_cols, ada_cols, axis=1)[:, None, :]
    c_act, mod_cols = _ada_fwd(c_all, w_ada, b_cols)
    (g_mod,) = _all_gather([mod_cols])
    g_mod = g_mod.reshape(N_DEV, DEPTH, N_DEV, ada_cols)
    mod = lax.dynamic_index_in_dim(g_mod, me, axis=2, keepdims=False)
    mod = mod.transpose(1, 0, 2).reshape(DEPTH, 3, D_MODEL)

    tables = _ret_tables(x.shape[1])
    h = x[0]
    saved = []
    for l in range(DEPTH):
        nxt = (w_in_m[l + 1], w_out_m[l + 1]) if l + 1 < DEPTH else ()
        h, sv, gathered = _layer_fwd(0, h, mod[l], norm_g[l:l + 1], w_in_g[l], w_out_g[l], tables, nxt)
        if nxt:
            w_in_g[l + 1] = gathered[0].reshape(N_DEV, 1, D_MODEL, in_cols)
            w_out_g[l + 1] = gathered[1].reshape(N_DEV, 1, out_rows, D_MODEL)
        saved.append(sv)
    dx, loss_part, dfg = _final_loss(h, final_g[None], loss_target[0])
    r_in, r_out, small = [None] * DEPTH, [None] * DEPTH, [None] * DEPTH
    for l in reversed(range(DEPTH)):
        dx, r_in[l], r_out[l], dmod, dng = _layer_bwd(0, me, dx, saved[l], norm_g[l:l + 1], w_in_g[l], w_out_g[l], tables)
        small[l] = (dmod.reshape(3, D_MODEL), dng)

    pad = jnp.zeros((SMALL_ROWS - 10, D_MODEL), F32)
    small_block = jnp.concatenate([small[0][0], small[1][0], small[0][1], small[1][1], dfg,
                                   jnp.broadcast_to(loss_part, (1, D_MODEL)), pad], axis=0)
    (g_small,) = _all_gather([small_block])
    g_small = g_small.reshape(N_DEV, SMALL_ROWS, D_MODEL)

    def small_pack(b, n, f, fill):
        return jnp.concatenate([b.reshape(6, D_MODEL), n, f[None],
                                jnp.full((SMALL_ROWS - 9, D_MODEL), fill, F32)], axis=0)

    s_g, s_d, s_m, s_v = _sum_adamw(g_small, small_pack(b_ada, norm_g, final_g, 0.0),
                                    small_pack(m_b_ada, m_norm_g, m_final_g, 0.0),
                                    small_pack(v_b_ada, v_norm_g, v_final_g, 1.0))
    loss = s_g[9, 0]

    def small_unpack(a):
        return a[0:6].reshape(DEPTH, 3 * D_MODEL), a[6:8], a[8]

    dmod_all = g_small[:, 0:6].reshape(N_DEV, DEPTH, 3 * D_MODEL).transpose(1, 0, 2)
    dmod_cols = lax.dynamic_slice_in_dim(dmod_all, me * ada_cols, ada_cols, axis=2)
    g_ada = _ada_bwd(c_act.T, dmod_cols).reshape(1, DEPTH * D_MODEL, ada_cols)
    ada = _sum_adamw(g_ada, *[a.reshape(DEPTH * D_MODEL, ada_cols) for a in (w_ada, m_w_ada, v_w_ada)])
    ada = [a.reshape(DEPTH, D_MODEL, ada_cols) for a in ada]

    win = _sum_adamw_layers(r_in, w_in, m_w_in, v_w_in)
    wout = _sum_adamw_layers(r_out, w_out, m_w_out, v_w_out)

    outs = [loss, dx[None]]
    for k in range(4):
        b, n, f = small_unpack((s_g, s_d, s_m, s_v)[k])
        outs += [n, ada[k], b, win[k], wout[k], f]
    return tuple(outs)
```

```python
import functools

import jax
import jax.numpy as jnp
from jax import lax
from jax.experimental import pallas as pl
from jax.experimental.pallas import tpu as pltpu

F32 = jnp.float32
MXU_DTYPE = jnp.bfloat16

D_MODEL = 1024
DEPTH = 2
N_DEV = 8
CHUNK = 64
D_RET = 512
D_SB = 512
RET_HEADS = 4
RET_HEAD_DIM = 128
SB_HEADS = 8
SB_HEAD_DIM = 64
D_IN = 4096
ROPE_BASE = 10000.0
EPS = 1e-6
SB_SCALE = SB_HEAD_DIM ** -0.5
RET_KSCALE = RET_HEAD_DIM ** -0.5

ADAM_LR = 0.001
ADAM_B1 = 0.9
ADAM_B2 = 0.999
ADAM_EPS = 1e-08
ADAM_WD = 0.01
ADAM_STEP = 10

V7X_VMEM_BYTES = 64 * 2 ** 20
VMEM_LIMIT = V7X_VMEM_BYTES - 8 * 2 ** 20
LANES = 128

_NT = (((1,), (1,)), ((), ()))
_TN = (((0,), (0,)), ((), ()))


def _dot(a, b):
    return jnp.dot(a, b, preferred_element_type=F32)


def _dot_nt(a, b):
    return lax.dot_general(a, b, _NT, preferred_element_type=F32)


def _dot_tn(a, b):
    return lax.dot_general(a, b, _TN, preferred_element_type=F32)


def _mx(x):
    return x.astype(MXU_DTYPE)


def _sigmoid(x):
    return 1.0 / (1.0 + jnp.exp(-x))


def _params(sem=None):
    return pltpu.CompilerParams(dimension_semantics=sem, vmem_limit_bytes=VMEM_LIMIT)


def _row_tile(s):
    return min(512, s)


def _w_in_spec(w_in_g, layer):
    return pl.BlockSpec((N_DEV, None) + w_in_g.shape[2:], lambda i: (0, layer, 0, 0))


def _w_out_spec(w_out_g, layer):
    return pl.BlockSpec((N_DEV, None) + w_out_g.shape[2:], lambda i: (0, layer, 0, 0))


def _ln_proj(x, shift, scale1p, g, w_in_g, layer, gather=()):
    s = x.shape[0]
    ts = _row_tile(s)
    ns = s // ts
    n_g = len(gather)

    def body(x_ref, sh_ref, sc_ref, g_ref, w_ref, *rest):
        ret_ref, qkv_ref, sg_ref = rest[n_g:n_g + 3]
        if n_g:
            start, forward, finish = _gather_plan(rest[:n_g], rest[n_g + 3:2 * n_g + 3], *rest[2 * n_g + 3:])
            i = pl.program_id(0)
            pl.when(i == 0)(start)
            pl.when(i == max(ns - 4, 0))(forward)
        xv = x_ref[...]
        rstd = lax.rsqrt(jnp.mean(xv * xv, axis=-1, keepdims=True) + EPS)
        h = (xv * rstd * g_ref[...]) * sc_ref[...] + sh_ref[...]
        hb = _mx(h)
        for n in range(4):
            ret_ref[:, n * 512:(n + 1) * 512] = _dot(hb, w_ref[n])
        qkv_ref[:, 0:512] = _mx(_dot(hb, w_ref[4]) * SB_SCALE)
        qkv_ref[:, 512:1024] = _mx(_dot(hb, w_ref[5]))
        qkv_ref[:, 1024:1536] = _mx(_dot(hb, w_ref[6]))
        sg_ref[...] = _dot(hb, w_ref[7])
        if n_g:
            pl.when(i == ns - 1)(finish)

    vec = pl.BlockSpec((1, D_MODEL), lambda i: (0, 0))
    return pl.pallas_call(
        body, name="ln_proj_gather" if n_g else "ln_proj", grid=(ns,),
        in_specs=[pl.BlockSpec((ts, D_MODEL), lambda i: (i, 0)), vec, vec, vec,
                  _w_in_spec(w_in_g, layer)] + [HBM_SPEC] * n_g,
        out_specs=[pl.BlockSpec((ts, 2048), lambda i: (i, 0)),
                   pl.BlockSpec((ts, 1536), lambda i: (i, 0)),
                   pl.BlockSpec((ts, 512), lambda i: (i, 0))] + [HBM_SPEC] * n_g,
        out_shape=[jax.ShapeDtypeStruct((s, 2048), F32),
                   jax.ShapeDtypeStruct((s, 1536), MXU_DTYPE),
                   jax.ShapeDtypeStruct((s, 512), F32)] + _gathered_shapes(gather),
        scratch_shapes=_gather_sems(n_g) if n_g else (),
        compiler_params=_params(("arbitrary",)),
    )(x, shift, scale1p, g, w_in_g, *gather)


def _w_out_halves(w_ref):
    half = N_DEV // 2
    return (w_ref[0:half].reshape(D_RET, D_MODEL), w_ref[half:N_DEV].reshape(D_SB, D_MODEL))


def _out_proj(x, gate, y_r, y_s, w_out_g, layer):
    s = x.shape[0]
    ts = _row_tile(s)

    def body(x_ref, gate_ref, yr_ref, ys_ref, w_ref, o_ref):
        w_r, w_s = _w_out_halves(w_ref)
        t = _dot(yr_ref[...], w_r) + _dot(ys_ref[...], w_s)
        o_ref[...] = x_ref[...] + gate_ref[...] * t

    return pl.pallas_call(
        body, name="out_proj", grid=(s // ts,),
        in_specs=[pl.BlockSpec((ts, D_MODEL), lambda i: (i, 0)),
                  pl.BlockSpec((1, D_MODEL), lambda i: (0, 0)),
                  pl.BlockSpec((ts, 512), lambda i: (i, 0)),
                  pl.BlockSpec((ts, 512), lambda i: (i, 0)),
                  _w_out_spec(w_out_g, layer)],
        out_specs=pl.BlockSpec((ts, D_MODEL), lambda i: (i, 0)),
        out_shape=jax.ShapeDtypeStruct((s, D_MODEL), F32),
        compiler_params=_params(("arbitrary",)),
    )(x, gate, y_r, y_s, w_out_g)


def _final_loss(x, fg, target):
    s = x.shape[0]
    ts = _row_tile(s)

    def body(x_ref, fg_ref, t_ref, dx_ref, loss_ref, dfg_ref):
        i = pl.program_id(0)

        @pl.when(i == 0)
        def _():
            loss_ref[...] = jnp.zeros_like(loss_ref)
            dfg_ref[...] = jnp.zeros_like(dfg_ref)

        xv = x_ref[...]
        fgv = fg_ref[...]
        rstd = lax.rsqrt(jnp.mean(xv * xv, axis=-1, keepdims=True) + EPS)
        xn = xv * rstd
        err = xn * fgv - t_ref[...]
        tok = jnp.mean(err * err, axis=-1, keepdims=True)
        loss_ref[...] += 0.5 * jnp.sum(tok, axis=0, keepdims=True)
        dy = err * (1.0 / D_MODEL)
        dfg_ref[...] += jnp.sum(dy * xn, axis=0, keepdims=True)
        dxn = dy * fgv
        dx_ref[...] = rstd * (dxn - xn * jnp.mean(dxn * xn, axis=-1, keepdims=True))

    return pl.pallas_call(
        body, name="final_loss", grid=(s // ts,),
        in_specs=[pl.BlockSpec((ts, D_MODEL), lambda i: (i, 0)),
                  pl.BlockSpec((1, D_MODEL), lambda i: (0, 0)),
                  pl.BlockSpec((ts, D_MODEL), lambda i: (i, 0))],
        out_specs=[pl.BlockSpec((ts, D_MODEL), lambda i: (i, 0)),
                   pl.BlockSpec((1, 1), lambda i: (0, 0)),
                   pl.BlockSpec((1, D_MODEL), lambda i: (0, 0))],
        out_shape=[jax.ShapeDtypeStruct((s, D_MODEL), F32),
                   jax.ShapeDtypeStruct((1, 1), F32),
                   jax.ShapeDtypeStruct((1, D_MODEL), F32)],
        compiler_params=_params(("arbitrary",)),
    )(x, fg, target)


def _out_proj_bwd(dx_out, gate, y_r, y_s, w_out_g, layer):
    s = dx_out.shape[0]
    ts = _row_tile(s)

    def body(dx_ref, gate_ref, yr_ref, ys_ref, w_ref, dy_ref, dw_ref, dgate_ref):
        i = pl.program_id(0)

        @pl.when(i == 0)
        def _():
            dw_ref[...] = jnp.zeros_like(dw_ref)
            dgate_ref[...] = jnp.zeros_like(dgate_ref)

        dxv = dx_ref[...]
        dt = _mx(dxv * gate_ref[...])
        yr = yr_ref[...]
        ys = ys_ref[...]
        w_r, w_s = _w_out_halves(w_ref)
        dy_ref[:, 0:512] = _dot_nt(dt, w_r)
        dy_ref[:, 512:1024] = _dot_nt(dt, w_s)
        dw_ref[0:512, :] += _dot_tn(yr, dt)
        dw_ref[512:1024, :] += _dot_tn(ys, dt)
        t = _dot(yr, w_r) + _dot(ys, w_s)
        dgate_ref[...] += jnp.sum(dxv * t, axis=0, keepdims=True)

    return pl.pallas_call(
        body, name="out_proj_bwd", grid=(s // ts,),
        in_specs=[pl.BlockSpec((ts, D_MODEL), lambda i: (i, 0)),
                  pl.BlockSpec((1, D_MODEL), lambda i: (0, 0)),
                  pl.BlockSpec((ts, 512), lambda i: (i, 0)),
                  pl.BlockSpec((ts, 512), lambda i: (i, 0)),
                  _w_out_spec(w_out_g, layer)],
        out_specs=[pl.BlockSpec((ts, D_MODEL), lambda i: (i, 0)),
                   pl.BlockSpec((D_MODEL, D_MODEL), lambda i: (0, 0)),
                   pl.BlockSpec((1, D_MODEL), lambda i: (0, 0))],
        out_shape=[jax.ShapeDtypeStruct((s, D_MODEL), F32),
                   jax.ShapeDtypeStruct((D_MODEL, D_MODEL), F32),
                   jax.ShapeDtypeStruct((1, D_MODEL), F32)],
        compiler_params=_params(("arbitrary",)),
    )(dx_out, gate, y_r, y_s, w_out_g)


def _scatter_plan(parts_ref, recv_ref, send_sems, recv_sems, local_sem):
    px, py, pc = _my_place()
    mine = _linear(px, py, pc)

    def copy(r):
        peer = (1 - px if r & 4 else px, 1 - py if r & 2 else py, 1 - pc if r & 1 else pc)
        return pltpu.make_async_remote_copy(
            src_ref=parts_ref.at[_linear(*peer)], dst_ref=recv_ref.at[mine],
            send_sem=send_sems.at[r - 1], recv_sem=recv_sems.at[r - 1],
            device_id=peer, device_id_type=MESH_IDS)

    own = pltpu.make_async_copy(parts_ref.at[mine], recv_ref.at[mine], local_sem.at[0])

    def start():
        own.start()
        for r in range(1, N_DEV):
            copy(r).start()

    def finish():
        for r in range(1, N_DEV):
            copy(r).wait_recv()
            copy(r).wait_send()
        own.wait()

    return start, finish


def _in_proj_bwd_x(x, dx_out, dproj, shift, scale1p, g, w_in_g, layer, dwo_parts):
    s = x.shape[0]
    ts = _row_tile(s)
    ns = s // ts
    nb = D_IN // N_DEV

    def body(x_ref, dxo_ref, dr_ref, d4_ref, d5_ref, d6_ref, d7_ref, sh_ref, sc_ref, g_ref, w_ref, dwo_ref,
             dx_ref, dsh_ref, dsc_ref, dg_ref, ht_ref, rout_ref, send_sems, recv_sems, local_sem):
        i = pl.program_id(0)
        start, finish = _scatter_plan(dwo_ref, rout_ref, send_sems, recv_sems, local_sem)

        @pl.when(i == 0)
        def _():
            start()
            dsh_ref[...] = jnp.zeros_like(dsh_ref)
            dsc_ref[...] = jnp.zeros_like(dsc_ref)
            dg_ref[...] = jnp.zeros_like(dg_ref)

        dh = _dot_nt(dr_ref[:, 0:nb], w_ref[0])
        for n in range(1, 4):
            dh += _dot_nt(dr_ref[:, n * nb:(n + 1) * nb], w_ref[n])
        for n, d_ref in zip(range(4, N_DEV), (d4_ref, d5_ref, d6_ref, d7_ref)):
            dh += _dot_nt(d_ref[...], w_ref[n])
        xv = x_ref[...]
        gv = g_ref[...]
        scv = sc_ref[...]
        rstd = lax.rsqrt(jnp.mean(xv * xv, axis=-1, keepdims=True) + EPS)
        xn = xv * rstd
        xg = xn * gv
        ht_ref[...] = _mx((xg * scv + sh_ref[...]).T)
        dsh_ref[...] += jnp.sum(dh, axis=0, keepdims=True)
        dsc_ref[...] += jnp.sum(dh * xg, axis=0, keepdims=True)
        dhs = dh * scv
        dg_ref[...] += jnp.sum(dhs * xn, axis=0, keepdims=True)
        dxn = dhs * gv
        dx_ref[...] = rstd * (dxn - xn * jnp.mean(dxn * xn, axis=-1, keepdims=True)) + dxo_ref[...]
        pl.when(i == ns - 1)(finish)

    vec = pl.BlockSpec((1, D_MODEL), lambda i: (0, 0))
    return pl.pallas_call(
        body, name="in_proj_bwd_x", grid=(ns,),
        in_specs=[pl.BlockSpec((ts, D_MODEL), lambda i: (i, 0)),
                  pl.BlockSpec((ts, D_MODEL), lambda i: (i, 0)),
                  pl.BlockSpec((ts, 4 * nb), lambda i: (i, 0))]
                 + [pl.BlockSpec((ts, nb), lambda i: (i, 0))] * 4
                 + [vec, vec, vec, _w_in_spec(w_in_g, layer), HBM_SPEC],
        out_specs=[pl.BlockSpec((ts, D_MODEL), lambda i: (i, 0)), vec, vec, vec,
                   pl.BlockSpec((D_MODEL, ts), lambda i: (0, i)), HBM_SPEC],
        out_shape=[jax.ShapeDtypeStruct((s, D_MODEL), F32),
                   jax.ShapeDtypeStruct((1, D_MODEL), F32),
                   jax.ShapeDtypeStruct((1, D_MODEL), F32),
                   jax.ShapeDtypeStruct((1, D_MODEL), F32),
                   jax.ShapeDtypeStruct((D_MODEL, s), MXU_DTYPE),
                   jax.ShapeDtypeStruct(dwo_parts.shape, dwo_parts.dtype)],
        scratch_shapes=[pltpu.SemaphoreType.DMA((N_PEERS,)), pltpu.SemaphoreType.DMA((N_PEERS,)),
                        pltpu.SemaphoreType.DMA((1,))],
        compiler_params=_params(("arbitrary",)),
    )(x, dx_out, *dproj, shift, scale1p, g, w_in_g, dwo_parts)


def _in_proj_bwd_w(me, h_t, dproj):
    s = h_t.shape[1]
    ts = min(4 * _row_tile(s), s)
    ns = s // ts
    nb = D_IN // N_DEV
    n_chip = N_DEV // 2

    def flip_bits(j):
        return jnp.where(j == 0, 4, jnp.where(j == 1, 2, jnp.where(j == 2, 6, 0)))

    def slab_of(t, me_ref):
        return jnp.bitwise_xor(me_ref[0], flip_bits(t // 2) + 1 - t % 2)

    def body(me_ref, ht_ref, dr_ref, d4_ref, d5_ref, d6_ref, d7_ref, rin_ref,
             acc, stage, pre_buf, pre_send, pre_recv, sum_send, sum_recv, local_sem):
        t = pl.program_id(0)
        i = pl.program_id(1)
        j = t // 2
        summing = t % 2 == 1
        slab = slab_of(t, me_ref)
        px, py, pc = _my_place()

        def pre_copy(jj):
            return pltpu.make_async_remote_copy(
                src_ref=stage.at[jj % 2], dst_ref=pre_buf.at[jj],
                send_sem=pre_send.at[jj], recv_sem=pre_recv.at[jj],
                device_id=(px, py, 1 - pc), device_id_type=MESH_IDS)

        def sum_copy(jj):
            fx = jnp.logical_or(jj == 0, jj == 2)
            fy = jnp.logical_or(jj == 1, jj == 2)
            return pltpu.make_async_remote_copy(
                src_ref=stage.at[2 + jj % 2], dst_ref=rin_ref.at[jj],
                send_sem=sum_send.at[jj], recv_sem=sum_recv.at[jj],
                device_id=(jnp.where(fx, 1 - px, px), jnp.where(fy, 1 - py, py), pc), device_id_type=MESH_IDS)

        own = pltpu.make_async_copy(stage.at[3], rin_ref.at[n_chip - 1], local_sem.at[0])

        @pl.when(i == 0)
        def _():
            acc[...] = jnp.zeros_like(acc)

        @pl.when(slab < 4)
        def _():
            acc[...] += _dot(ht_ref[...], dr_ref[...])

        for n, d_ref in zip(range(4, N_DEV), (d4_ref, d5_ref, d6_ref, d7_ref)):
            @pl.when(slab == n)
            def _():
                acc[...] += _dot(ht_ref[...], d_ref[...])

        @pl.when(jnp.logical_and(i == ns - 1, jnp.logical_not(summing)))
        def _():
            @pl.when(j >= 2)
            def _():
                pre_copy(j - 2).wait_send()

            stage[j % 2] = acc[...].astype(stage.dtype)
            pre_copy(j).start()

        @pl.when(jnp.logical_and(i == ns - 1, summing))
        def _():
            pre_copy(j).wait_recv()

            @pl.when(j >= 2)
            def _():
                sum_copy(j - 2).wait_send()

            stage[2 + j % 2] = (acc[...] + pre_buf[j].astype(F32)).astype(stage.dtype)

            @pl.when(j < n_chip - 1)
            def _():
                sum_copy(j).start()

            @pl.when(j == n_chip - 1)
            def _():
                own.start()
                pre_copy(n_chip - 2).wait_send()
                pre_copy(n_chip - 1).wait_send()
                sum_copy(n_chip - 2).wait_send()
                for jj in range(n_chip - 1):
                    sum_copy(jj).wait_recv()
                own.wait()

    def part_rows(n, t, i, me_ref):
        return jnp.where(slab_of(t, me_ref) == n, i, ns - 1), 0

    return pl.pallas_call(
        body, name="in_proj_bwd_w",
        grid_spec=pltpu.PrefetchScalarGridSpec(
            num_scalar_prefetch=1, grid=(N_DEV, ns),
            in_specs=[pl.BlockSpec((D_MODEL, ts), lambda t, i, me_ref: (0, i)),
                      pl.BlockSpec((ts, nb), lambda t, i, me_ref: (
                          jnp.where(slab_of(t, me_ref) < 4, i, ns - 1), jnp.minimum(slab_of(t, me_ref), 3)))]
                     + [pl.BlockSpec((ts, nb), functools.partial(part_rows, n)) for n in range(4, N_DEV)],
            out_specs=HBM_SPEC,
            scratch_shapes=[pltpu.VMEM((D_MODEL, nb), F32),
                            pltpu.VMEM((4, D_MODEL, nb), MXU_DTYPE),
                            pltpu.VMEM((n_chip, D_MODEL, nb), MXU_DTYPE),
                            pltpu.SemaphoreType.DMA((n_chip,)), pltpu.SemaphoreType.DMA((n_chip,)),
                            pltpu.SemaphoreType.DMA((n_chip - 1,)), pltpu.SemaphoreType.DMA((n_chip - 1,)),
                            pltpu.SemaphoreType.DMA((1,))]),
        out_shape=jax.ShapeDtypeStruct((n_chip, D_MODEL, nb), MXU_DTYPE),
        compiler_params=_params(("arbitrary", "arbitrary")),
    )(jnp.reshape(me, (1,)).astype(jnp.int32), h_t, *dproj)


RET_TILE = 256


def _ret_tables(s):
    t = min(RET_TILE, s)
    half = RET_HEAD_DIM // 2
    pos = jnp.arange(s, dtype=F32)
    inv = ROPE_BASE ** (-jnp.arange(half, dtype=F32) / half)
    ang = pos[:, None] * inv[None, :]
    cos, sin = jnp.cos(ang), jnp.sin(ang)
    cos2 = jnp.concatenate([cos, cos], axis=1)
    sin2 = jnp.concatenate([-sin, sin], axis=1)
    lg = jnp.log1p(-(2.0 ** (-5.0 - jnp.arange(RET_HEADS, dtype=F32))))[:, None, None]
    n = jnp.arange(t)
    dist = (n[:, None] - n[None, :]).astype(F32)[None]
    cn = (n // CHUNK)[:, None]
    cm = (n // CHUNK)[None, :]
    mask = jnp.where((cn == cm)[None], jnp.exp(jnp.abs(dist) * lg),
                     jnp.where((cm < cn)[None], jnp.exp(dist * lg), 0.0))
    nf = n.astype(F32)[None, :, None]
    dq = jnp.broadcast_to(jnp.exp((nf + 1.0) * lg), (RET_HEADS, t, LANES))
    dk = jnp.broadcast_to(jnp.exp((t - 1.0 - nf) * lg), (RET_HEADS, t, LANES))
    gt = jnp.broadcast_to(jnp.exp(float(t) * lg), (RET_HEADS, 1, LANES))
    return cos2, sin2, mask, dq, dk, gt


def _roll_half(x):
    return pltpu.roll(x, RET_HEAD_DIM // 2, 1)


def _ret_heads_fwd(ret_ref, cos, sin, m_ref, dq_ref, dk_ref, s0):
    hd = RET_HEAD_DIM
    heads = range(RET_HEADS)
    qb, kb, vb, kdb = [], [], [], []
    for h in heads:
        q = ret_ref[:, h * hd:(h + 1) * hd]
        k = ret_ref[:, 512 + h * hd:512 + (h + 1) * hd]
        kr = (k * cos + _roll_half(k) * sin) * RET_KSCALE
        qb.append(_mx(q * cos + _roll_half(q) * sin))
        kb.append(_mx(kr))
        kdb.append(_mx(kr * dk_ref[h]))
        vb.append(_mx(ret_ref[:, 1024 + h * hd:1024 + (h + 1) * hd]))
    p = [_dot_nt(qb[h], kb[h]) for h in heads]
    cross = [_dot(qb[h], _mx(s0[h])) for h in heads]
    pb = [_mx(p[h] * m_ref[h]) for h in heads]
    o = [_dot(pb[h], vb[h]) + cross[h] * dq_ref[h] for h in heads]
    gn, rstd = [], []
    for h in heads:
        oc = o[h] - jnp.mean(o[h], axis=-1, keepdims=True)
        rstd.append(lax.rsqrt(jnp.mean(oc * oc, axis=-1, keepdims=True) + EPS))
        gn.append(oc * rstd[h])
    return qb, kb, vb, pb, kdb, gn, rstd


def _retention_fwd(ret, tables):
    cos2, sin2, mask, dq, dk, gt = tables
    s = ret.shape[0]
    t = mask.shape[1]
    nt = s // t
    hd = RET_HEAD_DIM

    def body(ret_ref, cos_ref, sin_ref, m_ref, dq_ref, dk_ref, gt_ref, y_ref, st_ref, s_scr):
        i = pl.program_id(0)

        @pl.when(i == 0)
        def _():
            s_scr[...] = jnp.zeros_like(s_scr)

        s0 = [s_scr[h] for h in range(RET_HEADS)]
        _, _, vb, _, kdb, gn, _ = _ret_heads_fwd(ret_ref, cos_ref[...], sin_ref[...], m_ref, dq_ref, dk_ref, s0)
        kv = [_dot_tn(kdb[h], vb[h]) for h in range(RET_HEADS)]
        for h in range(RET_HEADS):
            g = ret_ref[:, 1536 + h * hd:1536 + (h + 1) * hd]
            st_ref[h] = s0[h]
            y_ref[:, h * hd:(h + 1) * hd] = (gn[h] * (g * _sigmoid(g))).astype(y_ref.dtype)
            s_scr[h] = s0[h] * gt_ref[h] + kv[h]

    full3 = lambda a: pl.BlockSpec(a.shape, lambda i: (0, 0, 0))
    return pl.pallas_call(
        body, name="retention_fwd", grid=(nt,),
        in_specs=[pl.BlockSpec((t, 2048), lambda i: (i, 0)),
                  pl.BlockSpec((t, LANES), lambda i: (i, 0)),
                  pl.BlockSpec((t, LANES), lambda i: (i, 0)),
                  full3(mask), full3(dq), full3(dk), full3(gt)],
        out_specs=[pl.BlockSpec((t, 512), lambda i: (i, 0)),
                   pl.BlockSpec((None, RET_HEADS, hd, hd), lambda i: (i, 0, 0, 0))],
        out_shape=[jax.ShapeDtypeStruct((s, 512), MXU_DTYPE),
                   jax.ShapeDtypeStruct((nt, RET_HEADS, hd, hd), F32)],
        scratch_shapes=[pltpu.VMEM((RET_HEADS, hd, hd), F32)],
        compiler_params=_params(("arbitrary",)),
    )(ret, cos2, sin2, mask, dq, dk, gt)


def _retention_bwd(ret, states, dy, tables):
    cos2, sin2, mask, dq, dk, gt = tables
    s = ret.shape[0]
    t = mask.shape[1]
    nt = s // t
    hd = RET_HEAD_DIM

    def body(ret_ref, st_ref, dy_ref, cos_ref, sin_ref, m_ref, dq_ref, dk_ref, gt_ref, d_ref, ds_scr):
        i = pl.program_id(0)

        @pl.when(i == 0)
        def _():
            ds_scr[...] = jnp.zeros_like(ds_scr)

        cos = cos_ref[...]
        sin = sin_ref[...]
        heads = range(RET_HEADS)
        s0 = [st_ref[h] for h in heads]
        ds = [ds_scr[h] for h in heads]
        dsb = [_mx(ds[h]) for h in heads]
        qb, kb, vb, pb, kdb, gn, rstd = _ret_heads_fwd(ret_ref, cos, sin, m_ref, dq_ref, dk_ref, s0)
        dob, dodb = [], []
        for h in heads:
            g = ret_ref[:, 1536 + h * hd:1536 + (h + 1) * hd]
            dyv = dy_ref[:, h * hd:(h + 1) * hd]
            sg = _sigmoid(g)
            d_ref[:, 1536 + h * hd:1536 + (h + 1) * hd] = (
                dyv * gn[h] * (sg * (1.0 + g * (1.0 - sg)))).astype(d_ref.dtype)
            dgn = dyv * (g * sg)
            do = rstd[h] * (dgn - jnp.mean(dgn, axis=-1, keepdims=True)
                            - gn[h] * jnp.mean(dgn * gn[h], axis=-1, keepdims=True))
            dob.append(_mx(do))
            dodb.append(_mx(do * dq_ref[h]))
        dp = [_dot_nt(dob[h], vb[h]) for h in heads]
        dv = [_dot_tn(pb[h], dob[h]) + _dot(kdb[h], dsb[h]) for h in heads]
        dq_cross = [_dot_nt(dodb[h], _mx(s0[h])) for h in heads]
        dk_cross = [_dot_nt(vb[h], dsb[h]) for h in heads]
        ds_new = [_dot_tn(qb[h], dodb[h]) for h in heads]
        dpb = [_mx(dp[h] * m_ref[h]) for h in heads]
        dqr = [_dot(dpb[h], kb[h]) + dq_cross[h] for h in heads]
        dkr = [(_dot_tn(dpb[h], qb[h]) + dk_cross[h] * dk_ref[h]) * RET_KSCALE for h in heads]
        for h in heads:
            d_ref[:, 1024 + h * hd:1024 + (h + 1) * hd] = dv[h].astype(d_ref.dtype)
            d_ref[:, h * hd:(h + 1) * hd] = (dqr[h] * cos + _roll_half(dqr[h] * sin)).astype(d_ref.dtype)
            d_ref[:, 512 + h * hd:512 + (h + 1) * hd] = (
                dkr[h] * cos + _roll_half(dkr[h] * sin)).astype(d_ref.dtype)
            ds_scr[h] = ds[h] * gt_ref[h] + ds_new[h]

    full3 = lambda a: pl.BlockSpec(a.shape, lambda i: (0, 0, 0))
    rev = lambda i: (nt - 1 - i, 0)
    return pl.pallas_call(
        body, name="retention_bwd", grid=(nt,),
        in_specs=[pl.BlockSpec((t, 2048), rev),
                  pl.BlockSpec((None, RET_HEADS, hd, hd), lambda i: (nt - 1 - i, 0, 0, 0)),
                  pl.BlockSpec((t, 512), rev),
                  pl.BlockSpec((t, LANES), rev),
                  pl.BlockSpec((t, LANES), rev),
                  full3(mask), full3(dq), full3(dk), full3(gt)],
        out_specs=pl.BlockSpec((t, 2048), rev),
        out_shape=jax.ShapeDtypeStruct((s, 2048), MXU_DTYPE),
        scratch_shapes=[pltpu.VMEM((RET_HEADS, hd, hd), F32)],
        compiler_params=_params(("arbitrary",)),
    )(ret, states, dy, cos2, sin2, mask, dq, dk, gt)


SB_BLOCK = 256
SB_CHUNK = 32


SB_SKIP = 104.0
NO_KEYS = -1e30


def _split_dots(xs, u):
    parts = []
    for x in xs:
        hi = lax.bitcast_convert_type(lax.bitcast_convert_type(x, jnp.uint32) & jnp.uint32(0xFFFF0000), F32)
        parts += [_mx(hi), _mx(x - hi)]
    out = _dot(jnp.concatenate(parts, axis=0), u)
    n = xs[0].shape[0]
    return [out[2 * k * n:(2 * k + 1) * n] + out[(2 * k + 1) * n:(2 * k + 2) * n] for k in range(len(xs))]


def _split_dot(x, u):
    return _split_dots([x], u)[0]


def _sb_pair_weights(lb, lk, allowed, u_gt):
    blk = lb.shape[0]
    lk_p, lk_d = lk[:, :blk], lk[:, blk:]
    r_d = _rowsum(lk_d)
    cs_p, cs_d = _split_dots([lk_p, lk_d], u_gt)
    a = jnp.exp(lb + jnp.concatenate([cs_p + r_d, cs_d], axis=1))
    return jnp.where(allowed, a, 0.0), r_d, r_d + _rowsum(lk_p)


def _sb_logits(q, k, causal):
    z = _dot_nt(q, k)
    l1p = jnp.log(1.0 + jnp.exp(-jnp.abs(z)))
    lk = -(jnp.maximum(z, 0.0) + l1p)
    if causal is not None:
        lk = jnp.where(causal, lk, 0.0)
    return jnp.minimum(z, 0.0) - l1p, lk


def _sb_weights(lb, lk, r, u_gt, causal):
    a = jnp.exp(lb + _split_dot(lk, u_gt) + r)
    return a if causal is None else jnp.where(causal, a, 0.0)


def _rowsum(x):
    return jnp.sum(x, axis=1, keepdims=True)


def _sb_pair_tile(i, blk):
    row = lax.broadcasted_iota(jnp.int32, (blk, 2 * blk), 0)
    col = lax.broadcasted_iota(jnp.int32, (blk, 2 * blk), 1)
    first_col = jnp.where(i >= 1, 0, blk)
    allowed = jnp.logical_and(row > col - blk, col >= first_col)
    rows_p = pl.ds(pl.multiple_of(jnp.maximum(i - 1, 0) * blk, blk), blk)
    rows_d = pl.ds(pl.multiple_of(i * blk, blk), blk)
    return allowed, rows_p, rows_d


def _sb_fwd(qkv, sg):
    s = qkv.shape[0]
    blk = min(SB_BLOCK, s)
    nq = s // blk
    hd = SB_HEAD_DIM

    ch = min(SB_CHUNK, blk)

    def body(q_ref, k_ref, v_ref, g_ref, y_ref, o_ref, z_scr, lhs_scr, cs_scr, a_scr):
        i = pl.program_id(1)
        row = lax.broadcasted_iota(jnp.int32, (blk, blk), 0)
        col = lax.broadcasted_iota(jnp.int32, (blk, blk), 1)
        u_gt = (row > col).astype(MXU_DTYPE)
        heads = [slice(hh * hd, (hh + 1) * hd) for hh in range(2)]
        qs = [q_ref[:, ls] for ls in heads]
        _, rows_p, rows_d = _sb_pair_tile(i, blk)
        has_prev = i >= 1
        crow = lax.broadcasted_iota(jnp.int32, (ch, blk), 0)
        ccol = lax.broadcasted_iota(jnp.int32, (ch, blk), 1)

        def logits(hh):
            kc = jnp.concatenate([k_ref[rows_p, heads[hh]], k_ref[rows_d, heads[hh]]], axis=0)
            z_scr[hh] = _dot_nt(qs[hh], kc)

        def keep_parts(hh):
            r_d, r_all = [], []
            for c in range(blk // ch):
                rows = pl.ds(c * ch, ch)
                causal = crow + c * ch > ccol
                z = z_scr[hh, rows, :]
                l1p = jnp.log(1.0 + jnp.exp(-jnp.abs(z)))
                lb = jnp.minimum(z, 0.0) - l1p
                z_scr[hh, rows, :] = lb
                lk = lb - z
                lk_p = lk[:, :blk]
                lk_d = jnp.where(causal, lk[:, blk:], 0.0)
                lhs_scr[hh, pl.ds(c * ch, ch), :] = _mx(lk_p)
                lhs_scr[hh, pl.ds(blk + c * ch, ch), :] = _mx(lk_d)
                r_d.append(_rowsum(lk_d))
                r_all.append(r_d[c] + _rowsum(lk_p))
            return r_d, jnp.concatenate(r_all, axis=0)

        def suffix_sums(hh):
            cs_scr[hh] = _dot(lhs_scr[hh], u_gt)

        def weights(hh, r_d):
            for c in range(blk // ch):
                rows = pl.ds(c * ch, ch)
                causal = crow + c * ch > ccol
                cs_p = cs_scr[hh, pl.ds(c * ch, ch), :] + jnp.where(has_prev, r_d[c], NO_KEYS)
                cs_d = cs_scr[hh, pl.ds(blk + c * ch, ch), :]
                lb = z_scr[hh, rows, :]
                a_p = jnp.exp(lb[:, :blk] + cs_p)
                a_d = jnp.where(causal, jnp.exp(lb[:, blk:] + cs_d), 0.0)
                a_scr[hh, rows, :] = _mx(jnp.concatenate([a_p, a_d], axis=1))

        def values(hh):
            vc = jnp.concatenate([v_ref[rows_p, heads[hh]], v_ref[rows_d, heads[hh]]], axis=0)
            return _dot(a_scr[hh], vc)

        def block(hh, j, r):
            start = pl.multiple_of(j * blk, blk)
            lb, lk = _sb_logits(qs[hh], k_ref[pl.ds(start, blk), heads[hh]], None)
            a = _sb_weights(lb, lk, r, u_gt, None)
            return _dot(_mx(a), v_ref[pl.ds(start, blk), heads[hh]]), r + _rowsum(lk)

        def more(n, r0, r1):
            return jnp.logical_and(n < i, jnp.max(jnp.maximum(r0, r1)) > -SB_SKIP)

        logits(0)
        logits(1)
        gates = [g_ref[:, ls] * _sigmoid(g_ref[:, ls]) for ls in heads]
        rd0, r0 = keep_parts(0)
        suffix_sums(0)
        rd1, r1 = keep_parts(1)
        suffix_sums(1)
        go = more(jnp.int32(1), r0, r1)
        weights(0, rd0)
        acc0 = values(0)
        weights(1, rd1)
        acc1 = values(1)

        def step(c):
            _, n, acc0, r0, acc1, r1 = c
            pv0, r0 = block(0, i - 1 - n, r0)
            pv1, r1 = block(1, i - 1 - n, r1)
            return more(n + 1, r0, r1), n + 1, acc0 + pv0, r0, acc1 + pv1, r1

        _, _, acc0, _, acc1, _ = lax.while_loop(lambda c: c[0], step, (go, jnp.int32(1), acc0, r0, acc1, r1))
        for ls, acc, gate in zip(heads, (acc0, acc1), gates):
            o_ref[:, ls] = acc
            y_ref[:, ls] = (acc * gate).astype(y_ref.dtype)

    qblk = pl.BlockSpec((blk, LANES), lambda p, i: (i, p))
    return pl.pallas_call(
        body, name="stickbreak_fwd", grid=(SB_HEADS // 2, nq),
        in_specs=[qblk,
                  pl.BlockSpec((s, LANES), lambda p, i: (0, 4 + p)),
                  pl.BlockSpec((s, LANES), lambda p, i: (0, 8 + p)),
                  qblk],
        out_specs=[qblk, qblk],
        out_shape=[jax.ShapeDtypeStruct((s, 512), MXU_DTYPE),
                   jax.ShapeDtypeStruct((s, 512), F32)],
        scratch_shapes=[pltpu.VMEM((2, blk, 2 * blk), F32),
                        pltpu.VMEM((2, 2 * blk, blk), MXU_DTYPE),
                        pltpu.VMEM((2, 2 * blk, blk), F32),
                        pltpu.VMEM((2, blk, 2 * blk), MXU_DTYPE)],
        compiler_params=_params(("arbitrary", "arbitrary")),
    )(qkv, qkv, qkv, sg)


def _sb_bwd(qkv, sg, o, dy):
    s = qkv.shape[0]
    blk = min(SB_BLOCK, s)
    nq = s // blk
    hd = SB_HEAD_DIM
    assert nq <= LANES
    ch = min(SB_CHUNK, blk)

    def body(q_ref, k_ref, v_ref, g_ref, o_ref, dy_ref, dq_ref, dk_ref, dv_ref, dg_ref, dk_scr, dv_scr,
             z_scr, g_scr, lhs_scr, cs_scr, a_scr, dz_scr):
        i = pl.program_id(1)

        @pl.when(i == 0)
        def _():
            dk_scr[...] = jnp.zeros_like(dk_scr)
            dv_scr[...] = jnp.zeros_like(dv_scr)

        row = lax.broadcasted_iota(jnp.int32, (blk, blk), 0)
        col = lax.broadcasted_iota(jnp.int32, (blk, blk), 1)
        lane = lax.broadcasted_iota(jnp.int32, (blk, LANES), 1)
        u_gt = (row > col).astype(MXU_DTYPE)
        u_lt = (row < col).astype(MXU_DTYPE)
        heads = [slice(hh * hd, (hh + 1) * hd) for hh in range(2)]
        qs = [q_ref[:, ls] for ls in heads]
        _, rows_p, rows_d = _sb_pair_tile(i, blk)
        has_prev = i >= 1
        crow = lax.broadcasted_iota(jnp.int32, (ch, blk), 0)
        ccol = lax.broadcasted_iota(jnp.int32, (ch, blk), 1)
        nch = blk // ch
        kcs = [jnp.concatenate([k_ref[rows_p, ls], k_ref[rows_d, ls]], axis=0) for ls in heads]
        dobs = []

        def gate_grads(hh):
            ls = heads[hh]
            g = g_ref[:, ls]
            dyv = dy_ref[:, ls]
            sgm = _sigmoid(g)
            dg_ref[:, ls] = (dyv * o_ref[:, ls] * (sgm * (1.0 + g * (1.0 - sgm)))).astype(dg_ref.dtype)
            dobs.append(_mx(dyv * (g * sgm)))

        def split_rows(hh, c, part, x):
            lhs_scr[hh, pl.ds(part * blk + c * ch, ch), :] = _mx(x)

        def summed_rows(hh, c, part):
            return cs_scr[hh, pl.ds(part * blk + c * ch, ch), :]

        def logits(hh):
            z_scr[hh] = _dot_nt(qs[hh], kcs[hh])

        def weight_grads(hh):
            vc = jnp.concatenate([v_ref[rows_p, heads[hh]], v_ref[rows_d, heads[hh]]], axis=0)
            g_scr[hh] = _dot_nt(dobs[hh], vc)

        def keep_parts(hh):
            r_d, r_all = [], []
            for c in range(nch):
                rows = pl.ds(c * ch, ch)
                z = z_scr[hh, rows, :]
                l1p = jnp.log(1.0 + jnp.exp(-jnp.abs(z)))
                lb = jnp.minimum(z, 0.0) - l1p
                z_scr[hh, rows, :] = lb
                lk = lb - z
                lk_p = lk[:, :blk]
                lk_d = jnp.where(crow + c * ch > ccol, lk[:, blk:], 0.0)
                split_rows(hh, c, 0, lk_p)
                split_rows(hh, c, 1, lk_d)
                r_d.append(_rowsum(lk_d))
                r_all.append(r_d[c] + _rowsum(lk_p))
            return r_d, jnp.concatenate(r_all, axis=0)

        def weights(hh, r_d):
            g_p = []
            for c in range(nch):
                rows = pl.ds(c * ch, ch)
                lb = z_scr[hh, rows, :]
                a_p = jnp.exp(lb[:, :blk] + (summed_rows(hh, c, 0) + jnp.where(has_prev, r_d[c], NO_KEYS)))
                a_d = jnp.where(crow + c * ch > ccol, jnp.exp(lb[:, blk:] + summed_rows(hh, c, 1)), 0.0)
                a = jnp.concatenate([a_p, a_d], axis=1)
                a_scr[hh, rows, :] = _mx(a)
                gm = g_scr[hh, rows, :] * a
                g_scr[hh, rows, :] = gm
                split_rows(hh, c, 0, gm[:, :blk])
                split_rows(hh, c, 1, gm[:, blk:])
                g_p.append(_rowsum(gm[:, :blk]))
            return g_p

        def logit_grads(hh, pg, g_p):
            for c in range(nch):
                rows = pl.ds(c * ch, ch)
                pre = jnp.concatenate([summed_rows(hh, c, 0) + pg[c * ch:(c + 1) * ch],
                                       summed_rows(hh, c, 1) + (pg[c * ch:(c + 1) * ch] + g_p[c])], axis=1)
                gm = g_scr[hh, rows, :]
                dz = gm - (gm + pre) * jnp.exp(z_scr[hh, rows, :])
                dz_p = dz[:, :blk]
                dz_d = jnp.where(crow + c * ch > ccol, dz[:, blk:], 0.0)
                dz_scr[hh, rows, :] = _mx(jnp.concatenate([dz_p, dz_d], axis=1))

        def products(hh, acc):
            ls = heads[hh]
            dk_scr[hh, rows_p, :] += _dot_tn(dz_scr[hh, :, 0:blk], qs[hh])
            dk_scr[hh, rows_d, :] += _dot_tn(dz_scr[hh, :, blk:2 * blk], qs[hh])
            dv_scr[hh, rows_p, :] += _dot_tn(a_scr[hh, :, 0:blk], dobs[hh])
            dv_scr[hh, rows_d, :] += _dot_tn(a_scr[hh, :, blk:2 * blk], dobs[hh])
            dq_ref[:, ls] = ((acc + _dot(dz_scr[hh], kcs[hh])) * SB_SCALE).astype(dq_ref.dtype)

        def suffix_sums(hh):
            cs_scr[hh] = _dot(lhs_scr[hh], u_gt)

        def prefix_sums(hh):
            cs_scr[hh] = _dot(lhs_scr[hh], u_lt)

        def more(n, r0, r1):
            return jnp.logical_and(n < i, jnp.max(jnp.maximum(r0, r1)) > -SB_SKIP)

        gate_grads(0)
        gate_grads(1)
        logits(0)
        weight_grads(0)
        logits(1)
        weight_grads(1)
        rd0, ra0 = keep_parts(0)
        suffix_sums(0)
        rd1, ra1 = keep_parts(1)
        suffix_sums(1)
        go = more(jnp.int32(1), ra0, ra1)
        gp0 = weights(0, rd0)
        prefix_sums(0)
        gp1 = weights(1, rd1)
        prefix_sums(1)

        def scan_block(hh, j, r, rmat):
            start = pl.multiple_of(j * blk, blk)
            _, lk = _sb_logits(qs[hh], k_ref[pl.ds(start, blk), heads[hh]], None)
            return r + _rowsum(lk), jnp.where(lane == j, r, rmat)

        def scan_step(c):
            _, n, r0, rmat0, r1, rmat1 = c
            r0, rmat0 = scan_block(0, i - 1 - n, r0, rmat0)
            r1, rmat1 = scan_block(1, i - 1 - n, r1, rmat1)
            return more(n + 1, r0, r1), n + 1, r0, rmat0, r1, rmat1

        zmat = jnp.zeros((blk, LANES), F32)
        _, n, _, rmat0, _, rmat1 = lax.while_loop(lambda c: c[0], scan_step,
                                                  (go, jnp.int32(1), ra0, zmat, ra1, zmat))
        rmats = (rmat0, rmat1)

        def block(hh, j, pg):
            ls = heads[hh]
            start = pl.multiple_of(j * blk, blk)
            k = k_ref[pl.ds(start, blk), ls]
            lb, lk = _sb_logits(qs[hh], k, None)
            r = _rowsum(jnp.where(lane == j, rmats[hh], 0.0))
            a = _sb_weights(lb, lk, r, u_gt, None)
            gm = _dot_nt(dobs[hh], v_ref[pl.ds(start, blk), ls]) * a
            dzb = _mx(gm - (gm + (pg + _split_dot(gm, u_lt))) * jnp.exp(lb))
            dk_scr[hh, pl.ds(start, blk), :] += _dot_tn(dzb, qs[hh])
            dv_scr[hh, pl.ds(start, blk), :] += _dot_tn(_mx(a), dobs[hh])
            return _dot(dzb, k), pg + _rowsum(gm)

        def step(t, c):
            acc0, pg0, acc1, pg1 = c
            dq0, pg0 = block(0, i - n + t, pg0)
            dq1, pg1 = block(1, i - n + t, pg1)
            return acc0 + dq0, pg0, acc1 + dq1, pg1

        zero = jnp.zeros((blk, 1), F32)
        zacc = jnp.zeros((blk, hd), F32)
        acc0, pg0, acc1, pg1 = lax.fori_loop(0, n - 1, step, (zacc, zero, zacc, zero))
        logit_grads(0, pg0, gp0)
        logit_grads(1, pg1, gp1)
        products(0, acc0)
        products(1, acc1)

        @pl.when(i == nq - 1)
        def _():
            for hh in range(2):
                ls = slice(hh * hd, (hh + 1) * hd)
                dk_ref[:, ls] = dk_scr[hh].astype(dk_ref.dtype)
                dv_ref[:, ls] = dv_scr[hh].astype(dv_ref.dtype)

    qblk = lambda c0: pl.BlockSpec((blk, LANES), lambda p, i: (i, c0 + p))
    full = lambda c0: pl.BlockSpec((s, LANES), lambda p, i: (0, c0 + p))
    half = jax.ShapeDtypeStruct((s, 512), MXU_DTYPE)
    return pl.pallas_call(
        body, name="stickbreak_bwd", grid=(SB_HEADS // 2, nq),
        in_specs=[qblk(0), full(4), full(8), qblk(0), qblk(0), qblk(4)],
        out_specs=[qblk(0), full(0), full(0), qblk(0)],
        out_shape=[half, half, half, half],
        scratch_shapes=[pltpu.VMEM((2, s, hd), F32), pltpu.VMEM((2, s, hd), F32),
                        pltpu.VMEM((2, blk, 2 * blk), F32),
                        pltpu.VMEM((2, blk, 2 * blk), F32),
                        pltpu.VMEM((2, 2 * blk, blk), MXU_DTYPE),
                        pltpu.VMEM((2, 2 * blk, blk), F32),
                        pltpu.VMEM((2, blk, 2 * blk), MXU_DTYPE),
                        pltpu.VMEM((2, blk, 2 * blk), MXU_DTYPE)],
        compiler_params=_params(("arbitrary", "arbitrary")),
    )(qkv, qkv, qkv, sg, o, dy)


def _layer_fwd(layer, x, mod, norm_g, w_in_g, w_out_g, tables, gather=()):
    shift, scale1p, gate = mod[0:1], 1.0 + mod[1:2], mod[2:3]
    ret, qkv, sg, *gathered = _ln_proj(x, shift, scale1p, norm_g, w_in_g, layer, gather)
    y_r, states = _retention_fwd(ret, tables)
    y_s, o_s = _sb_fwd(qkv, sg)
    x_next = _out_proj(x, gate, y_r, y_s, w_out_g, layer)
    saved = (x, shift, scale1p, gate, ret, qkv, sg, y_r, states, y_s, o_s)
    return x_next, saved, gathered


def _layer_bwd(layer, me, dx_out, saved, norm_g, w_in_g, w_out_g, tables):
    x, shift, scale1p, gate, ret, qkv, sg, y_r, states, y_s, o_s = saved
    dy, dw_out, dgate = _out_proj_bwd(dx_out, gate, y_r, y_s, w_out_g, layer)
    d_ret = _retention_bwd(ret, states, dy, tables)
    d_q, d_k, d_v, d_g = _sb_bwd(qkv, sg, o_s, dy)
    dproj = (d_ret, d_q, d_k, d_v, d_g)
    dwo_parts = _mx(dw_out.reshape(N_DEV, D_MODEL // N_DEV, D_MODEL))
    dx, dshift, dscale, dnorm_g, h_t, r_out = _in_proj_bwd_x(
        x, dx_out, dproj, shift, scale1p, norm_g, w_in_g, layer, dwo_parts)
    r_in = _in_proj_bwd_w(me, h_t, dproj)
    dmod = jnp.concatenate([dshift, dscale, dgate], axis=1)
    return dx, r_in, r_out, dmod, dnorm_g


MESH_IDS = pl.DeviceIdType.MESH
N_PEERS = N_DEV - 1
HBM_SPEC = pl.BlockSpec(memory_space=pl.ANY)


def _my_place():
    return lax.axis_index("x"), lax.axis_index("y"), lax.axis_index("c")


def _linear(px, py, pc):
    return 4 * px + 2 * py + pc


def _all_gather(blocks):
    n_arr = len(blocks)

    def body(*refs):
        start, forward, finish = _gather_plan(refs[:n_arr], refs[n_arr:2 * n_arr], *refs[2 * n_arr:])
        start()
        forward()
        finish()

    return pl.pallas_call(
        body, name="all_gather",
        out_shape=_gathered_shapes(blocks),
        in_specs=[HBM_SPEC] * n_arr, out_specs=[HBM_SPEC] * n_arr,
        scratch_shapes=_gather_sems(n_arr),
    )(*blocks)


def _gathered_shapes(blocks):
    return [jax.ShapeDtypeStruct((N_DEV * b.shape[0], b.shape[1]), b.dtype) for b in blocks]


def _gather_sems(n_arr):
    return [pltpu.SemaphoreType.DMA((n_arr * N_PEERS,)), pltpu.SemaphoreType.DMA((n_arr * N_PEERS,)),
            pltpu.SemaphoreType.DMA((n_arr,))]


def _gather_plan(x_refs, out_refs, send_sems, recv_sems, local_sems):
    n_arr = len(x_refs)
    x, y, c = _my_place()
    me, sibling = (x, y, c), (x, y, 1 - c)
    chips = [(1 - x, y), (x, 1 - y), (1 - x, 1 - y)]

    def rows(a, place):
        m = x_refs[a].shape[0]
        return out_refs[a].at[pl.ds(_linear(*place) * m, m), :]

    def copy(a, k, block, to, src=None):
        return pltpu.make_async_remote_copy(
            src_ref=rows(a, block) if src is None else src, dst_ref=rows(a, block),
            send_sem=send_sems.at[a * N_PEERS + k], recv_sem=recv_sems.at[a * N_PEERS + k],
            device_id=to, device_id_type=MESH_IDS)

    mine = [pltpu.make_async_copy(x_refs[a], rows(a, me), local_sems.at[a]) for a in range(n_arr)]
    first = []
    for a in range(n_arr):
        first.append(copy(a, 0, me, sibling, src=x_refs[a]))
        first += [copy(a, 1 + j, me, (*chip, c), src=x_refs[a]) for j, chip in enumerate(chips)]
    passed = [copy(a, 4 + j, (*chip, c), sibling) for j, chip in enumerate(chips) for a in range(n_arr)]

    def start():
        for cp in mine + first:
            cp.start()

    def forward():
        for j, chip in enumerate(chips):
            for a in range(n_arr):
                copy(a, 1 + j, (*chip, c), me).wait_recv()
                passed[j * n_arr + a].start()

    def finish():
        for a in range(n_arr):
            copy(a, 0, sibling, me).wait_recv()
            for j, chip in enumerate(chips):
                copy(a, 4 + j, (*chip, 1 - c), me).wait_recv()
        for cp in first + passed:
            cp.wait_send()
        for cp in mine:
            cp.wait()

    return start, forward, finish


def _ada_fwd(c_all, w_ada, b_cols):
    cols = w_ada.shape[2]

    def body(c_ref, w_ref, b_ref, ca_ref, mod_ref):
        cv = c_ref[...]
        ca = cv * _sigmoid(cv)
        ca_ref[...] = ca
        cb = _mx(jnp.concatenate([ca, ca], axis=0))
        for l in range(DEPTH):
            mod_ref[l * N_DEV:(l + 1) * N_DEV, :] = _dot(cb, _mx(w_ref[l]))[0:N_DEV] + b_ref[l]

    return pl.pallas_call(
        body, name="ada_fwd",
        out_shape=[jax.ShapeDtypeStruct((N_DEV, D_MODEL), F32),
                   jax.ShapeDtypeStruct((DEPTH * N_DEV, cols), F32)],
        compiler_params=_params(),
    )(c_all, w_ada, b_cols)


def _ada_bwd(c_act_t, dmod_cols):
    cols = dmod_cols.shape[2]

    def body(ca_ref, dm_ref, o_ref):
        ca = _mx(ca_ref[...]).astype(F32)
        for l in range(DEPTH):
            o_ref[l] = jnp.dot(ca, _mx(dm_ref[l]).astype(F32),
                               precision=lax.Precision.HIGHEST, preferred_element_type=F32)

    return pl.pallas_call(
        body, name="ada_bwd",
        out_shape=jax.ShapeDtypeStruct((DEPTH, D_MODEL, cols), F32),
        compiler_params=_params(),
    )(c_act_t, dmod_cols)


def _adamw_store(g, w_ref, m_ref, v_ref, g_ref, d_ref, mo_ref, vo_ref):
    m2 = ADAM_B1 * m_ref[...] + (1.0 - ADAM_B1) * g
    v2 = ADAM_B2 * v_ref[...] + (1.0 - ADAM_B2) * (g * g)
    m_hat = m2 / (1.0 - ADAM_B1 ** ADAM_STEP)
    v_hat = v2 / (1.0 - ADAM_B2 ** ADAM_STEP)
    g_ref[...] = g
    d_ref[...] = -ADAM_LR * (m_hat / (jnp.sqrt(v_hat) + ADAM_EPS) + ADAM_WD * w_ref[...])
    mo_ref[...] = m2
    vo_ref[...] = v2


def _slab_sum(p_ref):
    g = p_ref[0].astype(F32)
    for sl in range(1, p_ref.shape[0]):
        g = g + p_ref[sl].astype(F32)
    return g


def _sum_adamw_layers(parts, w, m, v):
    n_slab, rows, cols = parts[0].shape
    tr = min(256, rows)
    nt = rows // tr

    def body(p0_ref, p1_ref, w_ref, m_ref, v_ref, g_ref, d_ref, mo_ref, vo_ref):
        for l, p_ref in enumerate((p0_ref, p1_ref)):
            @pl.when(pl.program_id(0) == l)
            def _():
                _adamw_store(_slab_sum(p_ref), w_ref, m_ref, v_ref, g_ref, d_ref, mo_ref, vo_ref)

    p_specs = [pl.BlockSpec((n_slab, tr, cols), lambda l, i: (0, i * (1 - l) + (nt - 1) * l, 0)),
               pl.BlockSpec((n_slab, tr, cols), lambda l, i: (0, i * l, 0))]
    blk = pl.BlockSpec((None, tr, cols), lambda l, i: (l, i, 0))
    shp = jax.ShapeDtypeStruct((DEPTH, rows, cols), F32)
    return pl.pallas_call(
        body, name="sum_adamw_layers", grid=(DEPTH, nt),
        in_specs=p_specs + [blk, blk, blk],
        out_specs=[blk, blk, blk, blk],
        out_shape=[shp, shp, shp, shp],
        compiler_params=_params(("arbitrary", "arbitrary")),
    )(parts[0], parts[1], w, m, v)


def _sum_adamw(parts, w, m, v):
    n_slab, rows, cols = parts.shape
    tr = min(256, rows)

    def body(p_ref, w_ref, m_ref, v_ref, g_ref, d_ref, mo_ref, vo_ref):
        _adamw_store(_slab_sum(p_ref), w_ref, m_ref, v_ref, g_ref, d_ref, mo_ref, vo_ref)

    blk = pl.BlockSpec((tr, cols), lambda i: (i, 0))
    shp = jax.ShapeDtypeStruct((rows, cols), F32)
    return pl.pallas_call(
        body, name="sum_adamw", grid=(rows // tr,),
        in_specs=[pl.BlockSpec((n_slab, tr, cols), lambda i: (0, i, 0)), blk, blk, blk],
        out_specs=[blk, blk, blk, blk],
        out_shape=[shp, shp, shp, shp],
        compiler_params=_params(("arbitrary",)),
    )(parts, w, m, v)


SMALL_ROWS = 16


def kernel(x, c, norm_g, w_ada, b_ada, w_in, w_out, final_g, loss_target, m_norm_g, m_w_ada, m_b_ada, m_w_in, m_w_out, m_final_g, v_norm_g, v_w_ada, v_b_ada, v_w_in, v_w_out, v_final_g):
    me = _linear(*_my_place())
    in_cols = w_in.shape[2]
    out_rows = w_out.shape[1]
    ada_cols = w_ada.shape[2]

    w_in_m, w_out_m = _mx(w_in), _mx(w_out)
    g_in, g_out, g_c = _all_gather([w_in_m[0], w_out_m[0], jnp.broadcast_to(c, (8, D_MODEL))])
    w_in_g = [g_in.reshape(N_DEV, 1, D_MODEL, in_cols), None]
    w_out_g = [g_out.reshape(N_DEV, 1, out_rows, D_MODEL), None]
    c_all = g_c.reshape(N_DEV, 8, D_MODEL)[:, 0]

    b_cols = lax.dynamic_slice_in_dim(b_ada, me * ada_cols, ada_cols, axis=1)[:, None, :]
    c_act, mod_cols = _ada_fwd(c_all, w_ada, b_cols)
    (g_mod,) = _all_gather([mod_cols])
    g_mod = g_mod.reshape(N_DEV, DEPTH, N_DEV, ada_cols)
    mod = lax.dynamic_index_in_dim(g_mod, me, axis=2, keepdims=False)
    mod = mod.transpose(1, 0, 2).reshape(DEPTH, 3, D_MODEL)

    tables = _ret_tables(x.shape[1])
    h = x[0]
    saved = []
    for l in range(DEPTH):
        nxt = (w_in_m[l + 1], w_out_m[l + 1]) if l + 1 < DEPTH else ()
        h, sv, gathered = _layer_fwd(0, h, mod[l], norm_g[l:l + 1], w_in_g[l], w_out_g[l], tables, nxt)
        if nxt:
            w_in_g[l + 1] = gathered[0].reshape(N_DEV, 1, D_MODEL, in_cols)
            w_out_g[l + 1] = gathered[1].reshape(N_DEV, 1, out_rows, D_MODEL)
        saved.append(sv)
    dx, loss_part, dfg = _final_loss(h, final_g[None], loss_target[0])
    r_in, r_out, small = [None] * DEPTH, [None] * DEPTH, [None] * DEPTH
    for l in reversed(range(DEPTH)):
        dx, r_in[l], r_out[l], dmod, dng = _layer_bwd(0, me, dx, saved[l], norm_g[l:l + 1], w_in_g[l], w_out_g[l], tables)
        small[l] = (dmod.reshape(3, D_MODEL), dng)

    pad = jnp.zeros((SMALL_ROWS - 10, D_MODEL), F32)
    small_block = jnp.concatenate([small[0][0], small[1][0], small[0][1], small[1][1], dfg,
                                   jnp.broadcast_to(loss_part, (1, D_MODEL)), pad], axis=0)
    (g_small,) = _all_gather([small_block])
    g_small = g_small.reshape(N_DEV, SMALL_ROWS, D_MODEL)

    def small_pack(b, n, f, fill):
        return jnp.concatenate([b.reshape(6, D_MODEL), n, f[None],
                                jnp.full((SMALL_ROWS - 9, D_MODEL), fill, F32)], axis=0)

    s_g, s_d, s_m, s_v = _sum_adamw(g_small, small_pack(b_ada, norm_g, final_g, 0.0),
                                    small_pack(m_b_ada, m_norm_g, m_final_g, 0.0),
                                    small_pack(v_b_ada, v_norm_g, v_final_g, 1.0))
    loss = s_g[9, 0]

    def small_unpack(a):
        return a[0:6].reshape(DEPTH, 3 * D_MODEL), a[6:8], a[8]

    dmod_all = g_small[:, 0:6].reshape(N_DEV, DEPTH, 3 * D_MODEL).transpose(1, 0, 2)
    dmod_cols = lax.dynamic_slice_in_dim(dmod_all, me * ada_cols, ada_cols, axis=2)
    g_ada = _ada_bwd(c_act.T, dmod_cols).reshape(1, DEPTH * D_MODEL, ada_cols)
    ada = _sum_adamw(g_ada, *[a.reshape(DEPTH * D_MODEL, ada_cols) for a in (w_ada, m_w_ada, v_w_ada)])
    ada = [a.reshape(DEPTH, D_MODEL, ada_cols) for a in ada]

    win = _sum_adamw_layers(r_in, w_in, m_w_in, v_w_in)
    wout = _sum_adamw_layers(r_out, w_out, m_w_out, v_w_out)

    outs = [loss, dx[None]]
    for k in range(4):
        b, n, f = small_unpack((s_g, s_d, s_m, s_v)[k])
        outs += [n, ada[k], b, win[k], wout[k], f]
    return tuple(outs)
```

```python
import functools

import jax
import jax.numpy as jnp
from jax import lax
from jax.experimental import pallas as pl
from jax.experimental.pallas import tpu as pltpu

F32 = jnp.float32
MXU_DTYPE = jnp.bfloat16

D_MODEL = 1024
DEPTH = 2
N_DEV = 8
CHUNK = 64
D_RET = 512
D_SB = 512
RET_HEADS = 4
RET_HEAD_DIM = 128
SB_HEADS = 8
SB_HEAD_DIM = 64
D_IN = 4096
ROPE_BASE = 10000.0
EPS = 1e-6
SB_SCALE = SB_HEAD_DIM ** -0.5
RET_KSCALE = RET_HEAD_DIM ** -0.5

ADAM_LR = 0.001
ADAM_B1 = 0.9
ADAM_B2 = 0.999
ADAM_EPS = 1e-08
ADAM_WD = 0.01
ADAM_STEP = 10

V7X_VMEM_BYTES = 64 * 2 ** 20
VMEM_LIMIT = V7X_VMEM_BYTES - 8 * 2 ** 20
LANES = 128

_NT = (((1,), (1,)), ((), ()))
_TN = (((0,), (0,)), ((), ()))


def _dot(a, b):
    return jnp.dot(a, b, preferred_element_type=F32)


def _dot_nt(a, b):
    return lax.dot_general(a, b, _NT, preferred_element_type=F32)


def _dot_tn(a, b):
    return lax.dot_general(a, b, _TN, preferred_element_type=F32)


def _mx(x):
    return x.astype(MXU_DTYPE)


def _sigmoid(x):
    return 1.0 / (1.0 + jnp.exp(-x))


def _params(sem=None):
    return pltpu.CompilerParams(dimension_semantics=sem, vmem_limit_bytes=VMEM_LIMIT)


def _row_tile(s):
    return min(512, s)


def _w_in_spec(w_in_g, layer):
    return pl.BlockSpec((N_DEV, None) + w_in_g.shape[2:], lambda i: (0, layer, 0, 0))


def _w_out_spec(w_out_g, layer):
    return pl.BlockSpec((N_DEV, None) + w_out_g.shape[2:], lambda i: (0, layer, 0, 0))


def _ln_proj(x, shift, scale1p, g, w_in_g, layer, gather=()):
    s = x.shape[0]
    ts = _row_tile(s)
    ns = s // ts
    n_g = len(gather)

    def body(x_ref, sh_ref, sc_ref, g_ref, w_ref, *rest):
        ret_ref, qkv_ref, sg_ref = rest[n_g:n_g + 3]
        if n_g:
            start, forward, finish = _gather_plan(rest[:n_g], rest[n_g + 3:2 * n_g + 3], *rest[2 * n_g + 3:])
            i = pl.program_id(0)
            pl.when(i == 0)(start)
            pl.when(i == max(ns - 4, 0))(forward)
        xv = x_ref[...]
        rstd = lax.rsqrt(jnp.mean(xv * xv, axis=-1, keepdims=True) + EPS)
        h = (xv * rstd * g_ref[...]) * sc_ref[...] + sh_ref[...]
        hb = _mx(h)
        for n in range(4):
            ret_ref[:, n * 512:(n + 1) * 512] = _dot(hb, w_ref[n])
        qkv_ref[:, 0:512] = _mx(_dot(hb, w_ref[4]) * SB_SCALE)
        qkv_ref[:, 512:1024] = _mx(_dot(hb, w_ref[5]))
        qkv_ref[:, 1024:1536] = _mx(_dot(hb, w_ref[6]))
        sg_ref[...] = _dot(hb, w_ref[7])
        if n_g:
            pl.when(i == ns - 1)(finish)

    vec = pl.BlockSpec((1, D_MODEL), lambda i: (0, 0))
    return pl.pallas_call(
        body, name="ln_proj_gather" if n_g else "ln_proj", grid=(ns,),
        in_specs=[pl.BlockSpec((ts, D_MODEL), lambda i: (i, 0)), vec, vec, vec,
                  _w_in_spec(w_in_g, layer)] + [HBM_SPEC] * n_g,
        out_specs=[pl.BlockSpec((ts, 2048), lambda i: (i, 0)),
                   pl.BlockSpec((ts, 1536), lambda i: (i, 0)),
                   pl.BlockSpec((ts, 512), lambda i: (i, 0))] + [HBM_SPEC] * n_g,
        out_shape=[jax.ShapeDtypeStruct((s, 2048), F32),
                   jax.ShapeDtypeStruct((s, 1536), MXU_DTYPE),
                   jax.ShapeDtypeStruct((s, 512), F32)] + _gathered_shapes(gather),
        scratch_shapes=_gather_sems(n_g) if n_g else (),
        compiler_params=_params(("arbitrary",)),
    )(x, shift, scale1p, g, w_in_g, *gather)


def _w_out_halves(w_ref):
    half = N_DEV // 2
    return (w_ref[0:half].reshape(D_RET, D_MODEL), w_ref[half:N_DEV].reshape(D_SB, D_MODEL))


def _out_proj(x, gate, y_r, y_s, w_out_g, layer, loss_head=None):
    s = x.shape[0]
    ts = _row_tile(s)

    def layer_out(x_ref, gate_ref, yr_ref, ys_ref, w_ref):
        w_r, w_s = _w_out_halves(w_ref)
        return x_ref[...] + gate_ref[...] * (_dot(yr_ref[...], w_r) + _dot(ys_ref[...], w_s))

    def body(x_ref, gate_ref, yr_ref, ys_ref, w_ref, o_ref):
        o_ref[...] = layer_out(x_ref, gate_ref, yr_ref, ys_ref, w_ref)

    def body_loss(x_ref, gate_ref, yr_ref, ys_ref, w_ref, fg_ref, t_ref, dx_ref, loss_ref, dfg_ref):
        @pl.when(pl.program_id(0) == 0)
        def _():
            loss_ref[...] = jnp.zeros_like(loss_ref)
            dfg_ref[...] = jnp.zeros_like(dfg_ref)

        xv = layer_out(x_ref, gate_ref, yr_ref, ys_ref, w_ref)
        fgv = fg_ref[...]
        rstd = lax.rsqrt(jnp.mean(xv * xv, axis=-1, keepdims=True) + EPS)
        xn = xv * rstd
        err = xn * fgv - t_ref[...]
        tok = jnp.mean(err * err, axis=-1, keepdims=True)
        loss_ref[...] += 0.5 * jnp.sum(tok, axis=0, keepdims=True)
        dy = err * (1.0 / D_MODEL)
        dfg_ref[...] += jnp.sum(dy * xn, axis=0, keepdims=True)
        dxn = dy * fgv
        dx_ref[...] = rstd * (dxn - xn * jnp.mean(dxn * xn, axis=-1, keepdims=True))

    rows = pl.BlockSpec((ts, D_MODEL), lambda i: (i, 0))
    vec = pl.BlockSpec((1, D_MODEL), lambda i: (0, 0))
    in_specs = [rows, vec, pl.BlockSpec((ts, 512), lambda i: (i, 0)), pl.BlockSpec((ts, 512), lambda i: (i, 0)),
                _w_out_spec(w_out_g, layer)]
    if loss_head is None:
        return pl.pallas_call(
            body, name="out_proj", grid=(s // ts,), in_specs=in_specs, out_specs=rows,
            out_shape=jax.ShapeDtypeStruct((s, D_MODEL), F32),
            compiler_params=_params(("arbitrary",)),
        )(x, gate, y_r, y_s, w_out_g)
    return pl.pallas_call(
        body_loss, name="out_proj_loss", grid=(s // ts,), in_specs=in_specs + [vec, rows],
        out_specs=[rows, pl.BlockSpec((1, 1), lambda i: (0, 0)), vec],
        out_shape=[jax.ShapeDtypeStruct((s, D_MODEL), F32),
                   jax.ShapeDtypeStruct((1, 1), F32),
                   jax.ShapeDtypeStruct((1, D_MODEL), F32)],
        compiler_params=_params(("arbitrary",)),
    )(x, gate, y_r, y_s, w_out_g, *loss_head)


def _out_proj_bwd(dx_out, gate, y_r, y_s, w_out_g, layer):
    s = dx_out.shape[0]
    ts = _row_tile(s)

    def body(dx_ref, gate_ref, yr_ref, ys_ref, w_ref, dy_ref, dw_ref, dgate_ref):
        i = pl.program_id(0)

        @pl.when(i == 0)
        def _():
            dw_ref[...] = jnp.zeros_like(dw_ref)
            dgate_ref[...] = jnp.zeros_like(dgate_ref)

        dxv = dx_ref[...]
        dt = _mx(dxv * gate_ref[...])
        yr = yr_ref[...]
        ys = ys_ref[...]
        w_r, w_s = _w_out_halves(w_ref)
        dy_ref[:, 0:512] = _dot_nt(dt, w_r)
        dy_ref[:, 512:1024] = _dot_nt(dt, w_s)
        dw_ref[0:512, :] += _dot_tn(yr, dt)
        dw_ref[512:1024, :] += _dot_tn(ys, dt)
        t = _dot(yr, w_r) + _dot(ys, w_s)
        dgate_ref[...] += jnp.sum(dxv * t, axis=0, keepdims=True)

    return pl.pallas_call(
        body, name="out_proj_bwd", grid=(s // ts,),
        in_specs=[pl.BlockSpec((ts, D_MODEL), lambda i: (i, 0)),
                  pl.BlockSpec((1, D_MODEL), lambda i: (0, 0)),
                  pl.BlockSpec((ts, 512), lambda i: (i, 0)),
                  pl.BlockSpec((ts, 512), lambda i: (i, 0)),
                  _w_out_spec(w_out_g, layer)],
        out_specs=[pl.BlockSpec((ts, D_MODEL), lambda i: (i, 0)),
                   pl.BlockSpec((D_MODEL, D_MODEL), lambda i: (0, 0)),
                   pl.BlockSpec((1, D_MODEL), lambda i: (0, 0))],
        out_shape=[jax.ShapeDtypeStruct((s, D_MODEL), F32),
                   jax.ShapeDtypeStruct((D_MODEL, D_MODEL), F32),
                   jax.ShapeDtypeStruct((1, D_MODEL), F32)],
        compiler_params=_params(("arbitrary",)),
    )(dx_out, gate, y_r, y_s, w_out_g)


def _scatter_plan(parts_ref, recv_ref, send_sems, recv_sems, local_sem):
    px, py, pc = _my_place()
    mine = _linear(px, py, pc)

    def copy(r):
        peer = (1 - px if r & 4 else px, 1 - py if r & 2 else py, 1 - pc if r & 1 else pc)
        return pltpu.make_async_remote_copy(
            src_ref=parts_ref.at[_linear(*peer)], dst_ref=recv_ref.at[mine],
            send_sem=send_sems.at[r - 1], recv_sem=recv_sems.at[r - 1],
            device_id=peer, device_id_type=MESH_IDS)

    own = pltpu.make_async_copy(parts_ref.at[mine], recv_ref.at[mine], local_sem.at[0])

    def start():
        own.start()
        for r in range(1, N_DEV):
            copy(r).start()

    def finish():
        for r in range(1, N_DEV):
            copy(r).wait_recv()
            copy(r).wait_send()
        own.wait()

    return start, finish


def _in_proj_bwd_x(x, dx_out, dproj, shift, scale1p, g, w_in_g, layer, dwo_parts):
    s = x.shape[0]
    ts = _row_tile(s)
    ns = s // ts
    nb = D_IN // N_DEV

    def body(x_ref, dxo_ref, dr_ref, d4_ref, d5_ref, d6_ref, d7_ref, sh_ref, sc_ref, g_ref, w_ref, dwo_ref,
             dx_ref, dsh_ref, dsc_ref, dg_ref, ht_ref, rout_ref, send_sems, recv_sems, local_sem):
        i = pl.program_id(0)
        start, finish = _scatter_plan(dwo_ref, rout_ref, send_sems, recv_sems, local_sem)

        @pl.when(i == 0)
        def _():
            start()
            dsh_ref[...] = jnp.zeros_like(dsh_ref)
            dsc_ref[...] = jnp.zeros_like(dsc_ref)
            dg_ref[...] = jnp.zeros_like(dg_ref)

        dh = _dot_nt(dr_ref[:, 0:nb], w_ref[0])
        for n in range(1, 4):
            dh += _dot_nt(dr_ref[:, n * nb:(n + 1) * nb], w_ref[n])
        for n, d_ref in zip(range(4, N_DEV), (d4_ref, d5_ref, d6_ref, d7_ref)):
            dh += _dot_nt(d_ref[...], w_ref[n])
        xv = x_ref[...]
        gv = g_ref[...]
        scv = sc_ref[...]
        rstd = lax.rsqrt(jnp.mean(xv * xv, axis=-1, keepdims=True) + EPS)
        xn = xv * rstd
        xg = xn * gv
        ht_ref[...] = _mx((xg * scv + sh_ref[...]).T)
        dsh_ref[...] += jnp.sum(dh, axis=0, keepdims=True)
        dsc_ref[...] += jnp.sum(dh * xg, axis=0, keepdims=True)
        dhs = dh * scv
        dg_ref[...] += jnp.sum(dhs * xn, axis=0, keepdims=True)
        dxn = dhs * gv
        dx_ref[...] = rstd * (dxn - xn * jnp.mean(dxn * xn, axis=-1, keepdims=True)) + dxo_ref[...]
        pl.when(i == ns - 1)(finish)

    vec = pl.BlockSpec((1, D_MODEL), lambda i: (0, 0))
    return pl.pallas_call(
        body, name="in_proj_bwd_x", grid=(ns,),
        in_specs=[pl.BlockSpec((ts, D_MODEL), lambda i: (i, 0)),
                  pl.BlockSpec((ts, D_MODEL), lambda i: (i, 0)),
                  pl.BlockSpec((ts, 4 * nb), lambda i: (i, 0))]
                 + [pl.BlockSpec((ts, nb), lambda i: (i, 0))] * 4
                 + [vec, vec, vec, _w_in_spec(w_in_g, layer), HBM_SPEC],
        out_specs=[pl.BlockSpec((ts, D_MODEL), lambda i: (i, 0)), vec, vec, vec,
                   pl.BlockSpec((D_MODEL, ts), lambda i: (0, i)), HBM_SPEC],
        out_shape=[jax.ShapeDtypeStruct((s, D_MODEL), F32),
                   jax.ShapeDtypeStruct((1, D_MODEL), F32),
                   jax.ShapeDtypeStruct((1, D_MODEL), F32),
                   jax.ShapeDtypeStruct((1, D_MODEL), F32),
                   jax.ShapeDtypeStruct((D_MODEL, s), MXU_DTYPE),
                   jax.ShapeDtypeStruct(dwo_parts.shape, dwo_parts.dtype)],
        scratch_shapes=[pltpu.SemaphoreType.DMA((N_PEERS,)), pltpu.SemaphoreType.DMA((N_PEERS,)),
                        pltpu.SemaphoreType.DMA((1,))],
        compiler_params=_params(("arbitrary",)),
    )(x, dx_out, *dproj, shift, scale1p, g, w_in_g, dwo_parts)


def _in_proj_bwd_w(me, h_t, dproj):
    s = h_t.shape[1]
    ts = min(4 * _row_tile(s), s)
    ns = s // ts
    nb = D_IN // N_DEV
    n_chip = N_DEV // 2

    def flip_bits(j):
        return jnp.where(j == 0, 4, jnp.where(j == 1, 2, jnp.where(j == 2, 6, 0)))

    def slab_of(t, me_ref):
        return jnp.bitwise_xor(me_ref[0], flip_bits(t // 2) + 1 - t % 2)

    def body(me_ref, ht_ref, dr_ref, d4_ref, d5_ref, d6_ref, d7_ref, rin_ref,
             acc, stage, pre_buf, pre_send, pre_recv, sum_send, sum_recv, local_sem):
        t = pl.program_id(0)
        i = pl.program_id(1)
        j = t // 2
        summing = t % 2 == 1
        slab = slab_of(t, me_ref)
        px, py, pc = _my_place()

        def pre_copy(jj):
            return pltpu.make_async_remote_copy(
                src_ref=stage.at[jj % 2], dst_ref=pre_buf.at[jj],
                send_sem=pre_send.at[jj], recv_sem=pre_recv.at[jj],
                device_id=(px, py, 1 - pc), device_id_type=MESH_IDS)

        def sum_copy(jj):
            fx = jnp.logical_or(jj == 0, jj == 2)
            fy = jnp.logical_or(jj == 1, jj == 2)
            return pltpu.make_async_remote_copy(
                src_ref=stage.at[2 + jj % 2], dst_ref=rin_ref.at[jj],
                send_sem=sum_send.at[jj], recv_sem=sum_recv.at[jj],
                device_id=(jnp.where(fx, 1 - px, px), jnp.where(fy, 1 - py, py), pc), device_id_type=MESH_IDS)

        own = pltpu.make_async_copy(stage.at[3], rin_ref.at[n_chip - 1], local_sem.at[0])

        @pl.when(i == 0)
        def _():
            acc[...] = jnp.zeros_like(acc)

        @pl.when(slab < 4)
        def _():
            acc[...] += _dot(ht_ref[...], dr_ref[...])

        for n, d_ref in zip(range(4, N_DEV), (d4_ref, d5_ref, d6_ref, d7_ref)):
            @pl.when(slab == n)
            def _():
                acc[...] += _dot(ht_ref[...], d_ref[...])

        @pl.when(jnp.logical_and(i == ns - 1, jnp.logical_not(summing)))
        def _():
            @pl.when(j >= 2)
            def _():
                pre_copy(j - 2).wait_send()

            stage[j % 2] = acc[...].astype(stage.dtype)
            pre_copy(j).start()

        @pl.when(jnp.logical_and(i == ns - 1, summing))
        def _():
            pre_copy(j).wait_recv()

            @pl.when(j >= 2)
            def _():
                sum_copy(j - 2).wait_send()

            stage[2 + j % 2] = (acc[...] + pre_buf[j].astype(F32)).astype(stage.dtype)

            @pl.when(j < n_chip - 1)
            def _():
                sum_copy(j).start()

            @pl.when(j == n_chip - 1)
            def _():
                own.start()
                pre_copy(n_chip - 2).wait_send()
                pre_copy(n_chip - 1).wait_send()
                sum_copy(n_chip - 2).wait_send()
                for jj in range(n_chip - 1):
                    sum_copy(jj).wait_recv()
                own.wait()

    def part_rows(n, t, i, me_ref):
        return jnp.where(slab_of(t, me_ref) == n, i, ns - 1), 0

    return pl.pallas_call(
        body, name="in_proj_bwd_w",
        grid_spec=pltpu.PrefetchScalarGridSpec(
            num_scalar_prefetch=1, grid=(N_DEV, ns),
            in_specs=[pl.BlockSpec((D_MODEL, ts), lambda t, i, me_ref: (0, i)),
                      pl.BlockSpec((ts, nb), lambda t, i, me_ref: (
                          jnp.where(slab_of(t, me_ref) < 4, i, ns - 1), jnp.minimum(slab_of(t, me_ref), 3)))]
                     + [pl.BlockSpec((ts, nb), functools.partial(part_rows, n)) for n in range(4, N_DEV)],
            out_specs=HBM_SPEC,
            scratch_shapes=[pltpu.VMEM((D_MODEL, nb), F32),
                            pltpu.VMEM((4, D_MODEL, nb), MXU_DTYPE),
                            pltpu.VMEM((n_chip, D_MODEL, nb), MXU_DTYPE),
                            pltpu.SemaphoreType.DMA((n_chip,)), pltpu.SemaphoreType.DMA((n_chip,)),
                            pltpu.SemaphoreType.DMA((n_chip - 1,)), pltpu.SemaphoreType.DMA((n_chip - 1,)),
                            pltpu.SemaphoreType.DMA((1,))]),
        out_shape=jax.ShapeDtypeStruct((n_chip, D_MODEL, nb), MXU_DTYPE),
        compiler_params=_params(("arbitrary", "arbitrary")),
    )(jnp.reshape(me, (1,)).astype(jnp.int32), h_t, *dproj)


RET_TILE = 256


def _ret_tables(s):
    t = min(RET_TILE, s)
    half = RET_HEAD_DIM // 2
    pos = jnp.arange(s, dtype=F32)
    inv = ROPE_BASE ** (-jnp.arange(half, dtype=F32) / half)
    ang = pos[:, None] * inv[None, :]
    cos, sin = jnp.cos(ang), jnp.sin(ang)
    cos2 = jnp.concatenate([cos, cos], axis=1)
    sin2 = jnp.concatenate([-sin, sin], axis=1)
    lg = jnp.log1p(-(2.0 ** (-5.0 - jnp.arange(RET_HEADS, dtype=F32))))[:, None, None]
    n = jnp.arange(t)
    dist = (n[:, None] - n[None, :]).astype(F32)[None]
    cn = (n // CHUNK)[:, None]
    cm = (n // CHUNK)[None, :]
    mask = jnp.where((cn == cm)[None], jnp.exp(jnp.abs(dist) * lg),
                     jnp.where((cm < cn)[None], jnp.exp(dist * lg), 0.0))
    nf = n.astype(F32)[None, :, None]
    dq = jnp.broadcast_to(jnp.exp((nf + 1.0) * lg), (RET_HEADS, t, LANES))
    dk = jnp.broadcast_to(jnp.exp((t - 1.0 - nf) * lg), (RET_HEADS, t, LANES))
    gt = jnp.broadcast_to(jnp.exp(float(t) * lg), (RET_HEADS, 1, LANES))
    return cos2, sin2, mask, dq, dk, gt


def _roll_half(x):
    return pltpu.roll(x, RET_HEAD_DIM // 2, 1)


def _ret_heads_fwd(ret_ref, cos, sin, m_ref, dq_ref, dk_ref, s0):
    hd = RET_HEAD_DIM
    heads = range(RET_HEADS)
    qb, kb, vb, kdb = [], [], [], []
    for h in heads:
        q = ret_ref[:, h * hd:(h + 1) * hd]
        k = ret_ref[:, 512 + h * hd:512 + (h + 1) * hd]
        kr = (k * cos + _roll_half(k) * sin) * RET_KSCALE
        qb.append(_mx(q * cos + _roll_half(q) * sin))
        kb.append(_mx(kr))
        kdb.append(_mx(kr * dk_ref[h]))
        vb.append(_mx(ret_ref[:, 1024 + h * hd:1024 + (h + 1) * hd]))
    p = [_dot_nt(qb[h], kb[h]) for h in heads]
    cross = [_dot(qb[h], _mx(s0[h])) for h in heads]
    pb = [_mx(p[h] * m_ref[h]) for h in heads]
    o = [_dot(pb[h], vb[h]) + cross[h] * dq_ref[h] for h in heads]
    gn, rstd = [], []
    for h in heads:
        oc = o[h] - jnp.mean(o[h], axis=-1, keepdims=True)
        rstd.append(lax.rsqrt(jnp.mean(oc * oc, axis=-1, keepdims=True) + EPS))
        gn.append(oc * rstd[h])
    return qb, kb, vb, pb, kdb, gn, rstd


def _retention_fwd(ret, tables):
    cos2, sin2, mask, dq, dk, gt = tables
    s = ret.shape[0]
    t = mask.shape[1]
    nt = s // t
    hd = RET_HEAD_DIM

    def body(ret_ref, cos_ref, sin_ref, m_ref, dq_ref, dk_ref, gt_ref, y_ref, st_ref, s_scr):
        i = pl.program_id(0)

        @pl.when(i == 0)
        def _():
            s_scr[...] = jnp.zeros_like(s_scr)

        s0 = [s_scr[h] for h in range(RET_HEADS)]
        _, _, vb, _, kdb, gn, _ = _ret_heads_fwd(ret_ref, cos_ref[...], sin_ref[...], m_ref, dq_ref, dk_ref, s0)
        kv = [_dot_tn(kdb[h], vb[h]) for h in range(RET_HEADS)]
        for h in range(RET_HEADS):
            g = ret_ref[:, 1536 + h * hd:1536 + (h + 1) * hd]
            st_ref[h] = s0[h]
            y_ref[:, h * hd:(h + 1) * hd] = (gn[h] * (g * _sigmoid(g))).astype(y_ref.dtype)
            s_scr[h] = s0[h] * gt_ref[h] + kv[h]

    full3 = lambda a: pl.BlockSpec(a.shape, lambda i: (0, 0, 0))
    return pl.pallas_call(
        body, name="retention_fwd", grid=(nt,),
        in_specs=[pl.BlockSpec((t, 2048), lambda i: (i, 0)),
                  pl.BlockSpec((t, LANES), lambda i: (i, 0)),
                  pl.BlockSpec((t, LANES), lambda i: (i, 0)),
                  full3(mask), full3(dq), full3(dk), full3(gt)],
        out_specs=[pl.BlockSpec((t, 512), lambda i: (i, 0)),
                   pl.BlockSpec((None, RET_HEADS, hd, hd), lambda i: (i, 0, 0, 0))],
        out_shape=[jax.ShapeDtypeStruct((s, 512), MXU_DTYPE),
                   jax.ShapeDtypeStruct((nt, RET_HEADS, hd, hd), F32)],
        scratch_shapes=[pltpu.VMEM((RET_HEADS, hd, hd), F32)],
        compiler_params=_params(("arbitrary",)),
    )(ret, cos2, sin2, mask, dq, dk, gt)


def _retention_bwd(ret, states, dy, tables):
    cos2, sin2, mask, dq, dk, gt = tables
    s = ret.shape[0]
    t = mask.shape[1]
    nt = s // t
    hd = RET_HEAD_DIM

    def body(ret_ref, st_ref, dy_ref, cos_ref, sin_ref, m_ref, dq_ref, dk_ref, gt_ref, d_ref, ds_scr):
        i = pl.program_id(0)

        @pl.when(i == 0)
        def _():
            ds_scr[...] = jnp.zeros_like(ds_scr)

        cos = cos_ref[...]
        sin = sin_ref[...]
        heads = range(RET_HEADS)
        s0 = [st_ref[h] for h in heads]
        ds = [ds_scr[h] for h in heads]
        dsb = [_mx(ds[h]) for h in heads]
        qb, kb, vb, pb, kdb, gn, rstd = _ret_heads_fwd(ret_ref, cos, sin, m_ref, dq_ref, dk_ref, s0)
        dob, dodb = [], []
        for h in heads:
            g = ret_ref[:, 1536 + h * hd:1536 + (h + 1) * hd]
            dyv = dy_ref[:, h * hd:(h + 1) * hd]
            sg = _sigmoid(g)
            d_ref[:, 1536 + h * hd:1536 + (h + 1) * hd] = (
                dyv * gn[h] * (sg * (1.0 + g * (1.0 - sg)))).astype(d_ref.dtype)
            dgn = dyv * (g * sg)
            do = rstd[h] * (dgn - jnp.mean(dgn, axis=-1, keepdims=True)
                            - gn[h] * jnp.mean(dgn * gn[h], axis=-1, keepdims=True))
            dob.append(_mx(do))
            dodb.append(_mx(do * dq_ref[h]))
        dp = [_dot_nt(dob[h], vb[h]) for h in heads]
        dv = [_dot_tn(pb[h], dob[h]) + _dot(kdb[h], dsb[h]) for h in heads]
        dq_cross = [_dot_nt(dodb[h], _mx(s0[h])) for h in heads]
        dk_cross = [_dot_nt(vb[h], dsb[h]) for h in heads]
        ds_new = [_dot_tn(qb[h], dodb[h]) for h in heads]
        dpb = [_mx(dp[h] * m_ref[h]) for h in heads]
        dqr = [_dot(dpb[h], kb[h]) + dq_cross[h] for h in heads]
        dkr = [(_dot_tn(dpb[h], qb[h]) + dk_cross[h] * dk_ref[h]) * RET_KSCALE for h in heads]
        for h in heads:
            d_ref[:, 1024 + h * hd:1024 + (h + 1) * hd] = dv[h].astype(d_ref.dtype)
            d_ref[:, h * hd:(h + 1) * hd] = (dqr[h] * cos + _roll_half(dqr[h] * sin)).astype(d_ref.dtype)
            d_ref[:, 512 + h * hd:512 + (h + 1) * hd] = (
                dkr[h] * cos + _roll_half(dkr[h] * sin)).astype(d_ref.dtype)
            ds_scr[h] = ds[h] * gt_ref[h] + ds_new[h]

    full3 = lambda a: pl.BlockSpec(a.shape, lambda i: (0, 0, 0))
    rev = lambda i: (nt - 1 - i, 0)
    return pl.pallas_call(
        body, name="retention_bwd", grid=(nt,),
        in_specs=[pl.BlockSpec((t, 2048), rev),
                  pl.BlockSpec((None, RET_HEADS, hd, hd), lambda i: (nt - 1 - i, 0, 0, 0)),
                  pl.BlockSpec((t, 512), rev),
                  pl.BlockSpec((t, LANES), rev),
                  pl.BlockSpec((t, LANES), rev),
                  full3(mask), full3(dq), full3(dk), full3(gt)],
        out_specs=pl.BlockSpec((t, 2048), rev),
        out_shape=jax.ShapeDtypeStruct((s, 2048), MXU_DTYPE),
        scratch_shapes=[pltpu.VMEM((RET_HEADS, hd, hd), F32)],
        compiler_params=_params(("arbitrary",)),
    )(ret, states, dy, cos2, sin2, mask, dq, dk, gt)


SB_BLOCK = 256
SB_CHUNK = 32


SB_SKIP = 104.0
NO_KEYS = -1e30


def _split_dots(xs, u):
    parts = []
    for x in xs:
        hi = lax.bitcast_convert_type(lax.bitcast_convert_type(x, jnp.uint32) & jnp.uint32(0xFFFF0000), F32)
        parts += [_mx(hi), _mx(x - hi)]
    out = _dot(jnp.concatenate(parts, axis=0), u)
    n = xs[0].shape[0]
    return [out[2 * k * n:(2 * k + 1) * n] + out[(2 * k + 1) * n:(2 * k + 2) * n] for k in range(len(xs))]


def _split_dot(x, u):
    return _split_dots([x], u)[0]


def _sb_pair_weights(lb, lk, allowed, u_gt):
    blk = lb.shape[0]
    lk_p, lk_d = lk[:, :blk], lk[:, blk:]
    r_d = _rowsum(lk_d)
    cs_p, cs_d = _split_dots([lk_p, lk_d], u_gt)
    a = jnp.exp(lb + jnp.concatenate([cs_p + r_d, cs_d], axis=1))
    return jnp.where(allowed, a, 0.0), r_d, r_d + _rowsum(lk_p)


def _sb_logits(q, k, causal):
    z = _dot_nt(q, k)
    l1p = jnp.log(1.0 + jnp.exp(-jnp.abs(z)))
    lk = -(jnp.maximum(z, 0.0) + l1p)
    if causal is not None:
        lk = jnp.where(causal, lk, 0.0)
    return jnp.minimum(z, 0.0) - l1p, lk


def _sb_weights(lb, lk, r, u_gt, causal):
    a = jnp.exp(lb + _split_dot(lk, u_gt) + r)
    return a if causal is None else jnp.where(causal, a, 0.0)


def _rowsum(x):
    return jnp.sum(x, axis=1, keepdims=True)


def _sb_pair_tile(i, blk):
    row = lax.broadcasted_iota(jnp.int32, (blk, 2 * blk), 0)
    col = lax.broadcasted_iota(jnp.int32, (blk, 2 * blk), 1)
    first_col = jnp.where(i >= 1, 0, blk)
    allowed = jnp.logical_and(row > col - blk, col >= first_col)
    rows_p = pl.ds(pl.multiple_of(jnp.maximum(i - 1, 0) * blk, blk), blk)
    rows_d = pl.ds(pl.multiple_of(i * blk, blk), blk)
    return allowed, rows_p, rows_d


def _sb_fwd(qkv, sg):
    s = qkv.shape[0]
    blk = min(SB_BLOCK, s)
    nq = s // blk
    hd = SB_HEAD_DIM

    ch = min(SB_CHUNK, blk)

    def body(q_ref, k_ref, v_ref, g_ref, y_ref, o_ref, z_scr, lhs_scr, cs_scr, a_scr):
        i = pl.program_id(1)
        row = lax.broadcasted_iota(jnp.int32, (blk, blk), 0)
        col = lax.broadcasted_iota(jnp.int32, (blk, blk), 1)
        u_gt = (row > col).astype(MXU_DTYPE)
        heads = [slice(hh * hd, (hh + 1) * hd) for hh in range(2)]
        qs = [q_ref[:, ls] for ls in heads]
        _, rows_p, rows_d = _sb_pair_tile(i, blk)
        has_prev = i >= 1
        crow = lax.broadcasted_iota(jnp.int32, (ch, blk), 0)
        ccol = lax.broadcasted_iota(jnp.int32, (ch, blk), 1)

        def logits(hh):
            kc = jnp.concatenate([k_ref[rows_p, heads[hh]], k_ref[rows_d, heads[hh]]], axis=0)
            z_scr[hh] = _dot_nt(qs[hh], kc)

        def keep_parts(hh):
            r_d, r_all = [], []
            for c in range(blk // ch):
                rows = pl.ds(c * ch, ch)
                causal = crow + c * ch > ccol
                z = z_scr[hh, rows, :]
                l1p = jnp.log(1.0 + jnp.exp(-jnp.abs(z)))
                lb = jnp.minimum(z, 0.0) - l1p
                z_scr[hh, rows, :] = lb
                lk = lb - z
                lk_p = lk[:, :blk]
                lk_d = jnp.where(causal, lk[:, blk:], 0.0)
                lhs_scr[hh, pl.ds(c * ch, ch), :] = _mx(lk_p)
                lhs_scr[hh, pl.ds(blk + c * ch, ch), :] = _mx(lk_d)
                r_d.append(_rowsum(lk_d))
                r_all.append(r_d[c] + _rowsum(lk_p))
            return r_d, jnp.concatenate(r_all, axis=0)

        def suffix_sums(hh):
            cs_scr[hh] = _dot(lhs_scr[hh], u_gt)

        def weights(hh, r_d):
            for c in range(blk // ch):
                rows = pl.ds(c * ch, ch)
                causal = crow + c * ch > ccol
                cs_p = cs_scr[hh, pl.ds(c * ch, ch), :] + jnp.where(has_prev, r_d[c], NO_KEYS)
                cs_d = cs_scr[hh, pl.ds(blk + c * ch, ch), :]
                lb = z_scr[hh, rows, :]
                a_p = jnp.exp(lb[:, :blk] + cs_p)
                a_d = jnp.where(causal, jnp.exp(lb[:, blk:] + cs_d), 0.0)
                a_scr[hh, rows, :] = _mx(jnp.concatenate([a_p, a_d], axis=1))

        def values(hh):
            vc = jnp.concatenate([v_ref[rows_p, heads[hh]], v_ref[rows_d, heads[hh]]], axis=0)
            return _dot(a_scr[hh], vc)

        def block(hh, j, r):
            start = pl.multiple_of(j * blk, blk)
            lb, lk = _sb_logits(qs[hh], k_ref[pl.ds(start, blk), heads[hh]], None)
            a = _sb_weights(lb, lk, r, u_gt, None)
            return _dot(_mx(a), v_ref[pl.ds(start, blk), heads[hh]]), r + _rowsum(lk)

        def more(n, r0, r1):
            return jnp.logical_and(n < i, jnp.max(jnp.maximum(r0, r1)) > -SB_SKIP)

        logits(0)
        logits(1)
        gates = [g_ref[:, ls] * _sigmoid(g_ref[:, ls]) for ls in heads]
        rd0, r0 = keep_parts(0)
        suffix_sums(0)
        rd1, r1 = keep_parts(1)
        suffix_sums(1)
        go = more(jnp.int32(1), r0, r1)
        weights(0, rd0)
        acc0 = values(0)
        weights(1, rd1)
        acc1 = values(1)

        def step(c):
            _, n, acc0, r0, acc1, r1 = c
            pv0, r0 = block(0, i - 1 - n, r0)
            pv1, r1 = block(1, i - 1 - n, r1)
            return more(n + 1, r0, r1), n + 1, acc0 + pv0, r0, acc1 + pv1, r1

        _, _, acc0, _, acc1, _ = lax.while_loop(lambda c: c[0], step, (go, jnp.int32(1), acc0, r0, acc1, r1))
        for ls, acc, gate in zip(heads, (acc0, acc1), gates):
            o_ref[:, ls] = acc
            y_ref[:, ls] = (acc * gate).astype(y_ref.dtype)

    qblk = pl.BlockSpec((blk, LANES), lambda p, i: (i, p))
    return pl.pallas_call(
        body, name="stickbreak_fwd", grid=(SB_HEADS // 2, nq),
        in_specs=[qblk,
                  pl.BlockSpec((s, LANES), lambda p, i: (0, 4 + p)),
                  pl.BlockSpec((s, LANES), lambda p, i: (0, 8 + p)),
                  qblk],
        out_specs=[qblk, qblk],
        out_shape=[jax.ShapeDtypeStruct((s, 512), MXU_DTYPE),
                   jax.ShapeDtypeStruct((s, 512), F32)],
        scratch_shapes=[pltpu.VMEM((2, blk, 2 * blk), F32),
                        pltpu.VMEM((2, 2 * blk, blk), MXU_DTYPE),
                        pltpu.VMEM((2, 2 * blk, blk), F32),
                        pltpu.VMEM((2, blk, 2 * blk), MXU_DTYPE)],
        compiler_params=_params(("arbitrary", "arbitrary")),
    )(qkv, qkv, qkv, sg)


def _sb_bwd(qkv, sg, o, dy):
    s = qkv.shape[0]
    blk = min(SB_BLOCK, s)
    nq = s // blk
    hd = SB_HEAD_DIM
    assert nq <= LANES
    ch = min(SB_CHUNK, blk)

    def body(q_ref, k_ref, v_ref, g_ref, o_ref, dy_ref, dq_ref, dk_ref, dv_ref, dg_ref, dk_scr, dv_scr,
             z_scr, g_scr, lhs_scr, cs_scr, a_scr, dz_scr):
        i = pl.program_id(1)

        @pl.when(i == 0)
        def _():
            dk_scr[...] = jnp.zeros_like(dk_scr)
            dv_scr[...] = jnp.zeros_like(dv_scr)

        row = lax.broadcasted_iota(jnp.int32, (blk, blk), 0)
        col = lax.broadcasted_iota(jnp.int32, (blk, blk), 1)
        lane = lax.broadcasted_iota(jnp.int32, (blk, LANES), 1)
        u_gt = (row > col).astype(MXU_DTYPE)
        u_lt = (row < col).astype(MXU_DTYPE)
        heads = [slice(hh * hd, (hh + 1) * hd) for hh in range(2)]
        qs = [q_ref[:, ls] for ls in heads]
        _, rows_p, rows_d = _sb_pair_tile(i, blk)
        has_prev = i >= 1
        crow = lax.broadcasted_iota(jnp.int32, (ch, blk), 0)
        ccol = lax.broadcasted_iota(jnp.int32, (ch, blk), 1)
        nch = blk // ch
        kcs = [jnp.concatenate([k_ref[rows_p, ls], k_ref[rows_d, ls]], axis=0) for ls in heads]
        dobs = []

        def gate_grads(hh):
            ls = heads[hh]
            g = g_ref[:, ls]
            dyv = dy_ref[:, ls]
            sgm = _sigmoid(g)
            dg_ref[:, ls] = (dyv * o_ref[:, ls] * (sgm * (1.0 + g * (1.0 - sgm)))).astype(dg_ref.dtype)
            dobs.append(_mx(dyv * (g * sgm)))

        def split_rows(hh, c, part, x):
            lhs_scr[hh, pl.ds(part * blk + c * ch, ch), :] = _mx(x)

        def summed_rows(hh, c, part):
            return cs_scr[hh, pl.ds(part * blk + c * ch, ch), :]

        def logits(hh):
            z_scr[hh] = _dot_nt(qs[hh], kcs[hh])

        def weight_grads(hh):
            vc = jnp.concatenate([v_ref[rows_p, heads[hh]], v_ref[rows_d, heads[hh]]], axis=0)
            g_scr[hh] = _dot_nt(dobs[hh], vc)

        def keep_parts(hh):
            r_d, r_all = [], []
            for c in range(nch):
                rows = pl.ds(c * ch, ch)
                z = z_scr[hh, rows, :]
                l1p = jnp.log(1.0 + jnp.exp(-jnp.abs(z)))
                lb = jnp.minimum(z, 0.0) - l1p
                z_scr[hh, rows, :] = lb
                lk = lb - z
                lk_p = lk[:, :blk]
                lk_d = jnp.where(crow + c * ch > ccol, lk[:, blk:], 0.0)
                split_rows(hh, c, 0, lk_p)
                split_rows(hh, c, 1, lk_d)
                r_d.append(_rowsum(lk_d))
                r_all.append(r_d[c] + _rowsum(lk_p))
            return r_d, jnp.concatenate(r_all, axis=0)

        def weights(hh, r_d):
            g_p = []
            for c in range(nch):
                rows = pl.ds(c * ch, ch)
                lb = z_scr[hh, rows, :]
                a_p = jnp.exp(lb[:, :blk] + (summed_rows(hh, c, 0) + jnp.where(has_prev, r_d[c], NO_KEYS)))
                a_d = jnp.where(crow + c * ch > ccol, jnp.exp(lb[:, blk:] + summed_rows(hh, c, 1)), 0.0)
                a = jnp.concatenate([a_p, a_d], axis=1)
                a_scr[hh, rows, :] = _mx(a)
                gm = g_scr[hh, rows, :] * a
                g_scr[hh, rows, :] = gm
                split_rows(hh, c, 0, gm[:, :blk])
                split_rows(hh, c, 1, gm[:, blk:])
                g_p.append(_rowsum(gm[:, :blk]))
            return g_p

        def logit_grads(hh, pg, g_p):
            for c in range(nch):
                rows = pl.ds(c * ch, ch)
                pre = jnp.concatenate([summed_rows(hh, c, 0) + pg[c * ch:(c + 1) * ch],
                                       summed_rows(hh, c, 1) + (pg[c * ch:(c + 1) * ch] + g_p[c])], axis=1)
                gm = g_scr[hh, rows, :]
                dz = gm - (gm + pre) * jnp.exp(z_scr[hh, rows, :])
                dz_p = dz[:, :blk]
                dz_d = jnp.where(crow + c * ch > ccol, dz[:, blk:], 0.0)
                dz_scr[hh, rows, :] = _mx(jnp.concatenate([dz_p, dz_d], axis=1))

        def products(hh, acc):
            ls = heads[hh]
            dk_scr[hh, rows_p, :] += _dot_tn(dz_scr[hh, :, 0:blk], qs[hh])
            dk_scr[hh, rows_d, :] += _dot_tn(dz_scr[hh, :, blk:2 * blk], qs[hh])
            dv_scr[hh, rows_p, :] += _dot_tn(a_scr[hh, :, 0:blk], dobs[hh])
            dv_scr[hh, rows_d, :] += _dot_tn(a_scr[hh, :, blk:2 * blk], dobs[hh])
            dq_ref[:, ls] = ((acc + _dot(dz_scr[hh], kcs[hh])) * SB_SCALE).astype(dq_ref.dtype)

        def suffix_sums(hh):
            cs_scr[hh] = _dot(lhs_scr[hh], u_gt)

        def prefix_sums(hh):
            cs_scr[hh] = _dot(lhs_scr[hh], u_lt)

        def more(n, r0, r1):
            return jnp.logical_and(n < i, jnp.max(jnp.maximum(r0, r1)) > -SB_SKIP)

        gate_grads(0)
        gate_grads(1)
        logits(0)
        weight_grads(0)
        logits(1)
        weight_grads(1)
        rd0, ra0 = keep_parts(0)
        suffix_sums(0)
        rd1, ra1 = keep_parts(1)
        suffix_sums(1)
        go = more(jnp.int32(1), ra0, ra1)
        gp0 = weights(0, rd0)
        prefix_sums(0)
        gp1 = weights(1, rd1)
        prefix_sums(1)

        def scan_block(hh, j, r, rmat):
            start = pl.multiple_of(j * blk, blk)
            _, lk = _sb_logits(qs[hh], k_ref[pl.ds(start, blk), heads[hh]], None)
            return r + _rowsum(lk), jnp.where(lane == j, r, rmat)

        def scan_step(c):
            _, n, r0, rmat0, r1, rmat1 = c
            r0, rmat0 = scan_block(0, i - 1 - n, r0, rmat0)
            r1, rmat1 = scan_block(1, i - 1 - n, r1, rmat1)
            return more(n + 1, r0, r1), n + 1, r0, rmat0, r1, rmat1

        zmat = jnp.zeros((blk, LANES), F32)
        _, n, _, rmat0, _, rmat1 = lax.while_loop(lambda c: c[0], scan_step,
                                                  (go, jnp.int32(1), ra0, zmat, ra1, zmat))
        rmats = (rmat0, rmat1)

        def block(hh, j, pg):
            ls = heads[hh]
            start = pl.multiple_of(j * blk, blk)
            k = k_ref[pl.ds(start, blk), ls]
            lb, lk = _sb_logits(qs[hh], k, None)
            r = _rowsum(jnp.where(lane == j, rmats[hh], 0.0))
            a = _sb_weights(lb, lk, r, u_gt, None)
            gm = _dot_nt(dobs[hh], v_ref[pl.ds(start, blk), ls]) * a
            dzb = _mx(gm - (gm + (pg + _split_dot(gm, u_lt))) * jnp.exp(lb))
            dk_scr[hh, pl.ds(start, blk), :] += _dot_tn(dzb, qs[hh])
            dv_scr[hh, pl.ds(start, blk), :] += _dot_tn(_mx(a), dobs[hh])
            return _dot(dzb, k), pg + _rowsum(gm)

        def step(t, c):
            acc0, pg0, acc1, pg1 = c
            dq0, pg0 = block(0, i - n + t, pg0)
            dq1, pg1 = block(1, i - n + t, pg1)
            return acc0 + dq0, pg0, acc1 + dq1, pg1

        zero = jnp.zeros((blk, 1), F32)
        zacc = jnp.zeros((blk, hd), F32)
        acc0, pg0, acc1, pg1 = lax.fori_loop(0, n - 1, step, (zacc, zero, zacc, zero))
        logit_grads(0, pg0, gp0)
        logit_grads(1, pg1, gp1)
        products(0, acc0)
        products(1, acc1)

        @pl.when(i == nq - 1)
        def _():
            for hh in range(2):
                ls = slice(hh * hd, (hh + 1) * hd)
                dk_ref[:, ls] = dk_scr[hh].astype(dk_ref.dtype)
                dv_ref[:, ls] = dv_scr[hh].astype(dv_ref.dtype)

    qblk = lambda c0: pl.BlockSpec((blk, LANES), lambda p, i: (i, c0 + p))
    full = lambda c0: pl.BlockSpec((s, LANES), lambda p, i: (0, c0 + p))
    half = jax.ShapeDtypeStruct((s, 512), MXU_DTYPE)
    return pl.pallas_call(
        body, name="stickbreak_bwd", grid=(SB_HEADS // 2, nq),
        in_specs=[qblk(0), full(4), full(8), qblk(0), qblk(0), qblk(4)],
        out_specs=[qblk(0), full(0), full(0), qblk(0)],
        out_shape=[half, half, half, half],
        scratch_shapes=[pltpu.VMEM((2, s, hd), F32), pltpu.VMEM((2, s, hd), F32),
                        pltpu.VMEM((2, blk, 2 * blk), F32),
                        pltpu.VMEM((2, blk, 2 * blk), F32),
                        pltpu.VMEM((2, 2 * blk, blk), MXU_DTYPE),
                        pltpu.VMEM((2, 2 * blk, blk), F32),
                        pltpu.VMEM((2, blk, 2 * blk), MXU_DTYPE),
                        pltpu.VMEM((2, blk, 2 * blk), MXU_DTYPE)],
        compiler_params=_params(("arbitrary", "arbitrary")),
    )(qkv, qkv, qkv, sg, o, dy)


def _layer_fwd(layer, x, mod, norm_g, w_in_g, w_out_g, tables, gather=(), loss_head=None):
    shift, scale1p, gate = mod[0:1], 1.0 + mod[1:2], mod[2:3]
    ret, qkv, sg, *gathered = _ln_proj(x, shift, scale1p, norm_g, w_in_g, layer, gather)
    y_r, states = _retention_fwd(ret, tables)
    y_s, o_s = _sb_fwd(qkv, sg)
    x_next = _out_proj(x, gate, y_r, y_s, w_out_g, layer, loss_head)
    saved = (x, shift, scale1p, gate, ret, qkv, sg, y_r, states, y_s, o_s)
    return x_next, saved, gathered


def _layer_bwd(layer, me, dx_out, saved, norm_g, w_in_g, w_out_g, tables):
    x, shift, scale1p, gate, ret, qkv, sg, y_r, states, y_s, o_s = saved
    dy, dw_out, dgate = _out_proj_bwd(dx_out, gate, y_r, y_s, w_out_g, layer)
    d_ret = _retention_bwd(ret, states, dy, tables)
    d_q, d_k, d_v, d_g = _sb_bwd(qkv, sg, o_s, dy)
    dproj = (d_ret, d_q, d_k, d_v, d_g)
    dwo_parts = _mx(dw_out.reshape(N_DEV, D_MODEL // N_DEV, D_MODEL))
    dx, dshift, dscale, dnorm_g, h_t, r_out = _in_proj_bwd_x(
        x, dx_out, dproj, shift, scale1p, norm_g, w_in_g, layer, dwo_parts)
    r_in = _in_proj_bwd_w(me, h_t, dproj)
    dmod = jnp.concatenate([dshift, dscale, dgate], axis=1)
    return dx, r_in, r_out, dmod, dnorm_g


MESH_IDS = pl.DeviceIdType.MESH
N_PEERS = N_DEV - 1
HBM_SPEC = pl.BlockSpec(memory_space=pl.ANY)


def _my_place():
    return lax.axis_index("x"), lax.axis_index("y"), lax.axis_index("c")


def _linear(px, py, pc):
    return 4 * px + 2 * py + pc


def _all_gather(blocks):
    n_arr = len(blocks)

    def body(*refs):
        start, forward, finish = _gather_plan(refs[:n_arr], refs[n_arr:2 * n_arr], *refs[2 * n_arr:])
        start()
        forward()
        finish()

    return pl.pallas_call(
        body, name="all_gather",
        out_shape=_gathered_shapes(blocks),
        in_specs=[HBM_SPEC] * n_arr, out_specs=[HBM_SPEC] * n_arr,
        scratch_shapes=_gather_sems(n_arr),
    )(*blocks)


def _gathered_shapes(blocks):
    return [jax.ShapeDtypeStruct((N_DEV * b.shape[0], b.shape[1]), b.dtype) for b in blocks]


def _gather_sems(n_arr):
    return [pltpu.SemaphoreType.DMA((n_arr * N_PEERS,)), pltpu.SemaphoreType.DMA((n_arr * N_PEERS,)),
            pltpu.SemaphoreType.DMA((n_arr,))]


def _gather_plan(x_refs, out_refs, send_sems, recv_sems, local_sems):
    n_arr = len(x_refs)
    x, y, c = _my_place()
    me, sibling = (x, y, c), (x, y, 1 - c)
    chips = [(1 - x, y), (x, 1 - y), (1 - x, 1 - y)]

    def rows(a, place):
        m = x_refs[a].shape[0]
        return out_refs[a].at[pl.ds(_linear(*place) * m, m), :]

    def copy(a, k, block, to, src=None):
        return pltpu.make_async_remote_copy(
            src_ref=rows(a, block) if src is None else src, dst_ref=rows(a, block),
            send_sem=send_sems.at[a * N_PEERS + k], recv_sem=recv_sems.at[a * N_PEERS + k],
            device_id=to, device_id_type=MESH_IDS)

    mine = [pltpu.make_async_copy(x_refs[a], rows(a, me), local_sems.at[a]) for a in range(n_arr)]
    first = []
    for a in range(n_arr):
        first.append(copy(a, 0, me, sibling, src=x_refs[a]))
        first += [copy(a, 1 + j, me, (*chip, c), src=x_refs[a]) for j, chip in enumerate(chips)]
    passed = [copy(a, 4 + j, (*chip, c), sibling) for j, chip in enumerate(chips) for a in range(n_arr)]

    def start():
        for cp in mine + first:
            cp.start()

    def forward():
        for j, chip in enumerate(chips):
            for a in range(n_arr):
                copy(a, 1 + j, (*chip, c), me).wait_recv()
                passed[j * n_arr + a].start()

    def finish():
        for a in range(n_arr):
            copy(a, 0, sibling, me).wait_recv()
            for j, chip in enumerate(chips):
                copy(a, 4 + j, (*chip, 1 - c), me).wait_recv()
        for cp in first + passed:
            cp.wait_send()
        for cp in mine:
            cp.wait()

    return start, forward, finish


def _ada_fwd(c_all, w_ada, b_cols):
    cols = w_ada.shape[2]

    def body(c_ref, w_ref, b_ref, ca_ref, mod_ref):
        cv = c_ref[...]
        ca = cv * _sigmoid(cv)
        ca_ref[...] = ca
        cb = _mx(jnp.concatenate([ca, ca], axis=0))
        for l in range(DEPTH):
            mod_ref[l * N_DEV:(l + 1) * N_DEV, :] = _dot(cb, _mx(w_ref[l]))[0:N_DEV] + b_ref[l]

    return pl.pallas_call(
        body, name="ada_fwd",
        out_shape=[jax.ShapeDtypeStruct((N_DEV, D_MODEL), F32),
                   jax.ShapeDtypeStruct((DEPTH * N_DEV, cols), F32)],
        compiler_params=_params(),
    )(c_all, w_ada, b_cols)


def _ada_bwd(c_act_t, dmod_cols):
    cols = dmod_cols.shape[2]

    def body(ca_ref, dm_ref, o_ref):
        ca = _mx(ca_ref[...]).astype(F32)
        for l in range(DEPTH):
            o_ref[l] = jnp.dot(ca, _mx(dm_ref[l]).astype(F32),
                               precision=lax.Precision.HIGHEST, preferred_element_type=F32)

    return pl.pallas_call(
        body, name="ada_bwd",
        out_shape=jax.ShapeDtypeStruct((DEPTH, D_MODEL, cols), F32),
        compiler_params=_params(),
    )(c_act_t, dmod_cols)


def _adamw_store(g, w_ref, m_ref, v_ref, g_ref, d_ref, mo_ref, vo_ref):
    m2 = ADAM_B1 * m_ref[...] + (1.0 - ADAM_B1) * g
    v2 = ADAM_B2 * v_ref[...] + (1.0 - ADAM_B2) * (g * g)
    m_hat = m2 / (1.0 - ADAM_B1 ** ADAM_STEP)
    v_hat = v2 / (1.0 - ADAM_B2 ** ADAM_STEP)
    g_ref[...] = g
    d_ref[...] = -ADAM_LR * (m_hat / (jnp.sqrt(v_hat) + ADAM_EPS) + ADAM_WD * w_ref[...])
    mo_ref[...] = m2
    vo_ref[...] = v2


def _slab_sum(p_ref):
    g = p_ref[0].astype(F32)
    for sl in range(1, p_ref.shape[0]):
        g = g + p_ref[sl].astype(F32)
    return g


def _sum_adamw_layers(parts, w, m, v):
    n_slab, rows, cols = parts[0].shape
    tr = min(256, rows)
    nt = rows // tr

    def body(p0_ref, p1_ref, w_ref, m_ref, v_ref, g_ref, d_ref, mo_ref, vo_ref):
        for l, p_ref in enumerate((p0_ref, p1_ref)):
            @pl.when(pl.program_id(0) == l)
            def _():
                _adamw_store(_slab_sum(p_ref), w_ref, m_ref, v_ref, g_ref, d_ref, mo_ref, vo_ref)

    p_specs = [pl.BlockSpec((n_slab, tr, cols), lambda l, i: (0, i * (1 - l) + (nt - 1) * l, 0)),
               pl.BlockSpec((n_slab, tr, cols), lambda l, i: (0, i * l, 0))]
    blk = pl.BlockSpec((None, tr, cols), lambda l, i: (l, i, 0))
    shp = jax.ShapeDtypeStruct((DEPTH, rows, cols), F32)
    return pl.pallas_call(
        body, name="sum_adamw_layers", grid=(DEPTH, nt),
        in_specs=p_specs + [blk, blk, blk],
        out_specs=[blk, blk, blk, blk],
        out_shape=[shp, shp, shp, shp],
        compiler_params=_params(("arbitrary", "arbitrary")),
    )(parts[0], parts[1], w, m, v)


def _sum_adamw(parts, w, m, v):
    n_slab, rows, cols = parts.shape
    tr = min(256, rows)

    def body(p_ref, w_ref, m_ref, v_ref, g_ref, d_ref, mo_ref, vo_ref):
        _adamw_store(_slab_sum(p_ref), w_ref, m_ref, v_ref, g_ref, d_ref, mo_ref, vo_ref)

    blk = pl.BlockSpec((tr, cols), lambda i: (i, 0))
    shp = jax.ShapeDtypeStruct((rows, cols), F32)
    return pl.pallas_call(
        body, name="sum_adamw", grid=(rows // tr,),
        in_specs=[pl.BlockSpec((n_slab, tr, cols), lambda i: (0, i, 0)), blk, blk, blk],
        out_specs=[blk, blk, blk, blk],
        out_shape=[shp, shp, shp, shp],
        compiler_params=_params(("arbitrary",)),
    )(parts, w, m, v)


SMALL_ROWS = 16


def kernel(x, c, norm_g, w_ada, b_ada, w_in, w_out, final_g, loss_target, m_norm_g, m_w_ada, m_b_ada, m_w_in, m_w_out, m_final_g, v_norm_g, v_w_ada, v_b_ada, v_w_in, v_w_out, v_final_g):
    me = _linear(*_my_place())
    in_cols = w_in.shape[2]
    out_rows = w_out.shape[1]
    ada_cols = w_ada.shape[2]

    w_in_m, w_out_m = _mx(w_in), _mx(w_out)
    g_in, g_out, g_c = _all_gather([w_in_m[0], w_out_m[0], jnp.broadcast_to(c, (8, D_MODEL))])
    w_in_g = [g_in.reshape(N_DEV, 1, D_MODEL, in_cols), None]
    w_out_g = [g_out.reshape(N_DEV, 1, out_rows, D_MODEL), None]
    c_all = g_c.reshape(N_DEV, 8, D_MODEL)[:, 0]

    b_cols = lax.dynamic_slice_in_dim(b_ada, me * ada_cols, ada_cols, axis=1)[:, None, :]
    c_act, mod_cols = _ada_fwd(c_all, w_ada, b_cols)
    (g_mod,) = _all_gather([mod_cols])
    g_mod = g_mod.reshape(N_DEV, DEPTH, N_DEV, ada_cols)
    mod = lax.dynamic_index_in_dim(g_mod, me, axis=2, keepdims=False)
    mod = mod.transpose(1, 0, 2).reshape(DEPTH, 3, D_MODEL)

    tables = _ret_tables(x.shape[1])
    h = x[0]
    saved = []
    for l in range(DEPTH):
        nxt = (w_in_m[l + 1], w_out_m[l + 1]) if l + 1 < DEPTH else ()
        head = None if nxt else (final_g[None], loss_target[0])
        h, sv, gathered = _layer_fwd(0, h, mod[l], norm_g[l:l + 1], w_in_g[l], w_out_g[l], tables, nxt, head)
        if nxt:
            w_in_g[l + 1] = gathered[0].reshape(N_DEV, 1, D_MODEL, in_cols)
            w_out_g[l + 1] = gathered[1].reshape(N_DEV, 1, out_rows, D_MODEL)
        saved.append(sv)
    dx, loss_part, dfg = h
    r_in, r_out, small = [None] * DEPTH, [None] * DEPTH, [None] * DEPTH
    for l in reversed(range(DEPTH)):
        dx, r_in[l], r_out[l], dmod, dng = _layer_bwd(0, me, dx, saved[l], norm_g[l:l + 1], w_in_g[l], w_out_g[l], tables)
        small[l] = (dmod.reshape(3, D_MODEL), dng)

    pad = jnp.zeros((SMALL_ROWS - 10, D_MODEL), F32)
    small_block = jnp.concatenate([small[0][0], small[1][0], small[0][1], small[1][1], dfg,
                                   jnp.broadcast_to(loss_part, (1, D_MODEL)), pad], axis=0)
    (g_small,) = _all_gather([small_block])
    g_small = g_small.reshape(N_DEV, SMALL_ROWS, D_MODEL)

    def small_pack(b, n, f, fill):
        return jnp.concatenate([b.reshape(6, D_MODEL), n, f[None],
                                jnp.full((SMALL_ROWS - 9, D_MODEL), fill, F32)], axis=0)

    s_g, s_d, s_m, s_v = _sum_adamw(g_small, small_pack(b_ada, norm_g, final_g, 0.0),
                                    small_pack(m_b_ada, m_norm_g, m_final_g, 0.0),
                                    small_pack(v_b_ada, v_norm_g, v_final_g, 1.0))
    loss = s_g[9, 0]

    def small_unpack(a):
        return a[0:6].reshape(DEPTH, 3 * D_MODEL), a[6:8], a[8]

    dmod_all = g_small[:, 0:6].reshape(N_DEV, DEPTH, 3 * D_MODEL).transpose(1, 0, 2)
    dmod_cols = lax.dynamic_slice_in_dim(dmod_all, me * ada_cols, ada_cols, axis=2)
    g_ada = _ada_bwd(c_act.T, dmod_cols).reshape(1, DEPTH * D_MODEL, ada_cols)
    ada = _sum_adamw(g_ada, *[a.reshape(DEPTH * D_MODEL, ada_cols) for a in (w_ada, m_w_ada, v_w_ada)])
    ada = [a.reshape(DEPTH, D_MODEL, ada_cols) for a in ada]

    win = _sum_adamw_layers(r_in, w_in, m_w_in, v_w_in)
    wout = _sum_adamw_layers(r_out, w_out, m_w_out, v_w_out)

    outs = [loss, dx[None]]
    for k in range(4):
        b, n, f = small_unpack((s_g, s_d, s_m, s_v)[k])
        outs += [n, ada[k], b, win[k], wout[k], f]
    return tuple(outs)
```

```python
import functools

import jax
import jax.numpy as jnp
from jax import lax
from jax.experimental import pallas as pl
from jax.experimental.pallas import tpu as pltpu

F32 = jnp.float32
MXU_DTYPE = jnp.bfloat16

D_MODEL = 1024
DEPTH = 2
N_DEV = 8
CHUNK = 64
D_RET = 512
D_SB = 512
RET_HEADS = 4
RET_HEAD_DIM = 128
SB_HEADS = 8
SB_HEAD_DIM = 64
D_IN = 4096
ROPE_BASE = 10000.0
EPS = 1e-6
SB_SCALE = SB_HEAD_DIM ** -0.5
RET_KSCALE = RET_HEAD_DIM ** -0.5

ADAM_LR = 0.001
ADAM_B1 = 0.9
ADAM_B2 = 0.999
ADAM_EPS = 1e-08
ADAM_WD = 0.01
ADAM_STEP = 10

V7X_VMEM_BYTES = 64 * 2 ** 20
VMEM_LIMIT = V7X_VMEM_BYTES - 8 * 2 ** 20
LANES = 128

_NT = (((1,), (1,)), ((), ()))
_TN = (((0,), (0,)), ((), ()))


def _dot(a, b):
    return jnp.dot(a, b, preferred_element_type=F32)


def _dot_nt(a, b):
    return lax.dot_general(a, b, _NT, preferred_element_type=F32)


def _dot_tn(a, b):
    return lax.dot_general(a, b, _TN, preferred_element_type=F32)


def _mx(x):
    return x.astype(MXU_DTYPE)


def _sigmoid(x):
    return 1.0 / (1.0 + jnp.exp(-x))


def _params(sem=None):
    return pltpu.CompilerParams(dimension_semantics=sem, vmem_limit_bytes=VMEM_LIMIT)


def _row_tile(s):
    return min(512, s)


def _w_in_spec(w_in_g, layer):
    return pl.BlockSpec((N_DEV, None) + w_in_g.shape[2:], lambda i: (0, layer, 0, 0))


def _w_out_spec(w_out_g, layer):
    return pl.BlockSpec((N_DEV, None) + w_out_g.shape[2:], lambda i: (0, layer, 0, 0))


def _ln_proj(x, shift, scale1p, g, w_in_g, layer, gather=()):
    s = x.shape[0]
    ts = _row_tile(s)
    ns = s // ts
    n_g = len(gather)

    def body(x_ref, sh_ref, sc_ref, g_ref, w_ref, *rest):
        ret_ref, qkv_ref, sg_ref = rest[n_g:n_g + 3]
        if n_g:
            start, forward, finish = _gather_plan(rest[:n_g], rest[n_g + 3:2 * n_g + 3], *rest[2 * n_g + 3:])
            i = pl.program_id(0)
            pl.when(i == 0)(start)
            pl.when(i == max(ns - 4, 0))(forward)
        xv = x_ref[...]
        rstd = lax.rsqrt(jnp.mean(xv * xv, axis=-1, keepdims=True) + EPS)
        h = (xv * rstd * g_ref[...]) * sc_ref[...] + sh_ref[...]
        hb = _mx(h)
        for n in range(4):
            ret_ref[:, n * 512:(n + 1) * 512] = _dot(hb, w_ref[n])
        qkv_ref[:, 0:512] = _mx(_dot(hb, w_ref[4]) * SB_SCALE)
        qkv_ref[:, 512:1024] = _mx(_dot(hb, w_ref[5]))
        qkv_ref[:, 1024:1536] = _mx(_dot(hb, w_ref[6]))
        sg_ref[...] = _dot(hb, w_ref[7])
        if n_g:
            pl.when(i == ns - 1)(finish)

    vec = pl.BlockSpec((1, D_MODEL), lambda i: (0, 0))
    return pl.pallas_call(
        body, name="ln_proj_gather" if n_g else "ln_proj", grid=(ns,),
        in_specs=[pl.BlockSpec((ts, D_MODEL), lambda i: (i, 0)), vec, vec, vec,
                  _w_in_spec(w_in_g, layer)] + [HBM_SPEC] * n_g,
        out_specs=[pl.BlockSpec((ts, 2048), lambda i: (i, 0)),
                   pl.BlockSpec((ts, 1536), lambda i: (i, 0)),
                   pl.BlockSpec((ts, 512), lambda i: (i, 0))] + [HBM_SPEC] * n_g,
        out_shape=[jax.ShapeDtypeStruct((s, 2048), F32),
                   jax.ShapeDtypeStruct((s, 1536), MXU_DTYPE),
                   jax.ShapeDtypeStruct((s, 512), F32)] + _gathered_shapes(gather),
        scratch_shapes=_gather_sems(n_g) if n_g else (),
        compiler_params=_params(("arbitrary",)),
    )(x, shift, scale1p, g, w_in_g, *gather)


def _w_out_halves(w_ref):
    half = N_DEV // 2
    return (w_ref[0:half].reshape(D_RET, D_MODEL), w_ref[half:N_DEV].reshape(D_SB, D_MODEL))


def _out_proj(x, gate, y_r, y_s, w_out_g, layer, loss_head=None):
    s = x.shape[0]
    ts = _row_tile(s)

    def layer_out(x_ref, gate_ref, yr_ref, ys_ref, w_ref):
        w_r, w_s = _w_out_halves(w_ref)
        return x_ref[...] + gate_ref[...] * (_dot(yr_ref[...], w_r) + _dot(ys_ref[...], w_s))

    def body(x_ref, gate_ref, yr_ref, ys_ref, w_ref, o_ref):
        o_ref[...] = layer_out(x_ref, gate_ref, yr_ref, ys_ref, w_ref)

    def body_loss(x_ref, gate_ref, yr_ref, ys_ref, w_ref, fg_ref, t_ref, dx_ref, loss_ref, dfg_ref):
        @pl.when(pl.program_id(0) == 0)
        def _():
            loss_ref[...] = jnp.zeros_like(loss_ref)
            dfg_ref[...] = jnp.zeros_like(dfg_ref)

        xv = layer_out(x_ref, gate_ref, yr_ref, ys_ref, w_ref)
        fgv = fg_ref[...]
        rstd = lax.rsqrt(jnp.mean(xv * xv, axis=-1, keepdims=True) + EPS)
        xn = xv * rstd
        err = xn * fgv - t_ref[...]
        tok = jnp.mean(err * err, axis=-1, keepdims=True)
        loss_ref[...] += 0.5 * jnp.sum(tok, axis=0, keepdims=True)
        dy = err * (1.0 / D_MODEL)
        dfg_ref[...] += jnp.sum(dy * xn, axis=0, keepdims=True)
        dxn = dy * fgv
        dx_ref[...] = rstd * (dxn - xn * jnp.mean(dxn * xn, axis=-1, keepdims=True))

    rows = pl.BlockSpec((ts, D_MODEL), lambda i: (i, 0))
    vec = pl.BlockSpec((1, D_MODEL), lambda i: (0, 0))
    in_specs = [rows, vec, pl.BlockSpec((ts, 512), lambda i: (i, 0)), pl.BlockSpec((ts, 512), lambda i: (i, 0)),
                _w_out_spec(w_out_g, layer)]
    if loss_head is None:
        return pl.pallas_call(
            body, name="out_proj", grid=(s // ts,), in_specs=in_specs, out_specs=rows,
            out_shape=jax.ShapeDtypeStruct((s, D_MODEL), F32),
            compiler_params=_params(("arbitrary",)),
        )(x, gate, y_r, y_s, w_out_g)
    return pl.pallas_call(
        body_loss, name="out_proj_loss", grid=(s // ts,), in_specs=in_specs + [vec, rows],
        out_specs=[rows, pl.BlockSpec((1, 1), lambda i: (0, 0)), vec],
        out_shape=[jax.ShapeDtypeStruct((s, D_MODEL), F32),
                   jax.ShapeDtypeStruct((1, 1), F32),
                   jax.ShapeDtypeStruct((1, D_MODEL), F32)],
        compiler_params=_params(("arbitrary",)),
    )(x, gate, y_r, y_s, w_out_g, *loss_head)


def _out_proj_bwd(dx_out, gate, y_r, y_s, w_out_g, layer):
    s = dx_out.shape[0]
    ts = _row_tile(s)

    def body(dx_ref, gate_ref, yr_ref, ys_ref, w_ref, dy_ref, dw_ref, dgate_ref):
        i = pl.program_id(0)

        @pl.when(i == 0)
        def _():
            dw_ref[...] = jnp.zeros_like(dw_ref)
            dgate_ref[...] = jnp.zeros_like(dgate_ref)

        dxv = dx_ref[...]
        dt = _mx(dxv * gate_ref[...])
        yr = yr_ref[...]
        ys = ys_ref[...]
        w_r, w_s = _w_out_halves(w_ref)
        dy_ref[:, 0:512] = _dot_nt(dt, w_r)
        dy_ref[:, 512:1024] = _dot_nt(dt, w_s)
        dw_ref[0:512, :] += _dot_tn(yr, dt)
        dw_ref[512:1024, :] += _dot_tn(ys, dt)
        t = _dot(yr, w_r) + _dot(ys, w_s)
        dgate_ref[...] += jnp.sum(dxv * t, axis=0, keepdims=True)

    return pl.pallas_call(
        body, name="out_proj_bwd", grid=(s // ts,),
        in_specs=[pl.BlockSpec((ts, D_MODEL), lambda i: (i, 0)),
                  pl.BlockSpec((1, D_MODEL), lambda i: (0, 0)),
                  pl.BlockSpec((ts, 512), lambda i: (i, 0)),
                  pl.BlockSpec((ts, 512), lambda i: (i, 0)),
                  _w_out_spec(w_out_g, layer)],
        out_specs=[pl.BlockSpec((ts, D_MODEL), lambda i: (i, 0)),
                   pl.BlockSpec((D_MODEL, D_MODEL), lambda i: (0, 0)),
                   pl.BlockSpec((1, D_MODEL), lambda i: (0, 0))],
        out_shape=[jax.ShapeDtypeStruct((s, D_MODEL), F32),
                   jax.ShapeDtypeStruct((D_MODEL, D_MODEL), F32),
                   jax.ShapeDtypeStruct((1, D_MODEL), F32)],
        compiler_params=_params(("arbitrary",)),
    )(dx_out, gate, y_r, y_s, w_out_g)


def _scatter_plan(parts_ref, recv_ref, send_sems, recv_sems, local_sem):
    px, py, pc = _my_place()
    mine = _linear(px, py, pc)

    def copy(r):
        peer = (1 - px if r & 4 else px, 1 - py if r & 2 else py, 1 - pc if r & 1 else pc)
        return pltpu.make_async_remote_copy(
            src_ref=parts_ref.at[_linear(*peer)], dst_ref=recv_ref.at[mine],
            send_sem=send_sems.at[r - 1], recv_sem=recv_sems.at[r - 1],
            device_id=peer, device_id_type=MESH_IDS)

    own = pltpu.make_async_copy(parts_ref.at[mine], recv_ref.at[mine], local_sem.at[0])

    def start():
        own.start()
        for r in range(1, N_DEV):
            copy(r).start()

    def finish():
        for r in range(1, N_DEV):
            copy(r).wait_recv()
            copy(r).wait_send()
        own.wait()

    return start, finish


def _in_proj_bwd_x(x, dx_out, dproj, shift, scale1p, g, w_in_g, layer):
    s = x.shape[0]
    ts = _row_tile(s)
    ns = s // ts
    nb = D_IN // N_DEV

    def body(x_ref, dxo_ref, dr_ref, d4_ref, d5_ref, d6_ref, d7_ref, sh_ref, sc_ref, g_ref, w_ref,
             dx_ref, dsh_ref, dsc_ref, dg_ref, ht_ref):
        i = pl.program_id(0)

        @pl.when(i == 0)
        def _():
            dsh_ref[...] = jnp.zeros_like(dsh_ref)
            dsc_ref[...] = jnp.zeros_like(dsc_ref)
            dg_ref[...] = jnp.zeros_like(dg_ref)

        dh = _dot_nt(dr_ref[:, 0:nb], w_ref[0])
        for n in range(1, 4):
            dh += _dot_nt(dr_ref[:, n * nb:(n + 1) * nb], w_ref[n])
        for n, d_ref in zip(range(4, N_DEV), (d4_ref, d5_ref, d6_ref, d7_ref)):
            dh += _dot_nt(d_ref[...], w_ref[n])
        xv = x_ref[...]
        gv = g_ref[...]
        scv = sc_ref[...]
        rstd = lax.rsqrt(jnp.mean(xv * xv, axis=-1, keepdims=True) + EPS)
        xn = xv * rstd
        xg = xn * gv
        ht_ref[...] = _mx((xg * scv + sh_ref[...]).T)
        dsh_ref[...] += jnp.sum(dh, axis=0, keepdims=True)
        dsc_ref[...] += jnp.sum(dh * xg, axis=0, keepdims=True)
        dhs = dh * scv
        dg_ref[...] += jnp.sum(dhs * xn, axis=0, keepdims=True)
        dxn = dhs * gv
        dx_ref[...] = rstd * (dxn - xn * jnp.mean(dxn * xn, axis=-1, keepdims=True)) + dxo_ref[...]

    vec = pl.BlockSpec((1, D_MODEL), lambda i: (0, 0))
    return pl.pallas_call(
        body, name="in_proj_bwd_x", grid=(ns,),
        in_specs=[pl.BlockSpec((ts, D_MODEL), lambda i: (i, 0)),
                  pl.BlockSpec((ts, D_MODEL), lambda i: (i, 0)),
                  pl.BlockSpec((ts, 4 * nb), lambda i: (i, 0))]
                 + [pl.BlockSpec((ts, nb), lambda i: (i, 0))] * 4
                 + [vec, vec, vec, _w_in_spec(w_in_g, layer)],
        out_specs=[pl.BlockSpec((ts, D_MODEL), lambda i: (i, 0)), vec, vec, vec,
                   pl.BlockSpec((D_MODEL, ts), lambda i: (0, i))],
        out_shape=[jax.ShapeDtypeStruct((s, D_MODEL), F32),
                   jax.ShapeDtypeStruct((1, D_MODEL), F32),
                   jax.ShapeDtypeStruct((1, D_MODEL), F32),
                   jax.ShapeDtypeStruct((1, D_MODEL), F32),
                   jax.ShapeDtypeStruct((D_MODEL, s), MXU_DTYPE)],
        compiler_params=_params(("arbitrary",)),
    )(x, dx_out, *dproj, shift, scale1p, g, w_in_g)


def _in_proj_bwd_w(me, h_t, dproj):
    s = h_t.shape[1]
    ts = min(4 * _row_tile(s), s)
    ns = s // ts
    nb = D_IN // N_DEV
    n_chip = N_DEV // 2

    def flip_bits(j):
        return jnp.where(j == 0, 4, jnp.where(j == 1, 2, jnp.where(j == 2, 6, 0)))

    def slab_of(t, me_ref):
        return jnp.bitwise_xor(me_ref[0], flip_bits(t // 2) + 1 - t % 2)

    def body(me_ref, ht_ref, dr_ref, d4_ref, d5_ref, d6_ref, d7_ref, rin_ref,
             acc, stage, pre_buf, pre_send, pre_recv, sum_send, sum_recv, local_sem):
        t = pl.program_id(0)
        i = pl.program_id(1)
        j = t // 2
        summing = t % 2 == 1
        slab = slab_of(t, me_ref)
        px, py, pc = _my_place()

        def pre_copy(jj):
            return pltpu.make_async_remote_copy(
                src_ref=stage.at[jj % 2], dst_ref=pre_buf.at[jj],
                send_sem=pre_send.at[jj], recv_sem=pre_recv.at[jj],
                device_id=(px, py, 1 - pc), device_id_type=MESH_IDS)

        def sum_copy(jj):
            fx = jnp.logical_or(jj == 0, jj == 2)
            fy = jnp.logical_or(jj == 1, jj == 2)
            return pltpu.make_async_remote_copy(
                src_ref=stage.at[2 + jj % 2], dst_ref=rin_ref.at[jj],
                send_sem=sum_send.at[jj], recv_sem=sum_recv.at[jj],
                device_id=(jnp.where(fx, 1 - px, px), jnp.where(fy, 1 - py, py), pc), device_id_type=MESH_IDS)

        own = pltpu.make_async_copy(stage.at[3], rin_ref.at[n_chip - 1], local_sem.at[0])

        @pl.when(i == 0)
        def _():
            acc[...] = jnp.zeros_like(acc)

        @pl.when(slab < 4)
        def _():
            acc[...] += _dot(ht_ref[...], dr_ref[...])

        for n, d_ref in zip(range(4, N_DEV), (d4_ref, d5_ref, d6_ref, d7_ref)):
            @pl.when(slab == n)
            def _():
                acc[...] += _dot(ht_ref[...], d_ref[...])

        @pl.when(jnp.logical_and(i == ns - 1, jnp.logical_not(summing)))
        def _():
            @pl.when(j >= 2)
            def _():
                pre_copy(j - 2).wait_send()

            stage[j % 2] = acc[...].astype(stage.dtype)
            pre_copy(j).start()

        @pl.when(jnp.logical_and(i == ns - 1, summing))
        def _():
            pre_copy(j).wait_recv()

            @pl.when(j >= 2)
            def _():
                sum_copy(j - 2).wait_send()

            stage[2 + j % 2] = (acc[...] + pre_buf[j].astype(F32)).astype(stage.dtype)

            @pl.when(j < n_chip - 1)
            def _():
                sum_copy(j).start()

            @pl.when(j == n_chip - 1)
            def _():
                own.start()
                pre_copy(n_chip - 2).wait_send()
                pre_copy(n_chip - 1).wait_send()
                sum_copy(n_chip - 2).wait_send()
                for jj in range(n_chip - 1):
                    sum_copy(jj).wait_recv()
                own.wait()

    def part_rows(n, t, i, me_ref):
        return jnp.where(slab_of(t, me_ref) == n, i, ns - 1), 0

    return pl.pallas_call(
        body, name="in_proj_bwd_w",
        grid_spec=pltpu.PrefetchScalarGridSpec(
            num_scalar_prefetch=1, grid=(N_DEV, ns),
            in_specs=[pl.BlockSpec((D_MODEL, ts), lambda t, i, me_ref: (0, i)),
                      pl.BlockSpec((ts, nb), lambda t, i, me_ref: (
                          jnp.where(slab_of(t, me_ref) < 4, i, ns - 1), jnp.minimum(slab_of(t, me_ref), 3)))]
                     + [pl.BlockSpec((ts, nb), functools.partial(part_rows, n)) for n in range(4, N_DEV)],
            out_specs=HBM_SPEC,
            scratch_shapes=[pltpu.VMEM((D_MODEL, nb), F32),
                            pltpu.VMEM((4, D_MODEL, nb), MXU_DTYPE),
                            pltpu.VMEM((n_chip, D_MODEL, nb), MXU_DTYPE),
                            pltpu.SemaphoreType.DMA((n_chip,)), pltpu.SemaphoreType.DMA((n_chip,)),
                            pltpu.SemaphoreType.DMA((n_chip - 1,)), pltpu.SemaphoreType.DMA((n_chip - 1,)),
                            pltpu.SemaphoreType.DMA((1,))]),
        out_shape=jax.ShapeDtypeStruct((n_chip, D_MODEL, nb), MXU_DTYPE),
        compiler_params=_params(("arbitrary", "arbitrary")),
    )(jnp.reshape(me, (1,)).astype(jnp.int32), h_t, *dproj)


RET_TILE = 256


def _ret_tables(s):
    t = min(RET_TILE, s)
    half = RET_HEAD_DIM // 2
    pos = jnp.arange(s, dtype=F32)
    inv = ROPE_BASE ** (-jnp.arange(half, dtype=F32) / half)
    ang = pos[:, None] * inv[None, :]
    cos, sin = jnp.cos(ang), jnp.sin(ang)
    cos2 = jnp.concatenate([cos, cos], axis=1)
    sin2 = jnp.concatenate([-sin, sin], axis=1)
    lg = jnp.log1p(-(2.0 ** (-5.0 - jnp.arange(RET_HEADS, dtype=F32))))[:, None, None]
    n = jnp.arange(t)
    dist = (n[:, None] - n[None, :]).astype(F32)[None]
    cn = (n // CHUNK)[:, None]
    cm = (n // CHUNK)[None, :]
    mask = jnp.where((cn == cm)[None], jnp.exp(jnp.abs(dist) * lg),
                     jnp.where((cm < cn)[None], jnp.exp(dist * lg), 0.0))
    nf = n.astype(F32)[None, :, None]
    dq = jnp.broadcast_to(jnp.exp((nf + 1.0) * lg), (RET_HEADS, t, LANES))
    dk = jnp.broadcast_to(jnp.exp((t - 1.0 - nf) * lg), (RET_HEADS, t, LANES))
    gt = jnp.broadcast_to(jnp.exp(float(t) * lg), (RET_HEADS, 1, LANES))
    return cos2, sin2, mask, dq, dk, gt


def _roll_half(x):
    return pltpu.roll(x, RET_HEAD_DIM // 2, 1)


def _ret_heads_fwd(ret_ref, cos, sin, m_ref, dq_ref, dk_ref, s0):
    hd = RET_HEAD_DIM
    heads = range(RET_HEADS)
    qb, kb, vb, kdb = [], [], [], []
    for h in heads:
        q = ret_ref[:, h * hd:(h + 1) * hd]
        k = ret_ref[:, 512 + h * hd:512 + (h + 1) * hd]
        kr = (k * cos + _roll_half(k) * sin) * RET_KSCALE
        qb.append(_mx(q * cos + _roll_half(q) * sin))
        kb.append(_mx(kr))
        kdb.append(_mx(kr * dk_ref[h]))
        vb.append(_mx(ret_ref[:, 1024 + h * hd:1024 + (h + 1) * hd]))
    p = [_dot_nt(qb[h], kb[h]) for h in heads]
    cross = [_dot(qb[h], _mx(s0[h])) for h in heads]
    pb = [_mx(p[h] * m_ref[h]) for h in heads]
    o = [_dot(pb[h], vb[h]) + cross[h] * dq_ref[h] for h in heads]
    gn, rstd = [], []
    for h in heads:
        oc = o[h] - jnp.mean(o[h], axis=-1, keepdims=True)
        rstd.append(lax.rsqrt(jnp.mean(oc * oc, axis=-1, keepdims=True) + EPS))
        gn.append(oc * rstd[h])
    return qb, kb, vb, pb, kdb, gn, rstd


def _retention_fwd(ret, tables):
    cos2, sin2, mask, dq, dk, gt = tables
    s = ret.shape[0]
    t = mask.shape[1]
    nt = s // t
    hd = RET_HEAD_DIM

    def body(ret_ref, cos_ref, sin_ref, m_ref, dq_ref, dk_ref, gt_ref, y_ref, st_ref, s_scr):
        i = pl.program_id(0)

        @pl.when(i == 0)
        def _():
            s_scr[...] = jnp.zeros_like(s_scr)

        s0 = [s_scr[h] for h in range(RET_HEADS)]
        _, _, vb, _, kdb, gn, _ = _ret_heads_fwd(ret_ref, cos_ref[...], sin_ref[...], m_ref, dq_ref, dk_ref, s0)
        kv = [_dot_tn(kdb[h], vb[h]) for h in range(RET_HEADS)]
        for h in range(RET_HEADS):
            g = ret_ref[:, 1536 + h * hd:1536 + (h + 1) * hd]
            st_ref[h] = s0[h]
            y_ref[:, h * hd:(h + 1) * hd] = (gn[h] * (g * _sigmoid(g))).astype(y_ref.dtype)
            s_scr[h] = s0[h] * gt_ref[h] + kv[h]

    full3 = lambda a: pl.BlockSpec(a.shape, lambda i: (0, 0, 0))
    return pl.pallas_call(
        body, name="retention_fwd", grid=(nt,),
        in_specs=[pl.BlockSpec((t, 2048), lambda i: (i, 0)),
                  pl.BlockSpec((t, LANES), lambda i: (i, 0)),
                  pl.BlockSpec((t, LANES), lambda i: (i, 0)),
                  full3(mask), full3(dq), full3(dk), full3(gt)],
        out_specs=[pl.BlockSpec((t, 512), lambda i: (i, 0)),
                   pl.BlockSpec((None, RET_HEADS, hd, hd), lambda i: (i, 0, 0, 0))],
        out_shape=[jax.ShapeDtypeStruct((s, 512), MXU_DTYPE),
                   jax.ShapeDtypeStruct((nt, RET_HEADS, hd, hd), F32)],
        scratch_shapes=[pltpu.VMEM((RET_HEADS, hd, hd), F32)],
        compiler_params=_params(("arbitrary",)),
    )(ret, cos2, sin2, mask, dq, dk, gt)


def _retention_bwd(ret, states, dy, tables, dwo_parts):
    cos2, sin2, mask, dq, dk, gt = tables
    s = ret.shape[0]
    t = mask.shape[1]
    nt = s // t
    hd = RET_HEAD_DIM

    def body(ret_ref, st_ref, dy_ref, cos_ref, sin_ref, m_ref, dq_ref, dk_ref, gt_ref, dwo_ref,
             d_ref, rout_ref, ds_scr, send_sems, recv_sems, local_sem):
        i = pl.program_id(0)
        start, finish = _scatter_plan(dwo_ref, rout_ref, send_sems, recv_sems, local_sem)

        @pl.when(i == 0)
        def _():
            start()
            ds_scr[...] = jnp.zeros_like(ds_scr)

        cos = cos_ref[...]
        sin = sin_ref[...]
        heads = range(RET_HEADS)
        s0 = [st_ref[h] for h in heads]
        ds = [ds_scr[h] for h in heads]
        dsb = [_mx(ds[h]) for h in heads]
        qb, kb, vb, pb, kdb, gn, rstd = _ret_heads_fwd(ret_ref, cos, sin, m_ref, dq_ref, dk_ref, s0)
        dob, dodb = [], []
        for h in heads:
            g = ret_ref[:, 1536 + h * hd:1536 + (h + 1) * hd]
            dyv = dy_ref[:, h * hd:(h + 1) * hd]
            sg = _sigmoid(g)
            d_ref[:, 1536 + h * hd:1536 + (h + 1) * hd] = (
                dyv * gn[h] * (sg * (1.0 + g * (1.0 - sg)))).astype(d_ref.dtype)
            dgn = dyv * (g * sg)
            do = rstd[h] * (dgn - jnp.mean(dgn, axis=-1, keepdims=True)
                            - gn[h] * jnp.mean(dgn * gn[h], axis=-1, keepdims=True))
            dob.append(_mx(do))
            dodb.append(_mx(do * dq_ref[h]))
        dp = [_dot_nt(dob[h], vb[h]) for h in heads]
        dv = [_dot_tn(pb[h], dob[h]) + _dot(kdb[h], dsb[h]) for h in heads]
        dq_cross = [_dot_nt(dodb[h], _mx(s0[h])) for h in heads]
        dk_cross = [_dot_nt(vb[h], dsb[h]) for h in heads]
        ds_new = [_dot_tn(qb[h], dodb[h]) for h in heads]
        dpb = [_mx(dp[h] * m_ref[h]) for h in heads]
        dqr = [_dot(dpb[h], kb[h]) + dq_cross[h] for h in heads]
        dkr = [(_dot_tn(dpb[h], qb[h]) + dk_cross[h] * dk_ref[h]) * RET_KSCALE for h in heads]
        for h in heads:
            d_ref[:, 1024 + h * hd:1024 + (h + 1) * hd] = dv[h].astype(d_ref.dtype)
            d_ref[:, h * hd:(h + 1) * hd] = (dqr[h] * cos + _roll_half(dqr[h] * sin)).astype(d_ref.dtype)
            d_ref[:, 512 + h * hd:512 + (h + 1) * hd] = (
                dkr[h] * cos + _roll_half(dkr[h] * sin)).astype(d_ref.dtype)
            ds_scr[h] = ds[h] * gt_ref[h] + ds_new[h]
        pl.when(i == nt - 1)(finish)

    full3 = lambda a: pl.BlockSpec(a.shape, lambda i: (0, 0, 0))
    rev = lambda i: (nt - 1 - i, 0)
    return pl.pallas_call(
        body, name="retention_bwd", grid=(nt,),
        in_specs=[pl.BlockSpec((t, 2048), rev),
                  pl.BlockSpec((None, RET_HEADS, hd, hd), lambda i: (nt - 1 - i, 0, 0, 0)),
                  pl.BlockSpec((t, 512), rev),
                  pl.BlockSpec((t, LANES), rev),
                  pl.BlockSpec((t, LANES), rev),
                  full3(mask), full3(dq), full3(dk), full3(gt), HBM_SPEC],
        out_specs=[pl.BlockSpec((t, 2048), rev), HBM_SPEC],
        out_shape=[jax.ShapeDtypeStruct((s, 2048), MXU_DTYPE),
                   jax.ShapeDtypeStruct(dwo_parts.shape, dwo_parts.dtype)],
        scratch_shapes=[pltpu.VMEM((RET_HEADS, hd, hd), F32),
                        pltpu.SemaphoreType.DMA((N_PEERS,)), pltpu.SemaphoreType.DMA((N_PEERS,)),
                        pltpu.SemaphoreType.DMA((1,))],
        compiler_params=_params(("arbitrary",)),
    )(ret, states, dy, cos2, sin2, mask, dq, dk, gt, dwo_parts)


SB_BLOCK = 256
SB_CHUNK = 32


SB_SKIP = 104.0
NO_KEYS = -1e30


def _split_dots(xs, u):
    parts = []
    for x in xs:
        hi = lax.bitcast_convert_type(lax.bitcast_convert_type(x, jnp.uint32) & jnp.uint32(0xFFFF0000), F32)
        parts += [_mx(hi), _mx(x - hi)]
    out = _dot(jnp.concatenate(parts, axis=0), u)
    n = xs[0].shape[0]
    return [out[2 * k * n:(2 * k + 1) * n] + out[(2 * k + 1) * n:(2 * k + 2) * n] for k in range(len(xs))]


def _split_dot(x, u):
    return _split_dots([x], u)[0]


def _sb_pair_weights(lb, lk, allowed, u_gt):
    blk = lb.shape[0]
    lk_p, lk_d = lk[:, :blk], lk[:, blk:]
    r_d = _rowsum(lk_d)
    cs_p, cs_d = _split_dots([lk_p, lk_d], u_gt)
    a = jnp.exp(lb + jnp.concatenate([cs_p + r_d, cs_d], axis=1))
    return jnp.where(allowed, a, 0.0), r_d, r_d + _rowsum(lk_p)


def _sb_logits(q, k, causal):
    z = _dot_nt(q, k)
    l1p = jnp.log(1.0 + jnp.exp(-jnp.abs(z)))
    lk = -(jnp.maximum(z, 0.0) + l1p)
    if causal is not None:
        lk = jnp.where(causal, lk, 0.0)
    return jnp.minimum(z, 0.0) - l1p, lk


def _sb_weights(lb, lk, r, u_gt, causal):
    a = jnp.exp(lb + _split_dot(lk, u_gt) + r)
    return a if causal is None else jnp.where(causal, a, 0.0)


def _rowsum(x):
    return jnp.sum(x, axis=1, keepdims=True)


def _sb_pair_tile(i, blk):
    row = lax.broadcasted_iota(jnp.int32, (blk, 2 * blk), 0)
    col = lax.broadcasted_iota(jnp.int32, (blk, 2 * blk), 1)
    first_col = jnp.where(i >= 1, 0, blk)
    allowed = jnp.logical_and(row > col - blk, col >= first_col)
    rows_p = pl.ds(pl.multiple_of(jnp.maximum(i - 1, 0) * blk, blk), blk)
    rows_d = pl.ds(pl.multiple_of(i * blk, blk), blk)
    return allowed, rows_p, rows_d


def _sb_fwd(qkv, sg):
    s = qkv.shape[0]
    blk = min(SB_BLOCK, s)
    nq = s // blk
    hd = SB_HEAD_DIM

    ch = min(SB_CHUNK, blk)

    def body(q_ref, k_ref, v_ref, g_ref, y_ref, o_ref, z_scr, lhs_scr, cs_scr, a_scr):
        i = pl.program_id(1)
        row = lax.broadcasted_iota(jnp.int32, (blk, blk), 0)
        col = lax.broadcasted_iota(jnp.int32, (blk, blk), 1)
        u_gt = (row > col).astype(MXU_DTYPE)
        heads = [slice(hh * hd, (hh + 1) * hd) for hh in range(2)]
        qs = [q_ref[:, ls] for ls in heads]
        _, rows_p, rows_d = _sb_pair_tile(i, blk)
        has_prev = i >= 1
        crow = lax.broadcasted_iota(jnp.int32, (ch, blk), 0)
        ccol = lax.broadcasted_iota(jnp.int32, (ch, blk), 1)

        def logits(hh):
            kc = jnp.concatenate([k_ref[rows_p, heads[hh]], k_ref[rows_d, heads[hh]]], axis=0)
            z_scr[hh] = _dot_nt(qs[hh], kc)

        def keep_parts(hh):
            r_d, r_all = [], []
            for c in range(blk // ch):
                rows = pl.ds(c * ch, ch)
                causal = crow + c * ch > ccol
                z = z_scr[hh, rows, :]
                l1p = jnp.log(1.0 + jnp.exp(-jnp.abs(z)))
                lb = jnp.minimum(z, 0.0) - l1p
                z_scr[hh, rows, :] = lb
                lk = lb - z
                lk_p = lk[:, :blk]
                lk_d = jnp.where(causal, lk[:, blk:], 0.0)
                lhs_scr[hh, pl.ds(c * ch, ch), :] = _mx(lk_p)
                lhs_scr[hh, pl.ds(blk + c * ch, ch), :] = _mx(lk_d)
                r_d.append(_rowsum(lk_d))
                r_all.append(r_d[c] + _rowsum(lk_p))
            return r_d, jnp.concatenate(r_all, axis=0)

        def suffix_sums(hh):
            cs_scr[hh] = _dot(lhs_scr[hh], u_gt)

        def weights(hh, r_d):
            for c in range(blk // ch):
                rows = pl.ds(c * ch, ch)
                causal = crow + c * ch > ccol
                cs_p = cs_scr[hh, pl.ds(c * ch, ch), :] + jnp.where(has_prev, r_d[c], NO_KEYS)
                cs_d = cs_scr[hh, pl.ds(blk + c * ch, ch), :]
                lb = z_scr[hh, rows, :]
                a_p = jnp.exp(lb[:, :blk] + cs_p)
                a_d = jnp.where(causal, jnp.exp(lb[:, blk:] + cs_d), 0.0)
                a_scr[hh, rows, :] = _mx(jnp.concatenate([a_p, a_d], axis=1))

        def values(hh):
            vc = jnp.concatenate([v_ref[rows_p, heads[hh]], v_ref[rows_d, heads[hh]]], axis=0)
            return _dot(a_scr[hh], vc)

        def block(hh, j, r):
            start = pl.multiple_of(j * blk, blk)
            lb, lk = _sb_logits(qs[hh], k_ref[pl.ds(start, blk), heads[hh]], None)
            a = _sb_weights(lb, lk, r, u_gt, None)
            return _dot(_mx(a), v_ref[pl.ds(start, blk), heads[hh]]), r + _rowsum(lk)

        def more(n, r0, r1):
            return jnp.logical_and(n < i, jnp.max(jnp.maximum(r0, r1)) > -SB_SKIP)

        logits(0)
        logits(1)
        gates = [g_ref[:, ls] * _sigmoid(g_ref[:, ls]) for ls in heads]
        rd0, r0 = keep_parts(0)
        suffix_sums(0)
        rd1, r1 = keep_parts(1)
        suffix_sums(1)
        go = more(jnp.int32(1), r0, r1)
        weights(0, rd0)
        acc0 = values(0)
        weights(1, rd1)
        acc1 = values(1)

        def step(c):
            _, n, acc0, r0, acc1, r1 = c
            pv0, r0 = block(0, i - 1 - n, r0)
            pv1, r1 = block(1, i - 1 - n, r1)
            return more(n + 1, r0, r1), n + 1, acc0 + pv0, r0, acc1 + pv1, r1

        _, _, acc0, _, acc1, _ = lax.while_loop(lambda c: c[0], step, (go, jnp.int32(1), acc0, r0, acc1, r1))
        for ls, acc, gate in zip(heads, (acc0, acc1), gates):
            o_ref[:, ls] = acc
            y_ref[:, ls] = (acc * gate).astype(y_ref.dtype)

    qblk = pl.BlockSpec((blk, LANES), lambda p, i: (i, p))
    return pl.pallas_call(
        body, name="stickbreak_fwd", grid=(SB_HEADS // 2, nq),
        in_specs=[qblk,
                  pl.BlockSpec((s, LANES), lambda p, i: (0, 4 + p)),
                  pl.BlockSpec((s, LANES), lambda p, i: (0, 8 + p)),
                  qblk],
        out_specs=[qblk, qblk],
        out_shape=[jax.ShapeDtypeStruct((s, 512), MXU_DTYPE),
                   jax.ShapeDtypeStruct((s, 512), F32)],
        scratch_shapes=[pltpu.VMEM((2, blk, 2 * blk), F32),
                        pltpu.VMEM((2, 2 * blk, blk), MXU_DTYPE),
                        pltpu.VMEM((2, 2 * blk, blk), F32),
                        pltpu.VMEM((2, blk, 2 * blk), MXU_DTYPE)],
        compiler_params=_params(("arbitrary", "arbitrary")),
    )(qkv, qkv, qkv, sg)


def _sb_bwd(qkv, sg, o, dy):
    s = qkv.shape[0]
    blk = min(SB_BLOCK, s)
    nq = s // blk
    hd = SB_HEAD_DIM
    assert nq <= LANES
    ch = min(SB_CHUNK, blk)

    def body(q_ref, k_ref, v_ref, g_ref, o_ref, dy_ref, dq_ref, dk_ref, dv_ref, dg_ref, dk_scr, dv_scr,
             z_scr, g_scr, lhs_scr, cs_scr, a_scr, dz_scr):
        i = pl.program_id(1)

        @pl.when(i == 0)
        def _():
            dk_scr[...] = jnp.zeros_like(dk_scr)
            dv_scr[...] = jnp.zeros_like(dv_scr)

        row = lax.broadcasted_iota(jnp.int32, (blk, blk), 0)
        col = lax.broadcasted_iota(jnp.int32, (blk, blk), 1)
        lane = lax.broadcasted_iota(jnp.int32, (blk, LANES), 1)
        u_gt = (row > col).astype(MXU_DTYPE)
        u_lt = (row < col).astype(MXU_DTYPE)
        heads = [slice(hh * hd, (hh + 1) * hd) for hh in range(2)]
        qs = [q_ref[:, ls] for ls in heads]
        _, rows_p, rows_d = _sb_pair_tile(i, blk)
        has_prev = i >= 1
        crow = lax.broadcasted_iota(jnp.int32, (ch, blk), 0)
        ccol = lax.broadcasted_iota(jnp.int32, (ch, blk), 1)
        nch = blk // ch
        kcs = [jnp.concatenate([k_ref[rows_p, ls], k_ref[rows_d, ls]], axis=0) for ls in heads]
        dobs = []

        def gate_grads(hh):
            ls = heads[hh]
            g = g_ref[:, ls]
            dyv = dy_ref[:, ls]
            sgm = _sigmoid(g)
            dg_ref[:, ls] = (dyv * o_ref[:, ls] * (sgm * (1.0 + g * (1.0 - sgm)))).astype(dg_ref.dtype)
            dobs.append(_mx(dyv * (g * sgm)))

        def split_rows(hh, c, part, x):
            lhs_scr[hh, pl.ds(part * blk + c * ch, ch), :] = _mx(x)

        def summed_rows(hh, c, part):
            return cs_scr[hh, pl.ds(part * blk + c * ch, ch), :]

        def logits(hh):
            z_scr[hh] = _dot_nt(qs[hh], kcs[hh])

        def weight_grads(hh):
            vc = jnp.concatenate([v_ref[rows_p, heads[hh]], v_ref[rows_d, heads[hh]]], axis=0)
            g_scr[hh] = _dot_nt(dobs[hh], vc)

        def keep_parts(hh):
            r_d, r_all = [], []
            for c in range(nch):
                rows = pl.ds(c * ch, ch)
                z = z_scr[hh, rows, :]
                l1p = jnp.log(1.0 + jnp.exp(-jnp.abs(z)))
                lb = jnp.minimum(z, 0.0) - l1p
                z_scr[hh, rows, :] = lb
                lk = lb - z
                lk_p = lk[:, :blk]
                lk_d = jnp.where(crow + c * ch > ccol, lk[:, blk:], 0.0)
                split_rows(hh, c, 0, lk_p)
                split_rows(hh, c, 1, lk_d)
                r_d.append(_rowsum(lk_d))
                r_all.append(r_d[c] + _rowsum(lk_p))
            return r_d, jnp.concatenate(r_all, axis=0)

        def weights(hh, r_d):
            g_p = []
            for c in range(nch):
                rows = pl.ds(c * ch, ch)
                lb = z_scr[hh, rows, :]
                a_p = jnp.exp(lb[:, :blk] + (summed_rows(hh, c, 0) + jnp.where(has_prev, r_d[c], NO_KEYS)))
                a_d = jnp.where(crow + c * ch > ccol, jnp.exp(lb[:, blk:] + summed_rows(hh, c, 1)), 0.0)
                a = jnp.concatenate([a_p, a_d], axis=1)
                a_scr[hh, rows, :] = _mx(a)
                gm = g_scr[hh, rows, :] * a
                g_scr[hh, rows, :] = gm
                split_rows(hh, c, 0, gm[:, :blk])
                split_rows(hh, c, 1, gm[:, blk:])
                g_p.append(_rowsum(gm[:, :blk]))
            return g_p

        def logit_grads(hh, pg, g_p):
            for c in range(nch):
                rows = pl.ds(c * ch, ch)
                pre = jnp.concatenate([summed_rows(hh, c, 0) + pg[c * ch:(c + 1) * ch],
                                       summed_rows(hh, c, 1) + (pg[c * ch:(c + 1) * ch] + g_p[c])], axis=1)
                gm = g_scr[hh, rows, :]
                dz = gm - (gm + pre) * jnp.exp(z_scr[hh, rows, :])
                dz_p = dz[:, :blk]
                dz_d = jnp.where(crow + c * ch > ccol, dz[:, blk:], 0.0)
                dz_scr[hh, rows, :] = _mx(jnp.concatenate([dz_p, dz_d], axis=1))

        def products(hh, acc):
            ls = heads[hh]
            dk_scr[hh, rows_p, :] += _dot_tn(dz_scr[hh, :, 0:blk], qs[hh])
            dk_scr[hh, rows_d, :] += _dot_tn(dz_scr[hh, :, blk:2 * blk], qs[hh])
            dv_scr[hh, rows_p, :] += _dot_tn(a_scr[hh, :, 0:blk], dobs[hh])
            dv_scr[hh, rows_d, :] += _dot_tn(a_scr[hh, :, blk:2 * blk], dobs[hh])
            dq_ref[:, ls] = ((acc + _dot(dz_scr[hh], kcs[hh])) * SB_SCALE).astype(dq_ref.dtype)

        def suffix_sums(hh):
            cs_scr[hh] = _dot(lhs_scr[hh], u_gt)

        def prefix_sums(hh):
            cs_scr[hh] = _dot(lhs_scr[hh], u_lt)

        def more(n, r0, r1):
            return jnp.logical_and(n < i, jnp.max(jnp.maximum(r0, r1)) > -SB_SKIP)

        gate_grads(0)
        gate_grads(1)
        logits(0)
        weight_grads(0)
        logits(1)
        weight_grads(1)
        rd0, ra0 = keep_parts(0)
        suffix_sums(0)
        rd1, ra1 = keep_parts(1)
        suffix_sums(1)
        go = more(jnp.int32(1), ra0, ra1)
        gp0 = weights(0, rd0)
        prefix_sums(0)
        gp1 = weights(1, rd1)
        prefix_sums(1)

        def scan_block(hh, j, r, rmat):
            start = pl.multiple_of(j * blk, blk)
            _, lk = _sb_logits(qs[hh], k_ref[pl.ds(start, blk), heads[hh]], None)
            return r + _rowsum(lk), jnp.where(lane == j, r, rmat)

        def scan_step(c):
            _, n, r0, rmat0, r1, rmat1 = c
            r0, rmat0 = scan_block(0, i - 1 - n, r0, rmat0)
            r1, rmat1 = scan_block(1, i - 1 - n, r1, rmat1)
            return more(n + 1, r0, r1), n + 1, r0, rmat0, r1, rmat1

        zmat = jnp.zeros((blk, LANES), F32)
        _, n, _, rmat0, _, rmat1 = lax.while_loop(lambda c: c[0], scan_step,
                                                  (go, jnp.int32(1), ra0, zmat, ra1, zmat))
        rmats = (rmat0, rmat1)

        def block(hh, j, pg):
            ls = heads[hh]
            start = pl.multiple_of(j * blk, blk)
            k = k_ref[pl.ds(start, blk), ls]
            lb, lk = _sb_logits(qs[hh], k, None)
            r = _rowsum(jnp.where(lane == j, rmats[hh], 0.0))
            a = _sb_weights(lb, lk, r, u_gt, None)
            gm = _dot_nt(dobs[hh], v_ref[pl.ds(start, blk), ls]) * a
            dzb = _mx(gm - (gm + (pg + _split_dot(gm, u_lt))) * jnp.exp(lb))
            dk_scr[hh, pl.ds(start, blk), :] += _dot_tn(dzb, qs[hh])
            dv_scr[hh, pl.ds(start, blk), :] += _dot_tn(_mx(a), dobs[hh])
            return _dot(dzb, k), pg + _rowsum(gm)

        def step(t, c):
            acc0, pg0, acc1, pg1 = c
            dq0, pg0 = block(0, i - n + t, pg0)
            dq1, pg1 = block(1, i - n + t, pg1)
            return acc0 + dq0, pg0, acc1 + dq1, pg1

        zero = jnp.zeros((blk, 1), F32)
        zacc = jnp.zeros((blk, hd), F32)
        acc0, pg0, acc1, pg1 = lax.fori_loop(0, n - 1, step, (zacc, zero, zacc, zero))
        logit_grads(0, pg0, gp0)
        logit_grads(1, pg1, gp1)
        products(0, acc0)
        products(1, acc1)

        @pl.when(i == nq - 1)
        def _():
            for hh in range(2):
                ls = slice(hh * hd, (hh + 1) * hd)
                dk_ref[:, ls] = dk_scr[hh].astype(dk_ref.dtype)
                dv_ref[:, ls] = dv_scr[hh].astype(dv_ref.dtype)

    qblk = lambda c0: pl.BlockSpec((blk, LANES), lambda p, i: (i, c0 + p))
    full = lambda c0: pl.BlockSpec((s, LANES), lambda p, i: (0, c0 + p))
    half = jax.ShapeDtypeStruct((s, 512), MXU_DTYPE)
    return pl.pallas_call(
        body, name="stickbreak_bwd", grid=(SB_HEADS // 2, nq),
        in_specs=[qblk(0), full(4), full(8), qblk(0), qblk(0), qblk(4)],
        out_specs=[qblk(0), full(0), full(0), qblk(0)],
        out_shape=[half, half, half, half],
        scratch_shapes=[pltpu.VMEM((2, s, hd), F32), pltpu.VMEM((2, s, hd), F32),
                        pltpu.VMEM((2, blk, 2 * blk), F32),
                        pltpu.VMEM((2, blk, 2 * blk), F32),
                        pltpu.VMEM((2, 2 * blk, blk), MXU_DTYPE),
                        pltpu.VMEM((2, 2 * blk, blk), F32),
                        pltpu.VMEM((2, blk, 2 * blk), MXU_DTYPE),
                        pltpu.VMEM((2, blk, 2 * blk), MXU_DTYPE)],
        compiler_params=_params(("arbitrary", "arbitrary")),
    )(qkv, qkv, qkv, sg, o, dy)


def _layer_fwd(layer, x, mod, norm_g, w_in_g, w_out_g, tables, gather=(), loss_head=None):
    shift, scale1p, gate = mod[0:1], 1.0 + mod[1:2], mod[2:3]
    ret, qkv, sg, *gathered = _ln_proj(x, shift, scale1p, norm_g, w_in_g, layer, gather)
    y_r, states = _retention_fwd(ret, tables)
    y_s, o_s = _sb_fwd(qkv, sg)
    x_next = _out_proj(x, gate, y_r, y_s, w_out_g, layer, loss_head)
    saved = (x, shift, scale1p, gate, ret, qkv, sg, y_r, states, y_s, o_s)
    return x_next, saved, gathered


def _layer_bwd(layer, me, dx_out, saved, norm_g, w_in_g, w_out_g, tables):
    x, shift, scale1p, gate, ret, qkv, sg, y_r, states, y_s, o_s = saved
    dy, dw_out, dgate = _out_proj_bwd(dx_out, gate, y_r, y_s, w_out_g, layer)
    dwo_parts = _mx(dw_out.reshape(N_DEV, D_MODEL // N_DEV, D_MODEL))
    d_ret, r_out = _retention_bwd(ret, states, dy, tables, dwo_parts)
    d_q, d_k, d_v, d_g = _sb_bwd(qkv, sg, o_s, dy)
    dproj = (d_ret, d_q, d_k, d_v, d_g)
    dx, dshift, dscale, dnorm_g, h_t = _in_proj_bwd_x(x, dx_out, dproj, shift, scale1p, norm_g, w_in_g, layer)
    r_in = _in_proj_bwd_w(me, h_t, dproj)
    dmod = jnp.concatenate([dshift, dscale, dgate], axis=1)
    return dx, r_in, r_out, dmod, dnorm_g


MESH_IDS = pl.DeviceIdType.MESH
N_PEERS = N_DEV - 1
HBM_SPEC = pl.BlockSpec(memory_space=pl.ANY)


def _my_place():
    return lax.axis_index("x"), lax.axis_index("y"), lax.axis_index("c")


def _linear(px, py, pc):
    return 4 * px + 2 * py + pc


def _all_gather(blocks):
    n_arr = len(blocks)

    def body(*refs):
        start, forward, finish = _gather_plan(refs[:n_arr], refs[n_arr:2 * n_arr], *refs[2 * n_arr:])
        start()
        forward()
        finish()

    return pl.pallas_call(
        body, name="all_gather",
        out_shape=_gathered_shapes(blocks),
        in_specs=[HBM_SPEC] * n_arr, out_specs=[HBM_SPEC] * n_arr,
        scratch_shapes=_gather_sems(n_arr),
    )(*blocks)


def _gathered_shapes(blocks):
    return [jax.ShapeDtypeStruct((N_DEV * b.shape[0], b.shape[1]), b.dtype) for b in blocks]


def _gather_sems(n_arr):
    return [pltpu.SemaphoreType.DMA((n_arr * N_PEERS,)), pltpu.SemaphoreType.DMA((n_arr * N_PEERS,)),
            pltpu.SemaphoreType.DMA((n_arr,))]


def _gather_plan(x_refs, out_refs, send_sems, recv_sems, local_sems):
    n_arr = len(x_refs)
    x, y, c = _my_place()
    me, sibling = (x, y, c), (x, y, 1 - c)
    chips = [(1 - x, y), (x, 1 - y), (1 - x, 1 - y)]

    def rows(a, place):
        m = x_refs[a].shape[0]
        return out_refs[a].at[pl.ds(_linear(*place) * m, m), :]

    def copy(a, k, block, to, src=None):
        return pltpu.make_async_remote_copy(
            src_ref=rows(a, block) if src is None else src, dst_ref=rows(a, block),
            send_sem=send_sems.at[a * N_PEERS + k], recv_sem=recv_sems.at[a * N_PEERS + k],
            device_id=to, device_id_type=MESH_IDS)

    mine = [pltpu.make_async_copy(x_refs[a], rows(a, me), local_sems.at[a]) for a in range(n_arr)]
    first = []
    for a in range(n_arr):
        first.append(copy(a, 0, me, sibling, src=x_refs[a]))
        first += [copy(a, 1 + j, me, (*chip, c), src=x_refs[a]) for j, chip in enumerate(chips)]
    passed = [copy(a, 4 + j, (*chip, c), sibling) for j, chip in enumerate(chips) for a in range(n_arr)]

    def start():
        for cp in mine + first:
            cp.start()

    def forward():
        for j, chip in enumerate(chips):
            for a in range(n_arr):
                copy(a, 1 + j, (*chip, c), me).wait_recv()
                passed[j * n_arr + a].start()

    def finish():
        for a in range(n_arr):
            copy(a, 0, sibling, me).wait_recv()
            for j, chip in enumerate(chips):
                copy(a, 4 + j, (*chip, 1 - c), me).wait_recv()
        for cp in first + passed:
            cp.wait_send()
        for cp in mine:
            cp.wait()

    return start, forward, finish


def _ada_fwd(c_all, w_ada, b_cols):
    cols = w_ada.shape[2]

    def body(c_ref, w_ref, b_ref, ca_ref, mod_ref):
        cv = c_ref[...]
        ca = cv * _sigmoid(cv)
        ca_ref[...] = ca
        cb = _mx(jnp.concatenate([ca, ca], axis=0))
        for l in range(DEPTH):
            mod_ref[l * N_DEV:(l + 1) * N_DEV, :] = _dot(cb, _mx(w_ref[l]))[0:N_DEV] + b_ref[l]

    return pl.pallas_call(
        body, name="ada_fwd",
        out_shape=[jax.ShapeDtypeStruct((N_DEV, D_MODEL), F32),
                   jax.ShapeDtypeStruct((DEPTH * N_DEV, cols), F32)],
        compiler_params=_params(),
    )(c_all, w_ada, b_cols)


def _ada_bwd(c_act_t, dmod_cols):
    cols = dmod_cols.shape[2]

    def body(ca_ref, dm_ref, o_ref):
        ca = _mx(ca_ref[...]).astype(F32)
        for l in range(DEPTH):
            o_ref[l] = jnp.dot(ca, _mx(dm_ref[l]).astype(F32),
                               precision=lax.Precision.HIGHEST, preferred_element_type=F32)

    return pl.pallas_call(
        body, name="ada_bwd",
        out_shape=jax.ShapeDtypeStruct((DEPTH, D_MODEL, cols), F32),
        compiler_params=_params(),
    )(c_act_t, dmod_cols)


def _adamw_store(g, w_ref, m_ref, v_ref, g_ref, d_ref, mo_ref, vo_ref):
    m2 = ADAM_B1 * m_ref[...] + (1.0 - ADAM_B1) * g
    v2 = ADAM_B2 * v_ref[...] + (1.0 - ADAM_B2) * (g * g)
    m_hat = m2 / (1.0 - ADAM_B1 ** ADAM_STEP)
    v_hat = v2 / (1.0 - ADAM_B2 ** ADAM_STEP)
    g_ref[...] = g
    d_ref[...] = -ADAM_LR * (m_hat / (jnp.sqrt(v_hat) + ADAM_EPS) + ADAM_WD * w_ref[...])
    mo_ref[...] = m2
    vo_ref[...] = v2


def _slab_sum(p_ref):
    g = p_ref[0].astype(F32)
    for sl in range(1, p_ref.shape[0]):
        g = g + p_ref[sl].astype(F32)
    return g


def _sum_adamw_layers(parts, w, m, v):
    n_slab, rows, cols = parts[0].shape
    tr = min(256, rows)
    nt = rows // tr

    def body(p0_ref, p1_ref, w_ref, m_ref, v_ref, g_ref, d_ref, mo_ref, vo_ref):
        for l, p_ref in enumerate((p0_ref, p1_ref)):
            @pl.when(pl.program_id(0) == l)
            def _():
                _adamw_store(_slab_sum(p_ref), w_ref, m_ref, v_ref, g_ref, d_ref, mo_ref, vo_ref)

    p_specs = [pl.BlockSpec((n_slab, tr, cols), lambda l, i: (0, i * (1 - l) + (nt - 1) * l, 0)),
               pl.BlockSpec((n_slab, tr, cols), lambda l, i: (0, i * l, 0))]
    blk = pl.BlockSpec((None, tr, cols), lambda l, i: (l, i, 0))
    shp = jax.ShapeDtypeStruct((DEPTH, rows, cols), F32)
    return pl.pallas_call(
        body, name="sum_adamw_layers", grid=(DEPTH, nt),
        in_specs=p_specs + [blk, blk, blk],
        out_specs=[blk, blk, blk, blk],
        out_shape=[shp, shp, shp, shp],
        compiler_params=_params(("arbitrary", "arbitrary")),
    )(parts[0], parts[1], w, m, v)


def _sum_adamw(parts, w, m, v):
    n_slab, rows, cols = parts.shape
    tr = min(256, rows)

    def body(p_ref, w_ref, m_ref, v_ref, g_ref, d_ref, mo_ref, vo_ref):
        _adamw_store(_slab_sum(p_ref), w_ref, m_ref, v_ref, g_ref, d_ref, mo_ref, vo_ref)

    blk = pl.BlockSpec((tr, cols), lambda i: (i, 0))
    shp = jax.ShapeDtypeStruct((rows, cols), F32)
    return pl.pallas_call(
        body, name="sum_adamw", grid=(rows // tr,),
        in_specs=[pl.BlockSpec((n_slab, tr, cols), lambda i: (0, i, 0)), blk, blk, blk],
        out_specs=[blk, blk, blk, blk],
        out_shape=[shp, shp, shp, shp],
        compiler_params=_params(("arbitrary",)),
    )(parts, w, m, v)


SMALL_ROWS = 16


def kernel(x, c, norm_g, w_ada, b_ada, w_in, w_out, final_g, loss_target, m_norm_g, m_w_ada, m_b_ada, m_w_in, m_w_out, m_final_g, v_norm_g, v_w_ada, v_b_ada, v_w_in, v_w_out, v_final_g):
    me = _linear(*_my_place())
    in_cols = w_in.shape[2]
    out_rows = w_out.shape[1]
    ada_cols = w_ada.shape[2]

    w_in_m, w_out_m = _mx(w_in), _mx(w_out)
    g_in, g_out, g_c = _all_gather([w_in_m[0], w_out_m[0], jnp.broadcast_to(c, (8, D_MODEL))])
    w_in_g = [g_in.reshape(N_DEV, 1, D_MODEL, in_cols), None]
    w_out_g = [g_out.reshape(N_DEV, 1, out_rows, D_MODEL), None]
    c_all = g_c.reshape(N_DEV, 8, D_MODEL)[:, 0]

    b_cols = lax.dynamic_slice_in_dim(b_ada, me * ada_cols, ada_cols, axis=1)[:, None, :]
    c_act, mod_cols = _ada_fwd(c_all, w_ada, b_cols)
    (g_mod,) = _all_gather([mod_cols])
    g_mod = g_mod.reshape(N_DEV, DEPTH, N_DEV, ada_cols)
    mod = lax.dynamic_index_in_dim(g_mod, me, axis=2, keepdims=False)
    mod = mod.transpose(1, 0, 2).reshape(DEPTH, 3, D_MODEL)

    tables = _ret_tables(x.shape[1])
    h = x[0]
    saved = []
    for l in range(DEPTH):
        nxt = (w_in_m[l + 1], w_out_m[l + 1]) if l + 1 < DEPTH else ()
        head = None if nxt else (final_g[None], loss_target[0])
        h, sv, gathered = _layer_fwd(0, h, mod[l], norm_g[l:l + 1], w_in_g[l], w_out_g[l], tables, nxt, head)
        if nxt:
            w_in_g[l + 1] = gathered[0].reshape(N_DEV, 1, D_MODEL, in_cols)
            w_out_g[l + 1] = gathered[1].reshape(N_DEV, 1, out_rows, D_MODEL)
        saved.append(sv)
    dx, loss_part, dfg = h
    r_in, r_out, small = [None] * DEPTH, [None] * DEPTH, [None] * DEPTH
    for l in reversed(range(DEPTH)):
        dx, r_in[l], r_out[l], dmod, dng = _layer_bwd(0, me, dx, saved[l], norm_g[l:l + 1], w_in_g[l], w_out_g[l], tables)
        small[l] = (dmod.reshape(3, D_MODEL), dng)

    pad = jnp.zeros((SMALL_ROWS - 10, D_MODEL), F32)
    small_block = jnp.concatenate([small[0][0], small[1][0], small[0][1], small[1][1], dfg,
                                   jnp.broadcast_to(loss_part, (1, D_MODEL)), pad], axis=0)
    (g_small,) = _all_gather([small_block])
    g_small = g_small.reshape(N_DEV, SMALL_ROWS, D_MODEL)

    def small_pack(b, n, f, fill):
        return jnp.concatenate([b.reshape(6, D_MODEL), n, f[None],
                                jnp.full((SMALL_ROWS - 9, D_MODEL), fill, F32)], axis=0)

    s_g, s_d, s_m, s_v = _sum_adamw(g_small, small_pack(b_ada, norm_g, final_g, 0.0),
                                    small_pack(m_b_ada, m_norm_g, m_final_g, 0.0),
                                    small_pack(v_b_ada, v_norm_g, v_final_g, 1.0))
    loss = s_g[9, 0]

    def small_unpack(a):
        return a[0:6].reshape(DEPTH, 3 * D_MODEL), a[6:8], a[8]

    dmod_all = g_small[:, 0:6].reshape(N_DEV, DEPTH, 3 * D_MODEL).transpose(1, 0, 2)
    dmod_cols = lax.dynamic_slice_in_dim(dmod_all, me * ada_cols, ada_cols, axis=2)
    g_ada = _ada_bwd(c_act.T, dmod_cols).reshape(1, DEPTH * D_MODEL, ada_cols)
    ada = _sum_adamw(g_ada, *[a.reshape(DEPTH * D_MODEL, ada_cols) for a in (w_ada, m_w_ada, v_w_ada)])
    ada = [a.reshape(DEPTH, D_MODEL, ada_cols) for a in ada]

    win = _sum_adamw_layers(r_in, w_in, m_w_in, v_w_in)
    wout = _sum_adamw_layers(r_out, w_out, m_w_out, v_w_out)

    outs = [loss, dx[None]]
    for k in range(4):
        b, n, f = small_unpack((s_g, s_d, s_m, s_v)[k])
        outs += [n, ada[k], b, win[k], wout[k], f]
    return tuple(outs)
```

```python
import functools

import jax
import jax.numpy as jnp
from jax import lax
from jax.experimental import pallas as pl
from jax.experimental.pallas import tpu as pltpu

F32 = jnp.float32
MXU_DTYPE = jnp.bfloat16

D_MODEL = 1024
DEPTH = 2
N_DEV = 8
CHUNK = 64
D_RET = 512
D_SB = 512
RET_HEADS = 4
RET_HEAD_DIM = 128
SB_HEADS = 8
SB_HEAD_DIM = 64
D_IN = 4096
ROPE_BASE = 10000.0
EPS = 1e-6
SB_SCALE = SB_HEAD_DIM ** -0.5
RET_KSCALE = RET_HEAD_DIM ** -0.5

ADAM_LR = 0.001
ADAM_B1 = 0.9
ADAM_B2 = 0.999
ADAM_EPS = 1e-08
ADAM_WD = 0.01
ADAM_STEP = 10

V7X_VMEM_BYTES = 64 * 2 ** 20
VMEM_LIMIT = V7X_VMEM_BYTES - 8 * 2 ** 20
LANES = 128

_NT = (((1,), (1,)), ((), ()))
_TN = (((0,), (0,)), ((), ()))


def _dot(a, b):
    return jnp.dot(a, b, preferred_element_type=F32)


def _dot_nt(a, b):
    return lax.dot_general(a, b, _NT, preferred_element_type=F32)


def _dot_tn(a, b):
    return lax.dot_general(a, b, _TN, preferred_element_type=F32)


def _mx(x):
    return x.astype(MXU_DTYPE)


def _sigmoid(x):
    return 1.0 / (1.0 + jnp.exp(-x))


def _params(sem=None):
    return pltpu.CompilerParams(dimension_semantics=sem, vmem_limit_bytes=VMEM_LIMIT)


def _row_tile(s):
    return min(512, s)


def _w_in_spec(w_in_g, layer):
    return pl.BlockSpec((N_DEV, None) + w_in_g.shape[2:], lambda i: (0, layer, 0, 0))


def _w_out_spec(w_out_g, layer):
    return pl.BlockSpec((N_DEV, None) + w_out_g.shape[2:], lambda i: (0, layer, 0, 0))


def _ln_proj(x, shift, scale1p, g, w_in_g, layer, gather=()):
    s = x.shape[0]
    ts = _row_tile(s)
    ns = s // ts
    n_g = len(gather)

    def body(x_ref, sh_ref, sc_ref, g_ref, w_ref, *rest):
        ret_ref, qkv_ref, sg_ref = rest[n_g:n_g + 3]
        if n_g:
            start, forward, finish = _gather_plan(rest[:n_g], rest[n_g + 3:2 * n_g + 3], *rest[2 * n_g + 3:])
            i = pl.program_id(0)
            pl.when(i == 0)(start)
            pl.when(i == max(ns - 4, 0))(forward)
        xv = x_ref[...]
        rstd = lax.rsqrt(jnp.mean(xv * xv, axis=-1, keepdims=True) + EPS)
        h = (xv * rstd * g_ref[...]) * sc_ref[...] + sh_ref[...]
        hb = _mx(h)
        for n in range(4):
            ret_ref[:, n * 512:(n + 1) * 512] = _dot(hb, w_ref[n])
        qkv_ref[:, 0:512] = _mx(_dot(hb, w_ref[4]) * SB_SCALE)
        qkv_ref[:, 512:1024] = _mx(_dot(hb, w_ref[5]))
        qkv_ref[:, 1024:1536] = _mx(_dot(hb, w_ref[6]))
        sg_ref[...] = _dot(hb, w_ref[7])
        if n_g:
            pl.when(i == ns - 1)(finish)

    vec = pl.BlockSpec((1, D_MODEL), lambda i: (0, 0))
    return pl.pallas_call(
        body, name="ln_proj_gather" if n_g else "ln_proj", grid=(ns,),
        in_specs=[pl.BlockSpec((ts, D_MODEL), lambda i: (i, 0)), vec, vec, vec,
                  _w_in_spec(w_in_g, layer)] + [HBM_SPEC] * n_g,
        out_specs=[pl.BlockSpec((ts, 2048), lambda i: (i, 0)),
                   pl.BlockSpec((ts, 1536), lambda i: (i, 0)),
                   pl.BlockSpec((ts, 512), lambda i: (i, 0))] + [HBM_SPEC] * n_g,
        out_shape=[jax.ShapeDtypeStruct((s, 2048), F32),
                   jax.ShapeDtypeStruct((s, 1536), MXU_DTYPE),
                   jax.ShapeDtypeStruct((s, 512), F32)] + _gathered_shapes(gather),
        scratch_shapes=_gather_sems(n_g) if n_g else (),
        compiler_params=_params(("arbitrary",)),
    )(x, shift, scale1p, g, w_in_g, *gather)


def _w_out_halves(w_ref):
    half = N_DEV // 2
    return (w_ref[0:half].reshape(D_RET, D_MODEL), w_ref[half:N_DEV].reshape(D_SB, D_MODEL))


def _out_proj(x, gate, y_r, y_s, w_out_g, layer, loss_head=None):
    s = x.shape[0]
    ts = _row_tile(s)

    def layer_out(x_ref, gate_ref, yr_ref, ys_ref, w_ref):
        w_r, w_s = _w_out_halves(w_ref)
        return x_ref[...] + gate_ref[...] * (_dot(yr_ref[...], w_r) + _dot(ys_ref[...], w_s))

    def body(x_ref, gate_ref, yr_ref, ys_ref, w_ref, o_ref):
        o_ref[...] = layer_out(x_ref, gate_ref, yr_ref, ys_ref, w_ref)

    def body_loss(x_ref, gate_ref, yr_ref, ys_ref, w_ref, fg_ref, t_ref, dx_ref, loss_ref, dfg_ref):
        @pl.when(pl.program_id(0) == 0)
        def _():
            loss_ref[...] = jnp.zeros_like(loss_ref)
            dfg_ref[...] = jnp.zeros_like(dfg_ref)

        xv = layer_out(x_ref, gate_ref, yr_ref, ys_ref, w_ref)
        fgv = fg_ref[...]
        rstd = lax.rsqrt(jnp.mean(xv * xv, axis=-1, keepdims=True) + EPS)
        xn = xv * rstd
        err = xn * fgv - t_ref[...]
        tok = jnp.mean(err * err, axis=-1, keepdims=True)
        loss_ref[...] += 0.5 * jnp.sum(tok, axis=0, keepdims=True)
        dy = err * (1.0 / D_MODEL)
        dfg_ref[...] += jnp.sum(dy * xn, axis=0, keepdims=True)
        dxn = dy * fgv
        dx_ref[...] = rstd * (dxn - xn * jnp.mean(dxn * xn, axis=-1, keepdims=True))

    rows = pl.BlockSpec((ts, D_MODEL), lambda i: (i, 0))
    vec = pl.BlockSpec((1, D_MODEL), lambda i: (0, 0))
    in_specs = [rows, vec, pl.BlockSpec((ts, 512), lambda i: (i, 0)), pl.BlockSpec((ts, 512), lambda i: (i, 0)),
                _w_out_spec(w_out_g, layer)]
    if loss_head is None:
        return pl.pallas_call(
            body, name="out_proj", grid=(s // ts,), in_specs=in_specs, out_specs=rows,
            out_shape=jax.ShapeDtypeStruct((s, D_MODEL), F32),
            compiler_params=_params(("arbitrary",)),
        )(x, gate, y_r, y_s, w_out_g)
    return pl.pallas_call(
        body_loss, name="out_proj_loss", grid=(s // ts,), in_specs=in_specs + [vec, rows],
        out_specs=[rows, pl.BlockSpec((1, 1), lambda i: (0, 0)), vec],
        out_shape=[jax.ShapeDtypeStruct((s, D_MODEL), F32),
                   jax.ShapeDtypeStruct((1, 1), F32),
                   jax.ShapeDtypeStruct((1, D_MODEL), F32)],
        compiler_params=_params(("arbitrary",)),
    )(x, gate, y_r, y_s, w_out_g, *loss_head)


def _out_proj_bwd(dx_out, gate, y_r, y_s, w_out_g, layer):
    s = dx_out.shape[0]
    ts = _row_tile(s)
    ns = s // ts

    def body(dx_ref, gate_ref, yr_ref, ys_ref, w_ref, dy_ref, dw_ref, dgate_ref):
        i = pl.program_id(0)

        @pl.when(i == 0)
        def _():
            dw_ref[...] = jnp.zeros_like(dw_ref)

        dxv = dx_ref[...]
        dt = _mx(dxv * gate_ref[...])
        dxb = _mx(dxv)
        w_r, w_s = _w_out_halves(w_ref)
        dy_ref[:, 0:512] = _dot_nt(dt, w_r)
        dy_ref[:, 512:1024] = _dot_nt(dt, w_s)
        dw_ref[0:512, :] += _dot_tn(yr_ref[...], dxb)
        dw_ref[512:1024, :] += _dot_tn(ys_ref[...], dxb)

        @pl.when(i == ns - 1)
        def _():
            m_r, m_s = dw_ref[0:512, :], dw_ref[512:1024, :]
            dgate_ref[...] = (jnp.sum(w_r.astype(F32) * m_r, axis=0, keepdims=True)
                              + jnp.sum(w_s.astype(F32) * m_s, axis=0, keepdims=True))
            dw_ref[...] = dw_ref[...] * gate_ref[...]

    return pl.pallas_call(
        body, name="out_proj_bwd", grid=(ns,),
        in_specs=[pl.BlockSpec((ts, D_MODEL), lambda i: (i, 0)),
                  pl.BlockSpec((1, D_MODEL), lambda i: (0, 0)),
                  pl.BlockSpec((ts, 512), lambda i: (i, 0)),
                  pl.BlockSpec((ts, 512), lambda i: (i, 0)),
                  _w_out_spec(w_out_g, layer)],
        out_specs=[pl.BlockSpec((ts, D_MODEL), lambda i: (i, 0)),
                   pl.BlockSpec((D_MODEL, D_MODEL), lambda i: (0, 0)),
                   pl.BlockSpec((1, D_MODEL), lambda i: (0, 0))],
        out_shape=[jax.ShapeDtypeStruct((s, D_MODEL), F32),
                   jax.ShapeDtypeStruct((D_MODEL, D_MODEL), F32),
                   jax.ShapeDtypeStruct((1, D_MODEL), F32)],
        compiler_params=_params(("arbitrary",)),
    )(dx_out, gate, y_r, y_s, w_out_g)


def _scatter_plan(parts_ref, recv_ref, send_sems, recv_sems, local_sem):
    px, py, pc = _my_place()
    mine = _linear(px, py, pc)

    def copy(r):
        peer = (1 - px if r & 4 else px, 1 - py if r & 2 else py, 1 - pc if r & 1 else pc)
        return pltpu.make_async_remote_copy(
            src_ref=parts_ref.at[_linear(*peer)], dst_ref=recv_ref.at[mine],
            send_sem=send_sems.at[r - 1], recv_sem=recv_sems.at[r - 1],
            device_id=peer, device_id_type=MESH_IDS)

    own = pltpu.make_async_copy(parts_ref.at[mine], recv_ref.at[mine], local_sem.at[0])

    def start():
        own.start()
        for r in range(1, N_DEV):
            copy(r).start()

    def finish():
        for r in range(1, N_DEV):
            copy(r).wait_recv()
            copy(r).wait_send()
        own.wait()

    return start, finish


def _in_proj_bwd_x(x, dx_out, dproj, shift, scale1p, g, w_in_g, layer):
    s = x.shape[0]
    ts = _row_tile(s)
    ns = s // ts
    nb = D_IN // N_DEV

    def body(x_ref, dxo_ref, dr_ref, d4_ref, d5_ref, d6_ref, d7_ref, sh_ref, sc_ref, g_ref, w_ref,
             dx_ref, dsh_ref, dsc_ref, dg_ref, ht_ref):
        i = pl.program_id(0)

        @pl.when(i == 0)
        def _():
            dsh_ref[...] = jnp.zeros_like(dsh_ref)
            dsc_ref[...] = jnp.zeros_like(dsc_ref)
            dg_ref[...] = jnp.zeros_like(dg_ref)

        dh = _dot_nt(dr_ref[:, 0:nb], w_ref[0])
        for n in range(1, 4):
            dh += _dot_nt(dr_ref[:, n * nb:(n + 1) * nb], w_ref[n])
        for n, d_ref in zip(range(4, N_DEV), (d4_ref, d5_ref, d6_ref, d7_ref)):
            dh += _dot_nt(d_ref[...], w_ref[n])
        xv = x_ref[...]
        gv = g_ref[...]
        scv = sc_ref[...]
        rstd = lax.rsqrt(jnp.mean(xv * xv, axis=-1, keepdims=True) + EPS)
        xn = xv * rstd
        xg = xn * gv
        ht_ref[...] = _mx((xg * scv + sh_ref[...]).T)
        dsh_ref[...] += jnp.sum(dh, axis=0, keepdims=True)
        dsc_ref[...] += jnp.sum(dh * xg, axis=0, keepdims=True)
        dhs = dh * scv
        dg_ref[...] += jnp.sum(dhs * xn, axis=0, keepdims=True)
        dxn = dhs * gv
        dx_ref[...] = rstd * (dxn - xn * jnp.mean(dxn * xn, axis=-1, keepdims=True)) + dxo_ref[...]

    vec = pl.BlockSpec((1, D_MODEL), lambda i: (0, 0))
    return pl.pallas_call(
        body, name="in_proj_bwd_x", grid=(ns,),
        in_specs=[pl.BlockSpec((ts, D_MODEL), lambda i: (i, 0)),
                  pl.BlockSpec((ts, D_MODEL), lambda i: (i, 0)),
                  pl.BlockSpec((ts, 4 * nb), lambda i: (i, 0))]
                 + [pl.BlockSpec((ts, nb), lambda i: (i, 0))] * 4
                 + [vec, vec, vec, _w_in_spec(w_in_g, layer)],
        out_specs=[pl.BlockSpec((ts, D_MODEL), lambda i: (i, 0)), vec, vec, vec,
                   pl.BlockSpec((D_MODEL, ts), lambda i: (0, i))],
        out_shape=[jax.ShapeDtypeStruct((s, D_MODEL), F32),
                   jax.ShapeDtypeStruct((1, D_MODEL), F32),
                   jax.ShapeDtypeStruct((1, D_MODEL), F32),
                   jax.ShapeDtypeStruct((1, D_MODEL), F32),
                   jax.ShapeDtypeStruct((D_MODEL, s), MXU_DTYPE)],
        compiler_params=_params(("arbitrary",)),
    )(x, dx_out, *dproj, shift, scale1p, g, w_in_g)


def _in_proj_bwd_w(me, h_t, dproj):
    s = h_t.shape[1]
    ts = min(4 * _row_tile(s), s)
    ns = s // ts
    nb = D_IN // N_DEV
    n_chip = N_DEV // 2

    def flip_bits(j):
        return jnp.where(j == 0, 4, jnp.where(j == 1, 2, jnp.where(j == 2, 6, 0)))

    def slab_of(t, me_ref):
        return jnp.bitwise_xor(me_ref[0], flip_bits(t // 2) + 1 - t % 2)

    def body(me_ref, ht_ref, dr_ref, d4_ref, d5_ref, d6_ref, d7_ref, rin_ref,
             acc, stage, pre_buf, pre_send, pre_recv, sum_send, sum_recv, local_sem):
        t = pl.program_id(0)
        i = pl.program_id(1)
        j = t // 2
        summing = t % 2 == 1
        slab = slab_of(t, me_ref)
        px, py, pc = _my_place()

        def pre_copy(jj):
            return pltpu.make_async_remote_copy(
                src_ref=stage.at[jj % 2], dst_ref=pre_buf.at[jj],
                send_sem=pre_send.at[jj], recv_sem=pre_recv.at[jj],
                device_id=(px, py, 1 - pc), device_id_type=MESH_IDS)

        def sum_copy(jj):
            fx = jnp.logical_or(jj == 0, jj == 2)
            fy = jnp.logical_or(jj == 1, jj == 2)
            return pltpu.make_async_remote_copy(
                src_ref=stage.at[2 + jj % 2], dst_ref=rin_ref.at[jj],
                send_sem=sum_send.at[jj], recv_sem=sum_recv.at[jj],
                device_id=(jnp.where(fx, 1 - px, px), jnp.where(fy, 1 - py, py), pc), device_id_type=MESH_IDS)

        own = pltpu.make_async_copy(stage.at[3], rin_ref.at[n_chip - 1], local_sem.at[0])

        @pl.when(i == 0)
        def _():
            acc[...] = jnp.zeros_like(acc)

        @pl.when(slab < 4)
        def _():
            acc[...] += _dot(ht_ref[...], dr_ref[...])

        for n, d_ref in zip(range(4, N_DEV), (d4_ref, d5_ref, d6_ref, d7_ref)):
            @pl.when(slab == n)
            def _():
                acc[...] += _dot(ht_ref[...], d_ref[...])

        @pl.when(jnp.logical_and(i == ns - 1, jnp.logical_not(summing)))
        def _():
            @pl.when(j >= 2)
            def _():
                pre_copy(j - 2).wait_send()

            stage[j % 2] = acc[...].astype(stage.dtype)
            pre_copy(j).start()

        @pl.when(jnp.logical_and(i == ns - 1, summing))
        def _():
            pre_copy(j).wait_recv()

            @pl.when(j >= 2)
            def _():
                sum_copy(j - 2).wait_send()

            stage[2 + j % 2] = (acc[...] + pre_buf[j].astype(F32)).astype(stage.dtype)

            @pl.when(j < n_chip - 1)
            def _():
                sum_copy(j).start()

            @pl.when(j == n_chip - 1)
            def _():
                own.start()
                pre_copy(n_chip - 2).wait_send()
                pre_copy(n_chip - 1).wait_send()
                sum_copy(n_chip - 2).wait_send()
                for jj in range(n_chip - 1):
                    sum_copy(jj).wait_recv()
                own.wait()

    def part_rows(n, t, i, me_ref):
        return jnp.where(slab_of(t, me_ref) == n, i, ns - 1), 0

    return pl.pallas_call(
        body, name="in_proj_bwd_w",
        grid_spec=pltpu.PrefetchScalarGridSpec(
            num_scalar_prefetch=1, grid=(N_DEV, ns),
            in_specs=[pl.BlockSpec((D_MODEL, ts), lambda t, i, me_ref: (0, i)),
                      pl.BlockSpec((ts, nb), lambda t, i, me_ref: (
                          jnp.where(slab_of(t, me_ref) < 4, i, ns - 1), jnp.minimum(slab_of(t, me_ref), 3)))]
                     + [pl.BlockSpec((ts, nb), functools.partial(part_rows, n)) for n in range(4, N_DEV)],
            out_specs=HBM_SPEC,
            scratch_shapes=[pltpu.VMEM((D_MODEL, nb), F32),
                            pltpu.VMEM((4, D_MODEL, nb), MXU_DTYPE),
                            pltpu.VMEM((n_chip, D_MODEL, nb), MXU_DTYPE),
                            pltpu.SemaphoreType.DMA((n_chip,)), pltpu.SemaphoreType.DMA((n_chip,)),
                            pltpu.SemaphoreType.DMA((n_chip - 1,)), pltpu.SemaphoreType.DMA((n_chip - 1,)),
                            pltpu.SemaphoreType.DMA((1,))]),
        out_shape=jax.ShapeDtypeStruct((n_chip, D_MODEL, nb), MXU_DTYPE),
        compiler_params=_params(("arbitrary", "arbitrary")),
    )(jnp.reshape(me, (1,)).astype(jnp.int32), h_t, *dproj)


RET_TILE = 256


def _ret_tables(s):
    t = min(RET_TILE, s)
    half = RET_HEAD_DIM // 2
    pos = jnp.arange(s, dtype=F32)
    inv = ROPE_BASE ** (-jnp.arange(half, dtype=F32) / half)
    ang = pos[:, None] * inv[None, :]
    cos, sin = jnp.cos(ang), jnp.sin(ang)
    cos2 = jnp.concatenate([cos, cos], axis=1)
    sin2 = jnp.concatenate([-sin, sin], axis=1)
    lg = jnp.log1p(-(2.0 ** (-5.0 - jnp.arange(RET_HEADS, dtype=F32))))[:, None, None]
    n = jnp.arange(t)
    dist = (n[:, None] - n[None, :]).astype(F32)[None]
    cn = (n // CHUNK)[:, None]
    cm = (n // CHUNK)[None, :]
    mask = jnp.where((cn == cm)[None], jnp.exp(jnp.abs(dist) * lg),
                     jnp.where((cm < cn)[None], jnp.exp(dist * lg), 0.0))
    nf = n.astype(F32)[None, :, None]
    dq = jnp.broadcast_to(jnp.exp((nf + 1.0) * lg), (RET_HEADS, t, LANES))
    dk = jnp.broadcast_to(jnp.exp((t - 1.0 - nf) * lg), (RET_HEADS, t, LANES))
    gt = jnp.broadcast_to(jnp.exp(float(t) * lg), (RET_HEADS, 1, LANES))
    return cos2, sin2, mask, dq, dk, gt


def _roll_half(x):
    return pltpu.roll(x, RET_HEAD_DIM // 2, 1)


def _ret_heads_fwd(ret_ref, cos, sin, m_ref, dq_ref, dk_ref, s0):
    hd = RET_HEAD_DIM
    heads = range(RET_HEADS)
    qb, kb, vb, kdb = [], [], [], []
    for h in heads:
        q = ret_ref[:, h * hd:(h + 1) * hd]
        k = ret_ref[:, 512 + h * hd:512 + (h + 1) * hd]
        kr = (k * cos + _roll_half(k) * sin) * RET_KSCALE
        qb.append(_mx(q * cos + _roll_half(q) * sin))
        kb.append(_mx(kr))
        kdb.append(_mx(kr * dk_ref[h]))
        vb.append(_mx(ret_ref[:, 1024 + h * hd:1024 + (h + 1) * hd]))
    p = [_dot_nt(qb[h], kb[h]) for h in heads]
    cross = [_dot(qb[h], _mx(s0[h])) for h in heads]
    pb = [_mx(p[h] * m_ref[h]) for h in heads]
    o = [_dot(pb[h], vb[h]) + cross[h] * dq_ref[h] for h in heads]
    gn, rstd = [], []
    for h in heads:
        oc = o[h] - jnp.mean(o[h], axis=-1, keepdims=True)
        rstd.append(lax.rsqrt(jnp.mean(oc * oc, axis=-1, keepdims=True) + EPS))
        gn.append(oc * rstd[h])
    return qb, kb, vb, pb, kdb, gn, rstd


def _retention_fwd(ret, tables):
    cos2, sin2, mask, dq, dk, gt = tables
    s = ret.shape[0]
    t = mask.shape[1]
    nt = s // t
    hd = RET_HEAD_DIM

    def body(ret_ref, cos_ref, sin_ref, m_ref, dq_ref, dk_ref, gt_ref, y_ref, st_ref, s_scr):
        i = pl.program_id(0)

        @pl.when(i == 0)
        def _():
            s_scr[...] = jnp.zeros_like(s_scr)

        s0 = [s_scr[h] for h in range(RET_HEADS)]
        _, _, vb, _, kdb, gn, _ = _ret_heads_fwd(ret_ref, cos_ref[...], sin_ref[...], m_ref, dq_ref, dk_ref, s0)
        kv = [_dot_tn(kdb[h], vb[h]) for h in range(RET_HEADS)]
        for h in range(RET_HEADS):
            g = ret_ref[:, 1536 + h * hd:1536 + (h + 1) * hd]
            st_ref[h] = s0[h]
            y_ref[:, h * hd:(h + 1) * hd] = (gn[h] * (g * _sigmoid(g))).astype(y_ref.dtype)
            s_scr[h] = s0[h] * gt_ref[h] + kv[h]

    full3 = lambda a: pl.BlockSpec(a.shape, lambda i: (0, 0, 0))
    return pl.pallas_call(
        body, name="retention_fwd", grid=(nt,),
        in_specs=[pl.BlockSpec((t, 2048), lambda i: (i, 0)),
                  pl.BlockSpec((t, LANES), lambda i: (i, 0)),
                  pl.BlockSpec((t, LANES), lambda i: (i, 0)),
                  full3(mask), full3(dq), full3(dk), full3(gt)],
        out_specs=[pl.BlockSpec((t, 512), lambda i: (i, 0)),
                   pl.BlockSpec((None, RET_HEADS, hd, hd), lambda i: (i, 0, 0, 0))],
        out_shape=[jax.ShapeDtypeStruct((s, 512), MXU_DTYPE),
                   jax.ShapeDtypeStruct((nt, RET_HEADS, hd, hd), F32)],
        scratch_shapes=[pltpu.VMEM((RET_HEADS, hd, hd), F32)],
        compiler_params=_params(("arbitrary",)),
    )(ret, cos2, sin2, mask, dq, dk, gt)


def _retention_bwd(ret, states, dy, tables, dwo_parts):
    cos2, sin2, mask, dq, dk, gt = tables
    s = ret.shape[0]
    t = mask.shape[1]
    nt = s // t
    hd = RET_HEAD_DIM

    def body(ret_ref, st_ref, dy_ref, cos_ref, sin_ref, m_ref, dq_ref, dk_ref, gt_ref, dwo_ref,
             d_ref, rout_ref, ds_scr, send_sems, recv_sems, local_sem):
        i = pl.program_id(0)
        start, finish = _scatter_plan(dwo_ref, rout_ref, send_sems, recv_sems, local_sem)

        @pl.when(i == 0)
        def _():
            start()
            ds_scr[...] = jnp.zeros_like(ds_scr)

        cos = cos_ref[...]
        sin = sin_ref[...]
        heads = range(RET_HEADS)
        s0 = [st_ref[h] for h in heads]
        ds = [ds_scr[h] for h in heads]
        dsb = [_mx(ds[h]) for h in heads]
        qb, kb, vb, pb, kdb, gn, rstd = _ret_heads_fwd(ret_ref, cos, sin, m_ref, dq_ref, dk_ref, s0)
        dob, dodb = [], []
        for h in heads:
            g = ret_ref[:, 1536 + h * hd:1536 + (h + 1) * hd]
            dyv = dy_ref[:, h * hd:(h + 1) * hd]
            sg = _sigmoid(g)
            d_ref[:, 1536 + h * hd:1536 + (h + 1) * hd] = (
                dyv * gn[h] * (sg * (1.0 + g * (1.0 - sg)))).astype(d_ref.dtype)
            dgn = dyv * (g * sg)
            do = rstd[h] * (dgn - jnp.mean(dgn, axis=-1, keepdims=True)
                            - gn[h] * jnp.mean(dgn * gn[h], axis=-1, keepdims=True))
            dob.append(_mx(do))
            dodb.append(_mx(do * dq_ref[h]))
        dp = [_dot_nt(dob[h], vb[h]) for h in heads]
        dv = [_dot_tn(pb[h], dob[h]) + _dot(kdb[h], dsb[h]) for h in heads]
        dq_cross = [_dot_nt(dodb[h], _mx(s0[h])) for h in heads]
        dk_cross = [_dot_nt(vb[h], dsb[h]) for h in heads]
        ds_new = [_dot_tn(qb[h], dodb[h]) for h in heads]
        dpb = [_mx(dp[h] * m_ref[h]) for h in heads]
        dqr = [_dot(dpb[h], kb[h]) + dq_cross[h] for h in heads]
        dkr = [(_dot_tn(dpb[h], qb[h]) + dk_cross[h] * dk_ref[h]) * RET_KSCALE for h in heads]
        for h in heads:
            d_ref[:, 1024 + h * hd:1024 + (h + 1) * hd] = dv[h].astype(d_ref.dtype)
            d_ref[:, h * hd:(h + 1) * hd] = (dqr[h] * cos + _roll_half(dqr[h] * sin)).astype(d_ref.dtype)
            d_ref[:, 512 + h * hd:512 + (h + 1) * hd] = (
                dkr[h] * cos + _roll_half(dkr[h] * sin)).astype(d_ref.dtype)
            ds_scr[h] = ds[h] * gt_ref[h] + ds_new[h]
        pl.when(i == nt - 1)(finish)

    full3 = lambda a: pl.BlockSpec(a.shape, lambda i: (0, 0, 0))
    rev = lambda i: (nt - 1 - i, 0)
    return pl.pallas_call(
        body, name="retention_bwd", grid=(nt,),
        in_specs=[pl.BlockSpec((t, 2048), rev),
                  pl.BlockSpec((None, RET_HEADS, hd, hd), lambda i: (nt - 1 - i, 0, 0, 0)),
                  pl.BlockSpec((t, 512), rev),
                  pl.BlockSpec((t, LANES), rev),
                  pl.BlockSpec((t, LANES), rev),
                  full3(mask), full3(dq), full3(dk), full3(gt), HBM_SPEC],
        out_specs=[pl.BlockSpec((t, 2048), rev), HBM_SPEC],
        out_shape=[jax.ShapeDtypeStruct((s, 2048), MXU_DTYPE),
                   jax.ShapeDtypeStruct(dwo_parts.shape, dwo_parts.dtype)],
        scratch_shapes=[pltpu.VMEM((RET_HEADS, hd, hd), F32),
                        pltpu.SemaphoreType.DMA((N_PEERS,)), pltpu.SemaphoreType.DMA((N_PEERS,)),
                        pltpu.SemaphoreType.DMA((1,))],
        compiler_params=_params(("arbitrary",)),
    )(ret, states, dy, cos2, sin2, mask, dq, dk, gt, dwo_parts)


SB_BLOCK = 256
SB_CHUNK = 32


SB_SKIP = 104.0
NO_KEYS = -1e30


def _split_dots(xs, u):
    parts = []
    for x in xs:
        hi = lax.bitcast_convert_type(lax.bitcast_convert_type(x, jnp.uint32) & jnp.uint32(0xFFFF0000), F32)
        parts += [_mx(hi), _mx(x - hi)]
    out = _dot(jnp.concatenate(parts, axis=0), u)
    n = xs[0].shape[0]
    return [out[2 * k * n:(2 * k + 1) * n] + out[(2 * k + 1) * n:(2 * k + 2) * n] for k in range(len(xs))]


def _split_dot(x, u):
    return _split_dots([x], u)[0]


def _sb_pair_weights(lb, lk, allowed, u_gt):
    blk = lb.shape[0]
    lk_p, lk_d = lk[:, :blk], lk[:, blk:]
    r_d = _rowsum(lk_d)
    cs_p, cs_d = _split_dots([lk_p, lk_d], u_gt)
    a = jnp.exp(lb + jnp.concatenate([cs_p + r_d, cs_d], axis=1))
    return jnp.where(allowed, a, 0.0), r_d, r_d + _rowsum(lk_p)


def _sb_logits(q, k, causal):
    z = _dot_nt(q, k)
    l1p = jnp.log(1.0 + jnp.exp(-jnp.abs(z)))
    lk = -(jnp.maximum(z, 0.0) + l1p)
    if causal is not None:
        lk = jnp.where(causal, lk, 0.0)
    return jnp.minimum(z, 0.0) - l1p, lk


def _sb_weights(lb, lk, r, u_gt, causal):
    a = jnp.exp(lb + _split_dot(lk, u_gt) + r)
    return a if causal is None else jnp.where(causal, a, 0.0)


def _rowsum(x):
    return jnp.sum(x, axis=1, keepdims=True)


def _sb_pair_tile(i, blk):
    row = lax.broadcasted_iota(jnp.int32, (blk, 2 * blk), 0)
    col = lax.broadcasted_iota(jnp.int32, (blk, 2 * blk), 1)
    first_col = jnp.where(i >= 1, 0, blk)
    allowed = jnp.logical_and(row > col - blk, col >= first_col)
    rows_p = pl.ds(pl.multiple_of(jnp.maximum(i - 1, 0) * blk, blk), blk)
    rows_d = pl.ds(pl.multiple_of(i * blk, blk), blk)
    return allowed, rows_p, rows_d


def _sb_fwd(qkv, sg):
    s = qkv.shape[0]
    blk = min(SB_BLOCK, s)
    nq = s // blk
    hd = SB_HEAD_DIM

    ch = min(SB_CHUNK, blk)

    def body(q_ref, k_ref, v_ref, g_ref, y_ref, o_ref, z_scr, lhs_scr, cs_scr, a_scr):
        i = pl.program_id(1)
        row = lax.broadcasted_iota(jnp.int32, (blk, blk), 0)
        col = lax.broadcasted_iota(jnp.int32, (blk, blk), 1)
        u_gt = (row > col).astype(MXU_DTYPE)
        heads = [slice(hh * hd, (hh + 1) * hd) for hh in range(2)]
        qs = [q_ref[:, ls] for ls in heads]
        _, rows_p, rows_d = _sb_pair_tile(i, blk)
        has_prev = i >= 1
        crow = lax.broadcasted_iota(jnp.int32, (ch, blk), 0)
        ccol = lax.broadcasted_iota(jnp.int32, (ch, blk), 1)

        def logits(hh):
            kc = jnp.concatenate([k_ref[rows_p, heads[hh]], k_ref[rows_d, heads[hh]]], axis=0)
            z_scr[hh] = _dot_nt(qs[hh], kc)

        def keep_parts(hh):
            r_d, r_all = [], []
            for c in range(blk // ch):
                rows = pl.ds(c * ch, ch)
                causal = crow + c * ch > ccol
                z = z_scr[hh, rows, :]
                l1p = jnp.log(1.0 + jnp.exp(-jnp.abs(z)))
                lb = jnp.minimum(z, 0.0) - l1p
                z_scr[hh, rows, :] = lb
                lk = lb - z
                lk_p = lk[:, :blk]
                lk_d = jnp.where(causal, lk[:, blk:], 0.0)
                lhs_scr[hh, pl.ds(c * ch, ch), :] = _mx(lk_p)
                lhs_scr[hh, pl.ds(blk + c * ch, ch), :] = _mx(lk_d)
                r_d.append(_rowsum(lk_d))
                r_all.append(r_d[c] + _rowsum(lk_p))
            return r_d, jnp.concatenate(r_all, axis=0)

        def suffix_sums(hh):
            cs_scr[hh] = _dot(lhs_scr[hh], u_gt)

        def weights(hh, r_d):
            for c in range(blk // ch):
                rows = pl.ds(c * ch, ch)
                causal = crow + c * ch > ccol
                cs_p = cs_scr[hh, pl.ds(c * ch, ch), :] + jnp.where(has_prev, r_d[c], NO_KEYS)
                cs_d = cs_scr[hh, pl.ds(blk + c * ch, ch), :]
                lb = z_scr[hh, rows, :]
                a_p = jnp.exp(lb[:, :blk] + cs_p)
                a_d = jnp.where(causal, jnp.exp(lb[:, blk:] + cs_d), 0.0)
                a_scr[hh, rows, :] = _mx(jnp.concatenate([a_p, a_d], axis=1))

        def values(hh):
            vc = jnp.concatenate([v_ref[rows_p, heads[hh]], v_ref[rows_d, heads[hh]]], axis=0)
            return _dot(a_scr[hh], vc)

        def block(hh, j, r):
            start = pl.multiple_of(j * blk, blk)
            lb, lk = _sb_logits(qs[hh], k_ref[pl.ds(start, blk), heads[hh]], None)
            a = _sb_weights(lb, lk, r, u_gt, None)
            return _dot(_mx(a), v_ref[pl.ds(start, blk), heads[hh]]), r + _rowsum(lk)

        def more(n, r0, r1):
            return jnp.logical_and(n < i, jnp.max(jnp.maximum(r0, r1)) > -SB_SKIP)

        logits(0)
        logits(1)
        gates = [g_ref[:, ls] * _sigmoid(g_ref[:, ls]) for ls in heads]
        rd0, r0 = keep_parts(0)
        suffix_sums(0)
        rd1, r1 = keep_parts(1)
        suffix_sums(1)
        go = more(jnp.int32(1), r0, r1)
        weights(0, rd0)
        acc0 = values(0)
        weights(1, rd1)
        acc1 = values(1)

        def step(c):
            _, n, acc0, r0, acc1, r1 = c
            pv0, r0 = block(0, i - 1 - n, r0)
            pv1, r1 = block(1, i - 1 - n, r1)
            return more(n + 1, r0, r1), n + 1, acc0 + pv0, r0, acc1 + pv1, r1

        _, _, acc0, _, acc1, _ = lax.while_loop(lambda c: c[0], step, (go, jnp.int32(1), acc0, r0, acc1, r1))
        for ls, acc, gate in zip(heads, (acc0, acc1), gates):
            o_ref[:, ls] = acc
            y_ref[:, ls] = (acc * gate).astype(y_ref.dtype)

    qblk = pl.BlockSpec((blk, LANES), lambda p, i: (i, p))
    return pl.pallas_call(
        body, name="stickbreak_fwd", grid=(SB_HEADS // 2, nq),
        in_specs=[qblk,
                  pl.BlockSpec((s, LANES), lambda p, i: (0, 4 + p)),
                  pl.BlockSpec((s, LANES), lambda p, i: (0, 8 + p)),
                  qblk],
        out_specs=[qblk, qblk],
        out_shape=[jax.ShapeDtypeStruct((s, 512), MXU_DTYPE),
                   jax.ShapeDtypeStruct((s, 512), F32)],
        scratch_shapes=[pltpu.VMEM((2, blk, 2 * blk), F32),
                        pltpu.VMEM((2, 2 * blk, blk), MXU_DTYPE),
                        pltpu.VMEM((2, 2 * blk, blk), F32),
                        pltpu.VMEM((2, blk, 2 * blk), MXU_DTYPE)],
        compiler_params=_params(("arbitrary", "arbitrary")),
    )(qkv, qkv, qkv, sg)


def _sb_bwd(qkv, sg, o, dy):
    s = qkv.shape[0]
    blk = min(SB_BLOCK, s)
    nq = s // blk
    hd = SB_HEAD_DIM
    assert nq <= LANES
    ch = min(SB_CHUNK, blk)

    def body(q_ref, k_ref, v_ref, g_ref, o_ref, dy_ref, dq_ref, dk_ref, dv_ref, dg_ref, dk_scr, dv_scr,
             z_scr, g_scr, lhs_scr, cs_scr, a_scr, dz_scr):
        i = pl.program_id(1)

        @pl.when(i == 0)
        def _():
            dk_scr[...] = jnp.zeros_like(dk_scr)
            dv_scr[...] = jnp.zeros_like(dv_scr)

        row = lax.broadcasted_iota(jnp.int32, (blk, blk), 0)
        col = lax.broadcasted_iota(jnp.int32, (blk, blk), 1)
        lane = lax.broadcasted_iota(jnp.int32, (blk, LANES), 1)
        u_gt = (row > col).astype(MXU_DTYPE)
        u_lt = (row < col).astype(MXU_DTYPE)
        heads = [slice(hh * hd, (hh + 1) * hd) for hh in range(2)]
        qs = [q_ref[:, ls] for ls in heads]
        _, rows_p, rows_d = _sb_pair_tile(i, blk)
        has_prev = i >= 1
        crow = lax.broadcasted_iota(jnp.int32, (ch, blk), 0)
        ccol = lax.broadcasted_iota(jnp.int32, (ch, blk), 1)
        nch = blk // ch
        kcs = [jnp.concatenate([k_ref[rows_p, ls], k_ref[rows_d, ls]], axis=0) for ls in heads]
        dobs = []

        def gate_grads(hh):
            ls = heads[hh]
            g = g_ref[:, ls]
            dyv = dy_ref[:, ls]
            sgm = _sigmoid(g)
            dg_ref[:, ls] = (dyv * o_ref[:, ls] * (sgm * (1.0 + g * (1.0 - sgm)))).astype(dg_ref.dtype)
            dobs.append(_mx(dyv * (g * sgm)))

        def split_rows(hh, c, part, x):
            lhs_scr[hh, pl.ds(part * blk + c * ch, ch), :] = _mx(x)

        def summed_rows(hh, c, part):
            return cs_scr[hh, pl.ds(part * blk + c * ch, ch), :]

        def logits(hh):
            z_scr[hh] = _dot_nt(qs[hh], kcs[hh])

        def weight_grads(hh):
            vc = jnp.concatenate([v_ref[rows_p, heads[hh]], v_ref[rows_d, heads[hh]]], axis=0)
            g_scr[hh] = _dot_nt(dobs[hh], vc)

        def keep_parts(hh):
            r_d, r_all = [], []
            for c in range(nch):
                rows = pl.ds(c * ch, ch)
                z = z_scr[hh, rows, :]
                l1p = jnp.log(1.0 + jnp.exp(-jnp.abs(z)))
                lb = jnp.minimum(z, 0.0) - l1p
                z_scr[hh, rows, :] = lb
                lk = lb - z
                lk_p = lk[:, :blk]
                lk_d = jnp.where(crow + c * ch > ccol, lk[:, blk:], 0.0)
                split_rows(hh, c, 0, lk_p)
                split_rows(hh, c, 1, lk_d)
                r_d.append(_rowsum(lk_d))
                r_all.append(r_d[c] + _rowsum(lk_p))
            return r_d, jnp.concatenate(r_all, axis=0)

        def weights(hh, r_d):
            g_p = []
            for c in range(nch):
                rows = pl.ds(c * ch, ch)
                lb = z_scr[hh, rows, :]
                a_p = jnp.exp(lb[:, :blk] + (summed_rows(hh, c, 0) + jnp.where(has_prev, r_d[c], NO_KEYS)))
                a_d = jnp.where(crow + c * ch > ccol, jnp.exp(lb[:, blk:] + summed_rows(hh, c, 1)), 0.0)
                a = jnp.concatenate([a_p, a_d], axis=1)
                a_scr[hh, rows, :] = _mx(a)
                gm = g_scr[hh, rows, :] * a
                g_scr[hh, rows, :] = gm
                split_rows(hh, c, 0, gm[:, :blk])
                split_rows(hh, c, 1, gm[:, blk:])
                g_p.append(_rowsum(gm[:, :blk]))
            return g_p

        def logit_grads(hh, pg, g_p):
            for c in range(nch):
                rows = pl.ds(c * ch, ch)
                pre = jnp.concatenate([summed_rows(hh, c, 0) + pg[c * ch:(c + 1) * ch],
                                       summed_rows(hh, c, 1) + (pg[c * ch:(c + 1) * ch] + g_p[c])], axis=1)
                gm = g_scr[hh, rows, :]
                dz = gm - (gm + pre) * jnp.exp(z_scr[hh, rows, :])
                dz_p = dz[:, :blk]
                dz_d = jnp.where(crow + c * ch > ccol, dz[:, blk:], 0.0)
                dz_scr[hh, rows, :] = _mx(jnp.concatenate([dz_p, dz_d], axis=1))

        def products(hh, acc):
            ls = heads[hh]
            dk_scr[hh, rows_p, :] += _dot_tn(dz_scr[hh, :, 0:blk], qs[hh])
            dk_scr[hh, rows_d, :] += _dot_tn(dz_scr[hh, :, blk:2 * blk], qs[hh])
            dv_scr[hh, rows_p, :] += _dot_tn(a_scr[hh, :, 0:blk], dobs[hh])
            dv_scr[hh, rows_d, :] += _dot_tn(a_scr[hh, :, blk:2 * blk], dobs[hh])
            dq_ref[:, ls] = ((acc + _dot(dz_scr[hh], kcs[hh])) * SB_SCALE).astype(dq_ref.dtype)

        def suffix_sums(hh):
            cs_scr[hh] = _dot(lhs_scr[hh], u_gt)

        def prefix_sums(hh):
            cs_scr[hh] = _dot(lhs_scr[hh], u_lt)

        def more(n, r0, r1):
            return jnp.logical_and(n < i, jnp.max(jnp.maximum(r0, r1)) > -SB_SKIP)

        gate_grads(0)
        gate_grads(1)
        logits(0)
        weight_grads(0)
        logits(1)
        weight_grads(1)
        rd0, ra0 = keep_parts(0)
        suffix_sums(0)
        rd1, ra1 = keep_parts(1)
        suffix_sums(1)
        go = more(jnp.int32(1), ra0, ra1)
        gp0 = weights(0, rd0)
        prefix_sums(0)
        gp1 = weights(1, rd1)
        prefix_sums(1)

        def scan_block(hh, j, r, rmat):
            start = pl.multiple_of(j * blk, blk)
            _, lk = _sb_logits(qs[hh], k_ref[pl.ds(start, blk), heads[hh]], None)
            return r + _rowsum(lk), jnp.where(lane == j, r, rmat)

        def scan_step(c):
            _, n, r0, rmat0, r1, rmat1 = c
            r0, rmat0 = scan_block(0, i - 1 - n, r0, rmat0)
            r1, rmat1 = scan_block(1, i - 1 - n, r1, rmat1)
            return more(n + 1, r0, r1), n + 1, r0, rmat0, r1, rmat1

        zmat = jnp.zeros((blk, LANES), F32)
        _, n, _, rmat0, _, rmat1 = lax.while_loop(lambda c: c[0], scan_step,
                                                  (go, jnp.int32(1), ra0, zmat, ra1, zmat))
        rmats = (rmat0, rmat1)

        def block(hh, j, pg):
            ls = heads[hh]
            start = pl.multiple_of(j * blk, blk)
            k = k_ref[pl.ds(start, blk), ls]
            lb, lk = _sb_logits(qs[hh], k, None)
            r = _rowsum(jnp.where(lane == j, rmats[hh], 0.0))
            a = _sb_weights(lb, lk, r, u_gt, None)
            gm = _dot_nt(dobs[hh], v_ref[pl.ds(start, blk), ls]) * a
            dzb = _mx(gm - (gm + (pg + _split_dot(gm, u_lt))) * jnp.exp(lb))
            dk_scr[hh, pl.ds(start, blk), :] += _dot_tn(dzb, qs[hh])
            dv_scr[hh, pl.ds(start, blk), :] += _dot_tn(_mx(a), dobs[hh])
            return _dot(dzb, k), pg + _rowsum(gm)

        def step(t, c):
            acc0, pg0, acc1, pg1 = c
            dq0, pg0 = block(0, i - n + t, pg0)
            dq1, pg1 = block(1, i - n + t, pg1)
            return acc0 + dq0, pg0, acc1 + dq1, pg1

        zero = jnp.zeros((blk, 1), F32)
        zacc = jnp.zeros((blk, hd), F32)
        acc0, pg0, acc1, pg1 = lax.fori_loop(0, n - 1, step, (zacc, zero, zacc, zero))
        logit_grads(0, pg0, gp0)
        logit_grads(1, pg1, gp1)
        products(0, acc0)
        products(1, acc1)

        @pl.when(i == nq - 1)
        def _():
            for hh in range(2):
                ls = slice(hh * hd, (hh + 1) * hd)
                dk_ref[:, ls] = dk_scr[hh].astype(dk_ref.dtype)
                dv_ref[:, ls] = dv_scr[hh].astype(dv_ref.dtype)

    qblk = lambda c0: pl.BlockSpec((blk, LANES), lambda p, i: (i, c0 + p))
    full = lambda c0: pl.BlockSpec((s, LANES), lambda p, i: (0, c0 + p))
    half = jax.ShapeDtypeStruct((s, 512), MXU_DTYPE)
    return pl.pallas_call(
        body, name="stickbreak_bwd", grid=(SB_HEADS // 2, nq),
        in_specs=[qblk(0), full(4), full(8), qblk(0), qblk(0), qblk(4)],
        out_specs=[qblk(0), full(0), full(0), qblk(0)],
        out_shape=[half, half, half, half],
        scratch_shapes=[pltpu.VMEM((2, s, hd), F32), pltpu.VMEM((2, s, hd), F32),
                        pltpu.VMEM((2, blk, 2 * blk), F32),
                        pltpu.VMEM((2, blk, 2 * blk), F32),
                        pltpu.VMEM((2, 2 * blk, blk), MXU_DTYPE),
                        pltpu.VMEM((2, 2 * blk, blk), F32),
                        pltpu.VMEM((2, blk, 2 * blk), MXU_DTYPE),
                        pltpu.VMEM((2, blk, 2 * blk), MXU_DTYPE)],
        compiler_params=_params(("arbitrary", "arbitrary")),
    )(qkv, qkv, qkv, sg, o, dy)


def _layer_fwd(layer, x, mod, norm_g, w_in_g, w_out_g, tables, gather=(), loss_head=None):
    shift, scale1p, gate = mod[0:1], 1.0 + mod[1:2], mod[2:3]
    ret, qkv, sg, *gathered = _ln_proj(x, shift, scale1p, norm_g, w_in_g, layer, gather)
    y_r, states = _retention_fwd(ret, tables)
    y_s, o_s = _sb_fwd(qkv, sg)
    x_next = _out_proj(x, gate, y_r, y_s, w_out_g, layer, loss_head)
    saved = (x, shift, scale1p, gate, ret, qkv, sg, y_r, states, y_s, o_s)
    return x_next, saved, gathered


def _layer_bwd(layer, me, dx_out, saved, norm_g, w_in_g, w_out_g, tables):
    x, shift, scale1p, gate, ret, qkv, sg, y_r, states, y_s, o_s = saved
    dy, dw_out, dgate = _out_proj_bwd(dx_out, gate, y_r, y_s, w_out_g, layer)
    dwo_parts = _mx(dw_out.reshape(N_DEV, D_MODEL // N_DEV, D_MODEL))
    d_ret, r_out = _retention_bwd(ret, states, dy, tables, dwo_parts)
    d_q, d_k, d_v, d_g = _sb_bwd(qkv, sg, o_s, dy)
    dproj = (d_ret, d_q, d_k, d_v, d_g)
    dx, dshift, dscale, dnorm_g, h_t = _in_proj_bwd_x(x, dx_out, dproj, shift, scale1p, norm_g, w_in_g, layer)
    r_in = _in_proj_bwd_w(me, h_t, dproj)
    dmod = jnp.concatenate([dshift, dscale, dgate], axis=1)
    return dx, r_in, r_out, dmod, dnorm_g


MESH_IDS = pl.DeviceIdType.MESH
N_PEERS = N_DEV - 1
HBM_SPEC = pl.BlockSpec(memory_space=pl.ANY)


def _my_place():
    return lax.axis_index("x"), lax.axis_index("y"), lax.axis_index("c")


def _linear(px, py, pc):
    return 4 * px + 2 * py + pc


def _all_gather(blocks):
    n_arr = len(blocks)

    def body(*refs):
        start, forward, finish = _gather_plan(refs[:n_arr], refs[n_arr:2 * n_arr], *refs[2 * n_arr:])
        start()
        forward()
        finish()

    return pl.pallas_call(
        body, name="all_gather",
        out_shape=_gathered_shapes(blocks),
        in_specs=[HBM_SPEC] * n_arr, out_specs=[HBM_SPEC] * n_arr,
        scratch_shapes=_gather_sems(n_arr),
    )(*blocks)


def _gathered_shapes(blocks):
    return [jax.ShapeDtypeStruct((N_DEV * b.shape[0], b.shape[1]), b.dtype) for b in blocks]


def _gather_sems(n_arr):
    return [pltpu.SemaphoreType.DMA((n_arr * N_PEERS,)), pltpu.SemaphoreType.DMA((n_arr * N_PEERS,)),
            pltpu.SemaphoreType.DMA((n_arr,))]


def _gather_plan(x_refs, out_refs, send_sems, recv_sems, local_sems):
    n_arr = len(x_refs)
    x, y, c = _my_place()
    me, sibling = (x, y, c), (x, y, 1 - c)
    chips = [(1 - x, y), (x, 1 - y), (1 - x, 1 - y)]

    def rows(a, place):
        m = x_refs[a].shape[0]
        return out_refs[a].at[pl.ds(_linear(*place) * m, m), :]

    def copy(a, k, block, to, src=None):
        return pltpu.make_async_remote_copy(
            src_ref=rows(a, block) if src is None else src, dst_ref=rows(a, block),
            send_sem=send_sems.at[a * N_PEERS + k], recv_sem=recv_sems.at[a * N_PEERS + k],
            device_id=to, device_id_type=MESH_IDS)

    mine = [pltpu.make_async_copy(x_refs[a], rows(a, me), local_sems.at[a]) for a in range(n_arr)]
    first = []
    for a in range(n_arr):
        first.append(copy(a, 0, me, sibling, src=x_refs[a]))
        first += [copy(a, 1 + j, me, (*chip, c), src=x_refs[a]) for j, chip in enumerate(chips)]
    passed = [copy(a, 4 + j, (*chip, c), sibling) for j, chip in enumerate(chips) for a in range(n_arr)]

    def start():
        for cp in mine + first:
            cp.start()

    def forward():
        for j, chip in enumerate(chips):
            for a in range(n_arr):
                copy(a, 1 + j, (*chip, c), me).wait_recv()
                passed[j * n_arr + a].start()

    def finish():
        for a in range(n_arr):
            copy(a, 0, sibling, me).wait_recv()
            for j, chip in enumerate(chips):
                copy(a, 4 + j, (*chip, 1 - c), me).wait_recv()
        for cp in first + passed:
            cp.wait_send()
        for cp in mine:
            cp.wait()

    return start, forward, finish


def _ada_fwd(c_all, w_ada, b_cols):
    cols = w_ada.shape[2]

    def body(c_ref, w_ref, b_ref, ca_ref, mod_ref):
        cv = c_ref[...]
        ca = cv * _sigmoid(cv)
        ca_ref[...] = ca
        cb = _mx(jnp.concatenate([ca, ca], axis=0))
        for l in range(DEPTH):
            mod_ref[l * N_DEV:(l + 1) * N_DEV, :] = _dot(cb, _mx(w_ref[l]))[0:N_DEV] + b_ref[l]

    return pl.pallas_call(
        body, name="ada_fwd",
        out_shape=[jax.ShapeDtypeStruct((N_DEV, D_MODEL), F32),
                   jax.ShapeDtypeStruct((DEPTH * N_DEV, cols), F32)],
        compiler_params=_params(),
    )(c_all, w_ada, b_cols)


def _ada_bwd(c_act_t, dmod_cols):
    cols = dmod_cols.shape[2]

    def body(ca_ref, dm_ref, o_ref):
        ca = _mx(ca_ref[...]).astype(F32)
        for l in range(DEPTH):
            o_ref[l] = jnp.dot(ca, _mx(dm_ref[l]).astype(F32),
                               precision=lax.Precision.HIGHEST, preferred_element_type=F32)

    return pl.pallas_call(
        body, name="ada_bwd",
        out_shape=jax.ShapeDtypeStruct((DEPTH, D_MODEL, cols), F32),
        compiler_params=_params(),
    )(c_act_t, dmod_cols)


def _adamw_store(g, w_ref, m_ref, v_ref, g_ref, d_ref, mo_ref, vo_ref):
    m2 = ADAM_B1 * m_ref[...] + (1.0 - ADAM_B1) * g
    v2 = ADAM_B2 * v_ref[...] + (1.0 - ADAM_B2) * (g * g)
    m_hat = m2 / (1.0 - ADAM_B1 ** ADAM_STEP)
    v_hat = v2 / (1.0 - ADAM_B2 ** ADAM_STEP)
    g_ref[...] = g
    d_ref[...] = -ADAM_LR * (m_hat / (jnp.sqrt(v_hat) + ADAM_EPS) + ADAM_WD * w_ref[...])
    mo_ref[...] = m2
    vo_ref[...] = v2


def _slab_sum(p_ref):
    g = p_ref[0].astype(F32)
    for sl in range(1, p_ref.shape[0]):
        g = g + p_ref[sl].astype(F32)
    return g


def _sum_adamw_layers(parts, w, m, v):
    n_slab, rows, cols = parts[0].shape
    tr = min(256, rows)
    nt = rows // tr

    def body(p0_ref, p1_ref, w_ref, m_ref, v_ref, g_ref, d_ref, mo_ref, vo_ref):
        for l, p_ref in enumerate((p0_ref, p1_ref)):
            @pl.when(pl.program_id(0) == l)
            def _():
                _adamw_store(_slab_sum(p_ref), w_ref, m_ref, v_ref, g_ref, d_ref, mo_ref, vo_ref)

    p_specs = [pl.BlockSpec((n_slab, tr, cols), lambda l, i: (0, i * (1 - l) + (nt - 1) * l, 0)),
               pl.BlockSpec((n_slab, tr, cols), lambda l, i: (0, i * l, 0))]
    blk = pl.BlockSpec((None, tr, cols), lambda l, i: (l, i, 0))
    shp = jax.ShapeDtypeStruct((DEPTH, rows, cols), F32)
    return pl.pallas_call(
        body, name="sum_adamw_layers", grid=(DEPTH, nt),
        in_specs=p_specs + [blk, blk, blk],
        out_specs=[blk, blk, blk, blk],
        out_shape=[shp, shp, shp, shp],
        compiler_params=_params(("arbitrary", "arbitrary")),
    )(parts[0], parts[1], w, m, v)


def _sum_adamw(parts, w, m, v):
    n_slab, rows, cols = parts.shape
    tr = min(256, rows)

    def body(p_ref, w_ref, m_ref, v_ref, g_ref, d_ref, mo_ref, vo_ref):
        _adamw_store(_slab_sum(p_ref), w_ref, m_ref, v_ref, g_ref, d_ref, mo_ref, vo_ref)

    blk = pl.BlockSpec((tr, cols), lambda i: (i, 0))
    shp = jax.ShapeDtypeStruct((rows, cols), F32)
    return pl.pallas_call(
        body, name="sum_adamw", grid=(rows // tr,),
        in_specs=[pl.BlockSpec((n_slab, tr, cols), lambda i: (0, i, 0)), blk, blk, blk],
        out_specs=[blk, blk, blk, blk],
        out_shape=[shp, shp, shp, shp],
        compiler_params=_params(("arbitrary",)),
    )(parts, w, m, v)


SMALL_ROWS = 16


def kernel(x, c, norm_g, w_ada, b_ada, w_in, w_out, final_g, loss_target, m_norm_g, m_w_ada, m_b_ada, m_w_in, m_w_out, m_final_g, v_norm_g, v_w_ada, v_b_ada, v_w_in, v_w_out, v_final_g):
    me = _linear(*_my_place())
    in_cols = w_in.shape[2]
    out_rows = w_out.shape[1]
    ada_cols = w_ada.shape[2]

    w_in_m, w_out_m = _mx(w_in), _mx(w_out)
    g_in, g_out, g_c = _all_gather([w_in_m[0], w_out_m[0], jnp.broadcast_to(c, (8, D_MODEL))])
    w_in_g = [g_in.reshape(N_DEV, 1, D_MODEL, in_cols), None]
    w_out_g = [g_out.reshape(N_DEV, 1, out_rows, D_MODEL), None]
    c_all = g_c.reshape(N_DEV, 8, D_MODEL)[:, 0]

    b_cols = lax.dynamic_slice_in_dim(b_ada, me * ada_cols, ada_cols, axis=1)[:, None, :]
    c_act, mod_cols = _ada_fwd(c_all, w_ada, b_cols)
    (g_mod,) = _all_gather([mod_cols])
    g_mod = g_mod.reshape(N_DEV, DEPTH, N_DEV, ada_cols)
    mod = lax.dynamic_index_in_dim(g_mod, me, axis=2, keepdims=False)
    mod = mod.transpose(1, 0, 2).reshape(DEPTH, 3, D_MODEL)

    tables = _ret_tables(x.shape[1])
    h = x[0]
    saved = []
    for l in range(DEPTH):
        nxt = (w_in_m[l + 1], w_out_m[l + 1]) if l + 1 < DEPTH else ()
        head = None if nxt else (final_g[None], loss_target[0])
        h, sv, gathered = _layer_fwd(0, h, mod[l], norm_g[l:l + 1], w_in_g[l], w_out_g[l], tables, nxt, head)
        if nxt:
            w_in_g[l + 1] = gathered[0].reshape(N_DEV, 1, D_MODEL, in_cols)
            w_out_g[l + 1] = gathered[1].reshape(N_DEV, 1, out_rows, D_MODEL)
        saved.append(sv)
    dx, loss_part, dfg = h
    r_in, r_out, small = [None] * DEPTH, [None] * DEPTH, [None] * DEPTH
    for l in reversed(range(DEPTH)):
        dx, r_in[l], r_out[l], dmod, dng = _layer_bwd(0, me, dx, saved[l], norm_g[l:l + 1], w_in_g[l], w_out_g[l], tables)
        small[l] = (dmod.reshape(3, D_MODEL), dng)

    pad = jnp.zeros((SMALL_ROWS - 10, D_MODEL), F32)
    small_block = jnp.concatenate([small[0][0], small[1][0], small[0][1], small[1][1], dfg,
                                   jnp.broadcast_to(loss_part, (1, D_MODEL)), pad], axis=0)
    (g_small,) = _all_gather([small_block])
    g_small = g_small.reshape(N_DEV, SMALL_ROWS, D_MODEL)

    def small_pack(b, n, f, fill):
        return jnp.concatenate([b.reshape(6, D_MODEL), n, f[None],
                                jnp.full((SMALL_ROWS - 9, D_MODEL), fill, F32)], axis=0)

    s_g, s_d, s_m, s_v = _sum_adamw(g_small, small_pack(b_ada, norm_g, final_g, 0.0),
                                    small_pack(m_b_ada, m_norm_g, m_final_g, 0.0),
                                    small_pack(v_b_ada, v_norm_g, v_final_g, 1.0))
    loss = s_g[9, 0]

    def small_unpack(a):
        return a[0:6].reshape(DEPTH, 3 * D_MODEL), a[6:8], a[8]

    dmod_all = g_small[:, 0:6].reshape(N_DEV, DEPTH, 3 * D_MODEL).transpose(1, 0, 2)
    dmod_cols = lax.dynamic_slice_in_dim(dmod_all, me * ada_cols, ada_cols, axis=2)
    g_ada = _ada_bwd(c_act.T, dmod_cols).reshape(1, DEPTH * D_MODEL, ada_cols)
    ada = _sum_adamw(g_ada, *[a.reshape(DEPTH * D_MODEL, ada_cols) for a in (w_ada, m_w_ada, v_w_ada)])
    ada = [a.reshape(DEPTH, D_MODEL, ada_cols) for a in ada]

    win = _sum_adamw_layers(r_in, w_in, m_w_in, v_w_in)
    wout = _sum_adamw_layers(r_out, w_out, m_w_out, v_w_out)

    outs = [loss, dx[None]]
    for k in range(4):
        b, n, f = small_unpack((s_g, s_d, s_m, s_v)[k])
        outs += [n, ada[k], b, win[k], wout[k], f]
    return tuple(outs)
```

```python
import functools

import jax
import jax.numpy as jnp
from jax import lax
from jax.experimental import pallas as pl
from jax.experimental.pallas import tpu as pltpu

F32 = jnp.float32
MXU_DTYPE = jnp.bfloat16

D_MODEL = 1024
DEPTH = 2
N_DEV = 8
CHUNK = 64
D_RET = 512
D_SB = 512
RET_HEADS = 4
RET_HEAD_DIM = 128
SB_HEADS = 8
SB_HEAD_DIM = 64
D_IN = 4096
ROPE_BASE = 10000.0
EPS = 1e-6
SB_SCALE = SB_HEAD_DIM ** -0.5
RET_KSCALE = RET_HEAD_DIM ** -0.5

ADAM_LR = 0.001
ADAM_B1 = 0.9
ADAM_B2 = 0.999
ADAM_EPS = 1e-08
ADAM_WD = 0.01
ADAM_STEP = 10

V7X_VMEM_BYTES = 64 * 2 ** 20
VMEM_LIMIT = V7X_VMEM_BYTES - 8 * 2 ** 20
LANES = 128

_NT = (((1,), (1,)), ((), ()))
_TN = (((0,), (0,)), ((), ()))


def _dot(a, b):
    return jnp.dot(a, b, preferred_element_type=F32)


def _dot_nt(a, b):
    return lax.dot_general(a, b, _NT, preferred_element_type=F32)


def _dot_tn(a, b):
    return lax.dot_general(a, b, _TN, preferred_element_type=F32)


def _mx(x):
    return x.astype(MXU_DTYPE)


def _sigmoid(x):
    return 1.0 / (1.0 + jnp.exp(-x))


def _params(sem=None):
    return pltpu.CompilerParams(dimension_semantics=sem, vmem_limit_bytes=VMEM_LIMIT)


def _row_tile(s):
    return min(512, s)


def _w_in_spec(w_in_g, layer):
    return pl.BlockSpec((N_DEV, None) + w_in_g.shape[2:], lambda i: (0, layer, 0, 0))


def _w_out_spec(w_out_g, layer):
    return pl.BlockSpec((N_DEV, None) + w_out_g.shape[2:], lambda i: (0, layer, 0, 0))


def _ln_proj(x, shift, scale1p, g, w_in_g, layer, gather=()):
    s = x.shape[0]
    ts = _row_tile(s)
    ns = s // ts
    n_g = len(gather)

    def body(x_ref, sh_ref, sc_ref, g_ref, w_ref, *rest):
        ret_ref, qkv_ref, sg_ref = rest[n_g:n_g + 3]
        if n_g:
            start, forward, finish = _gather_plan(rest[:n_g], rest[n_g + 3:2 * n_g + 3], *rest[2 * n_g + 3:])
            i = pl.program_id(0)
            pl.when(i == 0)(start)
            pl.when(i == max(ns - 4, 0))(forward)
        xv = x_ref[...]
        rstd = lax.rsqrt(jnp.mean(xv * xv, axis=-1, keepdims=True) + EPS)
        h = (xv * rstd * g_ref[...]) * sc_ref[...] + sh_ref[...]
        hb = _mx(h)
        for n in range(4):
            ret_ref[:, n * 512:(n + 1) * 512] = _dot(hb, w_ref[n])
        qkv_ref[:, 0:512] = _mx(_dot(hb, w_ref[4]) * SB_SCALE)
        qkv_ref[:, 512:1024] = _mx(_dot(hb, w_ref[5]))
        qkv_ref[:, 1024:1536] = _mx(_dot(hb, w_ref[6]))
        sg_ref[...] = _dot(hb, w_ref[7])
        if n_g:
            pl.when(i == ns - 1)(finish)

    vec = pl.BlockSpec((1, D_MODEL), lambda i: (0, 0))
    return pl.pallas_call(
        body, name="ln_proj_gather" if n_g else "ln_proj", grid=(ns,),
        in_specs=[pl.BlockSpec((ts, D_MODEL), lambda i: (i, 0)), vec, vec, vec,
                  _w_in_spec(w_in_g, layer)] + [HBM_SPEC] * n_g,
        out_specs=[pl.BlockSpec((ts, 2048), lambda i: (i, 0)),
                   pl.BlockSpec((ts, 1536), lambda i: (i, 0)),
                   pl.BlockSpec((ts, 512), lambda i: (i, 0))] + [HBM_SPEC] * n_g,
        out_shape=[jax.ShapeDtypeStruct((s, 2048), F32),
                   jax.ShapeDtypeStruct((s, 1536), MXU_DTYPE),
                   jax.ShapeDtypeStruct((s, 512), F32)] + _gathered_shapes(gather),
        scratch_shapes=_gather_sems(n_g) if n_g else (),
        compiler_params=_params(("arbitrary",)),
    )(x, shift, scale1p, g, w_in_g, *gather)


def _w_out_halves(w_ref):
    half = N_DEV // 2
    return (w_ref[0:half].reshape(D_RET, D_MODEL), w_ref[half:N_DEV].reshape(D_SB, D_MODEL))


def _out_proj(x, gate, y_r, y_s, w_out_g, layer, loss_head=None):
    s = x.shape[0]
    ts = _row_tile(s)

    def layer_out(x_ref, gate_ref, yr_ref, ys_ref, w_ref):
        w_r, w_s = _w_out_halves(w_ref)
        return x_ref[...] + gate_ref[...] * (_dot(yr_ref[...], w_r) + _dot(ys_ref[...], w_s))

    def body(x_ref, gate_ref, yr_ref, ys_ref, w_ref, o_ref):
        o_ref[...] = layer_out(x_ref, gate_ref, yr_ref, ys_ref, w_ref)

    def body_loss(x_ref, gate_ref, yr_ref, ys_ref, w_ref, fg_ref, t_ref, dx_ref, loss_ref, dfg_ref):
        @pl.when(pl.program_id(0) == 0)
        def _():
            loss_ref[...] = jnp.zeros_like(loss_ref)
            dfg_ref[...] = jnp.zeros_like(dfg_ref)

        xv = layer_out(x_ref, gate_ref, yr_ref, ys_ref, w_ref)
        fgv = fg_ref[...]
        rstd = lax.rsqrt(jnp.mean(xv * xv, axis=-1, keepdims=True) + EPS)
        xn = xv * rstd
        err = xn * fgv - t_ref[...]
        tok = jnp.mean(err * err, axis=-1, keepdims=True)
        loss_ref[...] += 0.5 * jnp.sum(tok, axis=0, keepdims=True)
        dy = err * (1.0 / D_MODEL)
        dfg_ref[...] += jnp.sum(dy * xn, axis=0, keepdims=True)
        dxn = dy * fgv
        dx_ref[...] = rstd * (dxn - xn * jnp.mean(dxn * xn, axis=-1, keepdims=True))

    rows = pl.BlockSpec((ts, D_MODEL), lambda i: (i, 0))
    vec = pl.BlockSpec((1, D_MODEL), lambda i: (0, 0))
    in_specs = [rows, vec, pl.BlockSpec((ts, 512), lambda i: (i, 0)), pl.BlockSpec((ts, 512), lambda i: (i, 0)),
                _w_out_spec(w_out_g, layer)]
    if loss_head is None:
        return pl.pallas_call(
            body, name="out_proj", grid=(s // ts,), in_specs=in_specs, out_specs=rows,
            out_shape=jax.ShapeDtypeStruct((s, D_MODEL), F32),
            compiler_params=_params(("arbitrary",)),
        )(x, gate, y_r, y_s, w_out_g)
    return pl.pallas_call(
        body_loss, name="out_proj_loss", grid=(s // ts,), in_specs=in_specs + [vec, rows],
        out_specs=[rows, pl.BlockSpec((1, 1), lambda i: (0, 0)), vec],
        out_shape=[jax.ShapeDtypeStruct((s, D_MODEL), F32),
                   jax.ShapeDtypeStruct((1, 1), F32),
                   jax.ShapeDtypeStruct((1, D_MODEL), F32)],
        compiler_params=_params(("arbitrary",)),
    )(x, gate, y_r, y_s, w_out_g, *loss_head)


def _out_proj_bwd(dx_out, gate, y_r, y_s, w_out_g, layer):
    s = dx_out.shape[0]
    ts = _row_tile(s)
    ns = s // ts

    def body(dx_ref, gate_ref, yr_ref, ys_ref, w_ref, dy_ref, dw_ref, dgate_ref):
        i = pl.program_id(0)

        @pl.when(i == 0)
        def _():
            dw_ref[...] = jnp.zeros_like(dw_ref)

        dxv = dx_ref[...]
        dt = _mx(dxv * gate_ref[...])
        dxb = _mx(dxv)
        w_r, w_s = _w_out_halves(w_ref)
        dy_ref[:, 0:512] = _dot_nt(dt, w_r)
        dy_ref[:, 512:1024] = _dot_nt(dt, w_s)
        dw_ref[0:512, :] += _dot_tn(yr_ref[...], dxb)
        dw_ref[512:1024, :] += _dot_tn(ys_ref[...], dxb)

        @pl.when(i == ns - 1)
        def _():
            m_r, m_s = dw_ref[0:512, :], dw_ref[512:1024, :]
            dgate_ref[...] = (jnp.sum(w_r.astype(F32) * m_r, axis=0, keepdims=True)
                              + jnp.sum(w_s.astype(F32) * m_s, axis=0, keepdims=True))
            dw_ref[...] = dw_ref[...] * gate_ref[...]

    return pl.pallas_call(
        body, name="out_proj_bwd", grid=(ns,),
        in_specs=[pl.BlockSpec((ts, D_MODEL), lambda i: (i, 0)),
                  pl.BlockSpec((1, D_MODEL), lambda i: (0, 0)),
                  pl.BlockSpec((ts, 512), lambda i: (i, 0)),
                  pl.BlockSpec((ts, 512), lambda i: (i, 0)),
                  _w_out_spec(w_out_g, layer)],
        out_specs=[pl.BlockSpec((ts, D_MODEL), lambda i: (i, 0)),
                   pl.BlockSpec((D_MODEL, D_MODEL), lambda i: (0, 0)),
                   pl.BlockSpec((1, D_MODEL), lambda i: (0, 0))],
        out_shape=[jax.ShapeDtypeStruct((s, D_MODEL), F32),
                   jax.ShapeDtypeStruct((D_MODEL, D_MODEL), F32),
                   jax.ShapeDtypeStruct((1, D_MODEL), F32)],
        compiler_params=_params(("arbitrary",)),
    )(dx_out, gate, y_r, y_s, w_out_g)


def _scatter_plan(parts_ref, recv_ref, send_sems, recv_sems, local_sem):
    px, py, pc = _my_place()
    mine = _linear(px, py, pc)

    def copy(r):
        peer = (1 - px if r & 4 else px, 1 - py if r & 2 else py, 1 - pc if r & 1 else pc)
        return pltpu.make_async_remote_copy(
            src_ref=parts_ref.at[_linear(*peer)], dst_ref=recv_ref.at[mine],
            send_sem=send_sems.at[r - 1], recv_sem=recv_sems.at[r - 1],
            device_id=peer, device_id_type=MESH_IDS)

    own = pltpu.make_async_copy(parts_ref.at[mine], recv_ref.at[mine], local_sem.at[0])

    def start():
        own.start()
        for r in range(1, N_DEV):
            copy(r).start()

    def finish():
        for r in range(1, N_DEV):
            copy(r).wait_recv()
            copy(r).wait_send()
        own.wait()

    return start, finish


def _in_proj_bwd_x(x, dx_out, dproj, shift, scale1p, g, w_in_g, layer):
    s = x.shape[0]
    ts = _row_tile(s)
    ns = s // ts
    nb = D_IN // N_DEV

    def body(x_ref, dxo_ref, dr_ref, d4_ref, d5_ref, d6_ref, d7_ref, sh_ref, sc_ref, g_ref, w_ref,
             dx_ref, dsh_ref, dsc_ref, dg_ref, ht_ref):
        i = pl.program_id(0)

        @pl.when(i == 0)
        def _():
            dsh_ref[...] = jnp.zeros_like(dsh_ref)
            dsc_ref[...] = jnp.zeros_like(dsc_ref)
            dg_ref[...] = jnp.zeros_like(dg_ref)

        dh = _dot_nt(dr_ref[:, 0:nb], w_ref[0])
        for n in range(1, 4):
            dh += _dot_nt(dr_ref[:, n * nb:(n + 1) * nb], w_ref[n])
        for n, d_ref in zip(range(4, N_DEV), (d4_ref, d5_ref, d6_ref, d7_ref)):
            dh += _dot_nt(d_ref[...], w_ref[n])
        xv = x_ref[...]
        gv = g_ref[...]
        scv = sc_ref[...]
        rstd = lax.rsqrt(jnp.mean(xv * xv, axis=-1, keepdims=True) + EPS)
        xn = xv * rstd
        xg = xn * gv
        ht_ref[...] = _mx((xg * scv + sh_ref[...]).T)
        dsh_ref[...] += jnp.sum(dh, axis=0, keepdims=True)
        dsc_ref[...] += jnp.sum(dh * xg, axis=0, keepdims=True)
        dhs = dh * scv
        dg_ref[...] += jnp.sum(dhs * xn, axis=0, keepdims=True)
        dxn = dhs * gv
        dx_ref[...] = rstd * (dxn - xn * jnp.mean(dxn * xn, axis=-1, keepdims=True)) + dxo_ref[...]

    vec = pl.BlockSpec((1, D_MODEL), lambda i: (0, 0))
    return pl.pallas_call(
        body, name="in_proj_bwd_x", grid=(ns,),
        in_specs=[pl.BlockSpec((ts, D_MODEL), lambda i: (i, 0)),
                  pl.BlockSpec((ts, D_MODEL), lambda i: (i, 0)),
                  pl.BlockSpec((ts, 4 * nb), lambda i: (i, 0))]
                 + [pl.BlockSpec((ts, nb), lambda i: (i, 0))] * 4
                 + [vec, vec, vec, _w_in_spec(w_in_g, layer)],
        out_specs=[pl.BlockSpec((ts, D_MODEL), lambda i: (i, 0)), vec, vec, vec,
                   pl.BlockSpec((D_MODEL, ts), lambda i: (0, i))],
        out_shape=[jax.ShapeDtypeStruct((s, D_MODEL), F32),
                   jax.ShapeDtypeStruct((1, D_MODEL), F32),
                   jax.ShapeDtypeStruct((1, D_MODEL), F32),
                   jax.ShapeDtypeStruct((1, D_MODEL), F32),
                   jax.ShapeDtypeStruct((D_MODEL, s), MXU_DTYPE)],
        compiler_params=_params(("arbitrary",)),
    )(x, dx_out, *dproj, shift, scale1p, g, w_in_g)


def _in_proj_bwd_w(me, h_t, dproj):
    s = h_t.shape[1]
    ts = min(4 * _row_tile(s), s)
    ns = s // ts
    nb = D_IN // N_DEV
    n_chip = N_DEV // 2

    def flip_bits(j):
        return jnp.where(j == 0, 4, jnp.where(j == 1, 2, jnp.where(j == 2, 6, 0)))

    def slab_of(t, me_ref):
        return jnp.bitwise_xor(me_ref[0], flip_bits(t // 2) + 1 - t % 2)

    def body(me_ref, ht_ref, dr_ref, d4_ref, d5_ref, d6_ref, d7_ref, rin_ref,
             acc, stage, pre_buf, pre_send, pre_recv, sum_send, sum_recv, local_sem):
        t = pl.program_id(0)
        i = pl.program_id(1)
        j = t // 2
        summing = t % 2 == 1
        slab = slab_of(t, me_ref)
        px, py, pc = _my_place()

        def pre_copy(jj):
            return pltpu.make_async_remote_copy(
                src_ref=stage.at[jj % 2], dst_ref=pre_buf.at[jj],
                send_sem=pre_send.at[jj], recv_sem=pre_recv.at[jj],
                device_id=(px, py, 1 - pc), device_id_type=MESH_IDS)

        def sum_copy(jj):
            fx = jnp.logical_or(jj == 0, jj == 2)
            fy = jnp.logical_or(jj == 1, jj == 2)
            return pltpu.make_async_remote_copy(
                src_ref=stage.at[2 + jj % 2], dst_ref=rin_ref.at[jj],
                send_sem=sum_send.at[jj], recv_sem=sum_recv.at[jj],
                device_id=(jnp.where(fx, 1 - px, px), jnp.where(fy, 1 - py, py), pc), device_id_type=MESH_IDS)

        own = pltpu.make_async_copy(stage.at[3], rin_ref.at[n_chip - 1], local_sem.at[0])

        @pl.when(i == 0)
        def _():
            acc[...] = jnp.zeros_like(acc)

        @pl.when(slab < 4)
        def _():
            acc[...] += _dot(ht_ref[...], dr_ref[...])

        for n, d_ref in zip(range(4, N_DEV), (d4_ref, d5_ref, d6_ref, d7_ref)):
            @pl.when(slab == n)
            def _():
                acc[...] += _dot(ht_ref[...], d_ref[...])

        @pl.when(jnp.logical_and(i == ns - 1, jnp.logical_not(summing)))
        def _():
            @pl.when(j >= 2)
            def _():
                pre_copy(j - 2).wait_send()

            stage[j % 2] = acc[...].astype(stage.dtype)
            pre_copy(j).start()

        @pl.when(jnp.logical_and(i == ns - 1, summing))
        def _():
            pre_copy(j).wait_recv()

            @pl.when(j >= 2)
            def _():
                sum_copy(j - 2).wait_send()

            stage[2 + j % 2] = (acc[...] + pre_buf[j].astype(F32)).astype(stage.dtype)

            @pl.when(j < n_chip - 1)
            def _():
                sum_copy(j).start()

            @pl.when(j == n_chip - 1)
            def _():
                own.start()
                pre_copy(n_chip - 2).wait_send()
                pre_copy(n_chip - 1).wait_send()
                sum_copy(n_chip - 2).wait_send()
                for jj in range(n_chip - 1):
                    sum_copy(jj).wait_recv()
                own.wait()

    def part_rows(n, t, i, me_ref):
        return jnp.where(slab_of(t, me_ref) == n, i, ns - 1), 0

    return pl.pallas_call(
        body, name="in_proj_bwd_w",
        grid_spec=pltpu.PrefetchScalarGridSpec(
            num_scalar_prefetch=1, grid=(N_DEV, ns),
            in_specs=[pl.BlockSpec((D_MODEL, ts), lambda t, i, me_ref: (0, i)),
                      pl.BlockSpec((ts, nb), lambda t, i, me_ref: (
                          jnp.where(slab_of(t, me_ref) < 4, i, ns - 1), jnp.minimum(slab_of(t, me_ref), 3)))]
                     + [pl.BlockSpec((ts, nb), functools.partial(part_rows, n)) for n in range(4, N_DEV)],
            out_specs=HBM_SPEC,
            scratch_shapes=[pltpu.VMEM((D_MODEL, nb), F32),
                            pltpu.VMEM((4, D_MODEL, nb), MXU_DTYPE),
                            pltpu.VMEM((n_chip, D_MODEL, nb), MXU_DTYPE),
                            pltpu.SemaphoreType.DMA((n_chip,)), pltpu.SemaphoreType.DMA((n_chip,)),
                            pltpu.SemaphoreType.DMA((n_chip - 1,)), pltpu.SemaphoreType.DMA((n_chip - 1,)),
                            pltpu.SemaphoreType.DMA((1,))]),
        out_shape=jax.ShapeDtypeStruct((n_chip, D_MODEL, nb), MXU_DTYPE),
        compiler_params=_params(("arbitrary", "arbitrary")),
    )(jnp.reshape(me, (1,)).astype(jnp.int32), h_t, *dproj)


RET_TILE = 256


def _ret_tables(s):
    t = min(RET_TILE, s)
    half = RET_HEAD_DIM // 2
    pos = jnp.arange(s, dtype=F32)
    inv = ROPE_BASE ** (-jnp.arange(half, dtype=F32) / half)
    ang = pos[:, None] * inv[None, :]
    cos, sin = jnp.cos(ang), jnp.sin(ang)
    cos2 = jnp.concatenate([cos, cos], axis=1)
    sin2 = jnp.concatenate([-sin, sin], axis=1)
    lg = jnp.log1p(-(2.0 ** (-5.0 - jnp.arange(RET_HEADS, dtype=F32))))[:, None, None]
    n = jnp.arange(t)
    dist = (n[:, None] - n[None, :]).astype(F32)[None]
    cn = (n // CHUNK)[:, None]
    cm = (n // CHUNK)[None, :]
    mask = jnp.where((cn == cm)[None], jnp.exp(jnp.abs(dist) * lg),
                     jnp.where((cm < cn)[None], jnp.exp(dist * lg), 0.0))
    nf = n.astype(F32)[None, :, None]
    dq = jnp.broadcast_to(jnp.exp((nf + 1.0) * lg), (RET_HEADS, t, LANES))
    dk = jnp.broadcast_to(jnp.exp((t - 1.0 - nf) * lg), (RET_HEADS, t, LANES))
    gt = jnp.broadcast_to(jnp.exp(float(t) * lg), (RET_HEADS, 1, LANES))
    return cos2, sin2, mask, dq, dk, gt


def _roll_half(x):
    return pltpu.roll(x, RET_HEAD_DIM // 2, 1)


def _ret_heads_fwd(ret_ref, cos, sin, m_ref, dq_ref, dk_ref, s0):
    hd = RET_HEAD_DIM
    heads = range(RET_HEADS)
    qb, kb, vb, kdb = [], [], [], []
    for h in heads:
        q = ret_ref[:, h * hd:(h + 1) * hd]
        k = ret_ref[:, 512 + h * hd:512 + (h + 1) * hd]
        kr = (k * cos + _roll_half(k) * sin) * RET_KSCALE
        qb.append(_mx(q * cos + _roll_half(q) * sin))
        kb.append(_mx(kr))
        kdb.append(_mx(kr * dk_ref[h]))
        vb.append(_mx(ret_ref[:, 1024 + h * hd:1024 + (h + 1) * hd]))
    p = [_dot_nt(qb[h], kb[h]) for h in heads]
    cross = [_dot(qb[h], _mx(s0[h])) for h in heads]
    pb = [_mx(p[h] * m_ref[h]) for h in heads]
    o = [_dot(pb[h], vb[h]) + cross[h] * dq_ref[h] for h in heads]
    gn, rstd = [], []
    for h in heads:
        oc = o[h] - jnp.mean(o[h], axis=-1, keepdims=True)
        rstd.append(lax.rsqrt(jnp.mean(oc * oc, axis=-1, keepdims=True) + EPS))
        gn.append(oc * rstd[h])
    return qb, kb, vb, pb, kdb, gn, rstd


def _retention_fwd(ret, tables):
    cos2, sin2, mask, dq, dk, gt = tables
    s = ret.shape[0]
    t = mask.shape[1]
    nt = s // t
    hd = RET_HEAD_DIM

    def body(ret_ref, cos_ref, sin_ref, m_ref, dq_ref, dk_ref, gt_ref, y_ref, st_ref, s_scr):
        i = pl.program_id(0)

        @pl.when(i == 0)
        def _():
            s_scr[...] = jnp.zeros_like(s_scr)

        s0 = [s_scr[h] for h in range(RET_HEADS)]
        _, _, vb, _, kdb, gn, _ = _ret_heads_fwd(ret_ref, cos_ref[...], sin_ref[...], m_ref, dq_ref, dk_ref, s0)
        kv = [_dot_tn(kdb[h], vb[h]) for h in range(RET_HEADS)]
        for h in range(RET_HEADS):
            g = ret_ref[:, 1536 + h * hd:1536 + (h + 1) * hd]
            st_ref[h] = s0[h]
            y_ref[:, h * hd:(h + 1) * hd] = (gn[h] * (g * _sigmoid(g))).astype(y_ref.dtype)
            s_scr[h] = s0[h] * gt_ref[h] + kv[h]

    full3 = lambda a: pl.BlockSpec(a.shape, lambda i: (0, 0, 0))
    return pl.pallas_call(
        body, name="retention_fwd", grid=(nt,),
        in_specs=[pl.BlockSpec((t, 2048), lambda i: (i, 0)),
                  pl.BlockSpec((t, LANES), lambda i: (i, 0)),
                  pl.BlockSpec((t, LANES), lambda i: (i, 0)),
                  full3(mask), full3(dq), full3(dk), full3(gt)],
        out_specs=[pl.BlockSpec((t, 512), lambda i: (i, 0)),
                   pl.BlockSpec((None, RET_HEADS, hd, hd), lambda i: (i, 0, 0, 0))],
        out_shape=[jax.ShapeDtypeStruct((s, 512), MXU_DTYPE),
                   jax.ShapeDtypeStruct((nt, RET_HEADS, hd, hd), F32)],
        scratch_shapes=[pltpu.VMEM((RET_HEADS, hd, hd), F32)],
        compiler_params=_params(("arbitrary",)),
    )(ret, cos2, sin2, mask, dq, dk, gt)


def _retention_bwd(ret, states, dy, tables, dwo_parts):
    cos2, sin2, mask, dq, dk, gt = tables
    s = ret.shape[0]
    t = mask.shape[1]
    nt = s // t
    hd = RET_HEAD_DIM

    def body(ret_ref, st_ref, dy_ref, cos_ref, sin_ref, m_ref, dq_ref, dk_ref, gt_ref, dwo_ref,
             d_ref, rout_ref, ds_scr, send_sems, recv_sems, local_sem):
        i = pl.program_id(0)
        start, finish = _scatter_plan(dwo_ref, rout_ref, send_sems, recv_sems, local_sem)

        @pl.when(i == 0)
        def _():
            start()
            ds_scr[...] = jnp.zeros_like(ds_scr)

        cos = cos_ref[...]
        sin = sin_ref[...]
        heads = range(RET_HEADS)
        s0 = [st_ref[h] for h in heads]
        ds = [ds_scr[h] for h in heads]
        dsb = [_mx(ds[h]) for h in heads]
        qb, kb, vb, pb, kdb, gn, rstd = _ret_heads_fwd(ret_ref, cos, sin, m_ref, dq_ref, dk_ref, s0)
        dob, dodb = [], []
        for h in heads:
            g = ret_ref[:, 1536 + h * hd:1536 + (h + 1) * hd]
            dyv = dy_ref[:, h * hd:(h + 1) * hd]
            sg = _sigmoid(g)
            d_ref[:, 1536 + h * hd:1536 + (h + 1) * hd] = (
                dyv * gn[h] * (sg * (1.0 + g * (1.0 - sg)))).astype(d_ref.dtype)
            dgn = dyv * (g * sg)
            do = rstd[h] * (dgn - jnp.mean(dgn, axis=-1, keepdims=True)
                            - gn[h] * jnp.mean(dgn * gn[h], axis=-1, keepdims=True))
            dob.append(_mx(do))
            dodb.append(_mx(do * dq_ref[h]))
        dp = [_dot_nt(dob[h], vb[h]) for h in heads]
        dv = [_dot_tn(pb[h], dob[h]) + _dot(kdb[h], dsb[h]) for h in heads]
        dq_cross = [_dot_nt(dodb[h], _mx(s0[h])) for h in heads]
        dk_cross = [_dot_nt(vb[h], dsb[h]) for h in heads]
        ds_new = [_dot_tn(qb[h], dodb[h]) for h in heads]
        dpb = [_mx(dp[h] * m_ref[h]) for h in heads]
        dqr = [_dot(dpb[h], kb[h]) + dq_cross[h] for h in heads]
        dkr = [(_dot_tn(dpb[h], qb[h]) + dk_cross[h] * dk_ref[h]) * RET_KSCALE for h in heads]
        for h in heads:
            d_ref[:, 1024 + h * hd:1024 + (h + 1) * hd] = dv[h].astype(d_ref.dtype)
            d_ref[:, h * hd:(h + 1) * hd] = (dqr[h] * cos + _roll_half(dqr[h] * sin)).astype(d_ref.dtype)
            d_ref[:, 512 + h * hd:512 + (h + 1) * hd] = (
                dkr[h] * cos + _roll_half(dkr[h] * sin)).astype(d_ref.dtype)
            ds_scr[h] = ds[h] * gt_ref[h] + ds_new[h]
        pl.when(i == nt - 1)(finish)

    full3 = lambda a: pl.BlockSpec(a.shape, lambda i: (0, 0, 0))
    rev = lambda i: (nt - 1 - i, 0)
    return pl.pallas_call(
        body, name="retention_bwd", grid=(nt,),
        in_specs=[pl.BlockSpec((t, 2048), rev),
                  pl.BlockSpec((None, RET_HEADS, hd, hd), lambda i: (nt - 1 - i, 0, 0, 0)),
                  pl.BlockSpec((t, 512), rev),
                  pl.BlockSpec((t, LANES), rev),
                  pl.BlockSpec((t, LANES), rev),
                  full3(mask), full3(dq), full3(dk), full3(gt), HBM_SPEC],
        out_specs=[pl.BlockSpec((t, 2048), rev), HBM_SPEC],
        out_shape=[jax.ShapeDtypeStruct((s, 2048), MXU_DTYPE),
                   jax.ShapeDtypeStruct(dwo_parts.shape, dwo_parts.dtype)],
        scratch_shapes=[pltpu.VMEM((RET_HEADS, hd, hd), F32),
                        pltpu.SemaphoreType.DMA((N_PEERS,)), pltpu.SemaphoreType.DMA((N_PEERS,)),
                        pltpu.SemaphoreType.DMA((1,))],
        compiler_params=_params(("arbitrary",)),
    )(ret, states, dy, cos2, sin2, mask, dq, dk, gt, dwo_parts)


SB_BLOCK = 256
SB_CHUNK = 32


SB_SKIP = 104.0
NO_KEYS = -1e30


def _split_dots(xs, u):
    parts = []
    for x in xs:
        hi = lax.bitcast_convert_type(lax.bitcast_convert_type(x, jnp.uint32) & jnp.uint32(0xFFFF0000), F32)
        parts += [_mx(hi), _mx(x - hi)]
    out = _dot(jnp.concatenate(parts, axis=0), u)
    n = xs[0].shape[0]
    return [out[2 * k * n:(2 * k + 1) * n] + out[(2 * k + 1) * n:(2 * k + 2) * n] for k in range(len(xs))]


def _split_dot(x, u):
    return _split_dots([x], u)[0]


def _sb_pair_weights(lb, lk, allowed, u_gt):
    blk = lb.shape[0]
    lk_p, lk_d = lk[:, :blk], lk[:, blk:]
    r_d = _rowsum(lk_d)
    cs_p, cs_d = _split_dots([lk_p, lk_d], u_gt)
    a = jnp.exp(lb + jnp.concatenate([cs_p + r_d, cs_d], axis=1))
    return jnp.where(allowed, a, 0.0), r_d, r_d + _rowsum(lk_p)


def _sb_logits(q, k, causal):
    z = _dot_nt(q, k)
    l1p = jnp.log(1.0 + jnp.exp(-jnp.abs(z)))
    lk = -(jnp.maximum(z, 0.0) + l1p)
    if causal is not None:
        lk = jnp.where(causal, lk, 0.0)
    return jnp.minimum(z, 0.0) - l1p, lk


def _sb_weights(lb, lk, r, u_gt, causal):
    a = jnp.exp(lb + _split_dot(lk, u_gt) + r)
    return a if causal is None else jnp.where(causal, a, 0.0)


def _rowsum(x):
    return jnp.sum(x, axis=1, keepdims=True)


def _sb_pair_tile(i, blk):
    row = lax.broadcasted_iota(jnp.int32, (blk, 2 * blk), 0)
    col = lax.broadcasted_iota(jnp.int32, (blk, 2 * blk), 1)
    first_col = jnp.where(i >= 1, 0, blk)
    allowed = jnp.logical_and(row > col - blk, col >= first_col)
    rows_p = pl.ds(pl.multiple_of(jnp.maximum(i - 1, 0) * blk, blk), blk)
    rows_d = pl.ds(pl.multiple_of(i * blk, blk), blk)
    return allowed, rows_p, rows_d


def _sb_fwd(qkv, sg):
    s = qkv.shape[0]
    blk = min(SB_BLOCK, s)
    nq = s // blk
    hd = SB_HEAD_DIM

    ch = min(SB_CHUNK, blk)

    def body(q_ref, k_ref, v_ref, g_ref, y_ref, o_ref, z_scr, lhs_scr, cs_scr, a_scr):
        i = pl.program_id(1)
        row = lax.broadcasted_iota(jnp.int32, (blk, blk), 0)
        col = lax.broadcasted_iota(jnp.int32, (blk, blk), 1)
        u_gt = (row > col).astype(MXU_DTYPE)
        heads = [slice(hh * hd, (hh + 1) * hd) for hh in range(2)]
        qs = [q_ref[:, ls] for ls in heads]
        _, rows_p, rows_d = _sb_pair_tile(i, blk)
        has_prev = i >= 1
        crow = lax.broadcasted_iota(jnp.int32, (ch, blk), 0)
        ccol = lax.broadcasted_iota(jnp.int32, (ch, blk), 1)

        def logits(hh):
            kc = jnp.concatenate([k_ref[rows_p, heads[hh]], k_ref[rows_d, heads[hh]]], axis=0)
            z_scr[hh] = _dot_nt(qs[hh], kc)

        def keep_parts(hh):
            r_d, r_all = [], []
            for c in range(blk // ch):
                rows = pl.ds(c * ch, ch)
                causal = crow + c * ch > ccol
                z = z_scr[hh, rows, :]
                l1p = jnp.log(1.0 + jnp.exp(-jnp.abs(z)))
                lb = jnp.minimum(z, 0.0) - l1p
                z_scr[hh, rows, :] = lb
                lk = lb - z
                lk_p = lk[:, :blk]
                lk_d = jnp.where(causal, lk[:, blk:], 0.0)
                lhs_scr[hh, pl.ds(c * ch, ch), :] = _mx(lk_p)
                lhs_scr[hh, pl.ds(blk + c * ch, ch), :] = _mx(lk_d)
                r_d.append(_rowsum(lk_d))
                r_all.append(r_d[c] + _rowsum(lk_p))
            return r_d, jnp.concatenate(r_all, axis=0)

        def suffix_sums(hh):
            cs_scr[hh] = _dot(lhs_scr[hh], u_gt)

        def weights(hh, r_d):
            for c in range(blk // ch):
                rows = pl.ds(c * ch, ch)
                causal = crow + c * ch > ccol
                cs_p = cs_scr[hh, pl.ds(c * ch, ch), :] + jnp.where(has_prev, r_d[c], NO_KEYS)
                cs_d = cs_scr[hh, pl.ds(blk + c * ch, ch), :]
                lb = z_scr[hh, rows, :]
                a_p = jnp.exp(lb[:, :blk] + cs_p)
                a_d = jnp.where(causal, jnp.exp(lb[:, blk:] + cs_d), 0.0)
                a_scr[hh, rows, :] = _mx(jnp.concatenate([a_p, a_d], axis=1))

        def values(hh):
            vc = jnp.concatenate([v_ref[rows_p, heads[hh]], v_ref[rows_d, heads[hh]]], axis=0)
            return _dot(a_scr[hh], vc)

        def block(hh, j, r):
            start = pl.multiple_of(j * blk, blk)
            lb, lk = _sb_logits(qs[hh], k_ref[pl.ds(start, blk), heads[hh]], None)
            a = _sb_weights(lb, lk, r, u_gt, None)
            return _dot(_mx(a), v_ref[pl.ds(start, blk), heads[hh]]), r + _rowsum(lk)

        def more(n, r0, r1):
            return jnp.logical_and(n < i, jnp.max(jnp.maximum(r0, r1)) > -SB_SKIP)

        logits(0)
        logits(1)
        gv = g_ref[...]
        gates = gv * _sigmoid(gv)
        rd0, r0 = keep_parts(0)
        suffix_sums(0)
        rd1, r1 = keep_parts(1)
        suffix_sums(1)
        go = more(jnp.int32(1), r0, r1)
        weights(0, rd0)
        acc0 = values(0)
        weights(1, rd1)
        acc1 = values(1)

        def step(c):
            _, n, acc0, r0, acc1, r1 = c
            pv0, r0 = block(0, i - 1 - n, r0)
            pv1, r1 = block(1, i - 1 - n, r1)
            return more(n + 1, r0, r1), n + 1, acc0 + pv0, r0, acc1 + pv1, r1

        _, _, acc0, _, acc1, _ = lax.while_loop(lambda c: c[0], step, (go, jnp.int32(1), acc0, r0, acc1, r1))
        o = jnp.concatenate([acc0, acc1], axis=1)
        o_ref[...] = o
        y_ref[...] = (o * gates).astype(y_ref.dtype)

    qblk = pl.BlockSpec((blk, LANES), lambda p, i: (i, p))
    return pl.pallas_call(
        body, name="stickbreak_fwd", grid=(SB_HEADS // 2, nq),
        in_specs=[qblk,
                  pl.BlockSpec((s, LANES), lambda p, i: (0, 4 + p)),
                  pl.BlockSpec((s, LANES), lambda p, i: (0, 8 + p)),
                  qblk],
        out_specs=[qblk, qblk],
        out_shape=[jax.ShapeDtypeStruct((s, 512), MXU_DTYPE),
                   jax.ShapeDtypeStruct((s, 512), F32)],
        scratch_shapes=[pltpu.VMEM((2, blk, 2 * blk), F32),
                        pltpu.VMEM((2, 2 * blk, blk), MXU_DTYPE),
                        pltpu.VMEM((2, 2 * blk, blk), F32),
                        pltpu.VMEM((2, blk, 2 * blk), MXU_DTYPE)],
        compiler_params=_params(("arbitrary", "arbitrary")),
    )(qkv, qkv, qkv, sg)


def _sb_bwd(qkv, sg, o, dy):
    s = qkv.shape[0]
    blk = min(SB_BLOCK, s)
    nq = s // blk
    hd = SB_HEAD_DIM
    assert nq <= LANES
    ch = min(SB_CHUNK, blk)

    def body(q_ref, k_ref, v_ref, g_ref, o_ref, dy_ref, dq_ref, dk_ref, dv_ref, dg_ref, dk_scr, dv_scr,
             z_scr, g_scr, lhs_scr, cs_scr, a_scr, dz_scr):
        i = pl.program_id(1)

        @pl.when(i == 0)
        def _():
            dk_scr[...] = jnp.zeros_like(dk_scr)
            dv_scr[...] = jnp.zeros_like(dv_scr)

        row = lax.broadcasted_iota(jnp.int32, (blk, blk), 0)
        col = lax.broadcasted_iota(jnp.int32, (blk, blk), 1)
        lane = lax.broadcasted_iota(jnp.int32, (blk, LANES), 1)
        u_gt = (row > col).astype(MXU_DTYPE)
        u_lt = (row < col).astype(MXU_DTYPE)
        heads = [slice(hh * hd, (hh + 1) * hd) for hh in range(2)]
        qs = [q_ref[:, ls] for ls in heads]
        _, rows_p, rows_d = _sb_pair_tile(i, blk)
        has_prev = i >= 1
        crow = lax.broadcasted_iota(jnp.int32, (ch, blk), 0)
        ccol = lax.broadcasted_iota(jnp.int32, (ch, blk), 1)
        nch = blk // ch
        kcs = [jnp.concatenate([k_ref[rows_p, ls], k_ref[rows_d, ls]], axis=0) for ls in heads]
        dobs = []

        def gate_grads():
            g = g_ref[...]
            dyv = dy_ref[...]
            sgm = _sigmoid(g)
            dg_ref[...] = (dyv * o_ref[...] * (sgm * (1.0 + g * (1.0 - sgm)))).astype(dg_ref.dtype)
            dob = _mx(dyv * (g * sgm))
            dobs.extend(dob[:, ls] for ls in heads)

        def split_rows(hh, c, part, x):
            lhs_scr[hh, pl.ds(part * blk + c * ch, ch), :] = _mx(x)

        def summed_rows(hh, c, part):
            return cs_scr[hh, pl.ds(part * blk + c * ch, ch), :]

        def logits(hh):
            z_scr[hh] = _dot_nt(qs[hh], kcs[hh])

        def weight_grads(hh):
            vc = jnp.concatenate([v_ref[rows_p, heads[hh]], v_ref[rows_d, heads[hh]]], axis=0)
            g_scr[hh] = _dot_nt(dobs[hh], vc)

        def keep_parts(hh):
            r_d, r_all = [], []
            for c in range(nch):
                rows = pl.ds(c * ch, ch)
                z = z_scr[hh, rows, :]
                l1p = jnp.log(1.0 + jnp.exp(-jnp.abs(z)))
                lb = jnp.minimum(z, 0.0) - l1p
                z_scr[hh, rows, :] = lb
                lk = lb - z
                lk_p = lk[:, :blk]
                lk_d = jnp.where(crow + c * ch > ccol, lk[:, blk:], 0.0)
                split_rows(hh, c, 0, lk_p)
                split_rows(hh, c, 1, lk_d)
                r_d.append(_rowsum(lk_d))
                r_all.append(r_d[c] + _rowsum(lk_p))
            return r_d, jnp.concatenate(r_all, axis=0)

        def weights(hh, r_d):
            g_p = []
            for c in range(nch):
                rows = pl.ds(c * ch, ch)
                lb = z_scr[hh, rows, :]
                a_p = jnp.exp(lb[:, :blk] + (summed_rows(hh, c, 0) + jnp.where(has_prev, r_d[c], NO_KEYS)))
                a_d = jnp.where(crow + c * ch > ccol, jnp.exp(lb[:, blk:] + summed_rows(hh, c, 1)), 0.0)
                a = jnp.concatenate([a_p, a_d], axis=1)
                a_scr[hh, rows, :] = _mx(a)
                gm = g_scr[hh, rows, :] * a
                g_scr[hh, rows, :] = gm
                split_rows(hh, c, 0, gm[:, :blk])
                split_rows(hh, c, 1, gm[:, blk:])
                g_p.append(_rowsum(gm[:, :blk]))
            return g_p

        def logit_grads(hh, pg, g_p):
            for c in range(nch):
                rows = pl.ds(c * ch, ch)
                pre = jnp.concatenate([summed_rows(hh, c, 0) + pg[c * ch:(c + 1) * ch],
                                       summed_rows(hh, c, 1) + (pg[c * ch:(c + 1) * ch] + g_p[c])], axis=1)
                gm = g_scr[hh, rows, :]
                dz = gm - (gm + pre) * jnp.exp(z_scr[hh, rows, :])
                dz_p = dz[:, :blk]
                dz_d = jnp.where(crow + c * ch > ccol, dz[:, blk:], 0.0)
                dz_scr[hh, rows, :] = _mx(jnp.concatenate([dz_p, dz_d], axis=1))

        def products(accs):
            both = lambda f: jnp.concatenate([f(0), f(1)], axis=1)
            dk_scr[rows_p, :] += both(lambda hh: _dot_tn(dz_scr[hh, :, 0:blk], qs[hh]))
            dk_scr[rows_d, :] += both(lambda hh: _dot_tn(dz_scr[hh, :, blk:2 * blk], qs[hh]))
            dv_scr[rows_p, :] += both(lambda hh: _dot_tn(a_scr[hh, :, 0:blk], dobs[hh]))
            dv_scr[rows_d, :] += both(lambda hh: _dot_tn(a_scr[hh, :, blk:2 * blk], dobs[hh]))
            dq = both(lambda hh: (accs[hh] + _dot(dz_scr[hh], kcs[hh])) * SB_SCALE)
            dq_ref[...] = dq.astype(dq_ref.dtype)

        def suffix_sums(hh):
            cs_scr[hh] = _dot(lhs_scr[hh], u_gt)

        def prefix_sums(hh):
            cs_scr[hh] = _dot(lhs_scr[hh], u_lt)

        def more(n, r0, r1):
            return jnp.logical_and(n < i, jnp.max(jnp.maximum(r0, r1)) > -SB_SKIP)

        gate_grads()
        logits(0)
        weight_grads(0)
        logits(1)
        weight_grads(1)
        rd0, ra0 = keep_parts(0)
        suffix_sums(0)
        rd1, ra1 = keep_parts(1)
        suffix_sums(1)
        go = more(jnp.int32(1), ra0, ra1)
        gp0 = weights(0, rd0)
        prefix_sums(0)
        gp1 = weights(1, rd1)
        prefix_sums(1)

        def scan_block(hh, j, r, rmat):
            start = pl.multiple_of(j * blk, blk)
            _, lk = _sb_logits(qs[hh], k_ref[pl.ds(start, blk), heads[hh]], None)
            return r + _rowsum(lk), jnp.where(lane == j, r, rmat)

        def scan_step(c):
            _, n, r0, rmat0, r1, rmat1 = c
            r0, rmat0 = scan_block(0, i - 1 - n, r0, rmat0)
            r1, rmat1 = scan_block(1, i - 1 - n, r1, rmat1)
            return more(n + 1, r0, r1), n + 1, r0, rmat0, r1, rmat1

        zmat = jnp.zeros((blk, LANES), F32)
        _, n, _, rmat0, _, rmat1 = lax.while_loop(lambda c: c[0], scan_step,
                                                  (go, jnp.int32(1), ra0, zmat, ra1, zmat))
        rmats = (rmat0, rmat1)

        def block(hh, j, pg):
            ls = heads[hh]
            start = pl.multiple_of(j * blk, blk)
            k = k_ref[pl.ds(start, blk), ls]
            lb, lk = _sb_logits(qs[hh], k, None)
            r = _rowsum(jnp.where(lane == j, rmats[hh], 0.0))
            a = _sb_weights(lb, lk, r, u_gt, None)
            gm = _dot_nt(dobs[hh], v_ref[pl.ds(start, blk), ls]) * a
            dzb = _mx(gm - (gm + (pg + _split_dot(gm, u_lt))) * jnp.exp(lb))
            dk_scr[pl.ds(start, blk), ls] += _dot_tn(dzb, qs[hh])
            dv_scr[pl.ds(start, blk), ls] += _dot_tn(_mx(a), dobs[hh])
            return _dot(dzb, k), pg + _rowsum(gm)

        def step(t, c):
            acc0, pg0, acc1, pg1 = c
            dq0, pg0 = block(0, i - n + t, pg0)
            dq1, pg1 = block(1, i - n + t, pg1)
            return acc0 + dq0, pg0, acc1 + dq1, pg1

        zero = jnp.zeros((blk, 1), F32)
        zacc = jnp.zeros((blk, hd), F32)
        acc0, pg0, acc1, pg1 = lax.fori_loop(0, n - 1, step, (zacc, zero, zacc, zero))
        logit_grads(0, pg0, gp0)
        logit_grads(1, pg1, gp1)
        products((acc0, acc1))

        @pl.when(i == nq - 1)
        def _():
            dk_ref[...] = dk_scr[...].astype(dk_ref.dtype)
            dv_ref[...] = dv_scr[...].astype(dv_ref.dtype)

    qblk = lambda c0: pl.BlockSpec((blk, LANES), lambda p, i: (i, c0 + p))
    full = lambda c0: pl.BlockSpec((s, LANES), lambda p, i: (0, c0 + p))
    half = jax.ShapeDtypeStruct((s, 512), MXU_DTYPE)
    return pl.pallas_call(
        body, name="stickbreak_bwd", grid=(SB_HEADS // 2, nq),
        in_specs=[qblk(0), full(4), full(8), qblk(0), qblk(0), qblk(4)],
        out_specs=[qblk(0), full(0), full(0), qblk(0)],
        out_shape=[half, half, half, half],
        scratch_shapes=[pltpu.VMEM((s, LANES), F32), pltpu.VMEM((s, LANES), F32),
                        pltpu.VMEM((2, blk, 2 * blk), F32),
                        pltpu.VMEM((2, blk, 2 * blk), F32),
                        pltpu.VMEM((2, 2 * blk, blk), MXU_DTYPE),
                        pltpu.VMEM((2, 2 * blk, blk), F32),
                        pltpu.VMEM((2, blk, 2 * blk), MXU_DTYPE),
                        pltpu.VMEM((2, blk, 2 * blk), MXU_DTYPE)],
        compiler_params=_params(("arbitrary", "arbitrary")),
    )(qkv, qkv, qkv, sg, o, dy)


def _layer_fwd(layer, x, mod, norm_g, w_in_g, w_out_g, tables, gather=(), loss_head=None):
    shift, scale1p, gate = mod[0:1], 1.0 + mod[1:2], mod[2:3]
    ret, qkv, sg, *gathered = _ln_proj(x, shift, scale1p, norm_g, w_in_g, layer, gather)
    y_r, states = _retention_fwd(ret, tables)
    y_s, o_s = _sb_fwd(qkv, sg)
    x_next = _out_proj(x, gate, y_r, y_s, w_out_g, layer, loss_head)
    saved = (x, shift, scale1p, gate, ret, qkv, sg, y_r, states, y_s, o_s)
    return x_next, saved, gathered


def _layer_bwd(layer, me, dx_out, saved, norm_g, w_in_g, w_out_g, tables):
    x, shift, scale1p, gate, ret, qkv, sg, y_r, states, y_s, o_s = saved
    dy, dw_out, dgate = _out_proj_bwd(dx_out, gate, y_r, y_s, w_out_g, layer)
    dwo_parts = _mx(dw_out.reshape(N_DEV, D_MODEL // N_DEV, D_MODEL))
    d_ret, r_out = _retention_bwd(ret, states, dy, tables, dwo_parts)
    d_q, d_k, d_v, d_g = _sb_bwd(qkv, sg, o_s, dy)
    dproj = (d_ret, d_q, d_k, d_v, d_g)
    dx, dshift, dscale, dnorm_g, h_t = _in_proj_bwd_x(x, dx_out, dproj, shift, scale1p, norm_g, w_in_g, layer)
    r_in = _in_proj_bwd_w(me, h_t, dproj)
    dmod = jnp.concatenate([dshift, dscale, dgate], axis=1)
    return dx, r_in, r_out, dmod, dnorm_g


MESH_IDS = pl.DeviceIdType.MESH
N_PEERS = N_DEV - 1
HBM_SPEC = pl.BlockSpec(memory_space=pl.ANY)


def _my_place():
    return lax.axis_index("x"), lax.axis_index("y"), lax.axis_index("c")


def _linear(px, py, pc):
    return 4 * px + 2 * py + pc


def _all_gather(blocks):
    n_arr = len(blocks)

    def body(*refs):
        start, forward, finish = _gather_plan(refs[:n_arr], refs[n_arr:2 * n_arr], *refs[2 * n_arr:])
        start()
        forward()
        finish()

    return pl.pallas_call(
        body, name="all_gather",
        out_shape=_gathered_shapes(blocks),
        in_specs=[HBM_SPEC] * n_arr, out_specs=[HBM_SPEC] * n_arr,
        scratch_shapes=_gather_sems(n_arr),
    )(*blocks)


def _gathered_shapes(blocks):
    return [jax.ShapeDtypeStruct((N_DEV * b.shape[0], b.shape[1]), b.dtype) for b in blocks]


def _gather_sems(n_arr):
    return [pltpu.SemaphoreType.DMA((n_arr * N_PEERS,)), pltpu.SemaphoreType.DMA((n_arr * N_PEERS,)),
            pltpu.SemaphoreType.DMA((n_arr,))]


def _gather_plan(x_refs, out_refs, send_sems, recv_sems, local_sems):
    n_arr = len(x_refs)
    x, y, c = _my_place()
    me, sibling = (x, y, c), (x, y, 1 - c)
    chips = [(1 - x, y), (x, 1 - y), (1 - x, 1 - y)]

    def rows(a, place):
        m = x_refs[a].shape[0]
        return out_refs[a].at[pl.ds(_linear(*place) * m, m), :]

    def copy(a, k, block, to, src=None):
        return pltpu.make_async_remote_copy(
            src_ref=rows(a, block) if src is None else src, dst_ref=rows(a, block),
            send_sem=send_sems.at[a * N_PEERS + k], recv_sem=recv_sems.at[a * N_PEERS + k],
            device_id=to, device_id_type=MESH_IDS)

    mine = [pltpu.make_async_copy(x_refs[a], rows(a, me), local_sems.at[a]) for a in range(n_arr)]
    first = []
    for a in range(n_arr):
        first.append(copy(a, 0, me, sibling, src=x_refs[a]))
        first += [copy(a, 1 + j, me, (*chip, c), src=x_refs[a]) for j, chip in enumerate(chips)]
    passed = [copy(a, 4 + j, (*chip, c), sibling) for j, chip in enumerate(chips) for a in range(n_arr)]

    def start():
        for cp in mine + first:
            cp.start()

    def forward():
        for j, chip in enumerate(chips):
            for a in range(n_arr):
                copy(a, 1 + j, (*chip, c), me).wait_recv()
                passed[j * n_arr + a].start()

    def finish():
        for a in range(n_arr):
            copy(a, 0, sibling, me).wait_recv()
            for j, chip in enumerate(chips):
                copy(a, 4 + j, (*chip, 1 - c), me).wait_recv()
        for cp in first + passed:
            cp.wait_send()
        for cp in mine:
            cp.wait()

    return start, forward, finish


def _ada_fwd(c_all, w_ada, b_cols):
    cols = w_ada.shape[2]

    def body(c_ref, w_ref, b_ref, ca_ref, mod_ref):
        cv = c_ref[...]
        ca = cv * _sigmoid(cv)
        ca_ref[...] = ca
        cb = _mx(jnp.concatenate([ca, ca], axis=0))
        for l in range(DEPTH):
            mod_ref[l * N_DEV:(l + 1) * N_DEV, :] = _dot(cb, _mx(w_ref[l]))[0:N_DEV] + b_ref[l]

    return pl.pallas_call(
        body, name="ada_fwd",
        out_shape=[jax.ShapeDtypeStruct((N_DEV, D_MODEL), F32),
                   jax.ShapeDtypeStruct((DEPTH * N_DEV, cols), F32)],
        compiler_params=_params(),
    )(c_all, w_ada, b_cols)


def _ada_bwd(c_act_t, dmod_cols):
    cols = dmod_cols.shape[2]

    def body(ca_ref, dm_ref, o_ref):
        ca = _mx(ca_ref[...]).astype(F32)
        for l in range(DEPTH):
            o_ref[l] = jnp.dot(ca, _mx(dm_ref[l]).astype(F32),
                               precision=lax.Precision.HIGHEST, preferred_element_type=F32)

    return pl.pallas_call(
        body, name="ada_bwd",
        out_shape=jax.ShapeDtypeStruct((DEPTH, D_MODEL, cols), F32),
        compiler_params=_params(),
    )(c_act_t, dmod_cols)


def _adamw_store(g, w_ref, m_ref, v_ref, g_ref, d_ref, mo_ref, vo_ref):
    m2 = ADAM_B1 * m_ref[...] + (1.0 - ADAM_B1) * g
    v2 = ADAM_B2 * v_ref[...] + (1.0 - ADAM_B2) * (g * g)
    m_hat = m2 / (1.0 - ADAM_B1 ** ADAM_STEP)
    v_hat = v2 / (1.0 - ADAM_B2 ** ADAM_STEP)
    g_ref[...] = g
    d_ref[...] = -ADAM_LR * (m_hat / (jnp.sqrt(v_hat) + ADAM_EPS) + ADAM_WD * w_ref[...])
    mo_ref[...] = m2
    vo_ref[...] = v2


def _slab_sum(p_ref):
    g = p_ref[0].astype(F32)
    for sl in range(1, p_ref.shape[0]):
        g = g + p_ref[sl].astype(F32)
    return g


def _sum_adamw_layers(parts, w, m, v):
    n_slab, rows, cols = parts[0].shape
    tr = min(256, rows)
    nt = rows // tr

    def body(p0_ref, p1_ref, w_ref, m_ref, v_ref, g_ref, d_ref, mo_ref, vo_ref):
        for l, p_ref in enumerate((p0_ref, p1_ref)):
            @pl.when(pl.program_id(0) == l)
            def _():
                _adamw_store(_slab_sum(p_ref), w_ref, m_ref, v_ref, g_ref, d_ref, mo_ref, vo_ref)

    p_specs = [pl.BlockSpec((n_slab, tr, cols), lambda l, i: (0, i * (1 - l) + (nt - 1) * l, 0)),
               pl.BlockSpec((n_slab, tr, cols), lambda l, i: (0, i * l, 0))]
    blk = pl.BlockSpec((None, tr, cols), lambda l, i: (l, i, 0))
    shp = jax.ShapeDtypeStruct((DEPTH, rows, cols), F32)
    return pl.pallas_call(
        body, name="sum_adamw_layers", grid=(DEPTH, nt),
        in_specs=p_specs + [blk, blk, blk],
        out_specs=[blk, blk, blk, blk],
        out_shape=[shp, shp, shp, shp],
        compiler_params=_params(("arbitrary", "arbitrary")),
    )(parts[0], parts[1], w, m, v)


def _sum_adamw(parts, w, m, v):
    n_slab, rows, cols = parts.shape
    tr = min(256, rows)

    def body(p_ref, w_ref, m_ref, v_ref, g_ref, d_ref, mo_ref, vo_ref):
        _adamw_store(_slab_sum(p_ref), w_ref, m_ref, v_ref, g_ref, d_ref, mo_ref, vo_ref)

    blk = pl.BlockSpec((tr, cols), lambda i: (i, 0))
    shp = jax.ShapeDtypeStruct((rows, cols), F32)
    return pl.pallas_call(
        body, name="sum_adamw", grid=(rows // tr,),
        in_specs=[pl.BlockSpec((n_slab, tr, cols), lambda i: (0, i, 0)), blk, blk, blk],
        out_specs=[blk, blk, blk, blk],
        out_shape=[shp, shp, shp, shp],
        compiler_params=_params(("arbitrary",)),
    )(parts, w, m, v)


SMALL_ROWS = 16


def kernel(x, c, norm_g, w_ada, b_ada, w_in, w_out, final_g, loss_target, m_norm_g, m_w_ada, m_b_ada, m_w_in, m_w_out, m_final_g, v_norm_g, v_w_ada, v_b_ada, v_w_in, v_w_out, v_final_g):
    me = _linear(*_my_place())
    in_cols = w_in.shape[2]
    out_rows = w_out.shape[1]
    ada_cols = w_ada.shape[2]

    w_in_m, w_out_m = _mx(w_in), _mx(w_out)
    g_in, g_out, g_c = _all_gather([w_in_m[0], w_out_m[0], jnp.broadcast_to(c, (8, D_MODEL))])
    w_in_g = [g_in.reshape(N_DEV, 1, D_MODEL, in_cols), None]
    w_out_g = [g_out.reshape(N_DEV, 1, out_rows, D_MODEL), None]
    c_all = g_c.reshape(N_DEV, 8, D_MODEL)[:, 0]

    b_cols = lax.dynamic_slice_in_dim(b_ada, me * ada_cols, ada_cols, axis=1)[:, None, :]
    c_act, mod_cols = _ada_fwd(c_all, w_ada, b_cols)
    (g_mod,) = _all_gather([mod_cols])
    g_mod = g_mod.reshape(N_DEV, DEPTH, N_DEV, ada_cols)
    mod = lax.dynamic_index_in_dim(g_mod, me, axis=2, keepdims=False)
    mod = mod.transpose(1, 0, 2).reshape(DEPTH, 3, D_MODEL)

    tables = _ret_tables(x.shape[1])
    h = x[0]
    saved = []
    for l in range(DEPTH):
        nxt = (w_in_m[l + 1], w_out_m[l + 1]) if l + 1 < DEPTH else ()
        head = None if nxt else (final_g[None], loss_target[0])
        h, sv, gathered = _layer_fwd(0, h, mod[l], norm_g[l:l + 1], w_in_g[l], w_out_g[l], tables, nxt, head)
        if nxt:
            w_in_g[l + 1] = gathered[0].reshape(N_DEV, 1, D_MODEL, in_cols)
            w_out_g[l + 1] = gathered[1].reshape(N_DEV, 1, out_rows, D_MODEL)
        saved.append(sv)
    dx, loss_part, dfg = h
    r_in, r_out, small = [None] * DEPTH, [None] * DEPTH, [None] * DEPTH
    for l in reversed(range(DEPTH)):
        dx, r_in[l], r_out[l], dmod, dng = _layer_bwd(0, me, dx, saved[l], norm_g[l:l + 1], w_in_g[l], w_out_g[l], tables)
        small[l] = (dmod.reshape(3, D_MODEL), dng)

    pad = jnp.zeros((SMALL_ROWS - 10, D_MODEL), F32)
    small_block = jnp.concatenate([small[0][0], small[1][0], small[0][1], small[1][1], dfg,
                                   jnp.broadcast_to(loss_part, (1, D_MODEL)), pad], axis=0)
    (g_small,) = _all_gather([small_block])
    g_small = g_small.reshape(N_DEV, SMALL_ROWS, D_MODEL)

    def small_pack(b, n, f, fill):
        return jnp.concatenate([b.reshape(6, D_MODEL), n, f[None],
                                jnp.full((SMALL_ROWS - 9, D_MODEL), fill, F32)], axis=0)

    s_g, s_d, s_m, s_v = _sum_adamw(g_small, small_pack(b_ada, norm_g, final_g, 0.0),
                                    small_pack(m_b_ada, m_norm_g, m_final_g, 0.0),
                                    small_pack(v_b_ada, v_norm_g, v_final_g, 1.0))
    loss = s_g[9, 0]

    def small_unpack(a):
        return a[0:6].reshape(DEPTH, 3 * D_MODEL), a[6:8], a[8]

    dmod_all = g_small[:, 0:6].reshape(N_DEV, DEPTH, 3 * D_MODEL).transpose(1, 0, 2)
    dmod_cols = lax.dynamic_slice_in_dim(dmod_all, me * ada_cols, ada_cols, axis=2)
    g_ada = _ada_bwd(c_act.T, dmod_cols).reshape(1, DEPTH * D_MODEL, ada_cols)
    ada = _sum_adamw(g_ada, *[a.reshape(DEPTH * D_MODEL, ada_cols) for a in (w_ada, m_w_ada, v_w_ada)])
    ada = [a.reshape(DEPTH, D_MODEL, ada_cols) for a in ada]

    win = _sum_adamw_layers(r_in, w_in, m_w_in, v_w_in)
    wout = _sum_adamw_layers(r_out, w_out, m_w_out, v_w_out)

    outs = [loss, dx[None]]
    for k in range(4):
        b, n, f = small_unpack((s_g, s_d, s_m, s_v)[k])
        outs += [n, ada[k], b, win[k], wout[k], f]
    return tuple(outs)
```

```python
import functools

import jax
import jax.numpy as jnp
from jax import lax
from jax.experimental import pallas as pl
from jax.experimental.pallas import tpu as pltpu

F32 = jnp.float32
MXU_DTYPE = jnp.bfloat16

D_MODEL = 1024
DEPTH = 2
N_DEV = 8
CHUNK = 64
D_RET = 512
D_SB = 512
RET_HEADS = 4
RET_HEAD_DIM = 128
SB_HEADS = 8
SB_HEAD_DIM = 64
D_IN = 4096
ROPE_BASE = 10000.0
EPS = 1e-6
SB_SCALE = SB_HEAD_DIM ** -0.5
RET_KSCALE = RET_HEAD_DIM ** -0.5

ADAM_LR = 0.001
ADAM_B1 = 0.9
ADAM_B2 = 0.999
ADAM_EPS = 1e-08
ADAM_WD = 0.01
ADAM_STEP = 10

V7X_VMEM_BYTES = 64 * 2 ** 20
VMEM_LIMIT = V7X_VMEM_BYTES - 8 * 2 ** 20
LANES = 128

_NT = (((1,), (1,)), ((), ()))
_TN = (((0,), (0,)), ((), ()))


def _dot(a, b):
    return jnp.dot(a, b, preferred_element_type=F32)


def _dot_nt(a, b):
    return lax.dot_general(a, b, _NT, preferred_element_type=F32)


def _dot_tn(a, b):
    return lax.dot_general(a, b, _TN, preferred_element_type=F32)


def _mx(x):
    return x.astype(MXU_DTYPE)


def _sigmoid(x):
    return 1.0 / (1.0 + jnp.exp(-x))


def _params(sem=None):
    return pltpu.CompilerParams(dimension_semantics=sem, vmem_limit_bytes=VMEM_LIMIT)


def _row_tile(s):
    return min(512, s)


def _w_in_spec(w_in_g, layer):
    return pl.BlockSpec((N_DEV, None) + w_in_g.shape[2:], lambda i: (0, layer, 0, 0))


def _w_out_spec(w_out_g, layer):
    return pl.BlockSpec((N_DEV, None) + w_out_g.shape[2:], lambda i: (0, layer, 0, 0))


def _ln_proj(x, shift, scale1p, g, w_in_g, layer, gather=()):
    s = x.shape[0]
    ts = _row_tile(s)
    ns = s // ts
    n_g = len(gather)

    def body(x_ref, sh_ref, sc_ref, g_ref, w_ref, *rest):
        ret_ref, qkv_ref, sg_ref = rest[n_g:n_g + 3]
        if n_g:
            start, forward, finish = _gather_plan(rest[:n_g], rest[n_g + 3:2 * n_g + 3], *rest[2 * n_g + 3:])
            i = pl.program_id(0)
            pl.when(i == 0)(start)
            pl.when(i == max(ns - 4, 0))(forward)
        xv = x_ref[...]
        rstd = lax.rsqrt(jnp.mean(xv * xv, axis=-1, keepdims=True) + EPS)
        h = (xv * rstd * g_ref[...]) * sc_ref[...] + sh_ref[...]
        hb = _mx(h)
        for n in range(4):
            ret_ref[:, n * 512:(n + 1) * 512] = _dot(hb, w_ref[n])
        qkv_ref[:, 0:512] = _mx(_dot(hb, w_ref[4]) * SB_SCALE)
        qkv_ref[:, 512:1024] = _mx(_dot(hb, w_ref[5]))
        qkv_ref[:, 1024:1536] = _mx(_dot(hb, w_ref[6]))
        sg_ref[...] = _dot(hb, w_ref[7])
        if n_g:
            pl.when(i == ns - 1)(finish)

    vec = pl.BlockSpec((1, D_MODEL), lambda i: (0, 0))
    return pl.pallas_call(
        body, name="ln_proj_gather" if n_g else "ln_proj", grid=(ns,),
        in_specs=[pl.BlockSpec((ts, D_MODEL), lambda i: (i, 0)), vec, vec, vec,
                  _w_in_spec(w_in_g, layer)] + [HBM_SPEC] * n_g,
        out_specs=[pl.BlockSpec((ts, 2048), lambda i: (i, 0)),
                   pl.BlockSpec((ts, 1536), lambda i: (i, 0)),
                   pl.BlockSpec((ts, 512), lambda i: (i, 0))] + [HBM_SPEC] * n_g,
        out_shape=[jax.ShapeDtypeStruct((s, 2048), F32),
                   jax.ShapeDtypeStruct((s, 1536), MXU_DTYPE),
                   jax.ShapeDtypeStruct((s, 512), F32)] + _gathered_shapes(gather),
        scratch_shapes=_gather_sems(n_g) if n_g else (),
        compiler_params=_params(("arbitrary",)),
    )(x, shift, scale1p, g, w_in_g, *gather)


def _w_out_halves(w_ref):
    half = N_DEV // 2
    return (w_ref[0:half].reshape(D_RET, D_MODEL), w_ref[half:N_DEV].reshape(D_SB, D_MODEL))


def _out_proj(x, gate, y_r, y_s, w_out_g, layer, loss_head=None):
    s = x.shape[0]
    ts = _row_tile(s)

    def layer_out(x_ref, gate_ref, yr_ref, ys_ref, w_ref):
        w_r, w_s = _w_out_halves(w_ref)
        return x_ref[...] + gate_ref[...] * (_dot(yr_ref[...], w_r) + _dot(ys_ref[...], w_s))

    def body(x_ref, gate_ref, yr_ref, ys_ref, w_ref, o_ref):
        o_ref[...] = layer_out(x_ref, gate_ref, yr_ref, ys_ref, w_ref)

    def body_loss(x_ref, gate_ref, yr_ref, ys_ref, w_ref, fg_ref, t_ref, dx_ref, loss_ref, dfg_ref):
        @pl.when(pl.program_id(0) == 0)
        def _():
            loss_ref[...] = jnp.zeros_like(loss_ref)
            dfg_ref[...] = jnp.zeros_like(dfg_ref)

        xv = layer_out(x_ref, gate_ref, yr_ref, ys_ref, w_ref)
        fgv = fg_ref[...]
        rstd = lax.rsqrt(jnp.mean(xv * xv, axis=-1, keepdims=True) + EPS)
        xn = xv * rstd
        err = xn * fgv - t_ref[...]
        tok = jnp.mean(err * err, axis=-1, keepdims=True)
        loss_ref[...] += 0.5 * jnp.sum(tok, axis=0, keepdims=True)
        dy = err * (1.0 / D_MODEL)
        dfg_ref[...] += jnp.sum(dy * xn, axis=0, keepdims=True)
        dxn = dy * fgv
        dx_ref[...] = rstd * (dxn - xn * jnp.mean(dxn * xn, axis=-1, keepdims=True))

    rows = pl.BlockSpec((ts, D_MODEL), lambda i: (i, 0))
    vec = pl.BlockSpec((1, D_MODEL), lambda i: (0, 0))
    in_specs = [rows, vec, pl.BlockSpec((ts, 512), lambda i: (i, 0)), pl.BlockSpec((ts, 512), lambda i: (i, 0)),
                _w_out_spec(w_out_g, layer)]
    if loss_head is None:
        return pl.pallas_call(
            body, name="out_proj", grid=(s // ts,), in_specs=in_specs, out_specs=rows,
            out_shape=jax.ShapeDtypeStruct((s, D_MODEL), F32),
            compiler_params=_params(("arbitrary",)),
        )(x, gate, y_r, y_s, w_out_g)
    return pl.pallas_call(
        body_loss, name="out_proj_loss", grid=(s // ts,), in_specs=in_specs + [vec, rows],
        out_specs=[rows, pl.BlockSpec((1, 1), lambda i: (0, 0)), vec],
        out_shape=[jax.ShapeDtypeStruct((s, D_MODEL), F32),
                   jax.ShapeDtypeStruct((1, 1), F32),
                   jax.ShapeDtypeStruct((1, D_MODEL), F32)],
        compiler_params=_params(("arbitrary",)),
    )(x, gate, y_r, y_s, w_out_g, *loss_head)


def _out_proj_bwd(dx_out, gate, y_r, y_s, w_out_g, layer):
    s = dx_out.shape[0]
    ts = _row_tile(s)
    ns = s // ts

    def body(dx_ref, gate_ref, yr_ref, ys_ref, w_ref, dy_ref, dw_ref, dgate_ref):
        i = pl.program_id(0)

        @pl.when(i == 0)
        def _():
            dw_ref[...] = jnp.zeros_like(dw_ref)

        dxv = dx_ref[...]
        dt = _mx(dxv * gate_ref[...])
        dxb = _mx(dxv)
        w_r, w_s = _w_out_halves(w_ref)
        dy_ref[:, 0:512] = _dot_nt(dt, w_r)
        dy_ref[:, 512:1024] = _dot_nt(dt, w_s)
        dw_ref[0:512, :] += _dot_tn(yr_ref[...], dxb)
        dw_ref[512:1024, :] += _dot_tn(ys_ref[...], dxb)

        @pl.when(i == ns - 1)
        def _():
            m_r, m_s = dw_ref[0:512, :], dw_ref[512:1024, :]
            dgate_ref[...] = (jnp.sum(w_r.astype(F32) * m_r, axis=0, keepdims=True)
                              + jnp.sum(w_s.astype(F32) * m_s, axis=0, keepdims=True))
            dw_ref[...] = dw_ref[...] * gate_ref[...]

    return pl.pallas_call(
        body, name="out_proj_bwd", grid=(ns,),
        in_specs=[pl.BlockSpec((ts, D_MODEL), lambda i: (i, 0)),
                  pl.BlockSpec((1, D_MODEL), lambda i: (0, 0)),
                  pl.BlockSpec((ts, 512), lambda i: (i, 0)),
                  pl.BlockSpec((ts, 512), lambda i: (i, 0)),
                  _w_out_spec(w_out_g, layer)],
        out_specs=[pl.BlockSpec((ts, D_MODEL), lambda i: (i, 0)),
                   pl.BlockSpec((D_MODEL, D_MODEL), lambda i: (0, 0)),
                   pl.BlockSpec((1, D_MODEL), lambda i: (0, 0))],
        out_shape=[jax.ShapeDtypeStruct((s, D_MODEL), F32),
                   jax.ShapeDtypeStruct((D_MODEL, D_MODEL), F32),
                   jax.ShapeDtypeStruct((1, D_MODEL), F32)],
        compiler_params=_params(("arbitrary",)),
    )(dx_out, gate, y_r, y_s, w_out_g)


def _scatter_plan(parts_ref, recv_ref, send_sems, recv_sems, local_sem):
    px, py, pc = _my_place()
    mine = _linear(px, py, pc)

    def copy(r):
        peer = (1 - px if r & 4 else px, 1 - py if r & 2 else py, 1 - pc if r & 1 else pc)
        return pltpu.make_async_remote_copy(
            src_ref=parts_ref.at[_linear(*peer)], dst_ref=recv_ref.at[mine],
            send_sem=send_sems.at[r - 1], recv_sem=recv_sems.at[r - 1],
            device_id=peer, device_id_type=MESH_IDS)

    own = pltpu.make_async_copy(parts_ref.at[mine], recv_ref.at[mine], local_sem.at[0])

    def start():
        own.start()
        for r in range(1, N_DEV):
            copy(r).start()

    def finish():
        for r in range(1, N_DEV):
            copy(r).wait_recv()
            copy(r).wait_send()
        own.wait()

    return start, finish


def _in_proj_bwd_x(x, dx_out, dproj, shift, scale1p, g, w_in_g, layer):
    s = x.shape[0]
    ts = _row_tile(s)
    ns = s // ts
    nb = D_IN // N_DEV

    def body(x_ref, dxo_ref, dr_ref, d4_ref, d5_ref, d6_ref, d7_ref, sh_ref, sc_ref, g_ref, w_ref,
             dx_ref, dsh_ref, dsc_ref, dg_ref, ht_ref):
        i = pl.program_id(0)

        @pl.when(i == 0)
        def _():
            dsh_ref[...] = jnp.zeros_like(dsh_ref)
            dsc_ref[...] = jnp.zeros_like(dsc_ref)
            dg_ref[...] = jnp.zeros_like(dg_ref)

        dh = _dot_nt(dr_ref[:, 0:nb], w_ref[0])
        for n in range(1, 4):
            dh += _dot_nt(dr_ref[:, n * nb:(n + 1) * nb], w_ref[n])
        for n, d_ref in zip(range(4, N_DEV), (d4_ref, d5_ref, d6_ref, d7_ref)):
            dh += _dot_nt(d_ref[...], w_ref[n])
        xv = x_ref[...]
        gv = g_ref[...]
        scv = sc_ref[...]
        rstd = lax.rsqrt(jnp.mean(xv * xv, axis=-1, keepdims=True) + EPS)
        xn = xv * rstd
        xg = xn * gv
        ht_ref[...] = _mx((xg * scv + sh_ref[...]).T)
        dsh_ref[...] += jnp.sum(dh, axis=0, keepdims=True)
        dsc_ref[...] += jnp.sum(dh * xg, axis=0, keepdims=True)
        dhs = dh * scv
        dg_ref[...] += jnp.sum(dhs * xn, axis=0, keepdims=True)
        dxn = dhs * gv
        dx_ref[...] = rstd * (dxn - xn * jnp.mean(dxn * xn, axis=-1, keepdims=True)) + dxo_ref[...]

    vec = pl.BlockSpec((1, D_MODEL), lambda i: (0, 0))
    return pl.pallas_call(
        body, name="in_proj_bwd_x", grid=(ns,),
        in_specs=[pl.BlockSpec((ts, D_MODEL), lambda i: (i, 0)),
                  pl.BlockSpec((ts, D_MODEL), lambda i: (i, 0)),
                  pl.BlockSpec((ts, 4 * nb), lambda i: (i, 0))]
                 + [pl.BlockSpec((ts, nb), lambda i: (i, 0))] * 4
                 + [vec, vec, vec, _w_in_spec(w_in_g, layer)],
        out_specs=[pl.BlockSpec((ts, D_MODEL), lambda i: (i, 0)), vec, vec, vec,
                   pl.BlockSpec((D_MODEL, ts), lambda i: (0, i))],
        out_shape=[jax.ShapeDtypeStruct((s, D_MODEL), F32),
                   jax.ShapeDtypeStruct((1, D_MODEL), F32),
                   jax.ShapeDtypeStruct((1, D_MODEL), F32),
                   jax.ShapeDtypeStruct((1, D_MODEL), F32),
                   jax.ShapeDtypeStruct((D_MODEL, s), MXU_DTYPE)],
        compiler_params=_params(("arbitrary",)),
    )(x, dx_out, *dproj, shift, scale1p, g, w_in_g)


def _in_proj_bwd_w(me, h_t, dproj):
    s = h_t.shape[1]
    ts = min(4 * _row_tile(s), s)
    ns = s // ts
    nb = D_IN // N_DEV
    n_chip = N_DEV // 2

    def flip_bits(j):
        return jnp.where(j == 0, 4, jnp.where(j == 1, 2, jnp.where(j == 2, 6, 0)))

    def slab_of(t, me_ref):
        return jnp.bitwise_xor(me_ref[0], flip_bits(t // 2) + 1 - t % 2)

    def body(me_ref, ht_ref, dr_ref, d4_ref, d5_ref, d6_ref, d7_ref, rin_ref,
             acc, stage, pre_buf, pre_send, pre_recv, sum_send, sum_recv, local_sem):
        t = pl.program_id(0)
        i = pl.program_id(1)
        j = t // 2
        summing = t % 2 == 1
        slab = slab_of(t, me_ref)
        px, py, pc = _my_place()

        def pre_copy(jj):
            return pltpu.make_async_remote_copy(
                src_ref=stage.at[jj % 2], dst_ref=pre_buf.at[jj],
                send_sem=pre_send.at[jj], recv_sem=pre_recv.at[jj],
                device_id=(px, py, 1 - pc), device_id_type=MESH_IDS)

        def sum_copy(jj):
            fx = jnp.logical_or(jj == 0, jj == 2)
            fy = jnp.logical_or(jj == 1, jj == 2)
            return pltpu.make_async_remote_copy(
                src_ref=stage.at[2 + jj % 2], dst_ref=rin_ref.at[jj],
                send_sem=sum_send.at[jj], recv_sem=sum_recv.at[jj],
                device_id=(jnp.where(fx, 1 - px, px), jnp.where(fy, 1 - py, py), pc), device_id_type=MESH_IDS)

        own = pltpu.make_async_copy(stage.at[3], rin_ref.at[n_chip - 1], local_sem.at[0])

        @pl.when(i == 0)
        def _():
            acc[...] = jnp.zeros_like(acc)

        @pl.when(slab < 4)
        def _():
            acc[...] += _dot(ht_ref[...], dr_ref[...])

        for n, d_ref in zip(range(4, N_DEV), (d4_ref, d5_ref, d6_ref, d7_ref)):
            @pl.when(slab == n)
            def _():
                acc[...] += _dot(ht_ref[...], d_ref[...])

        @pl.when(jnp.logical_and(i == ns - 1, jnp.logical_not(summing)))
        def _():
            @pl.when(j >= 2)
            def _():
                pre_copy(j - 2).wait_send()

            stage[j % 2] = acc[...].astype(stage.dtype)
            pre_copy(j).start()

        @pl.when(jnp.logical_and(i == ns - 1, summing))
        def _():
            pre_copy(j).wait_recv()

            @pl.when(j >= 2)
            def _():
                sum_copy(j - 2).wait_send()

            stage[2 + j % 2] = (acc[...] + pre_buf[j].astype(F32)).astype(stage.dtype)

            @pl.when(j < n_chip - 1)
            def _():
                sum_copy(j).start()

            @pl.when(j == n_chip - 1)
            def _():
                own.start()
                pre_copy(n_chip - 2).wait_send()
                pre_copy(n_chip - 1).wait_send()
                sum_copy(n_chip - 2).wait_send()
                for jj in range(n_chip - 1):
                    sum_copy(jj).wait_recv()
                own.wait()

    def part_rows(n, t, i, me_ref):
        return jnp.where(slab_of(t, me_ref) == n, i, ns - 1), 0

    return pl.pallas_call(
        body, name="in_proj_bwd_w",
        grid_spec=pltpu.PrefetchScalarGridSpec(
            num_scalar_prefetch=1, grid=(N_DEV, ns),
            in_specs=[pl.BlockSpec((D_MODEL, ts), lambda t, i, me_ref: (0, i)),
                      pl.BlockSpec((ts, nb), lambda t, i, me_ref: (
                          jnp.where(slab_of(t, me_ref) < 4, i, ns - 1), jnp.minimum(slab_of(t, me_ref), 3)))]
                     + [pl.BlockSpec((ts, nb), functools.partial(part_rows, n)) for n in range(4, N_DEV)],
            out_specs=HBM_SPEC,
            scratch_shapes=[pltpu.VMEM((D_MODEL, nb), F32),
                            pltpu.VMEM((4, D_MODEL, nb), MXU_DTYPE),
                            pltpu.VMEM((n_chip, D_MODEL, nb), MXU_DTYPE),
                            pltpu.SemaphoreType.DMA((n_chip,)), pltpu.SemaphoreType.DMA((n_chip,)),
                            pltpu.SemaphoreType.DMA((n_chip - 1,)), pltpu.SemaphoreType.DMA((n_chip - 1,)),
                            pltpu.SemaphoreType.DMA((1,))]),
        out_shape=jax.ShapeDtypeStruct((n_chip, D_MODEL, nb), MXU_DTYPE),
        compiler_params=_params(("arbitrary", "arbitrary")),
    )(jnp.reshape(me, (1,)).astype(jnp.int32), h_t, *dproj)


RET_TILE = 256


def _ret_tables(s):
    t = min(RET_TILE, s)
    half = RET_HEAD_DIM // 2
    pos = jnp.arange(s, dtype=F32)
    inv = ROPE_BASE ** (-jnp.arange(half, dtype=F32) / half)
    ang = pos[:, None] * inv[None, :]
    cos, sin = jnp.cos(ang), jnp.sin(ang)
    cos2 = jnp.concatenate([cos, cos], axis=1)
    sin2 = jnp.concatenate([-sin, sin], axis=1)
    lg = jnp.log1p(-(2.0 ** (-5.0 - jnp.arange(RET_HEADS, dtype=F32))))[:, None, None]
    n = jnp.arange(t)
    dist = (n[:, None] - n[None, :]).astype(F32)[None]
    cn = (n // CHUNK)[:, None]
    cm = (n // CHUNK)[None, :]
    mask = jnp.where((cn == cm)[None], jnp.exp(jnp.abs(dist) * lg),
                     jnp.where((cm < cn)[None], jnp.exp(dist * lg), 0.0))
    nf = n.astype(F32)[None, :, None]
    dq = jnp.broadcast_to(jnp.exp((nf + 1.0) * lg), (RET_HEADS, t, LANES))
    dk = jnp.broadcast_to(jnp.exp((t - 1.0 - nf) * lg), (RET_HEADS, t, LANES))
    gt = jnp.broadcast_to(jnp.exp(float(t) * lg), (RET_HEADS, 1, LANES))
    return cos2, sin2, mask, dq, dk, gt


def _roll_half(x):
    return pltpu.roll(x, RET_HEAD_DIM // 2, 1)


def _ret_heads_fwd(ret_ref, cos, sin, m_ref, dq_ref, dk_ref, s0):
    hd = RET_HEAD_DIM
    heads = range(RET_HEADS)
    qb, kb, vb, kdb = [], [], [], []
    for h in heads:
        q = ret_ref[:, h * hd:(h + 1) * hd]
        k = ret_ref[:, 512 + h * hd:512 + (h + 1) * hd]
        kr = (k * cos + _roll_half(k) * sin) * RET_KSCALE
        qb.append(_mx(q * cos + _roll_half(q) * sin))
        kb.append(_mx(kr))
        kdb.append(_mx(kr * dk_ref[h]))
        vb.append(_mx(ret_ref[:, 1024 + h * hd:1024 + (h + 1) * hd]))
    p = [_dot_nt(qb[h], kb[h]) for h in heads]
    cross = [_dot(qb[h], _mx(s0[h])) for h in heads]
    pb = [_mx(p[h] * m_ref[h]) for h in heads]
    o = [_dot(pb[h], vb[h]) + cross[h] * dq_ref[h] for h in heads]
    gn, rstd = [], []
    for h in heads:
        oc = o[h] - jnp.mean(o[h], axis=-1, keepdims=True)
        rstd.append(lax.rsqrt(jnp.mean(oc * oc, axis=-1, keepdims=True) + EPS))
        gn.append(oc * rstd[h])
    return qb, kb, vb, pb, kdb, gn, rstd


def _retention_fwd(ret, tables):
    cos2, sin2, mask, dq, dk, gt = tables
    s = ret.shape[0]
    t = mask.shape[1]
    nt = s // t
    hd = RET_HEAD_DIM

    def body(ret_ref, cos_ref, sin_ref, m_ref, dq_ref, dk_ref, gt_ref, y_ref, st_ref, s_scr):
        i = pl.program_id(0)

        @pl.when(i == 0)
        def _():
            s_scr[...] = jnp.zeros_like(s_scr)

        s0 = [s_scr[h] for h in range(RET_HEADS)]
        _, _, vb, _, kdb, gn, _ = _ret_heads_fwd(ret_ref, cos_ref[...], sin_ref[...], m_ref, dq_ref, dk_ref, s0)
        kv = [_dot_tn(kdb[h], vb[h]) for h in range(RET_HEADS)]
        for h in range(RET_HEADS):
            g = ret_ref[:, 1536 + h * hd:1536 + (h + 1) * hd]
            st_ref[h] = s0[h]
            y_ref[:, h * hd:(h + 1) * hd] = (gn[h] * (g * _sigmoid(g))).astype(y_ref.dtype)
            s_scr[h] = s0[h] * gt_ref[h] + kv[h]

    full3 = lambda a: pl.BlockSpec(a.shape, lambda i: (0, 0, 0))
    return pl.pallas_call(
        body, name="retention_fwd", grid=(nt,),
        in_specs=[pl.BlockSpec((t, 2048), lambda i: (i, 0)),
                  pl.BlockSpec((t, LANES), lambda i: (i, 0)),
                  pl.BlockSpec((t, LANES), lambda i: (i, 0)),
                  full3(mask), full3(dq), full3(dk), full3(gt)],
        out_specs=[pl.BlockSpec((t, 512), lambda i: (i, 0)),
                   pl.BlockSpec((None, RET_HEADS, hd, hd), lambda i: (i, 0, 0, 0))],
        out_shape=[jax.ShapeDtypeStruct((s, 512), MXU_DTYPE),
                   jax.ShapeDtypeStruct((nt, RET_HEADS, hd, hd), F32)],
        scratch_shapes=[pltpu.VMEM((RET_HEADS, hd, hd), F32)],
        compiler_params=_params(("arbitrary",)),
    )(ret, cos2, sin2, mask, dq, dk, gt)


def _retention_bwd(ret, states, dy, tables, dwo_parts):
    cos2, sin2, mask, dq, dk, gt = tables
    s = ret.shape[0]
    t = mask.shape[1]
    nt = s // t
    hd = RET_HEAD_DIM

    def body(ret_ref, st_ref, dy_ref, cos_ref, sin_ref, m_ref, dq_ref, dk_ref, gt_ref, dwo_ref,
             d_ref, rout_ref, ds_scr, send_sems, recv_sems, local_sem):
        i = pl.program_id(0)
        start, finish = _scatter_plan(dwo_ref, rout_ref, send_sems, recv_sems, local_sem)

        @pl.when(i == 0)
        def _():
            start()
            ds_scr[...] = jnp.zeros_like(ds_scr)

        cos = cos_ref[...]
        sin = sin_ref[...]
        heads = range(RET_HEADS)
        s0 = [st_ref[h] for h in heads]
        ds = [ds_scr[h] for h in heads]
        dsb = [_mx(ds[h]) for h in heads]
        qb, kb, vb, pb, kdb, gn, rstd = _ret_heads_fwd(ret_ref, cos, sin, m_ref, dq_ref, dk_ref, s0)
        dob, dodb = [], []
        for h in heads:
            g = ret_ref[:, 1536 + h * hd:1536 + (h + 1) * hd]
            dyv = dy_ref[:, h * hd:(h + 1) * hd]
            sg = _sigmoid(g)
            d_ref[:, 1536 + h * hd:1536 + (h + 1) * hd] = (
                dyv * gn[h] * (sg * (1.0 + g * (1.0 - sg)))).astype(d_ref.dtype)
            dgn = dyv * (g * sg)
            do = rstd[h] * (dgn - jnp.mean(dgn, axis=-1, keepdims=True)
                            - gn[h] * jnp.mean(dgn * gn[h], axis=-1, keepdims=True))
            dob.append(_mx(do))
            dodb.append(_mx(do * dq_ref[h]))
        dp = [_dot_nt(dob[h], vb[h]) for h in heads]
        dv = [_dot_tn(pb[h], dob[h]) + _dot(kdb[h], dsb[h]) for h in heads]
        dq_cross = [_dot_nt(dodb[h], _mx(s0[h])) for h in heads]
        dk_cross = [_dot_nt(vb[h], dsb[h]) for h in heads]
        ds_new = [_dot_tn(qb[h], dodb[h]) for h in heads]
        dpb = [_mx(dp[h] * m_ref[h]) for h in heads]
        dqr = [_dot(dpb[h], kb[h]) + dq_cross[h] for h in heads]
        dkr = [(_dot_tn(dpb[h], qb[h]) + dk_cross[h] * dk_ref[h]) * RET_KSCALE for h in heads]
        for h in heads:
            d_ref[:, 1024 + h * hd:1024 + (h + 1) * hd] = dv[h].astype(d_ref.dtype)
            d_ref[:, h * hd:(h + 1) * hd] = (dqr[h] * cos + _roll_half(dqr[h] * sin)).astype(d_ref.dtype)
            d_ref[:, 512 + h * hd:512 + (h + 1) * hd] = (
                dkr[h] * cos + _roll_half(dkr[h] * sin)).astype(d_ref.dtype)
            ds_scr[h] = ds[h] * gt_ref[h] + ds_new[h]
        pl.when(i == nt - 1)(finish)

    full3 = lambda a: pl.BlockSpec(a.shape, lambda i: (0, 0, 0))
    rev = lambda i: (nt - 1 - i, 0)
    return pl.pallas_call(
        body, name="retention_bwd", grid=(nt,),
        in_specs=[pl.BlockSpec((t, 2048), rev),
                  pl.BlockSpec((None, RET_HEADS, hd, hd), lambda i: (nt - 1 - i, 0, 0, 0)),
                  pl.BlockSpec((t, 512), rev),
                  pl.BlockSpec((t, LANES), rev),
                  pl.BlockSpec((t, LANES), rev),
                  full3(mask), full3(dq), full3(dk), full3(gt), HBM_SPEC],
        out_specs=[pl.BlockSpec((t, 2048), rev), HBM_SPEC],
        out_shape=[jax.ShapeDtypeStruct((s, 2048), MXU_DTYPE),
                   jax.ShapeDtypeStruct(dwo_parts.shape, dwo_parts.dtype)],
        scratch_shapes=[pltpu.VMEM((RET_HEADS, hd, hd), F32),
                        pltpu.SemaphoreType.DMA((N_PEERS,)), pltpu.SemaphoreType.DMA((N_PEERS,)),
                        pltpu.SemaphoreType.DMA((1,))],
        compiler_params=_params(("arbitrary",)),
    )(ret, states, dy, cos2, sin2, mask, dq, dk, gt, dwo_parts)


SB_BLOCK = 256
SB_CHUNK = 32


SB_SKIP = 104.0
NO_KEYS = -1e30


def _split_dots(xs, u):
    parts = []
    for x in xs:
        hi = lax.bitcast_convert_type(lax.bitcast_convert_type(x, jnp.uint32) & jnp.uint32(0xFFFF0000), F32)
        parts += [_mx(hi), _mx(x - hi)]
    out = _dot(jnp.concatenate(parts, axis=0), u)
    n = xs[0].shape[0]
    return [out[2 * k * n:(2 * k + 1) * n] + out[(2 * k + 1) * n:(2 * k + 2) * n] for k in range(len(xs))]


def _split_dot(x, u):
    return _split_dots([x], u)[0]


def _sb_pair_weights(lb, lk, allowed, u_gt):
    blk = lb.shape[0]
    lk_p, lk_d = lk[:, :blk], lk[:, blk:]
    r_d = _rowsum(lk_d)
    cs_p, cs_d = _split_dots([lk_p, lk_d], u_gt)
    a = jnp.exp(lb + jnp.concatenate([cs_p + r_d, cs_d], axis=1))
    return jnp.where(allowed, a, 0.0), r_d, r_d + _rowsum(lk_p)


def _sb_logits(q, k, causal):
    z = _dot_nt(q, k)
    l1p = jnp.log(1.0 + jnp.exp(-jnp.abs(z)))
    lk = -(jnp.maximum(z, 0.0) + l1p)
    if causal is not None:
        lk = jnp.where(causal, lk, 0.0)
    return jnp.minimum(z, 0.0) - l1p, lk


def _sb_weights(lb, lk, r, u_gt, causal):
    a = jnp.exp(lb + _split_dot(lk, u_gt) + r)
    return a if causal is None else jnp.where(causal, a, 0.0)


def _rowsum(x):
    return jnp.sum(x, axis=1, keepdims=True)


def _sb_pair_tile(i, blk):
    row = lax.broadcasted_iota(jnp.int32, (blk, 2 * blk), 0)
    col = lax.broadcasted_iota(jnp.int32, (blk, 2 * blk), 1)
    first_col = jnp.where(i >= 1, 0, blk)
    allowed = jnp.logical_and(row > col - blk, col >= first_col)
    rows_p = pl.ds(pl.multiple_of(jnp.maximum(i - 1, 0) * blk, blk), blk)
    rows_d = pl.ds(pl.multiple_of(i * blk, blk), blk)
    return allowed, rows_p, rows_d


def _sb_fwd(qkv, sg):
    s = qkv.shape[0]
    blk = min(SB_BLOCK, s)
    nq = s // blk
    hd = SB_HEAD_DIM

    ch = min(SB_CHUNK, blk)

    def body(q_ref, k_ref, v_ref, g_ref, y_ref, o_ref, z_scr, lhs_scr, cs_scr, a_scr):
        i = pl.program_id(1)
        row = lax.broadcasted_iota(jnp.int32, (blk, blk), 0)
        col = lax.broadcasted_iota(jnp.int32, (blk, blk), 1)
        u_gt = (row > col).astype(MXU_DTYPE)
        heads = [slice(hh * hd, (hh + 1) * hd) for hh in range(2)]
        qs = [q_ref[:, ls] for ls in heads]
        _, rows_p, rows_d = _sb_pair_tile(i, blk)
        has_prev = i >= 1
        crow = lax.broadcasted_iota(jnp.int32, (ch, blk), 0)
        ccol = lax.broadcasted_iota(jnp.int32, (ch, blk), 1)

        def logits(hh):
            kc = jnp.concatenate([k_ref[rows_p, heads[hh]], k_ref[rows_d, heads[hh]]], axis=0)
            z_scr[hh] = _dot_nt(qs[hh], kc)

        def keep_parts(hh):
            r_d, r_all = [], []
            for c in range(blk // ch):
                rows = pl.ds(c * ch, ch)
                causal = crow + c * ch > ccol
                z = z_scr[hh, rows, :]
                l1p = jnp.log(1.0 + jnp.exp(-jnp.abs(z)))
                lb = jnp.minimum(z, 0.0) - l1p
                z_scr[hh, rows, :] = lb
                lk = lb - z
                lk_p = lk[:, :blk]
                lk_d = jnp.where(causal, lk[:, blk:], 0.0)
                lhs_scr[hh, pl.ds(c * ch, ch), :] = _mx(lk_p)
                lhs_scr[hh, pl.ds(blk + c * ch, ch), :] = _mx(lk_d)
                r_d.append(_rowsum(lk_d))
                r_all.append(r_d[c] + _rowsum(lk_p))
            return r_d, jnp.concatenate(r_all, axis=0)

        def suffix_sums(hh):
            cs_scr[hh] = _dot(lhs_scr[hh], u_gt)

        def weights(hh, r_d):
            for c in range(blk // ch):
                rows = pl.ds(c * ch, ch)
                causal = crow + c * ch > ccol
                cs_p = cs_scr[hh, pl.ds(c * ch, ch), :] + jnp.where(has_prev, r_d[c], NO_KEYS)
                cs_d = cs_scr[hh, pl.ds(blk + c * ch, ch), :]
                lb = z_scr[hh, rows, :]
                a_p = jnp.exp(lb[:, :blk] + cs_p)
                a_d = jnp.where(causal, jnp.exp(lb[:, blk:] + cs_d), 0.0)
                a_scr[hh, rows, :] = _mx(jnp.concatenate([a_p, a_d], axis=1))

        def values(hh):
            vc = jnp.concatenate([v_ref[rows_p, heads[hh]], v_ref[rows_d, heads[hh]]], axis=0)
            return _dot(a_scr[hh], vc)

        def block(hh, j, r):
            start = pl.multiple_of(j * blk, blk)
            lb, lk = _sb_logits(qs[hh], k_ref[pl.ds(start, blk), heads[hh]], None)
            a = _sb_weights(lb, lk, r, u_gt, None)
            return _dot(_mx(a), v_ref[pl.ds(start, blk), heads[hh]]), r + _rowsum(lk)

        def more(n, r0, r1):
            return jnp.logical_and(n < i, jnp.max(jnp.maximum(r0, r1)) > -SB_SKIP)

        logits(0)
        logits(1)
        gv = g_ref[...]
        gates = gv * _sigmoid(gv)
        rd0, r0 = keep_parts(0)
        suffix_sums(0)
        rd1, r1 = keep_parts(1)
        suffix_sums(1)
        go = more(jnp.int32(1), r0, r1)
        weights(0, rd0)
        acc0 = values(0)
        weights(1, rd1)
        acc1 = values(1)

        def step(c):
            _, n, acc0, r0, acc1, r1 = c
            pv0, r0 = block(0, i - 1 - n, r0)
            pv1, r1 = block(1, i - 1 - n, r1)
            return more(n + 1, r0, r1), n + 1, acc0 + pv0, r0, acc1 + pv1, r1

        _, _, acc0, _, acc1, _ = lax.while_loop(lambda c: c[0], step, (go, jnp.int32(1), acc0, r0, acc1, r1))
        o = jnp.concatenate([acc0, acc1], axis=1)
        o_ref[...] = o
        y_ref[...] = (o * gates).astype(y_ref.dtype)

    qblk = pl.BlockSpec((blk, LANES), lambda p, i: (i, p))
    return pl.pallas_call(
        body, name="stickbreak_fwd", grid=(SB_HEADS // 2, nq),
        in_specs=[qblk,
                  pl.BlockSpec((s, LANES), lambda p, i: (0, 4 + p)),
                  pl.BlockSpec((s, LANES), lambda p, i: (0, 8 + p)),
                  qblk],
        out_specs=[qblk, qblk],
        out_shape=[jax.ShapeDtypeStruct((s, 512), MXU_DTYPE),
                   jax.ShapeDtypeStruct((s, 512), F32)],
        scratch_shapes=[pltpu.VMEM((2, blk, 2 * blk), F32),
                        pltpu.VMEM((2, 2 * blk, blk), MXU_DTYPE),
                        pltpu.VMEM((2, 2 * blk, blk), F32),
                        pltpu.VMEM((2, blk, 2 * blk), MXU_DTYPE)],
        compiler_params=_params(("arbitrary", "arbitrary")),
    )(qkv, qkv, qkv, sg)


def _sb_bwd(qkv, sg, o, dy):
    s = qkv.shape[0]
    blk = min(SB_BLOCK, s)
    nq = s // blk
    hd = SB_HEAD_DIM
    assert nq <= LANES
    ch = min(SB_CHUNK, blk)

    def body(q_ref, k_ref, v_ref, g_ref, o_ref, dy_ref, dq_ref, dk_ref, dv_ref, dg_ref, dk_scr, dv_scr,
             z_scr, g_scr, lhs_scr, cs_scr, a_scr, dz_scr):
        i = pl.program_id(1)

        @pl.when(i == 0)
        def _():
            dk_scr[...] = jnp.zeros_like(dk_scr)
            dv_scr[...] = jnp.zeros_like(dv_scr)

        row = lax.broadcasted_iota(jnp.int32, (blk, blk), 0)
        col = lax.broadcasted_iota(jnp.int32, (blk, blk), 1)
        lane = lax.broadcasted_iota(jnp.int32, (blk, LANES), 1)
        u_gt = (row > col).astype(MXU_DTYPE)
        u_lt = (row < col).astype(MXU_DTYPE)
        heads = [slice(hh * hd, (hh + 1) * hd) for hh in range(2)]
        qs = [q_ref[:, ls] for ls in heads]
        _, rows_p, rows_d = _sb_pair_tile(i, blk)
        has_prev = i >= 1
        crow = lax.broadcasted_iota(jnp.int32, (ch, blk), 0)
        ccol = lax.broadcasted_iota(jnp.int32, (ch, blk), 1)
        nch = blk // ch
        kcs = [jnp.concatenate([k_ref[rows_p, ls], k_ref[rows_d, ls]], axis=0) for ls in heads]
        dobs = []

        def gate_grads():
            g = g_ref[...]
            dyv = dy_ref[...]
            sgm = _sigmoid(g)
            dg_ref[...] = (dyv * o_ref[...] * (sgm * (1.0 + g * (1.0 - sgm)))).astype(dg_ref.dtype)
            dob = _mx(dyv * (g * sgm))
            dobs.extend(dob[:, ls] for ls in heads)

        def split_rows(hh, c, part, x):
            lhs_scr[hh, pl.ds(part * blk + c * ch, ch), :] = _mx(x)

        def summed_rows(hh, c, part):
            return cs_scr[hh, pl.ds(part * blk + c * ch, ch), :]

        def logits(hh):
            z_scr[hh] = _dot_nt(qs[hh], kcs[hh])

        def weight_grads(hh):
            vc = jnp.concatenate([v_ref[rows_p, heads[hh]], v_ref[rows_d, heads[hh]]], axis=0)
            g_scr[hh] = _dot_nt(dobs[hh], vc)

        def keep_parts(hh):
            r_d, r_all = [], []
            for c in range(nch):
                rows = pl.ds(c * ch, ch)
                z = z_scr[hh, rows, :]
                l1p = jnp.log(1.0 + jnp.exp(-jnp.abs(z)))
                lb = jnp.minimum(z, 0.0) - l1p
                z_scr[hh, rows, :] = lb
                lk = lb - z
                lk_p = lk[:, :blk]
                lk_d = jnp.where(crow + c * ch > ccol, lk[:, blk:], 0.0)
                split_rows(hh, c, 0, lk_p)
                split_rows(hh, c, 1, lk_d)
                r_d.append(_rowsum(lk_d))
                r_all.append(r_d[c] + _rowsum(lk_p))
            return r_d, jnp.concatenate(r_all, axis=0)

        def weights(hh, r_d):
            g_p = []
            for c in range(nch):
                rows = pl.ds(c * ch, ch)
                lb = z_scr[hh, rows, :]
                a_p = jnp.exp(lb[:, :blk] + (summed_rows(hh, c, 0) + jnp.where(has_prev, r_d[c], NO_KEYS)))
                a_d = jnp.where(crow + c * ch > ccol, jnp.exp(lb[:, blk:] + summed_rows(hh, c, 1)), 0.0)
                a = jnp.concatenate([a_p, a_d], axis=1)
                a_scr[hh, rows, :] = _mx(a)
                gm = g_scr[hh, rows, :] * a
                g_scr[hh, rows, :] = gm
                split_rows(hh, c, 0, gm[:, :blk])
                split_rows(hh, c, 1, gm[:, blk:])
                g_p.append(_rowsum(gm[:, :blk]))
            return g_p

        def logit_grads(hh, pg, g_p):
            for c in range(nch):
                rows = pl.ds(c * ch, ch)
                pre = jnp.concatenate([summed_rows(hh, c, 0) + pg[c * ch:(c + 1) * ch],
                                       summed_rows(hh, c, 1) + (pg[c * ch:(c + 1) * ch] + g_p[c])], axis=1)
                gm = g_scr[hh, rows, :]
                dz = gm - (gm + pre) * jnp.exp(z_scr[hh, rows, :])
                dz_p = dz[:, :blk]
                dz_d = jnp.where(crow + c * ch > ccol, dz[:, blk:], 0.0)
                dz_scr[hh, rows, :] = _mx(jnp.concatenate([dz_p, dz_d], axis=1))

        def products(hh, acc):
            ls = heads[hh]
            dk_scr[hh, rows_p, :] += _dot_tn(dz_scr[hh, :, 0:blk], qs[hh])
            dk_scr[hh, rows_d, :] += _dot_tn(dz_scr[hh, :, blk:2 * blk], qs[hh])
            dv_scr[hh, rows_p, :] += _dot_tn(a_scr[hh, :, 0:blk], dobs[hh])
            dv_scr[hh, rows_d, :] += _dot_tn(a_scr[hh, :, blk:2 * blk], dobs[hh])
            dq_ref[:, ls] = ((acc + _dot(dz_scr[hh], kcs[hh])) * SB_SCALE).astype(dq_ref.dtype)

        def suffix_sums(hh):
            cs_scr[hh] = _dot(lhs_scr[hh], u_gt)

        def prefix_sums(hh):
            cs_scr[hh] = _dot(lhs_scr[hh], u_lt)

        def more(n, r0, r1):
            return jnp.logical_and(n < i, jnp.max(jnp.maximum(r0, r1)) > -SB_SKIP)

        gate_grads()
        logits(0)
        weight_grads(0)
        logits(1)
        weight_grads(1)
        rd0, ra0 = keep_parts(0)
        suffix_sums(0)
        rd1, ra1 = keep_parts(1)
        suffix_sums(1)
        go = more(jnp.int32(1), ra0, ra1)
        gp0 = weights(0, rd0)
        prefix_sums(0)
        gp1 = weights(1, rd1)
        prefix_sums(1)

        def scan_block(hh, j, r, rmat):
            start = pl.multiple_of(j * blk, blk)
            _, lk = _sb_logits(qs[hh], k_ref[pl.ds(start, blk), heads[hh]], None)
            return r + _rowsum(lk), jnp.where(lane == j, r, rmat)

        def scan_step(c):
            _, n, r0, rmat0, r1, rmat1 = c
            r0, rmat0 = scan_block(0, i - 1 - n, r0, rmat0)
            r1, rmat1 = scan_block(1, i - 1 - n, r1, rmat1)
            return more(n + 1, r0, r1), n + 1, r0, rmat0, r1, rmat1

        zmat = jnp.zeros((blk, LANES), F32)
        _, n, _, rmat0, _, rmat1 = lax.while_loop(lambda c: c[0], scan_step,
                                                  (go, jnp.int32(1), ra0, zmat, ra1, zmat))
        rmats = (rmat0, rmat1)

        def block(hh, j, pg):
            ls = heads[hh]
            start = pl.multiple_of(j * blk, blk)
            k = k_ref[pl.ds(start, blk), ls]
            lb, lk = _sb_logits(qs[hh], k, None)
            r = _rowsum(jnp.where(lane == j, rmats[hh], 0.0))
            a = _sb_weights(lb, lk, r, u_gt, None)
            gm = _dot_nt(dobs[hh], v_ref[pl.ds(start, blk), ls]) * a
            dzb = _mx(gm - (gm + (pg + _split_dot(gm, u_lt))) * jnp.exp(lb))
            dk_scr[hh, pl.ds(start, blk), :] += _dot_tn(dzb, qs[hh])
            dv_scr[hh, pl.ds(start, blk), :] += _dot_tn(_mx(a), dobs[hh])
            return _dot(dzb, k), pg + _rowsum(gm)

        def step(t, c):
            acc0, pg0, acc1, pg1 = c
            dq0, pg0 = block(0, i - n + t, pg0)
            dq1, pg1 = block(1, i - n + t, pg1)
            return acc0 + dq0, pg0, acc1 + dq1, pg1

        zero = jnp.zeros((blk, 1), F32)
        zacc = jnp.zeros((blk, hd), F32)
        acc0, pg0, acc1, pg1 = lax.fori_loop(0, n - 1, step, (zacc, zero, zacc, zero))
        logit_grads(0, pg0, gp0)
        logit_grads(1, pg1, gp1)
        products(0, acc0)
        products(1, acc1)

        @pl.when(i == nq - 1)
        def _():
            for hh in range(2):
                ls = slice(hh * hd, (hh + 1) * hd)
                dk_ref[:, ls] = dk_scr[hh].astype(dk_ref.dtype)
                dv_ref[:, ls] = dv_scr[hh].astype(dv_ref.dtype)

    qblk = lambda c0: pl.BlockSpec((blk, LANES), lambda p, i: (i, c0 + p))
    full = lambda c0: pl.BlockSpec((s, LANES), lambda p, i: (0, c0 + p))
    half = jax.ShapeDtypeStruct((s, 512), MXU_DTYPE)
    return pl.pallas_call(
        body, name="stickbreak_bwd", grid=(SB_HEADS // 2, nq),
        in_specs=[qblk(0), full(4), full(8), qblk(0), qblk(0), qblk(4)],
        out_specs=[qblk(0), full(0), full(0), qblk(0)],
        out_shape=[half, half, half, half],
        scratch_shapes=[pltpu.VMEM((2, s, hd), F32), pltpu.VMEM((2, s, hd), F32),
                        pltpu.VMEM((2, blk, 2 * blk), F32),
                        pltpu.VMEM((2, blk, 2 * blk), F32),
                        pltpu.VMEM((2, 2 * blk, blk), MXU_DTYPE),
                        pltpu.VMEM((2, 2 * blk, blk), F32),
                        pltpu.VMEM((2, blk, 2 * blk), MXU_DTYPE),
                        pltpu.VMEM((2, blk, 2 * blk), MXU_DTYPE)],
        compiler_params=_params(("arbitrary", "arbitrary")),
    )(qkv, qkv, qkv, sg, o, dy)


def _layer_fwd(layer, x, mod, norm_g, w_in_g, w_out_g, tables, gather=(), loss_head=None):
    shift, scale1p, gate = mod[0:1], 1.0 + mod[1:2], mod[2:3]
    ret, qkv, sg, *gathered = _ln_proj(x, shift, scale1p, norm_g, w_in_g, layer, gather)
    y_r, states = _retention_fwd(ret, tables)
    y_s, o_s = _sb_fwd(qkv, sg)
    x_next = _out_proj(x, gate, y_r, y_s, w_out_g, layer, loss_head)
    saved = (x, shift, scale1p, gate, ret, qkv, sg, y_r, states, y_s, o_s)
    return x_next, saved, gathered


def _layer_bwd(layer, me, dx_out, saved, norm_g, w_in_g, w_out_g, tables):
    x, shift, scale1p, gate, ret, qkv, sg, y_r, states, y_s, o_s = saved
    dy, dw_out, dgate = _out_proj_bwd(dx_out, gate, y_r, y_s, w_out_g, layer)
    dwo_parts = _mx(dw_out.reshape(N_DEV, D_MODEL // N_DEV, D_MODEL))
    d_ret, r_out = _retention_bwd(ret, states, dy, tables, dwo_parts)
    d_q, d_k, d_v, d_g = _sb_bwd(qkv, sg, o_s, dy)
    dproj = (d_ret, d_q, d_k, d_v, d_g)
    dx, dshift, dscale, dnorm_g, h_t = _in_proj_bwd_x(x, dx_out, dproj, shift, scale1p, norm_g, w_in_g, layer)
    r_in = _in_proj_bwd_w(me, h_t, dproj)
    dmod = jnp.concatenate([dshift, dscale, dgate], axis=1)
    return dx, r_in, r_out, dmod, dnorm_g


MESH_IDS = pl.DeviceIdType.MESH
N_PEERS = N_DEV - 1
HBM_SPEC = pl.BlockSpec(memory_space=pl.ANY)


def _my_place():
    return lax.axis_index("x"), lax.axis_index("y"), lax.axis_index("c")


def _linear(px, py, pc):
    return 4 * px + 2 * py + pc


def _all_gather(blocks):
    n_arr = len(blocks)

    def body(*refs):
        start, forward, finish = _gather_plan(refs[:n_arr], refs[n_arr:2 * n_arr], *refs[2 * n_arr:])
        start()
        forward()
        finish()

    return pl.pallas_call(
        body, name="all_gather",
        out_shape=_gathered_shapes(blocks),
        in_specs=[HBM_SPEC] * n_arr, out_specs=[HBM_SPEC] * n_arr,
        scratch_shapes=_gather_sems(n_arr),
    )(*blocks)


def _gathered_shapes(blocks):
    return [jax.ShapeDtypeStruct((N_DEV * b.shape[0], b.shape[1]), b.dtype) for b in blocks]


def _gather_sems(n_arr):
    return [pltpu.SemaphoreType.DMA((n_arr * N_PEERS,)), pltpu.SemaphoreType.DMA((n_arr * N_PEERS,)),
            pltpu.SemaphoreType.DMA((n_arr,))]


def _gather_plan(x_refs, out_refs, send_sems, recv_sems, local_sems):
    n_arr = len(x_refs)
    x, y, c = _my_place()
    me, sibling = (x, y, c), (x, y, 1 - c)
    chips = [(1 - x, y), (x, 1 - y), (1 - x, 1 - y)]

    def rows(a, place):
        m = x_refs[a].shape[0]
        return out_refs[a].at[pl.ds(_linear(*place) * m, m), :]

    def copy(a, k, block, to, src=None):
        return pltpu.make_async_remote_copy(
            src_ref=rows(a, block) if src is None else src, dst_ref=rows(a, block),
            send_sem=send_sems.at[a * N_PEERS + k], recv_sem=recv_sems.at[a * N_PEERS + k],
            device_id=to, device_id_type=MESH_IDS)

    mine = [pltpu.make_async_copy(x_refs[a], rows(a, me), local_sems.at[a]) for a in range(n_arr)]
    first = []
    for a in range(n_arr):
        first.append(copy(a, 0, me, sibling, src=x_refs[a]))
        first += [copy(a, 1 + j, me, (*chip, c), src=x_refs[a]) for j, chip in enumerate(chips)]
    passed = [copy(a, 4 + j, (*chip, c), sibling) for j, chip in enumerate(chips) for a in range(n_arr)]

    def start():
        for cp in mine + first:
            cp.start()

    def forward():
        for j, chip in enumerate(chips):
            for a in range(n_arr):
                copy(a, 1 + j, (*chip, c), me).wait_recv()
                passed[j * n_arr + a].start()

    def finish():
        for a in range(n_arr):
            copy(a, 0, sibling, me).wait_recv()
            for j, chip in enumerate(chips):
                copy(a, 4 + j, (*chip, 1 - c), me).wait_recv()
        for cp in first + passed:
            cp.wait_send()
        for cp in mine:
            cp.wait()

    return start, forward, finish


def _ada_fwd(c_all, w_ada, b_cols):
    cols = w_ada.shape[2]

    def body(c_ref, w_ref, b_ref, ca_ref, mod_ref):
        cv = c_ref[...]
        ca = cv * _sigmoid(cv)
        ca_ref[...] = ca
        cb = _mx(jnp.concatenate([ca, ca], axis=0))
        for l in range(DEPTH):
            mod_ref[l * N_DEV:(l + 1) * N_DEV, :] = _dot(cb, _mx(w_ref[l]))[0:N_DEV] + b_ref[l]

    return pl.pallas_call(
        body, name="ada_fwd",
        out_shape=[jax.ShapeDtypeStruct((N_DEV, D_MODEL), F32),
                   jax.ShapeDtypeStruct((DEPTH * N_DEV, cols), F32)],
        compiler_params=_params(),
    )(c_all, w_ada, b_cols)


def _ada_bwd(c_act_t, dmod_cols):
    cols = dmod_cols.shape[2]

    def body(ca_ref, dm_ref, o_ref):
        ca = _mx(ca_ref[...]).astype(F32)
        for l in range(DEPTH):
            o_ref[l] = jnp.dot(ca, _mx(dm_ref[l]).astype(F32),
                               precision=lax.Precision.HIGHEST, preferred_element_type=F32)

    return pl.pallas_call(
        body, name="ada_bwd",
        out_shape=jax.ShapeDtypeStruct((DEPTH, D_MODEL, cols), F32),
        compiler_params=_params(),
    )(c_act_t, dmod_cols)


def _adamw_store(g, w_ref, m_ref, v_ref, g_ref, d_ref, mo_ref, vo_ref):
    m2 = ADAM_B1 * m_ref[...] + (1.0 - ADAM_B1) * g
    v2 = ADAM_B2 * v_ref[...] + (1.0 - ADAM_B2) * (g * g)
    m_hat = m2 / (1.0 - ADAM_B1 ** ADAM_STEP)
    v_hat = v2 / (1.0 - ADAM_B2 ** ADAM_STEP)
    g_ref[...] = g
    d_ref[...] = -ADAM_LR * (m_hat / (jnp.sqrt(v_hat) + ADAM_EPS) + ADAM_WD * w_ref[...])
    mo_ref[...] = m2
    vo_ref[...] = v2


def _slab_sum(p_ref):
    g = p_ref[0].astype(F32)
    for sl in range(1, p_ref.shape[0]):
        g = g + p_ref[sl].astype(F32)
    return g


def _sum_adamw_layers(parts, w, m, v):
    n_slab, rows, cols = parts[0].shape
    tr = min(256, rows)
    nt = rows // tr

    def body(p0_ref, p1_ref, w_ref, m_ref, v_ref, g_ref, d_ref, mo_ref, vo_ref):
        for l, p_ref in enumerate((p0_ref, p1_ref)):
            @pl.when(pl.program_id(0) == l)
            def _():
                _adamw_store(_slab_sum(p_ref), w_ref, m_ref, v_ref, g_ref, d_ref, mo_ref, vo_ref)

    p_specs = [pl.BlockSpec((n_slab, tr, cols), lambda l, i: (0, i * (1 - l) + (nt - 1) * l, 0)),
               pl.BlockSpec((n_slab, tr, cols), lambda l, i: (0, i * l, 0))]
    blk = pl.BlockSpec((None, tr, cols), lambda l, i: (l, i, 0))
    shp = jax.ShapeDtypeStruct((DEPTH, rows, cols), F32)
    return pl.pallas_call(
        body, name="sum_adamw_layers", grid=(DEPTH, nt),
        in_specs=p_specs + [blk, blk, blk],
        out_specs=[blk, blk, blk, blk],
        out_shape=[shp, shp, shp, shp],
        compiler_params=_params(("arbitrary", "arbitrary")),
    )(parts[0], parts[1], w, m, v)


def _sum_adamw(parts, w, m, v):
    n_slab, rows, cols = parts.shape
    tr = min(256, rows)

    def body(p_ref, w_ref, m_ref, v_ref, g_ref, d_ref, mo_ref, vo_ref):
        _adamw_store(_slab_sum(p_ref), w_ref, m_ref, v_ref, g_ref, d_ref, mo_ref, vo_ref)

    blk = pl.BlockSpec((tr, cols), lambda i: (i, 0))
    shp = jax.ShapeDtypeStruct((rows, cols), F32)
    return pl.pallas_call(
        body, name="sum_adamw", grid=(rows // tr,),
        in_specs=[pl.BlockSpec((n_slab, tr, cols), lambda i: (0, i, 0)), blk, blk, blk],
        out_specs=[blk, blk, blk, blk],
        out_shape=[shp, shp, shp, shp],
        compiler_params=_params(("arbitrary",)),
    )(parts, w, m, v)


SMALL_ROWS = 16


def kernel(x, c, norm_g, w_ada, b_ada, w_in, w_out, final_g, loss_target, m_norm_g, m_w_ada, m_b_ada, m_w_in, m_w_out, m_final_g, v_norm_g, v_w_ada, v_b_ada, v_w_in, v_w_out, v_final_g):
    me = _linear(*_my_place())
    in_cols = w_in.shape[2]
    out_rows = w_out.shape[1]
    ada_cols = w_ada.shape[2]

    w_in_m, w_out_m = _mx(w_in), _mx(w_out)
    g_in, g_out, g_c = _all_gather([w_in_m[0], w_out_m[0], jnp.broadcast_to(c, (8, D_MODEL))])
    w_in_g = [g_in.reshape(N_DEV, 1, D_MODEL, in_cols), None]
    w_out_g = [g_out.reshape(N_DEV, 1, out_rows, D_MODEL), None]
    c_all = g_c.reshape(N_DEV, 8, D_MODEL)[:, 0]

    b_cols = lax.dynamic_slice_in_dim(b_ada, me * ada_cols, ada_cols, axis=1)[:, None, :]
    c_act, mod_cols = _ada_fwd(c_all, w_ada, b_cols)
    (g_mod,) = _all_gather([mod_cols])
    g_mod = g_mod.reshape(N_DEV, DEPTH, N_DEV, ada_cols)
    mod = lax.dynamic_index_in_dim(g_mod, me, axis=2, keepdims=False)
    mod = mod.transpose(1, 0, 2).reshape(DEPTH, 3, D_MODEL)

    tables = _ret_tables(x.shape[1])
    h = x[0]
    saved = []
    for l in range(DEPTH):
        nxt = (w_in_m[l + 1], w_out_m[l + 1]) if l + 1 < DEPTH else ()
        head = None if nxt else (final_g[None], loss_target[0])
        h, sv, gathered = _layer_fwd(0, h, mod[l], norm_g[l:l + 1], w_in_g[l], w_out_g[l], tables, nxt, head)
        if nxt:
            w_in_g[l + 1] = gathered[0].reshape(N_DEV, 1, D_MODEL, in_cols)
            w_out_g[l + 1] = gathered[1].reshape(N_DEV, 1, out_rows, D_MODEL)
        saved.append(sv)
    dx, loss_part, dfg = h
    r_in, r_out, small = [None] * DEPTH, [None] * DEPTH, [None] * DEPTH
    for l in reversed(range(DEPTH)):
        dx, r_in[l], r_out[l], dmod, dng = _layer_bwd(0, me, dx, saved[l], norm_g[l:l + 1], w_in_g[l], w_out_g[l], tables)
        small[l] = (dmod.reshape(3, D_MODEL), dng)

    pad = jnp.zeros((SMALL_ROWS - 10, D_MODEL), F32)
    small_block = jnp.concatenate([small[0][0], small[1][0], small[0][1], small[1][1], dfg,
                                   jnp.broadcast_to(loss_part, (1, D_MODEL)), pad], axis=0)
    (g_small,) = _all_gather([small_block])
    g_small = g_small.reshape(N_DEV, SMALL_ROWS, D_MODEL)

    def small_pack(b, n, f, fill):
        return jnp.concatenate([b.reshape(6, D_MODEL), n, f[None],
                                jnp.full((SMALL_ROWS - 9, D_MODEL), fill, F32)], axis=0)

    s_g, s_d, s_m, s_v = _sum_adamw(g_small, small_pack(b_ada, norm_g, final_g, 0.0),
                                    small_pack(m_b_ada, m_norm_g, m_final_g, 0.0),
                                    small_pack(v_b_ada, v_norm_g, v_final_g, 1.0))
    loss = s_g[9, 0]

    def small_unpack(a):
        return a[0:6].reshape(DEPTH, 3 * D_MODEL), a[6:8], a[8]

    dmod_all = g_small[:, 0:6].reshape(N_DEV, DEPTH, 3 * D_MODEL).transpose(1, 0, 2)
    dmod_cols = lax.dynamic_slice_in_dim(dmod_all, me * ada_cols, ada_cols, axis=2)
    g_ada = _ada_bwd(c_act.T, dmod_cols).reshape(1, DEPTH * D_MODEL, ada_cols)
    ada = _sum_adamw(g_ada, *[a.reshape(DEPTH * D_MODEL, ada_cols) for a in (w_ada, m_w_ada, v_w_ada)])
    ada = [a.reshape(DEPTH, D_MODEL, ada_cols) for a in ada]

    win = _sum_adamw_layers(r_in, w_in, m_w_in, v_w_in)
    wout = _sum_adamw_layers(r_out, w_out, m_w_out, v_w_out)

    outs = [loss, dx[None]]
    for k in range(4):
        b, n, f = small_unpack((s_g, s_d, s_m, s_v)[k])
        outs += [n, ada[k], b, win[k], wout[k], f]
    return tuple(outs)
```

```python
import functools

import jax
import jax.numpy as jnp
from jax import lax
from jax.experimental import pallas as pl
from jax.experimental.pallas import tpu as pltpu

F32 = jnp.float32
MXU_DTYPE = jnp.bfloat16

D_MODEL = 1024
DEPTH = 2
N_DEV = 8
CHUNK = 64
D_RET = 512
D_SB = 512
RET_HEADS = 4
RET_HEAD_DIM = 128
SB_HEADS = 8
SB_HEAD_DIM = 64
D_IN = 4096
ROPE_BASE = 10000.0
EPS = 1e-6
SB_SCALE = SB_HEAD_DIM ** -0.5
RET_KSCALE = RET_HEAD_DIM ** -0.5

ADAM_LR = 0.001
ADAM_B1 = 0.9
ADAM_B2 = 0.999
ADAM_EPS = 1e-08
ADAM_WD = 0.01
ADAM_STEP = 10

V7X_VMEM_BYTES = 64 * 2 ** 20
VMEM_LIMIT = V7X_VMEM_BYTES - 8 * 2 ** 20
LANES = 128

_NT = (((1,), (1,)), ((), ()))
_TN = (((0,), (0,)), ((), ()))


def _dot(a, b):
    return jnp.dot(a, b, preferred_element_type=F32)


def _dot_nt(a, b):
    return lax.dot_general(a, b, _NT, preferred_element_type=F32)


def _dot_tn(a, b):
    return lax.dot_general(a, b, _TN, preferred_element_type=F32)


def _mx(x):
    return x.astype(MXU_DTYPE)


def _sigmoid(x):
    return 1.0 / (1.0 + jnp.exp(-x))


def _params(sem=None):
    return pltpu.CompilerParams(dimension_semantics=sem, vmem_limit_bytes=VMEM_LIMIT)


def _row_tile(s):
    return min(512, s)


def _w_in_spec(w_in_g, layer):
    return pl.BlockSpec((N_DEV, None) + w_in_g.shape[2:], lambda i: (0, layer, 0, 0))


def _w_out_spec(w_out_g, layer):
    return pl.BlockSpec((N_DEV, None) + w_out_g.shape[2:], lambda i: (0, layer, 0, 0))


def _ln_proj(x, shift, scale1p, g, w_in_g, layer, gather=()):
    s = x.shape[0]
    ts = _row_tile(s)
    ns = s // ts
    n_g = len(gather)

    def body(x_ref, sh_ref, sc_ref, g_ref, w_ref, *rest):
        ret_ref, qkv_ref, sg_ref = rest[n_g:n_g + 3]
        if n_g:
            start, forward, finish = _gather_plan(rest[:n_g], rest[n_g + 3:2 * n_g + 3], *rest[2 * n_g + 3:])
            i = pl.program_id(0)
            pl.when(i == 0)(start)
            pl.when(i == max(ns - 4, 0))(forward)
        xv = x_ref[...]
        rstd = lax.rsqrt(jnp.mean(xv * xv, axis=-1, keepdims=True) + EPS)
        h = (xv * rstd * g_ref[...]) * sc_ref[...] + sh_ref[...]
        hb = _mx(h)
        for n in range(4):
            ret_ref[:, n * 512:(n + 1) * 512] = _dot(hb, w_ref[n])
        qkv_ref[:, 0:512] = _mx(_dot(hb, w_ref[4]) * SB_SCALE)
        qkv_ref[:, 512:1024] = _mx(_dot(hb, w_ref[5]))
        qkv_ref[:, 1024:1536] = _mx(_dot(hb, w_ref[6]))
        sg_ref[...] = _dot(hb, w_ref[7])
        if n_g:
            pl.when(i == ns - 1)(finish)

    vec = pl.BlockSpec((1, D_MODEL), lambda i: (0, 0))
    return pl.pallas_call(
        body, name="ln_proj_gather" if n_g else "ln_proj", grid=(ns,),
        in_specs=[pl.BlockSpec((ts, D_MODEL), lambda i: (i, 0)), vec, vec, vec,
                  _w_in_spec(w_in_g, layer)] + [HBM_SPEC] * n_g,
        out_specs=[pl.BlockSpec((ts, 2048), lambda i: (i, 0)),
                   pl.BlockSpec((ts, 1536), lambda i: (i, 0)),
                   pl.BlockSpec((ts, 512), lambda i: (i, 0))] + [HBM_SPEC] * n_g,
        out_shape=[jax.ShapeDtypeStruct((s, 2048), F32),
                   jax.ShapeDtypeStruct((s, 1536), MXU_DTYPE),
                   jax.ShapeDtypeStruct((s, 512), F32)] + _gathered_shapes(gather),
        scratch_shapes=_gather_sems(n_g) if n_g else (),
        compiler_params=_params(("arbitrary",)),
    )(x, shift, scale1p, g, w_in_g, *gather)


def _w_out_halves(w_ref):
    half = N_DEV // 2
    return (w_ref[0:half].reshape(D_RET, D_MODEL), w_ref[half:N_DEV].reshape(D_SB, D_MODEL))


def _out_proj(x, gate, y_r, y_s, w_out_g, layer, loss_head=None):
    s = x.shape[0]
    ts = min(2 * _row_tile(s), s)

    def layer_out(x_ref, gate_ref, yr_ref, ys_ref, w_ref):
        w_r, w_s = _w_out_halves(w_ref)
        return x_ref[...] + gate_ref[...] * (_dot(yr_ref[...], w_r) + _dot(ys_ref[...], w_s))

    def body(x_ref, gate_ref, yr_ref, ys_ref, w_ref, o_ref):
        o_ref[...] = layer_out(x_ref, gate_ref, yr_ref, ys_ref, w_ref)

    def body_loss(x_ref, gate_ref, yr_ref, ys_ref, w_ref, fg_ref, t_ref, dx_ref, loss_ref, dfg_ref):
        @pl.when(pl.program_id(0) == 0)
        def _():
            loss_ref[...] = jnp.zeros_like(loss_ref)
            dfg_ref[...] = jnp.zeros_like(dfg_ref)

        xv = layer_out(x_ref, gate_ref, yr_ref, ys_ref, w_ref)
        fgv = fg_ref[...]
        rstd = lax.rsqrt(jnp.mean(xv * xv, axis=-1, keepdims=True) + EPS)
        xn = xv * rstd
        err = xn * fgv - t_ref[...]
        tok = jnp.mean(err * err, axis=-1, keepdims=True)
        loss_ref[...] += 0.5 * jnp.sum(tok, axis=0, keepdims=True)
        dy = err * (1.0 / D_MODEL)
        dfg_ref[...] += jnp.sum(dy * xn, axis=0, keepdims=True)
        dxn = dy * fgv
        dx_ref[...] = rstd * (dxn - xn * jnp.mean(dxn * xn, axis=-1, keepdims=True))

    rows = pl.BlockSpec((ts, D_MODEL), lambda i: (i, 0))
    vec = pl.BlockSpec((1, D_MODEL), lambda i: (0, 0))
    in_specs = [rows, vec, pl.BlockSpec((ts, 512), lambda i: (i, 0)), pl.BlockSpec((ts, 512), lambda i: (i, 0)),
                _w_out_spec(w_out_g, layer)]
    if loss_head is None:
        return pl.pallas_call(
            body, name="out_proj", grid=(s // ts,), in_specs=in_specs, out_specs=rows,
            out_shape=jax.ShapeDtypeStruct((s, D_MODEL), F32),
            compiler_params=_params(("arbitrary",)),
        )(x, gate, y_r, y_s, w_out_g)
    return pl.pallas_call(
        body_loss, name="out_proj_loss", grid=(s // ts,), in_specs=in_specs + [vec, rows],
        out_specs=[rows, pl.BlockSpec((1, 1), lambda i: (0, 0)), vec],
        out_shape=[jax.ShapeDtypeStruct((s, D_MODEL), F32),
                   jax.ShapeDtypeStruct((1, 1), F32),
                   jax.ShapeDtypeStruct((1, D_MODEL), F32)],
        compiler_params=_params(("arbitrary",)),
    )(x, gate, y_r, y_s, w_out_g, *loss_head)


def _out_proj_bwd(dx_out, gate, y_r, y_s, w_out_g, layer):
    s = dx_out.shape[0]
    ts = min(2 * _row_tile(s), s)
    ns = s // ts

    def body(dx_ref, gate_ref, yr_ref, ys_ref, w_ref, dy_ref, dw_ref, dgate_ref):
        i = pl.program_id(0)

        @pl.when(i == 0)
        def _():
            dw_ref[...] = jnp.zeros_like(dw_ref)

        dxv = dx_ref[...]
        dt = _mx(dxv * gate_ref[...])
        dxb = _mx(dxv)
        w_r, w_s = _w_out_halves(w_ref)
        dy_ref[:, 0:512] = _dot_nt(dt, w_r)
        dy_ref[:, 512:1024] = _dot_nt(dt, w_s)
        dw_ref[0:512, :] += _dot_tn(yr_ref[...], dxb)
        dw_ref[512:1024, :] += _dot_tn(ys_ref[...], dxb)

        @pl.when(i == ns - 1)
        def _():
            m_r, m_s = dw_ref[0:512, :], dw_ref[512:1024, :]
            dgate_ref[...] = (jnp.sum(w_r.astype(F32) * m_r, axis=0, keepdims=True)
                              + jnp.sum(w_s.astype(F32) * m_s, axis=0, keepdims=True))
            dw_ref[...] = dw_ref[...] * gate_ref[...]

    return pl.pallas_call(
        body, name="out_proj_bwd", grid=(ns,),
        in_specs=[pl.BlockSpec((ts, D_MODEL), lambda i: (i, 0)),
                  pl.BlockSpec((1, D_MODEL), lambda i: (0, 0)),
                  pl.BlockSpec((ts, 512), lambda i: (i, 0)),
                  pl.BlockSpec((ts, 512), lambda i: (i, 0)),
                  _w_out_spec(w_out_g, layer)],
        out_specs=[pl.BlockSpec((ts, D_MODEL), lambda i: (i, 0)),
                   pl.BlockSpec((D_MODEL, D_MODEL), lambda i: (0, 0)),
                   pl.BlockSpec((1, D_MODEL), lambda i: (0, 0))],
        out_shape=[jax.ShapeDtypeStruct((s, D_MODEL), F32),
                   jax.ShapeDtypeStruct((D_MODEL, D_MODEL), F32),
                   jax.ShapeDtypeStruct((1, D_MODEL), F32)],
        compiler_params=_params(("arbitrary",)),
    )(dx_out, gate, y_r, y_s, w_out_g)


def _scatter_plan(parts_ref, recv_ref, send_sems, recv_sems, local_sem):
    px, py, pc = _my_place()
    mine = _linear(px, py, pc)

    def copy(r):
        peer = (1 - px if r & 4 else px, 1 - py if r & 2 else py, 1 - pc if r & 1 else pc)
        return pltpu.make_async_remote_copy(
            src_ref=parts_ref.at[_linear(*peer)], dst_ref=recv_ref.at[mine],
            send_sem=send_sems.at[r - 1], recv_sem=recv_sems.at[r - 1],
            device_id=peer, device_id_type=MESH_IDS)

    own = pltpu.make_async_copy(parts_ref.at[mine], recv_ref.at[mine], local_sem.at[0])

    def start():
        own.start()
        for r in range(1, N_DEV):
            copy(r).start()

    def finish():
        for r in range(1, N_DEV):
            copy(r).wait_recv()
            copy(r).wait_send()
        own.wait()

    return start, finish


def _in_proj_bwd_x(x, dx_out, dproj, shift, scale1p, g, w_in_g, layer):
    s = x.shape[0]
    ts = _row_tile(s)
    ns = s // ts
    nb = D_IN // N_DEV

    def body(x_ref, dxo_ref, dr_ref, d4_ref, d5_ref, d6_ref, d7_ref, sh_ref, sc_ref, g_ref, w_ref,
             dx_ref, dsh_ref, dsc_ref, dg_ref, ht_ref):
        i = pl.program_id(0)

        @pl.when(i == 0)
        def _():
            dsh_ref[...] = jnp.zeros_like(dsh_ref)
            dsc_ref[...] = jnp.zeros_like(dsc_ref)
            dg_ref[...] = jnp.zeros_like(dg_ref)

        dh = _dot_nt(dr_ref[:, 0:nb], w_ref[0])
        for n in range(1, 4):
            dh += _dot_nt(dr_ref[:, n * nb:(n + 1) * nb], w_ref[n])
        for n, d_ref in zip(range(4, N_DEV), (d4_ref, d5_ref, d6_ref, d7_ref)):
            dh += _dot_nt(d_ref[...], w_ref[n])
        xv = x_ref[...]
        gv = g_ref[...]
        scv = sc_ref[...]
        rstd = lax.rsqrt(jnp.mean(xv * xv, axis=-1, keepdims=True) + EPS)
        xn = xv * rstd
        xg = xn * gv
        ht_ref[...] = _mx((xg * scv + sh_ref[...]).T)
        dsh_ref[...] += jnp.sum(dh, axis=0, keepdims=True)
        dsc_ref[...] += jnp.sum(dh * xg, axis=0, keepdims=True)
        dhs = dh * scv
        dg_ref[...] += jnp.sum(dhs * xn, axis=0, keepdims=True)
        dxn = dhs * gv
        dx_ref[...] = rstd * (dxn - xn * jnp.mean(dxn * xn, axis=-1, keepdims=True)) + dxo_ref[...]

    vec = pl.BlockSpec((1, D_MODEL), lambda i: (0, 0))
    return pl.pallas_call(
        body, name="in_proj_bwd_x", grid=(ns,),
        in_specs=[pl.BlockSpec((ts, D_MODEL), lambda i: (i, 0)),
                  pl.BlockSpec((ts, D_MODEL), lambda i: (i, 0)),
                  pl.BlockSpec((ts, 4 * nb), lambda i: (i, 0))]
                 + [pl.BlockSpec((ts, nb), lambda i: (i, 0))] * 4
                 + [vec, vec, vec, _w_in_spec(w_in_g, layer)],
        out_specs=[pl.BlockSpec((ts, D_MODEL), lambda i: (i, 0)), vec, vec, vec,
                   pl.BlockSpec((D_MODEL, ts), lambda i: (0, i))],
        out_shape=[jax.ShapeDtypeStruct((s, D_MODEL), F32),
                   jax.ShapeDtypeStruct((1, D_MODEL), F32),
                   jax.ShapeDtypeStruct((1, D_MODEL), F32),
                   jax.ShapeDtypeStruct((1, D_MODEL), F32),
                   jax.ShapeDtypeStruct((D_MODEL, s), MXU_DTYPE)],
        compiler_params=_params(("arbitrary",)),
    )(x, dx_out, *dproj, shift, scale1p, g, w_in_g)


def _in_proj_bwd_w(me, h_t, dproj):
    s = h_t.shape[1]
    ts = min(4 * _row_tile(s), s)
    ns = s // ts
    nb = D_IN // N_DEV
    n_chip = N_DEV // 2

    def flip_bits(j):
        return jnp.where(j == 0, 4, jnp.where(j == 1, 2, jnp.where(j == 2, 6, 0)))

    def slab_of(t, me_ref):
        return jnp.bitwise_xor(me_ref[0], flip_bits(t // 2) + 1 - t % 2)

    def body(me_ref, ht_ref, dr_ref, d4_ref, d5_ref, d6_ref, d7_ref, rin_ref,
             acc, stage, pre_buf, pre_send, pre_recv, sum_send, sum_recv, local_sem):
        t = pl.program_id(0)
        i = pl.program_id(1)
        j = t // 2
        summing = t % 2 == 1
        slab = slab_of(t, me_ref)
        px, py, pc = _my_place()

        def pre_copy(jj):
            return pltpu.make_async_remote_copy(
                src_ref=stage.at[jj % 2], dst_ref=pre_buf.at[jj],
                send_sem=pre_send.at[jj], recv_sem=pre_recv.at[jj],
                device_id=(px, py, 1 - pc), device_id_type=MESH_IDS)

        def sum_copy(jj):
            fx = jnp.logical_or(jj == 0, jj == 2)
            fy = jnp.logical_or(jj == 1, jj == 2)
            return pltpu.make_async_remote_copy(
                src_ref=stage.at[2 + jj % 2], dst_ref=rin_ref.at[jj],
                send_sem=sum_send.at[jj], recv_sem=sum_recv.at[jj],
                device_id=(jnp.where(fx, 1 - px, px), jnp.where(fy, 1 - py, py), pc), device_id_type=MESH_IDS)

        own = pltpu.make_async_copy(stage.at[3], rin_ref.at[n_chip - 1], local_sem.at[0])

        @pl.when(i == 0)
        def _():
            acc[...] = jnp.zeros_like(acc)

        @pl.when(slab < 4)
        def _():
            acc[...] += _dot(ht_ref[...], dr_ref[...])

        for n, d_ref in zip(range(4, N_DEV), (d4_ref, d5_ref, d6_ref, d7_ref)):
            @pl.when(slab == n)
            def _():
                acc[...] += _dot(ht_ref[...], d_ref[...])

        @pl.when(jnp.logical_and(i == ns - 1, jnp.logical_not(summing)))
        def _():
            @pl.when(j >= 2)
            def _():
                pre_copy(j - 2).wait_send()

            stage[j % 2] = acc[...].astype(stage.dtype)
            pre_copy(j).start()

        @pl.when(jnp.logical_and(i == ns - 1, summing))
        def _():
            pre_copy(j).wait_recv()

            @pl.when(j >= 2)
            def _():
                sum_copy(j - 2).wait_send()

            stage[2 + j % 2] = (acc[...] + pre_buf[j].astype(F32)).astype(stage.dtype)

            @pl.when(j < n_chip - 1)
            def _():
                sum_copy(j).start()

            @pl.when(j == n_chip - 1)
            def _():
                own.start()
                pre_copy(n_chip - 2).wait_send()
                pre_copy(n_chip - 1).wait_send()
                sum_copy(n_chip - 2).wait_send()
                for jj in range(n_chip - 1):
                    sum_copy(jj).wait_recv()
                own.wait()

    def part_rows(n, t, i, me_ref):
        return jnp.where(slab_of(t, me_ref) == n, i, ns - 1), 0

    return pl.pallas_call(
        body, name="in_proj_bwd_w",
        grid_spec=pltpu.PrefetchScalarGridSpec(
            num_scalar_prefetch=1, grid=(N_DEV, ns),
            in_specs=[pl.BlockSpec((D_MODEL, ts), lambda t, i, me_ref: (0, i)),
                      pl.BlockSpec((ts, nb), lambda t, i, me_ref: (
                          jnp.where(slab_of(t, me_ref) < 4, i, ns - 1), jnp.minimum(slab_of(t, me_ref), 3)))]
                     + [pl.BlockSpec((ts, nb), functools.partial(part_rows, n)) for n in range(4, N_DEV)],
            out_specs=HBM_SPEC,
            scratch_shapes=[pltpu.VMEM((D_MODEL, nb), F32),
                            pltpu.VMEM((4, D_MODEL, nb), MXU_DTYPE),
                            pltpu.VMEM((n_chip, D_MODEL, nb), MXU_DTYPE),
                            pltpu.SemaphoreType.DMA((n_chip,)), pltpu.SemaphoreType.DMA((n_chip,)),
                            pltpu.SemaphoreType.DMA((n_chip - 1,)), pltpu.SemaphoreType.DMA((n_chip - 1,)),
                            pltpu.SemaphoreType.DMA((1,))]),
        out_shape=jax.ShapeDtypeStruct((n_chip, D_MODEL, nb), MXU_DTYPE),
        compiler_params=_params(("arbitrary", "arbitrary")),
    )(jnp.reshape(me, (1,)).astype(jnp.int32), h_t, *dproj)


RET_TILE = 256


def _ret_tables(s):
    t = min(RET_TILE, s)
    half = RET_HEAD_DIM // 2
    pos = jnp.arange(s, dtype=F32)
    inv = ROPE_BASE ** (-jnp.arange(half, dtype=F32) / half)
    ang = pos[:, None] * inv[None, :]
    cos, sin = jnp.cos(ang), jnp.sin(ang)
    cos2 = jnp.concatenate([cos, cos], axis=1)
    sin2 = jnp.concatenate([-sin, sin], axis=1)
    lg = jnp.log1p(-(2.0 ** (-5.0 - jnp.arange(RET_HEADS, dtype=F32))))[:, None, None]
    n = jnp.arange(t)
    dist = (n[:, None] - n[None, :]).astype(F32)[None]
    cn = (n // CHUNK)[:, None]
    cm = (n // CHUNK)[None, :]
    mask = jnp.where((cn == cm)[None], jnp.exp(jnp.abs(dist) * lg),
                     jnp.where((cm < cn)[None], jnp.exp(dist * lg), 0.0))
    nf = n.astype(F32)[None, :, None]
    dq = jnp.broadcast_to(jnp.exp((nf + 1.0) * lg), (RET_HEADS, t, LANES))
    dk = jnp.broadcast_to(jnp.exp((t - 1.0 - nf) * lg), (RET_HEADS, t, LANES))
    gt = jnp.broadcast_to(jnp.exp(float(t) * lg), (RET_HEADS, 1, LANES))
    return cos2, sin2, mask, dq, dk, gt


def _roll_half(x):
    return pltpu.roll(x, RET_HEAD_DIM // 2, 1)


def _ret_heads_fwd(ret_ref, cos, sin, m_ref, dq_ref, dk_ref, s0):
    hd = RET_HEAD_DIM
    heads = range(RET_HEADS)
    qb, kb, vb, kdb = [], [], [], []
    for h in heads:
        q = ret_ref[:, h * hd:(h + 1) * hd]
        k = ret_ref[:, 512 + h * hd:512 + (h + 1) * hd]
        kr = (k * cos + _roll_half(k) * sin) * RET_KSCALE
        qb.append(_mx(q * cos + _roll_half(q) * sin))
        kb.append(_mx(kr))
        kdb.append(_mx(kr * dk_ref[h]))
        vb.append(_mx(ret_ref[:, 1024 + h * hd:1024 + (h + 1) * hd]))
    p = [_dot_nt(qb[h], kb[h]) for h in heads]
    cross = [_dot(qb[h], _mx(s0[h])) for h in heads]
    pb = [_mx(p[h] * m_ref[h]) for h in heads]
    o = [_dot(pb[h], vb[h]) + cross[h] * dq_ref[h] for h in heads]
    gn, rstd = [], []
    for h in heads:
        oc = o[h] - jnp.mean(o[h], axis=-1, keepdims=True)
        rstd.append(lax.rsqrt(jnp.mean(oc * oc, axis=-1, keepdims=True) + EPS))
        gn.append(oc * rstd[h])
    return qb, kb, vb, pb, kdb, gn, rstd


def _retention_fwd(ret, tables):
    cos2, sin2, mask, dq, dk, gt = tables
    s = ret.shape[0]
    t = mask.shape[1]
    nt = s // t
    hd = RET_HEAD_DIM

    def body(ret_ref, cos_ref, sin_ref, m_ref, dq_ref, dk_ref, gt_ref, y_ref, st_ref, s_scr):
        i = pl.program_id(0)

        @pl.when(i == 0)
        def _():
            s_scr[...] = jnp.zeros_like(s_scr)

        s0 = [s_scr[h] for h in range(RET_HEADS)]
        _, _, vb, _, kdb, gn, _ = _ret_heads_fwd(ret_ref, cos_ref[...], sin_ref[...], m_ref, dq_ref, dk_ref, s0)
        kv = [_dot_tn(kdb[h], vb[h]) for h in range(RET_HEADS)]
        for h in range(RET_HEADS):
            g = ret_ref[:, 1536 + h * hd:1536 + (h + 1) * hd]
            st_ref[h] = s0[h]
            y_ref[:, h * hd:(h + 1) * hd] = (gn[h] * (g * _sigmoid(g))).astype(y_ref.dtype)
            s_scr[h] = s0[h] * gt_ref[h] + kv[h]

    full3 = lambda a: pl.BlockSpec(a.shape, lambda i: (0, 0, 0))
    return pl.pallas_call(
        body, name="retention_fwd", grid=(nt,),
        in_specs=[pl.BlockSpec((t, 2048), lambda i: (i, 0)),
                  pl.BlockSpec((t, LANES), lambda i: (i, 0)),
                  pl.BlockSpec((t, LANES), lambda i: (i, 0)),
                  full3(mask), full3(dq), full3(dk), full3(gt)],
        out_specs=[pl.BlockSpec((t, 512), lambda i: (i, 0)),
                   pl.BlockSpec((None, RET_HEADS, hd, hd), lambda i: (i, 0, 0, 0))],
        out_shape=[jax.ShapeDtypeStruct((s, 512), MXU_DTYPE),
                   jax.ShapeDtypeStruct((nt, RET_HEADS, hd, hd), F32)],
        scratch_shapes=[pltpu.VMEM((RET_HEADS, hd, hd), F32)],
        compiler_params=_params(("arbitrary",)),
    )(ret, cos2, sin2, mask, dq, dk, gt)


def _retention_bwd(ret, states, dy, tables, dwo_parts):
    cos2, sin2, mask, dq, dk, gt = tables
    s = ret.shape[0]
    t = mask.shape[1]
    nt = s // t
    hd = RET_HEAD_DIM

    def body(ret_ref, st_ref, dy_ref, cos_ref, sin_ref, m_ref, dq_ref, dk_ref, gt_ref, dwo_ref,
             d_ref, rout_ref, ds_scr, send_sems, recv_sems, local_sem):
        i = pl.program_id(0)
        start, finish = _scatter_plan(dwo_ref, rout_ref, send_sems, recv_sems, local_sem)

        @pl.when(i == 0)
        def _():
            start()
            ds_scr[...] = jnp.zeros_like(ds_scr)

        cos = cos_ref[...]
        sin = sin_ref[...]
        heads = range(RET_HEADS)
        s0 = [st_ref[h] for h in heads]
        ds = [ds_scr[h] for h in heads]
        dsb = [_mx(ds[h]) for h in heads]
        qb, kb, vb, pb, kdb, gn, rstd = _ret_heads_fwd(ret_ref, cos, sin, m_ref, dq_ref, dk_ref, s0)
        dob, dodb = [], []
        for h in heads:
            g = ret_ref[:, 1536 + h * hd:1536 + (h + 1) * hd]
            dyv = dy_ref[:, h * hd:(h + 1) * hd]
            sg = _sigmoid(g)
            d_ref[:, 1536 + h * hd:1536 + (h + 1) * hd] = (
                dyv * gn[h] * (sg * (1.0 + g * (1.0 - sg)))).astype(d_ref.dtype)
            dgn = dyv * (g * sg)
            do = rstd[h] * (dgn - jnp.mean(dgn, axis=-1, keepdims=True)
                            - gn[h] * jnp.mean(dgn * gn[h], axis=-1, keepdims=True))
            dob.append(_mx(do))
            dodb.append(_mx(do * dq_ref[h]))
        dp = [_dot_nt(dob[h], vb[h]) for h in heads]
        dv = [_dot_tn(pb[h], dob[h]) + _dot(kdb[h], dsb[h]) for h in heads]
        dq_cross = [_dot_nt(dodb[h], _mx(s0[h])) for h in heads]
        dk_cross = [_dot_nt(vb[h], dsb[h]) for h in heads]
        ds_new = [_dot_tn(qb[h], dodb[h]) for h in heads]
        dpb = [_mx(dp[h] * m_ref[h]) for h in heads]
        dqr = [_dot(dpb[h], kb[h]) + dq_cross[h] for h in heads]
        dkr = [(_dot_tn(dpb[h], qb[h]) + dk_cross[h] * dk_ref[h]) * RET_KSCALE for h in heads]
        for h in heads:
            d_ref[:, 1024 + h * hd:1024 + (h + 1) * hd] = dv[h].astype(d_ref.dtype)
            d_ref[:, h * hd:(h + 1) * hd] = (dqr[h] * cos + _roll_half(dqr[h] * sin)).astype(d_ref.dtype)
            d_ref[:, 512 + h * hd:512 + (h + 1) * hd] = (
                dkr[h] * cos + _roll_half(dkr[h] * sin)).astype(d_ref.dtype)
            ds_scr[h] = ds[h] * gt_ref[h] + ds_new[h]
        pl.when(i == nt - 1)(finish)

    full3 = lambda a: pl.BlockSpec(a.shape, lambda i: (0, 0, 0))
    rev = lambda i: (nt - 1 - i, 0)
    return pl.pallas_call(
        body, name="retention_bwd", grid=(nt,),
        in_specs=[pl.BlockSpec((t, 2048), rev),
                  pl.BlockSpec((None, RET_HEADS, hd, hd), lambda i: (nt - 1 - i, 0, 0, 0)),
                  pl.BlockSpec((t, 512), rev),
                  pl.BlockSpec((t, LANES), rev),
                  pl.BlockSpec((t, LANES), rev),
                  full3(mask), full3(dq), full3(dk), full3(gt), HBM_SPEC],
        out_specs=[pl.BlockSpec((t, 2048), rev), HBM_SPEC],
        out_shape=[jax.ShapeDtypeStruct((s, 2048), MXU_DTYPE),
                   jax.ShapeDtypeStruct(dwo_parts.shape, dwo_parts.dtype)],
        scratch_shapes=[pltpu.VMEM((RET_HEADS, hd, hd), F32),
                        pltpu.SemaphoreType.DMA((N_PEERS,)), pltpu.SemaphoreType.DMA((N_PEERS,)),
                        pltpu.SemaphoreType.DMA((1,))],
        compiler_params=_params(("arbitrary",)),
    )(ret, states, dy, cos2, sin2, mask, dq, dk, gt, dwo_parts)


SB_BLOCK = 256
SB_CHUNK = 32


SB_SKIP = 104.0
NO_KEYS = -1e30


def _split_dots(xs, u):
    parts = []
    for x in xs:
        hi = lax.bitcast_convert_type(lax.bitcast_convert_type(x, jnp.uint32) & jnp.uint32(0xFFFF0000), F32)
        parts += [_mx(hi), _mx(x - hi)]
    out = _dot(jnp.concatenate(parts, axis=0), u)
    n = xs[0].shape[0]
    return [out[2 * k * n:(2 * k + 1) * n] + out[(2 * k + 1) * n:(2 * k + 2) * n] for k in range(len(xs))]


def _split_dot(x, u):
    return _split_dots([x], u)[0]


def _sb_pair_weights(lb, lk, allowed, u_gt):
    blk = lb.shape[0]
    lk_p, lk_d = lk[:, :blk], lk[:, blk:]
    r_d = _rowsum(lk_d)
    cs_p, cs_d = _split_dots([lk_p, lk_d], u_gt)
    a = jnp.exp(lb + jnp.concatenate([cs_p + r_d, cs_d], axis=1))
    return jnp.where(allowed, a, 0.0), r_d, r_d + _rowsum(lk_p)


def _sb_logits(q, k, causal):
    z = _dot_nt(q, k)
    l1p = jnp.log(1.0 + jnp.exp(-jnp.abs(z)))
    lk = -(jnp.maximum(z, 0.0) + l1p)
    if causal is not None:
        lk = jnp.where(causal, lk, 0.0)
    return jnp.minimum(z, 0.0) - l1p, lk


def _sb_weights(lb, lk, r, u_gt, causal):
    a = jnp.exp(lb + _split_dot(lk, u_gt) + r)
    return a if causal is None else jnp.where(causal, a, 0.0)


def _rowsum(x):
    return jnp.sum(x, axis=1, keepdims=True)


def _sb_pair_tile(i, blk):
    row = lax.broadcasted_iota(jnp.int32, (blk, 2 * blk), 0)
    col = lax.broadcasted_iota(jnp.int32, (blk, 2 * blk), 1)
    first_col = jnp.where(i >= 1, 0, blk)
    allowed = jnp.logical_and(row > col - blk, col >= first_col)
    rows_p = pl.ds(pl.multiple_of(jnp.maximum(i - 1, 0) * blk, blk), blk)
    rows_d = pl.ds(pl.multiple_of(i * blk, blk), blk)
    return allowed, rows_p, rows_d


def _sb_fwd(qkv, sg):
    s = qkv.shape[0]
    blk = min(SB_BLOCK, s)
    nq = s // blk
    hd = SB_HEAD_DIM

    ch = min(SB_CHUNK, blk)

    def body(q_ref, k_ref, v_ref, g_ref, y_ref, o_ref, z_scr, lhs_scr, cs_scr, a_scr):
        i = pl.program_id(1)
        row = lax.broadcasted_iota(jnp.int32, (blk, blk), 0)
        col = lax.broadcasted_iota(jnp.int32, (blk, blk), 1)
        u_gt = (row > col).astype(MXU_DTYPE)
        heads = [slice(hh * hd, (hh + 1) * hd) for hh in range(2)]
        qs = [q_ref[:, ls] for ls in heads]
        _, rows_p, rows_d = _sb_pair_tile(i, blk)
        has_prev = i >= 1
        crow = lax.broadcasted_iota(jnp.int32, (ch, blk), 0)
        ccol = lax.broadcasted_iota(jnp.int32, (ch, blk), 1)

        def logits(hh):
            kc = jnp.concatenate([k_ref[rows_p, heads[hh]], k_ref[rows_d, heads[hh]]], axis=0)
            z_scr[hh] = _dot_nt(qs[hh], kc)

        def keep_parts(hh):
            r_d, r_all = [], []
            for c in range(blk // ch):
                rows = pl.ds(c * ch, ch)
                causal = crow + c * ch > ccol
                z = z_scr[hh, rows, :]
                l1p = jnp.log(1.0 + jnp.exp(-jnp.abs(z)))
                lb = jnp.minimum(z, 0.0) - l1p
                z_scr[hh, rows, :] = lb
                lk = lb - z
                lk_p = lk[:, :blk]
                lk_d = jnp.where(causal, lk[:, blk:], 0.0)
                lhs_scr[hh, pl.ds(c * ch, ch), :] = _mx(lk_p)
                lhs_scr[hh, pl.ds(blk + c * ch, ch), :] = _mx(lk_d)
                r_d.append(_rowsum(lk_d))
                r_all.append(r_d[c] + _rowsum(lk_p))
            return r_d, jnp.concatenate(r_all, axis=0)

        def suffix_sums(hh):
            cs_scr[hh] = _dot(lhs_scr[hh], u_gt)

        def weights(hh, r_d):
            for c in range(blk // ch):
                rows = pl.ds(c * ch, ch)
                causal = crow + c * ch > ccol
                cs_p = cs_scr[hh, pl.ds(c * ch, ch), :] + jnp.where(has_prev, r_d[c], NO_KEYS)
                cs_d = cs_scr[hh, pl.ds(blk + c * ch, ch), :]
                lb = z_scr[hh, rows, :]
                a_p = jnp.exp(lb[:, :blk] + cs_p)
                a_d = jnp.where(causal, jnp.exp(lb[:, blk:] + cs_d), 0.0)
                a_scr[hh, rows, :] = _mx(jnp.concatenate([a_p, a_d], axis=1))

        def values(hh):
            vc = jnp.concatenate([v_ref[rows_p, heads[hh]], v_ref[rows_d, heads[hh]]], axis=0)
            return _dot(a_scr[hh], vc)

        def block(hh, j, r):
            start = pl.multiple_of(j * blk, blk)
            lb, lk = _sb_logits(qs[hh], k_ref[pl.ds(start, blk), heads[hh]], None)
            a = _sb_weights(lb, lk, r, u_gt, None)
            return _dot(_mx(a), v_ref[pl.ds(start, blk), heads[hh]]), r + _rowsum(lk)

        def more(n, r0, r1):
            return jnp.logical_and(n < i, jnp.max(jnp.maximum(r0, r1)) > -SB_SKIP)

        logits(0)
        logits(1)
        gv = g_ref[...]
        gates = gv * _sigmoid(gv)
        rd0, r0 = keep_parts(0)
        suffix_sums(0)
        rd1, r1 = keep_parts(1)
        suffix_sums(1)
        go = more(jnp.int32(1), r0, r1)
        weights(0, rd0)
        acc0 = values(0)
        weights(1, rd1)
        acc1 = values(1)

        def step(c):
            _, n, acc0, r0, acc1, r1 = c
            pv0, r0 = block(0, i - 1 - n, r0)
            pv1, r1 = block(1, i - 1 - n, r1)
            return more(n + 1, r0, r1), n + 1, acc0 + pv0, r0, acc1 + pv1, r1

        _, _, acc0, _, acc1, _ = lax.while_loop(lambda c: c[0], step, (go, jnp.int32(1), acc0, r0, acc1, r1))
        o = jnp.concatenate([acc0, acc1], axis=1)
        o_ref[...] = o
        y_ref[...] = (o * gates).astype(y_ref.dtype)

    qblk = pl.BlockSpec((blk, LANES), lambda p, i: (i, p))
    return pl.pallas_call(
        body, name="stickbreak_fwd", grid=(SB_HEADS // 2, nq),
        in_specs=[qblk,
                  pl.BlockSpec((s, LANES), lambda p, i: (0, 4 + p)),
                  pl.BlockSpec((s, LANES), lambda p, i: (0, 8 + p)),
                  qblk],
        out_specs=[qblk, qblk],
        out_shape=[jax.ShapeDtypeStruct((s, 512), MXU_DTYPE),
                   jax.ShapeDtypeStruct((s, 512), F32)],
        scratch_shapes=[pltpu.VMEM((2, blk, 2 * blk), F32),
                        pltpu.VMEM((2, 2 * blk, blk), MXU_DTYPE),
                        pltpu.VMEM((2, 2 * blk, blk), F32),
                        pltpu.VMEM((2, blk, 2 * blk), MXU_DTYPE)],
        compiler_params=_params(("arbitrary", "arbitrary")),
    )(qkv, qkv, qkv, sg)


def _sb_bwd(qkv, sg, o, dy):
    s = qkv.shape[0]
    blk = min(SB_BLOCK, s)
    nq = s // blk
    hd = SB_HEAD_DIM
    assert nq <= LANES
    ch = min(SB_CHUNK, blk)

    def body(q_ref, k_ref, v_ref, g_ref, o_ref, dy_ref, dq_ref, dk_ref, dv_ref, dg_ref, dk_scr, dv_scr,
             z_scr, g_scr, lhs_scr, cs_scr, a_scr, dz_scr):
        i = pl.program_id(1)

        @pl.when(i == 0)
        def _():
            dk_scr[...] = jnp.zeros_like(dk_scr)
            dv_scr[...] = jnp.zeros_like(dv_scr)

        row = lax.broadcasted_iota(jnp.int32, (blk, blk), 0)
        col = lax.broadcasted_iota(jnp.int32, (blk, blk), 1)
        lane = lax.broadcasted_iota(jnp.int32, (blk, LANES), 1)
        u_gt = (row > col).astype(MXU_DTYPE)
        u_lt = (row < col).astype(MXU_DTYPE)
        heads = [slice(hh * hd, (hh + 1) * hd) for hh in range(2)]
        qs = [q_ref[:, ls] for ls in heads]
        _, rows_p, rows_d = _sb_pair_tile(i, blk)
        has_prev = i >= 1
        crow = lax.broadcasted_iota(jnp.int32, (ch, blk), 0)
        ccol = lax.broadcasted_iota(jnp.int32, (ch, blk), 1)
        nch = blk // ch
        kcs = [jnp.concatenate([k_ref[rows_p, ls], k_ref[rows_d, ls]], axis=0) for ls in heads]
        dobs = []

        def gate_grads():
            g = g_ref[...]
            dyv = dy_ref[...]
            sgm = _sigmoid(g)
            dg_ref[...] = (dyv * o_ref[...] * (sgm * (1.0 + g * (1.0 - sgm)))).astype(dg_ref.dtype)
            dob = _mx(dyv * (g * sgm))
            dobs.extend(dob[:, ls] for ls in heads)

        def split_rows(hh, c, part, x):
            lhs_scr[hh, pl.ds(part * blk + c * ch, ch), :] = _mx(x)

        def summed_rows(hh, c, part):
            return cs_scr[hh, pl.ds(part * blk + c * ch, ch), :]

        def logits(hh):
            z_scr[hh] = _dot_nt(qs[hh], kcs[hh])

        def weight_grads(hh):
            vc = jnp.concatenate([v_ref[rows_p, heads[hh]], v_ref[rows_d, heads[hh]]], axis=0)
            g_scr[hh] = _dot_nt(dobs[hh], vc)

        def keep_parts(hh):
            r_d, r_all = [], []
            for c in range(nch):
                rows = pl.ds(c * ch, ch)
                z = z_scr[hh, rows, :]
                l1p = jnp.log(1.0 + jnp.exp(-jnp.abs(z)))
                lb = jnp.minimum(z, 0.0) - l1p
                z_scr[hh, rows, :] = lb
                lk = lb - z
                lk_p = lk[:, :blk]
                lk_d = jnp.where(crow + c * ch > ccol, lk[:, blk:], 0.0)
                split_rows(hh, c, 0, lk_p)
                split_rows(hh, c, 1, lk_d)
                r_d.append(_rowsum(lk_d))
                r_all.append(r_d[c] + _rowsum(lk_p))
            return r_d, jnp.concatenate(r_all, axis=0)

        def weights(hh, r_d):
            g_p = []
            for c in range(nch):
                rows = pl.ds(c * ch, ch)
                lb = z_scr[hh, rows, :]
                a_p = jnp.exp(lb[:, :blk] + (summed_rows(hh, c, 0) + jnp.where(has_prev, r_d[c], NO_KEYS)))
                a_d = jnp.where(crow + c * ch > ccol, jnp.exp(lb[:, blk:] + summed_rows(hh, c, 1)), 0.0)
                a = jnp.concatenate([a_p, a_d], axis=1)
                a_scr[hh, rows, :] = _mx(a)
                gm = g_scr[hh, rows, :] * a
                g_scr[hh, rows, :] = gm
                split_rows(hh, c, 0, gm[:, :blk])
                split_rows(hh, c, 1, gm[:, blk:])
                g_p.append(_rowsum(gm[:, :blk]))
            return g_p

        def logit_grads(hh, pg, g_p):
            for c in range(nch):
                rows = pl.ds(c * ch, ch)
                pre = jnp.concatenate([summed_rows(hh, c, 0) + pg[c * ch:(c + 1) * ch],
                                       summed_rows(hh, c, 1) + (pg[c * ch:(c + 1) * ch] + g_p[c])], axis=1)
                gm = g_scr[hh, rows, :]
                dz = gm - (gm + pre) * jnp.exp(z_scr[hh, rows, :])
                dz_p = dz[:, :blk]
                dz_d = jnp.where(crow + c * ch > ccol, dz[:, blk:], 0.0)
                dz_scr[hh, rows, :] = _mx(jnp.concatenate([dz_p, dz_d], axis=1))

        def products(hh, acc):
            ls = heads[hh]
            dk_scr[hh, rows_p, :] += _dot_tn(dz_scr[hh, :, 0:blk], qs[hh])
            dk_scr[hh, rows_d, :] += _dot_tn(dz_scr[hh, :, blk:2 * blk], qs[hh])
            dv_scr[hh, rows_p, :] += _dot_tn(a_scr[hh, :, 0:blk], dobs[hh])
            dv_scr[hh, rows_d, :] += _dot_tn(a_scr[hh, :, blk:2 * blk], dobs[hh])
            dq_ref[:, ls] = ((acc + _dot(dz_scr[hh], kcs[hh])) * SB_SCALE).astype(dq_ref.dtype)

        def suffix_sums(hh):
            cs_scr[hh] = _dot(lhs_scr[hh], u_gt)

        def prefix_sums(hh):
            cs_scr[hh] = _dot(lhs_scr[hh], u_lt)

        def more(n, r0, r1):
            return jnp.logical_and(n < i, jnp.max(jnp.maximum(r0, r1)) > -SB_SKIP)

        gate_grads()
        logits(0)
        weight_grads(0)
        logits(1)
        weight_grads(1)
        rd0, ra0 = keep_parts(0)
        suffix_sums(0)
        rd1, ra1 = keep_parts(1)
        suffix_sums(1)
        go = more(jnp.int32(1), ra0, ra1)
        gp0 = weights(0, rd0)
        prefix_sums(0)
        gp1 = weights(1, rd1)
        prefix_sums(1)

        def scan_block(hh, j, r, rmat):
            start = pl.multiple_of(j * blk, blk)
            _, lk = _sb_logits(qs[hh], k_ref[pl.ds(start, blk), heads[hh]], None)
            return r + _rowsum(lk), jnp.where(lane == j, r, rmat)

        def scan_step(c):
            _, n, r0, rmat0, r1, rmat1 = c
            r0, rmat0 = scan_block(0, i - 1 - n, r0, rmat0)
            r1, rmat1 = scan_block(1, i - 1 - n, r1, rmat1)
            return more(n + 1, r0, r1), n + 1, r0, rmat0, r1, rmat1

        zmat = jnp.zeros((blk, LANES), F32)
        _, n, _, rmat0, _, rmat1 = lax.while_loop(lambda c: c[0], scan_step,
                                                  (go, jnp.int32(1), ra0, zmat, ra1, zmat))
        rmats = (rmat0, rmat1)

        def block(hh, j, pg):
            ls = heads[hh]
            start = pl.multiple_of(j * blk, blk)
            k = k_ref[pl.ds(start, blk), ls]
            lb, lk = _sb_logits(qs[hh], k, None)
            r = _rowsum(jnp.where(lane == j, rmats[hh], 0.0))
            a = _sb_weights(lb, lk, r, u_gt, None)
            gm = _dot_nt(dobs[hh], v_ref[pl.ds(start, blk), ls]) * a
            dzb = _mx(gm - (gm + (pg + _split_dot(gm, u_lt))) * jnp.exp(lb))
            dk_scr[hh, pl.ds(start, blk), :] += _dot_tn(dzb, qs[hh])
            dv_scr[hh, pl.ds(start, blk), :] += _dot_tn(_mx(a), dobs[hh])
            return _dot(dzb, k), pg + _rowsum(gm)

        def step(t, c):
            acc0, pg0, acc1, pg1 = c
            dq0, pg0 = block(0, i - n + t, pg0)
            dq1, pg1 = block(1, i - n + t, pg1)
            return acc0 + dq0, pg0, acc1 + dq1, pg1

        zero = jnp.zeros((blk, 1), F32)
        zacc = jnp.zeros((blk, hd), F32)
        acc0, pg0, acc1, pg1 = lax.fori_loop(0, n - 1, step, (zacc, zero, zacc, zero))
        logit_grads(0, pg0, gp0)
        logit_grads(1, pg1, gp1)
        products(0, acc0)
        products(1, acc1)

        @pl.when(i == nq - 1)
        def _():
            for hh in range(2):
                ls = slice(hh * hd, (hh + 1) * hd)
                dk_ref[:, ls] = dk_scr[hh].astype(dk_ref.dtype)
                dv_ref[:, ls] = dv_scr[hh].astype(dv_ref.dtype)

    qblk = lambda c0: pl.BlockSpec((blk, LANES), lambda p, i: (i, c0 + p))
    full = lambda c0: pl.BlockSpec((s, LANES), lambda p, i: (0, c0 + p))
    half = jax.ShapeDtypeStruct((s, 512), MXU_DTYPE)
    return pl.pallas_call(
        body, name="stickbreak_bwd", grid=(SB_HEADS // 2, nq),
        in_specs=[qblk(0), full(4), full(8), qblk(0), qblk(0), qblk(4)],
        out_specs=[qblk(0), full(0), full(0), qblk(0)],
        out_shape=[half, half, half, half],
        scratch_shapes=[pltpu.VMEM((2, s, hd), F32), pltpu.VMEM((2, s, hd), F32),
                        pltpu.VMEM((2, blk, 2 * blk), F32),
                        pltpu.VMEM((2, blk, 2 * blk), F32),
                        pltpu.VMEM((2, 2 * blk, blk), MXU_DTYPE),
                        pltpu.VMEM((2, 2 * blk, blk), F32),
                        pltpu.VMEM((2, blk, 2 * blk), MXU_DTYPE),
                        pltpu.VMEM((2, blk, 2 * blk), MXU_DTYPE)],
        compiler_params=_params(("arbitrary", "arbitrary")),
    )(qkv, qkv, qkv, sg, o, dy)


def _layer_fwd(layer, x, mod, norm_g, w_in_g, w_out_g, tables, gather=(), loss_head=None):
    shift, scale1p, gate = mod[0:1], 1.0 + mod[1:2], mod[2:3]
    ret, qkv, sg, *gathered = _ln_proj(x, shift, scale1p, norm_g, w_in_g, layer, gather)
    y_r, states = _retention_fwd(ret, tables)
    y_s, o_s = _sb_fwd(qkv, sg)
    x_next = _out_proj(x, gate, y_r, y_s, w_out_g, layer, loss_head)
    saved = (x, shift, scale1p, gate, ret, qkv, sg, y_r, states, y_s, o_s)
    return x_next, saved, gathered


def _layer_bwd(layer, me, dx_out, saved, norm_g, w_in_g, w_out_g, tables):
    x, shift, scale1p, gate, ret, qkv, sg, y_r, states, y_s, o_s = saved
    dy, dw_out, dgate = _out_proj_bwd(dx_out, gate, y_r, y_s, w_out_g, layer)
    dwo_parts = _mx(dw_out.reshape(N_DEV, D_MODEL // N_DEV, D_MODEL))
    d_ret, r_out = _retention_bwd(ret, states, dy, tables, dwo_parts)
    d_q, d_k, d_v, d_g = _sb_bwd(qkv, sg, o_s, dy)
    dproj = (d_ret, d_q, d_k, d_v, d_g)
    dx, dshift, dscale, dnorm_g, h_t = _in_proj_bwd_x(x, dx_out, dproj, shift, scale1p, norm_g, w_in_g, layer)
    r_in = _in_proj_bwd_w(me, h_t, dproj)
    dmod = jnp.concatenate([dshift, dscale, dgate], axis=1)
    return dx, r_in, r_out, dmod, dnorm_g


MESH_IDS = pl.DeviceIdType.MESH
N_PEERS = N_DEV - 1
HBM_SPEC = pl.BlockSpec(memory_space=pl.ANY)


def _my_place():
    return lax.axis_index("x"), lax.axis_index("y"), lax.axis_index("c")


def _linear(px, py, pc):
    return 4 * px + 2 * py + pc


def _all_gather(blocks):
    n_arr = len(blocks)

    def body(*refs):
        start, forward, finish = _gather_plan(refs[:n_arr], refs[n_arr:2 * n_arr], *refs[2 * n_arr:])
        start()
        forward()
        finish()

    return pl.pallas_call(
        body, name="all_gather",
        out_shape=_gathered_shapes(blocks),
        in_specs=[HBM_SPEC] * n_arr, out_specs=[HBM_SPEC] * n_arr,
        scratch_shapes=_gather_sems(n_arr),
    )(*blocks)


def _gathered_shapes(blocks):
    return [jax.ShapeDtypeStruct((N_DEV * b.shape[0], b.shape[1]), b.dtype) for b in blocks]


def _gather_sems(n_arr):
    return [pltpu.SemaphoreType.DMA((n_arr * N_PEERS,)), pltpu.SemaphoreType.DMA((n_arr * N_PEERS,)),
            pltpu.SemaphoreType.DMA((n_arr,))]


def _gather_plan(x_refs, out_refs, send_sems, recv_sems, local_sems):
    n_arr = len(x_refs)
    x, y, c = _my_place()
    me, sibling = (x, y, c), (x, y, 1 - c)
    chips = [(1 - x, y), (x, 1 - y), (1 - x, 1 - y)]

    def rows(a, place):
        m = x_refs[a].shape[0]
        return out_refs[a].at[pl.ds(_linear(*place) * m, m), :]

    def copy(a, k, block, to, src=None):
        return pltpu.make_async_remote_copy(
            src_ref=rows(a, block) if src is None else src, dst_ref=rows(a, block),
            send_sem=send_sems.at[a * N_PEERS + k], recv_sem=recv_sems.at[a * N_PEERS + k],
            device_id=to, device_id_type=MESH_IDS)

    mine = [pltpu.make_async_copy(x_refs[a], rows(a, me), local_sems.at[a]) for a in range(n_arr)]
    first = []
    for a in range(n_arr):
        first.append(copy(a, 0, me, sibling, src=x_refs[a]))
        first += [copy(a, 1 + j, me, (*chip, c), src=x_refs[a]) for j, chip in enumerate(chips)]
    passed = [copy(a, 4 + j, (*chip, c), sibling) for j, chip in enumerate(chips) for a in range(n_arr)]

    def start():
        for cp in mine + first:
            cp.start()

    def forward():
        for j, chip in enumerate(chips):
            for a in range(n_arr):
                copy(a, 1 + j, (*chip, c), me).wait_recv()
                passed[j * n_arr + a].start()

    def finish():
        for a in range(n_arr):
            copy(a, 0, sibling, me).wait_recv()
            for j, chip in enumerate(chips):
                copy(a, 4 + j, (*chip, 1 - c), me).wait_recv()
        for cp in first + passed:
            cp.wait_send()
        for cp in mine:
            cp.wait()

    return start, forward, finish


def _ada_fwd(c_all, w_ada, b_cols):
    cols = w_ada.shape[2]

    def body(c_ref, w_ref, b_ref, ca_ref, mod_ref):
        cv = c_ref[...]
        ca = cv * _sigmoid(cv)
        ca_ref[...] = ca
        cb = _mx(jnp.concatenate([ca, ca], axis=0))
        for l in range(DEPTH):
            mod_ref[l * N_DEV:(l + 1) * N_DEV, :] = _dot(cb, _mx(w_ref[l]))[0:N_DEV] + b_ref[l]

    return pl.pallas_call(
        body, name="ada_fwd",
        out_shape=[jax.ShapeDtypeStruct((N_DEV, D_MODEL), F32),
                   jax.ShapeDtypeStruct((DEPTH * N_DEV, cols), F32)],
        compiler_params=_params(),
    )(c_all, w_ada, b_cols)


def _ada_bwd(c_act_t, dmod_cols):
    cols = dmod_cols.shape[2]

    def body(ca_ref, dm_ref, o_ref):
        ca = _mx(ca_ref[...]).astype(F32)
        for l in range(DEPTH):
            o_ref[l] = jnp.dot(ca, _mx(dm_ref[l]).astype(F32),
                               precision=lax.Precision.HIGHEST, preferred_element_type=F32)

    return pl.pallas_call(
        body, name="ada_bwd",
        out_shape=jax.ShapeDtypeStruct((DEPTH, D_MODEL, cols), F32),
        compiler_params=_params(),
    )(c_act_t, dmod_cols)


def _adamw_store(g, w_ref, m_ref, v_ref, g_ref, d_ref, mo_ref, vo_ref):
    m2 = ADAM_B1 * m_ref[...] + (1.0 - ADAM_B1) * g
    v2 = ADAM_B2 * v_ref[...] + (1.0 - ADAM_B2) * (g * g)
    m_hat = m2 / (1.0 - ADAM_B1 ** ADAM_STEP)
    v_hat = v2 / (1.0 - ADAM_B2 ** ADAM_STEP)
    g_ref[...] = g
    d_ref[...] = -ADAM_LR * (m_hat / (jnp.sqrt(v_hat) + ADAM_EPS) + ADAM_WD * w_ref[...])
    mo_ref[...] = m2
    vo_ref[...] = v2


def _slab_sum(p_ref):
    g = p_ref[0].astype(F32)
    for sl in range(1, p_ref.shape[0]):
        g = g + p_ref[sl].astype(F32)
    return g


def _sum_adamw_layers(parts, w, m, v):
    n_slab, rows, cols = parts[0].shape
    tr = min(256, rows)
    nt = rows // tr

    def body(p0_ref, p1_ref, w_ref, m_ref, v_ref, g_ref, d_ref, mo_ref, vo_ref):
        for l, p_ref in enumerate((p0_ref, p1_ref)):
            @pl.when(pl.program_id(0) == l)
            def _():
                _adamw_store(_slab_sum(p_ref), w_ref, m_ref, v_ref, g_ref, d_ref, mo_ref, vo_ref)

    p_specs = [pl.BlockSpec((n_slab, tr, cols), lambda l, i: (0, i * (1 - l) + (nt - 1) * l, 0)),
               pl.BlockSpec((n_slab, tr, cols), lambda l, i: (0, i * l, 0))]
    blk = pl.BlockSpec((None, tr, cols), lambda l, i: (l, i, 0))
    shp = jax.ShapeDtypeStruct((DEPTH, rows, cols), F32)
    return pl.pallas_call(
        body, name="sum_adamw_layers", grid=(DEPTH, nt),
        in_specs=p_specs + [blk, blk, blk],
        out_specs=[blk, blk, blk, blk],
        out_shape=[shp, shp, shp, shp],
        compiler_params=_params(("arbitrary", "arbitrary")),
    )(parts[0], parts[1], w, m, v)


def _sum_adamw(parts, w, m, v):
    n_slab, rows, cols = parts.shape
    tr = min(256, rows)

    def body(p_ref, w_ref, m_ref, v_ref, g_ref, d_ref, mo_ref, vo_ref):
        _adamw_store(_slab_sum(p_ref), w_ref, m_ref, v_ref, g_ref, d_ref, mo_ref, vo_ref)

    blk = pl.BlockSpec((tr, cols), lambda i: (i, 0))
    shp = jax.ShapeDtypeStruct((rows, cols), F32)
    return pl.pallas_call(
        body, name="sum_adamw", grid=(rows // tr,),
        in_specs=[pl.BlockSpec((n_slab, tr, cols), lambda i: (0, i, 0)), blk, blk, blk],
        out_specs=[blk, blk, blk, blk],
        out_shape=[shp, shp, shp, shp],
        compiler_params=_params(("arbitrary",)),
    )(parts, w, m, v)


SMALL_ROWS = 16


def kernel(x, c, norm_g, w_ada, b_ada, w_in, w_out, final_g, loss_target, m_norm_g, m_w_ada, m_b_ada, m_w_in, m_w_out, m_final_g, v_norm_g, v_w_ada, v_b_ada, v_w_in, v_w_out, v_final_g):
    me = _linear(*_my_place())
    in_cols = w_in.shape[2]
    out_rows = w_out.shape[1]
    ada_cols = w_ada.shape[2]

    w_in_m, w_out_m = _mx(w_in), _mx(w_out)
    g_in, g_out, g_c = _all_gather([w_in_m[0], w_out_m[0], jnp.broadcast_to(c, (8, D_MODEL))])
    w_in_g = [g_in.reshape(N_DEV, 1, D_MODEL, in_cols), None]
    w_out_g = [g_out.reshape(N_DEV, 1, out_rows, D_MODEL), None]
    c_all = g_c.reshape(N_DEV, 8, D_MODEL)[:, 0]

    b_cols = lax.dynamic_slice_in_dim(b_ada, me * ada_cols, ada_cols, axis=1)[:, None, :]
    c_act, mod_cols = _ada_fwd(c_all, w_ada, b_cols)
    (g_mod,) = _all_gather([mod_cols])
    g_mod = g_mod.reshape(N_DEV, DEPTH, N_DEV, ada_cols)
    mod = lax.dynamic_index_in_dim(g_mod, me, axis=2, keepdims=False)
    mod = mod.transpose(1, 0, 2).reshape(DEPTH, 3, D_MODEL)

    tables = _ret_tables(x.shape[1])
    h = x[0]
    saved = []
    for l in range(DEPTH):
        nxt = (w_in_m[l + 1], w_out_m[l + 1]) if l + 1 < DEPTH else ()
        head = None if nxt else (final_g[None], loss_target[0])
        h, sv, gathered = _layer_fwd(0, h, mod[l], norm_g[l:l + 1], w_in_g[l], w_out_g[l], tables, nxt, head)
        if nxt:
            w_in_g[l + 1] = gathered[0].reshape(N_DEV, 1, D_MODEL, in_cols)
            w_out_g[l + 1] = gathered[1].reshape(N_DEV, 1, out_rows, D_MODEL)
        saved.append(sv)
    dx, loss_part, dfg = h
    r_in, r_out, small = [None] * DEPTH, [None] * DEPTH, [None] * DEPTH
    for l in reversed(range(DEPTH)):
        dx, r_in[l], r_out[l], dmod, dng = _layer_bwd(0, me, dx, saved[l], norm_g[l:l + 1], w_in_g[l], w_out_g[l], tables)
        small[l] = (dmod.reshape(3, D_MODEL), dng)

    pad = jnp.zeros((SMALL_ROWS - 10, D_MODEL), F32)
    small_block = jnp.concatenate([small[0][0], small[1][0], small[0][1], small[1][1], dfg,
                                   jnp.broadcast_to(loss_part, (1, D_MODEL)), pad], axis=0)
    (g_small,) = _all_gather([small_block])
    g_small = g_small.reshape(N_DEV, SMALL_ROWS, D_MODEL)

    def small_pack(b, n, f, fill):
        return jnp.concatenate([b.reshape(6, D_MODEL), n, f[None],
                                jnp.full((SMALL_ROWS - 9, D_MODEL), fill, F32)], axis=0)

    s_g, s_d, s_m, s_v = _sum_adamw(g_small, small_pack(b_ada, norm_g, final_g, 0.0),
                                    small_pack(m_b_ada, m_norm_g, m_final_g, 0.0),
                                    small_pack(v_b_ada, v_norm_g, v_final_g, 1.0))
    loss = s_g[9, 0]

    def small_unpack(a):
        return a[0:6].reshape(DEPTH, 3 * D_MODEL), a[6:8], a[8]

    dmod_all = g_small[:, 0:6].reshape(N_DEV, DEPTH, 3 * D_MODEL).transpose(1, 0, 2)
    dmod_cols = lax.dynamic_slice_in_dim(dmod_all, me * ada_cols, ada_cols, axis=2)
    g_ada = _ada_bwd(c_act.T, dmod_cols).reshape(1, DEPTH * D_MODEL, ada_cols)
    ada = _sum_adamw(g_ada, *[a.reshape(DEPTH * D_MODEL, ada_cols) for a in (w_ada, m_w_ada, v_w_ada)])
    ada = [a.reshape(DEPTH, D_MODEL, ada_cols) for a in ada]

    win = _sum_adamw_layers(r_in, w_in, m_w_in, v_w_in)
    wout = _sum_adamw_layers(r_out, w_out, m_w_out, v_w_out)

    outs = [loss, dx[None]]
    for k in range(4):
        b, n, f = small_unpack((s_g, s_d, s_m, s_v)[k])
        outs += [n, ada[k], b, win[k], wout[k], f]
    return tuple(outs)
```

```python
import functools

import jax
import jax.numpy as jnp
from jax import lax
from jax.experimental import pallas as pl
from jax.experimental.pallas import tpu as pltpu

F32 = jnp.float32
MXU_DTYPE = jnp.bfloat16

D_MODEL = 1024
DEPTH = 2
N_DEV = 8
CHUNK = 64
D_RET = 512
D_SB = 512
RET_HEADS = 4
RET_HEAD_DIM = 128
SB_HEADS = 8
SB_HEAD_DIM = 64
D_IN = 4096
ROPE_BASE = 10000.0
EPS = 1e-6
SB_SCALE = SB_HEAD_DIM ** -0.5
RET_KSCALE = RET_HEAD_DIM ** -0.5

ADAM_LR = 0.001
ADAM_B1 = 0.9
ADAM_B2 = 0.999
ADAM_EPS = 1e-08
ADAM_WD = 0.01
ADAM_STEP = 10

V7X_VMEM_BYTES = 64 * 2 ** 20
VMEM_LIMIT = V7X_VMEM_BYTES - 8 * 2 ** 20
LANES = 128

_NT = (((1,), (1,)), ((), ()))
_TN = (((0,), (0,)), ((), ()))


def _dot(a, b):
    return jnp.dot(a, b, preferred_element_type=F32)


def _dot_nt(a, b):
    return lax.dot_general(a, b, _NT, preferred_element_type=F32)


def _dot_tn(a, b):
    return lax.dot_general(a, b, _TN, preferred_element_type=F32)


def _mx(x):
    return x.astype(MXU_DTYPE)


def _sigmoid(x):
    return 1.0 / (1.0 + jnp.exp(-x))


def _params(sem=None):
    return pltpu.CompilerParams(dimension_semantics=sem, vmem_limit_bytes=VMEM_LIMIT)


def _row_tile(s):
    return min(512, s)


def _w_in_spec(w_in_g, layer):
    return pl.BlockSpec((N_DEV, None) + w_in_g.shape[2:], lambda i: (0, layer, 0, 0))


def _w_out_spec(w_out_g, layer):
    return pl.BlockSpec((N_DEV, None) + w_out_g.shape[2:], lambda i: (0, layer, 0, 0))


def _ln_proj(x, shift, scale1p, g, w_in_g, layer, gather=()):
    s = x.shape[0]
    ts = _row_tile(s)
    ns = s // ts
    n_g = len(gather)

    def body(x_ref, sh_ref, sc_ref, g_ref, w_ref, *rest):
        ret_ref, qkv_ref, sg_ref = rest[n_g:n_g + 3]
        if n_g:
            start, forward, finish = _gather_plan(rest[:n_g], rest[n_g + 3:2 * n_g + 3], *rest[2 * n_g + 3:])
            i = pl.program_id(0)
            pl.when(i == 0)(start)
            pl.when(i == max(ns - 4, 0))(forward)
        xv = x_ref[...]
        rstd = lax.rsqrt(jnp.mean(xv * xv, axis=-1, keepdims=True) + EPS)
        h = (xv * rstd * g_ref[...]) * sc_ref[...] + sh_ref[...]
        hb = _mx(h)
        for n in range(4):
            ret_ref[:, n * 512:(n + 1) * 512] = _dot(hb, w_ref[n])
        qkv_ref[:, 0:512] = _mx(_dot(hb, w_ref[4]) * SB_SCALE)
        qkv_ref[:, 512:1024] = _mx(_dot(hb, w_ref[5]))
        qkv_ref[:, 1024:1536] = _mx(_dot(hb, w_ref[6]))
        sg_ref[...] = _dot(hb, w_ref[7])
        if n_g:
            pl.when(i == ns - 1)(finish)

    vec = pl.BlockSpec((1, D_MODEL), lambda i: (0, 0))
    return pl.pallas_call(
        body, name="ln_proj_gather" if n_g else "ln_proj", grid=(ns,),
        in_specs=[pl.BlockSpec((ts, D_MODEL), lambda i: (i, 0)), vec, vec, vec,
                  _w_in_spec(w_in_g, layer)] + [HBM_SPEC] * n_g,
        out_specs=[pl.BlockSpec((ts, 2048), lambda i: (i, 0)),
                   pl.BlockSpec((ts, 1536), lambda i: (i, 0)),
                   pl.BlockSpec((ts, 512), lambda i: (i, 0))] + [HBM_SPEC] * n_g,
        out_shape=[jax.ShapeDtypeStruct((s, 2048), F32),
                   jax.ShapeDtypeStruct((s, 1536), MXU_DTYPE),
                   jax.ShapeDtypeStruct((s, 512), F32)] + _gathered_shapes(gather),
        scratch_shapes=_gather_sems(n_g) if n_g else (),
        compiler_params=_params(("arbitrary",)),
    )(x, shift, scale1p, g, w_in_g, *gather)


def _w_out_halves(w_ref):
    half = N_DEV // 2
    return (w_ref[0:half].reshape(D_RET, D_MODEL), w_ref[half:N_DEV].reshape(D_SB, D_MODEL))


def _out_proj(x, gate, y_r, y_s, w_out_g, layer, loss_head=None):
    s = x.shape[0]
    ts = min(2 * _row_tile(s), s)

    def layer_out(x_ref, gate_ref, yr_ref, ys_ref, w_ref):
        w_r, w_s = _w_out_halves(w_ref)
        return x_ref[...] + gate_ref[...] * (_dot(yr_ref[...], w_r) + _dot(ys_ref[...], w_s))

    def body(x_ref, gate_ref, yr_ref, ys_ref, w_ref, o_ref):
        o_ref[...] = layer_out(x_ref, gate_ref, yr_ref, ys_ref, w_ref)

    def body_loss(x_ref, gate_ref, yr_ref, ys_ref, w_ref, fg_ref, t_ref, dx_ref, loss_ref, dfg_ref):
        @pl.when(pl.program_id(0) == 0)
        def _():
            loss_ref[...] = jnp.zeros_like(loss_ref)
            dfg_ref[...] = jnp.zeros_like(dfg_ref)

        xv = layer_out(x_ref, gate_ref, yr_ref, ys_ref, w_ref)
        fgv = fg_ref[...]
        rstd = lax.rsqrt(jnp.mean(xv * xv, axis=-1, keepdims=True) + EPS)
        xn = xv * rstd
        err = xn * fgv - t_ref[...]
        tok = jnp.mean(err * err, axis=-1, keepdims=True)
        loss_ref[...] += 0.5 * jnp.sum(tok, axis=0, keepdims=True)
        dy = err * (1.0 / D_MODEL)
        dfg_ref[...] += jnp.sum(dy * xn, axis=0, keepdims=True)
        dxn = dy * fgv
        dx_ref[...] = rstd * (dxn - xn * jnp.mean(dxn * xn, axis=-1, keepdims=True))

    rows = pl.BlockSpec((ts, D_MODEL), lambda i: (i, 0))
    vec = pl.BlockSpec((1, D_MODEL), lambda i: (0, 0))
    in_specs = [rows, vec, pl.BlockSpec((ts, 512), lambda i: (i, 0)), pl.BlockSpec((ts, 512), lambda i: (i, 0)),
                _w_out_spec(w_out_g, layer)]
    if loss_head is None:
        return pl.pallas_call(
            body, name="out_proj", grid=(s // ts,), in_specs=in_specs, out_specs=rows,
            out_shape=jax.ShapeDtypeStruct((s, D_MODEL), F32),
            compiler_params=_params(("arbitrary",)),
        )(x, gate, y_r, y_s, w_out_g)
    return pl.pallas_call(
        body_loss, name="out_proj_loss", grid=(s // ts,), in_specs=in_specs + [vec, rows],
        out_specs=[rows, pl.BlockSpec((1, 1), lambda i: (0, 0)), vec],
        out_shape=[jax.ShapeDtypeStruct((s, D_MODEL), F32),
                   jax.ShapeDtypeStruct((1, 1), F32),
                   jax.ShapeDtypeStruct((1, D_MODEL), F32)],
        compiler_params=_params(("arbitrary",)),
    )(x, gate, y_r, y_s, w_out_g, *loss_head)


def _out_proj_bwd(dx_out, gate, y_r, y_s, w_out_g, layer):
    s = dx_out.shape[0]
    ts = min(2 * _row_tile(s), s)
    ns = s // ts

    def body(dx_ref, gate_ref, yr_ref, ys_ref, w_ref, dy_ref, dw_ref, dgate_ref):
        i = pl.program_id(0)

        @pl.when(i == 0)
        def _():
            dw_ref[...] = jnp.zeros_like(dw_ref)

        dxv = dx_ref[...]
        dt = _mx(dxv * gate_ref[...])
        dxb = _mx(dxv)
        w_r, w_s = _w_out_halves(w_ref)
        dy_ref[:, 0:512] = _dot_nt(dt, w_r)
        dy_ref[:, 512:1024] = _dot_nt(dt, w_s)
        dw_ref[0:512, :] += _dot_tn(yr_ref[...], dxb)
        dw_ref[512:1024, :] += _dot_tn(ys_ref[...], dxb)

        @pl.when(i == ns - 1)
        def _():
            m_r, m_s = dw_ref[0:512, :], dw_ref[512:1024, :]
            dgate_ref[...] = (jnp.sum(w_r.astype(F32) * m_r, axis=0, keepdims=True)
                              + jnp.sum(w_s.astype(F32) * m_s, axis=0, keepdims=True))
            dw_ref[...] = dw_ref[...] * gate_ref[...]

    return pl.pallas_call(
        body, name="out_proj_bwd", grid=(ns,),
        in_specs=[pl.BlockSpec((ts, D_MODEL), lambda i: (i, 0)),
                  pl.BlockSpec((1, D_MODEL), lambda i: (0, 0)),
                  pl.BlockSpec((ts, 512), lambda i: (i, 0)),
                  pl.BlockSpec((ts, 512), lambda i: (i, 0)),
                  _w_out_spec(w_out_g, layer)],
        out_specs=[pl.BlockSpec((ts, D_MODEL), lambda i: (i, 0)),
                   pl.BlockSpec((D_MODEL, D_MODEL), lambda i: (0, 0)),
                   pl.BlockSpec((1, D_MODEL), lambda i: (0, 0))],
        out_shape=[jax.ShapeDtypeStruct((s, D_MODEL), F32),
                   jax.ShapeDtypeStruct((D_MODEL, D_MODEL), F32),
                   jax.ShapeDtypeStruct((1, D_MODEL), F32)],
        compiler_params=_params(("arbitrary",)),
    )(dx_out, gate, y_r, y_s, w_out_g)


def _scatter_plan(parts_ref, recv_ref, send_sems, recv_sems, local_sem):
    px, py, pc = _my_place()
    mine = _linear(px, py, pc)

    def copy(r):
        peer = (1 - px if r & 4 else px, 1 - py if r & 2 else py, 1 - pc if r & 1 else pc)
        return pltpu.make_async_remote_copy(
            src_ref=parts_ref.at[_linear(*peer)], dst_ref=recv_ref.at[mine],
            send_sem=send_sems.at[r - 1], recv_sem=recv_sems.at[r - 1],
            device_id=peer, device_id_type=MESH_IDS)

    own = pltpu.make_async_copy(parts_ref.at[mine], recv_ref.at[mine], local_sem.at[0])

    def start():
        own.start()
        for r in range(1, N_DEV):
            copy(r).start()

    def finish():
        for r in range(1, N_DEV):
            copy(r).wait_recv()
            copy(r).wait_send()
        own.wait()

    return start, finish


def _in_proj_bwd_x(x, dx_out, dproj, shift, scale1p, g, w_in_g, layer):
    s = x.shape[0]
    ts = _row_tile(s)
    ns = s // ts
    nb = D_IN // N_DEV

    def body(x_ref, dxo_ref, dr_ref, d4_ref, d5_ref, d6_ref, d7_ref, sh_ref, sc_ref, g_ref, w_ref,
             dx_ref, dsh_ref, dsc_ref, dg_ref, ht_ref):
        i = pl.program_id(0)

        @pl.when(i == 0)
        def _():
            dsh_ref[...] = jnp.zeros_like(dsh_ref)
            dsc_ref[...] = jnp.zeros_like(dsc_ref)
            dg_ref[...] = jnp.zeros_like(dg_ref)

        dh = _dot_nt(dr_ref[:, 0:nb], w_ref[0])
        for n in range(1, 4):
            dh += _dot_nt(dr_ref[:, n * nb:(n + 1) * nb], w_ref[n])
        for n, d_ref in zip(range(4, N_DEV), (d4_ref, d5_ref, d6_ref, d7_ref)):
            dh += _dot_nt(d_ref[...], w_ref[n])
        xv = x_ref[...]
        gv = g_ref[...]
        scv = sc_ref[...]
        rstd = lax.rsqrt(jnp.mean(xv * xv, axis=-1, keepdims=True) + EPS)
        xn = xv * rstd
        xg = xn * gv
        ht_ref[...] = _mx((xg * scv + sh_ref[...]).T)
        dsh_ref[...] += jnp.sum(dh, axis=0, keepdims=True)
        dsc_ref[...] += jnp.sum(dh * xg, axis=0, keepdims=True)
        dhs = dh * scv
        dg_ref[...] += jnp.sum(dhs * xn, axis=0, keepdims=True)
        dxn = dhs * gv
        dx_ref[...] = rstd * (dxn - xn * jnp.mean(dxn * xn, axis=-1, keepdims=True)) + dxo_ref[...]

    vec = pl.BlockSpec((1, D_MODEL), lambda i: (0, 0))
    return pl.pallas_call(
        body, name="in_proj_bwd_x", grid=(ns,),
        in_specs=[pl.BlockSpec((ts, D_MODEL), lambda i: (i, 0)),
                  pl.BlockSpec((ts, D_MODEL), lambda i: (i, 0)),
                  pl.BlockSpec((ts, 4 * nb), lambda i: (i, 0))]
                 + [pl.BlockSpec((ts, nb), lambda i: (i, 0))] * 4
                 + [vec, vec, vec, _w_in_spec(w_in_g, layer)],
        out_specs=[pl.BlockSpec((ts, D_MODEL), lambda i: (i, 0)), vec, vec, vec,
                   pl.BlockSpec((D_MODEL, ts), lambda i: (0, i))],
        out_shape=[jax.ShapeDtypeStruct((s, D_MODEL), F32),
                   jax.ShapeDtypeStruct((1, D_MODEL), F32),
                   jax.ShapeDtypeStruct((1, D_MODEL), F32),
                   jax.ShapeDtypeStruct((1, D_MODEL), F32),
                   jax.ShapeDtypeStruct((D_MODEL, s), MXU_DTYPE)],
        compiler_params=_params(("arbitrary",)),
    )(x, dx_out, *dproj, shift, scale1p, g, w_in_g)


def _in_proj_bwd_w(me, h_t, dproj):
    s = h_t.shape[1]
    ts = min(4 * _row_tile(s), s)
    ns = s // ts
    nb = D_IN // N_DEV
    n_chip = N_DEV // 2

    def flip_bits(j):
        return jnp.where(j == 0, 4, jnp.where(j == 1, 2, jnp.where(j == 2, 6, 0)))

    def slab_of(t, me_ref):
        return jnp.bitwise_xor(me_ref[0], flip_bits(t // 2) + 1 - t % 2)

    def body(me_ref, ht_ref, dr_ref, d4_ref, d5_ref, d6_ref, d7_ref, rin_ref,
             acc, stage, pre_buf, pre_send, pre_recv, sum_send, sum_recv, local_sem):
        t = pl.program_id(0)
        i = pl.program_id(1)
        j = t // 2
        summing = t % 2 == 1
        slab = slab_of(t, me_ref)
        px, py, pc = _my_place()

        def pre_copy(jj):
            return pltpu.make_async_remote_copy(
                src_ref=stage.at[jj % 2], dst_ref=pre_buf.at[jj],
                send_sem=pre_send.at[jj], recv_sem=pre_recv.at[jj],
                device_id=(px, py, 1 - pc), device_id_type=MESH_IDS)

        def sum_copy(jj):
            fx = jnp.logical_or(jj == 0, jj == 2)
            fy = jnp.logical_or(jj == 1, jj == 2)
            return pltpu.make_async_remote_copy(
                src_ref=stage.at[2 + jj % 2], dst_ref=rin_ref.at[jj],
                send_sem=sum_send.at[jj], recv_sem=sum_recv.at[jj],
                device_id=(jnp.where(fx, 1 - px, px), jnp.where(fy, 1 - py, py), pc), device_id_type=MESH_IDS)

        own = pltpu.make_async_copy(stage.at[3], rin_ref.at[n_chip - 1], local_sem.at[0])

        @pl.when(i == 0)
        def _():
            acc[...] = jnp.zeros_like(acc)

        @pl.when(slab < 4)
        def _():
            acc[...] += _dot(ht_ref[...], dr_ref[...])

        for n, d_ref in zip(range(4, N_DEV), (d4_ref, d5_ref, d6_ref, d7_ref)):
            @pl.when(slab == n)
            def _():
                acc[...] += _dot(ht_ref[...], d_ref[...])

        @pl.when(jnp.logical_and(i == ns - 1, jnp.logical_not(summing)))
        def _():
            @pl.when(j >= 2)
            def _():
                pre_copy(j - 2).wait_send()

            stage[j % 2] = acc[...].astype(stage.dtype)
            pre_copy(j).start()

        @pl.when(jnp.logical_and(i == ns - 1, summing))
        def _():
            pre_copy(j).wait_recv()

            @pl.when(j >= 2)
            def _():
                sum_copy(j - 2).wait_send()

            stage[2 + j % 2] = (acc[...] + pre_buf[j].astype(F32)).astype(stage.dtype)

            @pl.when(j < n_chip - 1)
            def _():
                sum_copy(j).start()

            @pl.when(j == n_chip - 1)
            def _():
                own.start()
                pre_copy(n_chip - 2).wait_send()
                pre_copy(n_chip - 1).wait_send()
                sum_copy(n_chip - 2).wait_send()
                for jj in range(n_chip - 1):
                    sum_copy(jj).wait_recv()
                own.wait()

    def part_rows(n, t, i, me_ref):
        return jnp.where(slab_of(t, me_ref) == n, i, ns - 1), 0

    return pl.pallas_call(
        body, name="in_proj_bwd_w",
        grid_spec=pltpu.PrefetchScalarGridSpec(
            num_scalar_prefetch=1, grid=(N_DEV, ns),
            in_specs=[pl.BlockSpec((D_MODEL, ts), lambda t, i, me_ref: (0, i)),
                      pl.BlockSpec((ts, nb), lambda t, i, me_ref: (
                          jnp.where(slab_of(t, me_ref) < 4, i, ns - 1), jnp.minimum(slab_of(t, me_ref), 3)))]
                     + [pl.BlockSpec((ts, nb), functools.partial(part_rows, n)) for n in range(4, N_DEV)],
            out_specs=HBM_SPEC,
            scratch_shapes=[pltpu.VMEM((D_MODEL, nb), F32),
                            pltpu.VMEM((4, D_MODEL, nb), MXU_DTYPE),
                            pltpu.VMEM((n_chip, D_MODEL, nb), MXU_DTYPE),
                            pltpu.SemaphoreType.DMA((n_chip,)), pltpu.SemaphoreType.DMA((n_chip,)),
                            pltpu.SemaphoreType.DMA((n_chip - 1,)), pltpu.SemaphoreType.DMA((n_chip - 1,)),
                            pltpu.SemaphoreType.DMA((1,))]),
        out_shape=jax.ShapeDtypeStruct((n_chip, D_MODEL, nb), MXU_DTYPE),
        compiler_params=_params(("arbitrary", "arbitrary")),
    )(jnp.reshape(me, (1,)).astype(jnp.int32), h_t, *dproj)


RET_TILE = 256


def _ret_tables(s):
    t = min(RET_TILE, s)
    half = RET_HEAD_DIM // 2
    pos = jnp.arange(s, dtype=F32)
    inv = ROPE_BASE ** (-jnp.arange(half, dtype=F32) / half)
    ang = pos[:, None] * jnp.concatenate([inv, inv])[None, :]
    cos2 = jnp.cos(ang)
    sin2 = jnp.sin(ang) * jnp.concatenate([-jnp.ones((half,), F32), jnp.ones((half,), F32)])[None, :]
    lg = jnp.log1p(-(2.0 ** (-5.0 - jnp.arange(RET_HEADS, dtype=F32))))[:, None, None]
    n = jnp.arange(t)
    dist = (n[:, None] - n[None, :]).astype(F32)[None]
    cn = (n // CHUNK)[:, None]
    cm = (n // CHUNK)[None, :]
    mask = jnp.where((cn == cm)[None], jnp.exp(jnp.abs(dist) * lg),
                     jnp.where((cm < cn)[None], jnp.exp(dist * lg), 0.0))
    nf = n.astype(F32)[None, :, None]
    dq = jnp.broadcast_to(jnp.exp((nf + 1.0) * lg), (RET_HEADS, t, LANES))
    dk = jnp.broadcast_to(jnp.exp((t - 1.0 - nf) * lg), (RET_HEADS, t, LANES))
    gt = jnp.broadcast_to(jnp.exp(float(t) * lg), (RET_HEADS, 1, LANES))
    return cos2, sin2, mask, dq, dk, gt


def _roll_half(x):
    return pltpu.roll(x, RET_HEAD_DIM // 2, 1)


def _ret_heads_fwd(ret_ref, cos, sin, m_ref, dq_ref, dk_ref, s0):
    hd = RET_HEAD_DIM
    heads = range(RET_HEADS)
    qb, kb, vb, kdb = [], [], [], []
    for h in heads:
        q = ret_ref[:, h * hd:(h + 1) * hd]
        k = ret_ref[:, 512 + h * hd:512 + (h + 1) * hd]
        kr = (k * cos + _roll_half(k) * sin) * RET_KSCALE
        qb.append(_mx(q * cos + _roll_half(q) * sin))
        kb.append(_mx(kr))
        kdb.append(_mx(kr * dk_ref[h]))
        vb.append(_mx(ret_ref[:, 1024 + h * hd:1024 + (h + 1) * hd]))
    p = [_dot_nt(qb[h], kb[h]) for h in heads]
    cross = [_dot(qb[h], _mx(s0[h])) for h in heads]
    pb = [_mx(p[h] * m_ref[h]) for h in heads]
    o = [_dot(pb[h], vb[h]) + cross[h] * dq_ref[h] for h in heads]
    gn, rstd = [], []
    for h in heads:
        oc = o[h] - jnp.mean(o[h], axis=-1, keepdims=True)
        rstd.append(lax.rsqrt(jnp.mean(oc * oc, axis=-1, keepdims=True) + EPS))
        gn.append(oc * rstd[h])
    return qb, kb, vb, pb, kdb, gn, rstd


def _retention_fwd(ret, tables):
    cos2, sin2, mask, dq, dk, gt = tables
    s = ret.shape[0]
    t = mask.shape[1]
    nt = s // t
    hd = RET_HEAD_DIM

    def body(ret_ref, cos_ref, sin_ref, m_ref, dq_ref, dk_ref, gt_ref, y_ref, st_ref, s_scr):
        i = pl.program_id(0)

        @pl.when(i == 0)
        def _():
            s_scr[...] = jnp.zeros_like(s_scr)

        s0 = [s_scr[h] for h in range(RET_HEADS)]
        _, _, vb, _, kdb, gn, _ = _ret_heads_fwd(ret_ref, cos_ref[...], sin_ref[...], m_ref, dq_ref, dk_ref, s0)
        kv = [_dot_tn(kdb[h], vb[h]) for h in range(RET_HEADS)]
        for h in range(RET_HEADS):
            g = ret_ref[:, 1536 + h * hd:1536 + (h + 1) * hd]
            st_ref[h] = s0[h]
            y_ref[:, h * hd:(h + 1) * hd] = (gn[h] * (g * _sigmoid(g))).astype(y_ref.dtype)
            s_scr[h] = s0[h] * gt_ref[h] + kv[h]

    full3 = lambda a: pl.BlockSpec(a.shape, lambda i: (0, 0, 0))
    return pl.pallas_call(
        body, name="retention_fwd", grid=(nt,),
        in_specs=[pl.BlockSpec((t, 2048), lambda i: (i, 0)),
                  pl.BlockSpec((t, LANES), lambda i: (i, 0)),
                  pl.BlockSpec((t, LANES), lambda i: (i, 0)),
                  full3(mask), full3(dq), full3(dk), full3(gt)],
        out_specs=[pl.BlockSpec((t, 512), lambda i: (i, 0)),
                   pl.BlockSpec((None, RET_HEADS, hd, hd), lambda i: (i, 0, 0, 0))],
        out_shape=[jax.ShapeDtypeStruct((s, 512), MXU_DTYPE),
                   jax.ShapeDtypeStruct((nt, RET_HEADS, hd, hd), F32)],
        scratch_shapes=[pltpu.VMEM((RET_HEADS, hd, hd), F32)],
        compiler_params=_params(("arbitrary",)),
    )(ret, cos2, sin2, mask, dq, dk, gt)


def _retention_bwd(ret, states, dy, tables, dwo_parts):
    cos2, sin2, mask, dq, dk, gt = tables
    s = ret.shape[0]
    t = mask.shape[1]
    nt = s // t
    hd = RET_HEAD_DIM

    def body(ret_ref, st_ref, dy_ref, cos_ref, sin_ref, m_ref, dq_ref, dk_ref, gt_ref, dwo_ref,
             d_ref, rout_ref, ds_scr, send_sems, recv_sems, local_sem):
        i = pl.program_id(0)
        start, finish = _scatter_plan(dwo_ref, rout_ref, send_sems, recv_sems, local_sem)

        @pl.when(i == 0)
        def _():
            start()
            ds_scr[...] = jnp.zeros_like(ds_scr)

        cos = cos_ref[...]
        sin = sin_ref[...]
        heads = range(RET_HEADS)
        s0 = [st_ref[h] for h in heads]
        ds = [ds_scr[h] for h in heads]
        dsb = [_mx(ds[h]) for h in heads]
        qb, kb, vb, pb, kdb, gn, rstd = _ret_heads_fwd(ret_ref, cos, sin, m_ref, dq_ref, dk_ref, s0)
        dob, dodb = [], []
        for h in heads:
            g = ret_ref[:, 1536 + h * hd:1536 + (h + 1) * hd]
            dyv = dy_ref[:, h * hd:(h + 1) * hd]
            sg = _sigmoid(g)
            d_ref[:, 1536 + h * hd:1536 + (h + 1) * hd] = (
                dyv * gn[h] * (sg * (1.0 + g * (1.0 - sg)))).astype(d_ref.dtype)
            dgn = dyv * (g * sg)
            do = rstd[h] * (dgn - jnp.mean(dgn, axis=-1, keepdims=True)
                            - gn[h] * jnp.mean(dgn * gn[h], axis=-1, keepdims=True))
            dob.append(_mx(do))
            dodb.append(_mx(do * dq_ref[h]))
        dp = [_dot_nt(dob[h], vb[h]) for h in heads]
        dv = [_dot_tn(pb[h], dob[h]) + _dot(kdb[h], dsb[h]) for h in heads]
        dq_cross = [_dot_nt(dodb[h], _mx(s0[h])) for h in heads]
        dk_cross = [_dot_nt(vb[h], dsb[h]) for h in heads]
        ds_new = [_dot_tn(qb[h], dodb[h]) for h in heads]
        dpb = [_mx(dp[h] * m_ref[h]) for h in heads]
        dqr = [_dot(dpb[h], kb[h]) + dq_cross[h] for h in heads]
        dkr = [(_dot_tn(dpb[h], qb[h]) + dk_cross[h] * dk_ref[h]) * RET_KSCALE for h in heads]
        for h in heads:
            d_ref[:, 1024 + h * hd:1024 + (h + 1) * hd] = dv[h].astype(d_ref.dtype)
            d_ref[:, h * hd:(h + 1) * hd] = (dqr[h] * cos + _roll_half(dqr[h] * sin)).astype(d_ref.dtype)
            d_ref[:, 512 + h * hd:512 + (h + 1) * hd] = (
                dkr[h] * cos + _roll_half(dkr[h] * sin)).astype(d_ref.dtype)
            ds_scr[h] = ds[h] * gt_ref[h] + ds_new[h]
        pl.when(i == nt - 1)(finish)

    full3 = lambda a: pl.BlockSpec(a.shape, lambda i: (0, 0, 0))
    rev = lambda i: (nt - 1 - i, 0)
    return pl.pallas_call(
        body, name="retention_bwd", grid=(nt,),
        in_specs=[pl.BlockSpec((t, 2048), rev),
                  pl.BlockSpec((None, RET_HEADS, hd, hd), lambda i: (nt - 1 - i, 0, 0, 0)),
                  pl.BlockSpec((t, 512), rev),
                  pl.BlockSpec((t, LANES), rev),
                  pl.BlockSpec((t, LANES), rev),
                  full3(mask), full3(dq), full3(dk), full3(gt), HBM_SPEC],
        out_specs=[pl.BlockSpec((t, 2048), rev), HBM_SPEC],
        out_shape=[jax.ShapeDtypeStruct((s, 2048), MXU_DTYPE),
                   jax.ShapeDtypeStruct(dwo_parts.shape, dwo_parts.dtype)],
        scratch_shapes=[pltpu.VMEM((RET_HEADS, hd, hd), F32),
                        pltpu.SemaphoreType.DMA((N_PEERS,)), pltpu.SemaphoreType.DMA((N_PEERS,)),
                        pltpu.SemaphoreType.DMA((1,))],
        compiler_params=_params(("arbitrary",)),
    )(ret, states, dy, cos2, sin2, mask, dq, dk, gt, dwo_parts)


SB_BLOCK = 256
SB_CHUNK = 32


SB_SKIP = 104.0
NO_KEYS = -1e30


def _split_dots(xs, u):
    parts = []
    for x in xs:
        hi = lax.bitcast_convert_type(lax.bitcast_convert_type(x, jnp.uint32) & jnp.uint32(0xFFFF0000), F32)
        parts += [_mx(hi), _mx(x - hi)]
    out = _dot(jnp.concatenate(parts, axis=0), u)
    n = xs[0].shape[0]
    return [out[2 * k * n:(2 * k + 1) * n] + out[(2 * k + 1) * n:(2 * k + 2) * n] for k in range(len(xs))]


def _split_dot(x, u):
    return _split_dots([x], u)[0]


def _sb_pair_weights(lb, lk, allowed, u_gt):
    blk = lb.shape[0]
    lk_p, lk_d = lk[:, :blk], lk[:, blk:]
    r_d = _rowsum(lk_d)
    cs_p, cs_d = _split_dots([lk_p, lk_d], u_gt)
    a = jnp.exp(lb + jnp.concatenate([cs_p + r_d, cs_d], axis=1))
    return jnp.where(allowed, a, 0.0), r_d, r_d + _rowsum(lk_p)


def _sb_logits(q, k, causal):
    z = _dot_nt(q, k)
    l1p = jnp.log(1.0 + jnp.exp(-jnp.abs(z)))
    lk = -(jnp.maximum(z, 0.0) + l1p)
    if causal is not None:
        lk = jnp.where(causal, lk, 0.0)
    return jnp.minimum(z, 0.0) - l1p, lk


def _sb_weights(lb, lk, r, u_gt, causal):
    a = jnp.exp(lb + _split_dot(lk, u_gt) + r)
    return a if causal is None else jnp.where(causal, a, 0.0)


def _rowsum(x):
    return jnp.sum(x, axis=1, keepdims=True)


def _sb_pair_tile(i, blk):
    row = lax.broadcasted_iota(jnp.int32, (blk, 2 * blk), 0)
    col = lax.broadcasted_iota(jnp.int32, (blk, 2 * blk), 1)
    first_col = jnp.where(i >= 1, 0, blk)
    allowed = jnp.logical_and(row > col - blk, col >= first_col)
    rows_p = pl.ds(pl.multiple_of(jnp.maximum(i - 1, 0) * blk, blk), blk)
    rows_d = pl.ds(pl.multiple_of(i * blk, blk), blk)
    return allowed, rows_p, rows_d


def _sb_fwd(qkv, sg):
    s = qkv.shape[0]
    blk = min(SB_BLOCK, s)
    nq = s // blk
    hd = SB_HEAD_DIM

    ch = min(SB_CHUNK, blk)

    def body(q_ref, k_ref, v_ref, g_ref, y_ref, o_ref, z_scr, lhs_scr, cs_scr, a_scr):
        i = pl.program_id(1)
        row = lax.broadcasted_iota(jnp.int32, (blk, blk), 0)
        col = lax.broadcasted_iota(jnp.int32, (blk, blk), 1)
        u_gt = (row > col).astype(MXU_DTYPE)
        heads = [slice(hh * hd, (hh + 1) * hd) for hh in range(2)]
        qs = [q_ref[:, ls] for ls in heads]
        _, rows_p, rows_d = _sb_pair_tile(i, blk)
        has_prev = i >= 1
        crow = lax.broadcasted_iota(jnp.int32, (ch, blk), 0)
        ccol = lax.broadcasted_iota(jnp.int32, (ch, blk), 1)

        def logits(hh):
            kc = jnp.concatenate([k_ref[rows_p, heads[hh]], k_ref[rows_d, heads[hh]]], axis=0)
            z_scr[hh] = _dot_nt(qs[hh], kc)

        def keep_parts(hh):
            r_d, r_all = [], []
            for c in range(blk // ch):
                rows = pl.ds(c * ch, ch)
                causal = crow + c * ch > ccol
                z = z_scr[hh, rows, :]
                l1p = jnp.log(1.0 + jnp.exp(-jnp.abs(z)))
                lb = jnp.minimum(z, 0.0) - l1p
                z_scr[hh, rows, :] = lb
                lk = lb - z
                lk_p = lk[:, :blk]
                lk_d = jnp.where(causal, lk[:, blk:], 0.0)
                lhs_scr[hh, pl.ds(c * ch, ch), :] = _mx(lk_p)
                lhs_scr[hh, pl.ds(blk + c * ch, ch), :] = _mx(lk_d)
                r_d.append(_rowsum(lk_d))
                r_all.append(r_d[c] + _rowsum(lk_p))
            return r_d, jnp.concatenate(r_all, axis=0)

        def suffix_sums(hh):
            cs_scr[hh] = _dot(lhs_scr[hh], u_gt)

        def weights(hh, r_d):
            for c in range(blk // ch):
                rows = pl.ds(c * ch, ch)
                causal = crow + c * ch > ccol
                cs_p = cs_scr[hh, pl.ds(c * ch, ch), :] + jnp.where(has_prev, r_d[c], NO_KEYS)
                cs_d = cs_scr[hh, pl.ds(blk + c * ch, ch), :]
                lb = z_scr[hh, rows, :]
                a_p = jnp.exp(lb[:, :blk] + cs_p)
                a_d = jnp.where(causal, jnp.exp(lb[:, blk:] + cs_d), 0.0)
                a_scr[hh, rows, :] = _mx(jnp.concatenate([a_p, a_d], axis=1))

        def values(hh):
            vc = jnp.concatenate([v_ref[rows_p, heads[hh]], v_ref[rows_d, heads[hh]]], axis=0)
            return _dot(a_scr[hh], vc)

        def block(hh, j, r):
            start = pl.multiple_of(j * blk, blk)
            lb, lk = _sb_logits(qs[hh], k_ref[pl.ds(start, blk), heads[hh]], None)
            a = _sb_weights(lb, lk, r, u_gt, None)
            return _dot(_mx(a), v_ref[pl.ds(start, blk), heads[hh]]), r + _rowsum(lk)

        def more(n, r0, r1):
            return jnp.logical_and(n < i, jnp.max(jnp.maximum(r0, r1)) > -SB_SKIP)

        logits(0)
        logits(1)
        gv = g_ref[...]
        gates = gv * _sigmoid(gv)
        rd0, r0 = keep_parts(0)
        suffix_sums(0)
        rd1, r1 = keep_parts(1)
        suffix_sums(1)
        go = more(jnp.int32(1), r0, r1)
        weights(0, rd0)
        acc0 = values(0)
        weights(1, rd1)
        acc1 = values(1)

        def step(c):
            _, n, acc0, r0, acc1, r1 = c
            pv0, r0 = block(0, i - 1 - n, r0)
            pv1, r1 = block(1, i - 1 - n, r1)
            return more(n + 1, r0, r1), n + 1, acc0 + pv0, r0, acc1 + pv1, r1

        _, _, acc0, _, acc1, _ = lax.while_loop(lambda c: c[0], step, (go, jnp.int32(1), acc0, r0, acc1, r1))
        o = jnp.concatenate([acc0, acc1], axis=1)
        o_ref[...] = o
        y_ref[...] = (o * gates).astype(y_ref.dtype)

    qblk = pl.BlockSpec((blk, LANES), lambda p, i: (i, p))
    return pl.pallas_call(
        body, name="stickbreak_fwd", grid=(SB_HEADS // 2, nq),
        in_specs=[qblk,
                  pl.BlockSpec((s, LANES), lambda p, i: (0, 4 + p)),
                  pl.BlockSpec((s, LANES), lambda p, i: (0, 8 + p)),
                  qblk],
        out_specs=[qblk, qblk],
        out_shape=[jax.ShapeDtypeStruct((s, 512), MXU_DTYPE),
                   jax.ShapeDtypeStruct((s, 512), F32)],
        scratch_shapes=[pltpu.VMEM((2, blk, 2 * blk), F32),
                        pltpu.VMEM((2, 2 * blk, blk), MXU_DTYPE),
                        pltpu.VMEM((2, 2 * blk, blk), F32),
                        pltpu.VMEM((2, blk, 2 * blk), MXU_DTYPE)],
        compiler_params=_params(("arbitrary", "arbitrary")),
    )(qkv, qkv, qkv, sg)


def _sb_bwd(qkv, sg, o, dy):
    s = qkv.shape[0]
    blk = min(SB_BLOCK, s)
    nq = s // blk
    hd = SB_HEAD_DIM
    assert nq <= LANES
    ch = min(SB_CHUNK, blk)

    def body(q_ref, k_ref, v_ref, g_ref, o_ref, dy_ref, dq_ref, dk_ref, dv_ref, dg_ref, dk_scr, dv_scr,
             z_scr, g_scr, lhs_scr, cs_scr, a_scr, dz_scr):
        i = pl.program_id(1)

        @pl.when(i == 0)
        def _():
            dk_scr[...] = jnp.zeros_like(dk_scr)
            dv_scr[...] = jnp.zeros_like(dv_scr)

        row = lax.broadcasted_iota(jnp.int32, (blk, blk), 0)
        col = lax.broadcasted_iota(jnp.int32, (blk, blk), 1)
        lane = lax.broadcasted_iota(jnp.int32, (blk, LANES), 1)
        u_gt = (row > col).astype(MXU_DTYPE)
        u_lt = (row < col).astype(MXU_DTYPE)
        heads = [slice(hh * hd, (hh + 1) * hd) for hh in range(2)]
        qs = [q_ref[:, ls] for ls in heads]
        _, rows_p, rows_d = _sb_pair_tile(i, blk)
        has_prev = i >= 1
        crow = lax.broadcasted_iota(jnp.int32, (ch, blk), 0)
        ccol = lax.broadcasted_iota(jnp.int32, (ch, blk), 1)
        nch = blk // ch
        kcs = [jnp.concatenate([k_ref[rows_p, ls], k_ref[rows_d, ls]], axis=0) for ls in heads]
        dobs = []

        def gate_grads():
            g = g_ref[...]
            dyv = dy_ref[...]
            sgm = _sigmoid(g)
            dg_ref[...] = (dyv * o_ref[...] * (sgm * (1.0 + g * (1.0 - sgm)))).astype(dg_ref.dtype)
            dob = _mx(dyv * (g * sgm))
            dobs.extend(dob[:, ls] for ls in heads)

        def split_rows(hh, c, part, x):
            lhs_scr[hh, pl.ds(part * blk + c * ch, ch), :] = _mx(x)

        def summed_rows(hh, c, part):
            return cs_scr[hh, pl.ds(part * blk + c * ch, ch), :]

        def logits(hh):
            z_scr[hh] = _dot_nt(qs[hh], kcs[hh])

        def weight_grads(hh):
            vc = jnp.concatenate([v_ref[rows_p, heads[hh]], v_ref[rows_d, heads[hh]]], axis=0)
            g_scr[hh] = _dot_nt(dobs[hh], vc)

        def keep_parts(hh):
            r_d, r_all = [], []
            for c in range(nch):
                rows = pl.ds(c * ch, ch)
                z = z_scr[hh, rows, :]
                l1p = jnp.log(1.0 + jnp.exp(-jnp.abs(z)))
                lb = jnp.minimum(z, 0.0) - l1p
                z_scr[hh, rows, :] = lb
                lk = lb - z
                lk_p = lk[:, :blk]
                lk_d = jnp.where(crow + c * ch > ccol, lk[:, blk:], 0.0)
                split_rows(hh, c, 0, lk_p)
                split_rows(hh, c, 1, lk_d)
                r_d.append(_rowsum(lk_d))
                r_all.append(r_d[c] + _rowsum(lk_p))
            return r_d, jnp.concatenate(r_all, axis=0)

        def weights(hh, r_d):
            g_p = []
            for c in range(nch):
                rows = pl.ds(c * ch, ch)
                lb = z_scr[hh, rows, :]
                a_p = jnp.exp(lb[:, :blk] + (summed_rows(hh, c, 0) + jnp.where(has_prev, r_d[c], NO_KEYS)))
                a_d = jnp.where(crow + c * ch > ccol, jnp.exp(lb[:, blk:] + summed_rows(hh, c, 1)), 0.0)
                a = jnp.concatenate([a_p, a_d], axis=1)
                a_scr[hh, rows, :] = _mx(a)
                gm = g_scr[hh, rows, :] * a
                g_scr[hh, rows, :] = gm
                split_rows(hh, c, 0, gm[:, :blk])
                split_rows(hh, c, 1, gm[:, blk:])
                g_p.append(_rowsum(gm[:, :blk]))
            return g_p

        def logit_grads(hh, pg, g_p):
            for c in range(nch):
                rows = pl.ds(c * ch, ch)
                pre = jnp.concatenate([summed_rows(hh, c, 0) + pg[c * ch:(c + 1) * ch],
                                       summed_rows(hh, c, 1) + (pg[c * ch:(c + 1) * ch] + g_p[c])], axis=1)
                gm = g_scr[hh, rows, :]
                dz = gm - (gm + pre) * jnp.exp(z_scr[hh, rows, :])
                dz_p = dz[:, :blk]
                dz_d = jnp.where(crow + c * ch > ccol, dz[:, blk:], 0.0)
                dz_scr[hh, rows, :] = _mx(jnp.concatenate([dz_p, dz_d], axis=1))

        def products(hh, acc):
            ls = heads[hh]
            dk_scr[hh, rows_p, :] += _dot_tn(dz_scr[hh, :, 0:blk], qs[hh])
            dk_scr[hh, rows_d, :] += _dot_tn(dz_scr[hh, :, blk:2 * blk], qs[hh])
            dv_scr[hh, rows_p, :] += _dot_tn(a_scr[hh, :, 0:blk], dobs[hh])
            dv_scr[hh, rows_d, :] += _dot_tn(a_scr[hh, :, blk:2 * blk], dobs[hh])
            dq_ref[:, ls] = ((acc + _dot(dz_scr[hh], kcs[hh])) * SB_SCALE).astype(dq_ref.dtype)

        def suffix_sums(hh):
            cs_scr[hh] = _dot(lhs_scr[hh], u_gt)

        def prefix_sums(hh):
            cs_scr[hh] = _dot(lhs_scr[hh], u_lt)

        def more(n, r0, r1):
            return jnp.logical_and(n < i, jnp.max(jnp.maximum(r0, r1)) > -SB_SKIP)

        gate_grads()
        logits(0)
        weight_grads(0)
        logits(1)
        weight_grads(1)
        rd0, ra0 = keep_parts(0)
        suffix_sums(0)
        rd1, ra1 = keep_parts(1)
        suffix_sums(1)
        go = more(jnp.int32(1), ra0, ra1)
        gp0 = weights(0, rd0)
        prefix_sums(0)
        gp1 = weights(1, rd1)
        prefix_sums(1)

        def scan_block(hh, j, r, rmat):
            start = pl.multiple_of(j * blk, blk)
            _, lk = _sb_logits(qs[hh], k_ref[pl.ds(start, blk), heads[hh]], None)
            return r + _rowsum(lk), jnp.where(lane == j, r, rmat)

        def scan_step(c):
            _, n, r0, rmat0, r1, rmat1 = c
            r0, rmat0 = scan_block(0, i - 1 - n, r0, rmat0)
            r1, rmat1 = scan_block(1, i - 1 - n, r1, rmat1)
            return more(n + 1, r0, r1), n + 1, r0, rmat0, r1, rmat1

        zmat = jnp.zeros((blk, LANES), F32)
        _, n, _, rmat0, _, rmat1 = lax.while_loop(lambda c: c[0], scan_step,
                                                  (go, jnp.int32(1), ra0, zmat, ra1, zmat))
        rmats = (rmat0, rmat1)

        def block(hh, j, pg):
            ls = heads[hh]
            start = pl.multiple_of(j * blk, blk)
            k = k_ref[pl.ds(start, blk), ls]
            lb, lk = _sb_logits(qs[hh], k, None)
            r = _rowsum(jnp.where(lane == j, rmats[hh], 0.0))
            a = _sb_weights(lb, lk, r, u_gt, None)
            gm = _dot_nt(dobs[hh], v_ref[pl.ds(start, blk), ls]) * a
            dzb = _mx(gm - (gm + (pg + _split_dot(gm, u_lt))) * jnp.exp(lb))
            dk_scr[hh, pl.ds(start, blk), :] += _dot_tn(dzb, qs[hh])
            dv_scr[hh, pl.ds(start, blk), :] += _dot_tn(_mx(a), dobs[hh])
            return _dot(dzb, k), pg + _rowsum(gm)

        def step(t, c):
            acc0, pg0, acc1, pg1 = c
            dq0, pg0 = block(0, i - n + t, pg0)
            dq1, pg1 = block(1, i - n + t, pg1)
            return acc0 + dq0, pg0, acc1 + dq1, pg1

        zero = jnp.zeros((blk, 1), F32)
        zacc = jnp.zeros((blk, hd), F32)
        acc0, pg0, acc1, pg1 = lax.fori_loop(0, n - 1, step, (zacc, zero, zacc, zero))
        logit_grads(0, pg0, gp0)
        logit_grads(1, pg1, gp1)
        products(0, acc0)
        products(1, acc1)

        @pl.when(i == nq - 1)
        def _():
            for hh in range(2):
                ls = slice(hh * hd, (hh + 1) * hd)
                dk_ref[:, ls] = dk_scr[hh].astype(dk_ref.dtype)
                dv_ref[:, ls] = dv_scr[hh].astype(dv_ref.dtype)

    qblk = lambda c0: pl.BlockSpec((blk, LANES), lambda p, i: (i, c0 + p))
    full = lambda c0: pl.BlockSpec((s, LANES), lambda p, i: (0, c0 + p))
    half = jax.ShapeDtypeStruct((s, 512), MXU_DTYPE)
    return pl.pallas_call(
        body, name="stickbreak_bwd", grid=(SB_HEADS // 2, nq),
        in_specs=[qblk(0), full(4), full(8), qblk(0), qblk(0), qblk(4)],
        out_specs=[qblk(0), full(0), full(0), qblk(0)],
        out_shape=[half, half, half, half],
        scratch_shapes=[pltpu.VMEM((2, s, hd), F32), pltpu.VMEM((2, s, hd), F32),
                        pltpu.VMEM((2, blk, 2 * blk), F32),
                        pltpu.VMEM((2, blk, 2 * blk), F32),
                        pltpu.VMEM((2, 2 * blk, blk), MXU_DTYPE),
                        pltpu.VMEM((2, 2 * blk, blk), F32),
                        pltpu.VMEM((2, blk, 2 * blk), MXU_DTYPE),
                        pltpu.VMEM((2, blk, 2 * blk), MXU_DTYPE)],
        compiler_params=_params(("arbitrary", "arbitrary")),
    )(qkv, qkv, qkv, sg, o, dy)


def _layer_fwd(layer, x, mod, norm_g, w_in_g, w_out_g, tables, gather=(), loss_head=None):
    shift, scale1p, gate = mod[0:1], 1.0 + mod[1:2], mod[2:3]
    ret, qkv, sg, *gathered = _ln_proj(x, shift, scale1p, norm_g, w_in_g, layer, gather)
    y_r, states = _retention_fwd(ret, tables)
    y_s, o_s = _sb_fwd(qkv, sg)
    x_next = _out_proj(x, gate, y_r, y_s, w_out_g, layer, loss_head)
    saved = (x, shift, scale1p, gate, ret, qkv, sg, y_r, states, y_s, o_s)
    return x_next, saved, gathered


def _layer_bwd(layer, me, dx_out, saved, norm_g, w_in_g, w_out_g, tables):
    x, shift, scale1p, gate, ret, qkv, sg, y_r, states, y_s, o_s = saved
    dy, dw_out, dgate = _out_proj_bwd(dx_out, gate, y_r, y_s, w_out_g, layer)
    dwo_parts = _mx(dw_out.reshape(N_DEV, D_MODEL // N_DEV, D_MODEL))
    d_ret, r_out = _retention_bwd(ret, states, dy, tables, dwo_parts)
    d_q, d_k, d_v, d_g = _sb_bwd(qkv, sg, o_s, dy)
    dproj = (d_ret, d_q, d_k, d_v, d_g)
    dx, dshift, dscale, dnorm_g, h_t = _in_proj_bwd_x(x, dx_out, dproj, shift, scale1p, norm_g, w_in_g, layer)
    r_in = _in_proj_bwd_w(me, h_t, dproj)
    dmod = jnp.concatenate([dshift, dscale, dgate], axis=1)
    return dx, r_in, r_out, dmod, dnorm_g


MESH_IDS = pl.DeviceIdType.MESH
N_PEERS = N_DEV - 1
HBM_SPEC = pl.BlockSpec(memory_space=pl.ANY)


def _my_place():
    return lax.axis_index("x"), lax.axis_index("y"), lax.axis_index("c")


def _linear(px, py, pc):
    return 4 * px + 2 * py + pc


def _all_gather(blocks):
    n_arr = len(blocks)

    def body(*refs):
        start, forward, finish = _gather_plan(refs[:n_arr], refs[n_arr:2 * n_arr], *refs[2 * n_arr:])
        start()
        forward()
        finish()

    return pl.pallas_call(
        body, name="all_gather",
        out_shape=_gathered_shapes(blocks),
        in_specs=[HBM_SPEC] * n_arr, out_specs=[HBM_SPEC] * n_arr,
        scratch_shapes=_gather_sems(n_arr),
    )(*blocks)


def _gathered_shapes(blocks):
    return [jax.ShapeDtypeStruct((N_DEV * b.shape[0], b.shape[1]), b.dtype) for b in blocks]


def _gather_sems(n_arr):
    return [pltpu.SemaphoreType.DMA((n_arr * N_PEERS,)), pltpu.SemaphoreType.DMA((n_arr * N_PEERS,)),
            pltpu.SemaphoreType.DMA((n_arr,))]


def _gather_plan(x_refs, out_refs, send_sems, recv_sems, local_sems):
    n_arr = len(x_refs)
    x, y, c = _my_place()
    me, sibling = (x, y, c), (x, y, 1 - c)
    chips = [(1 - x, y), (x, 1 - y), (1 - x, 1 - y)]

    def rows(a, place):
        m = x_refs[a].shape[0]
        return out_refs[a].at[pl.ds(_linear(*place) * m, m), :]

    def copy(a, k, block, to, src=None):
        return pltpu.make_async_remote_copy(
            src_ref=rows(a, block) if src is None else src, dst_ref=rows(a, block),
            send_sem=send_sems.at[a * N_PEERS + k], recv_sem=recv_sems.at[a * N_PEERS + k],
            device_id=to, device_id_type=MESH_IDS)

    mine = [pltpu.make_async_copy(x_refs[a], rows(a, me), local_sems.at[a]) for a in range(n_arr)]
    first = []
    for a in range(n_arr):
        first.append(copy(a, 0, me, sibling, src=x_refs[a]))
        first += [copy(a, 1 + j, me, (*chip, c), src=x_refs[a]) for j, chip in enumerate(chips)]
    passed = [copy(a, 4 + j, (*chip, c), sibling) for j, chip in enumerate(chips) for a in range(n_arr)]

    def start():
        for cp in mine + first:
            cp.start()

    def forward():
        for j, chip in enumerate(chips):
            for a in range(n_arr):
                copy(a, 1 + j, (*chip, c), me).wait_recv()
                passed[j * n_arr + a].start()

    def finish():
        for a in range(n_arr):
            copy(a, 0, sibling, me).wait_recv()
            for j, chip in enumerate(chips):
                copy(a, 4 + j, (*chip, 1 - c), me).wait_recv()
        for cp in first + passed:
            cp.wait_send()
        for cp in mine:
            cp.wait()

    return start, forward, finish


def _ada_fwd(c_all, w_ada, b_cols):
    cols = w_ada.shape[2]

    def body(c_ref, w_ref, b_ref, ca_ref, mod_ref):
        cv = c_ref[...]
        ca = cv * _sigmoid(cv)
        ca_ref[...] = ca
        cb = _mx(jnp.concatenate([ca, ca], axis=0))
        for l in range(DEPTH):
            mod_ref[l * N_DEV:(l + 1) * N_DEV, :] = _dot(cb, _mx(w_ref[l]))[0:N_DEV] + b_ref[l]

    return pl.pallas_call(
        body, name="ada_fwd",
        out_shape=[jax.ShapeDtypeStruct((N_DEV, D_MODEL), F32),
                   jax.ShapeDtypeStruct((DEPTH * N_DEV, cols), F32)],
        compiler_params=_params(),
    )(c_all, w_ada, b_cols)


def _ada_bwd(c_act_t, dmod_cols):
    cols = dmod_cols.shape[2]

    def body(ca_ref, dm_ref, o_ref):
        ca = _mx(ca_ref[...]).astype(F32)
        for l in range(DEPTH):
            o_ref[l] = jnp.dot(ca, _mx(dm_ref[l]).astype(F32),
                               precision=lax.Precision.HIGHEST, preferred_element_type=F32)

    return pl.pallas_call(
        body, name="ada_bwd",
        out_shape=jax.ShapeDtypeStruct((DEPTH, D_MODEL, cols), F32),
        compiler_params=_params(),
    )(c_act_t, dmod_cols)


def _adamw_store(g, w_ref, m_ref, v_ref, g_ref, d_ref, mo_ref, vo_ref):
    m2 = ADAM_B1 * m_ref[...] + (1.0 - ADAM_B1) * g
    v2 = ADAM_B2 * v_ref[...] + (1.0 - ADAM_B2) * (g * g)
    m_hat = m2 / (1.0 - ADAM_B1 ** ADAM_STEP)
    v_hat = v2 / (1.0 - ADAM_B2 ** ADAM_STEP)
    g_ref[...] = g
    d_ref[...] = -ADAM_LR * (m_hat / (jnp.sqrt(v_hat) + ADAM_EPS) + ADAM_WD * w_ref[...])
    mo_ref[...] = m2
    vo_ref[...] = v2


def _slab_sum(p_ref):
    g = p_ref[0].astype(F32)
    for sl in range(1, p_ref.shape[0]):
        g = g + p_ref[sl].astype(F32)
    return g


def _sum_adamw_layers(parts, w, m, v):
    n_slab, rows, cols = parts[0].shape
    tr = min(256, rows)
    nt = rows // tr

    def body(p0_ref, p1_ref, w_ref, m_ref, v_ref, g_ref, d_ref, mo_ref, vo_ref):
        for l, p_ref in enumerate((p0_ref, p1_ref)):
            @pl.when(pl.program_id(0) == l)
            def _():
                _adamw_store(_slab_sum(p_ref), w_ref, m_ref, v_ref, g_ref, d_ref, mo_ref, vo_ref)

    p_specs = [pl.BlockSpec((n_slab, tr, cols), lambda l, i: (0, i * (1 - l) + (nt - 1) * l, 0)),
               pl.BlockSpec((n_slab, tr, cols), lambda l, i: (0, i * l, 0))]
    blk = pl.BlockSpec((None, tr, cols), lambda l, i: (l, i, 0))
    shp = jax.ShapeDtypeStruct((DEPTH, rows, cols), F32)
    return pl.pallas_call(
        body, name="sum_adamw_layers", grid=(DEPTH, nt),
        in_specs=p_specs + [blk, blk, blk],
        out_specs=[blk, blk, blk, blk],
        out_shape=[shp, shp, shp, shp],
        compiler_params=_params(("arbitrary", "arbitrary")),
    )(parts[0], parts[1], w, m, v)


def _sum_adamw(parts, w, m, v):
    n_slab, rows, cols = parts.shape
    tr = min(256, rows)

    def body(p_ref, w_ref, m_ref, v_ref, g_ref, d_ref, mo_ref, vo_ref):
        _adamw_store(_slab_sum(p_ref), w_ref, m_ref, v_ref, g_ref, d_ref, mo_ref, vo_ref)

    blk = pl.BlockSpec((tr, cols), lambda i: (i, 0))
    shp = jax.ShapeDtypeStruct((rows, cols), F32)
    return pl.pallas_call(
        body, name="sum_adamw", grid=(rows // tr,),
        in_specs=[pl.BlockSpec((n_slab, tr, cols), lambda i: (0, i, 0)), blk, blk, blk],
        out_specs=[blk, blk, blk, blk],
        out_shape=[shp, shp, shp, shp],
        compiler_params=_params(("arbitrary",)),
    )(parts, w, m, v)


SMALL_ROWS = 16


def kernel(x, c, norm_g, w_ada, b_ada, w_in, w_out, final_g, loss_target, m_norm_g, m_w_ada, m_b_ada, m_w_in, m_w_out, m_final_g, v_norm_g, v_w_ada, v_b_ada, v_w_in, v_w_out, v_final_g):
    me = _linear(*_my_place())
    in_cols = w_in.shape[2]
    out_rows = w_out.shape[1]
    ada_cols = w_ada.shape[2]

    w_in_m, w_out_m = _mx(w_in), _mx(w_out)
    g_in, g_out, g_c = _all_gather([w_in_m[0], w_out_m[0], jnp.broadcast_to(c, (8, D_MODEL))])
    w_in_g = [g_in.reshape(N_DEV, 1, D_MODEL, in_cols), None]
    w_out_g = [g_out.reshape(N_DEV, 1, out_rows, D_MODEL), None]
    c_all = g_c.reshape(N_DEV, 8, D_MODEL)[:, 0]

    b_cols = lax.dynamic_slice_in_dim(b_ada, me * ada_cols, ada_cols, axis=1)[:, None, :]
    c_act, mod_cols = _ada_fwd(c_all, w_ada, b_cols)
    (g_mod,) = _all_gather([mod_cols])
    g_mod = g_mod.reshape(N_DEV, DEPTH, N_DEV, ada_cols)
    mod = lax.dynamic_index_in_dim(g_mod, me, axis=2, keepdims=False)
    mod = mod.transpose(1, 0, 2).reshape(DEPTH, 3, D_MODEL)

    tables = _ret_tables(x.shape[1])
    h = x[0]
    saved = []
    for l in range(DEPTH):
        nxt = (w_in_m[l + 1], w_out_m[l + 1]) if l + 1 < DEPTH else ()
        head = None if nxt else (final_g[None], loss_target[0])
        h, sv, gathered = _layer_fwd(0, h, mod[l], norm_g[l:l + 1], w_in_g[l], w_out_g[l], tables, nxt, head)
        if nxt:
            w_in_g[l + 1] = gathered[0].reshape(N_DEV, 1, D_MODEL, in_cols)
            w_out_g[l + 1] = gathered[1].reshape(N_DEV, 1, out_rows, D_MODEL)
        saved.append(sv)
    dx, loss_part, dfg = h
    r_in, r_out, small = [None] * DEPTH, [None] * DEPTH, [None] * DEPTH
    for l in reversed(range(DEPTH)):
        dx, r_in[l], r_out[l], dmod, dng = _layer_bwd(0, me, dx, saved[l], norm_g[l:l + 1], w_in_g[l], w_out_g[l], tables)
        small[l] = (dmod.reshape(3, D_MODEL), dng)

    pad = jnp.zeros((SMALL_ROWS - 10, D_MODEL), F32)
    small_block = jnp.concatenate([small[0][0], small[1][0], small[0][1], small[1][1], dfg,
                                   jnp.broadcast_to(loss_part, (1, D_MODEL)), pad], axis=0)
    (g_small,) = _all_gather([small_block])
    g_small = g_small.reshape(N_DEV, SMALL_ROWS, D_MODEL)

    def small_pack(b, n, f, fill):
        return jnp.concatenate([b.reshape(6, D_MODEL), n, f[None],
                                jnp.full((SMALL_ROWS - 9, D_MODEL), fill, F32)], axis=0)

    s_g, s_d, s_m, s_v = _sum_adamw(g_small, small_pack(b_ada, norm_g, final_g, 0.0),
                                    small_pack(m_b_ada, m_norm_g, m_final_g, 0.0),
                                    small_pack(v_b_ada, v_norm_g, v_final_g, 1.0))
    loss = s_g[9, 0]

    def small_unpack(a):
        return a[0:6].reshape(DEPTH, 3 * D_MODEL), a[6:8], a[8]

    dmod_all = g_small[:, 0:6].reshape(N_DEV, DEPTH, 3 * D_MODEL).transpose(1, 0, 2)
    dmod_cols = lax.dynamic_slice_in_dim(dmod_all, me * ada_cols, ada_cols, axis=2)
    g_ada = _ada_bwd(c_act.T, dmod_cols).reshape(1, DEPTH * D_MODEL, ada_cols)
    ada = _sum_adamw(g_ada, *[a.reshape(DEPTH * D_MODEL, ada_cols) for a in (w_ada, m_w_ada, v_w_ada)])
    ada = [a.reshape(DEPTH, D_MODEL, ada_cols) for a in ada]

    win = _sum_adamw_layers(r_in, w_in, m_w_in, v_w_in)
    wout = _sum_adamw_layers(r_out, w_out, m_w_out, v_w_out)

    outs = [loss, dx[None]]
    for k in range(4):
        b, n, f = small_unpack((s_g, s_d, s_m, s_v)[k])
        outs += [n, ada[k], b, win[k], wout[k], f]
    return tuple(outs)
```

```python
import functools

import jax
import jax.numpy as jnp
from jax import lax
from jax.experimental import pallas as pl
from jax.experimental.pallas import tpu as pltpu

F32 = jnp.float32
MXU_DTYPE = jnp.bfloat16

D_MODEL = 1024
DEPTH = 2
N_DEV = 8
CHUNK = 64
D_RET = 512
D_SB = 512
RET_HEADS = 4
RET_HEAD_DIM = 128
SB_HEADS = 8
SB_HEAD_DIM = 64
D_IN = 4096
ROPE_BASE = 10000.0
EPS = 1e-6
SB_SCALE = SB_HEAD_DIM ** -0.5
RET_KSCALE = RET_HEAD_DIM ** -0.5

ADAM_LR = 0.001
ADAM_B1 = 0.9
ADAM_B2 = 0.999
ADAM_EPS = 1e-08
ADAM_WD = 0.01
ADAM_STEP = 10

V7X_VMEM_BYTES = 64 * 2 ** 20
VMEM_LIMIT = V7X_VMEM_BYTES - 8 * 2 ** 20
LANES = 128

_NT = (((1,), (1,)), ((), ()))
_TN = (((0,), (0,)), ((), ()))


def _dot(a, b):
    return jnp.dot(a, b, preferred_element_type=F32)


def _dot_nt(a, b):
    return lax.dot_general(a, b, _NT, preferred_element_type=F32)


def _dot_tn(a, b):
    return lax.dot_general(a, b, _TN, preferred_element_type=F32)


def _mx(x):
    return x.astype(MXU_DTYPE)


def _sigmoid(x):
    return 1.0 / (1.0 + jnp.exp(-x))


def _params(sem=None):
    return pltpu.CompilerParams(dimension_semantics=sem, vmem_limit_bytes=VMEM_LIMIT)


def _row_tile(s):
    return min(512, s)


def _w_in_spec(w_in_g, layer):
    return pl.BlockSpec((N_DEV, None) + w_in_g.shape[2:], lambda i: (0, layer, 0, 0))


def _w_out_spec(w_out_g, layer):
    return pl.BlockSpec((N_DEV, None) + w_out_g.shape[2:], lambda i: (0, layer, 0, 0))


def _ln_proj(x, shift, scale1p, g, w_in_g, layer, gather=()):
    s = x.shape[0]
    ts = _row_tile(s)
    ns = s // ts
    n_g = len(gather)

    def body(x_ref, sh_ref, sc_ref, g_ref, w_ref, *rest):
        ret_ref, qkv_ref, sg_ref = rest[n_g:n_g + 3]
        if n_g:
            start, forward, finish = _gather_plan(rest[:n_g], rest[n_g + 3:2 * n_g + 3], *rest[2 * n_g + 3:])
            i = pl.program_id(0)
            pl.when(i == 0)(start)
            pl.when(i == max(ns - 4, 0))(forward)
        xv = x_ref[...]
        rstd = lax.rsqrt(jnp.mean(xv * xv, axis=-1, keepdims=True) + EPS)
        h = (xv * rstd * g_ref[...]) * sc_ref[...] + sh_ref[...]
        hb = _mx(h)
        for n in range(4):
            ret_ref[:, n * 512:(n + 1) * 512] = _dot(hb, w_ref[n])
        qkv_ref[:, 0:512] = _mx(_dot(hb, w_ref[4]) * SB_SCALE)
        qkv_ref[:, 512:1024] = _mx(_dot(hb, w_ref[5]))
        qkv_ref[:, 1024:1536] = _mx(_dot(hb, w_ref[6]))
        sg_ref[...] = _dot(hb, w_ref[7])
        if n_g:
            pl.when(i == ns - 1)(finish)

    vec = pl.BlockSpec((1, D_MODEL), lambda i: (0, 0))
    return pl.pallas_call(
        body, name="ln_proj_gather" if n_g else "ln_proj", grid=(ns,),
        in_specs=[pl.BlockSpec((ts, D_MODEL), lambda i: (i, 0)), vec, vec, vec,
                  _w_in_spec(w_in_g, layer)] + [HBM_SPEC] * n_g,
        out_specs=[pl.BlockSpec((ts, 2048), lambda i: (i, 0)),
                   pl.BlockSpec((ts, 1536), lambda i: (i, 0)),
                   pl.BlockSpec((ts, 512), lambda i: (i, 0))] + [HBM_SPEC] * n_g,
        out_shape=[jax.ShapeDtypeStruct((s, 2048), F32),
                   jax.ShapeDtypeStruct((s, 1536), MXU_DTYPE),
                   jax.ShapeDtypeStruct((s, 512), F32)] + _gathered_shapes(gather),
        scratch_shapes=_gather_sems(n_g) if n_g else (),
        compiler_params=_params(("arbitrary",)),
    )(x, shift, scale1p, g, w_in_g, *gather)


def _w_out_halves(w_ref):
    half = N_DEV // 2
    return (w_ref[0:half].reshape(D_RET, D_MODEL), w_ref[half:N_DEV].reshape(D_SB, D_MODEL))


def _out_proj(x, gate, y_r, y_s, w_out_g, layer, loss_head=None):
    s = x.shape[0]
    ts = min(2 * _row_tile(s), s)

    def layer_out(x_ref, gate_ref, yr_ref, ys_ref, w_ref):
        w_r, w_s = _w_out_halves(w_ref)
        return x_ref[...] + gate_ref[...] * (_dot(yr_ref[...], w_r) + _dot(ys_ref[...], w_s))

    def body(x_ref, gate_ref, yr_ref, ys_ref, w_ref, o_ref):
        o_ref[...] = layer_out(x_ref, gate_ref, yr_ref, ys_ref, w_ref)

    def body_loss(x_ref, gate_ref, yr_ref, ys_ref, w_ref, fg_ref, t_ref, dx_ref, loss_ref, dfg_ref):
        @pl.when(pl.program_id(0) == 0)
        def _():
            loss_ref[...] = jnp.zeros_like(loss_ref)
            dfg_ref[...] = jnp.zeros_like(dfg_ref)

        xv = layer_out(x_ref, gate_ref, yr_ref, ys_ref, w_ref)
        fgv = fg_ref[...]
        rstd = lax.rsqrt(jnp.mean(xv * xv, axis=-1, keepdims=True) + EPS)
        xn = xv * rstd
        err = xn * fgv - t_ref[...]
        tok = jnp.mean(err * err, axis=-1, keepdims=True)
        loss_ref[...] += 0.5 * jnp.sum(tok, axis=0, keepdims=True)
        dy = err * (1.0 / D_MODEL)
        dfg_ref[...] += jnp.sum(dy * xn, axis=0, keepdims=True)
        dxn = dy * fgv
        dx_ref[...] = rstd * (dxn - xn * jnp.mean(dxn * xn, axis=-1, keepdims=True))

    rows = pl.BlockSpec((ts, D_MODEL), lambda i: (i, 0))
    vec = pl.BlockSpec((1, D_MODEL), lambda i: (0, 0))
    in_specs = [rows, vec, pl.BlockSpec((ts, 512), lambda i: (i, 0)), pl.BlockSpec((ts, 512), lambda i: (i, 0)),
                _w_out_spec(w_out_g, layer)]
    if loss_head is None:
        return pl.pallas_call(
            body, name="out_proj", grid=(s // ts,), in_specs=in_specs, out_specs=rows,
            out_shape=jax.ShapeDtypeStruct((s, D_MODEL), F32),
            compiler_params=_params(("arbitrary",)),
        )(x, gate, y_r, y_s, w_out_g)
    return pl.pallas_call(
        body_loss, name="out_proj_loss", grid=(s // ts,), in_specs=in_specs + [vec, rows],
        out_specs=[rows, pl.BlockSpec((1, 1), lambda i: (0, 0)), vec],
        out_shape=[jax.ShapeDtypeStruct((s, D_MODEL), F32),
                   jax.ShapeDtypeStruct((1, 1), F32),
                   jax.ShapeDtypeStruct((1, D_MODEL), F32)],
        compiler_params=_params(("arbitrary",)),
    )(x, gate, y_r, y_s, w_out_g, *loss_head)


def _out_proj_bwd(dx_out, gate, y_r, y_s, w_out_g, layer):
    s = dx_out.shape[0]
    ts = min(2 * _row_tile(s), s)
    ns = s // ts

    def body(dx_ref, gate_ref, yr_ref, ys_ref, w_ref, dy_ref, dw_ref, dgate_ref):
        i = pl.program_id(0)

        @pl.when(i == 0)
        def _():
            dw_ref[...] = jnp.zeros_like(dw_ref)

        dxv = dx_ref[...]
        dt = _mx(dxv * gate_ref[...])
        dxb = _mx(dxv)
        w_r, w_s = _w_out_halves(w_ref)
        dy_ref[:, 0:512] = _dot_nt(dt, w_r)
        dy_ref[:, 512:1024] = _dot_nt(dt, w_s)
        dw_ref[0:512, :] += _dot_tn(yr_ref[...], dxb)
        dw_ref[512:1024, :] += _dot_tn(ys_ref[...], dxb)

        @pl.when(i == ns - 1)
        def _():
            m_r, m_s = dw_ref[0:512, :], dw_ref[512:1024, :]
            dgate_ref[...] = (jnp.sum(w_r.astype(F32) * m_r, axis=0, keepdims=True)
                              + jnp.sum(w_s.astype(F32) * m_s, axis=0, keepdims=True))
            dw_ref[...] = dw_ref[...] * gate_ref[...]

    return pl.pallas_call(
        body, name="out_proj_bwd", grid=(ns,),
        in_specs=[pl.BlockSpec((ts, D_MODEL), lambda i: (i, 0)),
                  pl.BlockSpec((1, D_MODEL), lambda i: (0, 0)),
                  pl.BlockSpec((ts, 512), lambda i: (i, 0)),
                  pl.BlockSpec((ts, 512), lambda i: (i, 0)),
                  _w_out_spec(w_out_g, layer)],
        out_specs=[pl.BlockSpec((ts, D_MODEL), lambda i: (i, 0)),
                   pl.BlockSpec((D_MODEL, D_MODEL), lambda i: (0, 0)),
                   pl.BlockSpec((1, D_MODEL), lambda i: (0, 0))],
        out_shape=[jax.ShapeDtypeStruct((s, D_MODEL), F32),
                   jax.ShapeDtypeStruct((D_MODEL, D_MODEL), F32),
                   jax.ShapeDtypeStruct((1, D_MODEL), F32)],
        compiler_params=_params(("arbitrary",)),
    )(dx_out, gate, y_r, y_s, w_out_g)


def _scatter_plan(parts_ref, recv_ref, send_sems, recv_sems, local_sem):
    px, py, pc = _my_place()
    mine = _linear(px, py, pc)

    def copy(r):
        peer = (1 - px if r & 4 else px, 1 - py if r & 2 else py, 1 - pc if r & 1 else pc)
        return pltpu.make_async_remote_copy(
            src_ref=parts_ref.at[_linear(*peer)], dst_ref=recv_ref.at[mine],
            send_sem=send_sems.at[r - 1], recv_sem=recv_sems.at[r - 1],
            device_id=peer, device_id_type=MESH_IDS)

    own = pltpu.make_async_copy(parts_ref.at[mine], recv_ref.at[mine], local_sem.at[0])

    def start():
        own.start()
        for r in range(1, N_DEV):
            copy(r).start()

    def finish():
        for r in range(1, N_DEV):
            copy(r).wait_recv()
            copy(r).wait_send()
        own.wait()

    return start, finish


def _in_proj_bwd_x(x, dx_out, dproj, shift, scale1p, g, w_in_g, layer):
    s = x.shape[0]
    ts = _row_tile(s)
    ns = s // ts
    nb = D_IN // N_DEV

    def body(x_ref, dxo_ref, dr_ref, d4_ref, d5_ref, d6_ref, d7_ref, sh_ref, sc_ref, g_ref, w_ref,
             dx_ref, dsh_ref, dsc_ref, dg_ref, ht_ref):
        i = pl.program_id(0)

        @pl.when(i == 0)
        def _():
            dsh_ref[...] = jnp.zeros_like(dsh_ref)
            dsc_ref[...] = jnp.zeros_like(dsc_ref)
            dg_ref[...] = jnp.zeros_like(dg_ref)

        dh = _dot_nt(dr_ref[:, 0:nb], w_ref[0])
        for n in range(1, 4):
            dh += _dot_nt(dr_ref[:, n * nb:(n + 1) * nb], w_ref[n])
        for n, d_ref in zip(range(4, N_DEV), (d4_ref, d5_ref, d6_ref, d7_ref)):
            dh += _dot_nt(d_ref[...], w_ref[n])
        xv = x_ref[...]
        gv = g_ref[...]
        scv = sc_ref[...]
        rstd = lax.rsqrt(jnp.mean(xv * xv, axis=-1, keepdims=True) + EPS)
        xn = xv * rstd
        xg = xn * gv
        ht_ref[...] = _mx((xg * scv + sh_ref[...]).T)
        dsh_ref[...] += jnp.sum(dh, axis=0, keepdims=True)
        dsc_ref[...] += jnp.sum(dh * xg, axis=0, keepdims=True)
        dhs = dh * scv
        dg_ref[...] += jnp.sum(dhs * xn, axis=0, keepdims=True)
        dxn = dhs * gv
        dx_ref[...] = rstd * (dxn - xn * jnp.mean(dxn * xn, axis=-1, keepdims=True)) + dxo_ref[...]

    vec = pl.BlockSpec((1, D_MODEL), lambda i: (0, 0))
    return pl.pallas_call(
        body, name="in_proj_bwd_x", grid=(ns,),
        in_specs=[pl.BlockSpec((ts, D_MODEL), lambda i: (i, 0)),
                  pl.BlockSpec((ts, D_MODEL), lambda i: (i, 0)),
                  pl.BlockSpec((ts, 4 * nb), lambda i: (i, 0))]
                 + [pl.BlockSpec((ts, nb), lambda i: (i, 0))] * 4
                 + [vec, vec, vec, _w_in_spec(w_in_g, layer)],
        out_specs=[pl.BlockSpec((ts, D_MODEL), lambda i: (i, 0)), vec, vec, vec,
                   pl.BlockSpec((D_MODEL, ts), lambda i: (0, i))],
        out_shape=[jax.ShapeDtypeStruct((s, D_MODEL), F32),
                   jax.ShapeDtypeStruct((1, D_MODEL), F32),
                   jax.ShapeDtypeStruct((1, D_MODEL), F32),
                   jax.ShapeDtypeStruct((1, D_MODEL), F32),
                   jax.ShapeDtypeStruct((D_MODEL, s), MXU_DTYPE)],
        compiler_params=_params(("arbitrary",)),
    )(x, dx_out, *dproj, shift, scale1p, g, w_in_g)


def _in_proj_bwd_w(me, h_t, dproj, gather=()):
    s = h_t.shape[1]
    ts = min(4 * _row_tile(s), s)
    ns = s // ts
    nb = D_IN // N_DEV
    n_chip = N_DEV // 2
    n_g = len(gather)

    def flip_bits(j):
        return jnp.where(j == 0, 4, jnp.where(j == 1, 2, jnp.where(j == 2, 6, 0)))

    def slab_of(t, me_ref):
        return jnp.bitwise_xor(me_ref[0], flip_bits(t // 2) + 1 - t % 2)

    def body(me_ref, ht_ref, dr_ref, d4_ref, d5_ref, d6_ref, d7_ref, *rest):
        rin_ref = rest[n_g]
        (acc, stage, pre_buf, pre_send, pre_recv, sum_send, sum_recv, local_sem) = rest[2 * n_g + 1:2 * n_g + 9]
        t = pl.program_id(0)
        i = pl.program_id(1)
        if n_g:
            g_start, g_forward, g_finish = _gather_plan(rest[:n_g], rest[n_g + 1:2 * n_g + 1], *rest[2 * n_g + 9:])
            pl.when(jnp.logical_and(t == 0, i == 0))(g_start)
            pl.when(jnp.logical_and(t == N_DEV // 2, i == 0))(g_forward)
        j = t // 2
        summing = t % 2 == 1
        slab = slab_of(t, me_ref)
        px, py, pc = _my_place()

        def pre_copy(jj):
            return pltpu.make_async_remote_copy(
                src_ref=stage.at[jj % 2], dst_ref=pre_buf.at[jj],
                send_sem=pre_send.at[jj], recv_sem=pre_recv.at[jj],
                device_id=(px, py, 1 - pc), device_id_type=MESH_IDS)

        def sum_copy(jj):
            fx = jnp.logical_or(jj == 0, jj == 2)
            fy = jnp.logical_or(jj == 1, jj == 2)
            return pltpu.make_async_remote_copy(
                src_ref=stage.at[2 + jj % 2], dst_ref=rin_ref.at[jj],
                send_sem=sum_send.at[jj], recv_sem=sum_recv.at[jj],
                device_id=(jnp.where(fx, 1 - px, px), jnp.where(fy, 1 - py, py), pc), device_id_type=MESH_IDS)

        own = pltpu.make_async_copy(stage.at[3], rin_ref.at[n_chip - 1], local_sem.at[0])

        @pl.when(i == 0)
        def _():
            acc[...] = jnp.zeros_like(acc)

        @pl.when(slab < 4)
        def _():
            acc[...] += _dot(ht_ref[...], dr_ref[...])

        for n, d_ref in zip(range(4, N_DEV), (d4_ref, d5_ref, d6_ref, d7_ref)):
            @pl.when(slab == n)
            def _():
                acc[...] += _dot(ht_ref[...], d_ref[...])

        @pl.when(jnp.logical_and(i == ns - 1, jnp.logical_not(summing)))
        def _():
            @pl.when(j >= 2)
            def _():
                pre_copy(j - 2).wait_send()

            stage[j % 2] = acc[...].astype(stage.dtype)
            pre_copy(j).start()

        @pl.when(jnp.logical_and(i == ns - 1, summing))
        def _():
            pre_copy(j).wait_recv()

            @pl.when(j >= 2)
            def _():
                sum_copy(j - 2).wait_send()

            stage[2 + j % 2] = (acc[...] + pre_buf[j].astype(F32)).astype(stage.dtype)

            @pl.when(j < n_chip - 1)
            def _():
                sum_copy(j).start()

            @pl.when(j == n_chip - 1)
            def _():
                own.start()
                pre_copy(n_chip - 2).wait_send()
                pre_copy(n_chip - 1).wait_send()
                sum_copy(n_chip - 2).wait_send()
                for jj in range(n_chip - 1):
                    sum_copy(jj).wait_recv()
                own.wait()
                if n_g:
                    g_finish()

    def part_rows(n, t, i, me_ref):
        return jnp.where(slab_of(t, me_ref) == n, i, ns - 1), 0

    return pl.pallas_call(
        body, name="in_proj_bwd_w",
        grid_spec=pltpu.PrefetchScalarGridSpec(
            num_scalar_prefetch=1, grid=(N_DEV, ns),
            in_specs=[pl.BlockSpec((D_MODEL, ts), lambda t, i, me_ref: (0, i)),
                      pl.BlockSpec((ts, nb), lambda t, i, me_ref: (
                          jnp.where(slab_of(t, me_ref) < 4, i, ns - 1), jnp.minimum(slab_of(t, me_ref), 3)))]
                     + [pl.BlockSpec((ts, nb), functools.partial(part_rows, n)) for n in range(4, N_DEV)]
                     + [HBM_SPEC] * n_g,
            out_specs=[HBM_SPEC] * (1 + n_g),
            scratch_shapes=[pltpu.VMEM((D_MODEL, nb), F32),
                            pltpu.VMEM((4, D_MODEL, nb), MXU_DTYPE),
                            pltpu.VMEM((n_chip, D_MODEL, nb), MXU_DTYPE),
                            pltpu.SemaphoreType.DMA((n_chip,)), pltpu.SemaphoreType.DMA((n_chip,)),
                            pltpu.SemaphoreType.DMA((n_chip - 1,)), pltpu.SemaphoreType.DMA((n_chip - 1,)),
                            pltpu.SemaphoreType.DMA((1,))] + (_gather_sems(n_g) if n_g else [])),
        out_shape=[jax.ShapeDtypeStruct((n_chip, D_MODEL, nb), MXU_DTYPE)] + _gathered_shapes(gather),
        compiler_params=_params(("arbitrary", "arbitrary")),
    )(jnp.reshape(me, (1,)).astype(jnp.int32), h_t, *dproj, *gather)


RET_TILE = 256


def _ret_tables(s):
    t = min(RET_TILE, s)
    half = RET_HEAD_DIM // 2
    pos = jnp.arange(s, dtype=F32)
    inv = ROPE_BASE ** (-jnp.arange(half, dtype=F32) / half)
    ang = pos[:, None] * jnp.concatenate([inv, inv])[None, :]
    cos2 = jnp.cos(ang)
    sin2 = jnp.sin(ang) * jnp.concatenate([-jnp.ones((half,), F32), jnp.ones((half,), F32)])[None, :]
    lg = jnp.log1p(-(2.0 ** (-5.0 - jnp.arange(RET_HEADS, dtype=F32))))[:, None, None]
    n = jnp.arange(t)
    dist = (n[:, None] - n[None, :]).astype(F32)[None]
    cn = (n // CHUNK)[:, None]
    cm = (n // CHUNK)[None, :]
    mask = jnp.where((cn == cm)[None], jnp.exp(jnp.abs(dist) * lg),
                     jnp.where((cm < cn)[None], jnp.exp(dist * lg), 0.0))
    nf = n.astype(F32)[None, :, None]
    dq = jnp.broadcast_to(jnp.exp((nf + 1.0) * lg), (RET_HEADS, t, LANES))
    dk = jnp.broadcast_to(jnp.exp((t - 1.0 - nf) * lg), (RET_HEADS, t, LANES))
    gt = jnp.broadcast_to(jnp.exp(float(t) * lg), (RET_HEADS, 1, LANES))
    return cos2, sin2, mask, dq, dk, gt


def _roll_half(x):
    return pltpu.roll(x, RET_HEAD_DIM // 2, 1)


def _ret_heads_fwd(ret_ref, cos, sin, m_ref, dq_ref, dk_ref, s0):
    hd = RET_HEAD_DIM
    heads = range(RET_HEADS)
    qb, kb, vb, kdb = [], [], [], []
    for h in heads:
        q = ret_ref[:, h * hd:(h + 1) * hd]
        k = ret_ref[:, 512 + h * hd:512 + (h + 1) * hd]
        kr = (k * cos + _roll_half(k) * sin) * RET_KSCALE
        qb.append(_mx(q * cos + _roll_half(q) * sin))
        kb.append(_mx(kr))
        kdb.append(_mx(kr * dk_ref[h]))
        vb.append(_mx(ret_ref[:, 1024 + h * hd:1024 + (h + 1) * hd]))
    p = [_dot_nt(qb[h], kb[h]) for h in heads]
    cross = [_dot(qb[h], _mx(s0[h])) for h in heads]
    pb = [_mx(p[h] * m_ref[h]) for h in heads]
    o = [_dot(pb[h], vb[h]) + cross[h] * dq_ref[h] for h in heads]
    gn, rstd = [], []
    for h in heads:
        oc = o[h] - jnp.mean(o[h], axis=-1, keepdims=True)
        rstd.append(lax.rsqrt(jnp.mean(oc * oc, axis=-1, keepdims=True) + EPS))
        gn.append(oc * rstd[h])
    return qb, kb, vb, pb, kdb, gn, rstd


def _retention_fwd(ret, tables):
    cos2, sin2, mask, dq, dk, gt = tables
    s = ret.shape[0]
    t = mask.shape[1]
    nt = s // t
    hd = RET_HEAD_DIM

    def body(ret_ref, cos_ref, sin_ref, m_ref, dq_ref, dk_ref, gt_ref, y_ref, st_ref, s_scr):
        i = pl.program_id(0)

        @pl.when(i == 0)
        def _():
            s_scr[...] = jnp.zeros_like(s_scr)

        s0 = [s_scr[h] for h in range(RET_HEADS)]
        _, _, vb, _, kdb, gn, _ = _ret_heads_fwd(ret_ref, cos_ref[...], sin_ref[...], m_ref, dq_ref, dk_ref, s0)
        kv = [_dot_tn(kdb[h], vb[h]) for h in range(RET_HEADS)]
        for h in range(RET_HEADS):
            g = ret_ref[:, 1536 + h * hd:1536 + (h + 1) * hd]
            st_ref[h] = s0[h]
            y_ref[:, h * hd:(h + 1) * hd] = (gn[h] * (g * _sigmoid(g))).astype(y_ref.dtype)
            s_scr[h] = s0[h] * gt_ref[h] + kv[h]

    full3 = lambda a: pl.BlockSpec(a.shape, lambda i: (0, 0, 0))
    return pl.pallas_call(
        body, name="retention_fwd", grid=(nt,),
        in_specs=[pl.BlockSpec((t, 2048), lambda i: (i, 0)),
                  pl.BlockSpec((t, LANES), lambda i: (i, 0)),
                  pl.BlockSpec((t, LANES), lambda i: (i, 0)),
                  full3(mask), full3(dq), full3(dk), full3(gt)],
        out_specs=[pl.BlockSpec((t, 512), lambda i: (i, 0)),
                   pl.BlockSpec((None, RET_HEADS, hd, hd), lambda i: (i, 0, 0, 0))],
        out_shape=[jax.ShapeDtypeStruct((s, 512), MXU_DTYPE),
                   jax.ShapeDtypeStruct((nt, RET_HEADS, hd, hd), F32)],
        scratch_shapes=[pltpu.VMEM((RET_HEADS, hd, hd), F32)],
        compiler_params=_params(("arbitrary",)),
    )(ret, cos2, sin2, mask, dq, dk, gt)


def _retention_bwd(ret, states, dy, tables, dwo_parts):
    cos2, sin2, mask, dq, dk, gt = tables
    s = ret.shape[0]
    t = mask.shape[1]
    nt = s // t
    hd = RET_HEAD_DIM

    def body(ret_ref, st_ref, dy_ref, cos_ref, sin_ref, m_ref, dq_ref, dk_ref, gt_ref, dwo_ref,
             d_ref, rout_ref, ds_scr, send_sems, recv_sems, local_sem):
        i = pl.program_id(0)
        start, finish = _scatter_plan(dwo_ref, rout_ref, send_sems, recv_sems, local_sem)

        @pl.when(i == 0)
        def _():
            start()
            ds_scr[...] = jnp.zeros_like(ds_scr)

        cos = cos_ref[...]
        sin = sin_ref[...]
        heads = range(RET_HEADS)
        s0 = [st_ref[h] for h in heads]
        ds = [ds_scr[h] for h in heads]
        dsb = [_mx(ds[h]) for h in heads]
        qb, kb, vb, pb, kdb, gn, rstd = _ret_heads_fwd(ret_ref, cos, sin, m_ref, dq_ref, dk_ref, s0)
        dob, dodb = [], []
        for h in heads:
            g = ret_ref[:, 1536 + h * hd:1536 + (h + 1) * hd]
            dyv = dy_ref[:, h * hd:(h + 1) * hd]
            sg = _sigmoid(g)
            d_ref[:, 1536 + h * hd:1536 + (h + 1) * hd] = (
                dyv * gn[h] * (sg * (1.0 + g * (1.0 - sg)))).astype(d_ref.dtype)
            dgn = dyv * (g * sg)
            do = rstd[h] * (dgn - jnp.mean(dgn, axis=-1, keepdims=True)
                            - gn[h] * jnp.mean(dgn * gn[h], axis=-1, keepdims=True))
            dob.append(_mx(do))
            dodb.append(_mx(do * dq_ref[h]))
        dp = [_dot_nt(dob[h], vb[h]) for h in heads]
        dv = [_dot_tn(pb[h], dob[h]) + _dot(kdb[h], dsb[h]) for h in heads]
        dq_cross = [_dot_nt(dodb[h], _mx(s0[h])) for h in heads]
        dk_cross = [_dot_nt(vb[h], dsb[h]) for h in heads]
        ds_new = [_dot_tn(qb[h], dodb[h]) for h in heads]
        dpb = [_mx(dp[h] * m_ref[h]) for h in heads]
        dqr = [_dot(dpb[h], kb[h]) + dq_cross[h] for h in heads]
        dkr = [(_dot_tn(dpb[h], qb[h]) + dk_cross[h] * dk_ref[h]) * RET_KSCALE for h in heads]
        for h in heads:
            d_ref[:, 1024 + h * hd:1024 + (h + 1) * hd] = dv[h].astype(d_ref.dtype)
            d_ref[:, h * hd:(h + 1) * hd] = (dqr[h] * cos + _roll_half(dqr[h] * sin)).astype(d_ref.dtype)
            d_ref[:, 512 + h * hd:512 + (h + 1) * hd] = (
                dkr[h] * cos + _roll_half(dkr[h] * sin)).astype(d_ref.dtype)
            ds_scr[h] = ds[h] * gt_ref[h] + ds_new[h]
        pl.when(i == nt - 1)(finish)

    full3 = lambda a: pl.BlockSpec(a.shape, lambda i: (0, 0, 0))
    rev = lambda i: (nt - 1 - i, 0)
    return pl.pallas_call(
        body, name="retention_bwd", grid=(nt,),
        in_specs=[pl.BlockSpec((t, 2048), rev),
                  pl.BlockSpec((None, RET_HEADS, hd, hd), lambda i: (nt - 1 - i, 0, 0, 0)),
                  pl.BlockSpec((t, 512), rev),
                  pl.BlockSpec((t, LANES), rev),
                  pl.BlockSpec((t, LANES), rev),
                  full3(mask), full3(dq), full3(dk), full3(gt), HBM_SPEC],
        out_specs=[pl.BlockSpec((t, 2048), rev), HBM_SPEC],
        out_shape=[jax.ShapeDtypeStruct((s, 2048), MXU_DTYPE),
                   jax.ShapeDtypeStruct(dwo_parts.shape, dwo_parts.dtype)],
        scratch_shapes=[pltpu.VMEM((RET_HEADS, hd, hd), F32),
                        pltpu.SemaphoreType.DMA((N_PEERS,)), pltpu.SemaphoreType.DMA((N_PEERS,)),
                        pltpu.SemaphoreType.DMA((1,))],
        compiler_params=_params(("arbitrary",)),
    )(ret, states, dy, cos2, sin2, mask, dq, dk, gt, dwo_parts)


SB_BLOCK = 256
SB_CHUNK = 32


SB_SKIP = 104.0
NO_KEYS = -1e30


def _split_dots(xs, u):
    parts = []
    for x in xs:
        hi = lax.bitcast_convert_type(lax.bitcast_convert_type(x, jnp.uint32) & jnp.uint32(0xFFFF0000), F32)
        parts += [_mx(hi), _mx(x - hi)]
    out = _dot(jnp.concatenate(parts, axis=0), u)
    n = xs[0].shape[0]
    return [out[2 * k * n:(2 * k + 1) * n] + out[(2 * k + 1) * n:(2 * k + 2) * n] for k in range(len(xs))]


def _split_dot(x, u):
    return _split_dots([x], u)[0]


def _sb_pair_weights(lb, lk, allowed, u_gt):
    blk = lb.shape[0]
    lk_p, lk_d = lk[:, :blk], lk[:, blk:]
    r_d = _rowsum(lk_d)
    cs_p, cs_d = _split_dots([lk_p, lk_d], u_gt)
    a = jnp.exp(lb + jnp.concatenate([cs_p + r_d, cs_d], axis=1))
    return jnp.where(allowed, a, 0.0), r_d, r_d + _rowsum(lk_p)


def _sb_logits(q, k, causal):
    z = _dot_nt(q, k)
    l1p = jnp.log(1.0 + jnp.exp(-jnp.abs(z)))
    lk = -(jnp.maximum(z, 0.0) + l1p)
    if causal is not None:
        lk = jnp.where(causal, lk, 0.0)
    return jnp.minimum(z, 0.0) - l1p, lk


def _sb_weights(lb, lk, r, u_gt, causal):
    a = jnp.exp(lb + _split_dot(lk, u_gt) + r)
    return a if causal is None else jnp.where(causal, a, 0.0)


def _rowsum(x):
    return jnp.sum(x, axis=1, keepdims=True)


def _sb_pair_tile(i, blk):
    row = lax.broadcasted_iota(jnp.int32, (blk, 2 * blk), 0)
    col = lax.broadcasted_iota(jnp.int32, (blk, 2 * blk), 1)
    first_col = jnp.where(i >= 1, 0, blk)
    allowed = jnp.logical_and(row > col - blk, col >= first_col)
    rows_p = pl.ds(pl.multiple_of(jnp.maximum(i - 1, 0) * blk, blk), blk)
    rows_d = pl.ds(pl.multiple_of(i * blk, blk), blk)
    return allowed, rows_p, rows_d


def _sb_fwd(qkv, sg):
    s = qkv.shape[0]
    blk = min(SB_BLOCK, s)
    nq = s // blk
    hd = SB_HEAD_DIM

    ch = min(SB_CHUNK, blk)

    def body(q_ref, k_ref, v_ref, g_ref, y_ref, o_ref, z_scr, lhs_scr, cs_scr, a_scr):
        i = pl.program_id(1)
        row = lax.broadcasted_iota(jnp.int32, (blk, blk), 0)
        col = lax.broadcasted_iota(jnp.int32, (blk, blk), 1)
        u_gt = (row > col).astype(MXU_DTYPE)
        heads = [slice(hh * hd, (hh + 1) * hd) for hh in range(2)]
        qs = [q_ref[:, ls] for ls in heads]
        _, rows_p, rows_d = _sb_pair_tile(i, blk)
        has_prev = i >= 1
        crow = lax.broadcasted_iota(jnp.int32, (ch, blk), 0)
        ccol = lax.broadcasted_iota(jnp.int32, (ch, blk), 1)

        def logits(hh):
            kc = jnp.concatenate([k_ref[rows_p, heads[hh]], k_ref[rows_d, heads[hh]]], axis=0)
            z_scr[hh] = _dot_nt(qs[hh], kc)

        def keep_parts(hh):
            r_d, r_all = [], []
            for c in range(blk // ch):
                rows = pl.ds(c * ch, ch)
                causal = crow + c * ch > ccol
                z = z_scr[hh, rows, :]
                l1p = jnp.log(1.0 + jnp.exp(-jnp.abs(z)))
                lb = jnp.minimum(z, 0.0) - l1p
                z_scr[hh, rows, :] = lb
                lk = lb - z
                lk_p = lk[:, :blk]
                lk_d = jnp.where(causal, lk[:, blk:], 0.0)
                lhs_scr[hh, pl.ds(c * ch, ch), :] = _mx(lk_p)
                lhs_scr[hh, pl.ds(blk + c * ch, ch), :] = _mx(lk_d)
                r_d.append(_rowsum(lk_d))
                r_all.append(r_d[c] + _rowsum(lk_p))
            return r_d, jnp.concatenate(r_all, axis=0)

        def suffix_sums(hh):
            cs_scr[hh] = _dot(lhs_scr[hh], u_gt)

        def weights(hh, r_d):
            for c in range(blk // ch):
                rows = pl.ds(c * ch, ch)
                causal = crow + c * ch > ccol
                cs_p = cs_scr[hh, pl.ds(c * ch, ch), :] + jnp.where(has_prev, r_d[c], NO_KEYS)
                cs_d = cs_scr[hh, pl.ds(blk + c * ch, ch), :]
                lb = z_scr[hh, rows, :]
                a_p = jnp.exp(lb[:, :blk] + cs_p)
                a_d = jnp.where(causal, jnp.exp(lb[:, blk:] + cs_d), 0.0)
                a_scr[hh, rows, :] = _mx(jnp.concatenate([a_p, a_d], axis=1))

        def values(hh):
            vc = jnp.concatenate([v_ref[rows_p, heads[hh]], v_ref[rows_d, heads[hh]]], axis=0)
            return _dot(a_scr[hh], vc)

        def block(hh, j, r):
            start = pl.multiple_of(j * blk, blk)
            lb, lk = _sb_logits(qs[hh], k_ref[pl.ds(start, blk), heads[hh]], None)
            a = _sb_weights(lb, lk, r, u_gt, None)
            return _dot(_mx(a), v_ref[pl.ds(start, blk), heads[hh]]), r + _rowsum(lk)

        def more(n, r0, r1):
            return jnp.logical_and(n < i, jnp.max(jnp.maximum(r0, r1)) > -SB_SKIP)

        logits(0)
        logits(1)
        gv = g_ref[...]
        gates = gv * _sigmoid(gv)
        rd0, r0 = keep_parts(0)
        suffix_sums(0)
        rd1, r1 = keep_parts(1)
        suffix_sums(1)
        go = more(jnp.int32(1), r0, r1)
        weights(0, rd0)
        acc0 = values(0)
        weights(1, rd1)
        acc1 = values(1)

        def step(c):
            _, n, acc0, r0, acc1, r1 = c
            pv0, r0 = block(0, i - 1 - n, r0)
            pv1, r1 = block(1, i - 1 - n, r1)
            return more(n + 1, r0, r1), n + 1, acc0 + pv0, r0, acc1 + pv1, r1

        _, _, acc0, _, acc1, _ = lax.while_loop(lambda c: c[0], step, (go, jnp.int32(1), acc0, r0, acc1, r1))
        o = jnp.concatenate([acc0, acc1], axis=1)
        o_ref[...] = o
        y_ref[...] = (o * gates).astype(y_ref.dtype)

    qblk = pl.BlockSpec((blk, LANES), lambda p, i: (i, p))
    return pl.pallas_call(
        body, name="stickbreak_fwd", grid=(SB_HEADS // 2, nq),
        in_specs=[qblk,
                  pl.BlockSpec((s, LANES), lambda p, i: (0, 4 + p)),
                  pl.BlockSpec((s, LANES), lambda p, i: (0, 8 + p)),
                  qblk],
        out_specs=[qblk, qblk],
        out_shape=[jax.ShapeDtypeStruct((s, 512), MXU_DTYPE),
                   jax.ShapeDtypeStruct((s, 512), F32)],
        scratch_shapes=[pltpu.VMEM((2, blk, 2 * blk), F32),
                        pltpu.VMEM((2, 2 * blk, blk), MXU_DTYPE),
                        pltpu.VMEM((2, 2 * blk, blk), F32),
                        pltpu.VMEM((2, blk, 2 * blk), MXU_DTYPE)],
        compiler_params=_params(("arbitrary", "arbitrary")),
    )(qkv, qkv, qkv, sg)


def _sb_bwd(qkv, sg, o, dy):
    s = qkv.shape[0]
    blk = min(SB_BLOCK, s)
    nq = s // blk
    hd = SB_HEAD_DIM
    assert nq <= LANES
    ch = min(SB_CHUNK, blk)

    def body(q_ref, k_ref, v_ref, g_ref, o_ref, dy_ref, dq_ref, dk_ref, dv_ref, dg_ref, dk_scr, dv_scr,
             z_scr, g_scr, lhs_scr, cs_scr, a_scr, dz_scr):
        i = pl.program_id(1)

        @pl.when(i == 0)
        def _():
            dk_scr[...] = jnp.zeros_like(dk_scr)
            dv_scr[...] = jnp.zeros_like(dv_scr)

        row = lax.broadcasted_iota(jnp.int32, (blk, blk), 0)
        col = lax.broadcasted_iota(jnp.int32, (blk, blk), 1)
        lane = lax.broadcasted_iota(jnp.int32, (blk, LANES), 1)
        u_gt = (row > col).astype(MXU_DTYPE)
        u_lt = (row < col).astype(MXU_DTYPE)
        heads = [slice(hh * hd, (hh + 1) * hd) for hh in range(2)]
        qs = [q_ref[:, ls] for ls in heads]
        _, rows_p, rows_d = _sb_pair_tile(i, blk)
        has_prev = i >= 1
        crow = lax.broadcasted_iota(jnp.int32, (ch, blk), 0)
        ccol = lax.broadcasted_iota(jnp.int32, (ch, blk), 1)
        nch = blk // ch
        kcs = [jnp.concatenate([k_ref[rows_p, ls], k_ref[rows_d, ls]], axis=0) for ls in heads]
        dobs = []

        def gate_grads():
            g = g_ref[...]
            dyv = dy_ref[...]
            sgm = _sigmoid(g)
            dg_ref[...] = (dyv * o_ref[...] * (sgm * (1.0 + g * (1.0 - sgm)))).astype(dg_ref.dtype)
            dob = _mx(dyv * (g * sgm))
            dobs.extend(dob[:, ls] for ls in heads)

        def split_rows(hh, c, part, x):
            lhs_scr[hh, pl.ds(part * blk + c * ch, ch), :] = _mx(x)

        def summed_rows(hh, c, part):
            return cs_scr[hh, pl.ds(part * blk + c * ch, ch), :]

        def logits(hh):
            z_scr[hh] = _dot_nt(qs[hh], kcs[hh])

        def weight_grads(hh):
            vc = jnp.concatenate([v_ref[rows_p, heads[hh]], v_ref[rows_d, heads[hh]]], axis=0)
            g_scr[hh] = _dot_nt(dobs[hh], vc)

        def keep_parts(hh):
            r_d, r_all = [], []
            for c in range(nch):
                rows = pl.ds(c * ch, ch)
                z = z_scr[hh, rows, :]
                l1p = jnp.log(1.0 + jnp.exp(-jnp.abs(z)))
                lb = jnp.minimum(z, 0.0) - l1p
                z_scr[hh, rows, :] = lb
                lk = lb - z
                lk_p = lk[:, :blk]
                lk_d = jnp.where(crow + c * ch > ccol, lk[:, blk:], 0.0)
                split_rows(hh, c, 0, lk_p)
                split_rows(hh, c, 1, lk_d)
                r_d.append(_rowsum(lk_d))
                r_all.append(r_d[c] + _rowsum(lk_p))
            return r_d, jnp.concatenate(r_all, axis=0)

        def weights(hh, r_d):
            g_p = []
            for c in range(nch):
                rows = pl.ds(c * ch, ch)
                lb = z_scr[hh, rows, :]
                a_p = jnp.exp(lb[:, :blk] + (summed_rows(hh, c, 0) + jnp.where(has_prev, r_d[c], NO_KEYS)))
                a_d = jnp.where(crow + c * ch > ccol, jnp.exp(lb[:, blk:] + summed_rows(hh, c, 1)), 0.0)
                a = jnp.concatenate([a_p, a_d], axis=1)
                a_scr[hh, rows, :] = _mx(a)
                gm = g_scr[hh, rows, :] * a
                g_scr[hh, rows, :] = gm
                split_rows(hh, c, 0, gm[:, :blk])
                split_rows(hh, c, 1, gm[:, blk:])
                g_p.append(_rowsum(gm[:, :blk]))
            return g_p

        def logit_grads(hh, pg, g_p):
            for c in range(nch):
                rows = pl.ds(c * ch, ch)
                pre = jnp.concatenate([summed_rows(hh, c, 0) + pg[c * ch:(c + 1) * ch],
                                       summed_rows(hh, c, 1) + (pg[c * ch:(c + 1) * ch] + g_p[c])], axis=1)
                gm = g_scr[hh, rows, :]
                dz = gm - (gm + pre) * jnp.exp(z_scr[hh, rows, :])
                dz_p = dz[:, :blk]
                dz_d = jnp.where(crow + c * ch > ccol, dz[:, blk:], 0.0)
                dz_scr[hh, rows, :] = _mx(jnp.concatenate([dz_p, dz_d], axis=1))

        def products(hh, acc):
            ls = heads[hh]
            dk_scr[hh, rows_p, :] += _dot_tn(dz_scr[hh, :, 0:blk], qs[hh])
            dk_scr[hh, rows_d, :] += _dot_tn(dz_scr[hh, :, blk:2 * blk], qs[hh])
            dv_scr[hh, rows_p, :] += _dot_tn(a_scr[hh, :, 0:blk], dobs[hh])
            dv_scr[hh, rows_d, :] += _dot_tn(a_scr[hh, :, blk:2 * blk], dobs[hh])
            dq_ref[:, ls] = ((acc + _dot(dz_scr[hh], kcs[hh])) * SB_SCALE).astype(dq_ref.dtype)

        def suffix_sums(hh):
            cs_scr[hh] = _dot(lhs_scr[hh], u_gt)

        def prefix_sums(hh):
            cs_scr[hh] = _dot(lhs_scr[hh], u_lt)

        def more(n, r0, r1):
            return jnp.logical_and(n < i, jnp.max(jnp.maximum(r0, r1)) > -SB_SKIP)

        gate_grads()
        logits(0)
        weight_grads(0)
        logits(1)
        weight_grads(1)
        rd0, ra0 = keep_parts(0)
        suffix_sums(0)
        rd1, ra1 = keep_parts(1)
        suffix_sums(1)
        go = more(jnp.int32(1), ra0, ra1)
        gp0 = weights(0, rd0)
        prefix_sums(0)
        gp1 = weights(1, rd1)
        prefix_sums(1)

        def scan_block(hh, j, r, rmat):
            start = pl.multiple_of(j * blk, blk)
            _, lk = _sb_logits(qs[hh], k_ref[pl.ds(start, blk), heads[hh]], None)
            return r + _rowsum(lk), jnp.where(lane == j, r, rmat)

        def scan_step(c):
            _, n, r0, rmat0, r1, rmat1 = c
            r0, rmat0 = scan_block(0, i - 1 - n, r0, rmat0)
            r1, rmat1 = scan_block(1, i - 1 - n, r1, rmat1)
            return more(n + 1, r0, r1), n + 1, r0, rmat0, r1, rmat1

        zmat = jnp.zeros((blk, LANES), F32)
        _, n, _, rmat0, _, rmat1 = lax.while_loop(lambda c: c[0], scan_step,
                                                  (go, jnp.int32(1), ra0, zmat, ra1, zmat))
        rmats = (rmat0, rmat1)

        def block(hh, j, pg):
            ls = heads[hh]
            start = pl.multiple_of(j * blk, blk)
            k = k_ref[pl.ds(start, blk), ls]
            lb, lk = _sb_logits(qs[hh], k, None)
            r = _rowsum(jnp.where(lane == j, rmats[hh], 0.0))
            a = _sb_weights(lb, lk, r, u_gt, None)
            gm = _dot_nt(dobs[hh], v_ref[pl.ds(start, blk), ls]) * a
            dzb = _mx(gm - (gm + (pg + _split_dot(gm, u_lt))) * jnp.exp(lb))
            dk_scr[hh, pl.ds(start, blk), :] += _dot_tn(dzb, qs[hh])
            dv_scr[hh, pl.ds(start, blk), :] += _dot_tn(_mx(a), dobs[hh])
            return _dot(dzb, k), pg + _rowsum(gm)

        def step(t, c):
            acc0, pg0, acc1, pg1 = c
            dq0, pg0 = block(0, i - n + t, pg0)
            dq1, pg1 = block(1, i - n + t, pg1)
            return acc0 + dq0, pg0, acc1 + dq1, pg1

        zero = jnp.zeros((blk, 1), F32)
        zacc = jnp.zeros((blk, hd), F32)
        acc0, pg0, acc1, pg1 = lax.fori_loop(0, n - 1, step, (zacc, zero, zacc, zero))
        logit_grads(0, pg0, gp0)
        logit_grads(1, pg1, gp1)
        products(0, acc0)
        products(1, acc1)

        @pl.when(i == nq - 1)
        def _():
            for hh in range(2):
                ls = slice(hh * hd, (hh + 1) * hd)
                dk_ref[:, ls] = dk_scr[hh].astype(dk_ref.dtype)
                dv_ref[:, ls] = dv_scr[hh].astype(dv_ref.dtype)

    qblk = lambda c0: pl.BlockSpec((blk, LANES), lambda p, i: (i, c0 + p))
    full = lambda c0: pl.BlockSpec((s, LANES), lambda p, i: (0, c0 + p))
    half = jax.ShapeDtypeStruct((s, 512), MXU_DTYPE)
    return pl.pallas_call(
        body, name="stickbreak_bwd", grid=(SB_HEADS // 2, nq),
        in_specs=[qblk(0), full(4), full(8), qblk(0), qblk(0), qblk(4)],
        out_specs=[qblk(0), full(0), full(0), qblk(0)],
        out_shape=[half, half, half, half],
        scratch_shapes=[pltpu.VMEM((2, s, hd), F32), pltpu.VMEM((2, s, hd), F32),
                        pltpu.VMEM((2, blk, 2 * blk), F32),
                        pltpu.VMEM((2, blk, 2 * blk), F32),
                        pltpu.VMEM((2, 2 * blk, blk), MXU_DTYPE),
                        pltpu.VMEM((2, 2 * blk, blk), F32),
                        pltpu.VMEM((2, blk, 2 * blk), MXU_DTYPE),
                        pltpu.VMEM((2, blk, 2 * blk), MXU_DTYPE)],
        compiler_params=_params(("arbitrary", "arbitrary")),
    )(qkv, qkv, qkv, sg, o, dy)


def _layer_fwd(layer, x, mod, norm_g, w_in_g, w_out_g, tables, gather=(), loss_head=None):
    shift, scale1p, gate = mod[0:1], 1.0 + mod[1:2], mod[2:3]
    ret, qkv, sg, *gathered = _ln_proj(x, shift, scale1p, norm_g, w_in_g, layer, gather)
    y_r, states = _retention_fwd(ret, tables)
    y_s, o_s = _sb_fwd(qkv, sg)
    x_next = _out_proj(x, gate, y_r, y_s, w_out_g, layer, loss_head)
    saved = (x, shift, scale1p, gate, ret, qkv, sg, y_r, states, y_s, o_s)
    return x_next, saved, gathered


def _layer_bwd(layer, me, dx_out, saved, norm_g, w_in_g, w_out_g, tables, ride_along=None):
    x, shift, scale1p, gate, ret, qkv, sg, y_r, states, y_s, o_s = saved
    dy, dw_out, dgate = _out_proj_bwd(dx_out, gate, y_r, y_s, w_out_g, layer)
    dwo_parts = _mx(dw_out.reshape(N_DEV, D_MODEL // N_DEV, D_MODEL))
    d_ret, r_out = _retention_bwd(ret, states, dy, tables, dwo_parts)
    d_q, d_k, d_v, d_g = _sb_bwd(qkv, sg, o_s, dy)
    dproj = (d_ret, d_q, d_k, d_v, d_g)
    dx, dshift, dscale, dnorm_g, h_t = _in_proj_bwd_x(x, dx_out, dproj, shift, scale1p, norm_g, w_in_g, layer)
    dmod = jnp.concatenate([dshift, dscale, dgate], axis=1)
    gather = ride_along(dmod, dnorm_g) if ride_along else ()
    r_in, *gathered = _in_proj_bwd_w(me, h_t, dproj, gather)
    return dx, r_in, r_out, dmod, dnorm_g, gathered


MESH_IDS = pl.DeviceIdType.MESH
N_PEERS = N_DEV - 1
HBM_SPEC = pl.BlockSpec(memory_space=pl.ANY)


def _my_place():
    return lax.axis_index("x"), lax.axis_index("y"), lax.axis_index("c")


def _linear(px, py, pc):
    return 4 * px + 2 * py + pc


def _all_gather(blocks):
    n_arr = len(blocks)

    def body(*refs):
        start, forward, finish = _gather_plan(refs[:n_arr], refs[n_arr:2 * n_arr], *refs[2 * n_arr:])
        start()
        forward()
        finish()

    return pl.pallas_call(
        body, name="all_gather",
        out_shape=_gathered_shapes(blocks),
        in_specs=[HBM_SPEC] * n_arr, out_specs=[HBM_SPEC] * n_arr,
        scratch_shapes=_gather_sems(n_arr),
    )(*blocks)


def _gathered_shapes(blocks):
    return [jax.ShapeDtypeStruct((N_DEV * b.shape[0], b.shape[1]), b.dtype) for b in blocks]


def _gather_sems(n_arr):
    return [pltpu.SemaphoreType.DMA((n_arr * N_PEERS,)), pltpu.SemaphoreType.DMA((n_arr * N_PEERS,)),
            pltpu.SemaphoreType.DMA((n_arr,))]


def _gather_plan(x_refs, out_refs, send_sems, recv_sems, local_sems):
    n_arr = len(x_refs)
    x, y, c = _my_place()
    me, sibling = (x, y, c), (x, y, 1 - c)
    chips = [(1 - x, y), (x, 1 - y), (1 - x, 1 - y)]

    def rows(a, place):
        m = x_refs[a].shape[0]
        return out_refs[a].at[pl.ds(_linear(*place) * m, m), :]

    def copy(a, k, block, to, src=None):
        return pltpu.make_async_remote_copy(
            src_ref=rows(a, block) if src is None else src, dst_ref=rows(a, block),
            send_sem=send_sems.at[a * N_PEERS + k], recv_sem=recv_sems.at[a * N_PEERS + k],
            device_id=to, device_id_type=MESH_IDS)

    mine = [pltpu.make_async_copy(x_refs[a], rows(a, me), local_sems.at[a]) for a in range(n_arr)]
    first = []
    for a in range(n_arr):
        first.append(copy(a, 0, me, sibling, src=x_refs[a]))
        first += [copy(a, 1 + j, me, (*chip, c), src=x_refs[a]) for j, chip in enumerate(chips)]
    passed = [copy(a, 4 + j, (*chip, c), sibling) for j, chip in enumerate(chips) for a in range(n_arr)]

    def start():
        for cp in mine + first:
            cp.start()

    def forward():
        for j, chip in enumerate(chips):
            for a in range(n_arr):
                copy(a, 1 + j, (*chip, c), me).wait_recv()
                passed[j * n_arr + a].start()

    def finish():
        for a in range(n_arr):
            copy(a, 0, sibling, me).wait_recv()
            for j, chip in enumerate(chips):
                copy(a, 4 + j, (*chip, 1 - c), me).wait_recv()
        for cp in first + passed:
            cp.wait_send()
        for cp in mine:
            cp.wait()

    return start, forward, finish


def _ada_fwd(c_all, w_ada, b_cols):
    cols = w_ada.shape[2]

    def body(c_ref, w_ref, b_ref, ca_ref, mod_ref):
        cv = c_ref[...]
        ca = cv * _sigmoid(cv)
        ca_ref[...] = ca
        cb = _mx(jnp.concatenate([ca, ca], axis=0))
        for l in range(DEPTH):
            mod_ref[l * N_DEV:(l + 1) * N_DEV, :] = _dot(cb, _mx(w_ref[l]))[0:N_DEV] + b_ref[l]

    return pl.pallas_call(
        body, name="ada_fwd",
        out_shape=[jax.ShapeDtypeStruct((N_DEV, D_MODEL), F32),
                   jax.ShapeDtypeStruct((DEPTH * N_DEV, cols), F32)],
        compiler_params=_params(),
    )(c_all, w_ada, b_cols)


def _ada_bwd(c_act_t, dmod_cols):
    cols = dmod_cols.shape[2]

    def body(ca_ref, dm_ref, o_ref):
        ca = _mx(ca_ref[...]).astype(F32)
        for l in range(DEPTH):
            o_ref[l] = jnp.dot(ca, _mx(dm_ref[l]).astype(F32),
                               precision=lax.Precision.HIGHEST, preferred_element_type=F32)

    return pl.pallas_call(
        body, name="ada_bwd",
        out_shape=jax.ShapeDtypeStruct((DEPTH, D_MODEL, cols), F32),
        compiler_params=_params(),
    )(c_act_t, dmod_cols)


def _adamw_store(g, w_ref, m_ref, v_ref, g_ref, d_ref, mo_ref, vo_ref):
    m2 = ADAM_B1 * m_ref[...] + (1.0 - ADAM_B1) * g
    v2 = ADAM_B2 * v_ref[...] + (1.0 - ADAM_B2) * (g * g)
    m_hat = m2 / (1.0 - ADAM_B1 ** ADAM_STEP)
    v_hat = v2 / (1.0 - ADAM_B2 ** ADAM_STEP)
    g_ref[...] = g
    d_ref[...] = -ADAM_LR * (m_hat / (jnp.sqrt(v_hat) + ADAM_EPS) + ADAM_WD * w_ref[...])
    mo_ref[...] = m2
    vo_ref[...] = v2


def _slab_sum(p_ref):
    g = p_ref[0].astype(F32)
    for sl in range(1, p_ref.shape[0]):
        g = g + p_ref[sl].astype(F32)
    return g


def _sum_adamw_layers(parts, w, m, v):
    n_slab, rows, cols = parts[0].shape
    tr = min(256, rows)
    nt = rows // tr

    def body(p0_ref, p1_ref, w_ref, m_ref, v_ref, g_ref, d_ref, mo_ref, vo_ref):
        for l, p_ref in enumerate((p0_ref, p1_ref)):
            @pl.when(pl.program_id(0) == l)
            def _():
                _adamw_store(_slab_sum(p_ref), w_ref, m_ref, v_ref, g_ref, d_ref, mo_ref, vo_ref)

    p_specs = [pl.BlockSpec((n_slab, tr, cols), lambda l, i: (0, i * (1 - l) + (nt - 1) * l, 0)),
               pl.BlockSpec((n_slab, tr, cols), lambda l, i: (0, i * l, 0))]
    blk = pl.BlockSpec((None, tr, cols), lambda l, i: (l, i, 0))
    shp = jax.ShapeDtypeStruct((DEPTH, rows, cols), F32)
    return pl.pallas_call(
        body, name="sum_adamw_layers", grid=(DEPTH, nt),
        in_specs=p_specs + [blk, blk, blk],
        out_specs=[blk, blk, blk, blk],
        out_shape=[shp, shp, shp, shp],
        compiler_params=_params(("arbitrary", "arbitrary")),
    )(parts[0], parts[1], w, m, v)


def _sum_adamw(parts, w, m, v):
    n_slab, rows, cols = parts.shape
    tr = min(256, rows)

    def body(p_ref, w_ref, m_ref, v_ref, g_ref, d_ref, mo_ref, vo_ref):
        _adamw_store(_slab_sum(p_ref), w_ref, m_ref, v_ref, g_ref, d_ref, mo_ref, vo_ref)

    blk = pl.BlockSpec((tr, cols), lambda i: (i, 0))
    shp = jax.ShapeDtypeStruct((rows, cols), F32)
    return pl.pallas_call(
        body, name="sum_adamw", grid=(rows // tr,),
        in_specs=[pl.BlockSpec((n_slab, tr, cols), lambda i: (0, i, 0)), blk, blk, blk],
        out_specs=[blk, blk, blk, blk],
        out_shape=[shp, shp, shp, shp],
        compiler_params=_params(("arbitrary",)),
    )(parts, w, m, v)


SMALL_ROWS = 16


def kernel(x, c, norm_g, w_ada, b_ada, w_in, w_out, final_g, loss_target, m_norm_g, m_w_ada, m_b_ada, m_w_in, m_w_out, m_final_g, v_norm_g, v_w_ada, v_b_ada, v_w_in, v_w_out, v_final_g):
    me = _linear(*_my_place())
    in_cols = w_in.shape[2]
    out_rows = w_out.shape[1]
    ada_cols = w_ada.shape[2]

    w_in_m, w_out_m = _mx(w_in), _mx(w_out)
    g_in, g_out, g_c = _all_gather([w_in_m[0], w_out_m[0], jnp.broadcast_to(c, (8, D_MODEL))])
    w_in_g = [g_in.reshape(N_DEV, 1, D_MODEL, in_cols), None]
    w_out_g = [g_out.reshape(N_DEV, 1, out_rows, D_MODEL), None]
    c_all = g_c.reshape(N_DEV, 8, D_MODEL)[:, 0]

    b_cols = lax.dynamic_slice_in_dim(b_ada, me * ada_cols, ada_cols, axis=1)[:, None, :]
    c_act, mod_cols = _ada_fwd(c_all, w_ada, b_cols)
    (g_mod,) = _all_gather([mod_cols])
    g_mod = g_mod.reshape(N_DEV, DEPTH, N_DEV, ada_cols)
    mod = lax.dynamic_index_in_dim(g_mod, me, axis=2, keepdims=False)
    mod = mod.transpose(1, 0, 2).reshape(DEPTH, 3, D_MODEL)

    tables = _ret_tables(x.shape[1])
    h = x[0]
    saved = []
    for l in range(DEPTH):
        nxt = (w_in_m[l + 1], w_out_m[l + 1]) if l + 1 < DEPTH else ()
        head = None if nxt else (final_g[None], loss_target[0])
        h, sv, gathered = _layer_fwd(0, h, mod[l], norm_g[l:l + 1], w_in_g[l], w_out_g[l], tables, nxt, head)
        if nxt:
            w_in_g[l + 1] = gathered[0].reshape(N_DEV, 1, D_MODEL, in_cols)
            w_out_g[l + 1] = gathered[1].reshape(N_DEV, 1, out_rows, D_MODEL)
        saved.append(sv)
    dx, loss_part, dfg = h
    r_in, r_out, small = [None] * DEPTH, [None] * DEPTH, [None] * DEPTH

    def small_block(dmod0, dng0):
        pad = jnp.zeros((SMALL_ROWS - 10, D_MODEL), F32)
        return (jnp.concatenate([dmod0.reshape(3, D_MODEL), small[1][0], dng0, small[1][1], dfg,
                                 jnp.broadcast_to(loss_part, (1, D_MODEL)), pad], axis=0),)

    for l in reversed(range(DEPTH)):
        dx, r_in[l], r_out[l], dmod, dng, gathered = _layer_bwd(
            0, me, dx, saved[l], norm_g[l:l + 1], w_in_g[l], w_out_g[l], tables, small_block if l == 0 else None)
        small[l] = (dmod.reshape(3, D_MODEL), dng)
    g_small = gathered[0].reshape(N_DEV, SMALL_ROWS, D_MODEL)

    def small_pack(b, n, f, fill):
        return jnp.concatenate([b.reshape(6, D_MODEL), n, f[None],
                                jnp.full((SMALL_ROWS - 9, D_MODEL), fill, F32)], axis=0)

    s_g, s_d, s_m, s_v = _sum_adamw(g_small, small_pack(b_ada, norm_g, final_g, 0.0),
                                    small_pack(m_b_ada, m_norm_g, m_final_g, 0.0),
                                    small_pack(v_b_ada, v_norm_g, v_final_g, 1.0))
    loss = s_g[9, 0]

    def small_unpack(a):
        return a[0:6].reshape(DEPTH, 3 * D_MODEL), a[6:8], a[8]

    dmod_all = g_small[:, 0:6].reshape(N_DEV, DEPTH, 3 * D_MODEL).transpose(1, 0, 2)
    dmod_cols = lax.dynamic_slice_in_dim(dmod_all, me * ada_cols, ada_cols, axis=2)
    g_ada = _ada_bwd(c_act.T, dmod_cols).reshape(1, DEPTH * D_MODEL, ada_cols)
    ada = _sum_adamw(g_ada, *[a.reshape(DEPTH * D_MODEL, ada_cols) for a in (w_ada, m_w_ada, v_w_ada)])
    ada = [a.reshape(DEPTH, D_MODEL, ada_cols) for a in ada]

    win = _sum_adamw_layers(r_in, w_in, m_w_in, v_w_in)
    wout = _sum_adamw_layers(r_out, w_out, m_w_out, v_w_out)

    outs = [loss, dx[None]]
    for k in range(4):
        b, n, f = small_unpack((s_g, s_d, s_m, s_v)[k])
        outs += [n, ada[k], b, win[k], wout[k], f]
    return tuple(outs)
```

```python
import functools

import jax
import jax.numpy as jnp
from jax import lax
from jax.experimental import pallas as pl
from jax.experimental.pallas import tpu as pltpu

F32 = jnp.float32
MXU_DTYPE = jnp.bfloat16

D_MODEL = 1024
DEPTH = 2
N_DEV = 8
CHUNK = 64
D_RET = 512
D_SB = 512
RET_HEADS = 4
RET_HEAD_DIM = 128
SB_HEADS = 8
SB_HEAD_DIM = 64
D_IN = 4096
ROPE_BASE = 10000.0
EPS = 1e-6
SB_SCALE = SB_HEAD_DIM ** -0.5
RET_KSCALE = RET_HEAD_DIM ** -0.5

ADAM_LR = 0.001
ADAM_B1 = 0.9
ADAM_B2 = 0.999
ADAM_EPS = 1e-08
ADAM_WD = 0.01
ADAM_STEP = 10

V7X_VMEM_BYTES = 64 * 2 ** 20
VMEM_LIMIT = V7X_VMEM_BYTES - 8 * 2 ** 20
LANES = 128

_NT = (((1,), (1,)), ((), ()))
_TN = (((0,), (0,)), ((), ()))


def _dot(a, b):
    return jnp.dot(a, b, preferred_element_type=F32)


def _dot_nt(a, b):
    return lax.dot_general(a, b, _NT, preferred_element_type=F32)


def _dot_tn(a, b):
    return lax.dot_general(a, b, _TN, preferred_element_type=F32)


def _mx(x):
    return x.astype(MXU_DTYPE)


def _sigmoid(x):
    return 1.0 / (1.0 + jnp.exp(-x))


def _params(sem=None):
    return pltpu.CompilerParams(dimension_semantics=sem, vmem_limit_bytes=VMEM_LIMIT)


def _row_tile(s):
    return min(512, s)


def _w_in_spec(w_in_g, layer):
    return pl.BlockSpec((N_DEV, None) + w_in_g.shape[2:], lambda i: (0, layer, 0, 0))


def _w_out_spec(w_out_g, layer):
    return pl.BlockSpec((N_DEV, None) + w_out_g.shape[2:], lambda i: (0, layer, 0, 0))


def _ln_proj(x, shift, scale1p, g, w_in_g, layer, gather=()):
    s = x.shape[0]
    ts = _row_tile(s)
    ns = s // ts
    n_g = len(gather)

    def body(x_ref, sh_ref, sc_ref, g_ref, w_ref, *rest):
        ret_ref, qkv_ref, sg_ref = rest[n_g:n_g + 3]
        if n_g:
            start, forward, finish = _gather_plan(rest[:n_g], rest[n_g + 3:2 * n_g + 3], *rest[2 * n_g + 3:])
            i = pl.program_id(0)
            pl.when(i == 0)(start)
            pl.when(i == max(ns - 4, 0))(forward)
        xv = x_ref[...]
        rstd = lax.rsqrt(jnp.mean(xv * xv, axis=-1, keepdims=True) + EPS)
        h = (xv * rstd * g_ref[...]) * sc_ref[...] + sh_ref[...]
        hb = _mx(h)
        for n in range(4):
            ret_ref[:, n * 512:(n + 1) * 512] = _dot(hb, w_ref[n])
        qkv_ref[:, 0:512] = _mx(_dot(hb, w_ref[4]) * SB_SCALE)
        qkv_ref[:, 512:1024] = _mx(_dot(hb, w_ref[5]))
        qkv_ref[:, 1024:1536] = _mx(_dot(hb, w_ref[6]))
        sg_ref[...] = _dot(hb, w_ref[7])
        if n_g:
            pl.when(i == ns - 1)(finish)

    vec = pl.BlockSpec((1, D_MODEL), lambda i: (0, 0))
    return pl.pallas_call(
        body, name="ln_proj_gather" if n_g else "ln_proj", grid=(ns,),
        in_specs=[pl.BlockSpec((ts, D_MODEL), lambda i: (i, 0)), vec, vec, vec,
                  _w_in_spec(w_in_g, layer)] + [HBM_SPEC] * n_g,
        out_specs=[pl.BlockSpec((ts, 2048), lambda i: (i, 0)),
                   pl.BlockSpec((ts, 1536), lambda i: (i, 0)),
                   pl.BlockSpec((ts, 512), lambda i: (i, 0))] + [HBM_SPEC] * n_g,
        out_shape=[jax.ShapeDtypeStruct((s, 2048), F32),
                   jax.ShapeDtypeStruct((s, 1536), MXU_DTYPE),
                   jax.ShapeDtypeStruct((s, 512), F32)] + _gathered_shapes(gather),
        scratch_shapes=_gather_sems(n_g) if n_g else (),
        compiler_params=_params(("arbitrary",)),
    )(x, shift, scale1p, g, w_in_g, *gather)


def _w_out_halves(w_ref):
    half = N_DEV // 2
    return (w_ref[0:half].reshape(D_RET, D_MODEL), w_ref[half:N_DEV].reshape(D_SB, D_MODEL))


def _out_proj(x, gate, y_r, y_s, w_out_g, layer, loss_head=None):
    s = x.shape[0]
    ts = min(2 * _row_tile(s), s)

    def layer_out(x_ref, gate_ref, yr_ref, ys_ref, w_ref):
        w_r, w_s = _w_out_halves(w_ref)
        return x_ref[...] + gate_ref[...] * (_dot(yr_ref[...], w_r) + _dot(ys_ref[...], w_s))

    def body(x_ref, gate_ref, yr_ref, ys_ref, w_ref, o_ref):
        o_ref[...] = layer_out(x_ref, gate_ref, yr_ref, ys_ref, w_ref)

    def body_loss(x_ref, gate_ref, yr_ref, ys_ref, w_ref, fg_ref, t_ref, dx_ref, loss_ref, dfg_ref):
        @pl.when(pl.program_id(0) == 0)
        def _():
            loss_ref[...] = jnp.zeros_like(loss_ref)
            dfg_ref[...] = jnp.zeros_like(dfg_ref)

        xv = layer_out(x_ref, gate_ref, yr_ref, ys_ref, w_ref)
        fgv = fg_ref[...]
        rstd = lax.rsqrt(jnp.mean(xv * xv, axis=-1, keepdims=True) + EPS)
        xn = xv * rstd
        err = xn * fgv - t_ref[...]
        tok = jnp.mean(err * err, axis=-1, keepdims=True)
        loss_ref[...] += 0.5 * jnp.sum(tok, axis=0, keepdims=True)
        dy = err * (1.0 / D_MODEL)
        dfg_ref[...] += jnp.sum(dy * xn, axis=0, keepdims=True)
        dxn = dy * fgv
        dx_ref[...] = rstd * (dxn - xn * jnp.mean(dxn * xn, axis=-1, keepdims=True))

    rows = pl.BlockSpec((ts, D_MODEL), lambda i: (i, 0))
    vec = pl.BlockSpec((1, D_MODEL), lambda i: (0, 0))
    in_specs = [rows, vec, pl.BlockSpec((ts, 512), lambda i: (i, 0)), pl.BlockSpec((ts, 512), lambda i: (i, 0)),
                _w_out_spec(w_out_g, layer)]
    if loss_head is None:
        return pl.pallas_call(
            body, name="out_proj", grid=(s // ts,), in_specs=in_specs, out_specs=rows,
            out_shape=jax.ShapeDtypeStruct((s, D_MODEL), F32),
            compiler_params=_params(("arbitrary",)),
        )(x, gate, y_r, y_s, w_out_g)
    return pl.pallas_call(
        body_loss, name="out_proj_loss", grid=(s // ts,), in_specs=in_specs + [vec, rows],
        out_specs=[rows, pl.BlockSpec((1, 1), lambda i: (0, 0)), vec],
        out_shape=[jax.ShapeDtypeStruct((s, D_MODEL), F32),
                   jax.ShapeDtypeStruct((1, 1), F32),
                   jax.ShapeDtypeStruct((1, D_MODEL), F32)],
        compiler_params=_params(("arbitrary",)),
    )(x, gate, y_r, y_s, w_out_g, *loss_head)


def _out_proj_bwd(dx_out, gate, y_r, y_s, w_out_g, layer):
    s = dx_out.shape[0]
    ts = min(2 * _row_tile(s), s)
    ns = s // ts

    def body(dx_ref, gate_ref, yr_ref, ys_ref, w_ref, dy_ref, dw_ref, dgate_ref):
        i = pl.program_id(0)

        @pl.when(i == 0)
        def _():
            dw_ref[...] = jnp.zeros_like(dw_ref)

        dxv = dx_ref[...]
        dt = _mx(dxv * gate_ref[...])
        dxb = _mx(dxv)
        w_r, w_s = _w_out_halves(w_ref)
        dy_ref[:, 0:512] = _dot_nt(dt, w_r)
        dy_ref[:, 512:1024] = _dot_nt(dt, w_s)
        dw_ref[0:512, :] += _dot_tn(yr_ref[...], dxb)
        dw_ref[512:1024, :] += _dot_tn(ys_ref[...], dxb)

        @pl.when(i == ns - 1)
        def _():
            m_r, m_s = dw_ref[0:512, :], dw_ref[512:1024, :]
            dgate_ref[...] = (jnp.sum(w_r.astype(F32) * m_r, axis=0, keepdims=True)
                              + jnp.sum(w_s.astype(F32) * m_s, axis=0, keepdims=True))
            dw_ref[...] = dw_ref[...] * gate_ref[...]

    return pl.pallas_call(
        body, name="out_proj_bwd", grid=(ns,),
        in_specs=[pl.BlockSpec((ts, D_MODEL), lambda i: (i, 0)),
                  pl.BlockSpec((1, D_MODEL), lambda i: (0, 0)),
                  pl.BlockSpec((ts, 512), lambda i: (i, 0)),
                  pl.BlockSpec((ts, 512), lambda i: (i, 0)),
                  _w_out_spec(w_out_g, layer)],
        out_specs=[pl.BlockSpec((ts, D_MODEL), lambda i: (i, 0)),
                   pl.BlockSpec((D_MODEL, D_MODEL), lambda i: (0, 0)),
                   pl.BlockSpec((1, D_MODEL), lambda i: (0, 0))],
        out_shape=[jax.ShapeDtypeStruct((s, D_MODEL), F32),
                   jax.ShapeDtypeStruct((D_MODEL, D_MODEL), F32),
                   jax.ShapeDtypeStruct((1, D_MODEL), F32)],
        compiler_params=_params(("arbitrary",)),
    )(dx_out, gate, y_r, y_s, w_out_g)


def _scatter_plan(parts_ref, recv_ref, send_sems, recv_sems, local_sem):
    px, py, pc = _my_place()
    mine = _linear(px, py, pc)

    def copy(r):
        peer = (1 - px if r & 4 else px, 1 - py if r & 2 else py, 1 - pc if r & 1 else pc)
        return pltpu.make_async_remote_copy(
            src_ref=parts_ref.at[_linear(*peer)], dst_ref=recv_ref.at[mine],
            send_sem=send_sems.at[r - 1], recv_sem=recv_sems.at[r - 1],
            device_id=peer, device_id_type=MESH_IDS)

    own = pltpu.make_async_copy(parts_ref.at[mine], recv_ref.at[mine], local_sem.at[0])

    def start():
        own.start()
        for r in range(1, N_DEV):
            copy(r).start()

    def finish():
        for r in range(1, N_DEV):
            copy(r).wait_recv()
            copy(r).wait_send()
        own.wait()

    return start, finish


def _in_proj_bwd_x(x, dx_out, dproj, shift, scale1p, g, w_in_g, layer):
    s = x.shape[0]
    ts = _row_tile(s)
    ns = s // ts
    nb = D_IN // N_DEV

    def body(x_ref, dxo_ref, dr_ref, d4_ref, d5_ref, d6_ref, d7_ref, sh_ref, sc_ref, g_ref, w_ref,
             dx_ref, dsh_ref, dsc_ref, dg_ref, ht_ref):
        i = pl.program_id(0)

        @pl.when(i == 0)
        def _():
            dsh_ref[...] = jnp.zeros_like(dsh_ref)
            dsc_ref[...] = jnp.zeros_like(dsc_ref)
            dg_ref[...] = jnp.zeros_like(dg_ref)

        dh = _dot_nt(dr_ref[:, 0:nb], w_ref[0])
        for n in range(1, 4):
            dh += _dot_nt(dr_ref[:, n * nb:(n + 1) * nb], w_ref[n])
        for n, d_ref in zip(range(4, N_DEV), (d4_ref, d5_ref, d6_ref, d7_ref)):
            dh += _dot_nt(d_ref[...], w_ref[n])
        xv = x_ref[...]
        gv = g_ref[...]
        scv = sc_ref[...]
        rstd = lax.rsqrt(jnp.mean(xv * xv, axis=-1, keepdims=True) + EPS)
        xn = xv * rstd
        xg = xn * gv
        ht_ref[...] = _mx((xg * scv + sh_ref[...]).T)
        dsh_ref[...] += jnp.sum(dh, axis=0, keepdims=True)
        dsc_ref[...] += jnp.sum(dh * xg, axis=0, keepdims=True)
        dhs = dh * scv
        dg_ref[...] += jnp.sum(dhs * xn, axis=0, keepdims=True)
        dxn = dhs * gv
        dx_ref[...] = rstd * (dxn - xn * jnp.mean(dxn * xn, axis=-1, keepdims=True)) + dxo_ref[...]

    vec = pl.BlockSpec((1, D_MODEL), lambda i: (0, 0))
    return pl.pallas_call(
        body, name="in_proj_bwd_x", grid=(ns,),
        in_specs=[pl.BlockSpec((ts, D_MODEL), lambda i: (i, 0)),
                  pl.BlockSpec((ts, D_MODEL), lambda i: (i, 0)),
                  pl.BlockSpec((ts, 4 * nb), lambda i: (i, 0))]
                 + [pl.BlockSpec((ts, nb), lambda i: (i, 0))] * 4
                 + [vec, vec, vec, _w_in_spec(w_in_g, layer)],
        out_specs=[pl.BlockSpec((ts, D_MODEL), lambda i: (i, 0)), vec, vec, vec,
                   pl.BlockSpec((D_MODEL, ts), lambda i: (0, i))],
        out_shape=[jax.ShapeDtypeStruct((s, D_MODEL), F32),
                   jax.ShapeDtypeStruct((1, D_MODEL), F32),
                   jax.ShapeDtypeStruct((1, D_MODEL), F32),
                   jax.ShapeDtypeStruct((1, D_MODEL), F32),
                   jax.ShapeDtypeStruct((D_MODEL, s), MXU_DTYPE)],
        compiler_params=_params(("arbitrary",)),
    )(x, dx_out, *dproj, shift, scale1p, g, w_in_g)


def _in_proj_bwd_w(me, h_t, dproj, gather=()):
    s = h_t.shape[1]
    ts = min(4 * _row_tile(s), s)
    ns = s // ts
    nb = D_IN // N_DEV
    n_chip = N_DEV // 2
    n_g = len(gather)

    def flip_bits(j):
        return jnp.where(j == 0, 4, jnp.where(j == 1, 2, jnp.where(j == 2, 6, 0)))

    def slab_of(t, me_ref):
        return jnp.bitwise_xor(me_ref[0], flip_bits(t // 2) + 1 - t % 2)

    def body(me_ref, ht_ref, dr_ref, d4_ref, d5_ref, d6_ref, d7_ref, *rest):
        rin_ref = rest[n_g]
        (acc, stage, pre_buf, pre_send, pre_recv, sum_send, sum_recv, local_sem) = rest[2 * n_g + 1:2 * n_g + 9]
        t = pl.program_id(0)
        i = pl.program_id(1)
        if n_g:
            g_start, g_forward, g_finish = _gather_plan(rest[:n_g], rest[n_g + 1:2 * n_g + 1], *rest[2 * n_g + 9:])
            pl.when(jnp.logical_and(t == 0, i == 0))(g_start)
            pl.when(jnp.logical_and(t == N_DEV // 2, i == 0))(g_forward)
        j = t // 2
        summing = t % 2 == 1
        slab = slab_of(t, me_ref)
        px, py, pc = _my_place()

        def pre_copy(jj):
            return pltpu.make_async_remote_copy(
                src_ref=stage.at[jj % 2], dst_ref=pre_buf.at[jj],
                send_sem=pre_send.at[jj], recv_sem=pre_recv.at[jj],
                device_id=(px, py, 1 - pc), device_id_type=MESH_IDS)

        def sum_copy(jj):
            fx = jnp.logical_or(jj == 0, jj == 2)
            fy = jnp.logical_or(jj == 1, jj == 2)
            return pltpu.make_async_remote_copy(
                src_ref=stage.at[2 + jj % 2], dst_ref=rin_ref.at[jj],
                send_sem=sum_send.at[jj], recv_sem=sum_recv.at[jj],
                device_id=(jnp.where(fx, 1 - px, px), jnp.where(fy, 1 - py, py), pc), device_id_type=MESH_IDS)

        own = pltpu.make_async_copy(stage.at[3], rin_ref.at[n_chip - 1], local_sem.at[0])

        @pl.when(i == 0)
        def _():
            acc[...] = jnp.zeros_like(acc)

        @pl.when(slab < 4)
        def _():
            acc[...] += _dot(ht_ref[...], dr_ref[...])

        for n, d_ref in zip(range(4, N_DEV), (d4_ref, d5_ref, d6_ref, d7_ref)):
            @pl.when(slab == n)
            def _():
                acc[...] += _dot(ht_ref[...], d_ref[...])

        @pl.when(jnp.logical_and(i == ns - 1, jnp.logical_not(summing)))
        def _():
            @pl.when(j >= 2)
            def _():
                pre_copy(j - 2).wait_send()

            stage[j % 2] = acc[...].astype(stage.dtype)
            pre_copy(j).start()

        @pl.when(jnp.logical_and(i == ns - 1, summing))
        def _():
            pre_copy(j).wait_recv()

            @pl.when(j >= 2)
            def _():
                sum_copy(j - 2).wait_send()

            stage[2 + j % 2] = (acc[...] + pre_buf[j].astype(F32)).astype(stage.dtype)

            @pl.when(j < n_chip - 1)
            def _():
                sum_copy(j).start()

            @pl.when(j == n_chip - 1)
            def _():
                own.start()
                pre_copy(n_chip - 2).wait_send()
                pre_copy(n_chip - 1).wait_send()
                sum_copy(n_chip - 2).wait_send()
                for jj in range(n_chip - 1):
                    sum_copy(jj).wait_recv()
                own.wait()
                if n_g:
                    g_finish()

    def part_rows(n, t, i, me_ref):
        return jnp.where(slab_of(t, me_ref) == n, i, ns - 1), 0

    return pl.pallas_call(
        body, name="in_proj_bwd_w",
        grid_spec=pltpu.PrefetchScalarGridSpec(
            num_scalar_prefetch=1, grid=(N_DEV, ns),
            in_specs=[pl.BlockSpec((D_MODEL, ts), lambda t, i, me_ref: (0, i)),
                      pl.BlockSpec((ts, nb), lambda t, i, me_ref: (
                          jnp.where(slab_of(t, me_ref) < 4, i, ns - 1), jnp.minimum(slab_of(t, me_ref), 3)))]
                     + [pl.BlockSpec((ts, nb), functools.partial(part_rows, n)) for n in range(4, N_DEV)]
                     + [HBM_SPEC] * n_g,
            out_specs=[HBM_SPEC] * (1 + n_g),
            scratch_shapes=[pltpu.VMEM((D_MODEL, nb), F32),
                            pltpu.VMEM((4, D_MODEL, nb), MXU_DTYPE),
                            pltpu.VMEM((n_chip, D_MODEL, nb), MXU_DTYPE),
                            pltpu.SemaphoreType.DMA((n_chip,)), pltpu.SemaphoreType.DMA((n_chip,)),
                            pltpu.SemaphoreType.DMA((n_chip - 1,)), pltpu.SemaphoreType.DMA((n_chip - 1,)),
                            pltpu.SemaphoreType.DMA((1,))] + (_gather_sems(n_g) if n_g else [])),
        out_shape=[jax.ShapeDtypeStruct((n_chip, D_MODEL, nb), MXU_DTYPE)] + _gathered_shapes(gather),
        compiler_params=_params(("arbitrary", "arbitrary")),
    )(jnp.reshape(me, (1,)).astype(jnp.int32), h_t, *dproj, *gather)


RET_TILE = 256


def _ret_tables(s):
    t = min(RET_TILE, s)
    half = RET_HEAD_DIM // 2
    pos = jnp.arange(s, dtype=F32)
    inv = ROPE_BASE ** (-jnp.arange(half, dtype=F32) / half)
    ang = pos[:, None] * jnp.concatenate([inv, inv])[None, :]
    cos2 = jnp.cos(ang)
    sin2 = jnp.sin(ang) * jnp.concatenate([-jnp.ones((half,), F32), jnp.ones((half,), F32)])[None, :]
    lg = jnp.log1p(-(2.0 ** (-5.0 - jnp.arange(RET_HEADS, dtype=F32))))[:, None, None]
    n = jnp.arange(t)
    dist = (n[:, None] - n[None, :]).astype(F32)[None]
    cn = (n // CHUNK)[:, None]
    cm = (n // CHUNK)[None, :]
    mask = jnp.where((cn == cm)[None], jnp.exp(jnp.abs(dist) * lg),
                     jnp.where((cm < cn)[None], jnp.exp(dist * lg), 0.0))
    nf = n.astype(F32)[None, :, None]
    dq = jnp.broadcast_to(jnp.exp((nf + 1.0) * lg), (RET_HEADS, t, LANES))
    dk = jnp.broadcast_to(jnp.exp((t - 1.0 - nf) * lg), (RET_HEADS, t, LANES))
    gt = jnp.broadcast_to(jnp.exp(float(t) * lg), (RET_HEADS, 1, LANES))
    return cos2, sin2, mask, dq, dk, gt


def _roll_half(x):
    return pltpu.roll(x, RET_HEAD_DIM // 2, 1)


def _ret_heads_fwd(ret_ref, cos, sin, m_ref, dq_ref, dk_ref, s0):
    hd = RET_HEAD_DIM
    heads = range(RET_HEADS)
    qb, kb, vb, kdb = [], [], [], []
    for h in heads:
        q = ret_ref[:, h * hd:(h + 1) * hd]
        k = ret_ref[:, 512 + h * hd:512 + (h + 1) * hd]
        kr = (k * cos + _roll_half(k) * sin) * RET_KSCALE
        qb.append(_mx(q * cos + _roll_half(q) * sin))
        kb.append(_mx(kr))
        kdb.append(_mx(kr * dk_ref[h]))
        vb.append(_mx(ret_ref[:, 1024 + h * hd:1024 + (h + 1) * hd]))
    p = [_dot_nt(qb[h], kb[h]) for h in heads]
    cross = [_dot(qb[h], _mx(s0[h])) for h in heads]
    pb = [_mx(p[h] * m_ref[h]) for h in heads]
    o = [_dot(pb[h], vb[h]) + cross[h] * dq_ref[h] for h in heads]
    gn, rstd = [], []
    for h in heads:
        oc = o[h] - jnp.mean(o[h], axis=-1, keepdims=True)
        rstd.append(lax.rsqrt(jnp.mean(oc * oc, axis=-1, keepdims=True) + EPS))
        gn.append(oc * rstd[h])
    return qb, kb, vb, pb, kdb, gn, rstd


def _retention_fwd(ret, tables):
    cos2, sin2, mask, dq, dk, gt = tables
    s = ret.shape[0]
    t = mask.shape[1]
    nt = s // t
    hd = RET_HEAD_DIM

    def body(ret_ref, cos_ref, sin_ref, m_ref, dq_ref, dk_ref, gt_ref, y_ref, st_ref, s_scr):
        i = pl.program_id(0)

        @pl.when(i == 0)
        def _():
            s_scr[...] = jnp.zeros_like(s_scr)

        s0 = [s_scr[h] for h in range(RET_HEADS)]
        _, _, vb, _, kdb, gn, _ = _ret_heads_fwd(ret_ref, cos_ref[...], sin_ref[...], m_ref, dq_ref, dk_ref, s0)
        kv = [_dot_tn(kdb[h], vb[h]) for h in range(RET_HEADS)]
        for h in range(RET_HEADS):
            g = ret_ref[:, 1536 + h * hd:1536 + (h + 1) * hd]
            st_ref[h] = s0[h]
            y_ref[:, h * hd:(h + 1) * hd] = (gn[h] * (g * _sigmoid(g))).astype(y_ref.dtype)
            s_scr[h] = s0[h] * gt_ref[h] + kv[h]

    full3 = lambda a: pl.BlockSpec(a.shape, lambda i: (0, 0, 0))
    return pl.pallas_call(
        body, name="retention_fwd", grid=(nt,),
        in_specs=[pl.BlockSpec((t, 2048), lambda i: (i, 0)),
                  pl.BlockSpec((t, LANES), lambda i: (i, 0)),
                  pl.BlockSpec((t, LANES), lambda i: (i, 0)),
                  full3(mask), full3(dq), full3(dk), full3(gt)],
        out_specs=[pl.BlockSpec((t, 512), lambda i: (i, 0)),
                   pl.BlockSpec((None, RET_HEADS, hd, hd), lambda i: (i, 0, 0, 0))],
        out_shape=[jax.ShapeDtypeStruct((s, 512), MXU_DTYPE),
                   jax.ShapeDtypeStruct((nt, RET_HEADS, hd, hd), F32)],
        scratch_shapes=[pltpu.VMEM((RET_HEADS, hd, hd), F32)],
        compiler_params=_params(("arbitrary",)),
    )(ret, cos2, sin2, mask, dq, dk, gt)


def _retention_bwd(ret, states, dy, tables, dwo_parts):
    cos2, sin2, mask, dq, dk, gt = tables
    s = ret.shape[0]
    t = mask.shape[1]
    nt = s // t
    hd = RET_HEAD_DIM

    def body(ret_ref, st_ref, dy_ref, cos_ref, sin_ref, m_ref, dq_ref, dk_ref, gt_ref, dwo_ref,
             d_ref, rout_ref, ds_scr, send_sems, recv_sems, local_sem):
        i = pl.program_id(0)
        start, finish = _scatter_plan(dwo_ref, rout_ref, send_sems, recv_sems, local_sem)

        @pl.when(i == 0)
        def _():
            start()
            ds_scr[...] = jnp.zeros_like(ds_scr)

        cos = cos_ref[...]
        sin = sin_ref[...]
        heads = range(RET_HEADS)
        s0 = [st_ref[h] for h in heads]
        ds = [ds_scr[h] for h in heads]
        dsb = [_mx(ds[h]) for h in heads]
        qb, kb, vb, pb, kdb, gn, rstd = _ret_heads_fwd(ret_ref, cos, sin, m_ref, dq_ref, dk_ref, s0)
        dob, dodb = [], []
        for h in heads:
            g = ret_ref[:, 1536 + h * hd:1536 + (h + 1) * hd]
            dyv = dy_ref[:, h * hd:(h + 1) * hd]
            sg = _sigmoid(g)
            d_ref[:, 1536 + h * hd:1536 + (h + 1) * hd] = (
                dyv * gn[h] * (sg * (1.0 + g * (1.0 - sg)))).astype(d_ref.dtype)
            dgn = dyv * (g * sg)
            do = rstd[h] * (dgn - jnp.mean(dgn, axis=-1, keepdims=True)
                            - gn[h] * jnp.mean(dgn * gn[h], axis=-1, keepdims=True))
            dob.append(_mx(do))
            dodb.append(_mx(do * dq_ref[h]))
        dp = [_dot_nt(dob[h], vb[h]) for h in heads]
        dv = [_dot_tn(pb[h], dob[h]) + _dot(kdb[h], dsb[h]) for h in heads]
        dq_cross = [_dot_nt(dodb[h], _mx(s0[h])) for h in heads]
        dk_cross = [_dot_nt(vb[h], dsb[h]) for h in heads]
        ds_new = [_dot_tn(qb[h], dodb[h]) for h in heads]
        dpb = [_mx(dp[h] * m_ref[h]) for h in heads]
        dqr = [_dot(dpb[h], kb[h]) + dq_cross[h] for h in heads]
        dkr = [(_dot_tn(dpb[h], qb[h]) + dk_cross[h] * dk_ref[h]) * RET_KSCALE for h in heads]
        for h in heads:
            d_ref[:, 1024 + h * hd:1024 + (h + 1) * hd] = dv[h].astype(d_ref.dtype)
            d_ref[:, h * hd:(h + 1) * hd] = (dqr[h] * cos + _roll_half(dqr[h] * sin)).astype(d_ref.dtype)
            d_ref[:, 512 + h * hd:512 + (h + 1) * hd] = (
                dkr[h] * cos + _roll_half(dkr[h] * sin)).astype(d_ref.dtype)
            ds_scr[h] = ds[h] * gt_ref[h] + ds_new[h]
        pl.when(i == nt - 1)(finish)

    full3 = lambda a: pl.BlockSpec(a.shape, lambda i: (0, 0, 0))
    rev = lambda i: (nt - 1 - i, 0)
    return pl.pallas_call(
        body, name="retention_bwd", grid=(nt,),
        in_specs=[pl.BlockSpec((t, 2048), rev),
                  pl.BlockSpec((None, RET_HEADS, hd, hd), lambda i: (nt - 1 - i, 0, 0, 0)),
                  pl.BlockSpec((t, 512), rev),
                  pl.BlockSpec((t, LANES), rev),
                  pl.BlockSpec((t, LANES), rev),
                  full3(mask), full3(dq), full3(dk), full3(gt), HBM_SPEC],
        out_specs=[pl.BlockSpec((t, 2048), rev), HBM_SPEC],
        out_shape=[jax.ShapeDtypeStruct((s, 2048), MXU_DTYPE),
                   jax.ShapeDtypeStruct(dwo_parts.shape, dwo_parts.dtype)],
        scratch_shapes=[pltpu.VMEM((RET_HEADS, hd, hd), F32),
                        pltpu.SemaphoreType.DMA((N_PEERS,)), pltpu.SemaphoreType.DMA((N_PEERS,)),
                        pltpu.SemaphoreType.DMA((1,))],
        compiler_params=_params(("arbitrary",)),
    )(ret, states, dy, cos2, sin2, mask, dq, dk, gt, dwo_parts)


SB_BLOCK = 256
SB_CHUNK = 32


SB_SKIP = 104.0
NO_KEYS = -1e30


def _split_dots(xs, u):
    parts = []
    for x in xs:
        hi = lax.bitcast_convert_type(lax.bitcast_convert_type(x, jnp.uint32) & jnp.uint32(0xFFFF0000), F32)
        parts += [_mx(hi), _mx(x - hi)]
    out = _dot(jnp.concatenate(parts, axis=0), u)
    n = xs[0].shape[0]
    return [out[2 * k * n:(2 * k + 1) * n] + out[(2 * k + 1) * n:(2 * k + 2) * n] for k in range(len(xs))]


def _split_dot(x, u):
    return _split_dots([x], u)[0]


def _sb_pair_weights(lb, lk, allowed, u_gt):
    blk = lb.shape[0]
    lk_p, lk_d = lk[:, :blk], lk[:, blk:]
    r_d = _rowsum(lk_d)
    cs_p, cs_d = _split_dots([lk_p, lk_d], u_gt)
    a = jnp.exp(lb + jnp.concatenate([cs_p + r_d, cs_d], axis=1))
    return jnp.where(allowed, a, 0.0), r_d, r_d + _rowsum(lk_p)


def _sb_logits(q, k, causal):
    z = _dot_nt(q, k)
    l1p = jnp.log(1.0 + jnp.exp(-jnp.abs(z)))
    lk = -(jnp.maximum(z, 0.0) + l1p)
    if causal is not None:
        lk = jnp.where(causal, lk, 0.0)
    return jnp.minimum(z, 0.0) - l1p, lk


def _sb_weights(lb, lk, r, u_gt, causal):
    a = jnp.exp(lb + _split_dot(lk, u_gt) + r)
    return a if causal is None else jnp.where(causal, a, 0.0)


def _rowsum(x):
    return jnp.sum(x, axis=1, keepdims=True)


def _sb_pair_tile(i, blk):
    row = lax.broadcasted_iota(jnp.int32, (blk, 2 * blk), 0)
    col = lax.broadcasted_iota(jnp.int32, (blk, 2 * blk), 1)
    first_col = jnp.where(i >= 1, 0, blk)
    allowed = jnp.logical_and(row > col - blk, col >= first_col)
    rows_p = pl.ds(pl.multiple_of(jnp.maximum(i - 1, 0) * blk, blk), blk)
    rows_d = pl.ds(pl.multiple_of(i * blk, blk), blk)
    return allowed, rows_p, rows_d


def _sb_fwd(qkv, sg):
    s = qkv.shape[0]
    blk = min(SB_BLOCK, s)
    nq = s // blk
    hd = SB_HEAD_DIM

    ch = min(SB_CHUNK, blk)

    def body(q_ref, k_ref, v_ref, g_ref, y_ref, o_ref, z_scr, lhs_scr, cs_scr, a_scr):
        i = pl.program_id(1)
        row = lax.broadcasted_iota(jnp.int32, (blk, blk), 0)
        col = lax.broadcasted_iota(jnp.int32, (blk, blk), 1)
        u_gt = (row > col).astype(MXU_DTYPE)
        heads = [slice(hh * hd, (hh + 1) * hd) for hh in range(2)]
        qs = [q_ref[:, ls] for ls in heads]
        _, rows_p, rows_d = _sb_pair_tile(i, blk)
        has_prev = i >= 1
        crow = lax.broadcasted_iota(jnp.int32, (ch, blk), 0)
        ccol = lax.broadcasted_iota(jnp.int32, (ch, blk), 1)

        def logits(hh):
            kc = jnp.concatenate([k_ref[rows_p, heads[hh]], k_ref[rows_d, heads[hh]]], axis=0)
            z_scr[hh] = _dot_nt(qs[hh], kc)

        def keep_parts(hh):
            r_d, r_all = [], []
            for c in range(blk // ch):
                rows = pl.ds(c * ch, ch)
                causal = crow + c * ch > ccol
                z = z_scr[hh, rows, :]
                l1p = jnp.log(1.0 + jnp.exp(-jnp.abs(z)))
                lb = jnp.minimum(z, 0.0) - l1p
                z_scr[hh, rows, :] = lb
                lk = lb - z
                lk_p = lk[:, :blk]
                lk_d = jnp.where(causal, lk[:, blk:], 0.0)
                lhs_scr[hh, pl.ds(c * ch, ch), :] = _mx(lk_p)
                lhs_scr[hh, pl.ds(blk + c * ch, ch), :] = _mx(lk_d)
                r_d.append(_rowsum(lk_d))
                r_all.append(r_d[c] + _rowsum(lk_p))
            return r_d, jnp.concatenate(r_all, axis=0)

        def suffix_sums(hh):
            cs_scr[hh] = _dot(lhs_scr[hh], u_gt)

        def weights(hh, r_d):
            for c in range(blk // ch):
                rows = pl.ds(c * ch, ch)
                causal = crow + c * ch > ccol
                cs_p = cs_scr[hh, pl.ds(c * ch, ch), :] + jnp.where(has_prev, r_d[c], NO_KEYS)
                cs_d = cs_scr[hh, pl.ds(blk + c * ch, ch), :]
                lb = z_scr[hh, rows, :]
                a_p = jnp.exp(lb[:, :blk] + cs_p)
                a_d = jnp.where(causal, jnp.exp(lb[:, blk:] + cs_d), 0.0)
                a_scr[hh, rows, :] = _mx(jnp.concatenate([a_p, a_d], axis=1))

        def values(hh):
            vc = jnp.concatenate([v_ref[rows_p, heads[hh]], v_ref[rows_d, heads[hh]]], axis=0)
            return _dot(a_scr[hh], vc)

        def block(hh, j, r):
            start = pl.multiple_of(j * blk, blk)
            lb, lk = _sb_logits(qs[hh], k_ref[pl.ds(start, blk), heads[hh]], None)
            a = _sb_weights(lb, lk, r, u_gt, None)
            return _dot(_mx(a), v_ref[pl.ds(start, blk), heads[hh]]), r + _rowsum(lk)

        def more(n, r0, r1):
            return jnp.logical_and(n < i, jnp.max(jnp.maximum(r0, r1)) > -SB_SKIP)

        logits(0)
        logits(1)
        gv = g_ref[...]
        gates = gv * _sigmoid(gv)
        rd0, r0 = keep_parts(0)
        suffix_sums(0)
        rd1, r1 = keep_parts(1)
        suffix_sums(1)
        go = more(jnp.int32(1), r0, r1)
        weights(0, rd0)
        acc0 = values(0)
        weights(1, rd1)
        acc1 = values(1)

        def step(c):
            _, n, acc0, r0, acc1, r1 = c
            pv0, r0 = block(0, i - 1 - n, r0)
            pv1, r1 = block(1, i - 1 - n, r1)
            return more(n + 1, r0, r1), n + 1, acc0 + pv0, r0, acc1 + pv1, r1

        _, _, acc0, _, acc1, _ = lax.while_loop(lambda c: c[0], step, (go, jnp.int32(1), acc0, r0, acc1, r1))
        o = jnp.concatenate([acc0, acc1], axis=1)
        o_ref[...] = o
        y_ref[...] = (o * gates).astype(y_ref.dtype)

    qblk = pl.BlockSpec((blk, LANES), lambda p, i: (i, p))
    return pl.pallas_call(
        body, name="stickbreak_fwd", grid=(SB_HEADS // 2, nq),
        in_specs=[qblk,
                  pl.BlockSpec((s, LANES), lambda p, i: (0, 4 + p)),
                  pl.BlockSpec((s, LANES), lambda p, i: (0, 8 + p)),
                  qblk],
        out_specs=[qblk, qblk],
        out_shape=[jax.ShapeDtypeStruct((s, 512), MXU_DTYPE),
                   jax.ShapeDtypeStruct((s, 512), F32)],
        scratch_shapes=[pltpu.VMEM((2, blk, 2 * blk), F32),
                        pltpu.VMEM((2, 2 * blk, blk), MXU_DTYPE),
                        pltpu.VMEM((2, 2 * blk, blk), F32),
                        pltpu.VMEM((2, blk, 2 * blk), MXU_DTYPE)],
        compiler_params=_params(("arbitrary", "arbitrary")),
    )(qkv, qkv, qkv, sg)


def _sb_bwd(qkv, sg, o, dy):
    s = qkv.shape[0]
    blk = min(SB_BLOCK, s)
    nq = s // blk
    hd = SB_HEAD_DIM
    assert nq <= LANES
    ch = min(SB_CHUNK, blk)

    def body(q_ref, k_ref, v_ref, g_ref, o_ref, dy_ref, dq_ref, dk_ref, dv_ref, dg_ref, dk_scr, dv_scr,
             z_scr, g_scr, lhs_scr, cs_scr, a_scr, dz_scr):
        i = pl.program_id(1)

        @pl.when(i == 0)
        def _():
            dk_scr[...] = jnp.zeros_like(dk_scr)
            dv_scr[...] = jnp.zeros_like(dv_scr)

        row = lax.broadcasted_iota(jnp.int32, (blk, blk), 0)
        col = lax.broadcasted_iota(jnp.int32, (blk, blk), 1)
        lane = lax.broadcasted_iota(jnp.int32, (blk, LANES), 1)
        u_gt = (row > col).astype(MXU_DTYPE)
        u_lt = (row < col).astype(MXU_DTYPE)
        heads = [slice(hh * hd, (hh + 1) * hd) for hh in range(2)]
        qs = [q_ref[:, ls] for ls in heads]
        _, rows_p, rows_d = _sb_pair_tile(i, blk)
        has_prev = i >= 1
        crow = lax.broadcasted_iota(jnp.int32, (ch, blk), 0)
        ccol = lax.broadcasted_iota(jnp.int32, (ch, blk), 1)
        nch = blk // ch
        kcs = [jnp.concatenate([k_ref[rows_p, ls], k_ref[rows_d, ls]], axis=0) for ls in heads]
        dobs = []

        def gate_grads():
            g = g_ref[...]
            dyv = dy_ref[...]
            sgm = _sigmoid(g)
            dg_ref[...] = (dyv * o_ref[...] * (sgm * (1.0 + g * (1.0 - sgm)))).astype(dg_ref.dtype)
            dob = _mx(dyv * (g * sgm))
            dobs.extend(dob[:, ls] for ls in heads)

        def split_rows(hh, c, part, x):
            lhs_scr[hh, pl.ds(part * blk + c * ch, ch), :] = _mx(x)

        def summed_rows(hh, c, part):
            return cs_scr[hh, pl.ds(part * blk + c * ch, ch), :]

        def logits(hh):
            z_scr[hh] = _dot_nt(qs[hh], kcs[hh])

        def weight_grads(hh):
            vc = jnp.concatenate([v_ref[rows_p, heads[hh]], v_ref[rows_d, heads[hh]]], axis=0)
            g_scr[hh] = _dot_nt(dobs[hh], vc)

        def keep_parts(hh):
            r_d, r_all = [], []
            for c in range(nch):
                rows = pl.ds(c * ch, ch)
                z = z_scr[hh, rows, :]
                l1p = jnp.log(1.0 + jnp.exp(-jnp.abs(z)))
                lb = jnp.minimum(z, 0.0) - l1p
                z_scr[hh, rows, :] = lb
                lk = lb - z
                lk_p = lk[:, :blk]
                lk_d = jnp.where(crow + c * ch > ccol, lk[:, blk:], 0.0)
                split_rows(hh, c, 0, lk_p)
                split_rows(hh, c, 1, lk_d)
                r_d.append(_rowsum(lk_d))
                r_all.append(r_d[c] + _rowsum(lk_p))
            return r_d, jnp.concatenate(r_all, axis=0)

        def weights(hh, r_d):
            g_p = []
            for c in range(nch):
                rows = pl.ds(c * ch, ch)
                lb = z_scr[hh, rows, :]
                a_p = jnp.exp(lb[:, :blk] + (summed_rows(hh, c, 0) + jnp.where(has_prev, r_d[c], NO_KEYS)))
                a_d = jnp.where(crow + c * ch > ccol, jnp.exp(lb[:, blk:] + summed_rows(hh, c, 1)), 0.0)
                a = jnp.concatenate([a_p, a_d], axis=1)
                a_scr[hh, rows, :] = _mx(a)
                gm = g_scr[hh, rows, :] * a
                g_scr[hh, rows, :] = gm
                split_rows(hh, c, 0, gm[:, :blk])
                split_rows(hh, c, 1, gm[:, blk:])
                g_p.append(_rowsum(gm[:, :blk]))
            return g_p

        def logit_grads(hh, pg, g_p):
            for c in range(nch):
                rows = pl.ds(c * ch, ch)
                pre = jnp.concatenate([summed_rows(hh, c, 0) + pg[c * ch:(c + 1) * ch],
                                       summed_rows(hh, c, 1) + (pg[c * ch:(c + 1) * ch] + g_p[c])], axis=1)
                gm = g_scr[hh, rows, :]
                dz = gm - (gm + pre) * jnp.exp(z_scr[hh, rows, :])
                dz_p = dz[:, :blk]
                dz_d = jnp.where(crow + c * ch > ccol, dz[:, blk:], 0.0)
                dz_scr[hh, rows, :] = _mx(jnp.concatenate([dz_p, dz_d], axis=1))

        def products(hh, acc):
            ls = heads[hh]
            dk_scr[hh, rows_p, :] += _dot_tn(dz_scr[hh, :, 0:blk], qs[hh])
            dk_scr[hh, rows_d, :] += _dot_tn(dz_scr[hh, :, blk:2 * blk], qs[hh])
            dv_scr[hh, rows_p, :] += _dot_tn(a_scr[hh, :, 0:blk], dobs[hh])
            dv_scr[hh, rows_d, :] += _dot_tn(a_scr[hh, :, blk:2 * blk], dobs[hh])
            dq_ref[:, ls] = ((acc + _dot(dz_scr[hh], kcs[hh])) * SB_SCALE).astype(dq_ref.dtype)

        def suffix_sums(hh):
            cs_scr[hh] = _dot(lhs_scr[hh], u_gt)

        def prefix_sums(hh):
            cs_scr[hh] = _dot(lhs_scr[hh], u_lt)

        def more(n, r0, r1):
            return jnp.logical_and(n < i, jnp.max(jnp.maximum(r0, r1)) > -SB_SKIP)

        gate_grads()
        logits(0)
        weight_grads(0)
        logits(1)
        weight_grads(1)
        rd0, ra0 = keep_parts(0)
        suffix_sums(0)
        rd1, ra1 = keep_parts(1)
        suffix_sums(1)
        go = more(jnp.int32(1), ra0, ra1)
        gp0 = weights(0, rd0)
        prefix_sums(0)
        gp1 = weights(1, rd1)
        prefix_sums(1)

        def scan_block(hh, j, r, rmat):
            start = pl.multiple_of(j * blk, blk)
            _, lk = _sb_logits(qs[hh], k_ref[pl.ds(start, blk), heads[hh]], None)
            return r + _rowsum(lk), jnp.where(lane == j, r, rmat)

        def scan_step(c):
            _, n, r0, rmat0, r1, rmat1 = c
            r0, rmat0 = scan_block(0, i - 1 - n, r0, rmat0)
            r1, rmat1 = scan_block(1, i - 1 - n, r1, rmat1)
            return more(n + 1, r0, r1), n + 1, r0, rmat0, r1, rmat1

        zmat = jnp.zeros((blk, LANES), F32)
        _, n, _, rmat0, _, rmat1 = lax.while_loop(lambda c: c[0], scan_step,
                                                  (go, jnp.int32(1), ra0, zmat, ra1, zmat))
        rmats = (rmat0, rmat1)

        def block(hh, j, pg):
            ls = heads[hh]
            start = pl.multiple_of(j * blk, blk)
            k = k_ref[pl.ds(start, blk), ls]
            lb, lk = _sb_logits(qs[hh], k, None)
            r = _rowsum(jnp.where(lane == j, rmats[hh], 0.0))
            a = _sb_weights(lb, lk, r, u_gt, None)
            gm = _dot_nt(dobs[hh], v_ref[pl.ds(start, blk), ls]) * a
            dzb = _mx(gm - (gm + (pg + _split_dot(gm, u_lt))) * jnp.exp(lb))
            dk_scr[hh, pl.ds(start, blk), :] += _dot_tn(dzb, qs[hh])
            dv_scr[hh, pl.ds(start, blk), :] += _dot_tn(_mx(a), dobs[hh])
            return _dot(dzb, k), pg + _rowsum(gm)

        def step(t, c):
            acc0, pg0, acc1, pg1 = c
            dq0, pg0 = block(0, i - n + t, pg0)
            dq1, pg1 = block(1, i - n + t, pg1)
            return acc0 + dq0, pg0, acc1 + dq1, pg1

        zero = jnp.zeros((blk, 1), F32)
        zacc = jnp.zeros((blk, hd), F32)
        acc0, pg0, acc1, pg1 = lax.fori_loop(0, n - 1, step, (zacc, zero, zacc, zero))
        logit_grads(0, pg0, gp0)
        logit_grads(1, pg1, gp1)
        products(0, acc0)
        products(1, acc1)

        @pl.when(i == nq - 1)
        def _():
            for hh in range(2):
                ls = slice(hh * hd, (hh + 1) * hd)
                dk_ref[:, ls] = dk_scr[hh].astype(dk_ref.dtype)
                dv_ref[:, ls] = dv_scr[hh].astype(dv_ref.dtype)

    qblk = lambda c0: pl.BlockSpec((blk, LANES), lambda p, i: (i, c0 + p))
    full = lambda c0: pl.BlockSpec((s, LANES), lambda p, i: (0, c0 + p))
    half = jax.ShapeDtypeStruct((s, 512), MXU_DTYPE)
    return pl.pallas_call(
        body, name="stickbreak_bwd", grid=(SB_HEADS // 2, nq),
        in_specs=[qblk(0), full(4), full(8), qblk(0), qblk(0), qblk(4)],
        out_specs=[qblk(0), full(0), full(0), qblk(0)],
        out_shape=[half, half, half, half],
        scratch_shapes=[pltpu.VMEM((2, s, hd), F32), pltpu.VMEM((2, s, hd), F32),
                        pltpu.VMEM((2, blk, 2 * blk), F32),
                        pltpu.VMEM((2, blk, 2 * blk), F32),
                        pltpu.VMEM((2, 2 * blk, blk), MXU_DTYPE),
                        pltpu.VMEM((2, 2 * blk, blk), F32),
                        pltpu.VMEM((2, blk, 2 * blk), MXU_DTYPE),
                        pltpu.VMEM((2, blk, 2 * blk), MXU_DTYPE)],
        compiler_params=_params(("arbitrary", "arbitrary")),
    )(qkv, qkv, qkv, sg, o, dy)


def _layer_fwd(layer, x, mod, norm_g, w_in_g, w_out_g, tables, gather=(), loss_head=None):
    shift, scale1p, gate = mod[0:1], 1.0 + mod[1:2], mod[2:3]
    ret, qkv, sg, *gathered = _ln_proj(x, shift, scale1p, norm_g, w_in_g, layer, gather)
    if w_out_g is None:
        w_out_g = gathered[-1].reshape(N_DEV, 1, D_MODEL // N_DEV, D_MODEL)
    y_r, states = _retention_fwd(ret, tables)
    y_s, o_s = _sb_fwd(qkv, sg)
    x_next = _out_proj(x, gate, y_r, y_s, w_out_g, layer, loss_head)
    saved = (x, shift, scale1p, gate, ret, qkv, sg, y_r, states, y_s, o_s)
    return x_next, saved, gathered


def _layer_bwd(layer, me, dx_out, saved, norm_g, w_in_g, w_out_g, tables, ride_along=None):
    x, shift, scale1p, gate, ret, qkv, sg, y_r, states, y_s, o_s = saved
    dy, dw_out, dgate = _out_proj_bwd(dx_out, gate, y_r, y_s, w_out_g, layer)
    dwo_parts = _mx(dw_out.reshape(N_DEV, D_MODEL // N_DEV, D_MODEL))
    d_ret, r_out = _retention_bwd(ret, states, dy, tables, dwo_parts)
    d_q, d_k, d_v, d_g = _sb_bwd(qkv, sg, o_s, dy)
    dproj = (d_ret, d_q, d_k, d_v, d_g)
    dx, dshift, dscale, dnorm_g, h_t = _in_proj_bwd_x(x, dx_out, dproj, shift, scale1p, norm_g, w_in_g, layer)
    dmod = jnp.concatenate([dshift, dscale, dgate], axis=1)
    gather = ride_along(dmod, dnorm_g) if ride_along else ()
    r_in, *gathered = _in_proj_bwd_w(me, h_t, dproj, gather)
    return dx, r_in, r_out, dmod, dnorm_g, gathered


MESH_IDS = pl.DeviceIdType.MESH
N_PEERS = N_DEV - 1
HBM_SPEC = pl.BlockSpec(memory_space=pl.ANY)


def _my_place():
    return lax.axis_index("x"), lax.axis_index("y"), lax.axis_index("c")


def _linear(px, py, pc):
    return 4 * px + 2 * py + pc


def _all_gather(blocks):
    n_arr = len(blocks)

    def body(*refs):
        start, forward, finish = _gather_plan(refs[:n_arr], refs[n_arr:2 * n_arr], *refs[2 * n_arr:])
        start()
        forward()
        finish()

    return pl.pallas_call(
        body, name="all_gather",
        out_shape=_gathered_shapes(blocks),
        in_specs=[HBM_SPEC] * n_arr, out_specs=[HBM_SPEC] * n_arr,
        scratch_shapes=_gather_sems(n_arr),
    )(*blocks)


def _gathered_shapes(blocks):
    return [jax.ShapeDtypeStruct((N_DEV * b.shape[0], b.shape[1]), b.dtype) for b in blocks]


def _gather_sems(n_arr):
    return [pltpu.SemaphoreType.DMA((n_arr * N_PEERS,)), pltpu.SemaphoreType.DMA((n_arr * N_PEERS,)),
            pltpu.SemaphoreType.DMA((n_arr,))]


def _gather_plan(x_refs, out_refs, send_sems, recv_sems, local_sems):
    n_arr = len(x_refs)
    x, y, c = _my_place()
    me, sibling = (x, y, c), (x, y, 1 - c)
    chips = [(1 - x, y), (x, 1 - y), (1 - x, 1 - y)]

    def rows(a, place):
        m = x_refs[a].shape[0]
        return out_refs[a].at[pl.ds(_linear(*place) * m, m), :]

    def copy(a, k, block, to, src=None):
        return pltpu.make_async_remote_copy(
            src_ref=rows(a, block) if src is None else src, dst_ref=rows(a, block),
            send_sem=send_sems.at[a * N_PEERS + k], recv_sem=recv_sems.at[a * N_PEERS + k],
            device_id=to, device_id_type=MESH_IDS)

    mine = [pltpu.make_async_copy(x_refs[a], rows(a, me), local_sems.at[a]) for a in range(n_arr)]
    first = []
    for a in range(n_arr):
        first.append(copy(a, 0, me, sibling, src=x_refs[a]))
        first += [copy(a, 1 + j, me, (*chip, c), src=x_refs[a]) for j, chip in enumerate(chips)]
    passed = [copy(a, 4 + j, (*chip, c), sibling) for j, chip in enumerate(chips) for a in range(n_arr)]

    def start():
        for cp in mine + first:
            cp.start()

    def forward():
        for j, chip in enumerate(chips):
            for a in range(n_arr):
                copy(a, 1 + j, (*chip, c), me).wait_recv()
                passed[j * n_arr + a].start()

    def finish():
        for a in range(n_arr):
            copy(a, 0, sibling, me).wait_recv()
            for j, chip in enumerate(chips):
                copy(a, 4 + j, (*chip, 1 - c), me).wait_recv()
        for cp in first + passed:
            cp.wait_send()
        for cp in mine:
            cp.wait()

    return start, forward, finish


def _ada_fwd(c_all, w_ada, b_cols):
    cols = w_ada.shape[2]

    def body(c_ref, w_ref, b_ref, ca_ref, mod_ref):
        cv = c_ref[...]
        ca = cv * _sigmoid(cv)
        ca_ref[...] = ca
        cb = _mx(jnp.concatenate([ca, ca], axis=0))
        for l in range(DEPTH):
            mod_ref[l * N_DEV:(l + 1) * N_DEV, :] = _dot(cb, _mx(w_ref[l]))[0:N_DEV] + b_ref[l]

    return pl.pallas_call(
        body, name="ada_fwd",
        out_shape=[jax.ShapeDtypeStruct((N_DEV, D_MODEL), F32),
                   jax.ShapeDtypeStruct((DEPTH * N_DEV, cols), F32)],
        compiler_params=_params(),
    )(c_all, w_ada, b_cols)


def _ada_bwd(c_act_t, dmod_cols):
    cols = dmod_cols.shape[2]

    def body(ca_ref, dm_ref, o_ref):
        ca = _mx(ca_ref[...]).astype(F32)
        for l in range(DEPTH):
            o_ref[l] = jnp.dot(ca, _mx(dm_ref[l]).astype(F32),
                               precision=lax.Precision.HIGHEST, preferred_element_type=F32)

    return pl.pallas_call(
        body, name="ada_bwd",
        out_shape=jax.ShapeDtypeStruct((DEPTH, D_MODEL, cols), F32),
        compiler_params=_params(),
    )(c_act_t, dmod_cols)


def _adamw_store(g, w_ref, m_ref, v_ref, g_ref, d_ref, mo_ref, vo_ref):
    m2 = ADAM_B1 * m_ref[...] + (1.0 - ADAM_B1) * g
    v2 = ADAM_B2 * v_ref[...] + (1.0 - ADAM_B2) * (g * g)
    m_hat = m2 / (1.0 - ADAM_B1 ** ADAM_STEP)
    v_hat = v2 / (1.0 - ADAM_B2 ** ADAM_STEP)
    g_ref[...] = g
    d_ref[...] = -ADAM_LR * (m_hat / (jnp.sqrt(v_hat) + ADAM_EPS) + ADAM_WD * w_ref[...])
    mo_ref[...] = m2
    vo_ref[...] = v2


def _slab_sum(p_ref):
    g = p_ref[0].astype(F32)
    for sl in range(1, p_ref.shape[0]):
        g = g + p_ref[sl].astype(F32)
    return g


def _sum_adamw_layers(parts, w, m, v):
    n_slab, rows, cols = parts[0].shape
    tr = min(256, rows)
    nt = rows // tr

    def body(p0_ref, p1_ref, w_ref, m_ref, v_ref, g_ref, d_ref, mo_ref, vo_ref):
        for l, p_ref in enumerate((p0_ref, p1_ref)):
            @pl.when(pl.program_id(0) == l)
            def _():
                _adamw_store(_slab_sum(p_ref), w_ref, m_ref, v_ref, g_ref, d_ref, mo_ref, vo_ref)

    p_specs = [pl.BlockSpec((n_slab, tr, cols), lambda l, i: (0, i * (1 - l) + (nt - 1) * l, 0)),
               pl.BlockSpec((n_slab, tr, cols), lambda l, i: (0, i * l, 0))]
    blk = pl.BlockSpec((None, tr, cols), lambda l, i: (l, i, 0))
    shp = jax.ShapeDtypeStruct((DEPTH, rows, cols), F32)
    return pl.pallas_call(
        body, name="sum_adamw_layers", grid=(DEPTH, nt),
        in_specs=p_specs + [blk, blk, blk],
        out_specs=[blk, blk, blk, blk],
        out_shape=[shp, shp, shp, shp],
        compiler_params=_params(("arbitrary", "arbitrary")),
    )(parts[0], parts[1], w, m, v)


def _sum_adamw(parts, w, m, v):
    n_slab, rows, cols = parts.shape
    tr = min(256, rows)

    def body(p_ref, w_ref, m_ref, v_ref, g_ref, d_ref, mo_ref, vo_ref):
        _adamw_store(_slab_sum(p_ref), w_ref, m_ref, v_ref, g_ref, d_ref, mo_ref, vo_ref)

    blk = pl.BlockSpec((tr, cols), lambda i: (i, 0))
    shp = jax.ShapeDtypeStruct((rows, cols), F32)
    return pl.pallas_call(
        body, name="sum_adamw", grid=(rows // tr,),
        in_specs=[pl.BlockSpec((n_slab, tr, cols), lambda i: (0, i, 0)), blk, blk, blk],
        out_specs=[blk, blk, blk, blk],
        out_shape=[shp, shp, shp, shp],
        compiler_params=_params(("arbitrary",)),
    )(parts, w, m, v)


SMALL_ROWS = 16


def kernel(x, c, norm_g, w_ada, b_ada, w_in, w_out, final_g, loss_target, m_norm_g, m_w_ada, m_b_ada, m_w_in, m_w_out, m_final_g, v_norm_g, v_w_ada, v_b_ada, v_w_in, v_w_out, v_final_g):
    me = _linear(*_my_place())
    in_cols = w_in.shape[2]
    out_rows = w_out.shape[1]
    ada_cols = w_ada.shape[2]

    w_in_m, w_out_m = _mx(w_in), _mx(w_out)
    g_in, g_c = _all_gather([w_in_m[0], jnp.broadcast_to(c, (8, D_MODEL))])
    w_in_g = [g_in.reshape(N_DEV, 1, D_MODEL, in_cols), None]
    w_out_g = [None, None]
    c_all = g_c.reshape(N_DEV, 8, D_MODEL)[:, 0]

    b_cols = lax.dynamic_slice_in_dim(b_ada, me * ada_cols, ada_cols, axis=1)[:, None, :]
    c_act, mod_cols = _ada_fwd(c_all, w_ada, b_cols)
    (g_mod,) = _all_gather([mod_cols])
    g_mod = g_mod.reshape(N_DEV, DEPTH, N_DEV, ada_cols)
    mod = lax.dynamic_index_in_dim(g_mod, me, axis=2, keepdims=False)
    mod = mod.transpose(1, 0, 2).reshape(DEPTH, 3, D_MODEL)

    tables = _ret_tables(x.shape[1])
    h = x[0]
    saved = []
    for l in range(DEPTH):
        nxt = (w_in_m[l + 1], w_out_m[l + 1], w_out_m[l]) if l + 1 < DEPTH else ()
        head = None if nxt else (final_g[None], loss_target[0])
        h, sv, gathered = _layer_fwd(0, h, mod[l], norm_g[l:l + 1], w_in_g[l], w_out_g[l], tables, nxt, head)
        if nxt:
            w_in_g[l + 1] = gathered[0].reshape(N_DEV, 1, D_MODEL, in_cols)
            w_out_g[l + 1] = gathered[1].reshape(N_DEV, 1, out_rows, D_MODEL)
            w_out_g[l] = gathered[2].reshape(N_DEV, 1, out_rows, D_MODEL)
        saved.append(sv)
    dx, loss_part, dfg = h
    r_in, r_out, small = [None] * DEPTH, [None] * DEPTH, [None] * DEPTH

    def small_block(dmod0, dng0):
        pad = jnp.zeros((SMALL_ROWS - 10, D_MODEL), F32)
        return (jnp.concatenate([dmod0.reshape(3, D_MODEL), small[1][0], dng0, small[1][1], dfg,
                                 jnp.broadcast_to(loss_part, (1, D_MODEL)), pad], axis=0),)

    for l in reversed(range(DEPTH)):
        dx, r_in[l], r_out[l], dmod, dng, gathered = _layer_bwd(
            0, me, dx, saved[l], norm_g[l:l + 1], w_in_g[l], w_out_g[l], tables, small_block if l == 0 else None)
        small[l] = (dmod.reshape(3, D_MODEL), dng)
    g_small = gathered[0].reshape(N_DEV, SMALL_ROWS, D_MODEL)

    def small_pack(b, n, f, fill):
        return jnp.concatenate([b.reshape(6, D_MODEL), n, f[None],
                                jnp.full((SMALL_ROWS - 9, D_MODEL), fill, F32)], axis=0)

    s_g, s_d, s_m, s_v = _sum_adamw(g_small, small_pack(b_ada, norm_g, final_g, 0.0),
                                    small_pack(m_b_ada, m_norm_g, m_final_g, 0.0),
                                    small_pack(v_b_ada, v_norm_g, v_final_g, 1.0))
    loss = s_g[9, 0]

    def small_unpack(a):
        return a[0:6].reshape(DEPTH, 3 * D_MODEL), a[6:8], a[8]

    dmod_all = g_small[:, 0:6].reshape(N_DEV, DEPTH, 3 * D_MODEL).transpose(1, 0, 2)
    dmod_cols = lax.dynamic_slice_in_dim(dmod_all, me * ada_cols, ada_cols, axis=2)
    g_ada = _ada_bwd(c_act.T, dmod_cols).reshape(1, DEPTH * D_MODEL, ada_cols)
    ada = _sum_adamw(g_ada, *[a.reshape(DEPTH * D_MODEL, ada_cols) for a in (w_ada, m_w_ada, v_w_ada)])
    ada = [a.reshape(DEPTH, D_MODEL, ada_cols) for a in ada]

    win = _sum_adamw_layers(r_in, w_in, m_w_in, v_w_in)
    wout = _sum_adamw_layers(r_out, w_out, m_w_out, v_w_out)

    outs = [loss, dx[None]]
    for k in range(4):
        b, n, f = small_unpack((s_g, s_d, s_m, s_v)[k])
        outs += [n, ada[k], b, win[k], wout[k], f]
    return tuple(outs)
```

```python
import functools

import jax
import jax.numpy as jnp
from jax import lax
from jax.experimental import pallas as pl
from jax.experimental.pallas import tpu as pltpu

F32 = jnp.float32
MXU_DTYPE = jnp.bfloat16

D_MODEL = 1024
DEPTH = 2
N_DEV = 8
CHUNK = 64
D_RET = 512
D_SB = 512
RET_HEADS = 4
RET_HEAD_DIM = 128
SB_HEADS = 8
SB_HEAD_DIM = 64
D_IN = 4096
ROPE_BASE = 10000.0
EPS = 1e-6
SB_SCALE = SB_HEAD_DIM ** -0.5
RET_KSCALE = RET_HEAD_DIM ** -0.5

ADAM_LR = 0.001
ADAM_B1 = 0.9
ADAM_B2 = 0.999
ADAM_EPS = 1e-08
ADAM_WD = 0.01
ADAM_STEP = 10

V7X_VMEM_BYTES = 64 * 2 ** 20
VMEM_LIMIT = V7X_VMEM_BYTES - 8 * 2 ** 20
LANES = 128

_NT = (((1,), (1,)), ((), ()))
_TN = (((0,), (0,)), ((), ()))


def _dot(a, b):
    return jnp.dot(a, b, preferred_element_type=F32)


def _dot_nt(a, b):
    return lax.dot_general(a, b, _NT, preferred_element_type=F32)


def _dot_tn(a, b):
    return lax.dot_general(a, b, _TN, preferred_element_type=F32)


def _mx(x):
    return x.astype(MXU_DTYPE)


def _sigmoid(x):
    return 1.0 / (1.0 + jnp.exp(-x))


def _params(sem=None):
    return pltpu.CompilerParams(dimension_semantics=sem, vmem_limit_bytes=VMEM_LIMIT)


def _row_tile(s):
    return min(512, s)


def _w_in_spec(w_in_g, layer):
    return pl.BlockSpec((N_DEV, None) + w_in_g.shape[2:], lambda i: (0, layer, 0, 0))


def _w_out_spec(w_out_g, layer):
    return pl.BlockSpec((N_DEV, None) + w_out_g.shape[2:], lambda i: (0, layer, 0, 0))


def _ln_proj(x, shift, scale1p, g, w_in_g, layer, gather=()):
    s = x.shape[0]
    ts = _row_tile(s)
    ns = s // ts
    n_g = len(gather)

    def body(x_ref, sh_ref, sc_ref, g_ref, w_ref, *rest):
        ret_ref, qkv_ref, sg_ref = rest[n_g:n_g + 3]
        if n_g:
            start, forward, finish = _gather_plan(rest[:n_g], rest[n_g + 3:2 * n_g + 3], *rest[2 * n_g + 3:])
            i = pl.program_id(0)
            pl.when(i == 0)(start)
            pl.when(i == max(ns - 2, 0))(forward)
        xv = x_ref[...]
        rstd = lax.rsqrt(jnp.mean(xv * xv, axis=-1, keepdims=True) + EPS)
        h = (xv * rstd * g_ref[...]) * sc_ref[...] + sh_ref[...]
        hb = _mx(h)
        for n in range(4):
            ret_ref[:, n * 512:(n + 1) * 512] = _dot(hb, w_ref[n])
        qkv_ref[:, 0:512] = _mx(_dot(hb, w_ref[4]) * SB_SCALE)
        qkv_ref[:, 512:1024] = _mx(_dot(hb, w_ref[5]))
        qkv_ref[:, 1024:1536] = _mx(_dot(hb, w_ref[6]))
        sg_ref[...] = _dot(hb, w_ref[7])
        if n_g:
            pl.when(i == ns - 1)(finish)

    vec = pl.BlockSpec((1, D_MODEL), lambda i: (0, 0))
    return pl.pallas_call(
        body, name="ln_proj_gather" if n_g else "ln_proj", grid=(ns,),
        in_specs=[pl.BlockSpec((ts, D_MODEL), lambda i: (i, 0)), vec, vec, vec,
                  _w_in_spec(w_in_g, layer)] + [HBM_SPEC] * n_g,
        out_specs=[pl.BlockSpec((ts, 2048), lambda i: (i, 0)),
                   pl.BlockSpec((ts, 1536), lambda i: (i, 0)),
                   pl.BlockSpec((ts, 512), lambda i: (i, 0))] + [HBM_SPEC] * n_g,
        out_shape=[jax.ShapeDtypeStruct((s, 2048), F32),
                   jax.ShapeDtypeStruct((s, 1536), MXU_DTYPE),
                   jax.ShapeDtypeStruct((s, 512), F32)] + _gathered_shapes(gather),
        scratch_shapes=_gather_sems(n_g) if n_g else (),
        compiler_params=_params(("arbitrary",)),
    )(x, shift, scale1p, g, w_in_g, *gather)


def _w_out_halves(w_ref):
    half = N_DEV // 2
    return (w_ref[0:half].reshape(D_RET, D_MODEL), w_ref[half:N_DEV].reshape(D_SB, D_MODEL))


def _out_proj(x, gate, y_r, y_s, w_out_g, layer, loss_head=None):
    s = x.shape[0]
    ts = min(2 * _row_tile(s), s)

    def layer_out(x_ref, gate_ref, yr_ref, ys_ref, w_ref):
        w_r, w_s = _w_out_halves(w_ref)
        return x_ref[...] + gate_ref[...] * (_dot(yr_ref[...], w_r) + _dot(ys_ref[...], w_s))

    def body(x_ref, gate_ref, yr_ref, ys_ref, w_ref, o_ref):
        o_ref[...] = layer_out(x_ref, gate_ref, yr_ref, ys_ref, w_ref)

    def body_loss(x_ref, gate_ref, yr_ref, ys_ref, w_ref, fg_ref, t_ref, dx_ref, loss_ref, dfg_ref):
        @pl.when(pl.program_id(0) == 0)
        def _():
            loss_ref[...] = jnp.zeros_like(loss_ref)
            dfg_ref[...] = jnp.zeros_like(dfg_ref)

        xv = layer_out(x_ref, gate_ref, yr_ref, ys_ref, w_ref)
        fgv = fg_ref[...]
        rstd = lax.rsqrt(jnp.mean(xv * xv, axis=-1, keepdims=True) + EPS)
        xn = xv * rstd
        err = xn * fgv - t_ref[...]
        tok = jnp.mean(err * err, axis=-1, keepdims=True)
        loss_ref[...] += 0.5 * jnp.sum(tok, axis=0, keepdims=True)
        dy = err * (1.0 / D_MODEL)
        dfg_ref[...] += jnp.sum(dy * xn, axis=0, keepdims=True)
        dxn = dy * fgv
        dx_ref[...] = rstd * (dxn - xn * jnp.mean(dxn * xn, axis=-1, keepdims=True))

    rows = pl.BlockSpec((ts, D_MODEL), lambda i: (i, 0))
    vec = pl.BlockSpec((1, D_MODEL), lambda i: (0, 0))
    in_specs = [rows, vec, pl.BlockSpec((ts, 512), lambda i: (i, 0)), pl.BlockSpec((ts, 512), lambda i: (i, 0)),
                _w_out_spec(w_out_g, layer)]
    if loss_head is None:
        return pl.pallas_call(
            body, name="out_proj", grid=(s // ts,), in_specs=in_specs, out_specs=rows,
            out_shape=jax.ShapeDtypeStruct((s, D_MODEL), F32),
            compiler_params=_params(("arbitrary",)),
        )(x, gate, y_r, y_s, w_out_g)
    return pl.pallas_call(
        body_loss, name="out_proj_loss", grid=(s // ts,), in_specs=in_specs + [vec, rows],
        out_specs=[rows, pl.BlockSpec((1, 1), lambda i: (0, 0)), vec],
        out_shape=[jax.ShapeDtypeStruct((s, D_MODEL), F32),
                   jax.ShapeDtypeStruct((1, 1), F32),
                   jax.ShapeDtypeStruct((1, D_MODEL), F32)],
        compiler_params=_params(("arbitrary",)),
    )(x, gate, y_r, y_s, w_out_g, *loss_head)


def _out_proj_bwd(dx_out, gate, y_r, y_s, w_out_g, layer):
    s = dx_out.shape[0]
    ts = min(2 * _row_tile(s), s)
    ns = s // ts

    def body(dx_ref, gate_ref, yr_ref, ys_ref, w_ref, dy_ref, dw_ref, dgate_ref):
        i = pl.program_id(0)

        @pl.when(i == 0)
        def _():
            dw_ref[...] = jnp.zeros_like(dw_ref)

        dxv = dx_ref[...]
        dt = _mx(dxv * gate_ref[...])
        dxb = _mx(dxv)
        w_r, w_s = _w_out_halves(w_ref)
        dy_ref[:, 0:512] = _dot_nt(dt, w_r)
        dy_ref[:, 512:1024] = _dot_nt(dt, w_s)
        dw_ref[0:512, :] += _dot_tn(yr_ref[...], dxb)
        dw_ref[512:1024, :] += _dot_tn(ys_ref[...], dxb)

        @pl.when(i == ns - 1)
        def _():
            m_r, m_s = dw_ref[0:512, :], dw_ref[512:1024, :]
            dgate_ref[...] = (jnp.sum(w_r.astype(F32) * m_r, axis=0, keepdims=True)
                              + jnp.sum(w_s.astype(F32) * m_s, axis=0, keepdims=True))
            dw_ref[...] = dw_ref[...] * gate_ref[...]

    return pl.pallas_call(
        body, name="out_proj_bwd", grid=(ns,),
        in_specs=[pl.BlockSpec((ts, D_MODEL), lambda i: (i, 0)),
                  pl.BlockSpec((1, D_MODEL), lambda i: (0, 0)),
                  pl.BlockSpec((ts, 512), lambda i: (i, 0)),
                  pl.BlockSpec((ts, 512), lambda i: (i, 0)),
                  _w_out_spec(w_out_g, layer)],
        out_specs=[pl.BlockSpec((ts, D_MODEL), lambda i: (i, 0)),
                   pl.BlockSpec((D_MODEL, D_MODEL), lambda i: (0, 0)),
                   pl.BlockSpec((1, D_MODEL), lambda i: (0, 0))],
        out_shape=[jax.ShapeDtypeStruct((s, D_MODEL), F32),
                   jax.ShapeDtypeStruct((D_MODEL, D_MODEL), F32),
                   jax.ShapeDtypeStruct((1, D_MODEL), F32)],
        compiler_params=_params(("arbitrary",)),
    )(dx_out, gate, y_r, y_s, w_out_g)


def _scatter_plan(parts_ref, recv_ref, send_sems, recv_sems, local_sem):
    px, py, pc = _my_place()
    mine = _linear(px, py, pc)

    def copy(r):
        peer = (1 - px if r & 4 else px, 1 - py if r & 2 else py, 1 - pc if r & 1 else pc)
        return pltpu.make_async_remote_copy(
            src_ref=parts_ref.at[_linear(*peer)], dst_ref=recv_ref.at[mine],
            send_sem=send_sems.at[r - 1], recv_sem=recv_sems.at[r - 1],
            device_id=peer, device_id_type=MESH_IDS)

    own = pltpu.make_async_copy(parts_ref.at[mine], recv_ref.at[mine], local_sem.at[0])

    def start():
        own.start()
        for r in range(1, N_DEV):
            copy(r).start()

    def finish():
        for r in range(1, N_DEV):
            copy(r).wait_recv()
            copy(r).wait_send()
        own.wait()

    return start, finish


def _in_proj_bwd_x(x, dx_out, dproj, shift, scale1p, g, w_in_g, layer):
    s = x.shape[0]
    ts = _row_tile(s)
    ns = s // ts
    nb = D_IN // N_DEV

    def body(x_ref, dxo_ref, dr_ref, d4_ref, d5_ref, d6_ref, d7_ref, sh_ref, sc_ref, g_ref, w_ref,
             dx_ref, dsh_ref, dsc_ref, dg_ref, ht_ref):
        i = pl.program_id(0)

        @pl.when(i == 0)
        def _():
            dsh_ref[...] = jnp.zeros_like(dsh_ref)
            dsc_ref[...] = jnp.zeros_like(dsc_ref)
            dg_ref[...] = jnp.zeros_like(dg_ref)

        dh = _dot_nt(dr_ref[:, 0:nb], w_ref[0])
        for n in range(1, 4):
            dh += _dot_nt(dr_ref[:, n * nb:(n + 1) * nb], w_ref[n])
        for n, d_ref in zip(range(4, N_DEV), (d4_ref, d5_ref, d6_ref, d7_ref)):
            dh += _dot_nt(d_ref[...], w_ref[n])
        xv = x_ref[...]
        gv = g_ref[...]
        scv = sc_ref[...]
        rstd = lax.rsqrt(jnp.mean(xv * xv, axis=-1, keepdims=True) + EPS)
        xn = xv * rstd
        xg = xn * gv
        ht_ref[...] = _mx((xg * scv + sh_ref[...]).T)
        dsh_ref[...] += jnp.sum(dh, axis=0, keepdims=True)
        dsc_ref[...] += jnp.sum(dh * xg, axis=0, keepdims=True)
        dhs = dh * scv
        dg_ref[...] += jnp.sum(dhs * xn, axis=0, keepdims=True)
        dxn = dhs * gv
        dx_ref[...] = rstd * (dxn - xn * jnp.mean(dxn * xn, axis=-1, keepdims=True)) + dxo_ref[...]

    vec = pl.BlockSpec((1, D_MODEL), lambda i: (0, 0))
    return pl.pallas_call(
        body, name="in_proj_bwd_x", grid=(ns,),
        in_specs=[pl.BlockSpec((ts, D_MODEL), lambda i: (i, 0)),
                  pl.BlockSpec((ts, D_MODEL), lambda i: (i, 0)),
                  pl.BlockSpec((ts, 4 * nb), lambda i: (i, 0))]
                 + [pl.BlockSpec((ts, nb), lambda i: (i, 0))] * 4
                 + [vec, vec, vec, _w_in_spec(w_in_g, layer)],
        out_specs=[pl.BlockSpec((ts, D_MODEL), lambda i: (i, 0)), vec, vec, vec,
                   pl.BlockSpec((D_MODEL, ts), lambda i: (0, i))],
        out_shape=[jax.ShapeDtypeStruct((s, D_MODEL), F32),
                   jax.ShapeDtypeStruct((1, D_MODEL), F32),
                   jax.ShapeDtypeStruct((1, D_MODEL), F32),
                   jax.ShapeDtypeStruct((1, D_MODEL), F32),
                   jax.ShapeDtypeStruct((D_MODEL, s), MXU_DTYPE)],
        compiler_params=_params(("arbitrary",)),
    )(x, dx_out, *dproj, shift, scale1p, g, w_in_g)


def _in_proj_bwd_w(me, h_t, dproj, gather=()):
    s = h_t.shape[1]
    ts = min(4 * _row_tile(s), s)
    ns = s // ts
    nb = D_IN // N_DEV
    n_chip = N_DEV // 2
    n_g = len(gather)

    def flip_bits(j):
        return jnp.where(j == 0, 4, jnp.where(j == 1, 2, jnp.where(j == 2, 6, 0)))

    def slab_of(t, me_ref):
        return jnp.bitwise_xor(me_ref[0], flip_bits(t // 2) + 1 - t % 2)

    def body(me_ref, ht_ref, dr_ref, d4_ref, d5_ref, d6_ref, d7_ref, *rest):
        rin_ref = rest[n_g]
        (acc, stage, pre_buf, pre_send, pre_recv, sum_send, sum_recv, local_sem) = rest[2 * n_g + 1:2 * n_g + 9]
        t = pl.program_id(0)
        i = pl.program_id(1)
        if n_g:
            g_start, g_forward, g_finish = _gather_plan(rest[:n_g], rest[n_g + 1:2 * n_g + 1], *rest[2 * n_g + 9:])
            pl.when(jnp.logical_and(t == 0, i == 0))(g_start)
            pl.when(jnp.logical_and(t == N_DEV // 2, i == 0))(g_forward)
        j = t // 2
        summing = t % 2 == 1
        slab = slab_of(t, me_ref)
        px, py, pc = _my_place()

        def pre_copy(jj):
            return pltpu.make_async_remote_copy(
                src_ref=stage.at[jj % 2], dst_ref=pre_buf.at[jj],
                send_sem=pre_send.at[jj], recv_sem=pre_recv.at[jj],
                device_id=(px, py, 1 - pc), device_id_type=MESH_IDS)

        def sum_copy(jj):
            fx = jnp.logical_or(jj == 0, jj == 2)
            fy = jnp.logical_or(jj == 1, jj == 2)
            return pltpu.make_async_remote_copy(
                src_ref=stage.at[2 + jj % 2], dst_ref=rin_ref.at[jj],
                send_sem=sum_send.at[jj], recv_sem=sum_recv.at[jj],
                device_id=(jnp.where(fx, 1 - px, px), jnp.where(fy, 1 - py, py), pc), device_id_type=MESH_IDS)

        own = pltpu.make_async_copy(stage.at[3], rin_ref.at[n_chip - 1], local_sem.at[0])

        @pl.when(i == 0)
        def _():
            acc[...] = jnp.zeros_like(acc)

        @pl.when(slab < 4)
        def _():
            acc[...] += _dot(ht_ref[...], dr_ref[...])

        for n, d_ref in zip(range(4, N_DEV), (d4_ref, d5_ref, d6_ref, d7_ref)):
            @pl.when(slab == n)
            def _():
                acc[...] += _dot(ht_ref[...], d_ref[...])

        @pl.when(jnp.logical_and(i == ns - 1, jnp.logical_not(summing)))
        def _():
            @pl.when(j >= 2)
            def _():
                pre_copy(j - 2).wait_send()

            stage[j % 2] = acc[...].astype(stage.dtype)
            pre_copy(j).start()

        @pl.when(jnp.logical_and(i == ns - 1, summing))
        def _():
            pre_copy(j).wait_recv()

            @pl.when(j >= 2)
            def _():
                sum_copy(j - 2).wait_send()

            stage[2 + j % 2] = (acc[...] + pre_buf[j].astype(F32)).astype(stage.dtype)

            @pl.when(j < n_chip - 1)
            def _():
                sum_copy(j).start()

            @pl.when(j == n_chip - 1)
            def _():
                own.start()
                pre_copy(n_chip - 2).wait_send()
                pre_copy(n_chip - 1).wait_send()
                sum_copy(n_chip - 2).wait_send()
                for jj in range(n_chip - 1):
                    sum_copy(jj).wait_recv()
                own.wait()
                if n_g:
                    g_finish()

    def part_rows(n, t, i, me_ref):
        return jnp.where(slab_of(t, me_ref) == n, i, ns - 1), 0

    return pl.pallas_call(
        body, name="in_proj_bwd_w",
        grid_spec=pltpu.PrefetchScalarGridSpec(
            num_scalar_prefetch=1, grid=(N_DEV, ns),
            in_specs=[pl.BlockSpec((D_MODEL, ts), lambda t, i, me_ref: (0, i)),
                      pl.BlockSpec((ts, nb), lambda t, i, me_ref: (
                          jnp.where(slab_of(t, me_ref) < 4, i, ns - 1), jnp.minimum(slab_of(t, me_ref), 3)))]
                     + [pl.BlockSpec((ts, nb), functools.partial(part_rows, n)) for n in range(4, N_DEV)]
                     + [HBM_SPEC] * n_g,
            out_specs=[HBM_SPEC] * (1 + n_g),
            scratch_shapes=[pltpu.VMEM((D_MODEL, nb), F32),
                            pltpu.VMEM((4, D_MODEL, nb), MXU_DTYPE),
                            pltpu.VMEM((n_chip, D_MODEL, nb), MXU_DTYPE),
                            pltpu.SemaphoreType.DMA((n_chip,)), pltpu.SemaphoreType.DMA((n_chip,)),
                            pltpu.SemaphoreType.DMA((n_chip - 1,)), pltpu.SemaphoreType.DMA((n_chip - 1,)),
                            pltpu.SemaphoreType.DMA((1,))] + (_gather_sems(n_g) if n_g else [])),
        out_shape=[jax.ShapeDtypeStruct((n_chip, D_MODEL, nb), MXU_DTYPE)] + _gathered_shapes(gather),
        compiler_params=_params(("arbitrary", "arbitrary")),
    )(jnp.reshape(me, (1,)).astype(jnp.int32), h_t, *dproj, *gather)


RET_TILE = 256


def _ret_tables(s):
    t = min(RET_TILE, s)
    half = RET_HEAD_DIM // 2
    pos = jnp.arange(s, dtype=F32)
    inv = ROPE_BASE ** (-jnp.arange(half, dtype=F32) / half)
    ang = pos[:, None] * jnp.concatenate([inv, inv])[None, :]
    cos2 = jnp.cos(ang)
    sin2 = jnp.sin(ang) * jnp.concatenate([-jnp.ones((half,), F32), jnp.ones((half,), F32)])[None, :]
    lg = jnp.log1p(-(2.0 ** (-5.0 - jnp.arange(RET_HEADS, dtype=F32))))[:, None, None]
    n = jnp.arange(t)
    dist = (n[:, None] - n[None, :]).astype(F32)[None]
    cn = (n // CHUNK)[:, None]
    cm = (n // CHUNK)[None, :]
    mask = jnp.where((cn == cm)[None], jnp.exp(jnp.abs(dist) * lg),
                     jnp.where((cm < cn)[None], jnp.exp(dist * lg), 0.0))
    nf = n.astype(F32)[None, :, None]
    dq = jnp.broadcast_to(jnp.exp((nf + 1.0) * lg), (RET_HEADS, t, LANES))
    dk = jnp.broadcast_to(jnp.exp((t - 1.0 - nf) * lg), (RET_HEADS, t, LANES))
    gt = jnp.broadcast_to(jnp.exp(float(t) * lg), (RET_HEADS, 1, LANES))
    return cos2, sin2, mask, dq, dk, gt


def _roll_half(x):
    return pltpu.roll(x, RET_HEAD_DIM // 2, 1)


def _ret_heads_fwd(ret_ref, cos, sin, m_ref, dq_ref, dk_ref, s0):
    hd = RET_HEAD_DIM
    heads = range(RET_HEADS)
    qb, kb, vb, kdb = [], [], [], []
    for h in heads:
        q = ret_ref[:, h * hd:(h + 1) * hd]
        k = ret_ref[:, 512 + h * hd:512 + (h + 1) * hd]
        kr = (k * cos + _roll_half(k) * sin) * RET_KSCALE
        qb.append(_mx(q * cos + _roll_half(q) * sin))
        kb.append(_mx(kr))
        kdb.append(_mx(kr * dk_ref[h]))
        vb.append(_mx(ret_ref[:, 1024 + h * hd:1024 + (h + 1) * hd]))
    p = [_dot_nt(qb[h], kb[h]) for h in heads]
    cross = [_dot(qb[h], _mx(s0[h])) for h in heads]
    pb = [_mx(p[h] * m_ref[h]) for h in heads]
    o = [_dot(pb[h], vb[h]) + cross[h] * dq_ref[h] for h in heads]
    gn, rstd = [], []
    for h in heads:
        oc = o[h] - jnp.mean(o[h], axis=-1, keepdims=True)
        rstd.append(lax.rsqrt(jnp.mean(oc * oc, axis=-1, keepdims=True) + EPS))
        gn.append(oc * rstd[h])
    return qb, kb, vb, pb, kdb, gn, rstd


def _retention_fwd(ret, tables):
    cos2, sin2, mask, dq, dk, gt = tables
    s = ret.shape[0]
    t = mask.shape[1]
    nt = s // t
    hd = RET_HEAD_DIM

    def body(ret_ref, cos_ref, sin_ref, m_ref, dq_ref, dk_ref, gt_ref, y_ref, st_ref, s_scr):
        i = pl.program_id(0)

        @pl.when(i == 0)
        def _():
            s_scr[...] = jnp.zeros_like(s_scr)

        s0 = [s_scr[h] for h in range(RET_HEADS)]
        _, _, vb, _, kdb, gn, _ = _ret_heads_fwd(ret_ref, cos_ref[...], sin_ref[...], m_ref, dq_ref, dk_ref, s0)
        kv = [_dot_tn(kdb[h], vb[h]) for h in range(RET_HEADS)]
        for h in range(RET_HEADS):
            g = ret_ref[:, 1536 + h * hd:1536 + (h + 1) * hd]
            st_ref[h] = s0[h]
            y_ref[:, h * hd:(h + 1) * hd] = (gn[h] * (g * _sigmoid(g))).astype(y_ref.dtype)
            s_scr[h] = s0[h] * gt_ref[h] + kv[h]

    full3 = lambda a: pl.BlockSpec(a.shape, lambda i: (0, 0, 0))
    return pl.pallas_call(
        body, name="retention_fwd", grid=(nt,),
        in_specs=[pl.BlockSpec((t, 2048), lambda i: (i, 0)),
                  pl.BlockSpec((t, LANES), lambda i: (i, 0)),
                  pl.BlockSpec((t, LANES), lambda i: (i, 0)),
                  full3(mask), full3(dq), full3(dk), full3(gt)],
        out_specs=[pl.BlockSpec((t, 512), lambda i: (i, 0)),
                   pl.BlockSpec((None, RET_HEADS, hd, hd), lambda i: (i, 0, 0, 0))],
        out_shape=[jax.ShapeDtypeStruct((s, 512), MXU_DTYPE),
                   jax.ShapeDtypeStruct((nt, RET_HEADS, hd, hd), F32)],
        scratch_shapes=[pltpu.VMEM((RET_HEADS, hd, hd), F32)],
        compiler_params=_params(("arbitrary",)),
    )(ret, cos2, sin2, mask, dq, dk, gt)


def _retention_bwd(ret, states, dy, tables, dwo_parts):
    cos2, sin2, mask, dq, dk, gt = tables
    s = ret.shape[0]
    t = mask.shape[1]
    nt = s // t
    hd = RET_HEAD_DIM

    def body(ret_ref, st_ref, dy_ref, cos_ref, sin_ref, m_ref, dq_ref, dk_ref, gt_ref, dwo_ref,
             d_ref, rout_ref, ds_scr, send_sems, recv_sems, local_sem):
        i = pl.program_id(0)
        start, finish = _scatter_plan(dwo_ref, rout_ref, send_sems, recv_sems, local_sem)

        @pl.when(i == 0)
        def _():
            start()
            ds_scr[...] = jnp.zeros_like(ds_scr)

        cos = cos_ref[...]
        sin = sin_ref[...]
        heads = range(RET_HEADS)
        s0 = [st_ref[h] for h in heads]
        ds = [ds_scr[h] for h in heads]
        dsb = [_mx(ds[h]) for h in heads]
        qb, kb, vb, pb, kdb, gn, rstd = _ret_heads_fwd(ret_ref, cos, sin, m_ref, dq_ref, dk_ref, s0)
        dob, dodb = [], []
        for h in heads:
            g = ret_ref[:, 1536 + h * hd:1536 + (h + 1) * hd]
            dyv = dy_ref[:, h * hd:(h + 1) * hd]
            sg = _sigmoid(g)
            d_ref[:, 1536 + h * hd:1536 + (h + 1) * hd] = (
                dyv * gn[h] * (sg * (1.0 + g * (1.0 - sg)))).astype(d_ref.dtype)
            dgn = dyv * (g * sg)
            do = rstd[h] * (dgn - jnp.mean(dgn, axis=-1, keepdims=True)
                            - gn[h] * jnp.mean(dgn * gn[h], axis=-1, keepdims=True))
            dob.append(_mx(do))
            dodb.append(_mx(do * dq_ref[h]))
        dp = [_dot_nt(dob[h], vb[h]) for h in heads]
        dv = [_dot_tn(pb[h], dob[h]) + _dot(kdb[h], dsb[h]) for h in heads]
        dq_cross = [_dot_nt(dodb[h], _mx(s0[h])) for h in heads]
        dk_cross = [_dot_nt(vb[h], dsb[h]) for h in heads]
        ds_new = [_dot_tn(qb[h], dodb[h]) for h in heads]
        dpb = [_mx(dp[h] * m_ref[h]) for h in heads]
        dqr = [_dot(dpb[h], kb[h]) + dq_cross[h] for h in heads]
        dkr = [(_dot_tn(dpb[h], qb[h]) + dk_cross[h] * dk_ref[h]) * RET_KSCALE for h in heads]
        for h in heads:
            d_ref[:, 1024 + h * hd:1024 + (h + 1) * hd] = dv[h].astype(d_ref.dtype)
            d_ref[:, h * hd:(h + 1) * hd] = (dqr[h] * cos + _roll_half(dqr[h] * sin)).astype(d_ref.dtype)
            d_ref[:, 512 + h * hd:512 + (h + 1) * hd] = (
                dkr[h] * cos + _roll_half(dkr[h] * sin)).astype(d_ref.dtype)
            ds_scr[h] = ds[h] * gt_ref[h] + ds_new[h]
        pl.when(i == nt - 1)(finish)

    full3 = lambda a: pl.BlockSpec(a.shape, lambda i: (0, 0, 0))
    rev = lambda i: (nt - 1 - i, 0)
    return pl.pallas_call(
        body, name="retention_bwd", grid=(nt,),
        in_specs=[pl.BlockSpec((t, 2048), rev),
                  pl.BlockSpec((None, RET_HEADS, hd, hd), lambda i: (nt - 1 - i, 0, 0, 0)),
                  pl.BlockSpec((t, 512), rev),
                  pl.BlockSpec((t, LANES), rev),
                  pl.BlockSpec((t, LANES), rev),
                  full3(mask), full3(dq), full3(dk), full3(gt), HBM_SPEC],
        out_specs=[pl.BlockSpec((t, 2048), rev), HBM_SPEC],
        out_shape=[jax.ShapeDtypeStruct((s, 2048), MXU_DTYPE),
                   jax.ShapeDtypeStruct(dwo_parts.shape, dwo_parts.dtype)],
        scratch_shapes=[pltpu.VMEM((RET_HEADS, hd, hd), F32),
                        pltpu.SemaphoreType.DMA((N_PEERS,)), pltpu.SemaphoreType.DMA((N_PEERS,)),
                        pltpu.SemaphoreType.DMA((1,))],
        compiler_params=_params(("arbitrary",)),
    )(ret, states, dy, cos2, sin2, mask, dq, dk, gt, dwo_parts)


SB_BLOCK = 256
SB_CHUNK = 32


SB_SKIP = 104.0
NO_KEYS = -1e30


def _split_dots(xs, u):
    parts = []
    for x in xs:
        hi = lax.bitcast_convert_type(lax.bitcast_convert_type(x, jnp.uint32) & jnp.uint32(0xFFFF0000), F32)
        parts += [_mx(hi), _mx(x - hi)]
    out = _dot(jnp.concatenate(parts, axis=0), u)
    n = xs[0].shape[0]
    return [out[2 * k * n:(2 * k + 1) * n] + out[(2 * k + 1) * n:(2 * k + 2) * n] for k in range(len(xs))]


def _split_dot(x, u):
    return _split_dots([x], u)[0]


def _sb_pair_weights(lb, lk, allowed, u_gt):
    blk = lb.shape[0]
    lk_p, lk_d = lk[:, :blk], lk[:, blk:]
    r_d = _rowsum(lk_d)
    cs_p, cs_d = _split_dots([lk_p, lk_d], u_gt)
    a = jnp.exp(lb + jnp.concatenate([cs_p + r_d, cs_d], axis=1))
    return jnp.where(allowed, a, 0.0), r_d, r_d + _rowsum(lk_p)


def _sb_logits(q, k, causal):
    z = _dot_nt(q, k)
    l1p = jnp.log(1.0 + jnp.exp(-jnp.abs(z)))
    lk = -(jnp.maximum(z, 0.0) + l1p)
    if causal is not None:
        lk = jnp.where(causal, lk, 0.0)
    return jnp.minimum(z, 0.0) - l1p, lk


def _sb_weights(lb, lk, r, u_gt, causal):
    a = jnp.exp(lb + _split_dot(lk, u_gt) + r)
    return a if causal is None else jnp.where(causal, a, 0.0)


def _rowsum(x):
    return jnp.sum(x, axis=1, keepdims=True)


def _sb_pair_tile(i, blk):
    row = lax.broadcasted_iota(jnp.int32, (blk, 2 * blk), 0)
    col = lax.broadcasted_iota(jnp.int32, (blk, 2 * blk), 1)
    first_col = jnp.where(i >= 1, 0, blk)
    allowed = jnp.logical_and(row > col - blk, col >= first_col)
    rows_p = pl.ds(pl.multiple_of(jnp.maximum(i - 1, 0) * blk, blk), blk)
    rows_d = pl.ds(pl.multiple_of(i * blk, blk), blk)
    return allowed, rows_p, rows_d


def _sb_fwd(qkv, sg):
    s = qkv.shape[0]
    blk = min(SB_BLOCK, s)
    nq = s // blk
    hd = SB_HEAD_DIM

    ch = min(SB_CHUNK, blk)

    def body(q_ref, k_ref, v_ref, g_ref, y_ref, o_ref, z_scr, lhs_scr, cs_scr, a_scr):
        i = pl.program_id(1)
        row = lax.broadcasted_iota(jnp.int32, (blk, blk), 0)
        col = lax.broadcasted_iota(jnp.int32, (blk, blk), 1)
        u_gt = (row > col).astype(MXU_DTYPE)
        heads = [slice(hh * hd, (hh + 1) * hd) for hh in range(2)]
        qs = [q_ref[:, ls] for ls in heads]
        _, rows_p, rows_d = _sb_pair_tile(i, blk)
        has_prev = i >= 1
        crow = lax.broadcasted_iota(jnp.int32, (ch, blk), 0)
        ccol = lax.broadcasted_iota(jnp.int32, (ch, blk), 1)

        def logits(hh):
            kc = jnp.concatenate([k_ref[rows_p, heads[hh]], k_ref[rows_d, heads[hh]]], axis=0)
            z_scr[hh] = _dot_nt(qs[hh], kc)

        def keep_parts(hh):
            r_d, r_all = [], []
            for c in range(blk // ch):
                rows = pl.ds(c * ch, ch)
                causal = crow + c * ch > ccol
                z = z_scr[hh, rows, :]
                l1p = jnp.log(1.0 + jnp.exp(-jnp.abs(z)))
                lb = jnp.minimum(z, 0.0) - l1p
                z_scr[hh, rows, :] = lb
                lk = lb - z
                lk_p = lk[:, :blk]
                lk_d = jnp.where(causal, lk[:, blk:], 0.0)
                lhs_scr[hh, pl.ds(c * ch, ch), :] = _mx(lk_p)
                lhs_scr[hh, pl.ds(blk + c * ch, ch), :] = _mx(lk_d)
                r_d.append(_rowsum(lk_d))
                r_all.append(r_d[c] + _rowsum(lk_p))
            return r_d, jnp.concatenate(r_all, axis=0)

        def suffix_sums(hh):
            cs_scr[hh] = _dot(lhs_scr[hh], u_gt)

        def weights(hh, r_d):
            for c in range(blk // ch):
                rows = pl.ds(c * ch, ch)
                causal = crow + c * ch > ccol
                cs_p = cs_scr[hh, pl.ds(c * ch, ch), :] + jnp.where(has_prev, r_d[c], NO_KEYS)
                cs_d = cs_scr[hh, pl.ds(blk + c * ch, ch), :]
                lb = z_scr[hh, rows, :]
                a_p = jnp.exp(lb[:, :blk] + cs_p)
                a_d = jnp.where(causal, jnp.exp(lb[:, blk:] + cs_d), 0.0)
                a_scr[hh, rows, :] = _mx(jnp.concatenate([a_p, a_d], axis=1))

        def values(hh):
            vc = jnp.concatenate([v_ref[rows_p, heads[hh]], v_ref[rows_d, heads[hh]]], axis=0)
            return _dot(a_scr[hh], vc)

        def block(hh, j, r):
            start = pl.multiple_of(j * blk, blk)
            lb, lk = _sb_logits(qs[hh], k_ref[pl.ds(start, blk), heads[hh]], None)
            a = _sb_weights(lb, lk, r, u_gt, None)
            return _dot(_mx(a), v_ref[pl.ds(start, blk), heads[hh]]), r + _rowsum(lk)

        def more(n, r0, r1):
            return jnp.logical_and(n < i, jnp.max(jnp.maximum(r0, r1)) > -SB_SKIP)

        logits(0)
        logits(1)
        gv = g_ref[...]
        gates = gv * _sigmoid(gv)
        rd0, r0 = keep_parts(0)
        suffix_sums(0)
        rd1, r1 = keep_parts(1)
        suffix_sums(1)
        go = more(jnp.int32(1), r0, r1)
        weights(0, rd0)
        acc0 = values(0)
        weights(1, rd1)
        acc1 = values(1)

        def step(c):
            _, n, acc0, r0, acc1, r1 = c
            pv0, r0 = block(0, i - 1 - n, r0)
            pv1, r1 = block(1, i - 1 - n, r1)
            return more(n + 1, r0, r1), n + 1, acc0 + pv0, r0, acc1 + pv1, r1

        _, _, acc0, _, acc1, _ = lax.while_loop(lambda c: c[0], step, (go, jnp.int32(1), acc0, r0, acc1, r1))
        o = jnp.concatenate([acc0, acc1], axis=1)
        o_ref[...] = o
        y_ref[...] = (o * gates).astype(y_ref.dtype)

    qblk = pl.BlockSpec((blk, LANES), lambda p, i: (i, p))
    return pl.pallas_call(
        body, name="stickbreak_fwd", grid=(SB_HEADS // 2, nq),
        in_specs=[qblk,
                  pl.BlockSpec((s, LANES), lambda p, i: (0, 4 + p)),
                  pl.BlockSpec((s, LANES), lambda p, i: (0, 8 + p)),
                  qblk],
        out_specs=[qblk, qblk],
        out_shape=[jax.ShapeDtypeStruct((s, 512), MXU_DTYPE),
                   jax.ShapeDtypeStruct((s, 512), F32)],
        scratch_shapes=[pltpu.VMEM((2, blk, 2 * blk), F32),
                        pltpu.VMEM((2, 2 * blk, blk), MXU_DTYPE),
                        pltpu.VMEM((2, 2 * blk, blk), F32),
                        pltpu.VMEM((2, blk, 2 * blk), MXU_DTYPE)],
        compiler_params=_params(("arbitrary", "arbitrary")),
    )(qkv, qkv, qkv, sg)


def _sb_bwd(qkv, sg, o, dy):
    s = qkv.shape[0]
    blk = min(SB_BLOCK, s)
    nq = s // blk
    hd = SB_HEAD_DIM
    assert nq <= LANES
    ch = min(SB_CHUNK, blk)

    def body(q_ref, k_ref, v_ref, g_ref, o_ref, dy_ref, dq_ref, dk_ref, dv_ref, dg_ref, dk_scr, dv_scr,
             z_scr, g_scr, lhs_scr, cs_scr, a_scr, dz_scr):
        i = pl.program_id(1)

        @pl.when(i == 0)
        def _():
            dk_scr[...] = jnp.zeros_like(dk_scr)
            dv_scr[...] = jnp.zeros_like(dv_scr)

        row = lax.broadcasted_iota(jnp.int32, (blk, blk), 0)
        col = lax.broadcasted_iota(jnp.int32, (blk, blk), 1)
        lane = lax.broadcasted_iota(jnp.int32, (blk, LANES), 1)
        u_gt = (row > col).astype(MXU_DTYPE)
        u_lt = (row < col).astype(MXU_DTYPE)
        heads = [slice(hh * hd, (hh + 1) * hd) for hh in range(2)]
        qs = [q_ref[:, ls] for ls in heads]
        _, rows_p, rows_d = _sb_pair_tile(i, blk)
        has_prev = i >= 1
        crow = lax.broadcasted_iota(jnp.int32, (ch, blk), 0)
        ccol = lax.broadcasted_iota(jnp.int32, (ch, blk), 1)
        nch = blk // ch
        kcs = [jnp.concatenate([k_ref[rows_p, ls], k_ref[rows_d, ls]], axis=0) for ls in heads]
        dobs = []

        def gate_grads():
            g = g_ref[...]
            dyv = dy_ref[...]
            sgm = _sigmoid(g)
            dg_ref[...] = (dyv * o_ref[...] * (sgm * (1.0 + g * (1.0 - sgm)))).astype(dg_ref.dtype)
            dob = _mx(dyv * (g * sgm))
            dobs.extend(dob[:, ls] for ls in heads)

        def split_rows(hh, c, part, x):
            lhs_scr[hh, pl.ds(part * blk + c * ch, ch), :] = _mx(x)

        def summed_rows(hh, c, part):
            return cs_scr[hh, pl.ds(part * blk + c * ch, ch), :]

        def logits(hh):
            z_scr[hh] = _dot_nt(qs[hh], kcs[hh])

        def weight_grads(hh):
            vc = jnp.concatenate([v_ref[rows_p, heads[hh]], v_ref[rows_d, heads[hh]]], axis=0)
            g_scr[hh] = _dot_nt(dobs[hh], vc)

        def keep_parts(hh):
            r_d, r_all = [], []
            for c in range(nch):
                rows = pl.ds(c * ch, ch)
                z = z_scr[hh, rows, :]
                l1p = jnp.log(1.0 + jnp.exp(-jnp.abs(z)))
                lb = jnp.minimum(z, 0.0) - l1p
                z_scr[hh, rows, :] = lb
                lk = lb - z
                lk_p = lk[:, :blk]
                lk_d = jnp.where(crow + c * ch > ccol, lk[:, blk:], 0.0)
                split_rows(hh, c, 0, lk_p)
                split_rows(hh, c, 1, lk_d)
                r_d.append(_rowsum(lk_d))
                r_all.append(r_d[c] + _rowsum(lk_p))
            return r_d, jnp.concatenate(r_all, axis=0)

        def weights(hh, r_d):
            g_p = []
            for c in range(nch):
                rows = pl.ds(c * ch, ch)
                lb = z_scr[hh, rows, :]
                a_p = jnp.exp(lb[:, :blk] + (summed_rows(hh, c, 0) + jnp.where(has_prev, r_d[c], NO_KEYS)))
                a_d = jnp.where(crow + c * ch > ccol, jnp.exp(lb[:, blk:] + summed_rows(hh, c, 1)), 0.0)
                a = jnp.concatenate([a_p, a_d], axis=1)
                a_scr[hh, rows, :] = _mx(a)
                gm = g_scr[hh, rows, :] * a
                g_scr[hh, rows, :] = gm
                split_rows(hh, c, 0, gm[:, :blk])
                split_rows(hh, c, 1, gm[:, blk:])
                g_p.append(_rowsum(gm[:, :blk]))
            return g_p

        def logit_grads(hh, pg, g_p):
            for c in range(nch):
                rows = pl.ds(c * ch, ch)
                pre = jnp.concatenate([summed_rows(hh, c, 0) + pg[c * ch:(c + 1) * ch],
                                       summed_rows(hh, c, 1) + (pg[c * ch:(c + 1) * ch] + g_p[c])], axis=1)
                gm = g_scr[hh, rows, :]
                dz = gm - (gm + pre) * jnp.exp(z_scr[hh, rows, :])
                dz_p = dz[:, :blk]
                dz_d = jnp.where(crow + c * ch > ccol, dz[:, blk:], 0.0)
                dz_scr[hh, rows, :] = _mx(jnp.concatenate([dz_p, dz_d], axis=1))

        def products(hh, acc):
            ls = heads[hh]
            dk_scr[hh, rows_p, :] += _dot_tn(dz_scr[hh, :, 0:blk], qs[hh])
            dk_scr[hh, rows_d, :] += _dot_tn(dz_scr[hh, :, blk:2 * blk], qs[hh])
            dv_scr[hh, rows_p, :] += _dot_tn(a_scr[hh, :, 0:blk], dobs[hh])
            dv_scr[hh, rows_d, :] += _dot_tn(a_scr[hh, :, blk:2 * blk], dobs[hh])
            dq_ref[:, ls] = ((acc + _dot(dz_scr[hh], kcs[hh])) * SB_SCALE).astype(dq_ref.dtype)

        def suffix_sums(hh):
            cs_scr[hh] = _dot(lhs_scr[hh], u_gt)

        def prefix_sums(hh):
            cs_scr[hh] = _dot(lhs_scr[hh], u_lt)

        def more(n, r0, r1):
            return jnp.logical_and(n < i, jnp.max(jnp.maximum(r0, r1)) > -SB_SKIP)

        gate_grads()
        logits(0)
        weight_grads(0)
        logits(1)
        weight_grads(1)
        rd0, ra0 = keep_parts(0)
        suffix_sums(0)
        rd1, ra1 = keep_parts(1)
        suffix_sums(1)
        go = more(jnp.int32(1), ra0, ra1)
        gp0 = weights(0, rd0)
        prefix_sums(0)
        gp1 = weights(1, rd1)
        prefix_sums(1)

        def scan_block(hh, j, r, rmat):
            start = pl.multiple_of(j * blk, blk)
            _, lk = _sb_logits(qs[hh], k_ref[pl.ds(start, blk), heads[hh]], None)
            return r + _rowsum(lk), jnp.where(lane == j, r, rmat)

        def scan_step(c):
            _, n, r0, rmat0, r1, rmat1 = c
            r0, rmat0 = scan_block(0, i - 1 - n, r0, rmat0)
            r1, rmat1 = scan_block(1, i - 1 - n, r1, rmat1)
            return more(n + 1, r0, r1), n + 1, r0, rmat0, r1, rmat1

        zmat = jnp.zeros((blk, LANES), F32)
        _, n, _, rmat0, _, rmat1 = lax.while_loop(lambda c: c[0], scan_step,
                                                  (go, jnp.int32(1), ra0, zmat, ra1, zmat))
        rmats = (rmat0, rmat1)

        def block(hh, j, pg):
            ls = heads[hh]
            start = pl.multiple_of(j * blk, blk)
            k = k_ref[pl.ds(start, blk), ls]
            lb, lk = _sb_logits(qs[hh], k, None)
            r = _rowsum(jnp.where(lane == j, rmats[hh], 0.0))
            a = _sb_weights(lb, lk, r, u_gt, None)
            gm = _dot_nt(dobs[hh], v_ref[pl.ds(start, blk), ls]) * a
            dzb = _mx(gm - (gm + (pg + _split_dot(gm, u_lt))) * jnp.exp(lb))
            dk_scr[hh, pl.ds(start, blk), :] += _dot_tn(dzb, qs[hh])
            dv_scr[hh, pl.ds(start, blk), :] += _dot_tn(_mx(a), dobs[hh])
            return _dot(dzb, k), pg + _rowsum(gm)

        def step(t, c):
            acc0, pg0, acc1, pg1 = c
            dq0, pg0 = block(0, i - n + t, pg0)
            dq1, pg1 = block(1, i - n + t, pg1)
            return acc0 + dq0, pg0, acc1 + dq1, pg1

        zero = jnp.zeros((blk, 1), F32)
        zacc = jnp.zeros((blk, hd), F32)
        acc0, pg0, acc1, pg1 = lax.fori_loop(0, n - 1, step, (zacc, zero, zacc, zero))
        logit_grads(0, pg0, gp0)
        logit_grads(1, pg1, gp1)
        products(0, acc0)
        products(1, acc1)

        @pl.when(i == nq - 1)
        def _():
            for hh in range(2):
                ls = slice(hh * hd, (hh + 1) * hd)
                dk_ref[:, ls] = dk_scr[hh].astype(dk_ref.dtype)
                dv_ref[:, ls] = dv_scr[hh].astype(dv_ref.dtype)

    qblk = lambda c0: pl.BlockSpec((blk, LANES), lambda p, i: (i, c0 + p))
    full = lambda c0: pl.BlockSpec((s, LANES), lambda p, i: (0, c0 + p))
    half = jax.ShapeDtypeStruct((s, 512), MXU_DTYPE)
    return pl.pallas_call(
        body, name="stickbreak_bwd", grid=(SB_HEADS // 2, nq),
        in_specs=[qblk(0), full(4), full(8), qblk(0), qblk(0), qblk(4)],
        out_specs=[qblk(0), full(0), full(0), qblk(0)],
        out_shape=[half, half, half, half],
        scratch_shapes=[pltpu.VMEM((2, s, hd), F32), pltpu.VMEM((2, s, hd), F32),
                        pltpu.VMEM((2, blk, 2 * blk), F32),
                        pltpu.VMEM((2, blk, 2 * blk), F32),
                        pltpu.VMEM((2, 2 * blk, blk), MXU_DTYPE),
                        pltpu.VMEM((2, 2 * blk, blk), F32),
                        pltpu.VMEM((2, blk, 2 * blk), MXU_DTYPE),
                        pltpu.VMEM((2, blk, 2 * blk), MXU_DTYPE)],
        compiler_params=_params(("arbitrary", "arbitrary")),
    )(qkv, qkv, qkv, sg, o, dy)


def _layer_fwd(layer, x, mod, norm_g, w_in_g, w_out_g, tables, gather=(), loss_head=None):
    shift, scale1p, gate = mod[0:1], 1.0 + mod[1:2], mod[2:3]
    ret, qkv, sg, *gathered = _ln_proj(x, shift, scale1p, norm_g, w_in_g, layer, gather)
    y_r, states = _retention_fwd(ret, tables)
    y_s, o_s = _sb_fwd(qkv, sg)
    x_next = _out_proj(x, gate, y_r, y_s, w_out_g, layer, loss_head)
    saved = (x, shift, scale1p, gate, ret, qkv, sg, y_r, states, y_s, o_s)
    return x_next, saved, gathered


def _layer_bwd(layer, me, dx_out, saved, norm_g, w_in_g, w_out_g, tables, ride_along=None):
    x, shift, scale1p, gate, ret, qkv, sg, y_r, states, y_s, o_s = saved
    dy, dw_out, dgate = _out_proj_bwd(dx_out, gate, y_r, y_s, w_out_g, layer)
    dwo_parts = _mx(dw_out.reshape(N_DEV, D_MODEL // N_DEV, D_MODEL))
    d_ret, r_out = _retention_bwd(ret, states, dy, tables, dwo_parts)
    d_q, d_k, d_v, d_g = _sb_bwd(qkv, sg, o_s, dy)
    dproj = (d_ret, d_q, d_k, d_v, d_g)
    dx, dshift, dscale, dnorm_g, h_t = _in_proj_bwd_x(x, dx_out, dproj, shift, scale1p, norm_g, w_in_g, layer)
    dmod = jnp.concatenate([dshift, dscale, dgate], axis=1)
    gather = ride_along(dmod, dnorm_g) if ride_along else ()
    r_in, *gathered = _in_proj_bwd_w(me, h_t, dproj, gather)
    return dx, r_in, r_out, dmod, dnorm_g, gathered


MESH_IDS = pl.DeviceIdType.MESH
N_PEERS = N_DEV - 1
HBM_SPEC = pl.BlockSpec(memory_space=pl.ANY)


def _my_place():
    return lax.axis_index("x"), lax.axis_index("y"), lax.axis_index("c")


def _linear(px, py, pc):
    return 4 * px + 2 * py + pc


def _all_gather(blocks):
    n_arr = len(blocks)

    def body(*refs):
        start, forward, finish = _gather_plan(refs[:n_arr], refs[n_arr:2 * n_arr], *refs[2 * n_arr:])
        start()
        forward()
        finish()

    return pl.pallas_call(
        body, name="all_gather",
        out_shape=_gathered_shapes(blocks),
        in_specs=[HBM_SPEC] * n_arr, out_specs=[HBM_SPEC] * n_arr,
        scratch_shapes=_gather_sems(n_arr),
    )(*blocks)


def _gathered_shapes(blocks):
    return [jax.ShapeDtypeStruct((N_DEV * b.shape[0], b.shape[1]), b.dtype) for b in blocks]


def _gather_sems(n_arr):
    return [pltpu.SemaphoreType.DMA((n_arr * N_PEERS,)), pltpu.SemaphoreType.DMA((n_arr * N_PEERS,)),
            pltpu.SemaphoreType.DMA((n_arr,))]


def _gather_plan(x_refs, out_refs, send_sems, recv_sems, local_sems):
    n_arr = len(x_refs)
    x, y, c = _my_place()
    me, sibling = (x, y, c), (x, y, 1 - c)
    chips = [(1 - x, y), (x, 1 - y), (1 - x, 1 - y)]

    def rows(a, place):
        m = x_refs[a].shape[0]
        return out_refs[a].at[pl.ds(_linear(*place) * m, m), :]

    def copy(a, k, block, to, src=None):
        return pltpu.make_async_remote_copy(
            src_ref=rows(a, block) if src is None else src, dst_ref=rows(a, block),
            send_sem=send_sems.at[a * N_PEERS + k], recv_sem=recv_sems.at[a * N_PEERS + k],
            device_id=to, device_id_type=MESH_IDS)

    mine = [pltpu.make_async_copy(x_refs[a], rows(a, me), local_sems.at[a]) for a in range(n_arr)]
    first = []
    for a in range(n_arr):
        first.append(copy(a, 0, me, sibling, src=x_refs[a]))
        first += [copy(a, 1 + j, me, (*chip, c), src=x_refs[a]) for j, chip in enumerate(chips)]
    passed = [copy(a, 4 + j, (*chip, c), sibling) for j, chip in enumerate(chips) for a in range(n_arr)]

    def start():
        for cp in mine + first:
            cp.start()

    def forward():
        for j, chip in enumerate(chips):
            for a in range(n_arr):
                copy(a, 1 + j, (*chip, c), me).wait_recv()
                passed[j * n_arr + a].start()

    def finish():
        for a in range(n_arr):
            copy(a, 0, sibling, me).wait_recv()
            for j, chip in enumerate(chips):
                copy(a, 4 + j, (*chip, 1 - c), me).wait_recv()
        for cp in first + passed:
            cp.wait_send()
        for cp in mine:
            cp.wait()

    return start, forward, finish


def _ada_fwd(c_all, w_ada, b_cols):
    cols = w_ada.shape[2]

    def body(c_ref, w_ref, b_ref, ca_ref, mod_ref):
        cv = c_ref[...]
        ca = cv * _sigmoid(cv)
        ca_ref[...] = ca
        cb = _mx(jnp.concatenate([ca, ca], axis=0))
        for l in range(DEPTH):
            mod_ref[l * N_DEV:(l + 1) * N_DEV, :] = _dot(cb, _mx(w_ref[l]))[0:N_DEV] + b_ref[l]

    return pl.pallas_call(
        body, name="ada_fwd",
        out_shape=[jax.ShapeDtypeStruct((N_DEV, D_MODEL), F32),
                   jax.ShapeDtypeStruct((DEPTH * N_DEV, cols), F32)],
        compiler_params=_params(),
    )(c_all, w_ada, b_cols)


def _ada_bwd(c_act_t, dmod_cols):
    cols = dmod_cols.shape[2]

    def body(ca_ref, dm_ref, o_ref):
        ca = _mx(ca_ref[...]).astype(F32)
        for l in range(DEPTH):
            o_ref[l] = jnp.dot(ca, _mx(dm_ref[l]).astype(F32),
                               precision=lax.Precision.HIGHEST, preferred_element_type=F32)

    return pl.pallas_call(
        body, name="ada_bwd",
        out_shape=jax.ShapeDtypeStruct((DEPTH, D_MODEL, cols), F32),
        compiler_params=_params(),
    )(c_act_t, dmod_cols)


def _adamw_store(g, w_ref, m_ref, v_ref, g_ref, d_ref, mo_ref, vo_ref):
    m2 = ADAM_B1 * m_ref[...] + (1.0 - ADAM_B1) * g
    v2 = ADAM_B2 * v_ref[...] + (1.0 - ADAM_B2) * (g * g)
    m_hat = m2 / (1.0 - ADAM_B1 ** ADAM_STEP)
    v_hat = v2 / (1.0 - ADAM_B2 ** ADAM_STEP)
    g_ref[...] = g
    d_ref[...] = -ADAM_LR * (m_hat / (jnp.sqrt(v_hat) + ADAM_EPS) + ADAM_WD * w_ref[...])
    mo_ref[...] = m2
    vo_ref[...] = v2


def _slab_sum(p_ref):
    g = p_ref[0].astype(F32)
    for sl in range(1, p_ref.shape[0]):
        g = g + p_ref[sl].astype(F32)
    return g


def _sum_adamw_layers(parts, w, m, v):
    n_slab, rows, cols = parts[0].shape
    tr = min(256, rows)
    nt = rows // tr

    def body(p0_ref, p1_ref, w_ref, m_ref, v_ref, g_ref, d_ref, mo_ref, vo_ref):
        for l, p_ref in enumerate((p0_ref, p1_ref)):
            @pl.when(pl.program_id(0) == l)
            def _():
                _adamw_store(_slab_sum(p_ref), w_ref, m_ref, v_ref, g_ref, d_ref, mo_ref, vo_ref)

    p_specs = [pl.BlockSpec((n_slab, tr, cols), lambda l, i: (0, i * (1 - l) + (nt - 1) * l, 0)),
               pl.BlockSpec((n_slab, tr, cols), lambda l, i: (0, i * l, 0))]
    blk = pl.BlockSpec((None, tr, cols), lambda l, i: (l, i, 0))
    shp = jax.ShapeDtypeStruct((DEPTH, rows, cols), F32)
    return pl.pallas_call(
        body, name="sum_adamw_layers", grid=(DEPTH, nt),
        in_specs=p_specs + [blk, blk, blk],
        out_specs=[blk, blk, blk, blk],
        out_shape=[shp, shp, shp, shp],
        compiler_params=_params(("arbitrary", "arbitrary")),
    )(parts[0], parts[1], w, m, v)


def _sum_adamw(parts, w, m, v):
    n_slab, rows, cols = parts.shape
    tr = min(256, rows)

    def body(p_ref, w_ref, m_ref, v_ref, g_ref, d_ref, mo_ref, vo_ref):
        _adamw_store(_slab_sum(p_ref), w_ref, m_ref, v_ref, g_ref, d_ref, mo_ref, vo_ref)

    blk = pl.BlockSpec((tr, cols), lambda i: (i, 0))
    shp = jax.ShapeDtypeStruct((rows, cols), F32)
    return pl.pallas_call(
        body, name="sum_adamw", grid=(rows // tr,),
        in_specs=[pl.BlockSpec((n_slab, tr, cols), lambda i: (0, i, 0)), blk, blk, blk],
        out_specs=[blk, blk, blk, blk],
        out_shape=[shp, shp, shp, shp],
        compiler_params=_params(("arbitrary",)),
    )(parts, w, m, v)


SMALL_ROWS = 16


def kernel(x, c, norm_g, w_ada, b_ada, w_in, w_out, final_g, loss_target, m_norm_g, m_w_ada, m_b_ada, m_w_in, m_w_out, m_final_g, v_norm_g, v_w_ada, v_b_ada, v_w_in, v_w_out, v_final_g):
    me = _linear(*_my_place())
    in_cols = w_in.shape[2]
    out_rows = w_out.shape[1]
    ada_cols = w_ada.shape[2]

    w_in_m, w_out_m = _mx(w_in), _mx(w_out)
    g_in, g_out, g_c = _all_gather([w_in_m[0], w_out_m[0], jnp.broadcast_to(c, (8, D_MODEL))])
    w_in_g = [g_in.reshape(N_DEV, 1, D_MODEL, in_cols), None]
    w_out_g = [g_out.reshape(N_DEV, 1, out_rows, D_MODEL), None]
    c_all = g_c.reshape(N_DEV, 8, D_MODEL)[:, 0]

    b_cols = lax.dynamic_slice_in_dim(b_ada, me * ada_cols, ada_cols, axis=1)[:, None, :]
    c_act, mod_cols = _ada_fwd(c_all, w_ada, b_cols)
    (g_mod,) = _all_gather([mod_cols])
    g_mod = g_mod.reshape(N_DEV, DEPTH, N_DEV, ada_cols)
    mod = lax.dynamic_index_in_dim(g_mod, me, axis=2, keepdims=False)
    mod = mod.transpose(1, 0, 2).reshape(DEPTH, 3, D_MODEL)

    tables = _ret_tables(x.shape[1])
    h = x[0]
    saved = []
    for l in range(DEPTH):
        nxt = (w_in_m[l + 1], w_out_m[l + 1]) if l + 1 < DEPTH else ()
        head = None if nxt else (final_g[None], loss_target[0])
        h, sv, gathered = _layer_fwd(0, h, mod[l], norm_g[l:l + 1], w_in_g[l], w_out_g[l], tables, nxt, head)
        if nxt:
            w_in_g[l + 1] = gathered[0].reshape(N_DEV, 1, D_MODEL, in_cols)
            w_out_g[l + 1] = gathered[1].reshape(N_DEV, 1, out_rows, D_MODEL)
        saved.append(sv)
    dx, loss_part, dfg = h
    r_in, r_out, small = [None] * DEPTH, [None] * DEPTH, [None] * DEPTH

    def small_block(dmod0, dng0):
        pad = jnp.zeros((SMALL_ROWS - 10, D_MODEL), F32)
        return (jnp.concatenate([dmod0.reshape(3, D_MODEL), small[1][0], dng0, small[1][1], dfg,
                                 jnp.broadcast_to(loss_part, (1, D_MODEL)), pad], axis=0),)

    for l in reversed(range(DEPTH)):
        dx, r_in[l], r_out[l], dmod, dng, gathered = _layer_bwd(
            0, me, dx, saved[l], norm_g[l:l + 1], w_in_g[l], w_out_g[l], tables, small_block if l == 0 else None)
        small[l] = (dmod.reshape(3, D_MODEL), dng)
    g_small = gathered[0].reshape(N_DEV, SMALL_ROWS, D_MODEL)

    def small_pack(b, n, f, fill):
        return jnp.concatenate([b.reshape(6, D_MODEL), n, f[None],
                                jnp.full((SMALL_ROWS - 9, D_MODEL), fill, F32)], axis=0)

    s_g, s_d, s_m, s_v = _sum_adamw(g_small, small_pack(b_ada, norm_g, final_g, 0.0),
                                    small_pack(m_b_ada, m_norm_g, m_final_g, 0.0),
                                    small_pack(v_b_ada, v_norm_g, v_final_g, 1.0))
    loss = s_g[9, 0]

    def small_unpack(a):
        return a[0:6].reshape(DEPTH, 3 * D_MODEL), a[6:8], a[8]

    dmod_all = g_small[:, 0:6].reshape(N_DEV, DEPTH, 3 * D_MODEL).transpose(1, 0, 2)
    dmod_cols = lax.dynamic_slice_in_dim(dmod_all, me * ada_cols, ada_cols, axis=2)
    g_ada = _ada_bwd(c_act.T, dmod_cols).reshape(1, DEPTH * D_MODEL, ada_cols)
    ada = _sum_adamw(g_ada, *[a.reshape(DEPTH * D_MODEL, ada_cols) for a in (w_ada, m_w_ada, v_w_ada)])
    ada = [a.reshape(DEPTH, D_MODEL, ada_cols) for a in ada]

    win = _sum_adamw_layers(r_in, w_in, m_w_in, v_w_in)
    wout = _sum_adamw_layers(r_out, w_out, m_w_out, v_w_out)

    outs = [loss, dx[None]]
    for k in range(4):
        b, n, f = small_unpack((s_g, s_d, s_m, s_v)[k])
        outs += [n, ada[k], b, win[k], wout[k], f]
    return tuple(outs)
```

```python
import functools

import jax
import jax.numpy as jnp
from jax import lax
from jax.experimental import pallas as pl
from jax.experimental.pallas import tpu as pltpu

F32 = jnp.float32
MXU_DTYPE = jnp.bfloat16

D_MODEL = 1024
DEPTH = 2
N_DEV = 8
CHUNK = 64
D_RET = 512
D_SB = 512
RET_HEADS = 4
RET_HEAD_DIM = 128
SB_HEADS = 8
SB_HEAD_DIM = 64
D_IN = 4096
ROPE_BASE = 10000.0
EPS = 1e-6
SB_SCALE = SB_HEAD_DIM ** -0.5
RET_KSCALE = RET_HEAD_DIM ** -0.5

ADAM_LR = 0.001
ADAM_B1 = 0.9
ADAM_B2 = 0.999
ADAM_EPS = 1e-08
ADAM_WD = 0.01
ADAM_STEP = 10

V7X_VMEM_BYTES = 64 * 2 ** 20
VMEM_LIMIT = V7X_VMEM_BYTES - 8 * 2 ** 20
LANES = 128

_NT = (((1,), (1,)), ((), ()))
_TN = (((0,), (0,)), ((), ()))


def _dot(a, b):
    return jnp.dot(a, b, preferred_element_type=F32)


def _dot_nt(a, b):
    return lax.dot_general(a, b, _NT, preferred_element_type=F32)


def _dot_tn(a, b):
    return lax.dot_general(a, b, _TN, preferred_element_type=F32)


def _mx(x):
    return x.astype(MXU_DTYPE)


def _sigmoid(x):
    return 1.0 / (1.0 + jnp.exp(-x))


def _params(sem=None):
    return pltpu.CompilerParams(dimension_semantics=sem, vmem_limit_bytes=VMEM_LIMIT)


def _row_tile(s):
    return min(512, s)


def _w_in_spec(w_in_g, layer):
    return pl.BlockSpec((N_DEV, None) + w_in_g.shape[2:], lambda i: (0, layer, 0, 0))


def _w_out_spec(w_out_g, layer):
    return pl.BlockSpec((N_DEV, None) + w_out_g.shape[2:], lambda i: (0, layer, 0, 0))


def _ln_proj(x, shift, scale1p, g, w_in_g, layer, gather=()):
    s = x.shape[0]
    ts = _row_tile(s)
    ns = s // ts
    n_g = len(gather)

    def body(x_ref, sh_ref, sc_ref, g_ref, w_ref, *rest):
        ret_ref, qkv_ref, sg_ref = rest[n_g:n_g + 3]
        if n_g:
            start, forward, finish = _gather_plan(rest[:n_g], rest[n_g + 3:2 * n_g + 3], *rest[2 * n_g + 3:])
            i = pl.program_id(0)
            pl.when(i == 0)(start)
            pl.when(i == max(ns - 4, 0))(forward)
        xv = x_ref[...]
        rstd = lax.rsqrt(jnp.mean(xv * xv, axis=-1, keepdims=True) + EPS)
        h = (xv * rstd * g_ref[...]) * sc_ref[...] + sh_ref[...]
        hb = _mx(h)
        for n in range(4):
            ret_ref[:, n * 512:(n + 1) * 512] = _dot(hb, w_ref[n])
        qkv_ref[:, 0:512] = _mx(_dot(hb, w_ref[4]) * SB_SCALE)
        qkv_ref[:, 512:1024] = _mx(_dot(hb, w_ref[5]))
        qkv_ref[:, 1024:1536] = _mx(_dot(hb, w_ref[6]))
        sg_ref[...] = _dot(hb, w_ref[7])
        if n_g:
            pl.when(i == ns - 1)(finish)

    vec = pl.BlockSpec((1, D_MODEL), lambda i: (0, 0))
    return pl.pallas_call(
        body, name="ln_proj_gather" if n_g else "ln_proj", grid=(ns,),
        in_specs=[pl.BlockSpec((ts, D_MODEL), lambda i: (i, 0)), vec, vec, vec,
                  _w_in_spec(w_in_g, layer)] + [HBM_SPEC] * n_g,
        out_specs=[pl.BlockSpec((ts, 2048), lambda i: (i, 0)),
                   pl.BlockSpec((ts, 1536), lambda i: (i, 0)),
                   pl.BlockSpec((ts, 512), lambda i: (i, 0))] + [HBM_SPEC] * n_g,
        out_shape=[jax.ShapeDtypeStruct((s, 2048), F32),
                   jax.ShapeDtypeStruct((s, 1536), MXU_DTYPE),
                   jax.ShapeDtypeStruct((s, 512), F32)] + _gathered_shapes(gather),
        scratch_shapes=_gather_sems(n_g) if n_g else (),
        compiler_params=_params(("arbitrary",)),
    )(x, shift, scale1p, g, w_in_g, *gather)


def _w_out_halves(w_ref):
    half = N_DEV // 2
    return (w_ref[0:half].reshape(D_RET, D_MODEL), w_ref[half:N_DEV].reshape(D_SB, D_MODEL))


def _out_proj(x, gate, y_r, y_s, w_out_g, layer, loss_head=None):
    s = x.shape[0]
    ts = min(2 * _row_tile(s), s)

    def layer_out(x_ref, gate_ref, yr_ref, ys_ref, w_ref):
        w_r, w_s = _w_out_halves(w_ref)
        return x_ref[...] + gate_ref[...] * (_dot(yr_ref[...], w_r) + _dot(ys_ref[...], w_s))

    def body(x_ref, gate_ref, yr_ref, ys_ref, w_ref, o_ref):
        o_ref[...] = layer_out(x_ref, gate_ref, yr_ref, ys_ref, w_ref)

    def body_loss(x_ref, gate_ref, yr_ref, ys_ref, w_ref, fg_ref, t_ref, dx_ref, loss_ref, dfg_ref):
        @pl.when(pl.program_id(0) == 0)
        def _():
            loss_ref[...] = jnp.zeros_like(loss_ref)
            dfg_ref[...] = jnp.zeros_like(dfg_ref)

        xv = layer_out(x_ref, gate_ref, yr_ref, ys_ref, w_ref)
        fgv = fg_ref[...]
        rstd = lax.rsqrt(jnp.mean(xv * xv, axis=-1, keepdims=True) + EPS)
        xn = xv * rstd
        err = xn * fgv - t_ref[...]
        tok = jnp.mean(err * err, axis=-1, keepdims=True)
        loss_ref[...] += 0.5 * jnp.sum(tok, axis=0, keepdims=True)
        dy = err * (1.0 / D_MODEL)
        dfg_ref[...] += jnp.sum(dy * xn, axis=0, keepdims=True)
        dxn = dy * fgv
        dx_ref[...] = rstd * (dxn - xn * jnp.mean(dxn * xn, axis=-1, keepdims=True))

    rows = pl.BlockSpec((ts, D_MODEL), lambda i: (i, 0))
    vec = pl.BlockSpec((1, D_MODEL), lambda i: (0, 0))
    in_specs = [rows, vec, pl.BlockSpec((ts, 512), lambda i: (i, 0)), pl.BlockSpec((ts, 512), lambda i: (i, 0)),
                _w_out_spec(w_out_g, layer)]
    if loss_head is None:
        return pl.pallas_call(
            body, name="out_proj", grid=(s // ts,), in_specs=in_specs, out_specs=rows,
            out_shape=jax.ShapeDtypeStruct((s, D_MODEL), F32),
            compiler_params=_params(("arbitrary",)),
        )(x, gate, y_r, y_s, w_out_g)
    return pl.pallas_call(
        body_loss, name="out_proj_loss", grid=(s // ts,), in_specs=in_specs + [vec, rows],
        out_specs=[rows, pl.BlockSpec((1, 1), lambda i: (0, 0)), vec],
        out_shape=[jax.ShapeDtypeStruct((s, D_MODEL), F32),
                   jax.ShapeDtypeStruct((1, 1), F32),
                   jax.ShapeDtypeStruct((1, D_MODEL), F32)],
        compiler_params=_params(("arbitrary",)),
    )(x, gate, y_r, y_s, w_out_g, *loss_head)


def _out_proj_bwd(dx_out, gate, y_r, y_s, w_out_g, layer):
    s = dx_out.shape[0]
    ts = min(2 * _row_tile(s), s)
    ns = s // ts

    def body(dx_ref, gate_ref, yr_ref, ys_ref, w_ref, dy_ref, dw_ref, dgate_ref):
        i = pl.program_id(0)

        @pl.when(i == 0)
        def _():
            dw_ref[...] = jnp.zeros_like(dw_ref)

        dxv = dx_ref[...]
        dt = _mx(dxv * gate_ref[...])
        dxb = _mx(dxv)
        w_r, w_s = _w_out_halves(w_ref)
        dy_ref[:, 0:512] = _dot_nt(dt, w_r)
        dy_ref[:, 512:1024] = _dot_nt(dt, w_s)
        dw_ref[0:512, :] += _dot_tn(yr_ref[...], dxb)
        dw_ref[512:1024, :] += _dot_tn(ys_ref[...], dxb)

        @pl.when(i == ns - 1)
        def _():
            m_r, m_s = dw_ref[0:512, :], dw_ref[512:1024, :]
            dgate_ref[...] = (jnp.sum(w_r.astype(F32) * m_r, axis=0, keepdims=True)
                              + jnp.sum(w_s.astype(F32) * m_s, axis=0, keepdims=True))
            dw_ref[...] = dw_ref[...] * gate_ref[...]

    return pl.pallas_call(
        body, name="out_proj_bwd", grid=(ns,),
        in_specs=[pl.BlockSpec((ts, D_MODEL), lambda i: (i, 0)),
                  pl.BlockSpec((1, D_MODEL), lambda i: (0, 0)),
                  pl.BlockSpec((ts, 512), lambda i: (i, 0)),
                  pl.BlockSpec((ts, 512), lambda i: (i, 0)),
                  _w_out_spec(w_out_g, layer)],
        out_specs=[pl.BlockSpec((ts, D_MODEL), lambda i: (i, 0)),
                   pl.BlockSpec((D_MODEL, D_MODEL), lambda i: (0, 0)),
                   pl.BlockSpec((1, D_MODEL), lambda i: (0, 0))],
        out_shape=[jax.ShapeDtypeStruct((s, D_MODEL), F32),
                   jax.ShapeDtypeStruct((D_MODEL, D_MODEL), F32),
                   jax.ShapeDtypeStruct((1, D_MODEL), F32)],
        compiler_params=_params(("arbitrary",)),
    )(dx_out, gate, y_r, y_s, w_out_g)


def _scatter_plan(parts_ref, recv_ref, send_sems, recv_sems, local_sem):
    px, py, pc = _my_place()
    mine = _linear(px, py, pc)

    def copy(r):
        peer = (1 - px if r & 4 else px, 1 - py if r & 2 else py, 1 - pc if r & 1 else pc)
        return pltpu.make_async_remote_copy(
            src_ref=parts_ref.at[_linear(*peer)], dst_ref=recv_ref.at[mine],
            send_sem=send_sems.at[r - 1], recv_sem=recv_sems.at[r - 1],
            device_id=peer, device_id_type=MESH_IDS)

    own = pltpu.make_async_copy(parts_ref.at[mine], recv_ref.at[mine], local_sem.at[0])

    def start():
        own.start()
        for r in range(1, N_DEV):
            copy(r).start()

    def finish():
        for r in range(1, N_DEV):
            copy(r).wait_recv()
            copy(r).wait_send()
        own.wait()

    return start, finish


def _in_proj_bwd_x(x, dx_out, dproj, shift, scale1p, g, w_in_g, layer):
    s = x.shape[0]
    ts = _row_tile(s)
    ns = s // ts
    nb = D_IN // N_DEV

    def body(x_ref, dxo_ref, dr_ref, d4_ref, d5_ref, d6_ref, d7_ref, sh_ref, sc_ref, g_ref, w_ref,
             dx_ref, dsh_ref, dsc_ref, dg_ref, ht_ref):
        i = pl.program_id(0)

        @pl.when(i == 0)
        def _():
            dsh_ref[...] = jnp.zeros_like(dsh_ref)
            dsc_ref[...] = jnp.zeros_like(dsc_ref)
            dg_ref[...] = jnp.zeros_like(dg_ref)

        dh = _dot_nt(dr_ref[:, 0:nb], w_ref[0])
        for n in range(1, 4):
            dh += _dot_nt(dr_ref[:, n * nb:(n + 1) * nb], w_ref[n])
        for n, d_ref in zip(range(4, N_DEV), (d4_ref, d5_ref, d6_ref, d7_ref)):
            dh += _dot_nt(d_ref[...], w_ref[n])
        xv = x_ref[...]
        gv = g_ref[...]
        scv = sc_ref[...]
        rstd = lax.rsqrt(jnp.mean(xv * xv, axis=-1, keepdims=True) + EPS)
        xn = xv * rstd
        xg = xn * gv
        ht_ref[...] = _mx((xg * scv + sh_ref[...]).T)
        dsh_ref[...] += jnp.sum(dh, axis=0, keepdims=True)
        dsc_ref[...] += jnp.sum(dh * xg, axis=0, keepdims=True)
        dhs = dh * scv
        dg_ref[...] += jnp.sum(dhs * xn, axis=0, keepdims=True)
        dxn = dhs * gv
        dx_ref[...] = rstd * (dxn - xn * jnp.mean(dxn * xn, axis=-1, keepdims=True)) + dxo_ref[...]

    vec = pl.BlockSpec((1, D_MODEL), lambda i: (0, 0))
    return pl.pallas_call(
        body, name="in_proj_bwd_x", grid=(ns,),
        in_specs=[pl.BlockSpec((ts, D_MODEL), lambda i: (i, 0)),
                  pl.BlockSpec((ts, D_MODEL), lambda i: (i, 0)),
                  pl.BlockSpec((ts, 4 * nb), lambda i: (i, 0))]
                 + [pl.BlockSpec((ts, nb), lambda i: (i, 0))] * 4
                 + [vec, vec, vec, _w_in_spec(w_in_g, layer)],
        out_specs=[pl.BlockSpec((ts, D_MODEL), lambda i: (i, 0)), vec, vec, vec,
                   pl.BlockSpec((D_MODEL, ts), lambda i: (0, i))],
        out_shape=[jax.ShapeDtypeStruct((s, D_MODEL), F32),
                   jax.ShapeDtypeStruct((1, D_MODEL), F32),
                   jax.ShapeDtypeStruct((1, D_MODEL), F32),
                   jax.ShapeDtypeStruct((1, D_MODEL), F32),
                   jax.ShapeDtypeStruct((D_MODEL, s), MXU_DTYPE)],
        compiler_params=_params(("arbitrary",)),
    )(x, dx_out, *dproj, shift, scale1p, g, w_in_g)


def _in_proj_bwd_w(me, h_t, dproj, gather=()):
    s = h_t.shape[1]
    ts = min(4 * _row_tile(s), s)
    ns = s // ts
    nb = D_IN // N_DEV
    n_chip = N_DEV // 2
    n_g = len(gather)

    def flip_bits(j):
        return jnp.where(j == 0, 4, jnp.where(j == 1, 2, jnp.where(j == 2, 6, 0)))

    def slab_of(t, me_ref):
        return jnp.bitwise_xor(me_ref[0], flip_bits(t // 2) + 1 - t % 2)

    def body(me_ref, ht_ref, dr_ref, d4_ref, d5_ref, d6_ref, d7_ref, *rest):
        rin_ref = rest[n_g]
        (acc, stage, pre_buf, pre_send, pre_recv, sum_send, sum_recv, local_sem) = rest[2 * n_g + 1:2 * n_g + 9]
        t = pl.program_id(0)
        i = pl.program_id(1)
        if n_g:
            g_start, g_forward, g_finish = _gather_plan(rest[:n_g], rest[n_g + 1:2 * n_g + 1], *rest[2 * n_g + 9:])
            pl.when(jnp.logical_and(t == 0, i == 0))(g_start)
            pl.when(jnp.logical_and(t == N_DEV // 2, i == 0))(g_forward)
        j = t // 2
        summing = t % 2 == 1
        slab = slab_of(t, me_ref)
        px, py, pc = _my_place()

        def pre_copy(jj):
            return pltpu.make_async_remote_copy(
                src_ref=stage.at[jj % 2], dst_ref=pre_buf.at[jj],
                send_sem=pre_send.at[jj], recv_sem=pre_recv.at[jj],
                device_id=(px, py, 1 - pc), device_id_type=MESH_IDS)

        def sum_copy(jj):
            fx = jnp.logical_or(jj == 0, jj == 2)
            fy = jnp.logical_or(jj == 1, jj == 2)
            return pltpu.make_async_remote_copy(
                src_ref=stage.at[2 + jj % 2], dst_ref=rin_ref.at[jj],
                send_sem=sum_send.at[jj], recv_sem=sum_recv.at[jj],
                device_id=(jnp.where(fx, 1 - px, px), jnp.where(fy, 1 - py, py), pc), device_id_type=MESH_IDS)

        own = pltpu.make_async_copy(stage.at[3], rin_ref.at[n_chip - 1], local_sem.at[0])

        @pl.when(i == 0)
        def _():
            acc[...] = jnp.zeros_like(acc)

        @pl.when(slab < 4)
        def _():
            acc[...] += _dot(ht_ref[...], dr_ref[...])

        for n, d_ref in zip(range(4, N_DEV), (d4_ref, d5_ref, d6_ref, d7_ref)):
            @pl.when(slab == n)
            def _():
                acc[...] += _dot(ht_ref[...], d_ref[...])

        @pl.when(jnp.logical_and(i == ns - 1, jnp.logical_not(summing)))
        def _():
            @pl.when(j >= 2)
            def _():
                pre_copy(j - 2).wait_send()

            stage[j % 2] = acc[...].astype(stage.dtype)
            pre_copy(j).start()

        @pl.when(jnp.logical_and(i == ns - 1, summing))
        def _():
            pre_copy(j).wait_recv()

            @pl.when(j >= 2)
            def _():
                sum_copy(j - 2).wait_send()

            stage[2 + j % 2] = (acc[...] + pre_buf[j].astype(F32)).astype(stage.dtype)

            @pl.when(j < n_chip - 1)
            def _():
                sum_copy(j).start()

            @pl.when(j == n_chip - 1)
            def _():
                own.start()
                pre_copy(n_chip - 2).wait_send()
                pre_copy(n_chip - 1).wait_send()
                sum_copy(n_chip - 2).wait_send()
                for jj in range(n_chip - 1):
                    sum_copy(jj).wait_recv()
                own.wait()
                if n_g:
                    g_finish()

    def part_rows(n, t, i, me_ref):
        return jnp.where(slab_of(t, me_ref) == n, i, ns - 1), 0

    return pl.pallas_call(
        body, name="in_proj_bwd_w",
        grid_spec=pltpu.PrefetchScalarGridSpec(
            num_scalar_prefetch=1, grid=(N_DEV, ns),
            in_specs=[pl.BlockSpec((D_MODEL, ts), lambda t, i, me_ref: (0, i)),
                      pl.BlockSpec((ts, nb), lambda t, i, me_ref: (
                          jnp.where(slab_of(t, me_ref) < 4, i, ns - 1), jnp.minimum(slab_of(t, me_ref), 3)))]
                     + [pl.BlockSpec((ts, nb), functools.partial(part_rows, n)) for n in range(4, N_DEV)]
                     + [HBM_SPEC] * n_g,
            out_specs=[HBM_SPEC] * (1 + n_g),
            scratch_shapes=[pltpu.VMEM((D_MODEL, nb), F32),
                            pltpu.VMEM((4, D_MODEL, nb), MXU_DTYPE),
                            pltpu.VMEM((n_chip, D_MODEL, nb), MXU_DTYPE),
                            pltpu.SemaphoreType.DMA((n_chip,)), pltpu.SemaphoreType.DMA((n_chip,)),
                            pltpu.SemaphoreType.DMA((n_chip - 1,)), pltpu.SemaphoreType.DMA((n_chip - 1,)),
                            pltpu.SemaphoreType.DMA((1,))] + (_gather_sems(n_g) if n_g else [])),
        out_shape=[jax.ShapeDtypeStruct((n_chip, D_MODEL, nb), MXU_DTYPE)] + _gathered_shapes(gather),
        compiler_params=_params(("arbitrary", "arbitrary")),
    )(jnp.reshape(me, (1,)).astype(jnp.int32), h_t, *dproj, *gather)


RET_TILE = 256


def _ret_tables(s):
    t = min(RET_TILE, s)
    half = RET_HEAD_DIM // 2
    pos = jnp.arange(s, dtype=F32)
    inv = ROPE_BASE ** (-jnp.arange(half, dtype=F32) / half)
    ang = pos[:, None] * jnp.concatenate([inv, inv])[None, :]
    cos2 = jnp.cos(ang)
    sin2 = jnp.sin(ang) * jnp.concatenate([-jnp.ones((half,), F32), jnp.ones((half,), F32)])[None, :]
    lg = jnp.log1p(-(2.0 ** (-5.0 - jnp.arange(RET_HEADS, dtype=F32))))[:, None, None]
    n = jnp.arange(t)
    dist = (n[:, None] - n[None, :]).astype(F32)[None]
    cn = (n // CHUNK)[:, None]
    cm = (n // CHUNK)[None, :]
    mask = jnp.where((cn == cm)[None], jnp.exp(jnp.abs(dist) * lg),
                     jnp.where((cm < cn)[None], jnp.exp(dist * lg), 0.0))
    nf = n.astype(F32)[None, :, None]
    dq = jnp.broadcast_to(jnp.exp((nf + 1.0) * lg), (RET_HEADS, t, LANES))
    dk = jnp.broadcast_to(jnp.exp((t - 1.0 - nf) * lg), (RET_HEADS, t, LANES))
    gt = jnp.broadcast_to(jnp.exp(float(t) * lg), (RET_HEADS, 1, LANES))
    return cos2, sin2, mask, dq, dk, gt


def _roll_half(x):
    return pltpu.roll(x, RET_HEAD_DIM // 2, 1)


def _ret_heads_fwd(ret_ref, cos, sin, m_ref, dq_ref, dk_ref, s0):
    hd = RET_HEAD_DIM
    heads = range(RET_HEADS)
    qb, kb, vb, kdb = [], [], [], []
    for h in heads:
        q = ret_ref[:, h * hd:(h + 1) * hd]
        k = ret_ref[:, 512 + h * hd:512 + (h + 1) * hd]
        kr = (k * cos + _roll_half(k) * sin) * RET_KSCALE
        qb.append(_mx(q * cos + _roll_half(q) * sin))
        kb.append(_mx(kr))
        kdb.append(_mx(kr * dk_ref[h]))
        vb.append(_mx(ret_ref[:, 1024 + h * hd:1024 + (h + 1) * hd]))
    p = [_dot_nt(qb[h], kb[h]) for h in heads]
    cross = [_dot(qb[h], _mx(s0[h])) for h in heads]
    pb = [_mx(p[h] * m_ref[h]) for h in heads]
    o = [_dot(pb[h], vb[h]) + cross[h] * dq_ref[h] for h in heads]
    gn, rstd = [], []
    for h in heads:
        oc = o[h] - jnp.mean(o[h], axis=-1, keepdims=True)
        rstd.append(lax.rsqrt(jnp.mean(oc * oc, axis=-1, keepdims=True) + EPS))
        gn.append(oc * rstd[h])
    return qb, kb, vb, pb, kdb, gn, rstd


def _retention_fwd(ret, tables):
    cos2, sin2, mask, dq, dk, gt = tables
    s = ret.shape[0]
    t = mask.shape[1]
    nt = s // t
    hd = RET_HEAD_DIM

    def body(ret_ref, cos_ref, sin_ref, m_ref, dq_ref, dk_ref, gt_ref, y_ref, st_ref, s_scr):
        i = pl.program_id(0)

        @pl.when(i == 0)
        def _():
            s_scr[...] = jnp.zeros_like(s_scr)

        s0 = [s_scr[h] for h in range(RET_HEADS)]
        _, _, vb, _, kdb, gn, _ = _ret_heads_fwd(ret_ref, cos_ref[...], sin_ref[...], m_ref, dq_ref, dk_ref, s0)
        kv = [_dot_tn(kdb[h], vb[h]) for h in range(RET_HEADS)]
        for h in range(RET_HEADS):
            g = ret_ref[:, 1536 + h * hd:1536 + (h + 1) * hd]
            st_ref[h] = s0[h]
            y_ref[:, h * hd:(h + 1) * hd] = (gn[h] * (g * _sigmoid(g))).astype(y_ref.dtype)
            s_scr[h] = s0[h] * gt_ref[h] + kv[h]

    full3 = lambda a: pl.BlockSpec(a.shape, lambda i: (0, 0, 0))
    return pl.pallas_call(
        body, name="retention_fwd", grid=(nt,),
        in_specs=[pl.BlockSpec((t, 2048), lambda i: (i, 0)),
                  pl.BlockSpec((t, LANES), lambda i: (i, 0)),
                  pl.BlockSpec((t, LANES), lambda i: (i, 0)),
                  full3(mask), full3(dq), full3(dk), full3(gt)],
        out_specs=[pl.BlockSpec((t, 512), lambda i: (i, 0)),
                   pl.BlockSpec((None, RET_HEADS, hd, hd), lambda i: (i, 0, 0, 0))],
        out_shape=[jax.ShapeDtypeStruct((s, 512), MXU_DTYPE),
                   jax.ShapeDtypeStruct((nt, RET_HEADS, hd, hd), F32)],
        scratch_shapes=[pltpu.VMEM((RET_HEADS, hd, hd), F32)],
        compiler_params=_params(("arbitrary",)),
    )(ret, cos2, sin2, mask, dq, dk, gt)


def _retention_bwd(ret, states, dy, tables, dwo_parts):
    cos2, sin2, mask, dq, dk, gt = tables
    s = ret.shape[0]
    t = mask.shape[1]
    nt = s // t
    hd = RET_HEAD_DIM

    def body(ret_ref, st_ref, dy_ref, cos_ref, sin_ref, m_ref, dq_ref, dk_ref, gt_ref, dwo_ref,
             d_ref, rout_ref, ds_scr, send_sems, recv_sems, local_sem):
        i = pl.program_id(0)
        start, finish = _scatter_plan(dwo_ref, rout_ref, send_sems, recv_sems, local_sem)

        @pl.when(i == 0)
        def _():
            start()
            ds_scr[...] = jnp.zeros_like(ds_scr)

        cos = cos_ref[...]
        sin = sin_ref[...]
        heads = range(RET_HEADS)
        s0 = [st_ref[h] for h in heads]
        ds = [ds_scr[h] for h in heads]
        dsb = [_mx(ds[h]) for h in heads]
        qb, kb, vb, pb, kdb, gn, rstd = _ret_heads_fwd(ret_ref, cos, sin, m_ref, dq_ref, dk_ref, s0)
        dob, dodb = [], []
        for h in heads:
            g = ret_ref[:, 1536 + h * hd:1536 + (h + 1) * hd]
            dyv = dy_ref[:, h * hd:(h + 1) * hd]
            sg = _sigmoid(g)
            d_ref[:, 1536 + h * hd:1536 + (h + 1) * hd] = (
                dyv * gn[h] * (sg * (1.0 + g * (1.0 - sg)))).astype(d_ref.dtype)
            dgn = dyv * (g * sg)
            do = rstd[h] * (dgn - jnp.mean(dgn, axis=-1, keepdims=True)
                            - gn[h] * jnp.mean(dgn * gn[h], axis=-1, keepdims=True))
            dob.append(_mx(do))
            dodb.append(_mx(do * dq_ref[h]))
        dp = [_dot_nt(dob[h], vb[h]) for h in heads]
        dv = [_dot_tn(pb[h], dob[h]) + _dot(kdb[h], dsb[h]) for h in heads]
        dq_cross = [_dot_nt(dodb[h], _mx(s0[h])) for h in heads]
        dk_cross = [_dot_nt(vb[h], dsb[h]) for h in heads]
        ds_new = [_dot_tn(qb[h], dodb[h]) for h in heads]
        dpb = [_mx(dp[h] * m_ref[h]) for h in heads]
        dqr = [_dot(dpb[h], kb[h]) + dq_cross[h] for h in heads]
        dkr = [(_dot_tn(dpb[h], qb[h]) + dk_cross[h] * dk_ref[h]) * RET_KSCALE for h in heads]
        for h in heads:
            d_ref[:, 1024 + h * hd:1024 + (h + 1) * hd] = dv[h].astype(d_ref.dtype)
            d_ref[:, h * hd:(h + 1) * hd] = (dqr[h] * cos + _roll_half(dqr[h] * sin)).astype(d_ref.dtype)
            d_ref[:, 512 + h * hd:512 + (h + 1) * hd] = (
                dkr[h] * cos + _roll_half(dkr[h] * sin)).astype(d_ref.dtype)
            ds_scr[h] = ds[h] * gt_ref[h] + ds_new[h]
        pl.when(i == nt - 1)(finish)

    full3 = lambda a: pl.BlockSpec(a.shape, lambda i: (0, 0, 0))
    rev = lambda i: (nt - 1 - i, 0)
    return pl.pallas_call(
        body, name="retention_bwd", grid=(nt,),
        in_specs=[pl.BlockSpec((t, 2048), rev),
                  pl.BlockSpec((None, RET_HEADS, hd, hd), lambda i: (nt - 1 - i, 0, 0, 0)),
                  pl.BlockSpec((t, 512), rev),
                  pl.BlockSpec((t, LANES), rev),
                  pl.BlockSpec((t, LANES), rev),
                  full3(mask), full3(dq), full3(dk), full3(gt), HBM_SPEC],
        out_specs=[pl.BlockSpec((t, 2048), rev), HBM_SPEC],
        out_shape=[jax.ShapeDtypeStruct((s, 2048), MXU_DTYPE),
                   jax.ShapeDtypeStruct(dwo_parts.shape, dwo_parts.dtype)],
        scratch_shapes=[pltpu.VMEM((RET_HEADS, hd, hd), F32),
                        pltpu.SemaphoreType.DMA((N_PEERS,)), pltpu.SemaphoreType.DMA((N_PEERS,)),
                        pltpu.SemaphoreType.DMA((1,))],
        compiler_params=_params(("arbitrary",)),
    )(ret, states, dy, cos2, sin2, mask, dq, dk, gt, dwo_parts)


SB_BLOCK = 256
SB_CHUNK = 32


SB_SKIP = 104.0
NO_KEYS = -1e30


def _split_dots(xs, u):
    parts = []
    for x in xs:
        hi = lax.bitcast_convert_type(lax.bitcast_convert_type(x, jnp.uint32) & jnp.uint32(0xFFFF0000), F32)
        parts += [_mx(hi), _mx(x - hi)]
    out = _dot(jnp.concatenate(parts, axis=0), u)
    n = xs[0].shape[0]
    return [out[2 * k * n:(2 * k + 1) * n] + out[(2 * k + 1) * n:(2 * k + 2) * n] for k in range(len(xs))]


def _split_dot(x, u):
    return _split_dots([x], u)[0]


def _sb_pair_weights(lb, lk, allowed, u_gt):
    blk = lb.shape[0]
    lk_p, lk_d = lk[:, :blk], lk[:, blk:]
    r_d = _rowsum(lk_d)
    cs_p, cs_d = _split_dots([lk_p, lk_d], u_gt)
    a = jnp.exp(lb + jnp.concatenate([cs_p + r_d, cs_d], axis=1))
    return jnp.where(allowed, a, 0.0), r_d, r_d + _rowsum(lk_p)


def _sb_logits(q, k, causal):
    z = _dot_nt(q, k)
    l1p = jnp.log(1.0 + jnp.exp(-jnp.abs(z)))
    lk = -(jnp.maximum(z, 0.0) + l1p)
    if causal is not None:
        lk = jnp.where(causal, lk, 0.0)
    return jnp.minimum(z, 0.0) - l1p, lk


def _sb_weights(lb, lk, r, u_gt, causal):
    a = jnp.exp(lb + _split_dot(lk, u_gt) + r)
    return a if causal is None else jnp.where(causal, a, 0.0)


def _rowsum(x):
    return jnp.sum(x, axis=1, keepdims=True)


def _sb_pair_tile(i, blk):
    row = lax.broadcasted_iota(jnp.int32, (blk, 2 * blk), 0)
    col = lax.broadcasted_iota(jnp.int32, (blk, 2 * blk), 1)
    first_col = jnp.where(i >= 1, 0, blk)
    allowed = jnp.logical_and(row > col - blk, col >= first_col)
    rows_p = pl.ds(pl.multiple_of(jnp.maximum(i - 1, 0) * blk, blk), blk)
    rows_d = pl.ds(pl.multiple_of(i * blk, blk), blk)
    return allowed, rows_p, rows_d


def _sb_fwd(qkv, sg):
    s = qkv.shape[0]
    blk = min(SB_BLOCK, s)
    nq = s // blk
    hd = SB_HEAD_DIM

    ch = min(SB_CHUNK, blk)

    def body(q_ref, k_ref, v_ref, g_ref, y_ref, o_ref, z_scr, lhs_scr, cs_scr, a_scr):
        i = pl.program_id(1)
        row = lax.broadcasted_iota(jnp.int32, (blk, blk), 0)
        col = lax.broadcasted_iota(jnp.int32, (blk, blk), 1)
        u_gt = (row > col).astype(MXU_DTYPE)
        heads = [slice(hh * hd, (hh + 1) * hd) for hh in range(2)]
        qs = [q_ref[:, ls] for ls in heads]
        _, rows_p, rows_d = _sb_pair_tile(i, blk)
        has_prev = i >= 1
        crow = lax.broadcasted_iota(jnp.int32, (ch, blk), 0)
        ccol = lax.broadcasted_iota(jnp.int32, (ch, blk), 1)

        def diag_width(c):
            return min(blk, -(-((c + 1) * ch) // LANES) * LANES)

        @pl.when(jnp.logical_and(pl.program_id(0) == 0, i == 0))
        def _():
            lhs_scr[...] = jnp.zeros_like(lhs_scr)
            a_scr[...] = jnp.zeros_like(a_scr)

        def logits(hh):
            kc = jnp.concatenate([k_ref[rows_p, heads[hh]], k_ref[rows_d, heads[hh]]], axis=0)
            z_scr[hh] = _dot_nt(qs[hh], kc)

        def keep_parts(hh):
            r_d, r_all = [], []
            for c in range(blk // ch):
                rows = pl.ds(c * ch, ch)
                wd = diag_width(c)
                causal = crow[:, :wd] + c * ch > ccol[:, :wd]
                z = z_scr[hh, rows, 0:blk + wd]
                l1p = jnp.log(1.0 + jnp.exp(-jnp.abs(z)))
                lb = jnp.minimum(z, 0.0) - l1p
                z_scr[hh, rows, 0:blk + wd] = lb
                lk = lb - z
                lk_p = lk[:, :blk]
                lk_d = jnp.where(causal, lk[:, blk:], 0.0)
                lhs_scr[hh, pl.ds(c * ch, ch), :] = _mx(lk_p)
                lhs_scr[hh, pl.ds(blk + c * ch, ch), 0:wd] = _mx(lk_d)
                r_d.append(_rowsum(lk_d))
                r_all.append(r_d[c] + _rowsum(lk_p))
            return r_d, jnp.concatenate(r_all, axis=0)

        def suffix_sums(hh):
            cs_scr[hh] = _dot(lhs_scr[hh], u_gt)

        def weights(hh, r_d):
            for c in range(blk // ch):
                rows = pl.ds(c * ch, ch)
                wd = diag_width(c)
                causal = crow[:, :wd] + c * ch > ccol[:, :wd]
                cs_p = cs_scr[hh, pl.ds(c * ch, ch), :] + jnp.where(has_prev, r_d[c], NO_KEYS)
                cs_d = cs_scr[hh, pl.ds(blk + c * ch, ch), 0:wd]
                lb = z_scr[hh, rows, 0:blk + wd]
                a_p = jnp.exp(lb[:, :blk] + cs_p)
                a_d = jnp.where(causal, jnp.exp(lb[:, blk:] + cs_d), 0.0)
                a_scr[hh, rows, 0:blk + wd] = _mx(jnp.concatenate([a_p, a_d], axis=1))

        def values(hh):
            vc = jnp.concatenate([v_ref[rows_p, heads[hh]], v_ref[rows_d, heads[hh]]], axis=0)
            return _dot(a_scr[hh], vc)

        def block(hh, j, r):
            start = pl.multiple_of(j * blk, blk)
            lb, lk = _sb_logits(qs[hh], k_ref[pl.ds(start, blk), heads[hh]], None)
            a = _sb_weights(lb, lk, r, u_gt, None)
            return _dot(_mx(a), v_ref[pl.ds(start, blk), heads[hh]]), r + _rowsum(lk)

        def more(n, r0, r1):
            return jnp.logical_and(n < i, jnp.max(jnp.maximum(r0, r1)) > -SB_SKIP)

        logits(0)
        logits(1)
        gv = g_ref[...]
        gates = gv * _sigmoid(gv)
        rd0, r0 = keep_parts(0)
        suffix_sums(0)
        rd1, r1 = keep_parts(1)
        suffix_sums(1)
        go = more(jnp.int32(1), r0, r1)
        weights(0, rd0)
        acc0 = values(0)
        weights(1, rd1)
        acc1 = values(1)

        def step(c):
            _, n, acc0, r0, acc1, r1 = c
            pv0, r0 = block(0, i - 1 - n, r0)
            pv1, r1 = block(1, i - 1 - n, r1)
            return more(n + 1, r0, r1), n + 1, acc0 + pv0, r0, acc1 + pv1, r1

        _, _, acc0, _, acc1, _ = lax.while_loop(lambda c: c[0], step, (go, jnp.int32(1), acc0, r0, acc1, r1))
        o = jnp.concatenate([acc0, acc1], axis=1)
        o_ref[...] = o
        y_ref[...] = (o * gates).astype(y_ref.dtype)

    qblk = pl.BlockSpec((blk, LANES), lambda p, i: (i, p))
    return pl.pallas_call(
        body, name="stickbreak_fwd", grid=(SB_HEADS // 2, nq),
        in_specs=[qblk,
                  pl.BlockSpec((s, LANES), lambda p, i: (0, 4 + p)),
                  pl.BlockSpec((s, LANES), lambda p, i: (0, 8 + p)),
                  qblk],
        out_specs=[qblk, qblk],
        out_shape=[jax.ShapeDtypeStruct((s, 512), MXU_DTYPE),
                   jax.ShapeDtypeStruct((s, 512), F32)],
        scratch_shapes=[pltpu.VMEM((2, blk, 2 * blk), F32),
                        pltpu.VMEM((2, 2 * blk, blk), MXU_DTYPE),
                        pltpu.VMEM((2, 2 * blk, blk), F32),
                        pltpu.VMEM((2, blk, 2 * blk), MXU_DTYPE)],
        compiler_params=_params(("arbitrary", "arbitrary")),
    )(qkv, qkv, qkv, sg)


def _sb_bwd(qkv, sg, o, dy):
    s = qkv.shape[0]
    blk = min(SB_BLOCK, s)
    nq = s // blk
    hd = SB_HEAD_DIM
    assert nq <= LANES
    ch = min(SB_CHUNK, blk)

    def body(q_ref, k_ref, v_ref, g_ref, o_ref, dy_ref, dq_ref, dk_ref, dv_ref, dg_ref, dk_scr, dv_scr,
             z_scr, g_scr, lhs_scr, cs_scr, a_scr, dz_scr):
        i = pl.program_id(1)

        @pl.when(i == 0)
        def _():
            dk_scr[...] = jnp.zeros_like(dk_scr)
            dv_scr[...] = jnp.zeros_like(dv_scr)

        row = lax.broadcasted_iota(jnp.int32, (blk, blk), 0)
        col = lax.broadcasted_iota(jnp.int32, (blk, blk), 1)
        lane = lax.broadcasted_iota(jnp.int32, (blk, LANES), 1)
        u_gt = (row > col).astype(MXU_DTYPE)
        u_lt = (row < col).astype(MXU_DTYPE)
        heads = [slice(hh * hd, (hh + 1) * hd) for hh in range(2)]
        qs = [q_ref[:, ls] for ls in heads]
        _, rows_p, rows_d = _sb_pair_tile(i, blk)
        has_prev = i >= 1
        crow = lax.broadcasted_iota(jnp.int32, (ch, blk), 0)
        ccol = lax.broadcasted_iota(jnp.int32, (ch, blk), 1)
        nch = blk // ch
        kcs = [jnp.concatenate([k_ref[rows_p, ls], k_ref[rows_d, ls]], axis=0) for ls in heads]
        dobs = []

        def gate_grads():
            g = g_ref[...]
            dyv = dy_ref[...]
            sgm = _sigmoid(g)
            dg_ref[...] = (dyv * o_ref[...] * (sgm * (1.0 + g * (1.0 - sgm)))).astype(dg_ref.dtype)
            dob = _mx(dyv * (g * sgm))
            dobs.extend(dob[:, ls] for ls in heads)

        def split_rows(hh, c, part, x):
            lhs_scr[hh, pl.ds(part * blk + c * ch, ch), :] = _mx(x)

        def summed_rows(hh, c, part):
            return cs_scr[hh, pl.ds(part * blk + c * ch, ch), :]

        def logits(hh):
            z_scr[hh] = _dot_nt(qs[hh], kcs[hh])

        def weight_grads(hh):
            vc = jnp.concatenate([v_ref[rows_p, heads[hh]], v_ref[rows_d, heads[hh]]], axis=0)
            g_scr[hh] = _dot_nt(dobs[hh], vc)

        def keep_parts(hh):
            r_d, r_all = [], []
            for c in range(nch):
                rows = pl.ds(c * ch, ch)
                z = z_scr[hh, rows, :]
                l1p = jnp.log(1.0 + jnp.exp(-jnp.abs(z)))
                lb = jnp.minimum(z, 0.0) - l1p
                z_scr[hh, rows, :] = lb
                lk = lb - z
                lk_p = lk[:, :blk]
                lk_d = jnp.where(crow + c * ch > ccol, lk[:, blk:], 0.0)
                split_rows(hh, c, 0, lk_p)
                split_rows(hh, c, 1, lk_d)
                r_d.append(_rowsum(lk_d))
                r_all.append(r_d[c] + _rowsum(lk_p))
            return r_d, jnp.concatenate(r_all, axis=0)

        def weights(hh, r_d):
            g_p = []
            for c in range(nch):
                rows = pl.ds(c * ch, ch)
                lb = z_scr[hh, rows, :]
                a_p = jnp.exp(lb[:, :blk] + (summed_rows(hh, c, 0) + jnp.where(has_prev, r_d[c], NO_KEYS)))
                a_d = jnp.where(crow + c * ch > ccol, jnp.exp(lb[:, blk:] + summed_rows(hh, c, 1)), 0.0)
                a = jnp.concatenate([a_p, a_d], axis=1)
                a_scr[hh, rows, :] = _mx(a)
                gm = g_scr[hh, rows, :] * a
                g_scr[hh, rows, :] = gm
                split_rows(hh, c, 0, gm[:, :blk])
                split_rows(hh, c, 1, gm[:, blk:])
                g_p.append(_rowsum(gm[:, :blk]))
            return g_p

        def logit_grads(hh, pg, g_p):
            for c in range(nch):
                rows = pl.ds(c * ch, ch)
                pre = jnp.concatenate([summed_rows(hh, c, 0) + pg[c * ch:(c + 1) * ch],
                                       summed_rows(hh, c, 1) + (pg[c * ch:(c + 1) * ch] + g_p[c])], axis=1)
                gm = g_scr[hh, rows, :]
                dz = gm - (gm + pre) * jnp.exp(z_scr[hh, rows, :])
                dz_p = dz[:, :blk]
                dz_d = jnp.where(crow + c * ch > ccol, dz[:, blk:], 0.0)
                dz_scr[hh, rows, :] = _mx(jnp.concatenate([dz_p, dz_d], axis=1))

        def products(hh, acc):
            ls = heads[hh]
            dk_scr[hh, rows_p, :] += _dot_tn(dz_scr[hh, :, 0:blk], qs[hh])
            dk_scr[hh, rows_d, :] += _dot_tn(dz_scr[hh, :, blk:2 * blk], qs[hh])
            dv_scr[hh, rows_p, :] += _dot_tn(a_scr[hh, :, 0:blk], dobs[hh])
            dv_scr[hh, rows_d, :] += _dot_tn(a_scr[hh, :, blk:2 * blk], dobs[hh])
            dq_ref[:, ls] = ((acc + _dot(dz_scr[hh], kcs[hh])) * SB_SCALE).astype(dq_ref.dtype)

        def suffix_sums(hh):
            cs_scr[hh] = _dot(lhs_scr[hh], u_gt)

        def prefix_sums(hh):
            cs_scr[hh] = _dot(lhs_scr[hh], u_lt)

        def more(n, r0, r1):
            return jnp.logical_and(n < i, jnp.max(jnp.maximum(r0, r1)) > -SB_SKIP)

        gate_grads()
        logits(0)
        weight_grads(0)
        logits(1)
        weight_grads(1)
        rd0, ra0 = keep_parts(0)
        suffix_sums(0)
        rd1, ra1 = keep_parts(1)
        suffix_sums(1)
        go = more(jnp.int32(1), ra0, ra1)
        gp0 = weights(0, rd0)
        prefix_sums(0)
        gp1 = weights(1, rd1)
        prefix_sums(1)

        def scan_block(hh, j, r, rmat):
            start = pl.multiple_of(j * blk, blk)
            _, lk = _sb_logits(qs[hh], k_ref[pl.ds(start, blk), heads[hh]], None)
            return r + _rowsum(lk), jnp.where(lane == j, r, rmat)

        def scan_step(c):
            _, n, r0, rmat0, r1, rmat1 = c
            r0, rmat0 = scan_block(0, i - 1 - n, r0, rmat0)
            r1, rmat1 = scan_block(1, i - 1 - n, r1, rmat1)
            return more(n + 1, r0, r1), n + 1, r0, rmat0, r1, rmat1

        zmat = jnp.zeros((blk, LANES), F32)
        _, n, _, rmat0, _, rmat1 = lax.while_loop(lambda c: c[0], scan_step,
                                                  (go, jnp.int32(1), ra0, zmat, ra1, zmat))
        rmats = (rmat0, rmat1)

        def block(hh, j, pg):
            ls = heads[hh]
            start = pl.multiple_of(j * blk, blk)
            k = k_ref[pl.ds(start, blk), ls]
            lb, lk = _sb_logits(qs[hh], k, None)
            r = _rowsum(jnp.where(lane == j, rmats[hh], 0.0))
            a = _sb_weights(lb, lk, r, u_gt, None)
            gm = _dot_nt(dobs[hh], v_ref[pl.ds(start, blk), ls]) * a
            dzb = _mx(gm - (gm + (pg + _split_dot(gm, u_lt))) * jnp.exp(lb))
            dk_scr[hh, pl.ds(start, blk), :] += _dot_tn(dzb, qs[hh])
            dv_scr[hh, pl.ds(start, blk), :] += _dot_tn(_mx(a), dobs[hh])
            return _dot(dzb, k), pg + _rowsum(gm)

        def step(t, c):
            acc0, pg0, acc1, pg1 = c
            dq0, pg0 = block(0, i - n + t, pg0)
            dq1, pg1 = block(1, i - n + t, pg1)
            return acc0 + dq0, pg0, acc1 + dq1, pg1

        zero = jnp.zeros((blk, 1), F32)
        zacc = jnp.zeros((blk, hd), F32)
        acc0, pg0, acc1, pg1 = lax.fori_loop(0, n - 1, step, (zacc, zero, zacc, zero))
        logit_grads(0, pg0, gp0)
        logit_grads(1, pg1, gp1)
        products(0, acc0)
        products(1, acc1)

        @pl.when(i == nq - 1)
        def _():
            for hh in range(2):
                ls = slice(hh * hd, (hh + 1) * hd)
                dk_ref[:, ls] = dk_scr[hh].astype(dk_ref.dtype)
                dv_ref[:, ls] = dv_scr[hh].astype(dv_ref.dtype)

    qblk = lambda c0: pl.BlockSpec((blk, LANES), lambda p, i: (i, c0 + p))
    full = lambda c0: pl.BlockSpec((s, LANES), lambda p, i: (0, c0 + p))
    half = jax.ShapeDtypeStruct((s, 512), MXU_DTYPE)
    return pl.pallas_call(
        body, name="stickbreak_bwd", grid=(SB_HEADS // 2, nq),
        in_specs=[qblk(0), full(4), full(8), qblk(0), qblk(0), qblk(4)],
        out_specs=[qblk(0), full(0), full(0), qblk(0)],
        out_shape=[half, half, half, half],
        scratch_shapes=[pltpu.VMEM((2, s, hd), F32), pltpu.VMEM((2, s, hd), F32),
                        pltpu.VMEM((2, blk, 2 * blk), F32),
                        pltpu.VMEM((2, blk, 2 * blk), F32),
                        pltpu.VMEM((2, 2 * blk, blk), MXU_DTYPE),
                        pltpu.VMEM((2, 2 * blk, blk), F32),
                        pltpu.VMEM((2, blk, 2 * blk), MXU_DTYPE),
                        pltpu.VMEM((2, blk, 2 * blk), MXU_DTYPE)],
        compiler_params=_params(("arbitrary", "arbitrary")),
    )(qkv, qkv, qkv, sg, o, dy)


def _layer_fwd(layer, x, mod, norm_g, w_in_g, w_out_g, tables, gather=(), loss_head=None):
    shift, scale1p, gate = mod[0:1], 1.0 + mod[1:2], mod[2:3]
    ret, qkv, sg, *gathered = _ln_proj(x, shift, scale1p, norm_g, w_in_g, layer, gather)
    y_r, states = _retention_fwd(ret, tables)
    y_s, o_s = _sb_fwd(qkv, sg)
    x_next = _out_proj(x, gate, y_r, y_s, w_out_g, layer, loss_head)
    saved = (x, shift, scale1p, gate, ret, qkv, sg, y_r, states, y_s, o_s)
    return x_next, saved, gathered


def _layer_bwd(layer, me, dx_out, saved, norm_g, w_in_g, w_out_g, tables, ride_along=None):
    x, shift, scale1p, gate, ret, qkv, sg, y_r, states, y_s, o_s = saved
    dy, dw_out, dgate = _out_proj_bwd(dx_out, gate, y_r, y_s, w_out_g, layer)
    dwo_parts = _mx(dw_out.reshape(N_DEV, D_MODEL // N_DEV, D_MODEL))
    d_ret, r_out = _retention_bwd(ret, states, dy, tables, dwo_parts)
    d_q, d_k, d_v, d_g = _sb_bwd(qkv, sg, o_s, dy)
    dproj = (d_ret, d_q, d_k, d_v, d_g)
    dx, dshift, dscale, dnorm_g, h_t = _in_proj_bwd_x(x, dx_out, dproj, shift, scale1p, norm_g, w_in_g, layer)
    dmod = jnp.concatenate([dshift, dscale, dgate], axis=1)
    gather = ride_along(dmod, dnorm_g) if ride_along else ()
    r_in, *gathered = _in_proj_bwd_w(me, h_t, dproj, gather)
    return dx, r_in, r_out, dmod, dnorm_g, gathered


MESH_IDS = pl.DeviceIdType.MESH
N_PEERS = N_DEV - 1
HBM_SPEC = pl.BlockSpec(memory_space=pl.ANY)


def _my_place():
    return lax.axis_index("x"), lax.axis_index("y"), lax.axis_index("c")


def _linear(px, py, pc):
    return 4 * px + 2 * py + pc


def _all_gather(blocks):
    n_arr = len(blocks)

    def body(*refs):
        start, forward, finish = _gather_plan(refs[:n_arr], refs[n_arr:2 * n_arr], *refs[2 * n_arr:])
        start()
        forward()
        finish()

    return pl.pallas_call(
        body, name="all_gather",
        out_shape=_gathered_shapes(blocks),
        in_specs=[HBM_SPEC] * n_arr, out_specs=[HBM_SPEC] * n_arr,
        scratch_shapes=_gather_sems(n_arr),
    )(*blocks)


def _gathered_shapes(blocks):
    return [jax.ShapeDtypeStruct((N_DEV * b.shape[0], b.shape[1]), b.dtype) for b in blocks]


def _gather_sems(n_arr):
    return [pltpu.SemaphoreType.DMA((n_arr * N_PEERS,)), pltpu.SemaphoreType.DMA((n_arr * N_PEERS,)),
            pltpu.SemaphoreType.DMA((n_arr,))]


def _gather_plan(x_refs, out_refs, send_sems, recv_sems, local_sems):
    n_arr = len(x_refs)
    x, y, c = _my_place()
    me, sibling = (x, y, c), (x, y, 1 - c)
    chips = [(1 - x, y), (x, 1 - y), (1 - x, 1 - y)]

    def rows(a, place):
        m = x_refs[a].shape[0]
        return out_refs[a].at[pl.ds(_linear(*place) * m, m), :]

    def copy(a, k, block, to, src=None):
        return pltpu.make_async_remote_copy(
            src_ref=rows(a, block) if src is None else src, dst_ref=rows(a, block),
            send_sem=send_sems.at[a * N_PEERS + k], recv_sem=recv_sems.at[a * N_PEERS + k],
            device_id=to, device_id_type=MESH_IDS)

    mine = [pltpu.make_async_copy(x_refs[a], rows(a, me), local_sems.at[a]) for a in range(n_arr)]
    first = []
    for a in range(n_arr):
        first.append(copy(a, 0, me, sibling, src=x_refs[a]))
        first += [copy(a, 1 + j, me, (*chip, c), src=x_refs[a]) for j, chip in enumerate(chips)]
    passed = [copy(a, 4 + j, (*chip, c), sibling) for j, chip in enumerate(chips) for a in range(n_arr)]

    def start():
        for cp in mine + first:
            cp.start()

    def forward():
        for j, chip in enumerate(chips):
            for a in range(n_arr):
                copy(a, 1 + j, (*chip, c), me).wait_recv()
                passed[j * n_arr + a].start()

    def finish():
        for a in range(n_arr):
            copy(a, 0, sibling, me).wait_recv()
            for j, chip in enumerate(chips):
                copy(a, 4 + j, (*chip, 1 - c), me).wait_recv()
        for cp in first + passed:
            cp.wait_send()
        for cp in mine:
            cp.wait()

    return start, forward, finish


def _ada_fwd(c_all, w_ada, b_cols):
    cols = w_ada.shape[2]

    def body(c_ref, w_ref, b_ref, ca_ref, mod_ref):
        cv = c_ref[...]
        ca = cv * _sigmoid(cv)
        ca_ref[...] = ca
        cb = _mx(jnp.concatenate([ca, ca], axis=0))
        for l in range(DEPTH):
            mod_ref[l * N_DEV:(l + 1) * N_DEV, :] = _dot(cb, _mx(w_ref[l]))[0:N_DEV] + b_ref[l]

    return pl.pallas_call(
        body, name="ada_fwd",
        out_shape=[jax.ShapeDtypeStruct((N_DEV, D_MODEL), F32),
                   jax.ShapeDtypeStruct((DEPTH * N_DEV, cols), F32)],
        compiler_params=_params(),
    )(c_all, w_ada, b_cols)


def _ada_bwd(c_act_t, dmod_cols):
    cols = dmod_cols.shape[2]

    def body(ca_ref, dm_ref, o_ref):
        ca = _mx(ca_ref[...]).astype(F32)
        for l in range(DEPTH):
            o_ref[l] = jnp.dot(ca, _mx(dm_ref[l]).astype(F32),
                               precision=lax.Precision.HIGHEST, preferred_element_type=F32)

    return pl.pallas_call(
        body, name="ada_bwd",
        out_shape=jax.ShapeDtypeStruct((DEPTH, D_MODEL, cols), F32),
        compiler_params=_params(),
    )(c_act_t, dmod_cols)


def _adamw_store(g, w_ref, m_ref, v_ref, g_ref, d_ref, mo_ref, vo_ref):
    m2 = ADAM_B1 * m_ref[...] + (1.0 - ADAM_B1) * g
    v2 = ADAM_B2 * v_ref[...] + (1.0 - ADAM_B2) * (g * g)
    m_hat = m2 / (1.0 - ADAM_B1 ** ADAM_STEP)
    v_hat = v2 / (1.0 - ADAM_B2 ** ADAM_STEP)
    g_ref[...] = g
    d_ref[...] = -ADAM_LR * (m_hat / (jnp.sqrt(v_hat) + ADAM_EPS) + ADAM_WD * w_ref[...])
    mo_ref[...] = m2
    vo_ref[...] = v2


def _slab_sum(p_ref):
    g = p_ref[0].astype(F32)
    for sl in range(1, p_ref.shape[0]):
        g = g + p_ref[sl].astype(F32)
    return g


def _sum_adamw_layers(parts, w, m, v):
    n_slab, rows, cols = parts[0].shape
    tr = min(256, rows)
    nt = rows // tr

    def body(p0_ref, p1_ref, w_ref, m_ref, v_ref, g_ref, d_ref, mo_ref, vo_ref):
        for l, p_ref in enumerate((p0_ref, p1_ref)):
            @pl.when(pl.program_id(0) == l)
            def _():
                _adamw_store(_slab_sum(p_ref), w_ref, m_ref, v_ref, g_ref, d_ref, mo_ref, vo_ref)

    p_specs = [pl.BlockSpec((n_slab, tr, cols), lambda l, i: (0, i * (1 - l) + (nt - 1) * l, 0)),
               pl.BlockSpec((n_slab, tr, cols), lambda l, i: (0, i * l, 0))]
    blk = pl.BlockSpec((None, tr, cols), lambda l, i: (l, i, 0))
    shp = jax.ShapeDtypeStruct((DEPTH, rows, cols), F32)
    return pl.pallas_call(
        body, name="sum_adamw_layers", grid=(DEPTH, nt),
        in_specs=p_specs + [blk, blk, blk],
        out_specs=[blk, blk, blk, blk],
        out_shape=[shp, shp, shp, shp],
        compiler_params=_params(("arbitrary", "arbitrary")),
    )(parts[0], parts[1], w, m, v)


def _sum_adamw(parts, w, m, v):
    n_slab, rows, cols = parts.shape
    tr = min(256, rows)

    def body(p_ref, w_ref, m_ref, v_ref, g_ref, d_ref, mo_ref, vo_ref):
        _adamw_store(_slab_sum(p_ref), w_ref, m_ref, v_ref, g_ref, d_ref, mo_ref, vo_ref)

    blk = pl.BlockSpec((tr, cols), lambda i: (i, 0))
    shp = jax.ShapeDtypeStruct((rows, cols), F32)
    return pl.pallas_call(
        body, name="sum_adamw", grid=(rows // tr,),
        in_specs=[pl.BlockSpec((n_slab, tr, cols), lambda i: (0, i, 0)), blk, blk, blk],
        out_specs=[blk, blk, blk, blk],
        out_shape=[shp, shp, shp, shp],
        compiler_params=_params(("arbitrary",)),
    )(parts, w, m, v)


SMALL_ROWS = 16


def kernel(x, c, norm_g, w_ada, b_ada, w_in, w_out, final_g, loss_target, m_norm_g, m_w_ada, m_b_ada, m_w_in, m_w_out, m_final_g, v_norm_g, v_w_ada, v_b_ada, v_w_in, v_w_out, v_final_g):
    me = _linear(*_my_place())
    in_cols = w_in.shape[2]
    out_rows = w_out.shape[1]
    ada_cols = w_ada.shape[2]

    w_in_m, w_out_m = _mx(w_in), _mx(w_out)
    g_in, g_out, g_c = _all_gather([w_in_m[0], w_out_m[0], jnp.broadcast_to(c, (8, D_MODEL))])
    w_in_g = [g_in.reshape(N_DEV, 1, D_MODEL, in_cols), None]
    w_out_g = [g_out.reshape(N_DEV, 1, out_rows, D_MODEL), None]
    c_all = g_c.reshape(N_DEV, 8, D_MODEL)[:, 0]

    b_cols = lax.dynamic_slice_in_dim(b_ada, me * ada_cols, ada_cols, axis=1)[:, None, :]
    c_act, mod_cols = _ada_fwd(c_all, w_ada, b_cols)
    (g_mod,) = _all_gather([mod_cols])
    g_mod = g_mod.reshape(N_DEV, DEPTH, N_DEV, ada_cols)
    mod = lax.dynamic_index_in_dim(g_mod, me, axis=2, keepdims=False)
    mod = mod.transpose(1, 0, 2).reshape(DEPTH, 3, D_MODEL)

    tables = _ret_tables(x.shape[1])
    h = x[0]
    saved = []
    for l in range(DEPTH):
        nxt = (w_in_m[l + 1], w_out_m[l + 1]) if l + 1 < DEPTH else ()
        head = None if nxt else (final_g[None], loss_target[0])
        h, sv, gathered = _layer_fwd(0, h, mod[l], norm_g[l:l + 1], w_in_g[l], w_out_g[l], tables, nxt, head)
        if nxt:
            w_in_g[l + 1] = gathered[0].reshape(N_DEV, 1, D_MODEL, in_cols)
            w_out_g[l + 1] = gathered[1].reshape(N_DEV, 1, out_rows, D_MODEL)
        saved.append(sv)
    dx, loss_part, dfg = h
    r_in, r_out, small = [None] * DEPTH, [None] * DEPTH, [None] * DEPTH

    def small_block(dmod0, dng0):
        pad = jnp.zeros((SMALL_ROWS - 10, D_MODEL), F32)
        return (jnp.concatenate([dmod0.reshape(3, D_MODEL), small[1][0], dng0, small[1][1], dfg,
                                 jnp.broadcast_to(loss_part, (1, D_MODEL)), pad], axis=0),)

    for l in reversed(range(DEPTH)):
        dx, r_in[l], r_out[l], dmod, dng, gathered = _layer_bwd(
            0, me, dx, saved[l], norm_g[l:l + 1], w_in_g[l], w_out_g[l], tables, small_block if l == 0 else None)
        small[l] = (dmod.reshape(3, D_MODEL), dng)
    g_small = gathered[0].reshape(N_DEV, SMALL_ROWS, D_MODEL)

    def small_pack(b, n, f, fill):
        return jnp.concatenate([b.reshape(6, D_MODEL), n, f[None],
                                jnp.full((SMALL_ROWS - 9, D_MODEL), fill, F32)], axis=0)

    s_g, s_d, s_m, s_v = _sum_adamw(g_small, small_pack(b_ada, norm_g, final_g, 0.0),
                                    small_pack(m_b_ada, m_norm_g, m_final_g, 0.0),
                                    small_pack(v_b_ada, v_norm_g, v_final_g, 1.0))
    loss = s_g[9, 0]

    def small_unpack(a):
        return a[0:6].reshape(DEPTH, 3 * D_MODEL), a[6:8], a[8]

    dmod_all = g_small[:, 0:6].reshape(N_DEV, DEPTH, 3 * D_MODEL).transpose(1, 0, 2)
    dmod_cols = lax.dynamic_slice_in_dim(dmod_all, me * ada_cols, ada_cols, axis=2)
    g_ada = _ada_bwd(c_act.T, dmod_cols).reshape(1, DEPTH * D_MODEL, ada_cols)
    ada = _sum_adamw(g_ada, *[a.reshape(DEPTH * D_MODEL, ada_cols) for a in (w_ada, m_w_ada, v_w_ada)])
    ada = [a.reshape(DEPTH, D_MODEL, ada_cols) for a in ada]

    win = _sum_adamw_layers(r_in, w_in, m_w_in, v_w_in)
    wout = _sum_adamw_layers(r_out, w_out, m_w_out, v_w_out)

    outs = [loss, dx[None]]
    for k in range(4):
        b, n, f = small_unpack((s_g, s_d, s_m, s_v)[k])
        outs += [n, ada[k], b, win[k], wout[k], f]
    return tuple(outs)
```

```python
import functools

import jax
import jax.numpy as jnp
from jax import lax
from jax.experimental import pallas as pl
from jax.experimental.pallas import tpu as pltpu

F32 = jnp.float32
MXU_DTYPE = jnp.bfloat16

D_MODEL = 1024
DEPTH = 2
N_DEV = 8
CHUNK = 64
D_RET = 512
D_SB = 512
RET_HEADS = 4
RET_HEAD_DIM = 128
SB_HEADS = 8
SB_HEAD_DIM = 64
D_IN = 4096
ROPE_BASE = 10000.0
EPS = 1e-6
SB_SCALE = SB_HEAD_DIM ** -0.5
RET_KSCALE = RET_HEAD_DIM ** -0.5

ADAM_LR = 0.001
ADAM_B1 = 0.9
ADAM_B2 = 0.999
ADAM_EPS = 1e-08
ADAM_WD = 0.01
ADAM_STEP = 10

V7X_VMEM_BYTES = 64 * 2 ** 20
VMEM_LIMIT = V7X_VMEM_BYTES - 8 * 2 ** 20
LANES = 128

_NT = (((1,), (1,)), ((), ()))
_TN = (((0,), (0,)), ((), ()))


def _dot(a, b):
    return jnp.dot(a, b, preferred_element_type=F32)


def _dot_nt(a, b):
    return lax.dot_general(a, b, _NT, preferred_element_type=F32)


def _dot_tn(a, b):
    return lax.dot_general(a, b, _TN, preferred_element_type=F32)


def _mx(x):
    return x.astype(MXU_DTYPE)


def _sigmoid(x):
    return 1.0 / (1.0 + jnp.exp(-x))


def _params(sem=None):
    return pltpu.CompilerParams(dimension_semantics=sem, vmem_limit_bytes=VMEM_LIMIT)


def _row_tile(s):
    return min(512, s)


def _w_in_spec(w_in_g, layer):
    return pl.BlockSpec((N_DEV, None) + w_in_g.shape[2:], lambda i: (0, layer, 0, 0))


def _w_out_spec(w_out_g, layer):
    return pl.BlockSpec((N_DEV, None) + w_out_g.shape[2:], lambda i: (0, layer, 0, 0))


def _ln_proj(x, shift, scale1p, g, w_in_g, layer, gather=()):
    s = x.shape[0]
    ts = _row_tile(s)
    ns = s // ts
    n_g = len(gather)

    def body(x_ref, sh_ref, sc_ref, g_ref, w_ref, *rest):
        ret_ref, qkv_ref, sg_ref = rest[n_g:n_g + 3]
        if n_g:
            start, forward, finish = _gather_plan(rest[:n_g], rest[n_g + 3:2 * n_g + 3], *rest[2 * n_g + 3:])
            i = pl.program_id(0)
            pl.when(i == 0)(start)
            pl.when(i == max(ns - 4, 0))(forward)
        xv = x_ref[...]
        rstd = lax.rsqrt(jnp.mean(xv * xv, axis=-1, keepdims=True) + EPS)
        h = (xv * rstd * g_ref[...]) * sc_ref[...] + sh_ref[...]
        hb = _mx(h)
        for n in range(4):
            ret_ref[:, n * 512:(n + 1) * 512] = _dot(hb, w_ref[n])
        qkv_ref[:, 0:512] = _mx(_dot(hb, w_ref[4]) * SB_SCALE)
        qkv_ref[:, 512:1024] = _mx(_dot(hb, w_ref[5]))
        qkv_ref[:, 1024:1536] = _mx(_dot(hb, w_ref[6]))
        sg_ref[...] = _dot(hb, w_ref[7])
        if n_g:
            pl.when(i == ns - 1)(finish)

    vec = pl.BlockSpec((1, D_MODEL), lambda i: (0, 0))
    return pl.pallas_call(
        body, name="ln_proj_gather" if n_g else "ln_proj", grid=(ns,),
        in_specs=[pl.BlockSpec((ts, D_MODEL), lambda i: (i, 0)), vec, vec, vec,
                  _w_in_spec(w_in_g, layer)] + [HBM_SPEC] * n_g,
        out_specs=[pl.BlockSpec((ts, 2048), lambda i: (i, 0)),
                   pl.BlockSpec((ts, 1536), lambda i: (i, 0)),
                   pl.BlockSpec((ts, 512), lambda i: (i, 0))] + [HBM_SPEC] * n_g,
        out_shape=[jax.ShapeDtypeStruct((s, 2048), F32),
                   jax.ShapeDtypeStruct((s, 1536), MXU_DTYPE),
                   jax.ShapeDtypeStruct((s, 512), F32)] + _gathered_shapes(gather),
        scratch_shapes=_gather_sems(n_g) if n_g else (),
        compiler_params=_params(("arbitrary",)),
    )(x, shift, scale1p, g, w_in_g, *gather)


def _w_out_halves(w_ref):
    half = N_DEV // 2
    return (w_ref[0:half].reshape(D_RET, D_MODEL), w_ref[half:N_DEV].reshape(D_SB, D_MODEL))


def _out_proj(x, gate, y_r, y_s, w_out_g, layer, loss_head=None):
    s = x.shape[0]
    ts = min(2 * _row_tile(s), s)

    def layer_out(x_ref, gate_ref, yr_ref, ys_ref, w_ref):
        w_r, w_s = _w_out_halves(w_ref)
        return x_ref[...] + gate_ref[...] * (_dot(yr_ref[...], w_r) + _dot(ys_ref[...], w_s))

    def body(x_ref, gate_ref, yr_ref, ys_ref, w_ref, o_ref):
        o_ref[...] = layer_out(x_ref, gate_ref, yr_ref, ys_ref, w_ref)

    def body_loss(x_ref, gate_ref, yr_ref, ys_ref, w_ref, fg_ref, t_ref, dx_ref, loss_ref, dfg_ref):
        @pl.when(pl.program_id(0) == 0)
        def _():
            loss_ref[...] = jnp.zeros_like(loss_ref)
            dfg_ref[...] = jnp.zeros_like(dfg_ref)

        xv = layer_out(x_ref, gate_ref, yr_ref, ys_ref, w_ref)
        fgv = fg_ref[...]
        rstd = lax.rsqrt(jnp.mean(xv * xv, axis=-1, keepdims=True) + EPS)
        xn = xv * rstd
        err = xn * fgv - t_ref[...]
        tok = jnp.mean(err * err, axis=-1, keepdims=True)
        loss_ref[...] += 0.5 * jnp.sum(tok, axis=0, keepdims=True)
        dy = err * (1.0 / D_MODEL)
        dfg_ref[...] += jnp.sum(dy * xn, axis=0, keepdims=True)
        dxn = dy * fgv
        dx_ref[...] = rstd * (dxn - xn * jnp.mean(dxn * xn, axis=-1, keepdims=True))

    rows = pl.BlockSpec((ts, D_MODEL), lambda i: (i, 0))
    vec = pl.BlockSpec((1, D_MODEL), lambda i: (0, 0))
    in_specs = [rows, vec, pl.BlockSpec((ts, 512), lambda i: (i, 0)), pl.BlockSpec((ts, 512), lambda i: (i, 0)),
                _w_out_spec(w_out_g, layer)]
    if loss_head is None:
        return pl.pallas_call(
            body, name="out_proj", grid=(s // ts,), in_specs=in_specs, out_specs=rows,
            out_shape=jax.ShapeDtypeStruct((s, D_MODEL), F32),
            compiler_params=_params(("arbitrary",)),
        )(x, gate, y_r, y_s, w_out_g)
    return pl.pallas_call(
        body_loss, name="out_proj_loss", grid=(s // ts,), in_specs=in_specs + [vec, rows],
        out_specs=[rows, pl.BlockSpec((1, 1), lambda i: (0, 0)), vec],
        out_shape=[jax.ShapeDtypeStruct((s, D_MODEL), F32),
                   jax.ShapeDtypeStruct((1, 1), F32),
                   jax.ShapeDtypeStruct((1, D_MODEL), F32)],
        compiler_params=_params(("arbitrary",)),
    )(x, gate, y_r, y_s, w_out_g, *loss_head)


def _out_proj_bwd(dx_out, gate, y_r, y_s, w_out_g, layer):
    s = dx_out.shape[0]
    ts = min(2 * _row_tile(s), s)
    ns = s // ts

    def body(dx_ref, gate_ref, yr_ref, ys_ref, w_ref, dy_ref, dw_ref, dgate_ref):
        i = pl.program_id(0)

        @pl.when(i == 0)
        def _():
            dw_ref[...] = jnp.zeros_like(dw_ref)

        dxv = dx_ref[...]
        dt = _mx(dxv * gate_ref[...])
        dxb = _mx(dxv)
        w_r, w_s = _w_out_halves(w_ref)
        dy_ref[:, 0:512] = _dot_nt(dt, w_r)
        dy_ref[:, 512:1024] = _dot_nt(dt, w_s)
        dw_ref[0:512, :] += _dot_tn(yr_ref[...], dxb)
        dw_ref[512:1024, :] += _dot_tn(ys_ref[...], dxb)

        @pl.when(i == ns - 1)
        def _():
            m_r, m_s = dw_ref[0:512, :], dw_ref[512:1024, :]
            dgate_ref[...] = (jnp.sum(w_r.astype(F32) * m_r, axis=0, keepdims=True)
                              + jnp.sum(w_s.astype(F32) * m_s, axis=0, keepdims=True))
            dw_ref[...] = dw_ref[...] * gate_ref[...]

    return pl.pallas_call(
        body, name="out_proj_bwd", grid=(ns,),
        in_specs=[pl.BlockSpec((ts, D_MODEL), lambda i: (i, 0)),
                  pl.BlockSpec((1, D_MODEL), lambda i: (0, 0)),
                  pl.BlockSpec((ts, 512), lambda i: (i, 0)),
                  pl.BlockSpec((ts, 512), lambda i: (i, 0)),
                  _w_out_spec(w_out_g, layer)],
        out_specs=[pl.BlockSpec((ts, D_MODEL), lambda i: (i, 0)),
                   pl.BlockSpec((D_MODEL, D_MODEL), lambda i: (0, 0)),
                   pl.BlockSpec((1, D_MODEL), lambda i: (0, 0))],
        out_shape=[jax.ShapeDtypeStruct((s, D_MODEL), F32),
                   jax.ShapeDtypeStruct((D_MODEL, D_MODEL), F32),
                   jax.ShapeDtypeStruct((1, D_MODEL), F32)],
        compiler_params=_params(("arbitrary",)),
    )(dx_out, gate, y_r, y_s, w_out_g)


def _scatter_plan(parts_ref, recv_ref, send_sems, recv_sems, local_sem):
    px, py, pc = _my_place()
    mine = _linear(px, py, pc)

    def copy(r):
        peer = (1 - px if r & 4 else px, 1 - py if r & 2 else py, 1 - pc if r & 1 else pc)
        return pltpu.make_async_remote_copy(
            src_ref=parts_ref.at[_linear(*peer)], dst_ref=recv_ref.at[mine],
            send_sem=send_sems.at[r - 1], recv_sem=recv_sems.at[r - 1],
            device_id=peer, device_id_type=MESH_IDS)

    own = pltpu.make_async_copy(parts_ref.at[mine], recv_ref.at[mine], local_sem.at[0])

    def start():
        own.start()
        for r in range(1, N_DEV):
            copy(r).start()

    def finish():
        for r in range(1, N_DEV):
            copy(r).wait_recv()
            copy(r).wait_send()
        own.wait()

    return start, finish


def _in_proj_bwd_x(x, dx_out, dproj, shift, scale1p, g, w_in_g, layer):
    s = x.shape[0]
    ts = _row_tile(s)
    ns = s // ts
    nb = D_IN // N_DEV

    def body(x_ref, dxo_ref, dr_ref, d4_ref, d5_ref, d6_ref, d7_ref, sh_ref, sc_ref, g_ref, w_ref,
             dx_ref, dsh_ref, dsc_ref, dg_ref, ht_ref):
        i = pl.program_id(0)

        @pl.when(i == 0)
        def _():
            dsh_ref[...] = jnp.zeros_like(dsh_ref)
            dsc_ref[...] = jnp.zeros_like(dsc_ref)
            dg_ref[...] = jnp.zeros_like(dg_ref)

        dh = _dot_nt(dr_ref[:, 0:nb], w_ref[0])
        for n in range(1, 4):
            dh += _dot_nt(dr_ref[:, n * nb:(n + 1) * nb], w_ref[n])
        for n, d_ref in zip(range(4, N_DEV), (d4_ref, d5_ref, d6_ref, d7_ref)):
            dh += _dot_nt(d_ref[...], w_ref[n])
        xv = x_ref[...]
        gv = g_ref[...]
        scv = sc_ref[...]
        rstd = lax.rsqrt(jnp.mean(xv * xv, axis=-1, keepdims=True) + EPS)
        xn = xv * rstd
        xg = xn * gv
        ht_ref[...] = _mx((xg * scv + sh_ref[...]).T)
        dsh_ref[...] += jnp.sum(dh, axis=0, keepdims=True)
        dsc_ref[...] += jnp.sum(dh * xg, axis=0, keepdims=True)
        dhs = dh * scv
        dg_ref[...] += jnp.sum(dhs * xn, axis=0, keepdims=True)
        dxn = dhs * gv
        dx_ref[...] = rstd * (dxn - xn * jnp.mean(dxn * xn, axis=-1, keepdims=True)) + dxo_ref[...]

    vec = pl.BlockSpec((1, D_MODEL), lambda i: (0, 0))
    return pl.pallas_call(
        body, name="in_proj_bwd_x", grid=(ns,),
        in_specs=[pl.BlockSpec((ts, D_MODEL), lambda i: (i, 0)),
                  pl.BlockSpec((ts, D_MODEL), lambda i: (i, 0)),
                  pl.BlockSpec((ts, 4 * nb), lambda i: (i, 0))]
                 + [pl.BlockSpec((ts, nb), lambda i: (i, 0))] * 4
                 + [vec, vec, vec, _w_in_spec(w_in_g, layer)],
        out_specs=[pl.BlockSpec((ts, D_MODEL), lambda i: (i, 0)), vec, vec, vec,
                   pl.BlockSpec((D_MODEL, ts), lambda i: (0, i))],
        out_shape=[jax.ShapeDtypeStruct((s, D_MODEL), F32),
                   jax.ShapeDtypeStruct((1, D_MODEL), F32),
                   jax.ShapeDtypeStruct((1, D_MODEL), F32),
                   jax.ShapeDtypeStruct((1, D_MODEL), F32),
                   jax.ShapeDtypeStruct((D_MODEL, s), MXU_DTYPE)],
        compiler_params=_params(("arbitrary",)),
    )(x, dx_out, *dproj, shift, scale1p, g, w_in_g)


def _in_proj_bwd_w(me, h_t, dproj, gather=()):
    s = h_t.shape[1]
    ts = min(4 * _row_tile(s), s)
    ns = s // ts
    nb = D_IN // N_DEV
    n_chip = N_DEV // 2
    n_g = len(gather)

    def flip_bits(j):
        return jnp.where(j == 0, 4, jnp.where(j == 1, 2, jnp.where(j == 2, 6, 0)))

    def slab_of(t, me_ref):
        return jnp.bitwise_xor(me_ref[0], flip_bits(t // 2) + 1 - t % 2)

    def body(me_ref, ht_ref, dr_ref, d4_ref, d5_ref, d6_ref, d7_ref, *rest):
        rin_ref = rest[n_g]
        (acc, stage, pre_buf, pre_send, pre_recv, sum_send, sum_recv, local_sem) = rest[2 * n_g + 1:2 * n_g + 9]
        t = pl.program_id(0)
        i = pl.program_id(1)
        if n_g:
            g_start, g_forward, g_finish = _gather_plan(rest[:n_g], rest[n_g + 1:2 * n_g + 1], *rest[2 * n_g + 9:])
            pl.when(jnp.logical_and(t == 0, i == 0))(g_start)
            pl.when(jnp.logical_and(t == N_DEV // 2, i == 0))(g_forward)
        j = t // 2
        summing = t % 2 == 1
        slab = slab_of(t, me_ref)
        px, py, pc = _my_place()

        def pre_copy(jj):
            return pltpu.make_async_remote_copy(
                src_ref=stage.at[jj % 2], dst_ref=pre_buf.at[jj],
                send_sem=pre_send.at[jj], recv_sem=pre_recv.at[jj],
                device_id=(px, py, 1 - pc), device_id_type=MESH_IDS)

        def sum_copy(jj):
            fx = jnp.logical_or(jj == 0, jj == 2)
            fy = jnp.logical_or(jj == 1, jj == 2)
            return pltpu.make_async_remote_copy(
                src_ref=stage.at[2 + jj % 2], dst_ref=rin_ref.at[jj],
                send_sem=sum_send.at[jj], recv_sem=sum_recv.at[jj],
                device_id=(jnp.where(fx, 1 - px, px), jnp.where(fy, 1 - py, py), pc), device_id_type=MESH_IDS)

        own = pltpu.make_async_copy(stage.at[3], rin_ref.at[n_chip - 1], local_sem.at[0])

        @pl.when(i == 0)
        def _():
            acc[...] = jnp.zeros_like(acc)

        @pl.when(slab < 4)
        def _():
            acc[...] += _dot(ht_ref[...], dr_ref[...])

        for n, d_ref in zip(range(4, N_DEV), (d4_ref, d5_ref, d6_ref, d7_ref)):
            @pl.when(slab == n)
            def _():
                acc[...] += _dot(ht_ref[...], d_ref[...])

        @pl.when(jnp.logical_and(i == ns - 1, jnp.logical_not(summing)))
        def _():
            @pl.when(j >= 2)
            def _():
                pre_copy(j - 2).wait_send()

            stage[j % 2] = acc[...].astype(stage.dtype)
            pre_copy(j).start()

        @pl.when(jnp.logical_and(i == ns - 1, summing))
        def _():
            pre_copy(j).wait_recv()

            @pl.when(j >= 2)
            def _():
                sum_copy(j - 2).wait_send()

            stage[2 + j % 2] = (acc[...] + pre_buf[j].astype(F32)).astype(stage.dtype)

            @pl.when(j < n_chip - 1)
            def _():
                sum_copy(j).start()

            @pl.when(j == n_chip - 1)
            def _():
                own.start()
                pre_copy(n_chip - 2).wait_send()
                pre_copy(n_chip - 1).wait_send()
                sum_copy(n_chip - 2).wait_send()
                for jj in range(n_chip - 1):
                    sum_copy(jj).wait_recv()
                own.wait()
                if n_g:
                    g_finish()

    def part_rows(n, t, i, me_ref):
        return jnp.where(slab_of(t, me_ref) == n, i, ns - 1), 0

    return pl.pallas_call(
        body, name="in_proj_bwd_w",
        grid_spec=pltpu.PrefetchScalarGridSpec(
            num_scalar_prefetch=1, grid=(N_DEV, ns),
            in_specs=[pl.BlockSpec((D_MODEL, ts), lambda t, i, me_ref: (0, i)),
                      pl.BlockSpec((ts, nb), lambda t, i, me_ref: (
                          jnp.where(slab_of(t, me_ref) < 4, i, ns - 1), jnp.minimum(slab_of(t, me_ref), 3)))]
                     + [pl.BlockSpec((ts, nb), functools.partial(part_rows, n)) for n in range(4, N_DEV)]
                     + [HBM_SPEC] * n_g,
            out_specs=[HBM_SPEC] * (1 + n_g),
            scratch_shapes=[pltpu.VMEM((D_MODEL, nb), F32),
                            pltpu.VMEM((4, D_MODEL, nb), MXU_DTYPE),
                            pltpu.VMEM((n_chip, D_MODEL, nb), MXU_DTYPE),
                            pltpu.SemaphoreType.DMA((n_chip,)), pltpu.SemaphoreType.DMA((n_chip,)),
                            pltpu.SemaphoreType.DMA((n_chip - 1,)), pltpu.SemaphoreType.DMA((n_chip - 1,)),
                            pltpu.SemaphoreType.DMA((1,))] + (_gather_sems(n_g) if n_g else [])),
        out_shape=[jax.ShapeDtypeStruct((n_chip, D_MODEL, nb), MXU_DTYPE)] + _gathered_shapes(gather),
        compiler_params=_params(("arbitrary", "arbitrary")),
    )(jnp.reshape(me, (1,)).astype(jnp.int32), h_t, *dproj, *gather)


RET_TILE = 256


def _ret_tables(s):
    t = min(RET_TILE, s)
    half = RET_HEAD_DIM // 2
    pos = jnp.arange(s, dtype=F32)
    inv = ROPE_BASE ** (-jnp.arange(half, dtype=F32) / half)
    ang = pos[:, None] * jnp.concatenate([inv, inv])[None, :]
    cos2 = jnp.cos(ang)
    sin2 = jnp.sin(ang) * jnp.concatenate([-jnp.ones((half,), F32), jnp.ones((half,), F32)])[None, :]
    lg = jnp.log1p(-(2.0 ** (-5.0 - jnp.arange(RET_HEADS, dtype=F32))))[:, None, None]
    n = jnp.arange(t)
    dist = (n[:, None] - n[None, :]).astype(F32)[None]
    cn = (n // CHUNK)[:, None]
    cm = (n // CHUNK)[None, :]
    mask = jnp.where((cn == cm)[None], jnp.exp(jnp.abs(dist) * lg),
                     jnp.where((cm < cn)[None], jnp.exp(dist * lg), 0.0))
    nf = n.astype(F32)[None, :, None]
    dq = jnp.broadcast_to(jnp.exp((nf + 1.0) * lg), (RET_HEADS, t, LANES))
    dk = jnp.broadcast_to(jnp.exp((t - 1.0 - nf) * lg), (RET_HEADS, t, LANES))
    gt = jnp.broadcast_to(jnp.exp(float(t) * lg), (RET_HEADS, 1, LANES))
    return cos2, sin2, mask, dq, dk, gt


def _roll_half(x):
    return pltpu.roll(x, RET_HEAD_DIM // 2, 1)


def _ret_heads_fwd(ret_ref, cos, sin, m_ref, dq_ref, dk_ref, s0):
    hd = RET_HEAD_DIM
    heads = range(RET_HEADS)
    qb, kb, vb, kdb = [], [], [], []
    for h in heads:
        q = ret_ref[:, h * hd:(h + 1) * hd]
        k = ret_ref[:, 512 + h * hd:512 + (h + 1) * hd]
        kr = (k * cos + _roll_half(k) * sin) * RET_KSCALE
        qb.append(_mx(q * cos + _roll_half(q) * sin))
        kb.append(_mx(kr))
        kdb.append(_mx(kr * dk_ref[h]))
        vb.append(_mx(ret_ref[:, 1024 + h * hd:1024 + (h + 1) * hd]))
    p = [_dot_nt(qb[h], kb[h]) for h in heads]
    cross = [_dot(qb[h], _mx(s0[h])) for h in heads]
    pb = [_mx(p[h] * m_ref[h]) for h in heads]
    o = [_dot(pb[h], vb[h]) + cross[h] * dq_ref[h] for h in heads]
    gn, rstd = [], []
    for h in heads:
        oc = o[h] - jnp.mean(o[h], axis=-1, keepdims=True)
        rstd.append(lax.rsqrt(jnp.mean(oc * oc, axis=-1, keepdims=True) + EPS))
        gn.append(oc * rstd[h])
    return qb, kb, vb, pb, kdb, gn, rstd


def _retention_fwd(ret, tables):
    cos2, sin2, mask, dq, dk, gt = tables
    s = ret.shape[0]
    t = mask.shape[1]
    nt = s // t
    hd = RET_HEAD_DIM

    def body(ret_ref, cos_ref, sin_ref, m_ref, dq_ref, dk_ref, gt_ref, y_ref, st_ref, s_scr):
        i = pl.program_id(0)

        @pl.when(i == 0)
        def _():
            s_scr[...] = jnp.zeros_like(s_scr)

        s0 = [s_scr[h] for h in range(RET_HEADS)]
        _, _, vb, _, kdb, gn, _ = _ret_heads_fwd(ret_ref, cos_ref[...], sin_ref[...], m_ref, dq_ref, dk_ref, s0)
        kv = [_dot_tn(kdb[h], vb[h]) for h in range(RET_HEADS)]
        for h in range(RET_HEADS):
            g = ret_ref[:, 1536 + h * hd:1536 + (h + 1) * hd]
            st_ref[h] = s0[h]
            y_ref[:, h * hd:(h + 1) * hd] = (gn[h] * (g * _sigmoid(g))).astype(y_ref.dtype)
            s_scr[h] = s0[h] * gt_ref[h] + kv[h]

    full3 = lambda a: pl.BlockSpec(a.shape, lambda i: (0, 0, 0))
    return pl.pallas_call(
        body, name="retention_fwd", grid=(nt,),
        in_specs=[pl.BlockSpec((t, 2048), lambda i: (i, 0)),
                  pl.BlockSpec((t, LANES), lambda i: (i, 0)),
                  pl.BlockSpec((t, LANES), lambda i: (i, 0)),
                  full3(mask), full3(dq), full3(dk), full3(gt)],
        out_specs=[pl.BlockSpec((t, 512), lambda i: (i, 0)),
                   pl.BlockSpec((None, RET_HEADS, hd, hd), lambda i: (i, 0, 0, 0))],
        out_shape=[jax.ShapeDtypeStruct((s, 512), MXU_DTYPE),
                   jax.ShapeDtypeStruct((nt, RET_HEADS, hd, hd), F32)],
        scratch_shapes=[pltpu.VMEM((RET_HEADS, hd, hd), F32)],
        compiler_params=_params(("arbitrary",)),
    )(ret, cos2, sin2, mask, dq, dk, gt)


def _retention_bwd(ret, states, dy, tables, dwo_parts):
    cos2, sin2, mask, dq, dk, gt = tables
    s = ret.shape[0]
    t = mask.shape[1]
    nt = s // t
    hd = RET_HEAD_DIM

    def body(ret_ref, st_ref, dy_ref, cos_ref, sin_ref, m_ref, dq_ref, dk_ref, gt_ref, dwo_ref,
             d_ref, rout_ref, ds_scr, send_sems, recv_sems, local_sem):
        i = pl.program_id(0)
        start, finish = _scatter_plan(dwo_ref, rout_ref, send_sems, recv_sems, local_sem)

        @pl.when(i == 0)
        def _():
            start()
            ds_scr[...] = jnp.zeros_like(ds_scr)

        cos = cos_ref[...]
        sin = sin_ref[...]
        heads = range(RET_HEADS)
        s0 = [st_ref[h] for h in heads]
        ds = [ds_scr[h] for h in heads]
        dsb = [_mx(ds[h]) for h in heads]
        qb, kb, vb, pb, kdb, gn, rstd = _ret_heads_fwd(ret_ref, cos, sin, m_ref, dq_ref, dk_ref, s0)
        dob, dodb = [], []
        for h in heads:
            g = ret_ref[:, 1536 + h * hd:1536 + (h + 1) * hd]
            dyv = dy_ref[:, h * hd:(h + 1) * hd]
            sg = _sigmoid(g)
            d_ref[:, 1536 + h * hd:1536 + (h + 1) * hd] = (
                dyv * gn[h] * (sg * (1.0 + g * (1.0 - sg)))).astype(d_ref.dtype)
            dgn = dyv * (g * sg)
            do = rstd[h] * (dgn - jnp.mean(dgn, axis=-1, keepdims=True)
                            - gn[h] * jnp.mean(dgn * gn[h], axis=-1, keepdims=True))
            dob.append(_mx(do))
            dodb.append(_mx(do * dq_ref[h]))
        dp = [_dot_nt(dob[h], vb[h]) for h in heads]
        dv = [_dot_tn(pb[h], dob[h]) + _dot(kdb[h], dsb[h]) for h in heads]
        dq_cross = [_dot_nt(dodb[h], _mx(s0[h])) for h in heads]
        dk_cross = [_dot_nt(vb[h], dsb[h]) for h in heads]
        ds_new = [_dot_tn(qb[h], dodb[h]) for h in heads]
        dpb = [_mx(dp[h] * m_ref[h]) for h in heads]
        dqr = [_dot(dpb[h], kb[h]) + dq_cross[h] for h in heads]
        dkr = [(_dot_tn(dpb[h], qb[h]) + dk_cross[h] * dk_ref[h]) * RET_KSCALE for h in heads]
        for h in heads:
            d_ref[:, 1024 + h * hd:1024 + (h + 1) * hd] = dv[h].astype(d_ref.dtype)
            d_ref[:, h * hd:(h + 1) * hd] = (dqr[h] * cos + _roll_half(dqr[h] * sin)).astype(d_ref.dtype)
            d_ref[:, 512 + h * hd:512 + (h + 1) * hd] = (
                dkr[h] * cos + _roll_half(dkr[h] * sin)).astype(d_ref.dtype)
            ds_scr[h] = ds[h] * gt_ref[h] + ds_new[h]
        pl.when(i == nt - 1)(finish)

    full3 = lambda a: pl.BlockSpec(a.shape, lambda i: (0, 0, 0))
    rev = lambda i: (nt - 1 - i, 0)
    return pl.pallas_call(
        body, name="retention_bwd", grid=(nt,),
        in_specs=[pl.BlockSpec((t, 2048), rev),
                  pl.BlockSpec((None, RET_HEADS, hd, hd), lambda i: (nt - 1 - i, 0, 0, 0)),
                  pl.BlockSpec((t, 512), rev),
                  pl.BlockSpec((t, LANES), rev),
                  pl.BlockSpec((t, LANES), rev),
                  full3(mask), full3(dq), full3(dk), full3(gt), HBM_SPEC],
        out_specs=[pl.BlockSpec((t, 2048), rev), HBM_SPEC],
        out_shape=[jax.ShapeDtypeStruct((s, 2048), MXU_DTYPE),
                   jax.ShapeDtypeStruct(dwo_parts.shape, dwo_parts.dtype)],
        scratch_shapes=[pltpu.VMEM((RET_HEADS, hd, hd), F32),
                        pltpu.SemaphoreType.DMA((N_PEERS,)), pltpu.SemaphoreType.DMA((N_PEERS,)),
                        pltpu.SemaphoreType.DMA((1,))],
        compiler_params=_params(("arbitrary",)),
    )(ret, states, dy, cos2, sin2, mask, dq, dk, gt, dwo_parts)


SB_BLOCK = 256
SB_CHUNK = 32


SB_SKIP = 104.0
NO_KEYS = -1e30


def _split_dots(xs, u):
    parts = []
    for x in xs:
        hi = lax.bitcast_convert_type(lax.bitcast_convert_type(x, jnp.uint32) & jnp.uint32(0xFFFF0000), F32)
        parts += [_mx(hi), _mx(x - hi)]
    out = _dot(jnp.concatenate(parts, axis=0), u)
    n = xs[0].shape[0]
    return [out[2 * k * n:(2 * k + 1) * n] + out[(2 * k + 1) * n:(2 * k + 2) * n] for k in range(len(xs))]


def _split_dot(x, u):
    return _split_dots([x], u)[0]


def _sb_pair_weights(lb, lk, allowed, u_gt):
    blk = lb.shape[0]
    lk_p, lk_d = lk[:, :blk], lk[:, blk:]
    r_d = _rowsum(lk_d)
    cs_p, cs_d = _split_dots([lk_p, lk_d], u_gt)
    a = jnp.exp(lb + jnp.concatenate([cs_p + r_d, cs_d], axis=1))
    return jnp.where(allowed, a, 0.0), r_d, r_d + _rowsum(lk_p)


def _sb_logits(q, k, causal):
    z = _dot_nt(q, k)
    l1p = jnp.log(1.0 + jnp.exp(-jnp.abs(z)))
    lk = -(jnp.maximum(z, 0.0) + l1p)
    if causal is not None:
        lk = jnp.where(causal, lk, 0.0)
    return jnp.minimum(z, 0.0) - l1p, lk


def _sb_weights(lb, lk, r, u_gt, causal):
    a = jnp.exp(lb + _split_dot(lk, u_gt) + r)
    return a if causal is None else jnp.where(causal, a, 0.0)


def _rowsum(x):
    return jnp.sum(x, axis=1, keepdims=True)


def _sb_pair_tile(i, blk):
    row = lax.broadcasted_iota(jnp.int32, (blk, 2 * blk), 0)
    col = lax.broadcasted_iota(jnp.int32, (blk, 2 * blk), 1)
    first_col = jnp.where(i >= 1, 0, blk)
    allowed = jnp.logical_and(row > col - blk, col >= first_col)
    rows_p = pl.ds(pl.multiple_of(jnp.maximum(i - 1, 0) * blk, blk), blk)
    rows_d = pl.ds(pl.multiple_of(i * blk, blk), blk)
    return allowed, rows_p, rows_d


def _sb_fwd(qkv, sg):
    s = qkv.shape[0]
    blk = min(SB_BLOCK, s)
    nq = s // blk
    hd = SB_HEAD_DIM

    ch = min(SB_CHUNK, blk)

    def body(q_ref, k_ref, v_ref, g_ref, y_ref, o_ref, z_scr, lhs_scr, cs_scr, a_scr):
        i = pl.program_id(1)
        row = lax.broadcasted_iota(jnp.int32, (blk, blk), 0)
        col = lax.broadcasted_iota(jnp.int32, (blk, blk), 1)
        u_gt = (row > col).astype(MXU_DTYPE)
        heads = [slice(hh * hd, (hh + 1) * hd) for hh in range(2)]
        qs = [q_ref[:, ls] for ls in heads]
        _, rows_p, rows_d = _sb_pair_tile(i, blk)
        has_prev = i >= 1
        crow = lax.broadcasted_iota(jnp.int32, (ch, blk), 0)
        ccol = lax.broadcasted_iota(jnp.int32, (ch, blk), 1)

        def logits(hh):
            kc = jnp.concatenate([k_ref[rows_p, heads[hh]], k_ref[rows_d, heads[hh]]], axis=0)
            z_scr[hh] = _dot_nt(qs[hh], kc)

        def keep_parts(hh):
            r_d, r_all = [], []
            for c in range(blk // ch):
                rows = pl.ds(c * ch, ch)
                causal = crow + c * ch > ccol
                z = z_scr[hh, rows, :]
                l1p = jnp.log(1.0 + jnp.exp(-jnp.abs(z)))
                lb = jnp.minimum(z, 0.0) - l1p
                z_scr[hh, rows, :] = lb
                lk = lb - z
                lk_p = lk[:, :blk]
                lk_d = jnp.where(causal, lk[:, blk:], 0.0)
                lhs_scr[hh, pl.ds(c * ch, ch), :] = _mx(lk_p)
                lhs_scr[hh, pl.ds(blk + c * ch, ch), :] = _mx(lk_d)
                r_d.append(_rowsum(lk_d))
                r_all.append(r_d[c] + _rowsum(lk_p))
            return r_d, jnp.concatenate(r_all, axis=0)

        def suffix_sums(hh):
            cs_scr[hh] = _dot(lhs_scr[hh], u_gt)

        def weights(hh, r_d):
            for c in range(blk // ch):
                rows = pl.ds(c * ch, ch)
                causal = crow + c * ch > ccol
                cs_p = cs_scr[hh, pl.ds(c * ch, ch), :] + jnp.where(has_prev, r_d[c], NO_KEYS)
                cs_d = cs_scr[hh, pl.ds(blk + c * ch, ch), :]
                lb = z_scr[hh, rows, :]
                a_p = jnp.exp(lb[:, :blk] + cs_p)
                a_d = jnp.where(causal, jnp.exp(lb[:, blk:] + cs_d), 0.0)
                a_scr[hh, rows, :] = _mx(jnp.concatenate([a_p, a_d], axis=1))

        def values(hh):
            vc = jnp.concatenate([v_ref[rows_p, heads[hh]], v_ref[rows_d, heads[hh]]], axis=0)
            return _dot(a_scr[hh], vc)

        def block(hh, j, r):
            start = pl.multiple_of(j * blk, blk)
            lb, lk = _sb_logits(qs[hh], k_ref[pl.ds(start, blk), heads[hh]], None)
            a = _sb_weights(lb, lk, r, u_gt, None)
            return _dot(_mx(a), v_ref[pl.ds(start, blk), heads[hh]]), r + _rowsum(lk)

        def more(n, r0, r1):
            return jnp.logical_and(n < i, jnp.max(jnp.maximum(r0, r1)) > -SB_SKIP)

        logits(0)
        logits(1)
        gv = g_ref[...]
        gates = gv * _sigmoid(gv)
        rd0, r0 = keep_parts(0)
        suffix_sums(0)
        rd1, r1 = keep_parts(1)
        suffix_sums(1)
        go = more(jnp.int32(1), r0, r1)
        weights(0, rd0)
        acc0 = values(0)
        weights(1, rd1)
        acc1 = values(1)

        def step(c):
            _, n, acc0, r0, acc1, r1 = c
            pv0, r0 = block(0, i - 1 - n, r0)
            pv1, r1 = block(1, i - 1 - n, r1)
            return more(n + 1, r0, r1), n + 1, acc0 + pv0, r0, acc1 + pv1, r1

        _, _, acc0, _, acc1, _ = lax.while_loop(lambda c: c[0], step, (go, jnp.int32(1), acc0, r0, acc1, r1))
        o = jnp.concatenate([acc0, acc1], axis=1)
        o_ref[...] = o
        y_ref[...] = (o * gates).astype(y_ref.dtype)

    qblk = pl.BlockSpec((blk, LANES), lambda p, i: (i, p))
    return pl.pallas_call(
        body, name="stickbreak_fwd", grid=(SB_HEADS // 2, nq),
        in_specs=[qblk,
                  pl.BlockSpec((s, LANES), lambda p, i: (0, 4 + p)),
                  pl.BlockSpec((s, LANES), lambda p, i: (0, 8 + p)),
                  qblk],
        out_specs=[qblk, qblk],
        out_shape=[jax.ShapeDtypeStruct((s, 512), MXU_DTYPE),
                   jax.ShapeDtypeStruct((s, 512), F32)],
        scratch_shapes=[pltpu.VMEM((2, blk, 2 * blk), F32),
                        pltpu.VMEM((2, 2 * blk, blk), MXU_DTYPE),
                        pltpu.VMEM((2, 2 * blk, blk), F32),
                        pltpu.VMEM((2, blk, 2 * blk), MXU_DTYPE)],
        compiler_params=_params(("arbitrary", "arbitrary")),
    )(qkv, qkv, qkv, sg)


def _sb_bwd(qkv, sg, o, dy):
    s = qkv.shape[0]
    blk = min(SB_BLOCK, s)
    nq = s // blk
    hd = SB_HEAD_DIM
    assert nq <= LANES
    ch = min(SB_CHUNK, blk)

    def body(q_ref, k_ref, v_ref, g_ref, o_ref, dy_ref, dq_ref, dk_ref, dv_ref, dg_ref, dk_scr, dv_scr,
             z_scr, g_scr, lhs_scr, cs_scr, a_scr, dz_scr):
        i = pl.program_id(1)

        @pl.when(i == 0)
        def _():
            dk_scr[...] = jnp.zeros_like(dk_scr)
            dv_scr[...] = jnp.zeros_like(dv_scr)

        row = lax.broadcasted_iota(jnp.int32, (blk, blk), 0)
        col = lax.broadcasted_iota(jnp.int32, (blk, blk), 1)
        lane = lax.broadcasted_iota(jnp.int32, (blk, LANES), 1)
        u_gt = (row > col).astype(MXU_DTYPE)
        u_lt = (row < col).astype(MXU_DTYPE)
        heads = [slice(hh * hd, (hh + 1) * hd) for hh in range(2)]
        qs = [q_ref[:, ls] for ls in heads]
        _, rows_p, rows_d = _sb_pair_tile(i, blk)
        has_prev = i >= 1
        crow = lax.broadcasted_iota(jnp.int32, (ch, blk), 0)
        ccol = lax.broadcasted_iota(jnp.int32, (ch, blk), 1)
        nch = blk // ch
        kcs = [jnp.concatenate([k_ref[rows_p, ls], k_ref[rows_d, ls]], axis=0) for ls in heads]
        dobs = []

        def gate_grads():
            g = g_ref[...]
            dyv = dy_ref[...]
            sgm = _sigmoid(g)
            dg_ref[...] = (dyv * o_ref[...] * (sgm * (1.0 + g * (1.0 - sgm)))).astype(dg_ref.dtype)
            dob = _mx(dyv * (g * sgm))
            dobs.extend(dob[:, ls] for ls in heads)

        def split_rows(hh, c, part, x):
            lhs_scr[hh, pl.ds(part * blk + c * ch, ch), :] = _mx(x)

        def summed_rows(hh, c, part):
            return cs_scr[hh, pl.ds(part * blk + c * ch, ch), :]

        def logits(hh):
            z_scr[hh] = _dot_nt(qs[hh], kcs[hh])

        def weight_grads(hh):
            vc = jnp.concatenate([v_ref[rows_p, heads[hh]], v_ref[rows_d, heads[hh]]], axis=0)
            g_scr[hh] = _dot_nt(dobs[hh], vc)

        def keep_parts(hh):
            r_d, r_all = [], []
            for c in range(nch):
                rows = pl.ds(c * ch, ch)
                z = z_scr[hh, rows, :]
                l1p = jnp.log(1.0 + jnp.exp(-jnp.abs(z)))
                lb = jnp.minimum(z, 0.0) - l1p
                z_scr[hh, rows, :] = lb
                lk = lb - z
                lk_p = lk[:, :blk]
                lk_d = jnp.where(crow + c * ch > ccol, lk[:, blk:], 0.0)
                split_rows(hh, c, 0, lk_p)
                split_rows(hh, c, 1, lk_d)
                r_d.append(_rowsum(lk_d))
                r_all.append(r_d[c] + _rowsum(lk_p))
            return r_d, jnp.concatenate(r_all, axis=0)

        def weights(hh, r_d):
            g_p = []
            for c in range(nch):
                rows = pl.ds(c * ch, ch)
                lb = z_scr[hh, rows, :]
                a_p = jnp.exp(lb[:, :blk] + (summed_rows(hh, c, 0) + jnp.where(has_prev, r_d[c], NO_KEYS)))
                a_d = jnp.where(crow + c * ch > ccol, jnp.exp(lb[:, blk:] + summed_rows(hh, c, 1)), 0.0)
                a = jnp.concatenate([a_p, a_d], axis=1)
                a_scr[hh, rows, :] = _mx(a)
                gm = g_scr[hh, rows, :] * a
                g_scr[hh, rows, :] = gm
                split_rows(hh, c, 0, gm[:, :blk])
                split_rows(hh, c, 1, gm[:, blk:])
                g_p.append(_rowsum(gm[:, :blk]))
            return g_p

        def logit_grads(hh, pg, g_p):
            for c in range(nch):
                rows = pl.ds(c * ch, ch)
                pre = jnp.concatenate([summed_rows(hh, c, 0) + pg[c * ch:(c + 1) * ch],
                                       summed_rows(hh, c, 1) + (pg[c * ch:(c + 1) * ch] + g_p[c])], axis=1)
                gm = g_scr[hh, rows, :]
                dz = gm - (gm + pre) * jnp.exp(z_scr[hh, rows, :])
                dz_p = dz[:, :blk]
                dz_d = jnp.where(crow + c * ch > ccol, dz[:, blk:], 0.0)
                dz_scr[hh, rows, :] = _mx(jnp.concatenate([dz_p, dz_d], axis=1))

        def products(hh, acc):
            ls = heads[hh]
            dk_scr[hh, rows_p, :] += _dot_tn(dz_scr[hh, :, 0:blk], qs[hh])
            dk_scr[hh, rows_d, :] += _dot_tn(dz_scr[hh, :, blk:2 * blk], qs[hh])
            dv_scr[hh, rows_p, :] += _dot_tn(a_scr[hh, :, 0:blk], dobs[hh])
            dv_scr[hh, rows_d, :] += _dot_tn(a_scr[hh, :, blk:2 * blk], dobs[hh])
            dq_ref[:, ls] = ((acc + _dot(dz_scr[hh], kcs[hh])) * SB_SCALE).astype(dq_ref.dtype)

        def suffix_sums(hh):
            cs_scr[hh] = _dot(lhs_scr[hh], u_gt)

        def prefix_sums(hh):
            cs_scr[hh] = _dot(lhs_scr[hh], u_lt)

        def more(n, r0, r1):
            return jnp.logical_and(n < i, jnp.max(jnp.maximum(r0, r1)) > -SB_SKIP)

        gate_grads()
        logits(0)
        weight_grads(0)
        logits(1)
        weight_grads(1)
        rd0, ra0 = keep_parts(0)
        suffix_sums(0)
        rd1, ra1 = keep_parts(1)
        suffix_sums(1)
        go = more(jnp.int32(1), ra0, ra1)
        gp0 = weights(0, rd0)
        prefix_sums(0)
        gp1 = weights(1, rd1)
        prefix_sums(1)

        def scan_block(hh, j, r, rmat):
            start = pl.multiple_of(j * blk, blk)
            _, lk = _sb_logits(qs[hh], k_ref[pl.ds(start, blk), heads[hh]], None)
            return r + _rowsum(lk), jnp.where(lane == j, r, rmat)

        def scan_step(c):
            _, n, r0, rmat0, r1, rmat1 = c
            r0, rmat0 = scan_block(0, i - 1 - n, r0, rmat0)
            r1, rmat1 = scan_block(1, i - 1 - n, r1, rmat1)
            return more(n + 1, r0, r1), n + 1, r0, rmat0, r1, rmat1

        zmat = jnp.zeros((blk, LANES), F32)
        _, n, _, rmat0, _, rmat1 = lax.while_loop(lambda c: c[0], scan_step,
                                                  (go, jnp.int32(1), ra0, zmat, ra1, zmat))
        rmats = (rmat0, rmat1)

        def block(hh, j, pg):
            ls = heads[hh]
            start = pl.multiple_of(j * blk, blk)
            k = k_ref[pl.ds(start, blk), ls]
            lb, lk = _sb_logits(qs[hh], k, None)
            r = _rowsum(jnp.where(lane == j, rmats[hh], 0.0))
            a = _sb_weights(lb, lk, r, u_gt, None)
            gm = _dot_nt(dobs[hh], v_ref[pl.ds(start, blk), ls]) * a
            dzb = _mx(gm - (gm + (pg + _split_dot(gm, u_lt))) * jnp.exp(lb))
            dk_scr[hh, pl.ds(start, blk), :] += _dot_tn(dzb, qs[hh])
            dv_scr[hh, pl.ds(start, blk), :] += _dot_tn(_mx(a), dobs[hh])
            return _dot(dzb, k), pg + _rowsum(gm)

        def step(t, c):
            acc0, pg0, acc1, pg1 = c
            dq0, pg0 = block(0, i - n + t, pg0)
            dq1, pg1 = block(1, i - n + t, pg1)
            return acc0 + dq0, pg0, acc1 + dq1, pg1

        zero = jnp.zeros((blk, 1), F32)
        zacc = jnp.zeros((blk, hd), F32)
        acc0, pg0, acc1, pg1 = lax.fori_loop(0, n - 1, step, (zacc, zero, zacc, zero))
        logit_grads(0, pg0, gp0)
        logit_grads(1, pg1, gp1)
        products(0, acc0)
        products(1, acc1)

        @pl.when(i == nq - 1)
        def _():
            for hh in range(2):
                ls = slice(hh * hd, (hh + 1) * hd)
                dk_ref[:, ls] = dk_scr[hh].astype(dk_ref.dtype)
                dv_ref[:, ls] = dv_scr[hh].astype(dv_ref.dtype)

    qblk = lambda c0: pl.BlockSpec((blk, LANES), lambda p, i: (i, c0 + p))
    full = lambda c0: pl.BlockSpec((s, LANES), lambda p, i: (0, c0 + p))
    half = jax.ShapeDtypeStruct((s, 512), MXU_DTYPE)
    return pl.pallas_call(
        body, name="stickbreak_bwd", grid=(SB_HEADS // 2, nq),
        in_specs=[qblk(0), full(4), full(8), qblk(0), qblk(0), qblk(4)],
        out_specs=[qblk(0), full(0), full(0), qblk(0)],
        out_shape=[half, half, half, half],
        scratch_shapes=[pltpu.VMEM((2, s, hd), F32), pltpu.VMEM((2, s, hd), F32),
                        pltpu.VMEM((2, blk, 2 * blk), F32),
                        pltpu.VMEM((2, blk, 2 * blk), F32),
                        pltpu.VMEM((2, 2 * blk, blk), MXU_DTYPE),
                        pltpu.VMEM((2, 2 * blk, blk), F32),
                        pltpu.VMEM((2, blk, 2 * blk), MXU_DTYPE),
                        pltpu.VMEM((2, blk, 2 * blk), MXU_DTYPE)],
        compiler_params=_params(("arbitrary", "arbitrary")),
    )(qkv, qkv, qkv, sg, o, dy)


def _layer_fwd(layer, x, mod, norm_g, w_in_g, w_out_g, tables, gather=(), loss_head=None):
    shift, scale1p, gate = mod[0:1], 1.0 + mod[1:2], mod[2:3]
    ret, qkv, sg, *gathered = _ln_proj(x, shift, scale1p, norm_g, w_in_g, layer, gather)
    y_r, states = _retention_fwd(ret, tables)
    y_s, o_s = _sb_fwd(qkv, sg)
    x_next = _out_proj(x, gate, y_r, y_s, w_out_g, layer, loss_head)
    saved = (x, shift, scale1p, gate, ret, qkv, sg, y_r, states, y_s, o_s)
    return x_next, saved, gathered


def _layer_bwd(layer, me, dx_out, saved, norm_g, w_in_g, w_out_g, tables, ride_along=None):
    x, shift, scale1p, gate, ret, qkv, sg, y_r, states, y_s, o_s = saved
    dy, dw_out, dgate = _out_proj_bwd(dx_out, gate, y_r, y_s, w_out_g, layer)
    dwo_parts = _mx(dw_out.reshape(N_DEV, D_MODEL // N_DEV, D_MODEL))
    d_ret, r_out = _retention_bwd(ret, states, dy, tables, dwo_parts)
    d_q, d_k, d_v, d_g = _sb_bwd(qkv, sg, o_s, dy)
    dproj = (d_ret, d_q, d_k, d_v, d_g)
    dx, dshift, dscale, dnorm_g, h_t = _in_proj_bwd_x(x, dx_out, dproj, shift, scale1p, norm_g, w_in_g, layer)
    dmod = jnp.concatenate([dshift, dscale, dgate], axis=1)
    gather = ride_along(dmod, dnorm_g) if ride_along else ()
    r_in, *gathered = _in_proj_bwd_w(me, h_t, dproj, gather)
    return dx, r_in, r_out, dmod, dnorm_g, gathered


MESH_IDS = pl.DeviceIdType.MESH
N_PEERS = N_DEV - 1
HBM_SPEC = pl.BlockSpec(memory_space=pl.ANY)


def _my_place():
    return lax.axis_index("x"), lax.axis_index("y"), lax.axis_index("c")


def _linear(px, py, pc):
    return 4 * px + 2 * py + pc


def _all_gather(blocks):
    n_arr = len(blocks)

    def body(*refs):
        start, forward, finish = _gather_plan(refs[:n_arr], refs[n_arr:2 * n_arr], *refs[2 * n_arr:])
        start()
        forward()
        finish()

    return pl.pallas_call(
        body, name="all_gather",
        out_shape=_gathered_shapes(blocks),
        in_specs=[HBM_SPEC] * n_arr, out_specs=[HBM_SPEC] * n_arr,
        scratch_shapes=_gather_sems(n_arr),
    )(*blocks)


def _gathered_shapes(blocks):
    return [jax.ShapeDtypeStruct((N_DEV * b.shape[0], b.shape[1]), b.dtype) for b in blocks]


def _gather_sems(n_arr):
    return [pltpu.SemaphoreType.DMA((n_arr * N_PEERS,)), pltpu.SemaphoreType.DMA((n_arr * N_PEERS,)),
            pltpu.SemaphoreType.DMA((n_arr,))]


def _gather_plan(x_refs, out_refs, send_sems, recv_sems, local_sems):
    n_arr = len(x_refs)
    x, y, c = _my_place()
    me, sibling = (x, y, c), (x, y, 1 - c)
    chips = [(1 - x, y), (x, 1 - y), (1 - x, 1 - y)]

    def rows(a, place):
        m = x_refs[a].shape[0]
        return out_refs[a].at[pl.ds(_linear(*place) * m, m), :]

    def copy(a, k, block, to, src=None):
        return pltpu.make_async_remote_copy(
            src_ref=rows(a, block) if src is None else src, dst_ref=rows(a, block),
            send_sem=send_sems.at[a * N_PEERS + k], recv_sem=recv_sems.at[a * N_PEERS + k],
            device_id=to, device_id_type=MESH_IDS)

    mine = [pltpu.make_async_copy(x_refs[a], rows(a, me), local_sems.at[a]) for a in range(n_arr)]
    first = []
    for a in range(n_arr):
        first.append(copy(a, 0, me, sibling, src=x_refs[a]))
        first += [copy(a, 1 + j, me, (*chip, c), src=x_refs[a]) for j, chip in enumerate(chips)]
    passed = [copy(a, 4 + j, (*chip, c), sibling) for j, chip in enumerate(chips) for a in range(n_arr)]

    def start():
        for cp in mine + first:
            cp.start()

    def forward():
        for j, chip in enumerate(chips):
            for a in range(n_arr):
                copy(a, 1 + j, (*chip, c), me).wait_recv()
                passed[j * n_arr + a].start()

    def finish():
        for a in range(n_arr):
            copy(a, 0, sibling, me).wait_recv()
            for j, chip in enumerate(chips):
                copy(a, 4 + j, (*chip, 1 - c), me).wait_recv()
        for cp in first + passed:
            cp.wait_send()
        for cp in mine:
            cp.wait()

    return start, forward, finish


def _ada_fwd(c_all, w_ada, b_cols):
    cols = w_ada.shape[2]

    def body(c_ref, w_ref, b_ref, ca_ref, mod_ref):
        cv = c_ref[...]
        ca = cv * _sigmoid(cv)
        ca_ref[...] = ca
        cb = _mx(jnp.concatenate([ca, ca], axis=0))
        for l in range(DEPTH):
            mod_ref[l * N_DEV:(l + 1) * N_DEV, :] = _dot(cb, _mx(w_ref[l]))[0:N_DEV] + b_ref[l]

    return pl.pallas_call(
        body, name="ada_fwd",
        out_shape=[jax.ShapeDtypeStruct((N_DEV, D_MODEL), F32),
                   jax.ShapeDtypeStruct((DEPTH * N_DEV, cols), F32)],
        compiler_params=_params(),
    )(c_all, w_ada, b_cols)


def _ada_bwd(c_act_t, dmod_cols):
    cols = dmod_cols.shape[2]

    def body(ca_ref, dm_ref, o_ref):
        ca = _mx(ca_ref[...]).astype(F32)
        for l in range(DEPTH):
            o_ref[l] = jnp.dot(ca, _mx(dm_ref[l]).astype(F32),
                               precision=lax.Precision.HIGHEST, preferred_element_type=F32)

    return pl.pallas_call(
        body, name="ada_bwd",
        out_shape=jax.ShapeDtypeStruct((DEPTH, D_MODEL, cols), F32),
        compiler_params=_params(),
    )(c_act_t, dmod_cols)


def _adamw_store(g, w_ref, m_ref, v_ref, g_ref, d_ref, mo_ref, vo_ref):
    m2 = ADAM_B1 * m_ref[...] + (1.0 - ADAM_B1) * g
    v2 = ADAM_B2 * v_ref[...] + (1.0 - ADAM_B2) * (g * g)
    m_hat = m2 / (1.0 - ADAM_B1 ** ADAM_STEP)
    v_hat = v2 / (1.0 - ADAM_B2 ** ADAM_STEP)
    g_ref[...] = g
    d_ref[...] = -ADAM_LR * (m_hat / (jnp.sqrt(v_hat) + ADAM_EPS) + ADAM_WD * w_ref[...])
    mo_ref[...] = m2
    vo_ref[...] = v2


def _slab_sum(p_ref):
    g = p_ref[0].astype(F32)
    for sl in range(1, p_ref.shape[0]):
        g = g + p_ref[sl].astype(F32)
    return g


def _sum_adamw_layers(parts, w, m, v):
    n_slab, rows, cols = parts[0].shape
    tr = min(512, rows)
    nt = rows // tr

    def body(p0_ref, p1_ref, w_ref, m_ref, v_ref, g_ref, d_ref, mo_ref, vo_ref):
        for l, p_ref in enumerate((p0_ref, p1_ref)):
            @pl.when(pl.program_id(0) == l)
            def _():
                _adamw_store(_slab_sum(p_ref), w_ref, m_ref, v_ref, g_ref, d_ref, mo_ref, vo_ref)

    p_specs = [pl.BlockSpec((n_slab, tr, cols), lambda l, i: (0, i * (1 - l) + (nt - 1) * l, 0)),
               pl.BlockSpec((n_slab, tr, cols), lambda l, i: (0, i * l, 0))]
    blk = pl.BlockSpec((None, tr, cols), lambda l, i: (l, i, 0))
    shp = jax.ShapeDtypeStruct((DEPTH, rows, cols), F32)
    return pl.pallas_call(
        body, name="sum_adamw_layers", grid=(DEPTH, nt),
        in_specs=p_specs + [blk, blk, blk],
        out_specs=[blk, blk, blk, blk],
        out_shape=[shp, shp, shp, shp],
        compiler_params=_params(("arbitrary", "arbitrary")),
    )(parts[0], parts[1], w, m, v)


def _sum_adamw(parts, w, m, v):
    n_slab, rows, cols = parts.shape
    tr = min(512, rows)

    def body(p_ref, w_ref, m_ref, v_ref, g_ref, d_ref, mo_ref, vo_ref):
        _adamw_store(_slab_sum(p_ref), w_ref, m_ref, v_ref, g_ref, d_ref, mo_ref, vo_ref)

    blk = pl.BlockSpec((tr, cols), lambda i: (i, 0))
    shp = jax.ShapeDtypeStruct((rows, cols), F32)
    return pl.pallas_call(
        body, name="sum_adamw", grid=(rows // tr,),
        in_specs=[pl.BlockSpec((n_slab, tr, cols), lambda i: (0, i, 0)), blk, blk, blk],
        out_specs=[blk, blk, blk, blk],
        out_shape=[shp, shp, shp, shp],
        compiler_params=_params(("arbitrary",)),
    )(parts, w, m, v)


SMALL_ROWS = 16


def kernel(x, c, norm_g, w_ada, b_ada, w_in, w_out, final_g, loss_target, m_norm_g, m_w_ada, m_b_ada, m_w_in, m_w_out, m_final_g, v_norm_g, v_w_ada, v_b_ada, v_w_in, v_w_out, v_final_g):
    me = _linear(*_my_place())
    in_cols = w_in.shape[2]
    out_rows = w_out.shape[1]
    ada_cols = w_ada.shape[2]

    w_in_m, w_out_m = _mx(w_in), _mx(w_out)
    g_in, g_out, g_c = _all_gather([w_in_m[0], w_out_m[0], jnp.broadcast_to(c, (8, D_MODEL))])
    w_in_g = [g_in.reshape(N_DEV, 1, D_MODEL, in_cols), None]
    w_out_g = [g_out.reshape(N_DEV, 1, out_rows, D_MODEL), None]
    c_all = g_c.reshape(N_DEV, 8, D_MODEL)[:, 0]

    b_cols = lax.dynamic_slice_in_dim(b_ada, me * ada_cols, ada_cols, axis=1)[:, None, :]
    c_act, mod_cols = _ada_fwd(c_all, w_ada, b_cols)
    (g_mod,) = _all_gather([mod_cols])
    g_mod = g_mod.reshape(N_DEV, DEPTH, N_DEV, ada_cols)
    mod = lax.dynamic_index_in_dim(g_mod, me, axis=2, keepdims=False)
    mod = mod.transpose(1, 0, 2).reshape(DEPTH, 3, D_MODEL)

    tables = _ret_tables(x.shape[1])
    h = x[0]
    saved = []
    for l in range(DEPTH):
        nxt = (w_in_m[l + 1], w_out_m[l + 1]) if l + 1 < DEPTH else ()
        head = None if nxt else (final_g[None], loss_target[0])
        h, sv, gathered = _layer_fwd(0, h, mod[l], norm_g[l:l + 1], w_in_g[l], w_out_g[l], tables, nxt, head)
        if nxt:
            w_in_g[l + 1] = gathered[0].reshape(N_DEV, 1, D_MODEL, in_cols)
            w_out_g[l + 1] = gathered[1].reshape(N_DEV, 1, out_rows, D_MODEL)
        saved.append(sv)
    dx, loss_part, dfg = h
    r_in, r_out, small = [None] * DEPTH, [None] * DEPTH, [None] * DEPTH

    def small_block(dmod0, dng0):
        pad = jnp.zeros((SMALL_ROWS - 10, D_MODEL), F32)
        return (jnp.concatenate([dmod0.reshape(3, D_MODEL), small[1][0], dng0, small[1][1], dfg,
                                 jnp.broadcast_to(loss_part, (1, D_MODEL)), pad], axis=0),)

    for l in reversed(range(DEPTH)):
        dx, r_in[l], r_out[l], dmod, dng, gathered = _layer_bwd(
            0, me, dx, saved[l], norm_g[l:l + 1], w_in_g[l], w_out_g[l], tables, small_block if l == 0 else None)
        small[l] = (dmod.reshape(3, D_MODEL), dng)
    g_small = gathered[0].reshape(N_DEV, SMALL_ROWS, D_MODEL)

    def small_pack(b, n, f, fill):
        return jnp.concatenate([b.reshape(6, D_MODEL), n, f[None],
                                jnp.full((SMALL_ROWS - 9, D_MODEL), fill, F32)], axis=0)

    s_g, s_d, s_m, s_v = _sum_adamw(g_small, small_pack(b_ada, norm_g, final_g, 0.0),
                                    small_pack(m_b_ada, m_norm_g, m_final_g, 0.0),
                                    small_pack(v_b_ada, v_norm_g, v_final_g, 1.0))
    loss = s_g[9, 0]

    def small_unpack(a):
        return a[0:6].reshape(DEPTH, 3 * D_MODEL), a[6:8], a[8]

    dmod_all = g_small[:, 0:6].reshape(N_DEV, DEPTH, 3 * D_MODEL).transpose(1, 0, 2)
    dmod_cols = lax.dynamic_slice_in_dim(dmod_all, me * ada_cols, ada_cols, axis=2)
    g_ada = _ada_bwd(c_act.T, dmod_cols).reshape(1, DEPTH * D_MODEL, ada_cols)
    ada = _sum_adamw(g_ada, *[a.reshape(DEPTH * D_MODEL, ada_cols) for a in (w_ada, m_w_ada, v_w_ada)])
    ada = [a.reshape(DEPTH, D_MODEL, ada_cols) for a in ada]

    win = _sum_adamw_layers(r_in, w_in, m_w_in, v_w_in)
    wout = _sum_adamw_layers(r_out, w_out, m_w_out, v_w_out)

    outs = [loss, dx[None]]
    for k in range(4):
        b, n, f = small_unpack((s_g, s_d, s_m, s_v)[k])
        outs += [n, ada[k], b, win[k], wout[k], f]
    return tuple(outs)
```
